```python
import jax, jax.numpy as jnp
from jax import lax
import numpy as np

D_MODEL = 1024
BATCH = 16
SEQ = 256
DEPTH = 4
DEC_BATCH = 2
DEC_SEQ = 2048
PAST_LEN = 512

GRID_W = 64
N_MIXERS = 3
N_POOL_LAYERS = (DEPTH + 2) // 3
N_MLSTM_LAYERS = (DEPTH + 1) // 3
N_MLA_LAYERS = DEPTH // 3
NORM_EPS = 1e-6

POOL_WINDOWS = (2, 4, 8, 16)
POOL_GROUPS = 4
POOL_GW = D_MODEL // POOL_GROUPS

MLSTM_HEADS = 4
MLSTM_DV = D_MODEL // MLSTM_HEADS
MLSTM_DK = MLSTM_DV // 2
MLSTM_CHUNK = 64
MLSTM_IN = 2 * MLSTM_HEADS * MLSTM_DK + 2 * MLSTM_HEADS * MLSTM_DV + 4 * MLSTM_HEADS

MLA_HEADS = 8
MLA_NOPE = 128
MLA_ROPE = 64
MLA_V = 128
MLA_Q_LORA = 512
MLA_KV_LORA = 256
MLA_IN = MLA_Q_LORA + MLA_KV_LORA + MLA_ROPE
ATTN_SCALE = (MLA_NOPE + MLA_ROPE) ** -0.5
ROPE_BASE = 10000.0
Q_BLOCK = 128

N_EXPERTS = 16
N_EXPERT_GROUPS = 4
EXPERTS_PER_GROUP = N_EXPERTS // N_EXPERT_GROUPS
TOP_K = 2
D_EXPERT = 256

kernel_name = "hybrid_pool_mlstm_mla_moe_diffusion_step"


def rmsnorm(x, g):
    xf = x.astype(jnp.float32)
    y = xf * lax.rsqrt(jnp.mean(xf * xf, axis=-1, keepdims=True) + NORM_EPS)
    return (y * g.astype(jnp.float32)).astype(x.dtype)


def ada_mod(cvec, w, b):
    m = jax.nn.silu(cvec) @ w + b
    return m.reshape(cvec.shape[0], 6, D_MODEL)


def modulate(h, shift, scale):
    return h * (1.0 + scale[:, None, :]) + shift[:, None, :]


def pool_mixer(h, w_pool, scale):
    B, N, D = h.shape
    hf = h.astype(jnp.float32)
    cs = jnp.pad(jnp.cumsum(hf, axis=1), ((0, 0), (1, 0), (0, 0)))
    t = jnp.arange(N)
    outs = []
    for g, w in enumerate(POOL_WINDOWS):
        lo = jnp.clip(t - w // 2, 0, N)
        hi = jnp.clip(t + w // 2, 0, N)
        cg = cs[:, :, g * POOL_GW:(g + 1) * POOL_GW]
        mean = (cg[:, hi] - cg[:, lo]) / (hi - lo).astype(jnp.float32)[None, :, None]
        outs.append(mean - hf[..., g * POOL_GW:(g + 1) * POOL_GW])
    pooled = jnp.stack(outs, axis=2).astype(h.dtype)
    y = jnp.einsum("bngc,gcd->bngd", pooled, w_pool).reshape(B, N, D)
    return y * scale


def mlstm_zero_state(b):
    return (jnp.zeros((b, MLSTM_HEADS, MLSTM_DK, MLSTM_DV), jnp.float32),
            jnp.zeros((b, MLSTM_HEADS, MLSTM_DK), jnp.float32),
            jnp.zeros((b, MLSTM_HEADS), jnp.float32))


def mlstm_chunkwise(q, k, v, i_raw, f_raw, state):
    C0, n0, m0 = (s.astype(jnp.float32) for s in state)
    B, H, N, _ = q.shape
    L = MLSTM_CHUNK
    nc = N // L

    def chunks(a):
        return jnp.moveaxis(a.reshape(a.shape[:2] + (nc, L) + a.shape[3:]), 2, 0)

    inputs = (chunks(q.astype(jnp.float32)), chunks(k.astype(jnp.float32)),
              chunks(v.astype(jnp.float32)), chunks(i_raw.astype(jnp.float32)),
              chunks(jax.nn.log_sigmoid(f_raw.astype(jnp.float32))))
    tri = jnp.tril(jnp.ones((L, L), dtype=bool))

    def step(carry, inp):
        C, n, m = carry
        qc, kc, vc, ic, lfc = inp
        b = jnp.cumsum(lfc, axis=-1)
        logD = jnp.where(tri, b[..., :, None] - b[..., None, :] + ic[..., None, :], -jnp.inf)
        log_state = b + m[..., None]
        m_t = jnp.maximum(log_state, jnp.max(logD, axis=-1))
        Dw = jnp.exp(logD - m_t[..., None])
        sw = jnp.exp(log_state - m_t)
        A = jnp.einsum("bhtd,bhsd->bhts", qc, kc) * Dw
        num = sw[..., None] * jnp.einsum("bhtd,bhde->bhte", qc, C) + jnp.einsum("bhts,bhse->bhte", A, vc)
        den = sw * jnp.einsum("bhtd,bhd->bht", qc, n) + jnp.sum(A, axis=-1)
        h = num / jnp.maximum(jnp.abs(den), jnp.exp(-m_t))[..., None]
        bL = b[..., -1]
        log_k = bL[..., None] - b + ic
        m_new = jnp.maximum(bL + m, jnp.max(log_k, axis=-1))
        kw = jnp.exp(log_k - m_new[..., None])
        decay = jnp.exp(bL + m - m_new)
        C_new = decay[..., None, None] * C + jnp.einsum("bhs,bhsd,bhse->bhde", kw, kc, vc)
        n_new = decay[..., None] * n + jnp.einsum("bhs,bhsd->bhd", kw, kc)
        return (C_new, n_new, m_new), h

    (C, n, m), hs = lax.scan(step, (C0, n0, m0), inputs)
    h = jnp.moveaxis(hs, 0, 2).reshape(B, H, N, MLSTM_DV)
    return h, (C, n, m)


def mlstm_mixer(h, w_in, b_gate, head_g, w_out, init_f, init_b):
    B, N, _ = h.shape
    hk, hv = MLSTM_HEADS * MLSTM_DK, MLSTM_HEADS * MLSTM_DV
    proj = h @ w_in

    def heads(a, d):
        return a.reshape(B, N, MLSTM_HEADS, d).transpose(0, 2, 1, 3)

    q = heads(proj[..., :hk], MLSTM_DK) * (MLSTM_DK ** -0.5)
    k = heads(proj[..., hk:2 * hk], MLSTM_DK)
    v = heads(proj[..., 2 * hk:2 * hk + hv], MLSTM_DV)
    o = proj[..., 2 * hk + hv:2 * hk + 2 * hv]
    g = (proj[..., 2 * hk + 2 * hv:].reshape(B, N, 4, MLSTM_HEADS) + b_gate).transpose(2, 0, 3, 1)
    h_f, st_f = mlstm_chunkwise(q, k, v, g[0], g[1], init_f)
    fl = lambda a: jnp.flip(a, axis=2)
    h_b, st_b = mlstm_chunkwise(fl(q), fl(k), fl(v), fl(g[2]), fl(g[3]), init_b)
    hs = h_f + fl(h_b)
    hs = hs * lax.rsqrt(jnp.mean(hs * hs, axis=-1, keepdims=True) + NORM_EPS)
    hs = hs.transpose(0, 2, 1, 3).reshape(B, N, hv) * head_g.astype(jnp.float32)
    y = (jax.nn.sigmoid(o.astype(jnp.float32)) * hs).astype(h.dtype) @ w_out
    return y, st_f, st_b


def rope_1d(x, pos):
    d = x.shape[-1]
    half = d // 2
    freq = jnp.power(ROPE_BASE, -jnp.arange(half, dtype=jnp.float32) / half)
    ang = pos.astype(jnp.float32)[:, None] * freq[None, :]
    shape = (x.shape[1],) + (1,) * (x.ndim - 3) + (d,)
    cos = jnp.concatenate([jnp.cos(ang), jnp.cos(ang)], -1).reshape(shape).astype(x.dtype)
    sin = jnp.concatenate([jnp.sin(ang), jnp.sin(ang)], -1).reshape(shape).astype(x.dtype)
    rot = jnp.concatenate([-x[..., half:], x[..., :half]], -1)
    return x * cos + rot * sin


def rope_axial(x, n):
    rows = n // GRID_W
    row = jnp.repeat(jnp.arange(rows), GRID_W)
    col = jnp.tile(jnp.arange(GRID_W), rows)
    r = MLA_ROPE // 2
    return jnp.concatenate([rope_1d(x[..., :r], row), rope_1d(x[..., r:], col)], -1)


def mla_project(h, w_in, q_g, kv_g, w_qb):
    B, N, _ = h.shape
    proj = h @ w_in
    q_lat = proj[..., :MLA_Q_LORA]
    ckv = rmsnorm(proj[..., MLA_Q_LORA:MLA_Q_LORA + MLA_KV_LORA], kv_g)
    kpe = proj[..., MLA_Q_LORA + MLA_KV_LORA:]
    q = (rmsnorm(q_lat, q_g) @ w_qb).reshape(B, N, MLA_HEADS, MLA_NOPE + MLA_ROPE)
    return q[..., :MLA_NOPE], q[..., MLA_NOPE:], ckv, kpe


def mla_expand(ckv, w_kvb):
    B, N, _ = ckv.shape
    kv = (ckv @ w_kvb).reshape(B, N, MLA_HEADS, MLA_NOPE + MLA_V)
    return kv[..., :MLA_NOPE], kv[..., MLA_NOPE:]


def block_attention(qn, qp, kn, kp, v):
    B, Nq, H, _ = qn.shape
    nb = Nq // Q_BLOCK
    qn_b = qn.reshape(B, nb, Q_BLOCK, H, MLA_NOPE).swapaxes(0, 1)
    qp_b = qp.reshape(B, nb, Q_BLOCK, H, MLA_ROPE).swapaxes(0, 1)

    def one(args):
        qnb, qpb = args
        s = (jnp.einsum("bqhd,bkhd->bhqk", qnb, kn) + jnp.einsum("bqhr,bkr->bhqk", qpb, kp))
        p = jax.nn.softmax(s.astype(jnp.float32) * ATTN_SCALE, axis=-1).astype(v.dtype)
        return jnp.einsum("bhqk,bkhd->bqhd", p, v)

    o = lax.map(one, (qn_b, qp_b))
    return o.swapaxes(0, 1).reshape(B, Nq, H * MLA_V)


def moe(h, w_router, b_router, wg, wu, wd):
    B, N, D = h.shape
    x = h.reshape(B * N, D)
    T = x.shape[0]
    scores = jax.nn.sigmoid((x @ w_router).astype(jnp.float32))
    sel = scores + b_router.astype(jnp.float32)
    gsc = jnp.sum(lax.top_k(sel.reshape(T, N_EXPERT_GROUPS, EXPERTS_PER_GROUP), 2)[0], axis=-1)
    best = jnp.argmax(gsc, axis=-1)
    in_group = (jnp.arange(N_EXPERTS) // EXPERTS_PER_GROUP)[None, :] == best[:, None]
    _, idx = lax.top_k(jnp.where(in_group, sel, -jnp.inf), TOP_K)
    w = jnp.take_along_axis(scores, idx, axis=-1)
    w = w / jnp.sum(w, axis=-1, keepdims=True)
    combine = jnp.sum(jax.nn.one_hot(idx, N_EXPERTS, dtype=jnp.float32) * w[..., None], axis=1)
    hid = jax.nn.silu(jnp.einsum("td,edf->tef", x, wg)) * jnp.einsum("td,edf->tef", x, wu)
    hid = hid * combine[..., None].astype(hid.dtype)
    y = jnp.einsum("tef,efd->td", hid, wd)
    return y.reshape(B, N, D)


def setup_inputs(seed: int = 0) -> dict:
    key = jax.random.key(seed)
    ks = jax.random.split(key, 32)
    f32 = jnp.float32

    def nrm(k, shape, scale):
        return jax.random.normal(k, shape, f32) * scale

    D = D_MODEL
    H = MLSTM_HEADS
    gate_off = jnp.array([0.0, 3.0, 0.0, 3.0], f32)[None, :, None]
    return {
        "x_prompt": nrm(ks[0], (BATCH, SEQ, D), 1.0),
        "x_sample": nrm(ks[1], (DEC_BATCH, DEC_SEQ, D), 1.0),
        "state_mlstm_C": nrm(ks[2], (DEC_BATCH, N_MLSTM_LAYERS, 2, H, MLSTM_DK, MLSTM_DV), 0.5),
        "state_mlstm_n": nrm(ks[3], (DEC_BATCH, N_MLSTM_LAYERS, 2, H, MLSTM_DK), 0.5),
        "state_mlstm_m": nrm(ks[4], (DEC_BATCH, N_MLSTM_LAYERS, 2, H), 0.5),
        "cache_mla_ckv": nrm(ks[5], (DEC_BATCH, N_MLA_LAYERS, PAST_LEN, MLA_KV_LORA), 1.0),
        "cache_mla_kpe": nrm(ks[6], (DEC_BATCH, N_MLA_LAYERS, PAST_LEN, MLA_ROPE), 1.0),
        "c": nrm(ks[7], (DEC_BATCH, D), 1.0),
        "c_ctx": nrm(ks[8], (D,), 1.0),
        "w_ada": nrm(ks[9], (DEPTH, D, 6 * D), 0.5 * D ** -0.5),
        "b_ada": nrm(ks[10], (DEPTH, 6 * D), 0.02),
        "norm_mix": 1.0 + nrm(ks[11], (DEPTH, D), 0.05),
        "norm_ffn": 1.0 + nrm(ks[12], (DEPTH, D), 0.05),
        "norm_final": 1.0 + nrm(ks[13], (D,), 0.05),
        "w_pool": nrm(ks[14], (N_POOL_LAYERS, POOL_GROUPS, POOL_GW, POOL_GW), POOL_GW ** -0.5),
        "pool_scale": 1.0 + nrm(ks[15], (N_POOL_LAYERS, D), 0.1),
        "w_mlstm_in": nrm(ks[16], (N_MLSTM_LAYERS, D, MLSTM_IN), D ** -0.5),
        "b_mlstm_gate": gate_off + nrm(ks[17], (N_MLSTM_LAYERS, 4, H), 0.1),
        "mlstm_head_g": 1.0 + nrm(ks[18], (N_MLSTM_LAYERS, H * MLSTM_DV), 0.05),
        "w_mlstm_out": nrm(ks[19], (N_MLSTM_LAYERS, H * MLSTM_DV, D), (H * MLSTM_DV) ** -0.5),
        "w_mla_in": nrm(ks[20], (N_MLA_LAYERS, D, MLA_IN), D ** -0.5),
        "mla_q_g": 1.0 + nrm(ks[21], (N_MLA_LAYERS, MLA_Q_LORA), 0.05),
        "mla_kv_g": 1.0 + nrm(ks[22], (N_MLA_LAYERS, MLA_KV_LORA), 0.05),
        "w_mla_qb": nrm(ks[23], (N_MLA_LAYERS, MLA_Q_LORA, MLA_HEADS * (MLA_NOPE + MLA_ROPE)), MLA_Q_LORA ** -0.5),
        "w_mla_kvb": nrm(ks[24], (N_MLA_LAYERS, MLA_KV_LORA, MLA_HEADS * (MLA_NOPE + MLA_V)), MLA_KV_LORA ** -0.5),
        "w_mla_out": nrm(ks[25], (N_MLA_LAYERS, MLA_HEADS * MLA_V, D), (MLA_HEADS * MLA_V) ** -0.5),
        "w_router": nrm(ks[26], (D, N_EXPERTS), D ** -0.5),
        "b_router": nrm(ks[27], (N_EXPERTS,), 0.01),
        "w_exp_gate": nrm(ks[28], (DEPTH, N_EXPERTS, D, D_EXPERT), D ** -0.5),
        "w_exp_up": nrm(ks[29], (DEPTH, N_EXPERTS, D, D_EXPERT), D ** -0.5),
        "w_exp_down": nrm(ks[30], (DEPTH, N_EXPERTS, D_EXPERT, D), D_EXPERT ** -0.5),
    }


def reference(x_prompt, x_sample, state_mlstm_C, state_mlstm_n, state_mlstm_m, cache_mla_ckv,
              cache_mla_kpe, c, c_ctx, w_ada, b_ada, norm_mix, norm_ffn, norm_final, w_pool,
              pool_scale, w_mlstm_in, b_mlstm_gate, mlstm_head_g, w_mlstm_out, w_mla_in, mla_q_g,
              mla_kv_g, w_mla_qb, w_mla_kvb, w_mla_out, w_router, b_router, w_exp_gate, w_exp_up,
              w_exp_down):
    x = x_prompt
    new_C, new_n, new_m, new_ckv, new_kpe = [], [], [], [], []
    for i in range(DEPTH):
        kind, j = i % N_MIXERS, i // N_MIXERS
        mod = ada_mod(c_ctx[None, :], w_ada[i], b_ada[i])
        h = modulate(rmsnorm(x, norm_mix[i]), mod[:, 0], mod[:, 1])
        if kind == 0:
            y = pool_mixer(h, w_pool[j], pool_scale[j])
        elif kind == 1:
            z = mlstm_zero_state(x.shape[0])
            y, st_f, st_b = mlstm_mixer(h, w_mlstm_in[j], b_mlstm_gate[j], mlstm_head_g[j],
                                        w_mlstm_out[j], z, z)
            new_C.append(jnp.stack([st_f[0], st_b[0]], axis=1))
            new_n.append(jnp.stack([st_f[1], st_b[1]], axis=1))
            new_m.append(jnp.stack([st_f[2], st_b[2]], axis=1))
        else:
            qn, qp, ckv, kpe = mla_project(h, w_mla_in[j], mla_q_g[j], mla_kv_g[j], w_mla_qb[j])
            kn, v = mla_expand(ckv, w_mla_kvb[j])
            y = block_attention(qn, qp, kn, kpe, v) @ w_mla_out[j]
            new_ckv.append(ckv)
            new_kpe.append(kpe)
        x = x + mod[:, None, 2] * y
        h = modulate(rmsnorm(x, norm_ffn[i]), mod[:, 3], mod[:, 4])
        x = x + mod[:, None, 5] * moe(h, w_router, b_router, w_exp_gate[i], w_exp_up[i], w_exp_down[i])
    y_prompt = rmsnorm(x, norm_final)
    new_state_mlstm_C = jnp.stack(new_C, axis=1)
    new_state_mlstm_n = jnp.stack(new_n, axis=1)
    new_state_mlstm_m = jnp.stack(new_m, axis=1)
    new_cache_mla_ckv = jnp.stack(new_ckv, axis=1)
    new_cache_mla_kpe = jnp.stack(new_kpe, axis=1)

    x = x_sample
    n_lat = x_sample.shape[1]
    for i in range(DEPTH):
        kind, j = i % N_MIXERS, i // N_MIXERS
        mod = ada_mod(c, w_ada[i], b_ada[i])
        h = modulate(rmsnorm(x, norm_mix[i]), mod[:, 0], mod[:, 1])
        if kind == 0:
            y = pool_mixer(h, w_pool[j], pool_scale[j])
        elif kind == 1:
            init_f = (state_mlstm_C[:, j, 0], state_mlstm_n[:, j, 0], state_mlstm_m[:, j, 0])
            init_b = (state_mlstm_C[:, j, 1], state_mlstm_n[:, j, 1], state_mlstm_m[:, j, 1])
            y, _, _ = mlstm_mixer(h, w_mlstm_in[j], b_mlstm_gate[j], mlstm_head_g[j],
                                  w_mlstm_out[j], init_f, init_b)
        else:
            qn, qp, ckv, kpe = mla_project(h, w_mla_in[j], mla_q_g[j], mla_kv_g[j], w_mla_qb[j])
            qp = rope_axial(qp, n_lat)
            kpe = rope_axial(kpe, n_lat)
            ckv_all = jnp.concatenate([cache_mla_ckv[:, j].astype(ckv.dtype), ckv], axis=1)
            kpe_all = jnp.concatenate([cache_mla_kpe[:, j].astype(kpe.dtype), kpe], axis=1)
            kn, v = mla_expand(ckv_all, w_mla_kvb[j])
            y = block_attention(qn, qp, kn, kpe_all, v) @ w_mla_out[j]
        x = x + mod[:, None, 2] * y
        h = modulate(rmsnorm(x, norm_ffn[i]), mod[:, 3], mod[:, 4])
        x = x + mod[:, None, 5] * moe(h, w_router, b_router, w_exp_gate[i], w_exp_up[i], w_exp_down[i])
    y_sample = rmsnorm(x, norm_final)
    return (y_prompt, y_sample, new_state_mlstm_C, new_state_mlstm_n, new_state_mlstm_m,
            new_cache_mla_ckv, new_cache_mla_kpe)
```

```python
import functools

import numpy as np
import jax
import jax.numpy as jnp
from jax import lax
from jax.experimental import pallas as pl
from jax.experimental.pallas import tpu as pltpu

F32 = jnp.float32
BF16 = jnp.bfloat16

NORM_EPS = 1e-6
GRID_W = 64
POOL_WINDOWS = (2, 4, 8, 16)
MLSTM_HEADS = 4
MLSTM_CHUNK = 64
MLA_HEADS = 8
MLA_NOPE = 128
MLA_ROPE = 64
MLA_V = 128
ROPE_BASE = 10000.0
N_EXPERTS = 16
N_EXPERT_GROUPS = 4
EXPERTS_PER_GROUP = N_EXPERTS // N_EXPERT_GROUPS

LANES = 128
SUBLANES = 8
VMEM_LIMIT = 56 * 1024 * 1024
POOL_TILE = 256
POOL_HALO = 8
ATTN_Q_BLOCK = 256


def _params(*sem):
    return pltpu.CompilerParams(dimension_semantics=sem, vmem_limit_bytes=VMEM_LIMIT)


def _rms(x, g):
    return x * lax.rsqrt(jnp.mean(x * x, axis=-1, keepdims=True) + NORM_EPS) * g


def _modulated(x, g, shift, scale):
    return _rms(x, g) * (1.0 + scale) + shift


def _silu(x):
    return x * jax.nn.sigmoid(x)


def _bdot(a, b):
    return jnp.dot(a.astype(BF16), b.astype(BF16), preferred_element_type=F32)


def _bdot_nt(a, b):
    return lax.dot_general(a.astype(BF16), b.astype(BF16), (((1,), (1,)), ((), ())),
                           preferred_element_type=F32)


def _bdot_tn(a, b):
    return lax.dot_general(a.astype(BF16), b.astype(BF16), (((0,), (0,)), ((), ())),
                           preferred_element_type=F32)


def _cvec_index(n_ctx, dec_seq, tm):
    def idx(i):
        r = i * tm
        return jnp.where(r < n_ctx, 0, (r - n_ctx) // dec_seq + 1)
    return idx


def _ada_kernel(c_ref, w_ref, b_ref, o_ref):
    o_ref[...] = _bdot(_silu(c_ref[...]), w_ref[...]) + b_ref[...]


def ada_mod_all(cvecs, w_ada, b_ada, tn=1536):
    depth, d, n6 = w_ada.shape
    rows = cvecs.shape[0]
    return pl.pallas_call(
        _ada_kernel,
        grid=(depth, n6 // tn),
        in_specs=[pl.BlockSpec((rows, d), lambda l, n: (0, 0)),
                  pl.BlockSpec((None, d, tn), lambda l, n: (l, 0, n)),
                  pl.BlockSpec((None, 1, tn), lambda l, n: (l, 0, n))],
        out_specs=pl.BlockSpec((None, rows, tn), lambda l, n: (l, 0, n)),
        out_shape=jax.ShapeDtypeStruct((depth, rows, n6), F32),
        compiler_params=_params("arbitrary", "arbitrary"),
        name="ada_mod",
    )(cvecs, w_ada, b_ada.reshape(depth, 1, n6))


def _pool_kernel(xc_ref, xp_ref, xn_ref, g_ref, sh_ref, sc_ref, gt_ref, wp_ref, ps_ref, o_ref,
                 buf_ref, *, n_ctx_tiles, ctx_seq_tiles, lat_seq_tiles):
    i = pl.program_id(0)
    is_ctx = i < n_ctx_tiles
    j = jnp.where(is_ctx, i % ctx_seq_tiles, (i - n_ctx_tiles) % lat_seq_tiles)
    nt = jnp.where(is_ctx, ctx_seq_tiles, lat_seq_tiles)
    g, sh, sc = g_ref[...], sh_ref[...], sc_ref[...]
    tp, hl = POOL_TILE, POOL_HALO
    gw = xc_ref.shape[1] // len(POOL_WINDOWS)

    xc = xc_ref[...]
    buf_ref[pl.ds(0, hl), :] = jnp.where(j == 0, 0.0, _modulated(xp_ref[...], g, sh, sc))
    buf_ref[pl.ds(hl, tp), :] = _modulated(xc, g, sh, sc)
    buf_ref[pl.ds(hl + tp, hl), :] = jnp.where(j == nt - 1, 0.0, _modulated(xn_ref[...], g, sh, sc))

    pos = j * tp + lax.broadcasted_iota(jnp.int32, (tp, 1), 0)
    seq_len = nt * tp
    for gi, w in enumerate(POOL_WINDOWS):
        cols = pl.ds(gi * gw, gw)
        acc = buf_ref[pl.ds(hl - w // 2, tp), cols]
        for d in range(-w // 2 + 1, w // 2):
            acc = acc + buf_ref[pl.ds(hl + d, tp), cols]
        cnt = jnp.minimum(pos + w // 2, seq_len) - jnp.maximum(pos - w // 2, 0)
        pooled = acc / cnt.astype(F32) - buf_ref[pl.ds(hl, tp), cols]
        y = _bdot(pooled, wp_ref[gi]) * ps_ref[:, cols]
        o_ref[:, cols] = xc[:, gi * gw:(gi + 1) * gw] + gt_ref[:, cols] * y


def pool_layer(x, g, shift, scale, gate, w_pool, pool_scale, n_ctx, seq, dec_seq):
    t, d = x.shape
    tp, hl = POOL_TILE, POOL_HALO
    cidx = _cvec_index(n_ctx, dec_seq, tp)
    hb = tp // hl
    last_hblk = t // hl - 1
    mod_spec = pl.BlockSpec((None, 1, d), lambda i: (cidx(i), 0, 0))
    row_spec = pl.BlockSpec((1, d), lambda i: (0, 0))
    kern = functools.partial(_pool_kernel, n_ctx_tiles=n_ctx // tp, ctx_seq_tiles=seq // tp,
                             lat_seq_tiles=dec_seq // tp)
    return pl.pallas_call(
        kern,
        grid=(t // tp,),
        in_specs=[pl.BlockSpec((tp, d), lambda i: (i, 0)),
                  pl.BlockSpec((hl, d), lambda i: (jnp.maximum(i * hb - 1, 0), 0)),
                  pl.BlockSpec((hl, d), lambda i: (jnp.minimum((i + 1) * hb, last_hblk), 0)),
                  row_spec, mod_spec, mod_spec, mod_spec,
                  pl.BlockSpec(w_pool.shape, lambda i: (0, 0, 0)),
                  row_spec],
        out_specs=pl.BlockSpec((tp, d), lambda i: (i, 0)),
        out_shape=jax.ShapeDtypeStruct((t, d), F32),
        scratch_shapes=[pltpu.VMEM((tp + 2 * hl, d), F32)],
        compiler_params=_params("arbitrary"),
        name="pool_mixer",
    )(x, x, x, g, shift, scale, gate, w_pool, pool_scale)


def _route(sel, scores):
    e, tm = sel.shape
    row = lax.broadcasted_iota(jnp.int32, (e, tm), 0)
    best = jnp.zeros((1, tm), jnp.int32)
    best_sc = None
    for gidx in range(N_EXPERT_GROUPS):
        r = [sel[gidx * EXPERTS_PER_GROUP + k:gidx * EXPERTS_PER_GROUP + k + 1, :]
             for k in range(EXPERTS_PER_GROUP)]
        top2 = None
        for a in range(EXPERTS_PER_GROUP):
            for b in range(a + 1, EXPERTS_PER_GROUP):
                s = r[a] + r[b]
                top2 = s if top2 is None else jnp.maximum(top2, s)
        if best_sc is None:
            best_sc = top2
        else:
            better = top2 > best_sc
            best = jnp.where(better, gidx, best)
            best_sc = jnp.where(better, top2, best_sc)
    neg = -jnp.inf
    masked = jnp.where(row // EXPERTS_PER_GROUP == best, sel, neg)
    m1 = jnp.max(masked, axis=0, keepdims=True)
    i1 = jnp.min(jnp.where(masked == m1, row, e), axis=0, keepdims=True)
    masked2 = jnp.where(row == i1, neg, masked)
    m2 = jnp.max(masked2, axis=0, keepdims=True)
    i2 = jnp.min(jnp.where(masked2 == m2, row, e), axis=0, keepdims=True)
    hot1 = row == i1
    hot2 = row == i2
    w1 = jnp.sum(jnp.where(hot1, scores, 0.0), axis=0, keepdims=True)
    w2 = jnp.sum(jnp.where(hot2, scores, 0.0), axis=0, keepdims=True)
    tot = w1 + w2
    return jnp.where(hot1, w1 / tot, 0.0) + jnp.where(hot2, w2 / tot, 0.0)


def _moe_kernel(x_ref, g_ref, sh_ref, sc_ref, gt_ref, wr_ref, br_ref, wg_ref, wu_ref, wd_ref,
                gf_ref, o_ref, h_scr, comb_scr, acc_scr, *, final_norm):
    e = pl.program_id(1)

    @pl.when(e == 0)
    def _():
        h = _modulated(x_ref[...], g_ref[...], sh_ref[...], sc_ref[...])
        h_scr[...] = h.astype(BF16)
        logits = jnp.dot(h, wr_ref[...], precision=lax.Precision.HIGHEST,
                         preferred_element_type=F32)
        logits_t = logits.T[:N_EXPERTS, :]
        scores = jax.nn.sigmoid(logits_t)
        comb_t = _route(scores + br_ref[...], scores)
        pad = jnp.zeros((LANES - N_EXPERTS, comb_t.shape[1]), F32)
        comb_scr[...] = jnp.concatenate([comb_t, pad], axis=0).T
        acc_scr[...] = jnp.zeros_like(acc_scr)

    hb = h_scr[...]
    gate = jnp.dot(hb, wg_ref[...].astype(BF16), preferred_element_type=F32)
    up = jnp.dot(hb, wu_ref[...].astype(BF16), preferred_element_type=F32)
    comb = comb_scr[...]
    lane = lax.broadcasted_iota(jnp.int32, comb.shape, 1)
    c_e = jnp.sum(jnp.where(lane == e, comb, 0.0), axis=1, keepdims=True)
    hid = _silu(gate) * up * c_e
    acc_scr[...] += _bdot(hid, wd_ref[...])

    @pl.when(e == pl.num_programs(1) - 1)
    def _():
        out = x_ref[...] + gt_ref[...] * acc_scr[...]
        if final_norm:
            out = _rms(out, gf_ref[...])
        o_ref[...] = out


def moe_layer(x, g, shift, scale, gate, w_router_pad, b_router_col, wg, wu, wd, g_final,
              n_ctx, dec_seq, final_norm, tm=1024):
    t, d = x.shape
    n_e, _, f = wg.shape
    cidx = _cvec_index(n_ctx, dec_seq, tm)
    mod_spec = pl.BlockSpec((None, 1, d), lambda i, e: (cidx(i), 0, 0))
    row_spec = pl.BlockSpec((1, d), lambda i, e: (0, 0))
    return pl.pallas_call(
        functools.partial(_moe_kernel, final_norm=final_norm),
        grid=(t // tm, n_e),
        in_specs=[pl.BlockSpec((tm, d), lambda i, e: (i, 0)),
                  row_spec, mod_spec, mod_spec, mod_spec,
                  pl.BlockSpec(w_router_pad.shape, lambda i, e: (0, 0)),
                  pl.BlockSpec(b_router_col.shape, lambda i, e: (0, 0)),
                  pl.BlockSpec((None, d, f), lambda i, e: (e, 0, 0)),
                  pl.BlockSpec((None, d, f), lambda i, e: (e, 0, 0)),
                  pl.BlockSpec((None, f, d), lambda i, e: (e, 0, 0)),
                  row_spec],
        out_specs=pl.BlockSpec((tm, d), lambda i, e: (i, 0)),
        out_shape=jax.ShapeDtypeStruct((t, d), F32),
        scratch_shapes=[pltpu.VMEM((tm, d), BF16), pltpu.VMEM((tm, LANES), F32),
                        pltpu.VMEM((tm, d), F32)],
        compiler_params=_params("arbitrary", "arbitrary"),
        name="moe",
    )(x, g, shift, scale, gate, w_router_pad, b_router_col, wg, wu, wd, g_final)


def _resid_proj_kernel(x_ref, a_ref, w_ref, gt_ref, o_ref):
    o_ref[...] = x_ref[...] + gt_ref[...] * _bdot(a_ref[...], w_ref[...])


def _mlstm_out_kernel(x_ref, hs_ref, og_ref, hg_ref, w_ref, gt_ref, o_ref):
    a = jax.nn.sigmoid(og_ref[...]) * (hs_ref[...] * hg_ref[...])
    o_ref[...] = x_ref[...] + gt_ref[...] * _bdot(a, w_ref[...])


def resid_proj(x, a, w, gate, n_ctx, dec_seq, tm=512):
    t, d = x.shape
    k = a.shape[1]
    cidx = _cvec_index(n_ctx, dec_seq, tm)
    return pl.pallas_call(
        _resid_proj_kernel,
        grid=(t // tm,),
        in_specs=[pl.BlockSpec((tm, d), lambda i: (i, 0)),
                  pl.BlockSpec((tm, k), lambda i: (i, 0)),
                  pl.BlockSpec((k, d), lambda i: (0, 0)),
                  pl.BlockSpec((None, 1, d), lambda i: (cidx(i), 0, 0))],
        out_specs=pl.BlockSpec((tm, d), lambda i: (i, 0)),
        out_shape=jax.ShapeDtypeStruct((t, d), F32),
        compiler_params=_params("arbitrary"),
        name="resid_proj",
    )(x, a, w, gate)


def mlstm_out(x, hs, qkvo_f32, head_g, w, gate, n_ctx, dec_seq, o_col_block, tm=512):
    t, d = x.shape
    k = hs.shape[1]
    cidx = _cvec_index(n_ctx, dec_seq, tm)
    return pl.pallas_call(
        _mlstm_out_kernel,
        grid=(t // tm,),
        in_specs=[pl.BlockSpec((tm, d), lambda i: (i, 0)),
                  pl.BlockSpec((tm, k), lambda i: (i, 0)),
                  pl.BlockSpec((tm, k), lambda i: (i, o_col_block)),
                  pl.BlockSpec((1, k), lambda i: (0, 0)),
                  pl.BlockSpec((k, d), lambda i: (0, 0)),
                  pl.BlockSpec((None, 1, d), lambda i: (cidx(i), 0, 0))],
        out_specs=pl.BlockSpec((tm, d), lambda i: (i, 0)),
        out_shape=jax.ShapeDtypeStruct((t, d), F32),
        compiler_params=_params("arbitrary"),
        name="mlstm_out",
    )(x, hs, qkvo_f32, head_g, w, gate)


def _mlstm_proj_kernel(x_ref, g_ref, sh_ref, sc_ref, wqk_ref, wv_ref, wo_ref, wgt_ref, bg_ref,
                       wgtt_ref, bgt_ref, qs_ref, qkv_ref, o_ref, gc_ref, gr_ref):
    h = _modulated(x_ref[...], g_ref[...], sh_ref[...], sc_ref[...]).astype(BF16)
    nqk = wqk_ref.shape[1]
    qkv_ref[:, :nqk] = (jnp.dot(h, wqk_ref[...].astype(BF16), preferred_element_type=F32)
                        * qs_ref[...]).astype(BF16)
    qkv_ref[:, nqk:] = jnp.dot(h, wv_ref[...].astype(BF16),
                               preferred_element_type=F32).astype(BF16)
    o_ref[...] = jnp.dot(h, wo_ref[...].astype(BF16), preferred_element_type=F32)
    gc_ref[...] = jnp.dot(h, wgt_ref[...].astype(BF16), preferred_element_type=F32) + bg_ref[...]
    gr_ref[...] = _bdot_nt(wgtt_ref[...], h) + bgt_ref[...]


def mlstm_proj(x, g, shift, scale, w_in, b_gate, n_ctx, dec_seq, tm=512):
    t, d = x.shape
    hh = MLSTM_HEADS
    hv = d
    hk = hv // 2
    ng = 4 * hh
    w_qk = w_in[:, :2 * hk]
    w_v = w_in[:, 2 * hk:2 * hk + hv]
    w_o = w_in[:, 2 * hk + hv:2 * hk + 2 * hv]
    w_g = w_in[:, 2 * hk + 2 * hv:]
    w_g_pad = jnp.pad(w_g, ((0, 0), (0, LANES - ng)))
    b_pad = jnp.pad(b_gate.reshape(1, ng), ((0, 0), (0, LANES - ng)))
    w_g_t = w_g.T
    b_t = b_gate.reshape(ng, 1)
    dk = hk // hh
    q_scale = jnp.concatenate([jnp.full((1, hk), dk ** -0.5, F32), jnp.ones((1, hk), F32)], axis=1)
    cidx = _cvec_index(n_ctx, dec_seq, tm)
    mod_spec = pl.BlockSpec((None, 1, d), lambda i: (cidx(i), 0, 0))
    full = lambda a: pl.BlockSpec(a.shape, lambda i: (0,) * a.ndim)
    return pl.pallas_call(
        _mlstm_proj_kernel,
        grid=(t // tm,),
        in_specs=[pl.BlockSpec((tm, d), lambda i: (i, 0)), full(g), mod_spec, mod_spec,
                  full(w_qk), full(w_v), full(w_o), full(w_g_pad), full(b_pad), full(w_g_t),
                  full(b_t), full(q_scale)],
        out_specs=[pl.BlockSpec((tm, 2 * hk + hv), lambda i: (i, 0)),
                   pl.BlockSpec((tm, hv), lambda i: (i, 0)),
                   pl.BlockSpec((tm, LANES), lambda i: (i, 0)),
                   pl.BlockSpec((ng, tm), lambda i: (0, i))],
        out_shape=[jax.ShapeDtypeStruct((t, 2 * hk + hv), BF16),
                   jax.ShapeDtypeStruct((t, hv), F32),
                   jax.ShapeDtypeStruct((t, LANES), F32),
                   jax.ShapeDtypeStruct((ng, t), F32)],
        compiler_params=_params("arbitrary"),
        name="mlstm_proj",
    )(x, g, shift, scale, w_qk, w_v, w_o, w_g_pad, b_pad, w_g_t, b_t, q_scale)


def _log_sigmoid(x):
    return jnp.minimum(x, 0.0) - jnp.log(1.0 + jnp.exp(-jnp.abs(x)))


def _mlstm_chunk(q, k, v, i_col, f_col, i_row, f_row, c_st, n_st, m_st, rev):
    l = q.shape[0]
    lf_col = _log_sigmoid(f_col)
    lf_row = _log_sigmoid(f_row)
    tt = lax.broadcasted_iota(jnp.int32, (l, l), 0)
    ss = lax.broadcasted_iota(jnp.int32, (l, l), 1)
    causal = (ss >= tt) if rev else (ss <= tt)
    b_col = jnp.sum(jnp.where(causal, lf_row, 0.0), axis=1, keepdims=True)
    feeds = (tt >= ss) if rev else (tt <= ss)
    b_row = jnp.sum(jnp.where(feeds, lf_col, 0.0), axis=0, keepdims=True)
    log_d = jnp.where(causal, b_col - b_row + i_row, -jnp.inf)
    log_state = b_col + m_st
    m_t = jnp.maximum(log_state, jnp.max(log_d, axis=1, keepdims=True))
    dw = jnp.exp(log_d - m_t)
    sw = jnp.exp(log_state - m_t)
    a = _bdot_nt(q, k) * dw
    num = sw * _bdot(q, c_st) + _bdot(a, v)
    qn = jnp.sum(q.astype(F32) * n_st, axis=1, keepdims=True)
    den = sw * qn + jnp.sum(a, axis=1, keepdims=True)
    h = num / jnp.maximum(jnp.abs(den), jnp.exp(-m_t))
    b_last = b_col[0:1, :] if rev else b_col[l - 1:l, :]
    log_k = b_last - b_col + i_col
    m_new = jnp.maximum(b_last + m_st, jnp.max(log_k, axis=0, keepdims=True))
    kw = jnp.exp(log_k - m_new)
    decay = jnp.exp(b_last + m_st - m_new)
    kwk = kw * k.astype(F32)
    c_new = decay * c_st + _bdot_tn(kwk, v)
    n_new = decay * n_st + jnp.sum(kwk, axis=0, keepdims=True)
    return h, c_new, n_new, m_new


def _mlstm_scan_kernel(q_ref, k_ref, v_ref, gc_ref, gr_ref, c0_ref, n0_ref, m0_ref,
                       hs_ref, c_ref, n_ref, m_ref, hb_scr, *, n_chunks):
    l = MLSTM_CHUNK

    def body(c, carry):
        cf, nf, mf, cb, nb, mb = carry
        rf = pl.ds(pl.multiple_of(c * l, l), l)
        cr = n_chunks - 1 - c
        rb = pl.ds(pl.multiple_of(cr * l, l), l)
        gcf, grf = gc_ref[rf, :], gr_ref[c]
        gcb, grb = gc_ref[rb, :], gr_ref[cr]
        hf, cf, nf, mf = _mlstm_chunk(q_ref[rf, :], k_ref[rf, :], v_ref[rf, :],
                                      gcf[:, 0:1], gcf[:, 1:2], grf[0:1, :], grf[1:2, :],
                                      cf, nf, mf, False)
        hb, cb, nb, mb = _mlstm_chunk(q_ref[rb, :], k_ref[rb, :], v_ref[rb, :],
                                      gcb[:, 2:3], gcb[:, 3:4], grb[2:3, :], grb[3:4, :],
                                      cb, nb, mb, True)
        hs_ref[rf, :] = hf
        hb_scr[rb, :] = hb
        return cf, nf, mf, cb, nb, mb

    init = (c0_ref[0], n0_ref[0], m0_ref[0], c0_ref[1], n0_ref[1], m0_ref[1])
    cf, nf, mf, cb, nb, mb = lax.fori_loop(0, n_chunks, body, init)
    c_ref[0], n_ref[0], m_ref[0] = cf, nf, mf
    c_ref[1], n_ref[1], m_ref[1] = cb, nb, mb
    hs = hs_ref[...] + hb_scr[...]
    hs_ref[...] = hs * lax.rsqrt(jnp.mean(hs * hs, axis=-1, keepdims=True) + NORM_EPS)


def mlstm_scan(qkv, gcol, grow, c0, n0, m0, row_off, n_seq, seq_len):
    hh = MLSTM_HEADS
    hv = qkv.shape[1] // 2
    dv = hv // hh
    dk = dv // 2
    l = MLSTM_CHUNK
    nc = seq_len // l
    ob = row_off // seq_len
    kern = functools.partial(_mlstm_scan_kernel, n_chunks=nc)
    st = lambda *tail: pl.BlockSpec((None, 2, None) + tail, lambda s, h: (s, 0, h) + (0,) * len(tail))
    return pl.pallas_call(
        kern,
        grid=(n_seq, hh),
        in_specs=[pl.BlockSpec((seq_len, dk), lambda s, h: (ob + s, h)),
                  pl.BlockSpec((seq_len, dk), lambda s, h: (ob + s, hh + h)),
                  pl.BlockSpec((seq_len, dv), lambda s, h: (ob + s, (2 * hh * dk) // dv + h)),
                  pl.BlockSpec((None, seq_len, 4), lambda s, h: (h, ob + s, 0)),
                  pl.BlockSpec((None, nc, 8, l), lambda s, h: (h, ob + s, 0, 0)),
                  st(dk, dv), st(1, dk), st(1, 1)],
        out_specs=[pl.BlockSpec((seq_len, dv), lambda s, h: (s, h)),
                   st(dk, dv), st(1, dk), st(1, 1)],
        out_shape=[jax.ShapeDtypeStruct((n_seq * seq_len, hv), F32),
                   jax.ShapeDtypeStruct((n_seq, 2, hh, dk, dv), F32),
                   jax.ShapeDtypeStruct((n_seq, 2, hh, 1, dk), F32),
                   jax.ShapeDtypeStruct((n_seq, 2, hh, 1, 1), F32)],
        scratch_shapes=[pltpu.VMEM((seq_len, dv), F32)],
        compiler_params=_params("arbitrary", "arbitrary"),
        name="mlstm_scan",
    )(qkv, qkv, qkv, gcol, grow, c0, n0, m0)


def _mla_proj_kernel(x_ref, g_ref, sh_ref, sc_ref, win_ref, qg_ref, kvg_ref, wqb_ref, cos_ref,
                     sin_ref, qn_ref, qp_ref, ckv_ref, kpe_ref, *, q_lora, kv_lora, rope, n_heads):
    h = _modulated(x_ref[...], g_ref[...], sh_ref[...], sc_ref[...])
    proj = _bdot(h, win_ref[...])
    q_lat = proj[:, :q_lora]
    ckv_ref[...] = _rms(proj[:, q_lora:q_lora + kv_lora], kvg_ref[...])
    cos, sin = cos_ref[...], sin_ref[...]
    kpe = proj[:, q_lora + kv_lora:q_lora + kv_lora + rope]
    kpe_rot = proj[:, q_lora + kv_lora + rope:q_lora + kv_lora + 2 * rope]
    kpe_ref[...] = kpe * cos[:, :rope] + kpe_rot * sin[:, :rope]
    q = _bdot(_rms(q_lat, qg_ref[...]), wqb_ref[...])
    n_nope = qn_ref.shape[1]
    n_rope = qp_ref.shape[1]
    qn_ref[...] = q[:, :n_nope].astype(BF16)
    reps = n_rope // cos.shape[1]
    cos_h = jnp.concatenate([cos] * reps, axis=1)
    sin_h = jnp.concatenate([sin] * reps, axis=1)
    qp_ref[...] = (q[:, n_nope:n_nope + n_rope] * cos_h
                   + q[:, n_nope + n_rope:] * sin_h).astype(BF16)


def _rot_cols(w):
    n = w.shape[1]
    j = np.arange(n)
    quarter = MLA_ROPE // 4
    first = (j % (2 * quarter)) < quarter
    src = np.where(first, j + quarter, j - quarter)
    sign = np.where(first, -1.0, 1.0).astype(np.float32)
    return w[:, src] * sign


def _rope_tables(n_ctx, dec_batch, dec_seq):
    quarter = MLA_ROPE // 4
    freq = np.power(np.float32(ROPE_BASE), -np.arange(quarter, dtype=np.float32) / np.float32(quarter))
    pos = np.arange(dec_seq)
    ang_r = (pos // GRID_W).astype(np.float32)[:, None] * freq[None, :]
    ang_c = (pos % GRID_W).astype(np.float32)[:, None] * freq[None, :]
    ang = np.concatenate([ang_r, ang_r, ang_c, ang_c], axis=1).astype(np.float32)
    cos = np.concatenate([np.ones((n_ctx, MLA_ROPE), np.float32)] + [np.cos(ang)] * dec_batch, axis=0)
    sin = np.concatenate([np.zeros((n_ctx, MLA_ROPE), np.float32)] + [np.sin(ang)] * dec_batch, axis=0)
    reps = LANES // MLA_ROPE
    return (jnp.asarray(np.tile(cos, (1, reps)), F32), jnp.asarray(np.tile(sin, (1, reps)), F32))


def mla_proj(x, g, shift, scale, w_in, q_g, kv_g, w_qb, cos, sin, n_ctx, dec_seq, tm=512):
    t, d = x.shape
    hh, nope, rope = MLA_HEADS, MLA_NOPE, MLA_ROPE
    q_lora = q_g.shape[1]
    kv_lora = kv_g.shape[1]
    w_in_ext = jnp.concatenate([w_in, _rot_cols(w_in[:, q_lora + kv_lora:])], axis=1)
    w3 = w_qb.reshape(q_lora, hh, nope + rope)
    w_qn = w3[:, :, :nope].reshape(q_lora, hh * nope)
    w_qp = w3[:, :, nope:].reshape(q_lora, hh * rope)
    w_qb_ext = jnp.concatenate([w_qn, w_qp, _rot_cols(w_qp)], axis=1)
    cidx = _cvec_index(n_ctx, dec_seq, tm)
    mod_spec = pl.BlockSpec((None, 1, d), lambda i: (cidx(i), 0, 0))
    full = lambda a: pl.BlockSpec(a.shape, lambda i: (0,) * a.ndim)
    kern = functools.partial(_mla_proj_kernel, q_lora=q_lora, kv_lora=kv_lora, rope=rope, n_heads=hh)
    return pl.pallas_call(
        kern,
        grid=(t // tm,),
        in_specs=[pl.BlockSpec((tm, d), lambda i: (i, 0)), full(g), mod_spec, mod_spec,
                  full(w_in_ext), full(q_g), full(kv_g), full(w_qb_ext),
                  pl.BlockSpec((tm, LANES), lambda i: (i, 0)),
                  pl.BlockSpec((tm, LANES), lambda i: (i, 0))],
        out_specs=[pl.BlockSpec((tm, hh * nope), lambda i: (i, 0)),
                   pl.BlockSpec((tm, hh * rope), lambda i: (i, 0)),
                   pl.BlockSpec((tm, kv_lora), lambda i: (i, 0)),
                   pl.BlockSpec((tm, rope), lambda i: (i, 0))],
        out_shape=[jax.ShapeDtypeStruct((t, hh * nope), BF16),
                   jax.ShapeDtypeStruct((t, hh * rope), BF16),
                   jax.ShapeDtypeStruct((t, kv_lora), F32),
                   jax.ShapeDtypeStruct((t, rope), F32)],
        compiler_params=_params("arbitrary"),
        name="mla_proj",
    )(x, g, shift, scale, w_in_ext, q_g, kv_g, w_qb_ext, cos, sin)


def _mla_kv_kernel(ckv_ref, w_ref, kn_ref, v_ref):
    kv = _bdot(ckv_ref[...], w_ref[...])
    n = kn_ref.shape[1]
    kn_ref[...] = kv[:, :n].astype(BF16)
    v_ref[...] = kv[:, n:].astype(BF16)


def mla_kv(ckv_all, w_kvb, tm=512):
    r, kv_lora = ckv_all.shape
    hh, nope, vd = MLA_HEADS, MLA_NOPE, MLA_V
    w3 = w_kvb.reshape(kv_lora, hh, nope + vd)
    w_perm = jnp.concatenate([w3[:, :, :nope].reshape(kv_lora, hh * nope),
                              w3[:, :, nope:].reshape(kv_lora, hh * vd)], axis=1)
    return pl.pallas_call(
        _mla_kv_kernel,
        grid=(r // tm,),
        in_specs=[pl.BlockSpec((tm, kv_lora), lambda i: (i, 0)),
                  pl.BlockSpec(w_perm.shape, lambda i: (0, 0))],
        out_specs=[pl.BlockSpec((tm, hh * nope), lambda i: (i, 0)),
                   pl.BlockSpec((tm, hh * vd), lambda i: (i, 0))],
        out_shape=[jax.ShapeDtypeStruct((r, hh * nope), BF16),
                   jax.ShapeDtypeStruct((r, hh * vd), BF16)],
        compiler_params=_params("arbitrary"),
        name="mla_kv",
    )(ckv_all, w_perm)


def _attn_kernel(qn_ref, qp_ref, kn_ref, kp_ref, v_ref, o_ref, *, scale):
    kp = kp_ref[...]
    for h in range(MLA_HEADS):
        s = _bdot_nt(qn_ref[:, h * MLA_NOPE:(h + 1) * MLA_NOPE], kn_ref[:, h * MLA_NOPE:(h + 1) * MLA_NOPE])
        s = (s + _bdot_nt(qp_ref[:, h * MLA_ROPE:(h + 1) * MLA_ROPE], kp)) * scale
        e = jnp.exp(s - jnp.max(s, axis=-1, keepdims=True))
        p = e / jnp.sum(e, axis=-1, keepdims=True)
        o_ref[:, h * MLA_V:(h + 1) * MLA_V] = _bdot(p, v_ref[:, h * MLA_V:(h + 1) * MLA_V]).astype(BF16)


def mla_attention(qn, qp, kn, kp, v, q_row_off, k_row_off, n_seq, q_len, k_len):
    tq = ATTN_Q_BLOCK
    qb = q_len // tq
    q0 = q_row_off // tq
    k0 = k_row_off // k_len
    dq, dp, dv = qn.shape[1], qp.shape[1], v.shape[1]
    kern = functools.partial(_attn_kernel, scale=(MLA_NOPE + MLA_ROPE) ** -0.5)
    return pl.pallas_call(
        kern,
        grid=(n_seq, qb),
        in_specs=[pl.BlockSpec((tq, dq), lambda s, j: (q0 + s * qb + j, 0)),
                  pl.BlockSpec((tq, dp), lambda s, j: (q0 + s * qb + j, 0)),
                  pl.BlockSpec((k_len, dq), lambda s, j: (k0 + s, 0)),
                  pl.BlockSpec((k_len, MLA_ROPE), lambda s, j: (k0 + s, 0)),
                  pl.BlockSpec((k_len, dv), lambda s, j: (k0 + s, 0))],
        out_specs=pl.BlockSpec((tq, dv), lambda s, j: (s * qb + j, 0)),
        out_shape=jax.ShapeDtypeStruct((n_seq * q_len, dv), BF16),
        compiler_params=_params("arbitrary", "arbitrary"),
        name="mla_attention",
    )(qn, qp, kn, kp, v)


def kernel(x_prompt, x_sample, state_mlstm_C, state_mlstm_n, state_mlstm_m, cache_mla_ckv,
           cache_mla_kpe, c, c_ctx, w_ada, b_ada, norm_mix, norm_ffn, norm_final, w_pool,
           pool_scale, w_mlstm_in, b_mlstm_gate, mlstm_head_g, w_mlstm_out, w_mla_in, mla_q_g,
           mla_kv_g, w_mla_qb, w_mla_kvb, w_mla_out, w_router, b_router, w_exp_gate, w_exp_up,
           w_exp_down):
    batch, seq, d = x_prompt.shape
    dec_batch, dec_seq, _ = x_sample.shape
    depth = w_ada.shape[0]
    n_ctx = batch * seq
    n_lat = dec_batch * dec_seq
    hh = MLSTM_HEADS
    past = cache_mla_ckv.shape[2]

    x = jnp.concatenate([x_prompt.reshape(n_ctx, d), x_sample.reshape(n_lat, d)], axis=0)

    n_cv = 1 + dec_batch
    cvecs = jnp.concatenate([c_ctx[None, :], c, jnp.zeros((SUBLANES - n_cv % SUBLANES, d), F32)], axis=0)
    mod = ada_mod_all(cvecs, w_ada, b_ada).reshape(depth, cvecs.shape[0], 6, 1, d)

    w_router_pad = jnp.pad(w_router, ((0, 0), (0, LANES - N_EXPERTS)))
    b_router_col = b_router.reshape(N_EXPERTS, 1)
    g_final = norm_final.reshape(1, d)
    row = lambda a: a.reshape(1, -1)

    outs = {}
    for i in range(depth):
        kind, j = i % 3, i // 3
        m = [mod[i, :n_cv, k] for k in range(6)]
        g_mix = row(norm_mix[i])
        if kind == 0:
            x = pool_layer(x, g_mix, m[0], m[1], m[2], w_pool[j], row(pool_scale[j]),
                           n_ctx, seq, dec_seq)
        elif kind == 1:
            qkv, o_gate, gcol, grow = mlstm_proj(x, g_mix, m[0], m[1], w_mlstm_in[j],
                                                 b_mlstm_gate[j], n_ctx, dec_seq)
            t = n_ctx + n_lat
            l = MLSTM_CHUNK
            gcol_h = gcol[:, :4 * hh].reshape(t, 4, hh).transpose(2, 0, 1)
            grow_h = grow.reshape(4, hh, t // l, l).transpose(1, 2, 0, 3)
            grow_h = jnp.pad(grow_h, ((0, 0), (0, 0), (0, SUBLANES - 4), (0, 0)))
            dk = d // hh // 2
            zc = jnp.zeros((batch, 2, hh, dk, d // hh), F32)
            zn = jnp.zeros((batch, 2, hh, 1, dk), F32)
            zm = jnp.zeros((batch, 2, hh, 1, 1), F32)
            hs_c, c_new, n_new, m_new = mlstm_scan(qkv, gcol_h, grow_h, zc, zn, zm, 0, batch, seq)
            hs_l, _, _, _ = mlstm_scan(qkv, gcol_h, grow_h, state_mlstm_C[:, j],
                                       state_mlstm_n[:, j][:, :, :, None, :],
                                       state_mlstm_m[:, j][:, :, :, None, None],
                                       n_ctx, dec_batch, dec_seq)
            outs["C"] = c_new[:, None]
            outs["n"] = n_new[:, None, :, :, 0, :]
            outs["m"] = m_new[:, None, :, :, 0, 0]
            hs = jnp.concatenate([hs_c, hs_l], axis=0)
            x = mlstm_out(x, hs, o_gate, row(mlstm_head_g[j]), w_mlstm_out[j], m[2], n_ctx, dec_seq, 0)
        else:
            cos, sin = _rope_tables(n_ctx, dec_batch, dec_seq)
            qn, qp, ckv, kpe = mla_proj(x, g_mix, m[0], m[1], w_mla_in[j], row(mla_q_g[j]),
                                        row(mla_kv_g[j]), w_mla_qb[j], cos, sin, n_ctx, dec_seq)
            lat_parts_c, lat_parts_p = [], []
            for b in range(dec_batch):
                lo = n_ctx + b * dec_seq
                lat_parts_c += [cache_mla_ckv[b, j], ckv[lo:lo + dec_seq]]
                lat_parts_p += [cache_mla_kpe[b, j], kpe[lo:lo + dec_seq]]
            ckv_all = jnp.concatenate(lat_parts_c + [ckv[:n_ctx]], axis=0)
            kp_all = jnp.concatenate(lat_parts_p + [kpe[:n_ctx]], axis=0).astype(BF16)
            kn, v = mla_kv(ckv_all, w_mla_kvb[j])
            k_lat = past + dec_seq
            o_c = mla_attention(qn, qp, kn, kp_all, v, 0, dec_batch * k_lat, batch, seq, seq)
            o_l = mla_attention(qn, qp, kn, kp_all, v, n_ctx, 0, dec_batch, dec_seq, k_lat)
            outs["ckv"] = ckv[:n_ctx].reshape(batch, 1, seq, -1)
            outs["kpe"] = kpe[:n_ctx].reshape(batch, 1, seq, -1)
            x = resid_proj(x, jnp.concatenate([o_c, o_l], axis=0), w_mla_out[j], m[2], n_ctx, dec_seq)
        x = moe_layer(x, row(norm_ffn[i]), m[3], m[4], m[5], w_router_pad, b_router_col,
                      w_exp_gate[i], w_exp_up[i], w_exp_down[i], g_final, n_ctx, dec_seq,
                      final_norm=(i == depth - 1))

    y_prompt = x[:n_ctx].reshape(batch, seq, d)
    y_sample = x[n_ctx:].reshape(dec_batch, dec_seq, d)
    return (y_prompt, y_sample, outs["C"], outs["n"], outs["m"], outs["ckv"], outs["kpe"])
```

```python
import functools

import numpy as np
import jax
import jax.numpy as jnp
from jax import lax
from jax.experimental import pallas as pl
from jax.experimental.pallas import tpu as pltpu

F32 = jnp.float32
BF16 = jnp.bfloat16

NORM_EPS = 1e-6
GRID_W = 64
POOL_WINDOWS = (2, 4, 8, 16)
MLSTM_HEADS = 4
MLSTM_CHUNK = 256
MLA_HEADS = 8
MLA_NOPE = 128
MLA_ROPE = 64
MLA_V = 128
ROPE_BASE = 10000.0
N_EXPERTS = 16
N_EXPERT_GROUPS = 4
EXPERTS_PER_GROUP = N_EXPERTS // N_EXPERT_GROUPS

LANES = 128
SUBLANES = 8
VMEM_LIMIT = 56 * 1024 * 1024
POOL_TILE = 256
POOL_HALO = 8
ATTN_Q_BLOCK = 256


def _params(*sem):
    return pltpu.CompilerParams(dimension_semantics=sem, vmem_limit_bytes=VMEM_LIMIT)


def _rms(x, g):
    return x * lax.rsqrt(jnp.mean(x * x, axis=-1, keepdims=True) + NORM_EPS) * g


def _modulated(x, g, shift, scale):
    return _rms(x, g) * (1.0 + scale) + shift


def _silu(x):
    return x * jax.nn.sigmoid(x)


def _bdot(a, b):
    return jnp.dot(a.astype(BF16), b.astype(BF16), preferred_element_type=F32)


def _bdot_nt(a, b):
    return lax.dot_general(a.astype(BF16), b.astype(BF16), (((1,), (1,)), ((), ())),
                           preferred_element_type=F32)


def _bdot_tn(a, b):
    return lax.dot_general(a.astype(BF16), b.astype(BF16), (((0,), (0,)), ((), ())),
                           preferred_element_type=F32)


def _cvec_index(n_ctx, dec_seq, tm):
    def idx(i):
        r = i * tm
        return jnp.where(r < n_ctx, 0, (r - n_ctx) // dec_seq + 1)
    return idx


def _ada_kernel(c_ref, w_ref, b_ref, o_ref):
    o_ref[...] = _bdot(_silu(c_ref[...]), w_ref[...]) + b_ref[...]


def ada_mod_all(cvecs, w_ada, b_ada, tn=1536):
    depth, d, n6 = w_ada.shape
    rows = cvecs.shape[0]
    return pl.pallas_call(
        _ada_kernel,
        grid=(depth, n6 // tn),
        in_specs=[pl.BlockSpec((rows, d), lambda l, n: (0, 0)),
                  pl.BlockSpec((None, d, tn), lambda l, n: (l, 0, n)),
                  pl.BlockSpec((None, 1, tn), lambda l, n: (l, 0, n))],
        out_specs=pl.BlockSpec((None, rows, tn), lambda l, n: (l, 0, n)),
        out_shape=jax.ShapeDtypeStruct((depth, rows, n6), F32),
        compiler_params=_params("arbitrary", "arbitrary"),
        name="ada_mod",
    )(cvecs, w_ada, b_ada.reshape(depth, 1, n6))


def _pool_kernel(xc_ref, xp_ref, xn_ref, g_ref, sh_ref, sc_ref, gt_ref, wp_ref, ps_ref, o_ref,
                 buf_ref, *, n_ctx_tiles, ctx_seq_tiles, lat_seq_tiles):
    i = pl.program_id(0)
    is_ctx = i < n_ctx_tiles
    j = jnp.where(is_ctx, i % ctx_seq_tiles, (i - n_ctx_tiles) % lat_seq_tiles)
    nt = jnp.where(is_ctx, ctx_seq_tiles, lat_seq_tiles)
    g, sh, sc = g_ref[...], sh_ref[...], sc_ref[...]
    tp, hl = POOL_TILE, POOL_HALO
    gw = xc_ref.shape[1] // len(POOL_WINDOWS)

    xc = xc_ref[...]
    buf_ref[pl.ds(0, hl), :] = jnp.where(j == 0, 0.0, _modulated(xp_ref[...], g, sh, sc))
    buf_ref[pl.ds(hl, tp), :] = _modulated(xc, g, sh, sc)
    buf_ref[pl.ds(hl + tp, hl), :] = jnp.where(j == nt - 1, 0.0, _modulated(xn_ref[...], g, sh, sc))

    pos = j * tp + lax.broadcasted_iota(jnp.int32, (tp, 1), 0)
    seq_len = nt * tp
    for gi, w in enumerate(POOL_WINDOWS):
        cols = pl.ds(gi * gw, gw)
        acc = buf_ref[pl.ds(hl - w // 2, tp), cols]
        for d in range(-w // 2 + 1, w // 2):
            acc = acc + buf_ref[pl.ds(hl + d, tp), cols]
        cnt = jnp.minimum(pos + w // 2, seq_len) - jnp.maximum(pos - w // 2, 0)
        pooled = acc / cnt.astype(F32) - buf_ref[pl.ds(hl, tp), cols]
        y = _bdot(pooled, wp_ref[gi]) * ps_ref[:, cols]
        o_ref[:, cols] = xc[:, gi * gw:(gi + 1) * gw] + gt_ref[:, cols] * y


def pool_layer(x, g, shift, scale, gate, w_pool, pool_scale, n_ctx, seq, dec_seq):
    t, d = x.shape
    tp, hl = POOL_TILE, POOL_HALO
    cidx = _cvec_index(n_ctx, dec_seq, tp)
    hb = tp // hl
    last_hblk = t // hl - 1
    mod_spec = pl.BlockSpec((None, 1, d), lambda i: (cidx(i), 0, 0))
    row_spec = pl.BlockSpec((1, d), lambda i: (0, 0))
    kern = functools.partial(_pool_kernel, n_ctx_tiles=n_ctx // tp, ctx_seq_tiles=seq // tp,
                             lat_seq_tiles=dec_seq // tp)
    return pl.pallas_call(
        kern,
        grid=(t // tp,),
        in_specs=[pl.BlockSpec((tp, d), lambda i: (i, 0)),
                  pl.BlockSpec((hl, d), lambda i: (jnp.maximum(i * hb - 1, 0), 0)),
                  pl.BlockSpec((hl, d), lambda i: (jnp.minimum((i + 1) * hb, last_hblk), 0)),
                  row_spec, mod_spec, mod_spec, mod_spec,
                  pl.BlockSpec(w_pool.shape, lambda i: (0, 0, 0)),
                  row_spec],
        out_specs=pl.BlockSpec((tp, d), lambda i: (i, 0)),
        out_shape=jax.ShapeDtypeStruct((t, d), F32),
        scratch_shapes=[pltpu.VMEM((tp + 2 * hl, d), F32)],
        compiler_params=_params("arbitrary"),
        name="pool_mixer",
    )(x, x, x, g, shift, scale, gate, w_pool, pool_scale)


def _route(sel, scores):
    e, tm = sel.shape
    row = lax.broadcasted_iota(jnp.int32, (e, tm), 0)
    best = jnp.zeros((1, tm), jnp.int32)
    best_sc = None
    for gidx in range(N_EXPERT_GROUPS):
        r = [sel[gidx * EXPERTS_PER_GROUP + k:gidx * EXPERTS_PER_GROUP + k + 1, :]
             for k in range(EXPERTS_PER_GROUP)]
        top2 = None
        for a in range(EXPERTS_PER_GROUP):
            for b in range(a + 1, EXPERTS_PER_GROUP):
                s = r[a] + r[b]
                top2 = s if top2 is None else jnp.maximum(top2, s)
        if best_sc is None:
            best_sc = top2
        else:
            better = top2 > best_sc
            best = jnp.where(better, gidx, best)
            best_sc = jnp.where(better, top2, best_sc)
    neg = -jnp.inf
    masked = jnp.where(row // EXPERTS_PER_GROUP == best, sel, neg)
    m1 = jnp.max(masked, axis=0, keepdims=True)
    i1 = jnp.min(jnp.where(masked == m1, row, e), axis=0, keepdims=True)
    masked2 = jnp.where(row == i1, neg, masked)
    m2 = jnp.max(masked2, axis=0, keepdims=True)
    i2 = jnp.min(jnp.where(masked2 == m2, row, e), axis=0, keepdims=True)
    hot1 = row == i1
    hot2 = row == i2
    w1 = jnp.sum(jnp.where(hot1, scores, 0.0), axis=0, keepdims=True)
    w2 = jnp.sum(jnp.where(hot2, scores, 0.0), axis=0, keepdims=True)
    tot = w1 + w2
    return jnp.where(hot1, w1 / tot, 0.0) + jnp.where(hot2, w2 / tot, 0.0)


def _moe_kernel(x_ref, g_ref, sh_ref, sc_ref, gt_ref, wr_ref, br_ref, wg_ref, wu_ref, wd_ref,
                gf_ref, o_ref, h_scr, comb_scr, acc_scr, *, final_norm):
    e = pl.program_id(1)

    @pl.when(e == 0)
    def _():
        h = _modulated(x_ref[...], g_ref[...], sh_ref[...], sc_ref[...])
        h_scr[...] = h.astype(BF16)
        logits = jnp.dot(h, wr_ref[...], precision=lax.Precision.HIGHEST,
                         preferred_element_type=F32)
        logits_t = logits.T[:N_EXPERTS, :]
        scores = jax.nn.sigmoid(logits_t)
        comb_t = _route(scores + br_ref[...], scores)
        pad = jnp.zeros((LANES - N_EXPERTS, comb_t.shape[1]), F32)
        comb_scr[...] = jnp.concatenate([comb_t, pad], axis=0).T
        acc_scr[...] = jnp.zeros_like(acc_scr)

    hb = h_scr[...]
    gate = jnp.dot(hb, wg_ref[...].astype(BF16), preferred_element_type=F32)
    up = jnp.dot(hb, wu_ref[...].astype(BF16), preferred_element_type=F32)
    comb = comb_scr[...]
    lane = lax.broadcasted_iota(jnp.int32, comb.shape, 1)
    c_e = jnp.sum(jnp.where(lane == e, comb, 0.0), axis=1, keepdims=True)
    hid = _silu(gate) * up * c_e
    acc_scr[...] += _bdot(hid, wd_ref[...])

    @pl.when(e == pl.num_programs(1) - 1)
    def _():
        out = x_ref[...] + gt_ref[...] * acc_scr[...]
        if final_norm:
            out = _rms(out, gf_ref[...])
        o_ref[...] = out


def moe_layer(x, g, shift, scale, gate, w_router_pad, b_router_col, wg, wu, wd, layer, g_final,
              n_ctx, dec_seq, final_norm, tm=1024):
    t, d = x.shape
    _, n_e, _, f = wg.shape
    cidx = _cvec_index(n_ctx, dec_seq, tm)
    mod_spec = pl.BlockSpec((None, 1, d), lambda i, e: (cidx(i), 0, 0))
    row_spec = pl.BlockSpec((1, d), lambda i, e: (0, 0))
    return pl.pallas_call(
        functools.partial(_moe_kernel, final_norm=final_norm),
        grid=(t // tm, n_e),
        in_specs=[pl.BlockSpec((tm, d), lambda i, e: (i, 0)),
                  row_spec, mod_spec, mod_spec, mod_spec,
                  pl.BlockSpec(w_router_pad.shape, lambda i, e: (0, 0)),
                  pl.BlockSpec(b_router_col.shape, lambda i, e: (0, 0)),
                  pl.BlockSpec((None, None, d, f), lambda i, e: (layer, e, 0, 0)),
                  pl.BlockSpec((None, None, d, f), lambda i, e: (layer, e, 0, 0)),
                  pl.BlockSpec((None, None, f, d), lambda i, e: (layer, e, 0, 0)),
                  row_spec],
        out_specs=pl.BlockSpec((tm, d), lambda i, e: (i, 0)),
        out_shape=jax.ShapeDtypeStruct((t, d), F32),
        scratch_shapes=[pltpu.VMEM((tm, d), BF16), pltpu.VMEM((tm, LANES), F32),
                        pltpu.VMEM((tm, d), F32)],
        compiler_params=_params("arbitrary", "arbitrary"),
        name="moe",
    )(x, g, shift, scale, gate, w_router_pad, b_router_col, wg, wu, wd, g_final)


def _per_stream(n_ctx_tiles, ctx_ref, lat_ref, fn):
    i = pl.program_id(0)

    @pl.when(i < n_ctx_tiles)
    def _():
        fn(ctx_ref[...])

    @pl.when(i >= n_ctx_tiles)
    def _():
        fn(lat_ref[...])


def _stream_specs(tm, k, n_ctx_tiles):
    return [pl.BlockSpec((tm, k), lambda i: (jnp.minimum(i, n_ctx_tiles - 1), 0)),
            pl.BlockSpec((tm, k), lambda i: (jnp.maximum(i - n_ctx_tiles, 0), 0))]


def _resid_proj_kernel(x_ref, ac_ref, al_ref, w_ref, gt_ref, o_ref, *, n_ctx_tiles):
    def run(a):
        o_ref[...] = x_ref[...] + gt_ref[...] * _bdot(a, w_ref[...])
    _per_stream(n_ctx_tiles, ac_ref, al_ref, run)


def _mlstm_out_kernel(x_ref, hc_ref, hl_ref, og_ref, hg_ref, w_ref, gt_ref, o_ref, *, n_ctx_tiles):
    def run(hs):
        a = jax.nn.sigmoid(og_ref[...]) * (hs * hg_ref[...])
        o_ref[...] = x_ref[...] + gt_ref[...] * _bdot(a, w_ref[...])
    _per_stream(n_ctx_tiles, hc_ref, hl_ref, run)


def resid_proj(x, a_ctx, a_lat, w, gate, n_ctx, dec_seq, tm=512):
    t, d = x.shape
    k = a_ctx.shape[1]
    cidx = _cvec_index(n_ctx, dec_seq, tm)
    nct = n_ctx // tm
    return pl.pallas_call(
        functools.partial(_resid_proj_kernel, n_ctx_tiles=nct),
        grid=(t // tm,),
        in_specs=[pl.BlockSpec((tm, d), lambda i: (i, 0))] + _stream_specs(tm, k, nct) + [
            pl.BlockSpec((k, d), lambda i: (0, 0)),
            pl.BlockSpec((None, 1, d), lambda i: (cidx(i), 0, 0))],
        out_specs=pl.BlockSpec((tm, d), lambda i: (i, 0)),
        out_shape=jax.ShapeDtypeStruct((t, d), F32),
        compiler_params=_params("arbitrary"),
        name="resid_proj",
    )(x, a_ctx, a_lat, w, gate)


def mlstm_out(x, hs_ctx, hs_lat, o_gate, head_g, w, gate, n_ctx, dec_seq, tm=512):
    t, d = x.shape
    k = hs_ctx.shape[1]
    cidx = _cvec_index(n_ctx, dec_seq, tm)
    nct = n_ctx // tm
    return pl.pallas_call(
        functools.partial(_mlstm_out_kernel, n_ctx_tiles=nct),
        grid=(t // tm,),
        in_specs=[pl.BlockSpec((tm, d), lambda i: (i, 0))] + _stream_specs(tm, k, nct) + [
            pl.BlockSpec((tm, k), lambda i: (i, 0)),
            pl.BlockSpec((1, k), lambda i: (0, 0)),
            pl.BlockSpec((k, d), lambda i: (0, 0)),
            pl.BlockSpec((None, 1, d), lambda i: (cidx(i), 0, 0))],
        out_specs=pl.BlockSpec((tm, d), lambda i: (i, 0)),
        out_shape=jax.ShapeDtypeStruct((t, d), F32),
        compiler_params=_params("arbitrary"),
        name="mlstm_out",
    )(x, hs_ctx, hs_lat, o_gate, head_g, w, gate)


def _mlstm_proj_kernel(x_ref, g_ref, sh_ref, sc_ref, wqk_ref, wv_ref, wo_ref, wgt_ref, bg_ref,
                       wgtt_ref, bgt_ref, qs_ref, qkv_ref, o_ref, gc_ref, gr_ref):
    h = _modulated(x_ref[...], g_ref[...], sh_ref[...], sc_ref[...]).astype(BF16)
    nqk = wqk_ref.shape[1]
    qkv_ref[:, :nqk] = (jnp.dot(h, wqk_ref[...].astype(BF16), preferred_element_type=F32)
                        * qs_ref[...]).astype(BF16)
    qkv_ref[:, nqk:] = jnp.dot(h, wv_ref[...].astype(BF16),
                               preferred_element_type=F32).astype(BF16)
    o_ref[...] = jnp.dot(h, wo_ref[...].astype(BF16), preferred_element_type=F32)
    gc_ref[...] = jnp.dot(h, wgt_ref[...].astype(BF16), preferred_element_type=F32) + bg_ref[...]
    gr_ref[...] = _bdot_nt(wgtt_ref[...], h) + bgt_ref[...]


def mlstm_proj(x, g, shift, scale, w_in_all, layer, b_gate, n_ctx, dec_seq, tm=512):
    t, d = x.shape
    hh = MLSTM_HEADS
    hv = d
    hk = hv // 2
    ng = 4 * hh
    w_g = w_in_all[layer, :, 2 * hk + 2 * hv:]
    w_g_pad = jnp.pad(w_g, ((0, 0), (0, LANES - ng)))
    b_pad = jnp.pad(b_gate.reshape(1, ng), ((0, 0), (0, LANES - ng)))
    w_g_t = w_g.T
    b_t = b_gate.reshape(ng, 1)
    dk = hk // hh
    q_scale = jnp.concatenate([jnp.full((1, hk), dk ** -0.5, F32), jnp.ones((1, hk), F32)], axis=1)
    cidx = _cvec_index(n_ctx, dec_seq, tm)
    mod_spec = pl.BlockSpec((None, 1, d), lambda i: (cidx(i), 0, 0))
    full = lambda a: pl.BlockSpec(a.shape, lambda i: (0,) * a.ndim)
    assert 2 * hk == hv
    w_col = lambda n: pl.BlockSpec((None, d, hv), lambda i: (layer, 0, n))
    return pl.pallas_call(
        _mlstm_proj_kernel,
        grid=(t // tm,),
        in_specs=[pl.BlockSpec((tm, d), lambda i: (i, 0)), full(g), mod_spec, mod_spec,
                  w_col(0), w_col(1), w_col(2), full(w_g_pad), full(b_pad), full(w_g_t),
                  full(b_t), full(q_scale)],
        out_specs=[pl.BlockSpec((tm, 2 * hk + hv), lambda i: (i, 0)),
                   pl.BlockSpec((tm, hv), lambda i: (i, 0)),
                   pl.BlockSpec((tm, LANES), lambda i: (i, 0)),
                   pl.BlockSpec((ng, tm), lambda i: (0, i))],
        out_shape=[jax.ShapeDtypeStruct((t, 2 * hk + hv), BF16),
                   jax.ShapeDtypeStruct((t, hv), F32),
                   jax.ShapeDtypeStruct((t, LANES), F32),
                   jax.ShapeDtypeStruct((ng, t), F32)],
        compiler_params=_params("arbitrary"),
        name="mlstm_proj",
    )(x, g, shift, scale, w_in_all, w_in_all, w_in_all, w_g_pad, b_pad, w_g_t, b_t, q_scale)


def _log_sigmoid(x):
    return jnp.minimum(x, 0.0) - jnp.log(1.0 + jnp.exp(-jnp.abs(x)))


def _mlstm_chunk(q, k, v, i_col, f_col, i_row, f_row, c_st, n_st, m_st, rev):
    l = q.shape[0]
    lf_col = _log_sigmoid(f_col)
    lf_row = _log_sigmoid(f_row)
    tt = lax.broadcasted_iota(jnp.int32, (l, l), 0)
    ss = lax.broadcasted_iota(jnp.int32, (l, l), 1)
    causal = (ss >= tt) if rev else (ss <= tt)
    b_col = jnp.sum(jnp.where(causal, lf_row, 0.0), axis=1, keepdims=True)
    feeds = (tt >= ss) if rev else (tt <= ss)
    b_row = jnp.sum(jnp.where(feeds, lf_col, 0.0), axis=0, keepdims=True)
    log_d = jnp.where(causal, b_col - b_row + i_row, -jnp.inf)
    log_state = b_col + m_st
    m_t = jnp.maximum(log_state, jnp.max(log_d, axis=1, keepdims=True))
    dw = jnp.exp(log_d - m_t)
    sw = jnp.exp(log_state - m_t)
    a = _bdot_nt(q, k) * dw
    num = sw * _bdot(q, c_st) + _bdot(a, v)
    qn = jnp.sum(q.astype(F32) * n_st, axis=1, keepdims=True)
    den = sw * qn + jnp.sum(a, axis=1, keepdims=True)
    h = num / jnp.maximum(jnp.abs(den), jnp.exp(-m_t))
    b_last = b_col[0:1, :] if rev else b_col[l - 1:l, :]
    log_k = b_last - b_col + i_col
    m_new = jnp.maximum(b_last + m_st, jnp.max(log_k, axis=0, keepdims=True))
    kw = jnp.exp(log_k - m_new)
    decay = jnp.exp(b_last + m_st - m_new)
    kwk = kw * k.astype(F32)
    c_new = decay * c_st + _bdot_tn(kwk, v)
    n_new = decay * n_st + jnp.sum(kwk, axis=0, keepdims=True)
    return h, c_new, n_new, m_new


def _mlstm_scan_kernel(*refs, n_chunks, zero_init):
    if zero_init:
        q_ref, k_ref, v_ref, gc_ref, gr_ref, hs_ref, c_ref, n_ref, m_ref, hb_scr = refs
        c_ref[...] = jnp.zeros_like(c_ref)
        n_ref[...] = jnp.zeros_like(n_ref)
        m_ref[...] = jnp.zeros_like(m_ref)
    else:
        (q_ref, k_ref, v_ref, gc_ref, gr_ref, c0_ref, n0_ref, m0_ref,
         hs_ref, c_ref, n_ref, m_ref, hb_scr) = refs
        c_ref[...] = c0_ref[...]
        n_ref[...] = n0_ref[...]
        m_ref[...] = m0_ref[...]
    l = MLSTM_CHUNK
    hh = MLSTM_HEADS
    dk = q_ref.shape[1] // hh
    dv = v_ref.shape[1] // hh

    def body(c, carry):
        cr = n_chunks - 1 - c
        rows = (pl.ds(pl.multiple_of(c * l, l), l), pl.ds(pl.multiple_of(cr * l, l), l))
        grs = (gr_ref[c], gr_ref[cr])
        outs, states = ([], []), []
        for d in range(2):
            q, k, v, gc, gr = q_ref[rows[d], :], k_ref[rows[d], :], v_ref[rows[d], :], gc_ref[rows[d], :], grs[d]
            for h in range(hh):
                ci, cf = 2 * d * hh + h, (2 * d + 1) * hh + h
                o, *st = _mlstm_chunk(q[:, h * dk:(h + 1) * dk], k[:, h * dk:(h + 1) * dk],
                                      v[:, h * dv:(h + 1) * dv], gc[:, ci:ci + 1], gc[:, cf:cf + 1],
                                      gr[ci:ci + 1, :], gr[cf:cf + 1, :],
                                      c_ref[d, h], n_ref[d, h], m_ref[d, h], d == 1)
                outs[d].append(o)
                states.append((d, h, st))
        hs_ref[rows[0], :] = jnp.concatenate(outs[0], axis=1)
        hb_scr[rows[1], :] = jnp.concatenate(outs[1], axis=1)
        for d, h, (c_new, n_new, m_new) in states:
            c_ref[d, h], n_ref[d, h], m_ref[d, h] = c_new, n_new, m_new
        return carry

    lax.fori_loop(0, n_chunks, body, 0)
    for h in range(hh):
        cols = pl.ds(h * dv, dv)
        hs = hs_ref[:, cols] + hb_scr[:, cols]
        hs_ref[:, cols] = hs * lax.rsqrt(jnp.mean(hs * hs, axis=-1, keepdims=True) + NORM_EPS)


def mlstm_scan(qkv, gcol, grow, init, row_off, n_seq, seq_len):
    hh = MLSTM_HEADS
    hv = qkv.shape[1] // 2
    dv = hv // hh
    dk = dv // 2
    l = MLSTM_CHUNK
    nc = seq_len // l
    ob = row_off // seq_len
    kern = functools.partial(_mlstm_scan_kernel, n_chunks=nc, zero_init=init is None)
    st = lambda *tail: pl.BlockSpec((None, 2, hh) + tail, lambda s: (s, 0, 0) + (0,) * len(tail))
    states = [st(dk, dv), st(1, dk), st(1, 1)]
    return pl.pallas_call(
        kern,
        grid=(n_seq,),
        in_specs=[pl.BlockSpec((seq_len, hh * dk), lambda s: (ob + s, 0)),
                  pl.BlockSpec((seq_len, hh * dk), lambda s: (ob + s, 1)),
                  pl.BlockSpec((seq_len, hv), lambda s: (ob + s, 1)),
                  pl.BlockSpec((seq_len, LANES), lambda s: (ob + s, 0)),
                  pl.BlockSpec((nc, 4 * hh, l), lambda s: (ob + s, 0, 0))]
                 + ([] if init is None else states),
        out_specs=[pl.BlockSpec((seq_len, hv), lambda s: (s, 0))] + states,
        out_shape=[jax.ShapeDtypeStruct((n_seq * seq_len, hv), F32),
                   jax.ShapeDtypeStruct((n_seq, 2, hh, dk, dv), F32),
                   jax.ShapeDtypeStruct((n_seq, 2, hh, 1, dk), F32),
                   jax.ShapeDtypeStruct((n_seq, 2, hh, 1, 1), F32)],
        scratch_shapes=[pltpu.VMEM((seq_len, hv), F32)],
        compiler_params=_params("arbitrary"),
        name="mlstm_scan",
    )(qkv, qkv, qkv, gcol, grow, *(() if init is None else init))


def _mla_proj_kernel(x_ref, g_ref, sh_ref, sc_ref, win_ref, qg_ref, kvg_ref, wqb_ref, cos_ref,
                     sin_ref, qn_ref, qp_ref, ckv_ref, kpe_ref, *, q_lora, kv_lora, rope, n_heads):
    h = _modulated(x_ref[...], g_ref[...], sh_ref[...], sc_ref[...])
    proj = _bdot(h, win_ref[...])
    q_lat = proj[:, :q_lora]
    ckv_ref[...] = _rms(proj[:, q_lora:q_lora + kv_lora], kvg_ref[...])
    cos, sin = cos_ref[...], sin_ref[...]
    kpe = proj[:, q_lora + kv_lora:q_lora + kv_lora + rope]
    kpe_rot = proj[:, q_lora + kv_lora + rope:q_lora + kv_lora + 2 * rope]
    kpe_ref[...] = kpe * cos[:, :rope] + kpe_rot * sin[:, :rope]
    q = _bdot(_rms(q_lat, qg_ref[...]), wqb_ref[...])
    n_nope = qn_ref.shape[1]
    n_rope = qp_ref.shape[1]
    qn_ref[...] = q[:, :n_nope].astype(BF16)
    reps = n_rope // cos.shape[1]
    cos_h = jnp.concatenate([cos] * reps, axis=1)
    sin_h = jnp.concatenate([sin] * reps, axis=1)
    qp_ref[...] = (q[:, n_nope:n_nope + n_rope] * cos_h
                   + q[:, n_nope + n_rope:] * sin_h).astype(BF16)


def _rot_cols(w):
    n = w.shape[1]
    j = np.arange(n)
    quarter = MLA_ROPE // 4
    first = (j % (2 * quarter)) < quarter
    src = np.where(first, j + quarter, j - quarter)
    sign = np.where(first, -1.0, 1.0).astype(np.float32)
    return w[:, src] * sign


def _rope_tables(n_ctx, dec_batch, dec_seq):
    quarter = MLA_ROPE // 4
    freq = np.power(np.float32(ROPE_BASE), -np.arange(quarter, dtype=np.float32) / np.float32(quarter))
    pos = np.arange(dec_seq)
    ang_r = (pos // GRID_W).astype(np.float32)[:, None] * freq[None, :]
    ang_c = (pos % GRID_W).astype(np.float32)[:, None] * freq[None, :]
    ang = np.concatenate([ang_r, ang_r, ang_c, ang_c], axis=1).astype(np.float32)
    cos = np.concatenate([np.ones((n_ctx, MLA_ROPE), np.float32)] + [np.cos(ang)] * dec_batch, axis=0)
    sin = np.concatenate([np.zeros((n_ctx, MLA_ROPE), np.float32)] + [np.sin(ang)] * dec_batch, axis=0)
    reps = LANES // MLA_ROPE
    return (jnp.asarray(np.tile(cos, (1, reps)), F32), jnp.asarray(np.tile(sin, (1, reps)), F32))


def mla_proj(x, g, shift, scale, w_in, q_g, kv_g, w_qb, cos, sin, n_ctx, dec_seq, tm=512):
    t, d = x.shape
    hh, nope, rope = MLA_HEADS, MLA_NOPE, MLA_ROPE
    q_lora = q_g.shape[1]
    kv_lora = kv_g.shape[1]
    w_in_ext = jnp.concatenate([w_in, _rot_cols(w_in[:, q_lora + kv_lora:])], axis=1)
    w3 = w_qb.reshape(q_lora, hh, nope + rope)
    w_qn = w3[:, :, :nope].reshape(q_lora, hh * nope)
    w_qp = w3[:, :, nope:].reshape(q_lora, hh * rope)
    w_qb_ext = jnp.concatenate([w_qn, w_qp, _rot_cols(w_qp)], axis=1)
    cidx = _cvec_index(n_ctx, dec_seq, tm)
    mod_spec = pl.BlockSpec((None, 1, d), lambda i: (cidx(i), 0, 0))
    full = lambda a: pl.BlockSpec(a.shape, lambda i: (0,) * a.ndim)
    kern = functools.partial(_mla_proj_kernel, q_lora=q_lora, kv_lora=kv_lora, rope=rope, n_heads=hh)
    return pl.pallas_call(
        kern,
        grid=(t // tm,),
        in_specs=[pl.BlockSpec((tm, d), lambda i: (i, 0)), full(g), mod_spec, mod_spec,
                  full(w_in_ext), full(q_g), full(kv_g), full(w_qb_ext),
                  pl.BlockSpec((tm, LANES), lambda i: (i, 0)),
                  pl.BlockSpec((tm, LANES), lambda i: (i, 0))],
        out_specs=[pl.BlockSpec((tm, hh * nope), lambda i: (i, 0)),
                   pl.BlockSpec((tm, hh * rope), lambda i: (i, 0)),
                   pl.BlockSpec((tm, kv_lora), lambda i: (i, 0)),
                   pl.BlockSpec((tm, rope), lambda i: (i, 0))],
        out_shape=[jax.ShapeDtypeStruct((t, hh * nope), BF16),
                   jax.ShapeDtypeStruct((t, hh * rope), BF16),
                   jax.ShapeDtypeStruct((t, kv_lora), F32),
                   jax.ShapeDtypeStruct((t, rope), F32)],
        compiler_params=_params("arbitrary"),
        name="mla_proj",
    )(x, g, shift, scale, w_in_ext, q_g, kv_g, w_qb_ext, cos, sin)


def _mla_kv_kernel(ckv_ref, w_ref, kn_ref, v_ref):
    kv = _bdot(ckv_ref[...], w_ref[...])
    n = kn_ref.shape[1]
    kn_ref[...] = kv[:, :n].astype(BF16)
    v_ref[...] = kv[:, n:].astype(BF16)


def mla_kv(ckv_all, w_kvb, tm=512):
    r, kv_lora = ckv_all.shape
    hh, nope, vd = MLA_HEADS, MLA_NOPE, MLA_V
    w3 = w_kvb.reshape(kv_lora, hh, nope + vd)
    w_perm = jnp.concatenate([w3[:, :, :nope].reshape(kv_lora, hh * nope),
                              w3[:, :, nope:].reshape(kv_lora, hh * vd)], axis=1)
    return pl.pallas_call(
        _mla_kv_kernel,
        grid=(r // tm,),
        in_specs=[pl.BlockSpec((tm, kv_lora), lambda i: (i, 0)),
                  pl.BlockSpec(w_perm.shape, lambda i: (0, 0))],
        out_specs=[pl.BlockSpec((tm, hh * nope), lambda i: (i, 0)),
                   pl.BlockSpec((tm, hh * vd), lambda i: (i, 0))],
        out_shape=[jax.ShapeDtypeStruct((r, hh * nope), BF16),
                   jax.ShapeDtypeStruct((r, hh * vd), BF16)],
        compiler_params=_params("arbitrary"),
        name="mla_kv",
    )(ckv_all, w_perm)


def _attn_kernel(qn_ref, qp_ref, kn_ref, kp_ref, v_ref, o_ref, *, scale):
    kp = kp_ref[...]
    for h in range(MLA_HEADS):
        s = _bdot_nt(qn_ref[:, h * MLA_NOPE:(h + 1) * MLA_NOPE], kn_ref[:, h * MLA_NOPE:(h + 1) * MLA_NOPE])
        s = (s + _bdot_nt(qp_ref[:, h * MLA_ROPE:(h + 1) * MLA_ROPE], kp)) * scale
        e = jnp.exp(s - jnp.max(s, axis=-1, keepdims=True))
        p = e / jnp.sum(e, axis=-1, keepdims=True)
        o_ref[:, h * MLA_V:(h + 1) * MLA_V] = _bdot(p, v_ref[:, h * MLA_V:(h + 1) * MLA_V]).astype(BF16)


def mla_attention(qn, qp, kn, kp, v, q_row_off, k_row_off, n_seq, q_len, k_len):
    tq = ATTN_Q_BLOCK
    qb = q_len // tq
    q0 = q_row_off // tq
    k0 = k_row_off // k_len
    dq, dp, dv = qn.shape[1], qp.shape[1], v.shape[1]
    kern = functools.partial(_attn_kernel, scale=(MLA_NOPE + MLA_ROPE) ** -0.5)
    return pl.pallas_call(
        kern,
        grid=(n_seq, qb),
        in_specs=[pl.BlockSpec((tq, dq), lambda s, j: (q0 + s * qb + j, 0)),
                  pl.BlockSpec((tq, dp), lambda s, j: (q0 + s * qb + j, 0)),
                  pl.BlockSpec((k_len, dq), lambda s, j: (k0 + s, 0)),
                  pl.BlockSpec((k_len, MLA_ROPE), lambda s, j: (k0 + s, 0)),
                  pl.BlockSpec((k_len, dv), lambda s, j: (k0 + s, 0))],
        out_specs=pl.BlockSpec((tq, dv), lambda s, j: (s * qb + j, 0)),
        out_shape=jax.ShapeDtypeStruct((n_seq * q_len, dv), BF16),
        compiler_params=_params("arbitrary", "arbitrary"),
        name="mla_attention",
    )(qn, qp, kn, kp, v)


def kernel(x_prompt, x_sample, state_mlstm_C, state_mlstm_n, state_mlstm_m, cache_mla_ckv,
           cache_mla_kpe, c, c_ctx, w_ada, b_ada, norm_mix, norm_ffn, norm_final, w_pool,
           pool_scale, w_mlstm_in, b_mlstm_gate, mlstm_head_g, w_mlstm_out, w_mla_in, mla_q_g,
           mla_kv_g, w_mla_qb, w_mla_kvb, w_mla_out, w_router, b_router, w_exp_gate, w_exp_up,
           w_exp_down):
    batch, seq, d = x_prompt.shape
    dec_batch, dec_seq, _ = x_sample.shape
    depth = w_ada.shape[0]
    n_ctx = batch * seq
    n_lat = dec_batch * dec_seq
    hh = MLSTM_HEADS
    past = cache_mla_ckv.shape[2]

    x = jnp.concatenate([x_prompt.reshape(n_ctx, d), x_sample.reshape(n_lat, d)], axis=0)

    n_cv = 1 + dec_batch
    cvecs = jnp.concatenate([c_ctx[None, :], c, jnp.zeros((SUBLANES - n_cv % SUBLANES, d), F32)], axis=0)
    mod = ada_mod_all(cvecs, w_ada, b_ada).reshape(depth, cvecs.shape[0], 6, 1, d)

    w_router_pad = jnp.pad(w_router, ((0, 0), (0, LANES - N_EXPERTS)))
    b_router_col = b_router.reshape(N_EXPERTS, 1)
    g_final = norm_final.reshape(1, d)
    row = lambda a: a.reshape(1, -1)

    outs = {}
    for i in range(depth):
        kind, j = i % 3, i // 3
        m = [mod[i, :n_cv, k] for k in range(6)]
        g_mix = row(norm_mix[i])
        if kind == 0:
            x = pool_layer(x, g_mix, m[0], m[1], m[2], w_pool[j], row(pool_scale[j]),
                           n_ctx, seq, dec_seq)
        elif kind == 1:
            qkv, o_gate, gcol, grow = mlstm_proj(x, g_mix, m[0], m[1], w_mlstm_in, j,
                                                 b_mlstm_gate[j], n_ctx, dec_seq)
            t = n_ctx + n_lat
            l = MLSTM_CHUNK
            grow_c = grow.reshape(4 * hh, t // l, l).transpose(1, 0, 2)
            hs_c, c_new, n_new, m_new = mlstm_scan(qkv, gcol, grow_c, None, 0, batch, seq)
            init = (state_mlstm_C[:, j], state_mlstm_n[:, j][:, :, :, None, :],
                    state_mlstm_m[:, j][:, :, :, None, None])
            hs_l, _, _, _ = mlstm_scan(qkv, gcol, grow_c, init, n_ctx, dec_batch, dec_seq)
            outs["C"] = c_new[:, None]
            outs["n"] = n_new[:, None, :, :, 0, :]
            outs["m"] = m_new[:, None, :, :, 0, 0]
            x = mlstm_out(x, hs_c, hs_l, o_gate, row(mlstm_head_g[j]), w_mlstm_out[j], m[2],
                          n_ctx, dec_seq)
        else:
            cos, sin = _rope_tables(n_ctx, dec_batch, dec_seq)
            qn, qp, ckv, kpe = mla_proj(x, g_mix, m[0], m[1], w_mla_in[j], row(mla_q_g[j]),
                                        row(mla_kv_g[j]), w_mla_qb[j], cos, sin, n_ctx, dec_seq)
            lat_parts_c, lat_parts_p = [], []
            for b in range(dec_batch):
                lo = n_ctx + b * dec_seq
                lat_parts_c += [cache_mla_ckv[b, j], ckv[lo:lo + dec_seq]]
                lat_parts_p += [cache_mla_kpe[b, j], kpe[lo:lo + dec_seq]]
            ckv_all = jnp.concatenate(lat_parts_c + [ckv[:n_ctx]], axis=0)
            kp_all = jnp.concatenate(lat_parts_p + [kpe[:n_ctx]], axis=0).astype(BF16)
            kn, v = mla_kv(ckv_all, w_mla_kvb[j])
            k_lat = past + dec_seq
            o_c = mla_attention(qn, qp, kn, kp_all, v, 0, dec_batch * k_lat, batch, seq, seq)
            o_l = mla_attention(qn, qp, kn, kp_all, v, n_ctx, 0, dec_batch, dec_seq, k_lat)
            outs["ckv"] = ckv[:n_ctx].reshape(batch, 1, seq, -1)
            outs["kpe"] = kpe[:n_ctx].reshape(batch, 1, seq, -1)
            x = resid_proj(x, o_c, o_l, w_mla_out[j], m[2], n_ctx, dec_seq)
        x = moe_layer(x, row(norm_ffn[i]), m[3], m[4], m[5], w_router_pad, b_router_col,
                      w_exp_gate, w_exp_up, w_exp_down, i, g_final, n_ctx, dec_seq,
                      final_norm=(i == depth - 1))

    y_prompt = x[:n_ctx].reshape(batch, seq, d)
    y_sample = x[n_ctx:].reshape(dec_batch, dec_seq, d)
    return (y_prompt, y_sample, outs["C"], outs["n"], outs["m"], outs["ckv"], outs["kpe"])
```

```python
import functools

import numpy as np
import jax
import jax.numpy as jnp
from jax import lax
from jax.experimental import pallas as pl
from jax.experimental.pallas import tpu as pltpu

F32 = jnp.float32
BF16 = jnp.bfloat16

NORM_EPS = 1e-6
GRID_W = 64
POOL_WINDOWS = (2, 4, 8, 16)
MLSTM_HEADS = 4
MLSTM_CHUNK = 256
MLA_HEADS = 8
MLA_NOPE = 128
MLA_ROPE = 64
MLA_V = 128
ROPE_BASE = 10000.0
N_EXPERTS = 16
N_EXPERT_GROUPS = 4
EXPERTS_PER_GROUP = N_EXPERTS // N_EXPERT_GROUPS

LANES = 128
SUBLANES = 8
VMEM_LIMIT = 56 * 1024 * 1024
POOL_TILE = 256
POOL_HALO = 8
ATTN_Q_BLOCK = 256
MOE_TILE = 512
MOE_SEG_ALIGN = 16
MOE_ROW_BLOCK = 160


def _params(*sem):
    return pltpu.CompilerParams(dimension_semantics=sem, vmem_limit_bytes=VMEM_LIMIT)


def _rms(x, g):
    return x * lax.rsqrt(jnp.mean(x * x, axis=-1, keepdims=True) + NORM_EPS) * g


def _modulated(x, g, shift, scale):
    return _rms(x, g) * (1.0 + scale) + shift


def _silu(x):
    return x * jax.nn.sigmoid(x)


def _bdot(a, b):
    return jnp.dot(a.astype(BF16), b.astype(BF16), preferred_element_type=F32)


def _bdot_nt(a, b):
    return lax.dot_general(a.astype(BF16), b.astype(BF16), (((1,), (1,)), ((), ())),
                           preferred_element_type=F32)


def _bdot_tn(a, b):
    return lax.dot_general(a.astype(BF16), b.astype(BF16), (((0,), (0,)), ((), ())),
                           preferred_element_type=F32)


def _cvec_index(n_ctx, dec_seq, tm):
    def idx(i):
        r = i * tm
        return jnp.where(r < n_ctx, 0, (r - n_ctx) // dec_seq + 1)
    return idx


def _ada_kernel(c_ref, w_ref, b_ref, o_ref):
    o_ref[...] = _bdot(_silu(c_ref[...]), w_ref[...]) + b_ref[...]


def ada_mod_all(cvecs, w_ada, b_ada, tn=1536):
    depth, d, n6 = w_ada.shape
    rows = cvecs.shape[0]
    return pl.pallas_call(
        _ada_kernel,
        grid=(depth, n6 // tn),
        in_specs=[pl.BlockSpec((rows, d), lambda l, n: (0, 0)),
                  pl.BlockSpec((None, d, tn), lambda l, n: (l, 0, n)),
                  pl.BlockSpec((None, 1, tn), lambda l, n: (l, 0, n))],
        out_specs=pl.BlockSpec((None, rows, tn), lambda l, n: (l, 0, n)),
        out_shape=jax.ShapeDtypeStruct((depth, rows, n6), F32),
        compiler_params=_params("arbitrary", "arbitrary"),
        name="ada_mod",
    )(cvecs, w_ada, b_ada.reshape(depth, 1, n6))


def _pool_kernel(xc_ref, xp_ref, xn_ref, g_ref, sh_ref, sc_ref, gt_ref, wp_ref, ps_ref, o_ref,
                 buf_ref, *, n_ctx_tiles, ctx_seq_tiles, lat_seq_tiles):
    i = pl.program_id(0)
    is_ctx = i < n_ctx_tiles
    j = jnp.where(is_ctx, i % ctx_seq_tiles, (i - n_ctx_tiles) % lat_seq_tiles)
    nt = jnp.where(is_ctx, ctx_seq_tiles, lat_seq_tiles)
    g, sh, sc = g_ref[...], sh_ref[...], sc_ref[...]
    tp, hl = POOL_TILE, POOL_HALO
    gw = xc_ref.shape[1] // len(POOL_WINDOWS)

    xc = xc_ref[...]
    buf_ref[pl.ds(0, hl), :] = jnp.where(j == 0, 0.0, _modulated(xp_ref[...], g, sh, sc))
    buf_ref[pl.ds(hl, tp), :] = _modulated(xc, g, sh, sc)
    buf_ref[pl.ds(hl + tp, hl), :] = jnp.where(j == nt - 1, 0.0, _modulated(xn_ref[...], g, sh, sc))

    pos = j * tp + lax.broadcasted_iota(jnp.int32, (tp, 1), 0)
    seq_len = nt * tp
    for gi, w in enumerate(POOL_WINDOWS):
        cols = pl.ds(gi * gw, gw)
        acc = buf_ref[pl.ds(hl - w // 2, tp), cols]
        for d in range(-w // 2 + 1, w // 2):
            acc = acc + buf_ref[pl.ds(hl + d, tp), cols]
        cnt = jnp.minimum(pos + w // 2, seq_len) - jnp.maximum(pos - w // 2, 0)
        pooled = acc / cnt.astype(F32) - buf_ref[pl.ds(hl, tp), cols]
        y = _bdot(pooled, wp_ref[gi]) * ps_ref[:, cols]
        o_ref[:, cols] = xc[:, gi * gw:(gi + 1) * gw] + gt_ref[:, cols] * y


def pool_layer(x, g, shift, scale, gate, w_pool, pool_scale, n_ctx, seq, dec_seq):
    t, d = x.shape
    tp, hl = POOL_TILE, POOL_HALO
    cidx = _cvec_index(n_ctx, dec_seq, tp)
    hb = tp // hl
    last_hblk = t // hl - 1
    mod_spec = pl.BlockSpec((None, 1, d), lambda i: (cidx(i), 0, 0))
    row_spec = pl.BlockSpec((1, d), lambda i: (0, 0))
    kern = functools.partial(_pool_kernel, n_ctx_tiles=n_ctx // tp, ctx_seq_tiles=seq // tp,
                             lat_seq_tiles=dec_seq // tp)
    return pl.pallas_call(
        kern,
        grid=(t // tp,),
        in_specs=[pl.BlockSpec((tp, d), lambda i: (i, 0)),
                  pl.BlockSpec((hl, d), lambda i: (jnp.maximum(i * hb - 1, 0), 0)),
                  pl.BlockSpec((hl, d), lambda i: (jnp.minimum((i + 1) * hb, last_hblk), 0)),
                  row_spec, mod_spec, mod_spec, mod_spec,
                  pl.BlockSpec(w_pool.shape, lambda i: (0, 0, 0)),
                  row_spec],
        out_specs=pl.BlockSpec((tp, d), lambda i: (i, 0)),
        out_shape=jax.ShapeDtypeStruct((t, d), F32),
        scratch_shapes=[pltpu.VMEM((tp + 2 * hl, d), F32)],
        compiler_params=_params("arbitrary"),
        name="pool_mixer",
    )(x, x, x, g, shift, scale, gate, w_pool, pool_scale)


def _route(sel, scores):
    e, tm = sel.shape
    row = lax.broadcasted_iota(jnp.int32, (e, tm), 0)
    best = jnp.zeros((1, tm), jnp.int32)
    best_sc = None
    for gidx in range(N_EXPERT_GROUPS):
        r = [sel[gidx * EXPERTS_PER_GROUP + k:gidx * EXPERTS_PER_GROUP + k + 1, :]
             for k in range(EXPERTS_PER_GROUP)]
        top2 = None
        for a in range(EXPERTS_PER_GROUP):
            for b in range(a + 1, EXPERTS_PER_GROUP):
                s = r[a] + r[b]
                top2 = s if top2 is None else jnp.maximum(top2, s)
        if best_sc is None:
            best_sc = top2
        else:
            better = top2 > best_sc
            best = jnp.where(better, gidx, best)
            best_sc = jnp.where(better, top2, best_sc)
    neg = -jnp.inf
    masked = jnp.where(row // EXPERTS_PER_GROUP == best, sel, neg)
    m1 = jnp.max(masked, axis=0, keepdims=True)
    i1 = jnp.min(jnp.where(masked == m1, row, e), axis=0, keepdims=True)
    masked2 = jnp.where(row == i1, neg, masked)
    m2 = jnp.max(masked2, axis=0, keepdims=True)
    i2 = jnp.min(jnp.where(masked2 == m2, row, e), axis=0, keepdims=True)
    hot1 = row == i1
    hot2 = row == i2
    w1 = jnp.sum(jnp.where(hot1, scores, 0.0), axis=0, keepdims=True)
    w2 = jnp.sum(jnp.where(hot2, scores, 0.0), axis=0, keepdims=True)
    tot = w1 + w2
    return best, jnp.where(hot1, w1 / tot, 0.0) + jnp.where(hot2, w2 / tot, 0.0)


def _split_bf16(a, parts):
    out = []
    for _ in range(parts):
        p = a.astype(BF16)
        out.append(p)
        a = a - p.astype(F32)
    return out


def _pad_rows(a, rows):
    return jnp.concatenate([a, jnp.zeros((rows - a.shape[0], a.shape[1]), a.dtype)], axis=0)


def _moe_kernel(x_ref, g_ref, sh_ref, sc_ref, gt_ref, wr_ref, br_ref, wg_ref, wu_ref, wd_ref,
                gf_ref, o_ref, hp_scr, cw_scr, yp_scr, *, final_norm):
    tr, d = x_ref.shape
    trp = hp_scr.shape[0]
    ng, eg = N_EXPERT_GROUPS, EXPERTS_PER_GROUP
    x = x_ref[...]
    h = _modulated(x, g_ref[...], sh_ref[...], sc_ref[...])
    hb = h.astype(BF16)

    h_lo = (h - hb.astype(F32)).astype(BF16)
    wr = wr_ref[...]
    lg = jnp.dot(hb, wr, preferred_element_type=F32)
    logits = lg[:, :LANES] + lg[:, LANES:] + jnp.dot(h_lo, wr[:, :LANES], preferred_element_type=F32)
    scores = jax.nn.sigmoid(logits.T[:N_EXPERTS, :])
    best, comb_t = _route(scores + br_ref[...], scores)

    grp = lax.broadcasted_iota(jnp.int32, (SUBLANES, tr), 0)
    hot_t = (grp == best).astype(F32)
    cw_t = hot_t[0:1, :] * comb_t[0:eg, :]
    for gi in range(1, ng):
        cw_t = cw_t + hot_t[gi:gi + 1, :] * comb_t[gi * eg:(gi + 1) * eg, :]
    hot_c = _pad_rows(hot_t, LANES).T
    cw_c = _pad_rows(cw_t, LANES).T

    ia = lax.broadcasted_iota(jnp.int32, (tr, tr), 0)
    ib = lax.broadcasted_iota(jnp.int32, (tr, tr), 1)
    before = jnp.where(ia < ib, 1.0, 0.0).astype(BF16)
    after = jnp.where(ia > ib, 1.0, 0.0).astype(BF16)
    rank_t = jnp.dot(hot_t.astype(BF16), before, preferred_element_type=F32)
    rank_c = jnp.dot(after, hot_c.astype(BF16), preferred_element_type=F32)

    starts, counts = [], []
    off = jnp.int32(0)
    for gi in range(ng):
        n = jnp.sum(hot_t[gi:gi + 1, :]).astype(jnp.int32)
        n = ((n + MOE_SEG_ALIGN - 1) // MOE_SEG_ALIGN) * MOE_SEG_ALIGN
        starts.append(off)
        counts.append(n)
        off = off + n

    pos_t = hot_t[0:1, :] * (rank_t[0:1, :] + starts[0].astype(F32))
    pos_c = hot_c[:, 0:1] * (rank_c[:, 0:1] + starts[0].astype(F32))
    for gi in range(1, ng):
        pos_t = pos_t + hot_t[gi:gi + 1, :] * (rank_t[gi:gi + 1, :] + starts[gi].astype(F32))
        pos_c = pos_c + hot_c[:, gi:gi + 1] * (rank_c[:, gi:gi + 1] + starts[gi].astype(F32))
    perm = jnp.where(lax.broadcasted_iota(jnp.int32, (trp, tr), 0) == pos_t.astype(jnp.int32),
                     1.0, 0.0).astype(BF16)
    perm_t = jnp.where(lax.broadcasted_iota(jnp.int32, (tr, trp), 1) == pos_c.astype(jnp.int32),
                       1.0, 0.0).astype(BF16)

    hp_scr[...] = jnp.dot(perm, hb, preferred_element_type=F32).astype(BF16)
    cw_p = None
    for part in _split_bf16(cw_c, 3):
        t = jnp.dot(perm, part, preferred_element_type=F32)
        cw_p = t if cw_p is None else cw_p + t
    cw_scr[...] = cw_p
    yp_scr[...] = jnp.zeros_like(yp_scr)

    sb = MOE_ROW_BLOCK
    for gi in range(ng):
        def block(b, carry, gi=gi):
            rows = pl.ds(pl.multiple_of(starts[gi] + b * sb, MOE_SEG_ALIGN), sb)
            hblk = hp_scr[rows, :]
            cwb = cw_scr[rows, :]
            y = None
            for k in range(eg):
                e = gi * eg + k
                gate = jnp.dot(hblk, wg_ref[e], preferred_element_type=F32)
                up = jnp.dot(hblk, wu_ref[e], preferred_element_type=F32)
                hid = (_silu(gate) * up * cwb[:, k:k + 1]).astype(BF16)
                t = jnp.dot(hid, wd_ref[e], preferred_element_type=F32)
                y = t if y is None else y + t
            yp_scr[rows, :] = y
            return carry
        lax.fori_loop(0, (counts[gi] + sb - 1) // sb, block, 0)

    moe = None
    for part in _split_bf16(yp_scr[...], 3):
        t = jnp.dot(perm_t, part, preferred_element_type=F32)
        moe = t if moe is None else moe + t
    out = x + gt_ref[...] * moe
    if final_norm:
        out = _rms(out, gf_ref[...])
    o_ref[...] = out


def moe_layer(x, g, shift, scale, gate, w_router_split, b_router_col, wg, wu, wd, layer, g_final,
              n_ctx, dec_seq, final_norm, tm=MOE_TILE):
    t, d = x.shape
    _, n_e, _, f = wg.shape
    trp = tm + N_EXPERT_GROUPS * MOE_SEG_ALIGN + MOE_ROW_BLOCK
    cidx = _cvec_index(n_ctx, dec_seq, tm)
    mod_spec = pl.BlockSpec((None, 1, d), lambda i: (cidx(i), 0, 0))
    row_spec = pl.BlockSpec((1, d), lambda i: (0, 0))
    once = pl.Buffered(1)
    return pl.pallas_call(
        functools.partial(_moe_kernel, final_norm=final_norm),
        grid=(t // tm,),
        in_specs=[pl.BlockSpec((tm, d), lambda i: (i, 0)),
                  row_spec, mod_spec, mod_spec, mod_spec,
                  pl.BlockSpec(w_router_split.shape, lambda i: (0, 0)),
                  pl.BlockSpec(b_router_col.shape, lambda i: (0, 0)),
                  pl.BlockSpec((None, n_e, d, f), lambda i: (layer, 0, 0, 0), pipeline_mode=once),
                  pl.BlockSpec((None, n_e, d, f), lambda i: (layer, 0, 0, 0), pipeline_mode=once),
                  pl.BlockSpec((None, n_e, f, d), lambda i: (layer, 0, 0, 0), pipeline_mode=once),
                  row_spec],
        out_specs=pl.BlockSpec((tm, d), lambda i: (i, 0)),
        out_shape=jax.ShapeDtypeStruct((t, d), F32),
        scratch_shapes=[pltpu.VMEM((trp, d), BF16), pltpu.VMEM((trp, LANES), F32),
                        pltpu.VMEM((trp, d), F32)],
        compiler_params=_params("arbitrary"),
        name="moe",
    )(x, g, shift, scale, gate, w_router_split, b_router_col, wg, wu, wd, g_final)


def _per_stream(n_ctx_tiles, ctx_ref, lat_ref, fn):
    i = pl.program_id(0)

    @pl.when(i < n_ctx_tiles)
    def _():
        fn(ctx_ref[...])

    @pl.when(i >= n_ctx_tiles)
    def _():
        fn(lat_ref[...])


def _stream_specs(tm, k, n_ctx_tiles):
    return [pl.BlockSpec((tm, k), lambda i: (jnp.minimum(i, n_ctx_tiles - 1), 0)),
            pl.BlockSpec((tm, k), lambda i: (jnp.maximum(i - n_ctx_tiles, 0), 0))]


def _resid_proj_kernel(x_ref, ac_ref, al_ref, w_ref, gt_ref, o_ref, *, n_ctx_tiles):
    def run(a):
        o_ref[...] = x_ref[...] + gt_ref[...] * _bdot(a, w_ref[...])
    _per_stream(n_ctx_tiles, ac_ref, al_ref, run)


def _mlstm_out_kernel(x_ref, hc_ref, hl_ref, og_ref, hg_ref, w_ref, gt_ref, o_ref, *, n_ctx_tiles):
    def run(hs):
        a = jax.nn.sigmoid(og_ref[...]) * (hs * hg_ref[...])
        o_ref[...] = x_ref[...] + gt_ref[...] * _bdot(a, w_ref[...])
    _per_stream(n_ctx_tiles, hc_ref, hl_ref, run)


def resid_proj(x, a_ctx, a_lat, w, gate, n_ctx, dec_seq, tm=512):
    t, d = x.shape
    k = a_ctx.shape[1]
    cidx = _cvec_index(n_ctx, dec_seq, tm)
    nct = n_ctx // tm
    return pl.pallas_call(
        functools.partial(_resid_proj_kernel, n_ctx_tiles=nct),
        grid=(t // tm,),
        in_specs=[pl.BlockSpec((tm, d), lambda i: (i, 0))] + _stream_specs(tm, k, nct) + [
            pl.BlockSpec((k, d), lambda i: (0, 0)),
            pl.BlockSpec((None, 1, d), lambda i: (cidx(i), 0, 0))],
        out_specs=pl.BlockSpec((tm, d), lambda i: (i, 0)),
        out_shape=jax.ShapeDtypeStruct((t, d), F32),
        compiler_params=_params("arbitrary"),
        name="resid_proj",
    )(x, a_ctx, a_lat, w, gate)


def mlstm_out(x, hs_ctx, hs_lat, o_gate, head_g, w, gate, n_ctx, dec_seq, tm=512):
    t, d = x.shape
    k = hs_ctx.shape[1]
    cidx = _cvec_index(n_ctx, dec_seq, tm)
    nct = n_ctx // tm
    return pl.pallas_call(
        functools.partial(_mlstm_out_kernel, n_ctx_tiles=nct),
        grid=(t // tm,),
        in_specs=[pl.BlockSpec((tm, d), lambda i: (i, 0))] + _stream_specs(tm, k, nct) + [
            pl.BlockSpec((tm, k), lambda i: (i, 0)),
            pl.BlockSpec((1, k), lambda i: (0, 0)),
            pl.BlockSpec((k, d), lambda i: (0, 0)),
            pl.BlockSpec((None, 1, d), lambda i: (cidx(i), 0, 0))],
        out_specs=pl.BlockSpec((tm, d), lambda i: (i, 0)),
        out_shape=jax.ShapeDtypeStruct((t, d), F32),
        compiler_params=_params("arbitrary"),
        name="mlstm_out",
    )(x, hs_ctx, hs_lat, o_gate, head_g, w, gate)


def _mlstm_proj_kernel(x_ref, g_ref, sh_ref, sc_ref, wqk_ref, wv_ref, wo_ref, wgt_ref, bg_ref,
                       wgtt_ref, bgt_ref, qs_ref, qkv_ref, o_ref, gc_ref, gr_ref):
    h = _modulated(x_ref[...], g_ref[...], sh_ref[...], sc_ref[...]).astype(BF16)
    nqk = wqk_ref.shape[1]
    qkv_ref[:, :nqk] = (jnp.dot(h, wqk_ref[...].astype(BF16), preferred_element_type=F32)
                        * qs_ref[...]).astype(BF16)
    qkv_ref[:, nqk:] = jnp.dot(h, wv_ref[...].astype(BF16),
                               preferred_element_type=F32).astype(BF16)
    o_ref[...] = jnp.dot(h, wo_ref[...].astype(BF16), preferred_element_type=F32)
    gc_ref[...] = jnp.dot(h, wgt_ref[...].astype(BF16), preferred_element_type=F32) + bg_ref[...]
    gr_ref[...] = _bdot_nt(wgtt_ref[...], h) + bgt_ref[...]


def mlstm_proj(x, g, shift, scale, w_in_all, layer, b_gate, n_ctx, dec_seq, tm=512):
    t, d = x.shape
    hh = MLSTM_HEADS
    hv = d
    hk = hv // 2
    ng = 4 * hh
    w_g = w_in_all[layer, :, 2 * hk + 2 * hv:]
    w_g_pad = jnp.pad(w_g, ((0, 0), (0, LANES - ng)))
    b_pad = jnp.pad(b_gate.reshape(1, ng), ((0, 0), (0, LANES - ng)))
    w_g_t = w_g.T
    b_t = b_gate.reshape(ng, 1)
    dk = hk // hh
    q_scale = jnp.concatenate([jnp.full((1, hk), dk ** -0.5, F32), jnp.ones((1, hk), F32)], axis=1)
    cidx = _cvec_index(n_ctx, dec_seq, tm)
    mod_spec = pl.BlockSpec((None, 1, d), lambda i: (cidx(i), 0, 0))
    full = lambda a: pl.BlockSpec(a.shape, lambda i: (0,) * a.ndim)
    assert 2 * hk == hv
    w_col = lambda n: pl.BlockSpec((None, d, hv), lambda i: (layer, 0, n))
    return pl.pallas_call(
        _mlstm_proj_kernel,
        grid=(t // tm,),
        in_specs=[pl.BlockSpec((tm, d), lambda i: (i, 0)), full(g), mod_spec, mod_spec,
                  w_col(0), w_col(1), w_col(2), full(w_g_pad), full(b_pad), full(w_g_t),
                  full(b_t), full(q_scale)],
        out_specs=[pl.BlockSpec((tm, 2 * hk + hv), lambda i: (i, 0)),
                   pl.BlockSpec((tm, hv), lambda i: (i, 0)),
                   pl.BlockSpec((tm, LANES), lambda i: (i, 0)),
                   pl.BlockSpec((ng, tm), lambda i: (0, i))],
        out_shape=[jax.ShapeDtypeStruct((t, 2 * hk + hv), BF16),
                   jax.ShapeDtypeStruct((t, hv), F32),
                   jax.ShapeDtypeStruct((t, LANES), F32),
                   jax.ShapeDtypeStruct((ng, t), F32)],
        compiler_params=_params("arbitrary"),
        name="mlstm_proj",
    )(x, g, shift, scale, w_in_all, w_in_all, w_in_all, w_g_pad, b_pad, w_g_t, b_t, q_scale)


def _log_sigmoid(x):
    return jnp.minimum(x, 0.0) - jnp.log(1.0 + jnp.exp(-jnp.abs(x)))


def _mlstm_chunk(q, k, v, i_col, f_col, i_row, f_row, c_st, n_st, m_st, rev):
    l = q.shape[0]
    lf_col = _log_sigmoid(f_col)
    lf_row = _log_sigmoid(f_row)
    tt = lax.broadcasted_iota(jnp.int32, (l, l), 0)
    ss = lax.broadcasted_iota(jnp.int32, (l, l), 1)
    causal = (ss >= tt) if rev else (ss <= tt)
    b_col = jnp.sum(jnp.where(causal, lf_row, 0.0), axis=1, keepdims=True)
    feeds = (tt >= ss) if rev else (tt <= ss)
    b_row = jnp.sum(jnp.where(feeds, lf_col, 0.0), axis=0, keepdims=True)
    log_d = jnp.where(causal, b_col - b_row + i_row, -jnp.inf)
    log_state = b_col + m_st
    m_t = jnp.maximum(log_state, jnp.max(log_d, axis=1, keepdims=True))
    dw = jnp.exp(log_d - m_t)
    sw = jnp.exp(log_state - m_t)
    a = _bdot_nt(q, k) * dw
    num = sw * _bdot(q, c_st) + _bdot(a, v)
    qn = jnp.sum(q.astype(F32) * n_st, axis=1, keepdims=True)
    den = sw * qn + jnp.sum(a, axis=1, keepdims=True)
    h = num / jnp.maximum(jnp.abs(den), jnp.exp(-m_t))
    b_last = b_col[0:1, :] if rev else b_col[l - 1:l, :]
    log_k = b_last - b_col + i_col
    m_new = jnp.maximum(b_last + m_st, jnp.max(log_k, axis=0, keepdims=True))
    kw = jnp.exp(log_k - m_new)
    decay = jnp.exp(b_last + m_st - m_new)
    kwk = kw * k.astype(F32)
    c_new = decay * c_st + _bdot_tn(kwk, v)
    n_new = decay * n_st + jnp.sum(kwk, axis=0, keepdims=True)
    return h, c_new, n_new, m_new


def _mlstm_scan_kernel(*refs, n_chunks, zero_init):
    if zero_init:
        q_ref, k_ref, v_ref, gc_ref, gr_ref, hs_ref, c_ref, n_ref, m_ref, hb_scr = refs
        c_ref[...] = jnp.zeros_like(c_ref)
        n_ref[...] = jnp.zeros_like(n_ref)
        m_ref[...] = jnp.zeros_like(m_ref)
    else:
        (q_ref, k_ref, v_ref, gc_ref, gr_ref, c0_ref, n0_ref, m0_ref,
         hs_ref, c_ref, n_ref, m_ref, hb_scr) = refs
        c_ref[...] = c0_ref[...]
        n_ref[...] = n0_ref[...]
        m_ref[...] = m0_ref[...]
    l = MLSTM_CHUNK
    hh = MLSTM_HEADS
    dk = q_ref.shape[1] // hh
    dv = v_ref.shape[1] // hh

    def body(c, carry):
        cr = n_chunks - 1 - c
        rows = (pl.ds(pl.multiple_of(c * l, l), l), pl.ds(pl.multiple_of(cr * l, l), l))
        grs = (gr_ref[c], gr_ref[cr])
        outs, states = ([], []), []
        for d in range(2):
            q, k, v, gc, gr = q_ref[rows[d], :], k_ref[rows[d], :], v_ref[rows[d], :], gc_ref[rows[d], :], grs[d]
            for h in range(hh):
                ci, cf = 2 * d * hh + h, (2 * d + 1) * hh + h
                o, *st = _mlstm_chunk(q[:, h * dk:(h + 1) * dk], k[:, h * dk:(h + 1) * dk],
                                      v[:, h * dv:(h + 1) * dv], gc[:, ci:ci + 1], gc[:, cf:cf + 1],
                                      gr[ci:ci + 1, :], gr[cf:cf + 1, :],
                                      c_ref[d, h], n_ref[d, h], m_ref[d, h], d == 1)
                outs[d].append(o)
                states.append((d, h, st))
        hs_ref[rows[0], :] = jnp.concatenate(outs[0], axis=1)
        hb_scr[rows[1], :] = jnp.concatenate(outs[1], axis=1)
        for d, h, (c_new, n_new, m_new) in states:
            c_ref[d, h], n_ref[d, h], m_ref[d, h] = c_new, n_new, m_new
        return carry

    lax.fori_loop(0, n_chunks, body, 0)
    for h in range(hh):
        cols = pl.ds(h * dv, dv)
        hs = hs_ref[:, cols] + hb_scr[:, cols]
        hs_ref[:, cols] = hs * lax.rsqrt(jnp.mean(hs * hs, axis=-1, keepdims=True) + NORM_EPS)


def mlstm_scan(qkv, gcol, grow, init, row_off, n_seq, seq_len):
    hh = MLSTM_HEADS
    hv = qkv.shape[1] // 2
    dv = hv // hh
    dk = dv // 2
    l = MLSTM_CHUNK
    nc = seq_len // l
    ob = row_off // seq_len
    kern = functools.partial(_mlstm_scan_kernel, n_chunks=nc, zero_init=init is None)
    st = lambda *tail: pl.BlockSpec((None, 2, hh) + tail, lambda s: (s, 0, 0) + (0,) * len(tail))
    states = [st(dk, dv), st(1, dk), st(1, 1)]
    return pl.pallas_call(
        kern,
        grid=(n_seq,),
        in_specs=[pl.BlockSpec((seq_len, hh * dk), lambda s: (ob + s, 0)),
                  pl.BlockSpec((seq_len, hh * dk), lambda s: (ob + s, 1)),
                  pl.BlockSpec((seq_len, hv), lambda s: (ob + s, 1)),
                  pl.BlockSpec((seq_len, LANES), lambda s: (ob + s, 0)),
                  pl.BlockSpec((nc, 4 * hh, l), lambda s: (ob + s, 0, 0))]
                 + ([] if init is None else states),
        out_specs=[pl.BlockSpec((seq_len, hv), lambda s: (s, 0))] + states,
        out_shape=[jax.ShapeDtypeStruct((n_seq * seq_len, hv), F32),
                   jax.ShapeDtypeStruct((n_seq, 2, hh, dk, dv), F32),
                   jax.ShapeDtypeStruct((n_seq, 2, hh, 1, dk), F32),
                   jax.ShapeDtypeStruct((n_seq, 2, hh, 1, 1), F32)],
        scratch_shapes=[pltpu.VMEM((seq_len, hv), F32)],
        compiler_params=_params("arbitrary"),
        name="mlstm_scan",
    )(qkv, qkv, qkv, gcol, grow, *(() if init is None else init))


def _mla_proj_kernel(x_ref, g_ref, sh_ref, sc_ref, win_ref, qg_ref, kvg_ref, wqb_ref, cos_ref,
                     sin_ref, qn_ref, qp_ref, ckv_ref, kpe_ref, *, q_lora, kv_lora, rope, n_heads):
    h = _modulated(x_ref[...], g_ref[...], sh_ref[...], sc_ref[...])
    proj = _bdot(h, win_ref[...])
    q_lat = proj[:, :q_lora]
    ckv_ref[...] = _rms(proj[:, q_lora:q_lora + kv_lora], kvg_ref[...])
    cos, sin = cos_ref[...], sin_ref[...]
    kpe = proj[:, q_lora + kv_lora:q_lora + kv_lora + rope]
    kpe_rot = proj[:, q_lora + kv_lora + rope:q_lora + kv_lora + 2 * rope]
    kpe_ref[...] = kpe * cos[:, :rope] + kpe_rot * sin[:, :rope]
    q = _bdot(_rms(q_lat, qg_ref[...]), wqb_ref[...])
    n_nope = qn_ref.shape[1]
    n_rope = qp_ref.shape[1]
    qn_ref[...] = q[:, :n_nope].astype(BF16)
    reps = n_rope // cos.shape[1]
    cos_h = jnp.concatenate([cos] * reps, axis=1)
    sin_h = jnp.concatenate([sin] * reps, axis=1)
    qp_ref[...] = (q[:, n_nope:n_nope + n_rope] * cos_h
                   + q[:, n_nope + n_rope:] * sin_h).astype(BF16)


def _rot_cols(w):
    n = w.shape[1]
    j = np.arange(n)
    quarter = MLA_ROPE // 4
    first = (j % (2 * quarter)) < quarter
    src = np.where(first, j + quarter, j - quarter)
    sign = np.where(first, -1.0, 1.0).astype(np.float32)
    return w[:, src] * sign


def _rope_tables(n_ctx, dec_batch, dec_seq):
    quarter = MLA_ROPE // 4
    freq = np.power(np.float32(ROPE_BASE), -np.arange(quarter, dtype=np.float32) / np.float32(quarter))
    pos = np.arange(dec_seq)
    ang_r = (pos // GRID_W).astype(np.float32)[:, None] * freq[None, :]
    ang_c = (pos % GRID_W).astype(np.float32)[:, None] * freq[None, :]
    ang = np.concatenate([ang_r, ang_r, ang_c, ang_c], axis=1).astype(np.float32)
    cos = np.concatenate([np.ones((n_ctx, MLA_ROPE), np.float32)] + [np.cos(ang)] * dec_batch, axis=0)
    sin = np.concatenate([np.zeros((n_ctx, MLA_ROPE), np.float32)] + [np.sin(ang)] * dec_batch, axis=0)
    reps = LANES // MLA_ROPE
    return (jnp.asarray(np.tile(cos, (1, reps)), F32), jnp.asarray(np.tile(sin, (1, reps)), F32))


def mla_proj(x, g, shift, scale, w_in, q_g, kv_g, w_qb, cos, sin, n_ctx, dec_seq, tm=512):
    t, d = x.shape
    hh, nope, rope = MLA_HEADS, MLA_NOPE, MLA_ROPE
    q_lora = q_g.shape[1]
    kv_lora = kv_g.shape[1]
    w_in_ext = jnp.concatenate([w_in, _rot_cols(w_in[:, q_lora + kv_lora:])], axis=1)
    w3 = w_qb.reshape(q_lora, hh, nope + rope)
    w_qn = w3[:, :, :nope].reshape(q_lora, hh * nope)
    w_qp = w3[:, :, nope:].reshape(q_lora, hh * rope)
    w_qb_ext = jnp.concatenate([w_qn, w_qp, _rot_cols(w_qp)], axis=1)
    cidx = _cvec_index(n_ctx, dec_seq, tm)
    mod_spec = pl.BlockSpec((None, 1, d), lambda i: (cidx(i), 0, 0))
    full = lambda a: pl.BlockSpec(a.shape, lambda i: (0,) * a.ndim)
    kern = functools.partial(_mla_proj_kernel, q_lora=q_lora, kv_lora=kv_lora, rope=rope, n_heads=hh)
    return pl.pallas_call(
        kern,
        grid=(t // tm,),
        in_specs=[pl.BlockSpec((tm, d), lambda i: (i, 0)), full(g), mod_spec, mod_spec,
                  full(w_in_ext), full(q_g), full(kv_g), full(w_qb_ext),
                  pl.BlockSpec((tm, LANES), lambda i: (i, 0)),
                  pl.BlockSpec((tm, LANES), lambda i: (i, 0))],
        out_specs=[pl.BlockSpec((tm, hh * nope), lambda i: (i, 0)),
                   pl.BlockSpec((tm, hh * rope), lambda i: (i, 0)),
                   pl.BlockSpec((tm, kv_lora), lambda i: (i, 0)),
                   pl.BlockSpec((tm, rope), lambda i: (i, 0))],
        out_shape=[jax.ShapeDtypeStruct((t, hh * nope), BF16),
                   jax.ShapeDtypeStruct((t, hh * rope), BF16),
                   jax.ShapeDtypeStruct((t, kv_lora), F32),
                   jax.ShapeDtypeStruct((t, rope), F32)],
        compiler_params=_params("arbitrary"),
        name="mla_proj",
    )(x, g, shift, scale, w_in_ext, q_g, kv_g, w_qb_ext, cos, sin)


def _mla_kv_kernel(ckv_ref, w_ref, kn_ref, v_ref):
    kv = _bdot(ckv_ref[...], w_ref[...])
    n = kn_ref.shape[1]
    kn_ref[...] = kv[:, :n].astype(BF16)
    v_ref[...] = kv[:, n:].astype(BF16)


def mla_kv(ckv_all, w_kvb, tm=512):
    r, kv_lora = ckv_all.shape
    hh, nope, vd = MLA_HEADS, MLA_NOPE, MLA_V
    w3 = w_kvb.reshape(kv_lora, hh, nope + vd)
    w_perm = jnp.concatenate([w3[:, :, :nope].reshape(kv_lora, hh * nope),
                              w3[:, :, nope:].reshape(kv_lora, hh * vd)], axis=1)
    return pl.pallas_call(
        _mla_kv_kernel,
        grid=(r // tm,),
        in_specs=[pl.BlockSpec((tm, kv_lora), lambda i: (i, 0)),
                  pl.BlockSpec(w_perm.shape, lambda i: (0, 0))],
        out_specs=[pl.BlockSpec((tm, hh * nope), lambda i: (i, 0)),
                   pl.BlockSpec((tm, hh * vd), lambda i: (i, 0))],
        out_shape=[jax.ShapeDtypeStruct((r, hh * nope), BF16),
                   jax.ShapeDtypeStruct((r, hh * vd), BF16)],
        compiler_params=_params("arbitrary"),
        name="mla_kv",
    )(ckv_all, w_perm)


def _attn_kernel(qn_ref, qp_ref, kn_ref, kp_ref, v_ref, o_ref, *, scale):
    kp = kp_ref[...]
    for h in range(MLA_HEADS):
        s = _bdot_nt(qn_ref[:, h * MLA_NOPE:(h + 1) * MLA_NOPE], kn_ref[:, h * MLA_NOPE:(h + 1) * MLA_NOPE])
        s = (s + _bdot_nt(qp_ref[:, h * MLA_ROPE:(h + 1) * MLA_ROPE], kp)) * scale
        e = jnp.exp(s - jnp.max(s, axis=-1, keepdims=True))
        p = e / jnp.sum(e, axis=-1, keepdims=True)
        o_ref[:, h * MLA_V:(h + 1) * MLA_V] = _bdot(p, v_ref[:, h * MLA_V:(h + 1) * MLA_V]).astype(BF16)


def mla_attention(qn, qp, kn, kp, v, q_row_off, k_row_off, n_seq, q_len, k_len):
    tq = ATTN_Q_BLOCK
    qb = q_len // tq
    q0 = q_row_off // tq
    k0 = k_row_off // k_len
    dq, dp, dv = qn.shape[1], qp.shape[1], v.shape[1]
    kern = functools.partial(_attn_kernel, scale=(MLA_NOPE + MLA_ROPE) ** -0.5)
    return pl.pallas_call(
        kern,
        grid=(n_seq, qb),
        in_specs=[pl.BlockSpec((tq, dq), lambda s, j: (q0 + s * qb + j, 0)),
                  pl.BlockSpec((tq, dp), lambda s, j: (q0 + s * qb + j, 0)),
                  pl.BlockSpec((k_len, dq), lambda s, j: (k0 + s, 0)),
                  pl.BlockSpec((k_len, MLA_ROPE), lambda s, j: (k0 + s, 0)),
                  pl.BlockSpec((k_len, dv), lambda s, j: (k0 + s, 0))],
        out_specs=pl.BlockSpec((tq, dv), lambda s, j: (s * qb + j, 0)),
        out_shape=jax.ShapeDtypeStruct((n_seq * q_len, dv), BF16),
        compiler_params=_params("arbitrary", "arbitrary"),
        name="mla_attention",
    )(qn, qp, kn, kp, v)


def kernel(x_prompt, x_sample, state_mlstm_C, state_mlstm_n, state_mlstm_m, cache_mla_ckv,
           cache_mla_kpe, c, c_ctx, w_ada, b_ada, norm_mix, norm_ffn, norm_final, w_pool,
           pool_scale, w_mlstm_in, b_mlstm_gate, mlstm_head_g, w_mlstm_out, w_mla_in, mla_q_g,
           mla_kv_g, w_mla_qb, w_mla_kvb, w_mla_out, w_router, b_router, w_exp_gate, w_exp_up,
           w_exp_down):
    batch, seq, d = x_prompt.shape
    dec_batch, dec_seq, _ = x_sample.shape
    depth = w_ada.shape[0]
    n_ctx = batch * seq
    n_lat = dec_batch * dec_seq
    hh = MLSTM_HEADS
    past = cache_mla_ckv.shape[2]

    x = jnp.concatenate([x_prompt.reshape(n_ctx, d), x_sample.reshape(n_lat, d)], axis=0)

    n_cv = 1 + dec_batch
    cvecs = jnp.concatenate([c_ctx[None, :], c, jnp.zeros((SUBLANES - n_cv % SUBLANES, d), F32)], axis=0)
    mod = ada_mod_all(cvecs, w_ada, b_ada).reshape(depth, cvecs.shape[0], 6, 1, d)

    w_router_pad = jnp.pad(w_router, ((0, 0), (0, LANES - N_EXPERTS)))
    w_router_hi = w_router_pad.astype(BF16)
    w_router_lo = (w_router_pad - w_router_hi.astype(F32)).astype(BF16)
    w_router_split = jnp.concatenate([w_router_hi, w_router_lo], axis=1)
    b_router_col = b_router.reshape(N_EXPERTS, 1)
    wg_bf, wu_bf, wd_bf = (w.astype(BF16) for w in (w_exp_gate, w_exp_up, w_exp_down))
    g_final = norm_final.reshape(1, d)
    row = lambda a: a.reshape(1, -1)

    outs = {}
    for i in range(depth):
        kind, j = i % 3, i // 3
        m = [mod[i, :n_cv, k] for k in range(6)]
        g_mix = row(norm_mix[i])
        if kind == 0:
            x = pool_layer(x, g_mix, m[0], m[1], m[2], w_pool[j], row(pool_scale[j]),
                           n_ctx, seq, dec_seq)
        elif kind == 1:
            qkv, o_gate, gcol, grow = mlstm_proj(x, g_mix, m[0], m[1], w_mlstm_in, j,
                                                 b_mlstm_gate[j], n_ctx, dec_seq)
            t = n_ctx + n_lat
            l = MLSTM_CHUNK
            grow_c = grow.reshape(4 * hh, t // l, l).transpose(1, 0, 2)
            hs_c, c_new, n_new, m_new = mlstm_scan(qkv, gcol, grow_c, None, 0, batch, seq)
            init = (state_mlstm_C[:, j], state_mlstm_n[:, j][:, :, :, None, :],
                    state_mlstm_m[:, j][:, :, :, None, None])
            hs_l, _, _, _ = mlstm_scan(qkv, gcol, grow_c, init, n_ctx, dec_batch, dec_seq)
            outs["C"] = c_new[:, None]
            outs["n"] = n_new[:, None, :, :, 0, :]
            outs["m"] = m_new[:, None, :, :, 0, 0]
            x = mlstm_out(x, hs_c, hs_l, o_gate, row(mlstm_head_g[j]), w_mlstm_out[j], m[2],
                          n_ctx, dec_seq)
        else:
            cos, sin = _rope_tables(n_ctx, dec_batch, dec_seq)
            qn, qp, ckv, kpe = mla_proj(x, g_mix, m[0], m[1], w_mla_in[j], row(mla_q_g[j]),
                                        row(mla_kv_g[j]), w_mla_qb[j], cos, sin, n_ctx, dec_seq)
            lat_parts_c, lat_parts_p = [], []
            for b in range(dec_batch):
                lo = n_ctx + b * dec_seq
                lat_parts_c += [cache_mla_ckv[b, j], ckv[lo:lo + dec_seq]]
                lat_parts_p += [cache_mla_kpe[b, j], kpe[lo:lo + dec_seq]]
            ckv_all = jnp.concatenate(lat_parts_c + [ckv[:n_ctx]], axis=0)
            kp_all = jnp.concatenate(lat_parts_p + [kpe[:n_ctx]], axis=0).astype(BF16)
            kn, v = mla_kv(ckv_all, w_mla_kvb[j])
            k_lat = past + dec_seq
            o_c = mla_attention(qn, qp, kn, kp_all, v, 0, dec_batch * k_lat, batch, seq, seq)
            o_l = mla_attention(qn, qp, kn, kp_all, v, n_ctx, 0, dec_batch, dec_seq, k_lat)
            outs["ckv"] = ckv[:n_ctx].reshape(batch, 1, seq, -1)
            outs["kpe"] = kpe[:n_ctx].reshape(batch, 1, seq, -1)
            x = resid_proj(x, o_c, o_l, w_mla_out[j], m[2], n_ctx, dec_seq)
        x = moe_layer(x, row(norm_ffn[i]), m[3], m[4], m[5], w_router_split, b_router_col,
                      wg_bf, wu_bf, wd_bf, i, g_final, n_ctx, dec_seq,
                      final_norm=(i == depth - 1))

    y_prompt = x[:n_ctx].reshape(batch, seq, d)
    y_sample = x[n_ctx:].reshape(dec_batch, dec_seq, d)
    return (y_prompt, y_sample, outs["C"], outs["n"], outs["m"], outs["ckv"], outs["kpe"])
```

```python
import functools

import numpy as np
import jax
import jax.numpy as jnp
from jax import lax
from jax.experimental import pallas as pl
from jax.experimental.pallas import tpu as pltpu

F32 = jnp.float32
BF16 = jnp.bfloat16

NORM_EPS = 1e-6
GRID_W = 64
POOL_WINDOWS = (2, 4, 8, 16)
MLSTM_HEADS = 4
MLSTM_CHUNK = 256
MLA_HEADS = 8
MLA_NOPE = 128
MLA_ROPE = 64
MLA_V = 128
ROPE_BASE = 10000.0
N_EXPERTS = 16
N_EXPERT_GROUPS = 4
EXPERTS_PER_GROUP = N_EXPERTS // N_EXPERT_GROUPS

LANES = 128
SUBLANES = 8
VMEM_LIMIT = 56 * 1024 * 1024
POOL_TILE = 256
POOL_HALO = 8
ATTN_Q_BLOCK = 256
MOE_TILE = 512
MOE_SEG_ALIGN = 16
MOE_ROW_BLOCK = 160


def _params(*sem):
    return pltpu.CompilerParams(dimension_semantics=sem, vmem_limit_bytes=VMEM_LIMIT)


def _rms(x, g):
    return x * lax.rsqrt(jnp.mean(x * x, axis=-1, keepdims=True) + NORM_EPS) * g


def _modulated(x, g, shift, scale):
    return _rms(x, g) * (1.0 + scale) + shift


def _silu(x):
    return x * jax.nn.sigmoid(x)


def _bdot(a, b):
    return jnp.dot(a.astype(BF16), b.astype(BF16), preferred_element_type=F32)


def _bdot_nt(a, b):
    return lax.dot_general(a.astype(BF16), b.astype(BF16), (((1,), (1,)), ((), ())),
                           preferred_element_type=F32)


def _bdot_tn(a, b):
    return lax.dot_general(a.astype(BF16), b.astype(BF16), (((0,), (0,)), ((), ())),
                           preferred_element_type=F32)


def _cvec_index(n_ctx, dec_seq, tm):
    def idx(i):
        r = i * tm
        return jnp.where(r < n_ctx, 0, (r - n_ctx) // dec_seq + 1)
    return idx


def _ada_kernel(c_ref, w_ref, b_ref, o_ref):
    o_ref[...] = _bdot(_silu(c_ref[...]), w_ref[...]) + b_ref[...]


def ada_mod_all(cvecs, w_ada, b_ada, tn=1536):
    depth, d, n6 = w_ada.shape
    rows = cvecs.shape[0]
    return pl.pallas_call(
        _ada_kernel,
        grid=(depth, n6 // tn),
        in_specs=[pl.BlockSpec((rows, d), lambda l, n: (0, 0)),
                  pl.BlockSpec((None, d, tn), lambda l, n: (l, 0, n)),
                  pl.BlockSpec((None, 1, tn), lambda l, n: (l, 0, n))],
        out_specs=pl.BlockSpec((None, rows, tn), lambda l, n: (l, 0, n)),
        out_shape=jax.ShapeDtypeStruct((depth, rows, n6), F32),
        compiler_params=_params("arbitrary", "arbitrary"),
        name="ada_mod",
    )(cvecs, w_ada, b_ada.reshape(depth, 1, n6))


def _pool_kernel(xc_ref, xp_ref, xn_ref, g_ref, sh_ref, sc_ref, gt_ref, wp_ref, ps_ref, o_ref,
                 buf_ref, *, n_ctx_tiles, ctx_seq_tiles, lat_seq_tiles):
    i = pl.program_id(0)
    is_ctx = i < n_ctx_tiles
    j = jnp.where(is_ctx, i % ctx_seq_tiles, (i - n_ctx_tiles) % lat_seq_tiles)
    nt = jnp.where(is_ctx, ctx_seq_tiles, lat_seq_tiles)
    g, sh, sc = g_ref[...], sh_ref[...], sc_ref[...]
    tp, hl = POOL_TILE, POOL_HALO
    gw = xc_ref.shape[1] // len(POOL_WINDOWS)

    xc = xc_ref[...]
    buf_ref[pl.ds(0, hl), :] = jnp.where(j == 0, 0.0, _modulated(xp_ref[...], g, sh, sc))
    buf_ref[pl.ds(hl, tp), :] = _modulated(xc, g, sh, sc)
    buf_ref[pl.ds(hl + tp, hl), :] = jnp.where(j == nt - 1, 0.0, _modulated(xn_ref[...], g, sh, sc))

    pos = j * tp + lax.broadcasted_iota(jnp.int32, (tp, 1), 0)
    seq_len = nt * tp
    for gi, w in enumerate(POOL_WINDOWS):
        cols = pl.ds(gi * gw, gw)
        acc = buf_ref[pl.ds(hl - w // 2, tp), cols]
        for d in range(-w // 2 + 1, w // 2):
            acc = acc + buf_ref[pl.ds(hl + d, tp), cols]
        cnt = jnp.minimum(pos + w // 2, seq_len) - jnp.maximum(pos - w // 2, 0)
        pooled = acc / cnt.astype(F32) - buf_ref[pl.ds(hl, tp), cols]
        y = _bdot(pooled, wp_ref[gi]) * ps_ref[:, cols]
        o_ref[:, cols] = xc[:, gi * gw:(gi + 1) * gw] + gt_ref[:, cols] * y


def pool_layer(x, g, shift, scale, gate, w_pool, pool_scale, n_ctx, seq, dec_seq):
    t, d = x.shape
    tp, hl = POOL_TILE, POOL_HALO
    cidx = _cvec_index(n_ctx, dec_seq, tp)
    hb = tp // hl
    last_hblk = t // hl - 1
    mod_spec = pl.BlockSpec((None, 1, d), lambda i: (cidx(i), 0, 0))
    row_spec = pl.BlockSpec((1, d), lambda i: (0, 0))
    kern = functools.partial(_pool_kernel, n_ctx_tiles=n_ctx // tp, ctx_seq_tiles=seq // tp,
                             lat_seq_tiles=dec_seq // tp)
    return pl.pallas_call(
        kern,
        grid=(t // tp,),
        in_specs=[pl.BlockSpec((tp, d), lambda i: (i, 0)),
                  pl.BlockSpec((hl, d), lambda i: (jnp.maximum(i * hb - 1, 0), 0)),
                  pl.BlockSpec((hl, d), lambda i: (jnp.minimum((i + 1) * hb, last_hblk), 0)),
                  row_spec, mod_spec, mod_spec, mod_spec,
                  pl.BlockSpec(w_pool.shape, lambda i: (0, 0, 0)),
                  row_spec],
        out_specs=pl.BlockSpec((tp, d), lambda i: (i, 0)),
        out_shape=jax.ShapeDtypeStruct((t, d), F32),
        scratch_shapes=[pltpu.VMEM((tp + 2 * hl, d), F32)],
        compiler_params=_params("arbitrary"),
        name="pool_mixer",
    )(x, x, x, g, shift, scale, gate, w_pool, pool_scale)


def _route(sel, scores):
    e, tm = sel.shape
    row = lax.broadcasted_iota(jnp.int32, (e, tm), 0)
    best = jnp.zeros((1, tm), jnp.int32)
    best_sc = None
    for gidx in range(N_EXPERT_GROUPS):
        r = [sel[gidx * EXPERTS_PER_GROUP + k:gidx * EXPERTS_PER_GROUP + k + 1, :]
             for k in range(EXPERTS_PER_GROUP)]
        top2 = None
        for a in range(EXPERTS_PER_GROUP):
            for b in range(a + 1, EXPERTS_PER_GROUP):
                s = r[a] + r[b]
                top2 = s if top2 is None else jnp.maximum(top2, s)
        if best_sc is None:
            best_sc = top2
        else:
            better = top2 > best_sc
            best = jnp.where(better, gidx, best)
            best_sc = jnp.where(better, top2, best_sc)
    neg = -jnp.inf
    masked = jnp.where(row // EXPERTS_PER_GROUP == best, sel, neg)
    m1 = jnp.max(masked, axis=0, keepdims=True)
    i1 = jnp.min(jnp.where(masked == m1, row, e), axis=0, keepdims=True)
    masked2 = jnp.where(row == i1, neg, masked)
    m2 = jnp.max(masked2, axis=0, keepdims=True)
    i2 = jnp.min(jnp.where(masked2 == m2, row, e), axis=0, keepdims=True)
    hot1 = row == i1
    hot2 = row == i2
    w1 = jnp.sum(jnp.where(hot1, scores, 0.0), axis=0, keepdims=True)
    w2 = jnp.sum(jnp.where(hot2, scores, 0.0), axis=0, keepdims=True)
    tot = w1 + w2
    return best, jnp.where(hot1, w1 / tot, 0.0) + jnp.where(hot2, w2 / tot, 0.0)


def _split_bf16(a, parts):
    out = []
    for _ in range(parts):
        p = a.astype(BF16)
        out.append(p)
        a = a - p.astype(F32)
    return out


def _pad_rows(a, rows):
    return jnp.concatenate([a, jnp.zeros((rows - a.shape[0], a.shape[1]), a.dtype)], axis=0)


def _moe_kernel(x_ref, g_ref, sh_ref, sc_ref, gt_ref, wr_ref, br_ref, wg_ref, wu_ref, wd_ref,
                gf_ref, o_ref, hp_scr, cw_scr, yp_scr, *, final_norm):
    tr, d = x_ref.shape
    trp = hp_scr.shape[0]
    ng, eg = N_EXPERT_GROUPS, EXPERTS_PER_GROUP
    x = x_ref[...]
    h = _modulated(x, g_ref[...], sh_ref[...], sc_ref[...])
    hb = h.astype(BF16)

    h_lo = (h - hb.astype(F32)).astype(BF16)
    wr = wr_ref[...]
    lg = jnp.dot(hb, wr, preferred_element_type=F32)
    logits = lg[:, :LANES] + lg[:, LANES:] + jnp.dot(h_lo, wr[:, :LANES], preferred_element_type=F32)
    scores = jax.nn.sigmoid(logits.T[:N_EXPERTS, :])
    best, comb_t = _route(scores + br_ref[...], scores)

    grp = lax.broadcasted_iota(jnp.int32, (SUBLANES, tr), 0)
    hot_t = (grp == best).astype(F32)
    cw_t = hot_t[0:1, :] * comb_t[0:eg, :]
    for gi in range(1, ng):
        cw_t = cw_t + hot_t[gi:gi + 1, :] * comb_t[gi * eg:(gi + 1) * eg, :]
    cw_c = _pad_rows(cw_t, LANES).T

    ia = lax.broadcasted_iota(jnp.int32, (tr, tr), 0)
    ib = lax.broadcasted_iota(jnp.int32, (tr, tr), 1)
    before = jnp.where(ia < ib, 1.0, 0.0).astype(BF16)
    rank_t = jnp.dot(hot_t.astype(BF16), before, preferred_element_type=F32)

    starts, counts = [], []
    off = jnp.int32(0)
    for gi in range(ng):
        n = jnp.sum(hot_t[gi:gi + 1, :]).astype(jnp.int32)
        n = ((n + MOE_SEG_ALIGN - 1) // MOE_SEG_ALIGN) * MOE_SEG_ALIGN
        starts.append(off)
        counts.append(n)
        off = off + n

    pos_t = hot_t[0:1, :] * (rank_t[0:1, :] + starts[0].astype(F32))
    for gi in range(1, ng):
        pos_t = pos_t + hot_t[gi:gi + 1, :] * (rank_t[gi:gi + 1, :] + starts[gi].astype(F32))
    pos_c = _pad_rows(pos_t, LANES).T[:, 0:1]
    used = tr + ng * MOE_SEG_ALIGN
    perm = jnp.where(lax.broadcasted_iota(jnp.int32, (used, tr), 0) == pos_t.astype(jnp.int32),
                     1.0, 0.0).astype(BF16)
    perm_t = jnp.where(lax.broadcasted_iota(jnp.int32, (tr, used), 1) == pos_c.astype(jnp.int32),
                       1.0, 0.0).astype(BF16)

    hp_scr[pl.ds(0, used), :] = jnp.dot(perm, hb, preferred_element_type=F32).astype(BF16)
    hp_scr[pl.ds(used, trp - used), :] = jnp.zeros((trp - used, d), BF16)
    cw_pair = jnp.dot(perm, jnp.concatenate(_split_bf16(cw_c, 2), axis=1), preferred_element_type=F32)
    cw_scr[pl.ds(0, used), :] = cw_pair[:, :LANES] + cw_pair[:, LANES:]
    cw_scr[pl.ds(used, trp - used), :] = jnp.zeros((trp - used, LANES), F32)
    yp_scr[...] = jnp.zeros_like(yp_scr)

    sb = MOE_ROW_BLOCK
    for gi in range(ng):
        def block(b, carry, gi=gi):
            rows = pl.ds(pl.multiple_of(starts[gi] + b * sb, MOE_SEG_ALIGN), sb)
            hblk = hp_scr[rows, :]
            cwb = cw_scr[rows, :]
            y = None
            for k in range(eg):
                e = gi * eg + k
                gate = jnp.dot(hblk, wg_ref[e], preferred_element_type=F32)
                up = jnp.dot(hblk, wu_ref[e], preferred_element_type=F32)
                hid = (_silu(gate) * up * cwb[:, k:k + 1]).astype(BF16)
                t = jnp.dot(hid, wd_ref[e], preferred_element_type=F32)
                y = t if y is None else y + t
            yp_scr[rows, :] = y
            return carry
        lax.fori_loop(0, (counts[gi] + sb - 1) // sb, block, 0)

    y_hi, y_lo = _split_bf16(yp_scr[pl.ds(0, used), :], 2)
    moe = (jnp.dot(perm_t, y_hi, preferred_element_type=F32)
           + jnp.dot(perm_t, y_lo, preferred_element_type=F32))
    out = x + gt_ref[...] * moe
    if final_norm:
        out = _rms(out, gf_ref[...])
    o_ref[...] = out


def moe_layer(x, g, shift, scale, gate, w_router_split, b_router_col, wg, wu, wd, layer, g_final,
              n_ctx, dec_seq, final_norm, tm=MOE_TILE):
    t, d = x.shape
    _, n_e, _, f = wg.shape
    trp = tm + N_EXPERT_GROUPS * MOE_SEG_ALIGN + MOE_ROW_BLOCK
    cidx = _cvec_index(n_ctx, dec_seq, tm)
    mod_spec = pl.BlockSpec((None, 1, d), lambda i: (cidx(i), 0, 0))
    row_spec = pl.BlockSpec((1, d), lambda i: (0, 0))
    once = pl.Buffered(1)
    return pl.pallas_call(
        functools.partial(_moe_kernel, final_norm=final_norm),
        grid=(t // tm,),
        in_specs=[pl.BlockSpec((tm, d), lambda i: (i, 0)),
                  row_spec, mod_spec, mod_spec, mod_spec,
                  pl.BlockSpec(w_router_split.shape, lambda i: (0, 0)),
                  pl.BlockSpec(b_router_col.shape, lambda i: (0, 0)),
                  pl.BlockSpec((None, n_e, d, f), lambda i: (layer, 0, 0, 0), pipeline_mode=once),
                  pl.BlockSpec((None, n_e, d, f), lambda i: (layer, 0, 0, 0), pipeline_mode=once),
                  pl.BlockSpec((None, n_e, f, d), lambda i: (layer, 0, 0, 0), pipeline_mode=once),
                  row_spec],
        out_specs=pl.BlockSpec((tm, d), lambda i: (i, 0)),
        out_shape=jax.ShapeDtypeStruct((t, d), F32),
        scratch_shapes=[pltpu.VMEM((trp, d), BF16), pltpu.VMEM((trp, LANES), F32),
                        pltpu.VMEM((trp, d), F32)],
        compiler_params=_params("arbitrary"),
        name="moe",
    )(x, g, shift, scale, gate, w_router_split, b_router_col, wg, wu, wd, g_final)


def _per_stream(n_ctx_tiles, ctx_ref, lat_ref, fn):
    i = pl.program_id(0)

    @pl.when(i < n_ctx_tiles)
    def _():
        fn(ctx_ref[...])

    @pl.when(i >= n_ctx_tiles)
    def _():
        fn(lat_ref[...])


def _stream_specs(tm, k, n_ctx_tiles):
    return [pl.BlockSpec((tm, k), lambda i: (jnp.minimum(i, n_ctx_tiles - 1), 0)),
            pl.BlockSpec((tm, k), lambda i: (jnp.maximum(i - n_ctx_tiles, 0), 0))]


def _resid_proj_kernel(x_ref, ac_ref, al_ref, w_ref, gt_ref, o_ref, *, n_ctx_tiles):
    def run(a):
        o_ref[...] = x_ref[...] + gt_ref[...] * _bdot(a, w_ref[...])
    _per_stream(n_ctx_tiles, ac_ref, al_ref, run)


def _mlstm_out_kernel(x_ref, hc_ref, hl_ref, og_ref, hg_ref, w_ref, gt_ref, o_ref, *, n_ctx_tiles):
    def run(hs):
        a = jax.nn.sigmoid(og_ref[...]) * (hs * hg_ref[...])
        o_ref[...] = x_ref[...] + gt_ref[...] * _bdot(a, w_ref[...])
    _per_stream(n_ctx_tiles, hc_ref, hl_ref, run)


def resid_proj(x, a_ctx, a_lat, w, gate, n_ctx, dec_seq, tm=512):
    t, d = x.shape
    k = a_ctx.shape[1]
    cidx = _cvec_index(n_ctx, dec_seq, tm)
    nct = n_ctx // tm
    return pl.pallas_call(
        functools.partial(_resid_proj_kernel, n_ctx_tiles=nct),
        grid=(t // tm,),
        in_specs=[pl.BlockSpec((tm, d), lambda i: (i, 0))] + _stream_specs(tm, k, nct) + [
            pl.BlockSpec((k, d), lambda i: (0, 0)),
            pl.BlockSpec((None, 1, d), lambda i: (cidx(i), 0, 0))],
        out_specs=pl.BlockSpec((tm, d), lambda i: (i, 0)),
        out_shape=jax.ShapeDtypeStruct((t, d), F32),
        compiler_params=_params("arbitrary"),
        name="resid_proj",
    )(x, a_ctx, a_lat, w, gate)


def mlstm_out(x, hs_ctx, hs_lat, o_gate, head_g, w, gate, n_ctx, dec_seq, tm=512):
    t, d = x.shape
    k = hs_ctx.shape[1]
    cidx = _cvec_index(n_ctx, dec_seq, tm)
    nct = n_ctx // tm
    return pl.pallas_call(
        functools.partial(_mlstm_out_kernel, n_ctx_tiles=nct),
        grid=(t // tm,),
        in_specs=[pl.BlockSpec((tm, d), lambda i: (i, 0))] + _stream_specs(tm, k, nct) + [
            pl.BlockSpec((tm, k), lambda i: (i, 0)),
            pl.BlockSpec((1, k), lambda i: (0, 0)),
            pl.BlockSpec((k, d), lambda i: (0, 0)),
            pl.BlockSpec((None, 1, d), lambda i: (cidx(i), 0, 0))],
        out_specs=pl.BlockSpec((tm, d), lambda i: (i, 0)),
        out_shape=jax.ShapeDtypeStruct((t, d), F32),
        compiler_params=_params("arbitrary"),
        name="mlstm_out",
    )(x, hs_ctx, hs_lat, o_gate, head_g, w, gate)


def _mlstm_proj_kernel(x_ref, g_ref, sh_ref, sc_ref, wqk_ref, wv_ref, wo_ref, wgt_ref, bg_ref,
                       wgtt_ref, bgt_ref, qs_ref, qkv_ref, o_ref, gc_ref, gr_ref):
    h = _modulated(x_ref[...], g_ref[...], sh_ref[...], sc_ref[...]).astype(BF16)
    nqk = wqk_ref.shape[1]
    qkv_ref[:, :nqk] = (jnp.dot(h, wqk_ref[...].astype(BF16), preferred_element_type=F32)
                        * qs_ref[...]).astype(BF16)
    qkv_ref[:, nqk:] = jnp.dot(h, wv_ref[...].astype(BF16),
                               preferred_element_type=F32).astype(BF16)
    o_ref[...] = jnp.dot(h, wo_ref[...].astype(BF16), preferred_element_type=F32)
    gc_ref[...] = jnp.dot(h, wgt_ref[...].astype(BF16), preferred_element_type=F32) + bg_ref[...]
    gr_ref[...] = _bdot_nt(wgtt_ref[...], h) + bgt_ref[...]


def mlstm_proj(x, g, shift, scale, w_in_all, layer, b_gate, n_ctx, dec_seq, tm=512):
    t, d = x.shape
    hh = MLSTM_HEADS
    hv = d
    hk = hv // 2
    ng = 4 * hh
    w_g = w_in_all[layer, :, 2 * hk + 2 * hv:]
    w_g_pad = jnp.pad(w_g, ((0, 0), (0, LANES - ng)))
    b_pad = jnp.pad(b_gate.reshape(1, ng), ((0, 0), (0, LANES - ng)))
    w_g_t = w_g.T
    b_t = b_gate.reshape(ng, 1)
    dk = hk // hh
    q_scale = jnp.concatenate([jnp.full((1, hk), dk ** -0.5, F32), jnp.ones((1, hk), F32)], axis=1)
    cidx = _cvec_index(n_ctx, dec_seq, tm)
    mod_spec = pl.BlockSpec((None, 1, d), lambda i: (cidx(i), 0, 0))
    full = lambda a: pl.BlockSpec(a.shape, lambda i: (0,) * a.ndim)
    assert 2 * hk == hv
    w_col = lambda n: pl.BlockSpec((None, d, hv), lambda i: (layer, 0, n))
    return pl.pallas_call(
        _mlstm_proj_kernel,
        grid=(t // tm,),
        in_specs=[pl.BlockSpec((tm, d), lambda i: (i, 0)), full(g), mod_spec, mod_spec,
                  w_col(0), w_col(1), w_col(2), full(w_g_pad), full(b_pad), full(w_g_t),
                  full(b_t), full(q_scale)],
        out_specs=[pl.BlockSpec((tm, 2 * hk + hv), lambda i: (i, 0)),
                   pl.BlockSpec((tm, hv), lambda i: (i, 0)),
                   pl.BlockSpec((tm, LANES), lambda i: (i, 0)),
                   pl.BlockSpec((ng, tm), lambda i: (0, i))],
        out_shape=[jax.ShapeDtypeStruct((t, 2 * hk + hv), BF16),
                   jax.ShapeDtypeStruct((t, hv), F32),
                   jax.ShapeDtypeStruct((t, LANES), F32),
                   jax.ShapeDtypeStruct((ng, t), F32)],
        compiler_params=_params("arbitrary"),
        name="mlstm_proj",
    )(x, g, shift, scale, w_in_all, w_in_all, w_in_all, w_g_pad, b_pad, w_g_t, b_t, q_scale)


def _log_sigmoid(x):
    return jnp.minimum(x, 0.0) - jnp.log(1.0 + jnp.exp(-jnp.abs(x)))


def _mlstm_chunk(q, k, v, i_col, f_col, i_row, f_row, c_st, n_st, m_st, rev):
    l = q.shape[0]
    lf_col = _log_sigmoid(f_col)
    lf_row = _log_sigmoid(f_row)
    tt = lax.broadcasted_iota(jnp.int32, (l, l), 0)
    ss = lax.broadcasted_iota(jnp.int32, (l, l), 1)
    causal = (ss >= tt) if rev else (ss <= tt)
    b_col = jnp.sum(jnp.where(causal, lf_row, 0.0), axis=1, keepdims=True)
    feeds = (tt >= ss) if rev else (tt <= ss)
    b_row = jnp.sum(jnp.where(feeds, lf_col, 0.0), axis=0, keepdims=True)
    log_d = jnp.where(causal, b_col - b_row + i_row, -jnp.inf)
    log_state = b_col + m_st
    m_t = jnp.maximum(log_state, jnp.max(log_d, axis=1, keepdims=True))
    dw = jnp.exp(log_d - m_t)
    sw = jnp.exp(log_state - m_t)
    a = _bdot_nt(q, k) * dw
    num = sw * _bdot(q, c_st) + _bdot(a, v)
    qn = jnp.sum(q.astype(F32) * n_st, axis=1, keepdims=True)
    den = sw * qn + jnp.sum(a, axis=1, keepdims=True)
    h = num / jnp.maximum(jnp.abs(den), jnp.exp(-m_t))
    b_last = b_col[0:1, :] if rev else b_col[l - 1:l, :]
    log_k = b_last - b_col + i_col
    m_new = jnp.maximum(b_last + m_st, jnp.max(log_k, axis=0, keepdims=True))
    kw = jnp.exp(log_k - m_new)
    decay = jnp.exp(b_last + m_st - m_new)
    kwk = kw * k.astype(F32)
    c_new = decay * c_st + _bdot_tn(kwk, v)
    n_new = decay * n_st + jnp.sum(kwk, axis=0, keepdims=True)
    return h, c_new, n_new, m_new


def _mlstm_scan_kernel(*refs, n_chunks, zero_init):
    if zero_init:
        q_ref, k_ref, v_ref, gc_ref, gr_ref, hs_ref, c_ref, n_ref, m_ref, hb_scr = refs
        c_ref[...] = jnp.zeros_like(c_ref)
        n_ref[...] = jnp.zeros_like(n_ref)
        m_ref[...] = jnp.zeros_like(m_ref)
    else:
        (q_ref, k_ref, v_ref, gc_ref, gr_ref, c0_ref, n0_ref, m0_ref,
         hs_ref, c_ref, n_ref, m_ref, hb_scr) = refs
        c_ref[...] = c0_ref[...]
        n_ref[...] = n0_ref[...]
        m_ref[...] = m0_ref[...]
    l = MLSTM_CHUNK
    hh = MLSTM_HEADS
    dk = q_ref.shape[1] // hh
    dv = v_ref.shape[1] // hh

    def body(c, carry):
        cr = n_chunks - 1 - c
        rows = (pl.ds(pl.multiple_of(c * l, l), l), pl.ds(pl.multiple_of(cr * l, l), l))
        grs = (gr_ref[c], gr_ref[cr])
        outs, states = ([], []), []
        for d in range(2):
            q, k, v, gc, gr = q_ref[rows[d], :], k_ref[rows[d], :], v_ref[rows[d], :], gc_ref[rows[d], :], grs[d]
            for h in range(hh):
                ci, cf = 2 * d * hh + h, (2 * d + 1) * hh + h
                o, *st = _mlstm_chunk(q[:, h * dk:(h + 1) * dk], k[:, h * dk:(h + 1) * dk],
                                      v[:, h * dv:(h + 1) * dv], gc[:, ci:ci + 1], gc[:, cf:cf + 1],
                                      gr[ci:ci + 1, :], gr[cf:cf + 1, :],
                                      c_ref[d, h], n_ref[d, h], m_ref[d, h], d == 1)
                outs[d].append(o)
                states.append((d, h, st))
        hs_ref[rows[0], :] = jnp.concatenate(outs[0], axis=1)
        hb_scr[rows[1], :] = jnp.concatenate(outs[1], axis=1)
        for d, h, (c_new, n_new, m_new) in states:
            c_ref[d, h], n_ref[d, h], m_ref[d, h] = c_new, n_new, m_new
        return carry

    lax.fori_loop(0, n_chunks, body, 0)
    for h in range(hh):
        cols = pl.ds(h * dv, dv)
        hs = hs_ref[:, cols] + hb_scr[:, cols]
        hs_ref[:, cols] = hs * lax.rsqrt(jnp.mean(hs * hs, axis=-1, keepdims=True) + NORM_EPS)


def mlstm_scan(qkv, gcol, grow, init, row_off, n_seq, seq_len):
    hh = MLSTM_HEADS
    hv = qkv.shape[1] // 2
    dv = hv // hh
    dk = dv // 2
    l = MLSTM_CHUNK
    nc = seq_len // l
    ob = row_off // seq_len
    kern = functools.partial(_mlstm_scan_kernel, n_chunks=nc, zero_init=init is None)
    st = lambda *tail: pl.BlockSpec((None, 2, hh) + tail, lambda s: (s, 0, 0) + (0,) * len(tail))
    states = [st(dk, dv), st(1, dk), st(1, 1)]
    return pl.pallas_call(
        kern,
        grid=(n_seq,),
        in_specs=[pl.BlockSpec((seq_len, hh * dk), lambda s: (ob + s, 0)),
                  pl.BlockSpec((seq_len, hh * dk), lambda s: (ob + s, 1)),
                  pl.BlockSpec((seq_len, hv), lambda s: (ob + s, 1)),
                  pl.BlockSpec((seq_len, LANES), lambda s: (ob + s, 0)),
                  pl.BlockSpec((nc, 4 * hh, l), lambda s: (ob + s, 0, 0))]
                 + ([] if init is None else states),
        out_specs=[pl.BlockSpec((seq_len, hv), lambda s: (s, 0))] + states,
        out_shape=[jax.ShapeDtypeStruct((n_seq * seq_len, hv), F32),
                   jax.ShapeDtypeStruct((n_seq, 2, hh, dk, dv), F32),
                   jax.ShapeDtypeStruct((n_seq, 2, hh, 1, dk), F32),
                   jax.ShapeDtypeStruct((n_seq, 2, hh, 1, 1), F32)],
        scratch_shapes=[pltpu.VMEM((seq_len, hv), F32)],
        compiler_params=_params("arbitrary"),
        name="mlstm_scan",
    )(qkv, qkv, qkv, gcol, grow, *(() if init is None else init))


def _mla_proj_kernel(x_ref, g_ref, sh_ref, sc_ref, win_ref, qg_ref, kvg_ref, wqb_ref, cos_ref,
                     sin_ref, qn_ref, qp_ref, ckv_ref, kpe_ref, *, q_lora, kv_lora, rope, n_heads):
    h = _modulated(x_ref[...], g_ref[...], sh_ref[...], sc_ref[...])
    proj = _bdot(h, win_ref[...])
    q_lat = proj[:, :q_lora]
    ckv_ref[...] = _rms(proj[:, q_lora:q_lora + kv_lora], kvg_ref[...])
    cos, sin = cos_ref[...], sin_ref[...]
    kpe = proj[:, q_lora + kv_lora:q_lora + kv_lora + rope]
    kpe_rot = proj[:, q_lora + kv_lora + rope:q_lora + kv_lora + 2 * rope]
    kpe_ref[...] = kpe * cos[:, :rope] + kpe_rot * sin[:, :rope]
    q = _bdot(_rms(q_lat, qg_ref[...]), wqb_ref[...])
    n_nope = qn_ref.shape[1]
    n_rope = qp_ref.shape[1]
    qn_ref[...] = q[:, :n_nope].astype(BF16)
    reps = n_rope // cos.shape[1]
    cos_h = jnp.concatenate([cos] * reps, axis=1)
    sin_h = jnp.concatenate([sin] * reps, axis=1)
    qp_ref[...] = (q[:, n_nope:n_nope + n_rope] * cos_h
                   + q[:, n_nope + n_rope:] * sin_h).astype(BF16)


def _rot_cols(w):
    n = w.shape[1]
    j = np.arange(n)
    quarter = MLA_ROPE // 4
    first = (j % (2 * quarter)) < quarter
    src = np.where(first, j + quarter, j - quarter)
    sign = np.where(first, -1.0, 1.0).astype(np.float32)
    return w[:, src] * sign


def _rope_tables(n_ctx, dec_batch, dec_seq):
    quarter = MLA_ROPE // 4
    freq = np.power(np.float32(ROPE_BASE), -np.arange(quarter, dtype=np.float32) / np.float32(quarter))
    pos = np.arange(dec_seq)
    ang_r = (pos // GRID_W).astype(np.float32)[:, None] * freq[None, :]
    ang_c = (pos % GRID_W).astype(np.float32)[:, None] * freq[None, :]
    ang = np.concatenate([ang_r, ang_r, ang_c, ang_c], axis=1).astype(np.float32)
    cos = np.concatenate([np.ones((n_ctx, MLA_ROPE), np.float32)] + [np.cos(ang)] * dec_batch, axis=0)
    sin = np.concatenate([np.zeros((n_ctx, MLA_ROPE), np.float32)] + [np.sin(ang)] * dec_batch, axis=0)
    reps = LANES // MLA_ROPE
    return (jnp.asarray(np.tile(cos, (1, reps)), F32), jnp.asarray(np.tile(sin, (1, reps)), F32))


def mla_proj(x, g, shift, scale, w_in, q_g, kv_g, w_qb, cos, sin, n_ctx, dec_seq, tm=512):
    t, d = x.shape
    hh, nope, rope = MLA_HEADS, MLA_NOPE, MLA_ROPE
    q_lora = q_g.shape[1]
    kv_lora = kv_g.shape[1]
    w_in_ext = jnp.concatenate([w_in, _rot_cols(w_in[:, q_lora + kv_lora:])], axis=1)
    w3 = w_qb.reshape(q_lora, hh, nope + rope)
    w_qn = w3[:, :, :nope].reshape(q_lora, hh * nope)
    w_qp = w3[:, :, nope:].reshape(q_lora, hh * rope)
    w_qb_ext = jnp.concatenate([w_qn, w_qp, _rot_cols(w_qp)], axis=1)
    cidx = _cvec_index(n_ctx, dec_seq, tm)
    mod_spec = pl.BlockSpec((None, 1, d), lambda i: (cidx(i), 0, 0))
    full = lambda a: pl.BlockSpec(a.shape, lambda i: (0,) * a.ndim)
    kern = functools.partial(_mla_proj_kernel, q_lora=q_lora, kv_lora=kv_lora, rope=rope, n_heads=hh)
    return pl.pallas_call(
        kern,
        grid=(t // tm,),
        in_specs=[pl.BlockSpec((tm, d), lambda i: (i, 0)), full(g), mod_spec, mod_spec,
                  full(w_in_ext), full(q_g), full(kv_g), full(w_qb_ext),
                  pl.BlockSpec((tm, LANES), lambda i: (i, 0)),
                  pl.BlockSpec((tm, LANES), lambda i: (i, 0))],
        out_specs=[pl.BlockSpec((tm, hh * nope), lambda i: (i, 0)),
                   pl.BlockSpec((tm, hh * rope), lambda i: (i, 0)),
                   pl.BlockSpec((tm, kv_lora), lambda i: (i, 0)),
                   pl.BlockSpec((tm, rope), lambda i: (i, 0))],
        out_shape=[jax.ShapeDtypeStruct((t, hh * nope), BF16),
                   jax.ShapeDtypeStruct((t, hh * rope), BF16),
                   jax.ShapeDtypeStruct((t, kv_lora), F32),
                   jax.ShapeDtypeStruct((t, rope), F32)],
        compiler_params=_params("arbitrary"),
        name="mla_proj",
    )(x, g, shift, scale, w_in_ext, q_g, kv_g, w_qb_ext, cos, sin)


def _mla_kv_kernel(ckv_ref, w_ref, kn_ref, v_ref):
    kv = _bdot(ckv_ref[...], w_ref[...])
    n = kn_ref.shape[1]
    kn_ref[...] = kv[:, :n].astype(BF16)
    v_ref[...] = kv[:, n:].astype(BF16)


def mla_kv(ckv_all, w_kvb, tm=512):
    r, kv_lora = ckv_all.shape
    hh, nope, vd = MLA_HEADS, MLA_NOPE, MLA_V
    w3 = w_kvb.reshape(kv_lora, hh, nope + vd)
    w_perm = jnp.concatenate([w3[:, :, :nope].reshape(kv_lora, hh * nope),
                              w3[:, :, nope:].reshape(kv_lora, hh * vd)], axis=1)
    return pl.pallas_call(
        _mla_kv_kernel,
        grid=(r // tm,),
        in_specs=[pl.BlockSpec((tm, kv_lora), lambda i: (i, 0)),
                  pl.BlockSpec(w_perm.shape, lambda i: (0, 0))],
        out_specs=[pl.BlockSpec((tm, hh * nope), lambda i: (i, 0)),
                   pl.BlockSpec((tm, hh * vd), lambda i: (i, 0))],
        out_shape=[jax.ShapeDtypeStruct((r, hh * nope), BF16),
                   jax.ShapeDtypeStruct((r, hh * vd), BF16)],
        compiler_params=_params("arbitrary"),
        name="mla_kv",
    )(ckv_all, w_perm)


def _attn_kernel(qn_ref, qp_ref, kn_ref, kp_ref, v_ref, o_ref, *, scale):
    kp = kp_ref[...]
    for h in range(MLA_HEADS):
        s = _bdot_nt(qn_ref[:, h * MLA_NOPE:(h + 1) * MLA_NOPE], kn_ref[:, h * MLA_NOPE:(h + 1) * MLA_NOPE])
        s = (s + _bdot_nt(qp_ref[:, h * MLA_ROPE:(h + 1) * MLA_ROPE], kp)) * scale
        e = jnp.exp(s - jnp.max(s, axis=-1, keepdims=True))
        p = e / jnp.sum(e, axis=-1, keepdims=True)
        o_ref[:, h * MLA_V:(h + 1) * MLA_V] = _bdot(p, v_ref[:, h * MLA_V:(h + 1) * MLA_V]).astype(BF16)


def mla_attention(qn, qp, kn, kp, v, q_row_off, k_row_off, n_seq, q_len, k_len):
    tq = ATTN_Q_BLOCK
    qb = q_len // tq
    q0 = q_row_off // tq
    k0 = k_row_off // k_len
    dq, dp, dv = qn.shape[1], qp.shape[1], v.shape[1]
    kern = functools.partial(_attn_kernel, scale=(MLA_NOPE + MLA_ROPE) ** -0.5)
    return pl.pallas_call(
        kern,
        grid=(n_seq, qb),
        in_specs=[pl.BlockSpec((tq, dq), lambda s, j: (q0 + s * qb + j, 0)),
                  pl.BlockSpec((tq, dp), lambda s, j: (q0 + s * qb + j, 0)),
                  pl.BlockSpec((k_len, dq), lambda s, j: (k0 + s, 0)),
                  pl.BlockSpec((k_len, MLA_ROPE), lambda s, j: (k0 + s, 0)),
                  pl.BlockSpec((k_len, dv), lambda s, j: (k0 + s, 0))],
        out_specs=pl.BlockSpec((tq, dv), lambda s, j: (s * qb + j, 0)),
        out_shape=jax.ShapeDtypeStruct((n_seq * q_len, dv), BF16),
        compiler_params=_params("arbitrary", "arbitrary"),
        name="mla_attention",
    )(qn, qp, kn, kp, v)


def kernel(x_prompt, x_sample, state_mlstm_C, state_mlstm_n, state_mlstm_m, cache_mla_ckv,
           cache_mla_kpe, c, c_ctx, w_ada, b_ada, norm_mix, norm_ffn, norm_final, w_pool,
           pool_scale, w_mlstm_in, b_mlstm_gate, mlstm_head_g, w_mlstm_out, w_mla_in, mla_q_g,
           mla_kv_g, w_mla_qb, w_mla_kvb, w_mla_out, w_router, b_router, w_exp_gate, w_exp_up,
           w_exp_down):
    batch, seq, d = x_prompt.shape
    dec_batch, dec_seq, _ = x_sample.shape
    depth = w_ada.shape[0]
    n_ctx = batch * seq
    n_lat = dec_batch * dec_seq
    hh = MLSTM_HEADS
    past = cache_mla_ckv.shape[2]

    x = jnp.concatenate([x_prompt.reshape(n_ctx, d), x_sample.reshape(n_lat, d)], axis=0)

    n_cv = 1 + dec_batch
    cvecs = jnp.concatenate([c_ctx[None, :], c, jnp.zeros((SUBLANES - n_cv % SUBLANES, d), F32)], axis=0)
    mod = ada_mod_all(cvecs, w_ada, b_ada).reshape(depth, cvecs.shape[0], 6, 1, d)

    w_router_pad = jnp.pad(w_router, ((0, 0), (0, LANES - N_EXPERTS)))
    w_router_hi = w_router_pad.astype(BF16)
    w_router_lo = (w_router_pad - w_router_hi.astype(F32)).astype(BF16)
    w_router_split = jnp.concatenate([w_router_hi, w_router_lo], axis=1)
    b_router_col = b_router.reshape(N_EXPERTS, 1)
    wg_bf, wu_bf, wd_bf = (w.astype(BF16) for w in (w_exp_gate, w_exp_up, w_exp_down))
    g_final = norm_final.reshape(1, d)
    row = lambda a: a.reshape(1, -1)

    outs = {}
    for i in range(depth):
        kind, j = i % 3, i // 3
        m = [mod[i, :n_cv, k] for k in range(6)]
        g_mix = row(norm_mix[i])
        if kind == 0:
            x = pool_layer(x, g_mix, m[0], m[1], m[2], w_pool[j], row(pool_scale[j]),
                           n_ctx, seq, dec_seq)
        elif kind == 1:
            qkv, o_gate, gcol, grow = mlstm_proj(x, g_mix, m[0], m[1], w_mlstm_in, j,
                                                 b_mlstm_gate[j], n_ctx, dec_seq)
            t = n_ctx + n_lat
            l = MLSTM_CHUNK
            grow_c = grow.reshape(4 * hh, t // l, l).transpose(1, 0, 2)
            hs_c, c_new, n_new, m_new = mlstm_scan(qkv, gcol, grow_c, None, 0, batch, seq)
            init = (state_mlstm_C[:, j], state_mlstm_n[:, j][:, :, :, None, :],
                    state_mlstm_m[:, j][:, :, :, None, None])
            hs_l, _, _, _ = mlstm_scan(qkv, gcol, grow_c, init, n_ctx, dec_batch, dec_seq)
            outs["C"] = c_new[:, None]
            outs["n"] = n_new[:, None, :, :, 0, :]
            outs["m"] = m_new[:, None, :, :, 0, 0]
            x = mlstm_out(x, hs_c, hs_l, o_gate, row(mlstm_head_g[j]), w_mlstm_out[j], m[2],
                          n_ctx, dec_seq)
        else:
            cos, sin = _rope_tables(n_ctx, dec_batch, dec_seq)
            qn, qp, ckv, kpe = mla_proj(x, g_mix, m[0], m[1], w_mla_in[j], row(mla_q_g[j]),
                                        row(mla_kv_g[j]), w_mla_qb[j], cos, sin, n_ctx, dec_seq)
            lat_parts_c, lat_parts_p = [], []
            for b in range(dec_batch):
                lo = n_ctx + b * dec_seq
                lat_parts_c += [cache_mla_ckv[b, j], ckv[lo:lo + dec_seq]]
                lat_parts_p += [cache_mla_kpe[b, j], kpe[lo:lo + dec_seq]]
            ckv_all = jnp.concatenate(lat_parts_c + [ckv[:n_ctx]], axis=0)
            kp_all = jnp.concatenate(lat_parts_p + [kpe[:n_ctx]], axis=0).astype(BF16)
            kn, v = mla_kv(ckv_all, w_mla_kvb[j])
            k_lat = past + dec_seq
            o_c = mla_attention(qn, qp, kn, kp_all, v, 0, dec_batch * k_lat, batch, seq, seq)
            o_l = mla_attention(qn, qp, kn, kp_all, v, n_ctx, 0, dec_batch, dec_seq, k_lat)
            outs["ckv"] = ckv[:n_ctx].reshape(batch, 1, seq, -1)
            outs["kpe"] = kpe[:n_ctx].reshape(batch, 1, seq, -1)
            x = resid_proj(x, o_c, o_l, w_mla_out[j], m[2], n_ctx, dec_seq)
        x = moe_layer(x, row(norm_ffn[i]), m[3], m[4], m[5], w_router_split, b_router_col,
                      wg_bf, wu_bf, wd_bf, i, g_final, n_ctx, dec_seq,
                      final_norm=(i == depth - 1))

    y_prompt = x[:n_ctx].reshape(batch, seq, d)
    y_sample = x[n_ctx:].reshape(dec_batch, dec_seq, d)
    return (y_prompt, y_sample, outs["C"], outs["n"], outs["m"], outs["ckv"], outs["kpe"])
```

```python
import functools

import numpy as np
import jax
import jax.numpy as jnp
from jax import lax
from jax.experimental import pallas as pl
from jax.experimental.pallas import tpu as pltpu

F32 = jnp.float32
BF16 = jnp.bfloat16

NORM_EPS = 1e-6
GRID_W = 64
POOL_WINDOWS = (2, 4, 8, 16)
MLSTM_HEADS = 4
MLSTM_CHUNK = 256
MLA_HEADS = 8
MLA_NOPE = 128
MLA_ROPE = 64
MLA_V = 128
ROPE_BASE = 10000.0
N_EXPERTS = 16
N_EXPERT_GROUPS = 4
EXPERTS_PER_GROUP = N_EXPERTS // N_EXPERT_GROUPS

LANES = 128
SUBLANES = 8
VMEM_LIMIT = 56 * 1024 * 1024
POOL_TILE = 256
POOL_HALO = 8
ATTN_Q_BLOCK = 256
MOE_TILE = 512
MOE_SEG_ALIGN = 16
MOE_ROW_BLOCK = 160


def _params(*sem):
    return pltpu.CompilerParams(dimension_semantics=sem, vmem_limit_bytes=VMEM_LIMIT)


def _rms(x, g):
    return x * lax.rsqrt(jnp.mean(x * x, axis=-1, keepdims=True) + NORM_EPS) * g


def _modulated(x, g, shift, scale):
    return _rms(x, g) * (1.0 + scale) + shift


def _silu(x):
    return x * jax.nn.sigmoid(x)


def _bdot(a, b):
    return jnp.dot(a.astype(BF16), b.astype(BF16), preferred_element_type=F32)


def _bdot_nt(a, b):
    return lax.dot_general(a.astype(BF16), b.astype(BF16), (((1,), (1,)), ((), ())),
                           preferred_element_type=F32)


def _bdot_tn(a, b):
    return lax.dot_general(a.astype(BF16), b.astype(BF16), (((0,), (0,)), ((), ())),
                           preferred_element_type=F32)


def _cvec_index(n_ctx, dec_seq, tm):
    def idx(i):
        r = i * tm
        return jnp.where(r < n_ctx, 0, (r - n_ctx) // dec_seq + 1)
    return idx


def _ada_kernel(c_ref, w_ref, b_ref, o_ref):
    o_ref[...] = _bdot(_silu(c_ref[...]), w_ref[...]) + b_ref[...]


def ada_mod_all(cvecs, w_ada, b_ada, tn=1536):
    depth, d, n6 = w_ada.shape
    rows = cvecs.shape[0]
    return pl.pallas_call(
        _ada_kernel,
        grid=(depth, n6 // tn),
        in_specs=[pl.BlockSpec((rows, d), lambda l, n: (0, 0)),
                  pl.BlockSpec((None, d, tn), lambda l, n: (l, 0, n)),
                  pl.BlockSpec((None, 1, tn), lambda l, n: (l, 0, n))],
        out_specs=pl.BlockSpec((None, rows, tn), lambda l, n: (l, 0, n)),
        out_shape=jax.ShapeDtypeStruct((depth, rows, n6), F32),
        compiler_params=_params("arbitrary", "arbitrary"),
        name="ada_mod",
    )(cvecs, w_ada, b_ada.reshape(depth, 1, n6))


def _pool_kernel(xc_ref, xp_ref, xn_ref, g_ref, sh_ref, sc_ref, gt_ref, wp_ref, ps_ref, o_ref,
                 buf_ref, *, n_ctx_tiles, ctx_seq_tiles, lat_seq_tiles):
    i = pl.program_id(0)
    is_ctx = i < n_ctx_tiles
    j = jnp.where(is_ctx, i % ctx_seq_tiles, (i - n_ctx_tiles) % lat_seq_tiles)
    nt = jnp.where(is_ctx, ctx_seq_tiles, lat_seq_tiles)
    g, sh, sc = g_ref[...], sh_ref[...], sc_ref[...]
    tp, hl = POOL_TILE, POOL_HALO
    gw = xc_ref.shape[1] // len(POOL_WINDOWS)

    xc = xc_ref[...]
    buf_ref[pl.ds(0, hl), :] = jnp.where(j == 0, 0.0, _modulated(xp_ref[...], g, sh, sc))
    buf_ref[pl.ds(hl, tp), :] = _modulated(xc, g, sh, sc)
    buf_ref[pl.ds(hl + tp, hl), :] = jnp.where(j == nt - 1, 0.0, _modulated(xn_ref[...], g, sh, sc))

    pos = j * tp + lax.broadcasted_iota(jnp.int32, (tp, 1), 0)
    seq_len = nt * tp
    for gi, w in enumerate(POOL_WINDOWS):
        cols = pl.ds(gi * gw, gw)
        acc = buf_ref[pl.ds(hl - w // 2, tp), cols]
        for d in range(-w // 2 + 1, w // 2):
            acc = acc + buf_ref[pl.ds(hl + d, tp), cols]
        cnt = jnp.minimum(pos + w // 2, seq_len) - jnp.maximum(pos - w // 2, 0)
        pooled = acc / cnt.astype(F32) - buf_ref[pl.ds(hl, tp), cols]
        y = _bdot(pooled, wp_ref[gi]) * ps_ref[:, cols]
        o_ref[:, cols] = xc[:, gi * gw:(gi + 1) * gw] + gt_ref[:, cols] * y


def pool_layer(x, g, shift, scale, gate, w_pool, pool_scale, n_ctx, seq, dec_seq):
    t, d = x.shape
    tp, hl = POOL_TILE, POOL_HALO
    cidx = _cvec_index(n_ctx, dec_seq, tp)
    hb = tp // hl
    last_hblk = t // hl - 1
    mod_spec = pl.BlockSpec((None, 1, d), lambda i: (cidx(i), 0, 0))
    row_spec = pl.BlockSpec((1, d), lambda i: (0, 0))
    kern = functools.partial(_pool_kernel, n_ctx_tiles=n_ctx // tp, ctx_seq_tiles=seq // tp,
                             lat_seq_tiles=dec_seq // tp)
    return pl.pallas_call(
        kern,
        grid=(t // tp,),
        in_specs=[pl.BlockSpec((tp, d), lambda i: (i, 0)),
                  pl.BlockSpec((hl, d), lambda i: (jnp.maximum(i * hb - 1, 0), 0)),
                  pl.BlockSpec((hl, d), lambda i: (jnp.minimum((i + 1) * hb, last_hblk), 0)),
                  row_spec, mod_spec, mod_spec, mod_spec,
                  pl.BlockSpec(w_pool.shape, lambda i: (0, 0, 0)),
                  row_spec],
        out_specs=pl.BlockSpec((tp, d), lambda i: (i, 0)),
        out_shape=jax.ShapeDtypeStruct((t, d), F32),
        scratch_shapes=[pltpu.VMEM((tp + 2 * hl, d), F32)],
        compiler_params=_params("arbitrary"),
        name="pool_mixer",
    )(x, x, x, g, shift, scale, gate, w_pool, pool_scale)


def _route(sel, scores):
    e, tm = sel.shape
    row = lax.broadcasted_iota(jnp.int32, (e, tm), 0)
    best = jnp.zeros((1, tm), jnp.int32)
    best_sc = None
    for gidx in range(N_EXPERT_GROUPS):
        r = [sel[gidx * EXPERTS_PER_GROUP + k:gidx * EXPERTS_PER_GROUP + k + 1, :]
             for k in range(EXPERTS_PER_GROUP)]
        top2 = None
        for a in range(EXPERTS_PER_GROUP):
            for b in range(a + 1, EXPERTS_PER_GROUP):
                s = r[a] + r[b]
                top2 = s if top2 is None else jnp.maximum(top2, s)
        if best_sc is None:
            best_sc = top2
        else:
            better = top2 > best_sc
            best = jnp.where(better, gidx, best)
            best_sc = jnp.where(better, top2, best_sc)
    neg = -jnp.inf
    masked = jnp.where(row // EXPERTS_PER_GROUP == best, sel, neg)
    m1 = jnp.max(masked, axis=0, keepdims=True)
    i1 = jnp.min(jnp.where(masked == m1, row, e), axis=0, keepdims=True)
    masked2 = jnp.where(row == i1, neg, masked)
    m2 = jnp.max(masked2, axis=0, keepdims=True)
    i2 = jnp.min(jnp.where(masked2 == m2, row, e), axis=0, keepdims=True)
    hot1 = row == i1
    hot2 = row == i2
    w1 = jnp.sum(jnp.where(hot1, scores, 0.0), axis=0, keepdims=True)
    w2 = jnp.sum(jnp.where(hot2, scores, 0.0), axis=0, keepdims=True)
    tot = w1 + w2
    return best, jnp.where(hot1, w1 / tot, 0.0) + jnp.where(hot2, w2 / tot, 0.0)


def _split_bf16(a, parts):
    out = []
    for _ in range(parts):
        p = a.astype(BF16)
        out.append(p)
        a = a - p.astype(F32)
    return out


def _pad_rows(a, rows):
    return jnp.concatenate([a, jnp.zeros((rows - a.shape[0], a.shape[1]), a.dtype)], axis=0)


def _moe_kernel(x_ref, g_ref, sh_ref, sc_ref, gt_ref, wr_ref, br_ref, wg_ref, wu_ref, wd_ref,
                gf_ref, o_ref, hp_scr, cw_scr, yp_scr, *, final_norm):
    tr, d = x_ref.shape
    trp = hp_scr.shape[0]
    ng, eg = N_EXPERT_GROUPS, EXPERTS_PER_GROUP
    x = x_ref[...]
    h = _modulated(x, g_ref[...], sh_ref[...], sc_ref[...])
    hb = h.astype(BF16)

    h_lo = (h - hb.astype(F32)).astype(BF16)
    wr = wr_ref[...]
    lg = jnp.dot(hb, wr, preferred_element_type=F32)
    logits = lg[:, :LANES] + lg[:, LANES:] + jnp.dot(h_lo, wr[:, :LANES], preferred_element_type=F32)
    scores = jax.nn.sigmoid(logits.T[:N_EXPERTS, :])
    best, comb_t = _route(scores + br_ref[...], scores)

    grp = lax.broadcasted_iota(jnp.int32, (SUBLANES, tr), 0)
    hot_t = (grp == best).astype(F32)
    cw_t = hot_t[0:1, :] * comb_t[0:eg, :]
    for gi in range(1, ng):
        cw_t = cw_t + hot_t[gi:gi + 1, :] * comb_t[gi * eg:(gi + 1) * eg, :]
    cw_c = _pad_rows(cw_t, LANES).T

    ia = lax.broadcasted_iota(jnp.int32, (tr, tr), 0)
    ib = lax.broadcasted_iota(jnp.int32, (tr, tr), 1)
    before = jnp.where(ia < ib, 1.0, 0.0).astype(BF16)
    rank_t = jnp.dot(hot_t.astype(BF16), before, preferred_element_type=F32)

    starts, counts = [], []
    off = jnp.int32(0)
    for gi in range(ng):
        n = jnp.sum(hot_t[gi:gi + 1, :]).astype(jnp.int32)
        n = ((n + MOE_SEG_ALIGN - 1) // MOE_SEG_ALIGN) * MOE_SEG_ALIGN
        starts.append(off)
        counts.append(n)
        off = off + n

    pos_t = hot_t[0:1, :] * (rank_t[0:1, :] + starts[0].astype(F32))
    for gi in range(1, ng):
        pos_t = pos_t + hot_t[gi:gi + 1, :] * (rank_t[gi:gi + 1, :] + starts[gi].astype(F32))
    pos_c = _pad_rows(pos_t, LANES).T[:, 0:1]
    used = tr + ng * MOE_SEG_ALIGN
    perm = jnp.where(lax.broadcasted_iota(jnp.int32, (used, tr), 0) == pos_t.astype(jnp.int32),
                     1.0, 0.0).astype(BF16)
    perm_t = jnp.where(lax.broadcasted_iota(jnp.int32, (tr, used), 1) == pos_c.astype(jnp.int32),
                       1.0, 0.0).astype(BF16)

    hp_scr[pl.ds(0, used), :] = jnp.dot(perm, hb, preferred_element_type=F32).astype(BF16)
    hp_scr[pl.ds(used, trp - used), :] = jnp.zeros((trp - used, d), BF16)
    cw_pair = jnp.dot(perm, jnp.concatenate(_split_bf16(cw_c, 2), axis=1), preferred_element_type=F32)
    cw_scr[pl.ds(0, used), :] = cw_pair[:, :LANES] + cw_pair[:, LANES:]
    cw_scr[pl.ds(used, trp - used), :] = jnp.zeros((trp - used, LANES), F32)
    yp_scr[...] = jnp.zeros_like(yp_scr)

    sb = MOE_ROW_BLOCK
    for gi in range(ng):
        def block(b, carry, gi=gi):
            rows = pl.ds(pl.multiple_of(starts[gi] + b * sb, MOE_SEG_ALIGN), sb)
            hblk = hp_scr[rows, :]
            cwb = cw_scr[rows, :]
            y = None
            for k in range(eg):
                e = gi * eg + k
                gate = jnp.dot(hblk, wg_ref[e], preferred_element_type=F32)
                up = jnp.dot(hblk, wu_ref[e], preferred_element_type=F32)
                hid = (_silu(gate) * up * cwb[:, k:k + 1]).astype(BF16)
                t = jnp.dot(hid, wd_ref[e], preferred_element_type=F32)
                y = t if y is None else y + t
            yp_scr[rows, :] = y
            return carry
        lax.fori_loop(0, (counts[gi] + sb - 1) // sb, block, 0)

    y_hi, y_lo = _split_bf16(yp_scr[pl.ds(0, used), :], 2)
    moe = (jnp.dot(perm_t, y_hi, preferred_element_type=F32)
           + jnp.dot(perm_t, y_lo, preferred_element_type=F32))
    out = x + gt_ref[...] * moe
    if final_norm:
        out = _rms(out, gf_ref[...])
    o_ref[...] = out


def moe_layer(x, g, shift, scale, gate, w_router_split, b_router_col, wg, wu, wd, layer, g_final,
              n_ctx, dec_seq, final_norm, tm=MOE_TILE):
    t, d = x.shape
    _, n_e, _, f = wg.shape
    trp = tm + N_EXPERT_GROUPS * MOE_SEG_ALIGN + MOE_ROW_BLOCK
    cidx = _cvec_index(n_ctx, dec_seq, tm)
    mod_spec = pl.BlockSpec((None, 1, d), lambda i: (cidx(i), 0, 0))
    row_spec = pl.BlockSpec((1, d), lambda i: (0, 0))
    once = pl.Buffered(1)
    return pl.pallas_call(
        functools.partial(_moe_kernel, final_norm=final_norm),
        grid=(t // tm,),
        in_specs=[pl.BlockSpec((tm, d), lambda i: (i, 0)),
                  row_spec, mod_spec, mod_spec, mod_spec,
                  pl.BlockSpec(w_router_split.shape, lambda i: (0, 0)),
                  pl.BlockSpec(b_router_col.shape, lambda i: (0, 0)),
                  pl.BlockSpec((None, n_e, d, f), lambda i: (layer, 0, 0, 0), pipeline_mode=once),
                  pl.BlockSpec((None, n_e, d, f), lambda i: (layer, 0, 0, 0), pipeline_mode=once),
                  pl.BlockSpec((None, n_e, f, d), lambda i: (layer, 0, 0, 0), pipeline_mode=once),
                  row_spec],
        out_specs=pl.BlockSpec((tm, d), lambda i: (i, 0)),
        out_shape=jax.ShapeDtypeStruct((t, d), F32),
        scratch_shapes=[pltpu.VMEM((trp, d), BF16), pltpu.VMEM((trp, LANES), F32),
                        pltpu.VMEM((trp, d), F32)],
        compiler_params=_params("arbitrary"),
        name="moe",
    )(x, g, shift, scale, gate, w_router_split, b_router_col, wg, wu, wd, g_final)


def _per_stream(n_ctx_tiles, ctx_ref, lat_ref, fn):
    i = pl.program_id(0)

    @pl.when(i < n_ctx_tiles)
    def _():
        fn(ctx_ref[...])

    @pl.when(i >= n_ctx_tiles)
    def _():
        fn(lat_ref[...])


def _stream_specs(tm, k, n_ctx_tiles):
    return [pl.BlockSpec((tm, k), lambda i: (jnp.minimum(i, n_ctx_tiles - 1), 0)),
            pl.BlockSpec((tm, k), lambda i: (jnp.maximum(i - n_ctx_tiles, 0), 0))]


def _resid_proj_kernel(x_ref, ac_ref, al_ref, w_ref, gt_ref, o_ref, *, n_ctx_tiles):
    def run(a):
        o_ref[...] = x_ref[...] + gt_ref[...] * _bdot(a, w_ref[...])
    _per_stream(n_ctx_tiles, ac_ref, al_ref, run)


def _mlstm_out_kernel(x_ref, hc_ref, hl_ref, og_ref, hg_ref, w_ref, gt_ref, o_ref, *, n_ctx_tiles):
    def run(hs):
        a = jax.nn.sigmoid(og_ref[...]) * (hs * hg_ref[...])
        o_ref[...] = x_ref[...] + gt_ref[...] * _bdot(a, w_ref[...])
    _per_stream(n_ctx_tiles, hc_ref, hl_ref, run)


def resid_proj(x, a_ctx, a_lat, w, gate, n_ctx, dec_seq, tm=512):
    t, d = x.shape
    k = a_ctx.shape[1]
    cidx = _cvec_index(n_ctx, dec_seq, tm)
    nct = n_ctx // tm
    return pl.pallas_call(
        functools.partial(_resid_proj_kernel, n_ctx_tiles=nct),
        grid=(t // tm,),
        in_specs=[pl.BlockSpec((tm, d), lambda i: (i, 0))] + _stream_specs(tm, k, nct) + [
            pl.BlockSpec((k, d), lambda i: (0, 0)),
            pl.BlockSpec((None, 1, d), lambda i: (cidx(i), 0, 0))],
        out_specs=pl.BlockSpec((tm, d), lambda i: (i, 0)),
        out_shape=jax.ShapeDtypeStruct((t, d), F32),
        compiler_params=_params("arbitrary"),
        name="resid_proj",
    )(x, a_ctx, a_lat, w, gate)


def mlstm_out(x, hs_ctx, hs_lat, o_gate, head_g, w, gate, n_ctx, dec_seq, tm=512):
    t, d = x.shape
    k = hs_ctx.shape[1]
    cidx = _cvec_index(n_ctx, dec_seq, tm)
    nct = n_ctx // tm
    return pl.pallas_call(
        functools.partial(_mlstm_out_kernel, n_ctx_tiles=nct),
        grid=(t // tm,),
        in_specs=[pl.BlockSpec((tm, d), lambda i: (i, 0))] + _stream_specs(tm, k, nct) + [
            pl.BlockSpec((tm, k), lambda i: (i, 0)),
            pl.BlockSpec((1, k), lambda i: (0, 0)),
            pl.BlockSpec((k, d), lambda i: (0, 0)),
            pl.BlockSpec((None, 1, d), lambda i: (cidx(i), 0, 0))],
        out_specs=pl.BlockSpec((tm, d), lambda i: (i, 0)),
        out_shape=jax.ShapeDtypeStruct((t, d), F32),
        compiler_params=_params("arbitrary"),
        name="mlstm_out",
    )(x, hs_ctx, hs_lat, o_gate, head_g, w, gate)


def _mlstm_proj_kernel(x_ref, g_ref, sh_ref, sc_ref, wqk_ref, wv_ref, wo_ref, wgt_ref, bg_ref,
                       wgtt_ref, bgt_ref, qs_ref, qkv_ref, o_ref, gc_ref, gr_ref):
    h = _modulated(x_ref[...], g_ref[...], sh_ref[...], sc_ref[...]).astype(BF16)
    nqk = wqk_ref.shape[1]
    qkv_ref[:, :nqk] = (jnp.dot(h, wqk_ref[...].astype(BF16), preferred_element_type=F32)
                        * qs_ref[...]).astype(BF16)
    qkv_ref[:, nqk:] = jnp.dot(h, wv_ref[...].astype(BF16),
                               preferred_element_type=F32).astype(BF16)
    o_ref[...] = jnp.dot(h, wo_ref[...].astype(BF16), preferred_element_type=F32)
    gc_ref[...] = jnp.dot(h, wgt_ref[...].astype(BF16), preferred_element_type=F32) + bg_ref[...]
    gr_ref[...] = _bdot_nt(wgtt_ref[...], h) + bgt_ref[...]


def mlstm_proj(x, g, shift, scale, w_in_all, layer, b_gate, n_ctx, dec_seq, tm=512):
    t, d = x.shape
    hh = MLSTM_HEADS
    hv = d
    hk = hv // 2
    ng = 4 * hh
    w_g = w_in_all[layer, :, 2 * hk + 2 * hv:]
    w_g_pad = jnp.pad(w_g, ((0, 0), (0, LANES - ng)))
    b_pad = jnp.pad(b_gate.reshape(1, ng), ((0, 0), (0, LANES - ng)))
    w_g_t = w_g.T
    b_t = b_gate.reshape(ng, 1)
    dk = hk // hh
    q_scale = jnp.concatenate([jnp.full((1, hk), dk ** -0.5, F32), jnp.ones((1, hk), F32)], axis=1)
    cidx = _cvec_index(n_ctx, dec_seq, tm)
    mod_spec = pl.BlockSpec((None, 1, d), lambda i: (cidx(i), 0, 0))
    full = lambda a: pl.BlockSpec(a.shape, lambda i: (0,) * a.ndim)
    assert 2 * hk == hv
    w_col = lambda n: pl.BlockSpec((None, d, hv), lambda i: (layer, 0, n))
    return pl.pallas_call(
        _mlstm_proj_kernel,
        grid=(t // tm,),
        in_specs=[pl.BlockSpec((tm, d), lambda i: (i, 0)), full(g), mod_spec, mod_spec,
                  w_col(0), w_col(1), w_col(2), full(w_g_pad), full(b_pad), full(w_g_t),
                  full(b_t), full(q_scale)],
        out_specs=[pl.BlockSpec((tm, 2 * hk + hv), lambda i: (i, 0)),
                   pl.BlockSpec((tm, hv), lambda i: (i, 0)),
                   pl.BlockSpec((tm, LANES), lambda i: (i, 0)),
                   pl.BlockSpec((ng, tm), lambda i: (0, i))],
        out_shape=[jax.ShapeDtypeStruct((t, 2 * hk + hv), BF16),
                   jax.ShapeDtypeStruct((t, hv), F32),
                   jax.ShapeDtypeStruct((t, LANES), F32),
                   jax.ShapeDtypeStruct((ng, t), F32)],
        compiler_params=_params("arbitrary"),
        name="mlstm_proj",
    )(x, g, shift, scale, w_in_all, w_in_all, w_in_all, w_g_pad, b_pad, w_g_t, b_t, q_scale)


def _log_sigmoid(x):
    return jnp.minimum(x, 0.0) - jnp.log(1.0 + jnp.exp(-jnp.abs(x)))


def _mlstm_chunk(q, k, v, i_col, f_col, i_row, f_row, c_st, n_st, m_st, rev):
    l = q.shape[0]
    lf_col = _log_sigmoid(f_col)
    lf_row = _log_sigmoid(f_row)
    tt = lax.broadcasted_iota(jnp.int32, (l, l), 0)
    ss = lax.broadcasted_iota(jnp.int32, (l, l), 1)
    causal = (ss >= tt) if rev else (ss <= tt)
    b_col = jnp.sum(jnp.where(causal, lf_row, 0.0), axis=1, keepdims=True)
    feeds = (tt >= ss) if rev else (tt <= ss)
    b_row = jnp.sum(jnp.where(feeds, lf_col, 0.0), axis=0, keepdims=True)
    log_d = jnp.where(causal, b_col - b_row + i_row, -jnp.inf)
    log_state = b_col + m_st
    m_t = jnp.maximum(log_state, jnp.max(log_d, axis=1, keepdims=True))
    dw = jnp.exp(log_d - m_t)
    sw = jnp.exp(log_state - m_t)
    a = _bdot_nt(q, k) * dw
    num = sw * _bdot(q, c_st) + _bdot(a, v)
    qn = jnp.sum(q.astype(F32) * n_st, axis=1, keepdims=True)
    den = sw * qn + jnp.sum(a, axis=1, keepdims=True)
    h = num / jnp.maximum(jnp.abs(den), jnp.exp(-m_t))
    b_last = b_col[0:1, :] if rev else b_col[l - 1:l, :]
    log_k = b_last - b_col + i_col
    m_new = jnp.maximum(b_last + m_st, jnp.max(log_k, axis=0, keepdims=True))
    kw = jnp.exp(log_k - m_new)
    decay = jnp.exp(b_last + m_st - m_new)
    kwk = kw * k.astype(F32)
    c_new = decay * c_st + _bdot_tn(kwk, v)
    n_new = decay * n_st + jnp.sum(kwk, axis=0, keepdims=True)
    return h, c_new, n_new, m_new


def _mlstm_scan_kernel(*refs, n_chunks, zero_init):
    if zero_init:
        q_ref, k_ref, v_ref, gc_ref, gr_ref, hs_ref, c_ref, n_ref, m_ref, hb_scr = refs
        c_ref[...] = jnp.zeros_like(c_ref)
        n_ref[...] = jnp.zeros_like(n_ref)
        m_ref[...] = jnp.zeros_like(m_ref)
    else:
        (q_ref, k_ref, v_ref, gc_ref, gr_ref, c0_ref, n0_ref, m0_ref,
         hs_ref, c_ref, n_ref, m_ref, hb_scr) = refs
        c_ref[...] = c0_ref[...]
        n_ref[...] = n0_ref[...]
        m_ref[...] = m0_ref[...]
    l = MLSTM_CHUNK
    hh = MLSTM_HEADS
    dk = q_ref.shape[1] // hh
    dv = v_ref.shape[1] // hh

    def body(c, carry):
        cr = n_chunks - 1 - c
        rows = (pl.ds(pl.multiple_of(c * l, l), l), pl.ds(pl.multiple_of(cr * l, l), l))
        grs = (gr_ref[c], gr_ref[cr])
        outs, states = ([], []), []
        for d in range(2):
            q, k, v, gc, gr = q_ref[rows[d], :], k_ref[rows[d], :], v_ref[rows[d], :], gc_ref[rows[d], :], grs[d]
            for h in range(hh):
                ci, cf = 2 * d * hh + h, (2 * d + 1) * hh + h
                o, *st = _mlstm_chunk(q[:, h * dk:(h + 1) * dk], k[:, h * dk:(h + 1) * dk],
                                      v[:, h * dv:(h + 1) * dv], gc[:, ci:ci + 1], gc[:, cf:cf + 1],
                                      gr[ci:ci + 1, :], gr[cf:cf + 1, :],
                                      c_ref[d, h], n_ref[d, h], m_ref[d, h], d == 1)
                outs[d].append(o)
                states.append((d, h, st))
        hs_ref[rows[0], :] = jnp.concatenate(outs[0], axis=1)
        hb_scr[rows[1], :] = jnp.concatenate(outs[1], axis=1)
        for d, h, (c_new, n_new, m_new) in states:
            c_ref[d, h], n_ref[d, h], m_ref[d, h] = c_new, n_new, m_new
        return carry

    lax.fori_loop(0, n_chunks, body, 0)
    for h in range(hh):
        cols = pl.ds(h * dv, dv)
        hs = hs_ref[:, cols] + hb_scr[:, cols]
        hs_ref[:, cols] = hs * lax.rsqrt(jnp.mean(hs * hs, axis=-1, keepdims=True) + NORM_EPS)


def mlstm_scan(qkv, gcol, grow, init, row_off, n_seq, seq_len):
    hh = MLSTM_HEADS
    hv = qkv.shape[1] // 2
    dv = hv // hh
    dk = dv // 2
    l = MLSTM_CHUNK
    nc = seq_len // l
    ob = row_off // seq_len
    kern = functools.partial(_mlstm_scan_kernel, n_chunks=nc, zero_init=init is None)
    st = lambda *tail: pl.BlockSpec((None, 2, hh) + tail, lambda s: (s, 0, 0) + (0,) * len(tail))
    states = [st(dk, dv), st(1, dk), st(1, 1)]
    return pl.pallas_call(
        kern,
        grid=(n_seq,),
        in_specs=[pl.BlockSpec((seq_len, hh * dk), lambda s: (ob + s, 0)),
                  pl.BlockSpec((seq_len, hh * dk), lambda s: (ob + s, 1)),
                  pl.BlockSpec((seq_len, hv), lambda s: (ob + s, 1)),
                  pl.BlockSpec((seq_len, LANES), lambda s: (ob + s, 0)),
                  pl.BlockSpec((nc, 4 * hh, l), lambda s: (ob + s, 0, 0))]
                 + ([] if init is None else states),
        out_specs=[pl.BlockSpec((seq_len, hv), lambda s: (s, 0))] + states,
        out_shape=[jax.ShapeDtypeStruct((n_seq * seq_len, hv), F32),
                   jax.ShapeDtypeStruct((n_seq, 2, hh, dk, dv), F32),
                   jax.ShapeDtypeStruct((n_seq, 2, hh, 1, dk), F32),
                   jax.ShapeDtypeStruct((n_seq, 2, hh, 1, 1), F32)],
        scratch_shapes=[pltpu.VMEM((seq_len, hv), F32)],
        compiler_params=_params("arbitrary"),
        name="mlstm_scan",
    )(qkv, qkv, qkv, gcol, grow, *(() if init is None else init))


def _mla_proj_kernel(x_ref, g_ref, sh_ref, sc_ref, win_ref, qg_ref, kvg_ref, wqb_ref, cos_ref,
                     sin_ref, q_ref, ckv_ref, kpe_ref, *, q_lora, kv_lora, rope, n_heads):
    h = _modulated(x_ref[...], g_ref[...], sh_ref[...], sc_ref[...])
    proj = _bdot(h, win_ref[...])
    q_lat = proj[:, :q_lora]
    ckv_ref[...] = _rms(proj[:, q_lora:q_lora + kv_lora], kvg_ref[...])
    cos, sin = cos_ref[...], sin_ref[...]
    kpe = proj[:, q_lora + kv_lora:q_lora + kv_lora + rope]
    kpe_rot = proj[:, q_lora + kv_lora + rope:q_lora + kv_lora + 2 * rope]
    kpe_ref[...] = kpe * cos[:, :rope] + kpe_rot * sin[:, :rope]
    q = _bdot(_rms(q_lat, qg_ref[...]), wqb_ref[...])
    n = n_heads * LANES
    pieces = []
    for hd in range(n_heads):
        c = slice(hd * LANES, (hd + 1) * LANES)
        pieces.append(q[:, c])
        pieces.append(q[:, n:2 * n][:, c] * cos + q[:, 2 * n:][:, c] * sin)
    q_ref[...] = jnp.concatenate(pieces, axis=1).astype(BF16)


def _rot_cols(w):
    n = w.shape[1]
    j = np.arange(n)
    quarter = MLA_ROPE // 4
    first = (j % (2 * quarter)) < quarter
    src = np.where(first, j + quarter, j - quarter)
    sign = np.where(first, -1.0, 1.0).astype(np.float32)
    return w[:, src] * sign


def _rope_tables(n_ctx, dec_batch, dec_seq):
    quarter = MLA_ROPE // 4
    freq = np.power(np.float32(ROPE_BASE), -np.arange(quarter, dtype=np.float32) / np.float32(quarter))
    pos = np.arange(dec_seq)
    ang_r = (pos // GRID_W).astype(np.float32)[:, None] * freq[None, :]
    ang_c = (pos % GRID_W).astype(np.float32)[:, None] * freq[None, :]
    ang = np.concatenate([ang_r, ang_r, ang_c, ang_c], axis=1).astype(np.float32)
    cos = np.concatenate([np.ones((n_ctx, MLA_ROPE), np.float32)] + [np.cos(ang)] * dec_batch, axis=0)
    sin = np.concatenate([np.zeros((n_ctx, MLA_ROPE), np.float32)] + [np.sin(ang)] * dec_batch, axis=0)
    reps = LANES // MLA_ROPE
    return (jnp.asarray(np.tile(cos, (1, reps)), F32), jnp.asarray(np.tile(sin, (1, reps)), F32))


def mla_proj(x, g, shift, scale, w_in, q_g, kv_g, w_qb, cos, sin, n_ctx, dec_seq, tm=512):
    t, d = x.shape
    hh, nope, rope = MLA_HEADS, MLA_NOPE, MLA_ROPE
    q_lora = q_g.shape[1]
    kv_lora = kv_g.shape[1]
    w_in_ext = jnp.concatenate([w_in, _rot_cols(w_in[:, q_lora + kv_lora:])], axis=1)
    w3 = w_qb.reshape(q_lora, hh, nope + rope)
    w_qn = w3[:, :, :nope].reshape(q_lora, hh * nope)
    w_qp = w3[:, :, nope:].reshape(q_lora, hh * rope)
    assert nope == LANES and rope <= LANES
    lane_pad = lambda w: jnp.pad(w.reshape(q_lora, hh, rope),
                                 ((0, 0), (0, 0), (0, LANES - rope))).reshape(q_lora, hh * LANES)
    w_qb_ext = jnp.concatenate([w_qn, lane_pad(w_qp), lane_pad(_rot_cols(w_qp))], axis=1)
    cidx = _cvec_index(n_ctx, dec_seq, tm)
    mod_spec = pl.BlockSpec((None, 1, d), lambda i: (cidx(i), 0, 0))
    full = lambda a: pl.BlockSpec(a.shape, lambda i: (0,) * a.ndim)
    kern = functools.partial(_mla_proj_kernel, q_lora=q_lora, kv_lora=kv_lora, rope=rope, n_heads=hh)
    return pl.pallas_call(
        kern,
        grid=(t // tm,),
        in_specs=[pl.BlockSpec((tm, d), lambda i: (i, 0)), full(g), mod_spec, mod_spec,
                  full(w_in_ext), full(q_g), full(kv_g), full(w_qb_ext),
                  pl.BlockSpec((tm, LANES), lambda i: (i, 0)),
                  pl.BlockSpec((tm, LANES), lambda i: (i, 0))],
        out_specs=[pl.BlockSpec((tm, 2 * hh * LANES), lambda i: (i, 0)),
                   pl.BlockSpec((tm, kv_lora), lambda i: (i, 0)),
                   pl.BlockSpec((tm, rope), lambda i: (i, 0))],
        out_shape=[jax.ShapeDtypeStruct((t, 2 * hh * LANES), BF16),
                   jax.ShapeDtypeStruct((t, kv_lora), F32),
                   jax.ShapeDtypeStruct((t, rope), F32)],
        compiler_params=_params("arbitrary"),
        name="mla_proj",
    )(x, g, shift, scale, w_in_ext, q_g, kv_g, w_qb_ext, cos, sin)


def _mla_kv_kernel(ckv_ref, kp_ref, w_ref, k_ref, v_ref):
    kv = _bdot(ckv_ref[...], w_ref[...])
    n = v_ref.shape[1]
    kp = kp_ref[...]
    pieces = []
    for hd in range(n // LANES):
        pieces += [kv[:, hd * LANES:(hd + 1) * LANES].astype(BF16), kp]
    k_ref[...] = jnp.concatenate(pieces, axis=1)
    v_ref[...] = kv[:, n:].astype(BF16)


def mla_kv(ckv_all, kp_pad, w_kvb, tm=512):
    r, kv_lora = ckv_all.shape
    hh, nope, vd = MLA_HEADS, MLA_NOPE, MLA_V
    w3 = w_kvb.reshape(kv_lora, hh, nope + vd)
    w_perm = jnp.concatenate([w3[:, :, :nope].reshape(kv_lora, hh * nope),
                              w3[:, :, nope:].reshape(kv_lora, hh * vd)], axis=1)
    return pl.pallas_call(
        _mla_kv_kernel,
        grid=(r // tm,),
        in_specs=[pl.BlockSpec((tm, kv_lora), lambda i: (i, 0)),
                  pl.BlockSpec((tm, LANES), lambda i: (i, 0)),
                  pl.BlockSpec(w_perm.shape, lambda i: (0, 0))],
        out_specs=[pl.BlockSpec((tm, 2 * hh * LANES), lambda i: (i, 0)),
                   pl.BlockSpec((tm, hh * vd), lambda i: (i, 0))],
        out_shape=[jax.ShapeDtypeStruct((r, 2 * hh * LANES), BF16),
                   jax.ShapeDtypeStruct((r, hh * vd), BF16)],
        compiler_params=_params("arbitrary"),
        name="mla_kv",
    )(ckv_all, kp_pad, w_perm)


def _attn_kernel(q_ref, k_ref, v_ref, o_ref, *, scale):
    c = scale * np.log2(np.e)
    for h in range(MLA_HEADS):
        hk = slice(2 * h * LANES, 2 * (h + 1) * LANES)
        s = _bdot_nt(q_ref[:, hk], k_ref[:, hk])
        e = jnp.exp2((s - jnp.max(s, axis=-1, keepdims=True)) * c)
        o = _bdot(e, v_ref[:, h * MLA_V:(h + 1) * MLA_V]) / jnp.sum(e, axis=-1, keepdims=True)
        o_ref[:, h * MLA_V:(h + 1) * MLA_V] = o.astype(BF16)


def mla_attention(q, k, v, q_row_off, k_row_off, n_seq, q_len, k_len):
    tq = ATTN_Q_BLOCK
    qb = q_len // tq
    q0 = q_row_off // tq
    k0 = k_row_off // k_len
    dq, dv = q.shape[1], v.shape[1]
    kern = functools.partial(_attn_kernel, scale=(MLA_NOPE + MLA_ROPE) ** -0.5)
    return pl.pallas_call(
        kern,
        grid=(n_seq, qb),
        in_specs=[pl.BlockSpec((tq, dq), lambda s, j: (q0 + s * qb + j, 0)),
                  pl.BlockSpec((k_len, dq), lambda s, j: (k0 + s, 0)),
                  pl.BlockSpec((k_len, dv), lambda s, j: (k0 + s, 0))],
        out_specs=pl.BlockSpec((tq, dv), lambda s, j: (s * qb + j, 0)),
        out_shape=jax.ShapeDtypeStruct((n_seq * q_len, dv), BF16),
        compiler_params=_params("arbitrary", "arbitrary"),
        name="mla_attention",
    )(q, k, v)


def kernel(x_prompt, x_sample, state_mlstm_C, state_mlstm_n, state_mlstm_m, cache_mla_ckv,
           cache_mla_kpe, c, c_ctx, w_ada, b_ada, norm_mix, norm_ffn, norm_final, w_pool,
           pool_scale, w_mlstm_in, b_mlstm_gate, mlstm_head_g, w_mlstm_out, w_mla_in, mla_q_g,
           mla_kv_g, w_mla_qb, w_mla_kvb, w_mla_out, w_router, b_router, w_exp_gate, w_exp_up,
           w_exp_down):
    batch, seq, d = x_prompt.shape
    dec_batch, dec_seq, _ = x_sample.shape
    depth = w_ada.shape[0]
    n_ctx = batch * seq
    n_lat = dec_batch * dec_seq
    hh = MLSTM_HEADS
    past = cache_mla_ckv.shape[2]

    x = jnp.concatenate([x_prompt.reshape(n_ctx, d), x_sample.reshape(n_lat, d)], axis=0)

    n_cv = 1 + dec_batch
    cvecs = jnp.concatenate([c_ctx[None, :], c, jnp.zeros((SUBLANES - n_cv % SUBLANES, d), F32)], axis=0)
    mod = ada_mod_all(cvecs, w_ada, b_ada).reshape(depth, cvecs.shape[0], 6, 1, d)

    w_router_pad = jnp.pad(w_router, ((0, 0), (0, LANES - N_EXPERTS)))
    w_router_hi = w_router_pad.astype(BF16)
    w_router_lo = (w_router_pad - w_router_hi.astype(F32)).astype(BF16)
    w_router_split = jnp.concatenate([w_router_hi, w_router_lo], axis=1)
    b_router_col = b_router.reshape(N_EXPERTS, 1)
    wg_bf, wu_bf, wd_bf = (w.astype(BF16) for w in (w_exp_gate, w_exp_up, w_exp_down))
    g_final = norm_final.reshape(1, d)
    row = lambda a: a.reshape(1, -1)

    outs = {}
    for i in range(depth):
        kind, j = i % 3, i // 3
        m = [mod[i, :n_cv, k] for k in range(6)]
        g_mix = row(norm_mix[i])
        if kind == 0:
            x = pool_layer(x, g_mix, m[0], m[1], m[2], w_pool[j], row(pool_scale[j]),
                           n_ctx, seq, dec_seq)
        elif kind == 1:
            qkv, o_gate, gcol, grow = mlstm_proj(x, g_mix, m[0], m[1], w_mlstm_in, j,
                                                 b_mlstm_gate[j], n_ctx, dec_seq)
            t = n_ctx + n_lat
            l = MLSTM_CHUNK
            grow_c = grow.reshape(4 * hh, t // l, l).transpose(1, 0, 2)
            hs_c, c_new, n_new, m_new = mlstm_scan(qkv, gcol, grow_c, None, 0, batch, seq)
            init = (state_mlstm_C[:, j], state_mlstm_n[:, j][:, :, :, None, :],
                    state_mlstm_m[:, j][:, :, :, None, None])
            hs_l, _, _, _ = mlstm_scan(qkv, gcol, grow_c, init, n_ctx, dec_batch, dec_seq)
            outs["C"] = c_new[:, None]
            outs["n"] = n_new[:, None, :, :, 0, :]
            outs["m"] = m_new[:, None, :, :, 0, 0]
            x = mlstm_out(x, hs_c, hs_l, o_gate, row(mlstm_head_g[j]), w_mlstm_out[j], m[2],
                          n_ctx, dec_seq)
        else:
            cos, sin = _rope_tables(n_ctx, dec_batch, dec_seq)
            q_cat, ckv, kpe = mla_proj(x, g_mix, m[0], m[1], w_mla_in[j], row(mla_q_g[j]),
                                       row(mla_kv_g[j]), w_mla_qb[j], cos, sin, n_ctx, dec_seq)
            lat_parts_c, lat_parts_p = [], []
            for b in range(dec_batch):
                lo = n_ctx + b * dec_seq
                lat_parts_c += [cache_mla_ckv[b, j], ckv[lo:lo + dec_seq]]
                lat_parts_p += [cache_mla_kpe[b, j], kpe[lo:lo + dec_seq]]
            ckv_all = jnp.concatenate(lat_parts_c + [ckv[:n_ctx]], axis=0)
            kp_all = jnp.concatenate(lat_parts_p + [kpe[:n_ctx]], axis=0).astype(BF16)
            kp_pad = jnp.pad(kp_all, ((0, 0), (0, LANES - kp_all.shape[1])))
            k_cat, v = mla_kv(ckv_all, kp_pad, w_mla_kvb[j])
            k_lat = past + dec_seq
            o_c = mla_attention(q_cat, k_cat, v, 0, dec_batch * k_lat, batch, seq, seq)
            o_l = mla_attention(q_cat, k_cat, v, n_ctx, 0, dec_batch, dec_seq, k_lat)
            outs["ckv"] = ckv[:n_ctx].reshape(batch, 1, seq, -1)
            outs["kpe"] = kpe[:n_ctx].reshape(batch, 1, seq, -1)
            x = resid_proj(x, o_c, o_l, w_mla_out[j], m[2], n_ctx, dec_seq)
        x = moe_layer(x, row(norm_ffn[i]), m[3], m[4], m[5], w_router_split, b_router_col,
                      wg_bf, wu_bf, wd_bf, i, g_final, n_ctx, dec_seq,
                      final_norm=(i == depth - 1))

    y_prompt = x[:n_ctx].reshape(batch, seq, d)
    y_sample = x[n_ctx:].reshape(dec_batch, dec_seq, d)
    return (y_prompt, y_sample, outs["C"], outs["n"], outs["m"], outs["ckv"], outs["kpe"])
```

```python
import functools

import numpy as np
import jax
import jax.numpy as jnp
from jax import lax
from jax.experimental import pallas as pl
from jax.experimental.pallas import tpu as pltpu

F32 = jnp.float32
BF16 = jnp.bfloat16

NORM_EPS = 1e-6
GRID_W = 64
POOL_WINDOWS = (2, 4, 8, 16)
MLSTM_HEADS = 4
MLSTM_CHUNK = 256
MLA_HEADS = 8
MLA_NOPE = 128
MLA_ROPE = 64
MLA_V = 128
ROPE_BASE = 10000.0
N_EXPERTS = 16
N_EXPERT_GROUPS = 4
EXPERTS_PER_GROUP = N_EXPERTS // N_EXPERT_GROUPS

LANES = 128
SUBLANES = 8
VMEM_LIMIT = 56 * 1024 * 1024
POOL_TILE = 256
POOL_HALO = 8
ATTN_Q_BLOCK = 256
MOE_TILE = 512
MOE_SEG_ALIGN = 16
MOE_ROW_BLOCK = 160
MOE_WIDE_BLOCK = 224


def _params(*sem):
    return pltpu.CompilerParams(dimension_semantics=sem, vmem_limit_bytes=VMEM_LIMIT)


def _rms(x, g):
    return x * lax.rsqrt(jnp.mean(x * x, axis=-1, keepdims=True) + NORM_EPS) * g


def _modulated(x, g, shift, scale):
    return _rms(x, g) * (1.0 + scale) + shift


def _silu(x):
    return x * jax.nn.sigmoid(x)


def _bdot(a, b):
    return jnp.dot(a.astype(BF16), b.astype(BF16), preferred_element_type=F32)


def _bdot_nt(a, b):
    return lax.dot_general(a.astype(BF16), b.astype(BF16), (((1,), (1,)), ((), ())),
                           preferred_element_type=F32)


def _bdot_tn(a, b):
    return lax.dot_general(a.astype(BF16), b.astype(BF16), (((0,), (0,)), ((), ())),
                           preferred_element_type=F32)


def _cvec_index(n_ctx, dec_seq, tm):
    def idx(i):
        r = i * tm
        return jnp.where(r < n_ctx, 0, (r - n_ctx) // dec_seq + 1)
    return idx


def _ada_kernel(c_ref, w_ref, b_ref, o_ref):
    o_ref[...] = _bdot(_silu(c_ref[...]), w_ref[...]) + b_ref[...]


def ada_mod_all(cvecs, w_ada, b_ada, tn=1536):
    depth, d, n6 = w_ada.shape
    rows = cvecs.shape[0]
    return pl.pallas_call(
        _ada_kernel,
        grid=(depth, n6 // tn),
        in_specs=[pl.BlockSpec((rows, d), lambda l, n: (0, 0)),
                  pl.BlockSpec((None, d, tn), lambda l, n: (l, 0, n)),
                  pl.BlockSpec((None, 1, tn), lambda l, n: (l, 0, n))],
        out_specs=pl.BlockSpec((None, rows, tn), lambda l, n: (l, 0, n)),
        out_shape=jax.ShapeDtypeStruct((depth, rows, n6), F32),
        compiler_params=_params("arbitrary", "arbitrary"),
        name="ada_mod",
    )(cvecs, w_ada, b_ada.reshape(depth, 1, n6))


def _pool_kernel(*refs, n_ctx_tiles, ctx_seq_tiles, lat_seq_tiles, split_input):
    n_x = 6 if split_input else 3
    x_refs = refs[:n_x]
    g_ref, sh_ref, sc_ref, gt_ref, wp_ref, ps_ref, o_ref, buf_ref = refs[n_x:]
    i = pl.program_id(0)
    is_ctx = i < n_ctx_tiles
    j = jnp.where(is_ctx, i % ctx_seq_tiles, (i - n_ctx_tiles) % lat_seq_tiles)
    nt = jnp.where(is_ctx, ctx_seq_tiles, lat_seq_tiles)
    g, sh, sc = g_ref[...], sh_ref[...], sc_ref[...]
    tp, hl = POOL_TILE, POOL_HALO
    gw = o_ref.shape[1] // len(POOL_WINDOWS)

    def fill(xc_ref, xp_ref, xn_ref):
        buf_ref[pl.ds(0, hl), :] = jnp.where(j == 0, 0.0, _modulated(xp_ref[...], g, sh, sc))
        buf_ref[pl.ds(hl, tp), :] = _modulated(xc_ref[...], g, sh, sc)
        buf_ref[pl.ds(hl + tp, hl), :] = jnp.where(j == nt - 1, 0.0,
                                                    _modulated(xn_ref[...], g, sh, sc))
        o_ref[...] = xc_ref[...]

    if split_input:
        pl.when(is_ctx)(lambda: fill(*x_refs[:3]))
        pl.when(jnp.logical_not(is_ctx))(lambda: fill(*x_refs[3:]))
    else:
        fill(*x_refs)

    pos = j * tp + lax.broadcasted_iota(jnp.int32, (tp, 1), 0)
    seq_len = nt * tp
    for gi, w in enumerate(POOL_WINDOWS):
        cols = pl.ds(gi * gw, gw)
        acc = buf_ref[pl.ds(hl - w // 2, tp), cols]
        for d in range(-w // 2 + 1, w // 2):
            acc = acc + buf_ref[pl.ds(hl + d, tp), cols]
        cnt = jnp.minimum(pos + w // 2, seq_len) - jnp.maximum(pos - w // 2, 0)
        pooled = acc / cnt.astype(F32) - buf_ref[pl.ds(hl, tp), cols]
        y = _bdot(pooled, wp_ref[gi]) * ps_ref[:, cols]
        o_ref[:, cols] = o_ref[:, cols] + gt_ref[:, cols] * y


def _halo_specs(tp, hl, d, tile_off, n_rows):
    hb = tp // hl
    last_tile, last_hblk = n_rows // tp - 1, n_rows // hl - 1
    tile = lambda i: jnp.clip(i - tile_off, 0, last_tile)
    return [pl.BlockSpec((tp, d), lambda i: (tile(i), 0)),
            pl.BlockSpec((hl, d), lambda i: (jnp.clip(tile(i) * hb - 1, 0, last_hblk), 0)),
            pl.BlockSpec((hl, d), lambda i: (jnp.clip((tile(i) + 1) * hb, 0, last_hblk), 0))]


def pool_layer(xs, g, shift, scale, gate, w_pool, pool_scale, n_ctx, seq, dec_seq):
    split = isinstance(xs, tuple)
    tp, hl = POOL_TILE, POOL_HALO
    if split:
        d = xs[0].shape[1]
        t = xs[0].shape[0] + xs[1].shape[0]
        x_specs = (_halo_specs(tp, hl, d, 0, xs[0].shape[0])
                   + _halo_specs(tp, hl, d, n_ctx // tp, xs[1].shape[0]))
        x_args = (xs[0],) * 3 + (xs[1],) * 3
    else:
        t, d = xs.shape
        x_specs = _halo_specs(tp, hl, d, 0, t)
        x_args = (xs,) * 3
    cidx = _cvec_index(n_ctx, dec_seq, tp)
    mod_spec = pl.BlockSpec((None, 1, d), lambda i: (cidx(i), 0, 0))
    row_spec = pl.BlockSpec((1, d), lambda i: (0, 0))
    kern = functools.partial(_pool_kernel, n_ctx_tiles=n_ctx // tp, ctx_seq_tiles=seq // tp,
                             lat_seq_tiles=dec_seq // tp, split_input=split)
    return pl.pallas_call(
        kern,
        grid=(t // tp,),
        in_specs=x_specs + [row_spec, mod_spec, mod_spec, mod_spec,
                            pl.BlockSpec(w_pool.shape, lambda i: (0, 0, 0)), row_spec],
        out_specs=pl.BlockSpec((tp, d), lambda i: (i, 0)),
        out_shape=jax.ShapeDtypeStruct((t, d), F32),
        scratch_shapes=[pltpu.VMEM((tp + 2 * hl, d), F32)],
        compiler_params=_params("arbitrary"),
        name="pool_mixer",
    )(*x_args, g, shift, scale, gate, w_pool, pool_scale)


def _route(sel, scores):
    e, tm = sel.shape
    row = lax.broadcasted_iota(jnp.int32, (e, tm), 0)
    best = jnp.zeros((1, tm), jnp.int32)
    best_sc = None
    for gidx in range(N_EXPERT_GROUPS):
        r = [sel[gidx * EXPERTS_PER_GROUP + k:gidx * EXPERTS_PER_GROUP + k + 1, :]
             for k in range(EXPERTS_PER_GROUP)]
        top2 = None
        for a in range(EXPERTS_PER_GROUP):
            for b in range(a + 1, EXPERTS_PER_GROUP):
                s = r[a] + r[b]
                top2 = s if top2 is None else jnp.maximum(top2, s)
        if best_sc is None:
            best_sc = top2
        else:
            better = top2 > best_sc
            best = jnp.where(better, gidx, best)
            best_sc = jnp.where(better, top2, best_sc)
    neg = -jnp.inf
    masked = jnp.where(row // EXPERTS_PER_GROUP == best, sel, neg)
    m1 = jnp.max(masked, axis=0, keepdims=True)
    i1 = jnp.min(jnp.where(masked == m1, row, e), axis=0, keepdims=True)
    masked2 = jnp.where(row == i1, neg, masked)
    m2 = jnp.max(masked2, axis=0, keepdims=True)
    i2 = jnp.min(jnp.where(masked2 == m2, row, e), axis=0, keepdims=True)
    hot1 = row == i1
    hot2 = row == i2
    w1 = jnp.sum(jnp.where(hot1, scores, 0.0), axis=0, keepdims=True)
    w2 = jnp.sum(jnp.where(hot2, scores, 0.0), axis=0, keepdims=True)
    tot = w1 + w2
    return best, jnp.where(hot1, w1 / tot, 0.0) + jnp.where(hot2, w2 / tot, 0.0)


def _split_bf16(a, parts):
    out = []
    for _ in range(parts):
        p = a.astype(BF16)
        out.append(p)
        a = a - p.astype(F32)
    return out


def _pad_rows(a, rows):
    return jnp.concatenate([a, jnp.zeros((rows - a.shape[0], a.shape[1]), a.dtype)], axis=0)


def _moe_kernel(x_ref, g_ref, sh_ref, sc_ref, gt_ref, wr_ref, br_ref, wg_ref, wu_ref, wd_ref,
                gf_ref, *rest, final_norm, n_ctx_tiles):
    out_refs, (hp_scr, cw_scr, yp_scr) = rest[:-3], rest[-3:]
    tr, d = x_ref.shape
    trp = hp_scr.shape[0]
    ng, eg = N_EXPERT_GROUPS, EXPERTS_PER_GROUP
    x = x_ref[...]
    h = _modulated(x, g_ref[...], sh_ref[...], sc_ref[...])
    hb = h.astype(BF16)

    h_lo = (h - hb.astype(F32)).astype(BF16)
    wr = wr_ref[...]
    lg = jnp.dot(hb, wr, preferred_element_type=F32)
    logits = lg[:, :LANES] + lg[:, LANES:] + jnp.dot(h_lo, wr[:, :LANES], preferred_element_type=F32)
    scores = jax.nn.sigmoid(logits.T[:N_EXPERTS, :])
    best, comb_t = _route(scores + br_ref[...], scores)

    grp = lax.broadcasted_iota(jnp.int32, (SUBLANES, tr), 0)
    hot_t = (grp == best).astype(F32)
    cw_t = hot_t[0:1, :] * comb_t[0:eg, :]
    for gi in range(1, ng):
        cw_t = cw_t + hot_t[gi:gi + 1, :] * comb_t[gi * eg:(gi + 1) * eg, :]
    cw_c = _pad_rows(cw_t, LANES).T

    ia = lax.broadcasted_iota(jnp.int32, (tr, tr), 0)
    ib = lax.broadcasted_iota(jnp.int32, (tr, tr), 1)
    before = jnp.where(ia < ib, 1.0, 0.0).astype(BF16)
    rank_t = jnp.dot(hot_t.astype(BF16), before, preferred_element_type=F32)

    starts, counts = [], []
    off = jnp.int32(0)
    for gi in range(ng):
        n = jnp.sum(hot_t[gi:gi + 1, :]).astype(jnp.int32)
        n = ((n + MOE_SEG_ALIGN - 1) // MOE_SEG_ALIGN) * MOE_SEG_ALIGN
        starts.append(off)
        counts.append(n)
        off = off + n

    pos_t = hot_t[0:1, :] * (rank_t[0:1, :] + starts[0].astype(F32))
    for gi in range(1, ng):
        pos_t = pos_t + hot_t[gi:gi + 1, :] * (rank_t[gi:gi + 1, :] + starts[gi].astype(F32))
    pos_c = _pad_rows(pos_t, LANES).T[:, 0:1]
    used = tr + ng * MOE_SEG_ALIGN
    perm = jnp.where(lax.broadcasted_iota(jnp.int32, (used, tr), 0) == pos_t.astype(jnp.int32),
                     1.0, 0.0).astype(BF16)
    perm_t = jnp.where(lax.broadcasted_iota(jnp.int32, (tr, used), 1) == pos_c.astype(jnp.int32),
                       1.0, 0.0).astype(BF16)

    hp_scr[pl.ds(0, used), :] = jnp.dot(perm, hb, preferred_element_type=F32).astype(BF16)
    hp_scr[pl.ds(used, trp - used), :] = jnp.zeros((trp - used, d), BF16)
    cw_pair = jnp.dot(perm, jnp.concatenate(_split_bf16(cw_c, 2), axis=1), preferred_element_type=F32)
    cw_scr[pl.ds(0, used), :] = cw_pair[:, :LANES] + cw_pair[:, LANES:]
    cw_scr[pl.ds(used, trp - used), :] = jnp.zeros((trp - used, LANES), F32)
    yp_scr[...] = jnp.zeros_like(yp_scr)

    def expert_block(gi, row0, sb):
        rows = pl.ds(pl.multiple_of(row0, MOE_SEG_ALIGN), sb)
        hblk = hp_scr[rows, :]
        cwb = cw_scr[rows, :]
        y = None
        for k in range(eg):
            e = gi * eg + k
            gate = jnp.dot(hblk, wg_ref[e], preferred_element_type=F32)
            up = jnp.dot(hblk, wu_ref[e], preferred_element_type=F32)
            hid = (_silu(gate) * up * cwb[:, k:k + 1]).astype(BF16)
            t = jnp.dot(hid, wd_ref[e], preferred_element_type=F32)
            y = t if y is None else y + t
        yp_scr[rows, :] = y

    sb, wide = MOE_ROW_BLOCK, MOE_WIDE_BLOCK
    for gi in range(ng):
        use_wide = jnp.logical_and(counts[gi] > sb, counts[gi] <= wide)
        pl.when(use_wide)(functools.partial(expert_block, gi, starts[gi], wide))
        n_blocks = jnp.where(use_wide, 0, (counts[gi] + sb - 1) // sb)

        def block(b, carry, gi=gi):
            expert_block(gi, starts[gi] + b * sb, sb)
            return carry
        lax.fori_loop(0, n_blocks, block, 0)

    y_hi, y_lo = _split_bf16(yp_scr[pl.ds(0, used), :], 2)
    moe = (jnp.dot(perm_t, y_hi, preferred_element_type=F32)
           + jnp.dot(perm_t, y_lo, preferred_element_type=F32))
    out = x + gt_ref[...] * moe
    if final_norm:
        out = _rms(out, gf_ref[...])
    if n_ctx_tiles is None:
        out_refs[0][...] = out
    else:
        is_ctx = pl.program_id(0) < n_ctx_tiles

        @pl.when(is_ctx)
        def _():
            out_refs[0][...] = out

        @pl.when(jnp.logical_not(is_ctx))
        def _():
            out_refs[1][...] = out


def moe_layer(x, g, shift, scale, gate, w_router_split, b_router_col, wg, wu, wd, layer, g_final,
              n_ctx, dec_seq, final_norm, split_output=False, tm=MOE_TILE):
    t, d = x.shape
    _, n_e, _, f = wg.shape
    trp = tm + N_EXPERT_GROUPS * MOE_SEG_ALIGN + max(MOE_ROW_BLOCK, MOE_WIDE_BLOCK)
    cidx = _cvec_index(n_ctx, dec_seq, tm)
    mod_spec = pl.BlockSpec((None, 1, d), lambda i: (cidx(i), 0, 0))
    row_spec = pl.BlockSpec((1, d), lambda i: (0, 0))
    once = pl.Buffered(1)
    nct = n_ctx // tm
    if split_output:
        out_specs = [pl.BlockSpec((tm, d), lambda i: (jnp.minimum(i, nct - 1), 0)),
                     pl.BlockSpec((tm, d), lambda i: (jnp.maximum(i - nct, 0), 0))]
        out_shape = [jax.ShapeDtypeStruct((n_ctx, d), F32), jax.ShapeDtypeStruct((t - n_ctx, d), F32)]
    else:
        out_specs = pl.BlockSpec((tm, d), lambda i: (i, 0))
        out_shape = jax.ShapeDtypeStruct((t, d), F32)
    return pl.pallas_call(
        functools.partial(_moe_kernel, final_norm=final_norm,
                          n_ctx_tiles=nct if split_output else None),
        grid=(t // tm,),
        in_specs=[pl.BlockSpec((tm, d), lambda i: (i, 0)),
                  row_spec, mod_spec, mod_spec, mod_spec,
                  pl.BlockSpec(w_router_split.shape, lambda i: (0, 0)),
                  pl.BlockSpec(b_router_col.shape, lambda i: (0, 0)),
                  pl.BlockSpec((None, n_e, d, f), lambda i: (layer, 0, 0, 0), pipeline_mode=once),
                  pl.BlockSpec((None, n_e, d, f), lambda i: (layer, 0, 0, 0), pipeline_mode=once),
                  pl.BlockSpec((None, n_e, f, d), lambda i: (layer, 0, 0, 0), pipeline_mode=once),
                  row_spec],
        out_specs=out_specs,
        out_shape=out_shape,
        scratch_shapes=[pltpu.VMEM((trp, d), BF16), pltpu.VMEM((trp, LANES), F32),
                        pltpu.VMEM((trp, d), F32)],
        compiler_params=_params("arbitrary"),
        name="moe",
    )(x, g, shift, scale, gate, w_router_split, b_router_col, wg, wu, wd, g_final)


def _per_stream(n_ctx_tiles, ctx_ref, lat_ref, fn):
    i = pl.program_id(0)

    @pl.when(i < n_ctx_tiles)
    def _():
        fn(ctx_ref[...])

    @pl.when(i >= n_ctx_tiles)
    def _():
        fn(lat_ref[...])


def _stream_specs(tm, k, n_ctx_tiles):
    return [pl.BlockSpec((tm, k), lambda i: (jnp.minimum(i, n_ctx_tiles - 1), 0)),
            pl.BlockSpec((tm, k), lambda i: (jnp.maximum(i - n_ctx_tiles, 0), 0))]


def _resid_proj_kernel(x_ref, ac_ref, al_ref, w_ref, gt_ref, o_ref, *, n_ctx_tiles):
    def run(a):
        o_ref[...] = x_ref[...] + gt_ref[...] * _bdot(a, w_ref[...])
    _per_stream(n_ctx_tiles, ac_ref, al_ref, run)


def _mlstm_out_kernel(x_ref, hc_ref, hl_ref, og_ref, hg_ref, w_ref, gt_ref, o_ref, *, n_ctx_tiles):
    def run(hs):
        a = jax.nn.sigmoid(og_ref[...]) * (hs * hg_ref[...])
        o_ref[...] = x_ref[...] + gt_ref[...] * _bdot(a, w_ref[...])
    _per_stream(n_ctx_tiles, hc_ref, hl_ref, run)


def resid_proj(x, a_ctx, a_lat, w, gate, n_ctx, dec_seq, tm=512):
    t, d = x.shape
    k = a_ctx.shape[1]
    cidx = _cvec_index(n_ctx, dec_seq, tm)
    nct = n_ctx // tm
    return pl.pallas_call(
        functools.partial(_resid_proj_kernel, n_ctx_tiles=nct),
        grid=(t // tm,),
        in_specs=[pl.BlockSpec((tm, d), lambda i: (i, 0))] + _stream_specs(tm, k, nct) + [
            pl.BlockSpec((k, d), lambda i: (0, 0)),
            pl.BlockSpec((None, 1, d), lambda i: (cidx(i), 0, 0))],
        out_specs=pl.BlockSpec((tm, d), lambda i: (i, 0)),
        out_shape=jax.ShapeDtypeStruct((t, d), F32),
        compiler_params=_params("arbitrary"),
        name="resid_proj",
    )(x, a_ctx, a_lat, w, gate)


def mlstm_out(x, hs_ctx, hs_lat, o_gate, head_g, w, gate, n_ctx, dec_seq, tm=512):
    t, d = x.shape
    k = hs_ctx.shape[1]
    cidx = _cvec_index(n_ctx, dec_seq, tm)
    nct = n_ctx // tm
    return pl.pallas_call(
        functools.partial(_mlstm_out_kernel, n_ctx_tiles=nct),
        grid=(t // tm,),
        in_specs=[pl.BlockSpec((tm, d), lambda i: (i, 0))] + _stream_specs(tm, k, nct) + [
            pl.BlockSpec((tm, k), lambda i: (i, 0)),
            pl.BlockSpec((1, k), lambda i: (0, 0)),
            pl.BlockSpec((k, d), lambda i: (0, 0)),
            pl.BlockSpec((None, 1, d), lambda i: (cidx(i), 0, 0))],
        out_specs=pl.BlockSpec((tm, d), lambda i: (i, 0)),
        out_shape=jax.ShapeDtypeStruct((t, d), F32),
        compiler_params=_params("arbitrary"),
        name="mlstm_out",
    )(x, hs_ctx, hs_lat, o_gate, head_g, w, gate)


def _mlstm_proj_kernel(x_ref, g_ref, sh_ref, sc_ref, wqk_ref, wv_ref, wo_ref, wgt_ref, bg_ref,
                       wgtt_ref, bgt_ref, qs_ref, qkv_ref, o_ref, gc_ref, gr_ref):
    h = _modulated(x_ref[...], g_ref[...], sh_ref[...], sc_ref[...]).astype(BF16)
    nqk = wqk_ref.shape[1]
    qkv_ref[:, :nqk] = (jnp.dot(h, wqk_ref[...].astype(BF16), preferred_element_type=F32)
                        * qs_ref[...]).astype(BF16)
    qkv_ref[:, nqk:] = jnp.dot(h, wv_ref[...].astype(BF16),
                               preferred_element_type=F32).astype(BF16)
    o_ref[...] = jnp.dot(h, wo_ref[...].astype(BF16), preferred_element_type=F32)
    gc_ref[...] = jnp.dot(h, wgt_ref[...].astype(BF16), preferred_element_type=F32) + bg_ref[...]
    gr_ref[...] = _bdot_nt(wgtt_ref[...], h) + bgt_ref[...]


def mlstm_proj(x, g, shift, scale, w_in_all, layer, b_gate, n_ctx, dec_seq, tm=512):
    t, d = x.shape
    hh = MLSTM_HEADS
    hv = d
    hk = hv // 2
    ng = 4 * hh
    w_g = w_in_all[layer, :, 2 * hk + 2 * hv:]
    w_g_pad = jnp.pad(w_g, ((0, 0), (0, LANES - ng)))
    b_pad = jnp.pad(b_gate.reshape(1, ng), ((0, 0), (0, LANES - ng)))
    w_g_t = w_g.T
    b_t = b_gate.reshape(ng, 1)
    dk = hk // hh
    q_scale = jnp.concatenate([jnp.full((1, hk), dk ** -0.5, F32), jnp.ones((1, hk), F32)], axis=1)
    cidx = _cvec_index(n_ctx, dec_seq, tm)
    mod_spec = pl.BlockSpec((None, 1, d), lambda i: (cidx(i), 0, 0))
    full = lambda a: pl.BlockSpec(a.shape, lambda i: (0,) * a.ndim)
    assert 2 * hk == hv
    w_col = lambda n: pl.BlockSpec((None, d, hv), lambda i: (layer, 0, n))
    return pl.pallas_call(
        _mlstm_proj_kernel,
        grid=(t // tm,),
        in_specs=[pl.BlockSpec((tm, d), lambda i: (i, 0)), full(g), mod_spec, mod_spec,
                  w_col(0), w_col(1), w_col(2), full(w_g_pad), full(b_pad), full(w_g_t),
                  full(b_t), full(q_scale)],
        out_specs=[pl.BlockSpec((tm, 2 * hk + hv), lambda i: (i, 0)),
                   pl.BlockSpec((tm, hv), lambda i: (i, 0)),
                   pl.BlockSpec((tm, LANES), lambda i: (i, 0)),
                   pl.BlockSpec((ng, tm), lambda i: (0, i))],
        out_shape=[jax.ShapeDtypeStruct((t, 2 * hk + hv), BF16),
                   jax.ShapeDtypeStruct((t, hv), F32),
                   jax.ShapeDtypeStruct((t, LANES), F32),
                   jax.ShapeDtypeStruct((ng, t), F32)],
        compiler_params=_params("arbitrary"),
        name="mlstm_proj",
    )(x, g, shift, scale, w_in_all, w_in_all, w_in_all, w_g_pad, b_pad, w_g_t, b_t, q_scale)


def _log_sigmoid(x):
    return jnp.minimum(x, 0.0) - jnp.log(1.0 + jnp.exp(-jnp.abs(x)))


def _mlstm_chunk(q, k, v, i_col, f_col, i_row, f_row, c_st, n_st, m_st, rev):
    l = q.shape[0]
    lf_col = _log_sigmoid(f_col)
    lf_row = _log_sigmoid(f_row)
    tt = lax.broadcasted_iota(jnp.int32, (l, l), 0)
    ss = lax.broadcasted_iota(jnp.int32, (l, l), 1)
    causal = (ss >= tt) if rev else (ss <= tt)
    b_col = jnp.sum(jnp.where(causal, lf_row, 0.0), axis=1, keepdims=True)
    feeds = (tt >= ss) if rev else (tt <= ss)
    b_row = jnp.sum(jnp.where(feeds, lf_col, 0.0), axis=0, keepdims=True)
    log_d = jnp.where(causal, b_col - b_row + i_row, -jnp.inf)
    log_state = b_col + m_st
    m_t = jnp.maximum(log_state, jnp.max(log_d, axis=1, keepdims=True))
    dw = jnp.exp(log_d - m_t)
    sw = jnp.exp(log_state - m_t)
    a = _bdot_nt(q, k) * dw
    num = sw * _bdot(q, c_st) + _bdot(a, v)
    qn = jnp.sum(q.astype(F32) * n_st, axis=1, keepdims=True)
    den = sw * qn + jnp.sum(a, axis=1, keepdims=True)
    h = num / jnp.maximum(jnp.abs(den), jnp.exp(-m_t))
    b_last = b_col[0:1, :] if rev else b_col[l - 1:l, :]
    log_k = b_last - b_col + i_col
    m_new = jnp.maximum(b_last + m_st, jnp.max(log_k, axis=0, keepdims=True))
    kw = jnp.exp(log_k - m_new)
    decay = jnp.exp(b_last + m_st - m_new)
    kwk = kw * k.astype(F32)
    c_new = decay * c_st + _bdot_tn(kwk, v)
    n_new = decay * n_st + jnp.sum(kwk, axis=0, keepdims=True)
    return h, c_new, n_new, m_new


def _mlstm_scan_kernel(*refs, n_chunks, zero_init):
    if zero_init:
        q_ref, k_ref, v_ref, gc_ref, gr_ref, hs_ref, c_ref, n_ref, m_ref, hb_scr = refs
        c_ref[...] = jnp.zeros_like(c_ref)
        n_ref[...] = jnp.zeros_like(n_ref)
        m_ref[...] = jnp.zeros_like(m_ref)
    else:
        (q_ref, k_ref, v_ref, gc_ref, gr_ref, c0_ref, n0_ref, m0_ref,
         hs_ref, c_ref, n_ref, m_ref, hb_scr) = refs
        c_ref[...] = c0_ref[...]
        n_ref[...] = n0_ref[...]
        m_ref[...] = m0_ref[...]
    l = MLSTM_CHUNK
    hh = MLSTM_HEADS
    dk = q_ref.shape[1] // hh
    dv = v_ref.shape[1] // hh

    def body(c, carry):
        cr = n_chunks - 1 - c
        rows = (pl.ds(pl.multiple_of(c * l, l), l), pl.ds(pl.multiple_of(cr * l, l), l))
        grs = (gr_ref[c], gr_ref[cr])
        outs, states = ([], []), []
        for d in range(2):
            q, k, v, gc, gr = q_ref[rows[d], :], k_ref[rows[d], :], v_ref[rows[d], :], gc_ref[rows[d], :], grs[d]
            for h in range(hh):
                ci, cf = 2 * d * hh + h, (2 * d + 1) * hh + h
                o, *st = _mlstm_chunk(q[:, h * dk:(h + 1) * dk], k[:, h * dk:(h + 1) * dk],
                                      v[:, h * dv:(h + 1) * dv], gc[:, ci:ci + 1], gc[:, cf:cf + 1],
                                      gr[ci:ci + 1, :], gr[cf:cf + 1, :],
                                      c_ref[d, h], n_ref[d, h], m_ref[d, h], d == 1)
                outs[d].append(o)
                states.append((d, h, st))
        hs_ref[rows[0], :] = jnp.concatenate(outs[0], axis=1)
        hb_scr[rows[1], :] = jnp.concatenate(outs[1], axis=1)
        for d, h, (c_new, n_new, m_new) in states:
            c_ref[d, h], n_ref[d, h], m_ref[d, h] = c_new, n_new, m_new
        return carry

    lax.fori_loop(0, n_chunks, body, 0)
    for h in range(hh):
        cols = pl.ds(h * dv, dv)
        hs = hs_ref[:, cols] + hb_scr[:, cols]
        hs_ref[:, cols] = hs * lax.rsqrt(jnp.mean(hs * hs, axis=-1, keepdims=True) + NORM_EPS)


def mlstm_scan(qkv, gcol, grow, init, row_off, n_seq, seq_len):
    hh = MLSTM_HEADS
    hv = qkv.shape[1] // 2
    dv = hv // hh
    dk = dv // 2
    l = MLSTM_CHUNK
    nc = seq_len // l
    ob = row_off // seq_len
    kern = functools.partial(_mlstm_scan_kernel, n_chunks=nc, zero_init=init is None)
    st = lambda *tail: pl.BlockSpec((None, 2, hh) + tail, lambda s: (s, 0, 0) + (0,) * len(tail))
    states = [st(dk, dv), st(1, dk), st(1, 1)]
    return pl.pallas_call(
        kern,
        grid=(n_seq,),
        in_specs=[pl.BlockSpec((seq_len, hh * dk), lambda s: (ob + s, 0)),
                  pl.BlockSpec((seq_len, hh * dk), lambda s: (ob + s, 1)),
                  pl.BlockSpec((seq_len, hv), lambda s: (ob + s, 1)),
                  pl.BlockSpec((seq_len, LANES), lambda s: (ob + s, 0)),
                  pl.BlockSpec((nc, 4 * hh, l), lambda s: (ob + s, 0, 0))]
                 + ([] if init is None else states),
        out_specs=[pl.BlockSpec((seq_len, hv), lambda s: (s, 0))] + states,
        out_shape=[jax.ShapeDtypeStruct((n_seq * seq_len, hv), F32),
                   jax.ShapeDtypeStruct((n_seq, 2, hh, dk, dv), F32),
                   jax.ShapeDtypeStruct((n_seq, 2, hh, 1, dk), F32),
                   jax.ShapeDtypeStruct((n_seq, 2, hh, 1, 1), F32)],
        scratch_shapes=[pltpu.VMEM((seq_len, hv), F32)],
        compiler_params=_params("arbitrary"),
        name="mlstm_scan",
    )(qkv, qkv, qkv, gcol, grow, *(() if init is None else init))


def _mla_proj_kernel(x_ref, g_ref, sh_ref, sc_ref, win_ref, qg_ref, kvg_ref, wqb_ref, cos_ref,
                     sin_ref, q_ref, ckv_ref, kpe_ref, *, q_lora, kv_lora, rope, n_heads):
    h = _modulated(x_ref[...], g_ref[...], sh_ref[...], sc_ref[...])
    proj = _bdot(h, win_ref[...])
    q_lat = proj[:, :q_lora]
    ckv_ref[...] = _rms(proj[:, q_lora:q_lora + kv_lora], kvg_ref[...])
    cos, sin = cos_ref[...], sin_ref[...]
    kpe = proj[:, q_lora + kv_lora:q_lora + kv_lora + rope]
    kpe_rot = proj[:, q_lora + kv_lora + rope:q_lora + kv_lora + 2 * rope]
    kpe_ref[...] = kpe * cos[:, :rope] + kpe_rot * sin[:, :rope]
    q = _bdot(_rms(q_lat, qg_ref[...]), wqb_ref[...])
    n = n_heads * LANES
    pieces = []
    for hd in range(n_heads):
        c = slice(hd * LANES, (hd + 1) * LANES)
        pieces.append(q[:, c])
        pieces.append(q[:, n:2 * n][:, c] * cos + q[:, 2 * n:][:, c] * sin)
    q_ref[...] = jnp.concatenate(pieces, axis=1).astype(BF16)


def _rot_cols(w):
    n = w.shape[1]
    j = np.arange(n)
    quarter = MLA_ROPE // 4
    first = (j % (2 * quarter)) < quarter
    src = np.where(first, j + quarter, j - quarter)
    sign = np.where(first, -1.0, 1.0).astype(np.float32)
    return w[:, src] * sign


def _rope_tables(n_ctx, dec_batch, dec_seq):
    quarter = MLA_ROPE // 4
    freq = np.power(np.float32(ROPE_BASE), -np.arange(quarter, dtype=np.float32) / np.float32(quarter))
    pos = np.arange(dec_seq)
    ang_r = (pos // GRID_W).astype(np.float32)[:, None] * freq[None, :]
    ang_c = (pos % GRID_W).astype(np.float32)[:, None] * freq[None, :]
    ang = np.concatenate([ang_r, ang_r, ang_c, ang_c], axis=1).astype(np.float32)
    cos = np.concatenate([np.ones((n_ctx, MLA_ROPE), np.float32)] + [np.cos(ang)] * dec_batch, axis=0)
    sin = np.concatenate([np.zeros((n_ctx, MLA_ROPE), np.float32)] + [np.sin(ang)] * dec_batch, axis=0)
    reps = LANES // MLA_ROPE
    return (jnp.asarray(np.tile(cos, (1, reps)), F32), jnp.asarray(np.tile(sin, (1, reps)), F32))


def mla_proj(x, g, shift, scale, w_in, q_g, kv_g, w_qb, cos, sin, n_ctx, dec_seq, tm=512):
    t, d = x.shape
    hh, nope, rope = MLA_HEADS, MLA_NOPE, MLA_ROPE
    q_lora = q_g.shape[1]
    kv_lora = kv_g.shape[1]
    w_in_ext = jnp.concatenate([w_in, _rot_cols(w_in[:, q_lora + kv_lora:])], axis=1)
    w3 = w_qb.reshape(q_lora, hh, nope + rope)
    w_qn = w3[:, :, :nope].reshape(q_lora, hh * nope)
    w_qp = w3[:, :, nope:].reshape(q_lora, hh * rope)
    assert nope == LANES and rope <= LANES
    lane_pad = lambda w: jnp.pad(w.reshape(q_lora, hh, rope),
                                 ((0, 0), (0, 0), (0, LANES - rope))).reshape(q_lora, hh * LANES)
    w_qb_ext = jnp.concatenate([w_qn, lane_pad(w_qp), lane_pad(_rot_cols(w_qp))], axis=1)
    cidx = _cvec_index(n_ctx, dec_seq, tm)
    mod_spec = pl.BlockSpec((None, 1, d), lambda i: (cidx(i), 0, 0))
    full = lambda a: pl.BlockSpec(a.shape, lambda i: (0,) * a.ndim)
    kern = functools.partial(_mla_proj_kernel, q_lora=q_lora, kv_lora=kv_lora, rope=rope, n_heads=hh)
    return pl.pallas_call(
        kern,
        grid=(t // tm,),
        in_specs=[pl.BlockSpec((tm, d), lambda i: (i, 0)), full(g), mod_spec, mod_spec,
                  full(w_in_ext), full(q_g), full(kv_g), full(w_qb_ext),
                  pl.BlockSpec((tm, LANES), lambda i: (i, 0)),
                  pl.BlockSpec((tm, LANES), lambda i: (i, 0))],
        out_specs=[pl.BlockSpec((tm, 2 * hh * LANES), lambda i: (i, 0)),
                   pl.BlockSpec((tm, kv_lora), lambda i: (i, 0)),
                   pl.BlockSpec((tm, rope), lambda i: (i, 0))],
        out_shape=[jax.ShapeDtypeStruct((t, 2 * hh * LANES), BF16),
                   jax.ShapeDtypeStruct((t, kv_lora), F32),
                   jax.ShapeDtypeStruct((t, rope), F32)],
        compiler_params=_params("arbitrary"),
        name="mla_proj",
    )(x, g, shift, scale, w_in_ext, q_g, kv_g, w_qb_ext, cos, sin)


def _mla_kv_kernel(ckv_ref, kp_ref, w_ref, k_ref, v_ref):
    kv = _bdot(ckv_ref[...], w_ref[...])
    n = v_ref.shape[1]
    kp = kp_ref[...]
    pieces = []
    for hd in range(n // LANES):
        pieces += [kv[:, hd * LANES:(hd + 1) * LANES].astype(BF16), kp]
    k_ref[...] = jnp.concatenate(pieces, axis=1)
    v_ref[...] = kv[:, n:].astype(BF16)


def mla_kv(ckv_all, kp_pad, w_kvb, tm=512):
    r, kv_lora = ckv_all.shape
    hh, nope, vd = MLA_HEADS, MLA_NOPE, MLA_V
    w3 = w_kvb.reshape(kv_lora, hh, nope + vd)
    w_perm = jnp.concatenate([w3[:, :, :nope].reshape(kv_lora, hh * nope),
                              w3[:, :, nope:].reshape(kv_lora, hh * vd)], axis=1)
    return pl.pallas_call(
        _mla_kv_kernel,
        grid=(r // tm,),
        in_specs=[pl.BlockSpec((tm, kv_lora), lambda i: (i, 0)),
                  pl.BlockSpec((tm, LANES), lambda i: (i, 0)),
                  pl.BlockSpec(w_perm.shape, lambda i: (0, 0))],
        out_specs=[pl.BlockSpec((tm, 2 * hh * LANES), lambda i: (i, 0)),
                   pl.BlockSpec((tm, hh * vd), lambda i: (i, 0))],
        out_shape=[jax.ShapeDtypeStruct((r, 2 * hh * LANES), BF16),
                   jax.ShapeDtypeStruct((r, hh * vd), BF16)],
        compiler_params=_params("arbitrary"),
        name="mla_kv",
    )(ckv_all, kp_pad, w_perm)


def _attn_kernel(q_ref, k_ref, v_ref, o_ref, *, scale):
    c = scale * np.log2(np.e)
    for h in range(MLA_HEADS):
        hk = slice(2 * h * LANES, 2 * (h + 1) * LANES)
        s = _bdot_nt(q_ref[:, hk], k_ref[:, hk])
        e = jnp.exp2((s - jnp.max(s, axis=-1, keepdims=True)) * c)
        o = _bdot(e, v_ref[:, h * MLA_V:(h + 1) * MLA_V]) / jnp.sum(e, axis=-1, keepdims=True)
        o_ref[:, h * MLA_V:(h + 1) * MLA_V] = o.astype(BF16)


def mla_attention(q, k, v, q_row_off, k_row_off, n_seq, q_len, k_len):
    tq = ATTN_Q_BLOCK
    qb = q_len // tq
    q0 = q_row_off // tq
    k0 = k_row_off // k_len
    dq, dv = q.shape[1], v.shape[1]
    kern = functools.partial(_attn_kernel, scale=(MLA_NOPE + MLA_ROPE) ** -0.5)
    return pl.pallas_call(
        kern,
        grid=(n_seq, qb),
        in_specs=[pl.BlockSpec((tq, dq), lambda s, j: (q0 + s * qb + j, 0)),
                  pl.BlockSpec((k_len, dq), lambda s, j: (k0 + s, 0)),
                  pl.BlockSpec((k_len, dv), lambda s, j: (k0 + s, 0))],
        out_specs=pl.BlockSpec((tq, dv), lambda s, j: (s * qb + j, 0)),
        out_shape=jax.ShapeDtypeStruct((n_seq * q_len, dv), BF16),
        compiler_params=_params("arbitrary", "arbitrary"),
        name="mla_attention",
    )(q, k, v)


def kernel(x_prompt, x_sample, state_mlstm_C, state_mlstm_n, state_mlstm_m, cache_mla_ckv,
           cache_mla_kpe, c, c_ctx, w_ada, b_ada, norm_mix, norm_ffn, norm_final, w_pool,
           pool_scale, w_mlstm_in, b_mlstm_gate, mlstm_head_g, w_mlstm_out, w_mla_in, mla_q_g,
           mla_kv_g, w_mla_qb, w_mla_kvb, w_mla_out, w_router, b_router, w_exp_gate, w_exp_up,
           w_exp_down):
    batch, seq, d = x_prompt.shape
    dec_batch, dec_seq, _ = x_sample.shape
    depth = w_ada.shape[0]
    n_ctx = batch * seq
    n_lat = dec_batch * dec_seq
    hh = MLSTM_HEADS
    past = cache_mla_ckv.shape[2]

    x = (x_prompt.reshape(n_ctx, d), x_sample.reshape(n_lat, d))

    n_cv = 1 + dec_batch
    cvecs = jnp.concatenate([c_ctx[None, :], c, jnp.zeros((SUBLANES - n_cv % SUBLANES, d), F32)], axis=0)
    mod = ada_mod_all(cvecs, w_ada, b_ada).reshape(depth, cvecs.shape[0], 6, 1, d)

    w_router_pad = jnp.pad(w_router, ((0, 0), (0, LANES - N_EXPERTS)))
    w_router_hi = w_router_pad.astype(BF16)
    w_router_lo = (w_router_pad - w_router_hi.astype(F32)).astype(BF16)
    w_router_split = jnp.concatenate([w_router_hi, w_router_lo], axis=1)
    b_router_col = b_router.reshape(N_EXPERTS, 1)
    wg_bf, wu_bf, wd_bf = (w.astype(BF16) for w in (w_exp_gate, w_exp_up, w_exp_down))
    g_final = norm_final.reshape(1, d)
    row = lambda a: a.reshape(1, -1)

    outs = {}
    for i in range(depth):
        kind, j = i % 3, i // 3
        m = [mod[i, :n_cv, k] for k in range(6)]
        g_mix = row(norm_mix[i])
        if kind == 0:
            x = pool_layer(x, g_mix, m[0], m[1], m[2], w_pool[j], row(pool_scale[j]),
                           n_ctx, seq, dec_seq)
        elif kind == 1:
            qkv, o_gate, gcol, grow = mlstm_proj(x, g_mix, m[0], m[1], w_mlstm_in, j,
                                                 b_mlstm_gate[j], n_ctx, dec_seq)
            t = n_ctx + n_lat
            l = MLSTM_CHUNK
            grow_c = grow.reshape(4 * hh, t // l, l).transpose(1, 0, 2)
            hs_c, c_new, n_new, m_new = mlstm_scan(qkv, gcol, grow_c, None, 0, batch, seq)
            init = (state_mlstm_C[:, j], state_mlstm_n[:, j][:, :, :, None, :],
                    state_mlstm_m[:, j][:, :, :, None, None])
            hs_l, _, _, _ = mlstm_scan(qkv, gcol, grow_c, init, n_ctx, dec_batch, dec_seq)
            outs["C"] = c_new[:, None]
            outs["n"] = n_new[:, None, :, :, 0, :]
            outs["m"] = m_new[:, None, :, :, 0, 0]
            x = mlstm_out(x, hs_c, hs_l, o_gate, row(mlstm_head_g[j]), w_mlstm_out[j], m[2],
                          n_ctx, dec_seq)
        else:
            cos, sin = _rope_tables(n_ctx, dec_batch, dec_seq)
            q_cat, ckv, kpe = mla_proj(x, g_mix, m[0], m[1], w_mla_in[j], row(mla_q_g[j]),
                                       row(mla_kv_g[j]), w_mla_qb[j], cos, sin, n_ctx, dec_seq)
            lat_parts_c, lat_parts_p = [], []
            for b in range(dec_batch):
                lo = n_ctx + b * dec_seq
                lat_parts_c += [cache_mla_ckv[b, j], ckv[lo:lo + dec_seq]]
                lat_parts_p += [cache_mla_kpe[b, j], kpe[lo:lo + dec_seq]]
            ckv_all = jnp.concatenate(lat_parts_c + [ckv[:n_ctx]], axis=0)
            kp_all = jnp.concatenate(lat_parts_p + [kpe[:n_ctx]], axis=0).astype(BF16)
            kp_pad = jnp.pad(kp_all, ((0, 0), (0, LANES - kp_all.shape[1])))
            k_cat, v = mla_kv(ckv_all, kp_pad, w_mla_kvb[j])
            k_lat = past + dec_seq
            o_c = mla_attention(q_cat, k_cat, v, 0, dec_batch * k_lat, batch, seq, seq)
            o_l = mla_attention(q_cat, k_cat, v, n_ctx, 0, dec_batch, dec_seq, k_lat)
            outs["ckv"] = ckv[:n_ctx].reshape(batch, 1, seq, -1)
            outs["kpe"] = kpe[:n_ctx].reshape(batch, 1, seq, -1)
            x = resid_proj(x, o_c, o_l, w_mla_out[j], m[2], n_ctx, dec_seq)
        x = moe_layer(x, row(norm_ffn[i]), m[3], m[4], m[5], w_router_split, b_router_col,
                      wg_bf, wu_bf, wd_bf, i, g_final, n_ctx, dec_seq,
                      final_norm=(i == depth - 1), split_output=(i == depth - 1))

    y_prompt = x[0].reshape(batch, seq, d)
    y_sample = x[1].reshape(dec_batch, dec_seq, d)
    return (y_prompt, y_sample, outs["C"], outs["n"], outs["m"], outs["ckv"], outs["kpe"])
```

```python
import functools

import numpy as np
import jax
import jax.numpy as jnp
from jax import lax
from jax.experimental import pallas as pl
from jax.experimental.pallas import tpu as pltpu

F32 = jnp.float32
BF16 = jnp.bfloat16

NORM_EPS = 1e-6
GRID_W = 64
POOL_WINDOWS = (2, 4, 8, 16)
MLSTM_HEADS = 4
MLSTM_CHUNK = 256
MLA_HEADS = 8
MLA_NOPE = 128
MLA_ROPE = 64
MLA_V = 128
ROPE_BASE = 10000.0
N_EXPERTS = 16
N_EXPERT_GROUPS = 4
EXPERTS_PER_GROUP = N_EXPERTS // N_EXPERT_GROUPS

LANES = 128
SUBLANES = 8
VMEM_LIMIT = 56 * 1024 * 1024
POOL_TILE = 256
POOL_HALO = 8
ATTN_Q_BLOCK = 256
MOE_TILE = 512
MOE_SEG_ALIGN = 16
MOE_ROW_BLOCK = 160
MOE_WIDE_BLOCK = 224


def _params(*sem):
    return pltpu.CompilerParams(dimension_semantics=sem, vmem_limit_bytes=VMEM_LIMIT)


def _rms(x, g):
    return x * lax.rsqrt(jnp.mean(x * x, axis=-1, keepdims=True) + NORM_EPS) * g


def _modulated(x, g, shift, scale):
    return _rms(x, g) * (1.0 + scale) + shift


def _silu(x):
    return x * jax.nn.sigmoid(x)


def _bdot(a, b):
    return jnp.dot(a.astype(BF16), b.astype(BF16), preferred_element_type=F32)


def _bdot_nt(a, b):
    return lax.dot_general(a.astype(BF16), b.astype(BF16), (((1,), (1,)), ((), ())),
                           preferred_element_type=F32)


def _bdot_tn(a, b):
    return lax.dot_general(a.astype(BF16), b.astype(BF16), (((0,), (0,)), ((), ())),
                           preferred_element_type=F32)


def _cvec_index(n_ctx, dec_seq, tm):
    def idx(i):
        r = i * tm
        return jnp.where(r < n_ctx, 0, (r - n_ctx) // dec_seq + 1)
    return idx


def _ada_kernel(c_ref, w_ref, b_ref, o_ref):
    o_ref[...] = _bdot(_silu(c_ref[...]), w_ref[...]) + b_ref[...]


def ada_mod_all(cvecs, w_ada, b_ada, tn=1536):
    depth, d, n6 = w_ada.shape
    rows = cvecs.shape[0]
    return pl.pallas_call(
        _ada_kernel,
        grid=(depth, n6 // tn),
        in_specs=[pl.BlockSpec((rows, d), lambda l, n: (0, 0)),
                  pl.BlockSpec((None, d, tn), lambda l, n: (l, 0, n)),
                  pl.BlockSpec((None, 1, tn), lambda l, n: (l, 0, n))],
        out_specs=pl.BlockSpec((None, rows, tn), lambda l, n: (l, 0, n)),
        out_shape=jax.ShapeDtypeStruct((depth, rows, n6), F32),
        compiler_params=_params("arbitrary", "arbitrary"),
        name="ada_mod",
    )(cvecs, w_ada, b_ada.reshape(depth, 1, n6))


def _pool_kernel(*refs, n_ctx_tiles, ctx_seq_tiles, lat_seq_tiles, split_input):
    n_x = 6 if split_input else 3
    x_refs = refs[:n_x]
    g_ref, sh_ref, sc_ref, gt_ref, wp_ref, ps_ref, o_ref, buf_ref = refs[n_x:]
    i = pl.program_id(0)
    is_ctx = i < n_ctx_tiles
    j = jnp.where(is_ctx, i % ctx_seq_tiles, (i - n_ctx_tiles) % lat_seq_tiles)
    nt = jnp.where(is_ctx, ctx_seq_tiles, lat_seq_tiles)
    g, sh, sc = g_ref[...], sh_ref[...], sc_ref[...]
    tp, hl = POOL_TILE, POOL_HALO
    gw = o_ref.shape[1] // len(POOL_WINDOWS)

    def fill(xc_ref, xp_ref, xn_ref):
        buf_ref[pl.ds(0, hl), :] = jnp.where(j == 0, 0.0, _modulated(xp_ref[...], g, sh, sc))
        buf_ref[pl.ds(hl, tp), :] = _modulated(xc_ref[...], g, sh, sc)
        buf_ref[pl.ds(hl + tp, hl), :] = jnp.where(j == nt - 1, 0.0,
                                                    _modulated(xn_ref[...], g, sh, sc))
        o_ref[...] = xc_ref[...]

    if split_input:
        pl.when(is_ctx)(lambda: fill(*x_refs[:3]))
        pl.when(jnp.logical_not(is_ctx))(lambda: fill(*x_refs[3:]))
    else:
        fill(*x_refs)

    pos = j * tp + lax.broadcasted_iota(jnp.int32, (tp, 1), 0)
    seq_len = nt * tp
    for gi, w in enumerate(POOL_WINDOWS):
        cols = pl.ds(gi * gw, gw)
        acc = buf_ref[pl.ds(hl - w // 2, tp), cols]
        for d in range(-w // 2 + 1, w // 2):
            acc = acc + buf_ref[pl.ds(hl + d, tp), cols]
        cnt = jnp.minimum(pos + w // 2, seq_len) - jnp.maximum(pos - w // 2, 0)
        pooled = acc / cnt.astype(F32) - buf_ref[pl.ds(hl, tp), cols]
        y = _bdot(pooled, wp_ref[gi]) * ps_ref[:, cols]
        o_ref[:, cols] = o_ref[:, cols] + gt_ref[:, cols] * y


def _halo_specs(tp, hl, d, tile_off, n_rows):
    hb = tp // hl
    last_tile, last_hblk = n_rows // tp - 1, n_rows // hl - 1
    tile = lambda i: jnp.clip(i - tile_off, 0, last_tile)
    return [pl.BlockSpec((tp, d), lambda i: (tile(i), 0)),
            pl.BlockSpec((hl, d), lambda i: (jnp.clip(tile(i) * hb - 1, 0, last_hblk), 0)),
            pl.BlockSpec((hl, d), lambda i: (jnp.clip((tile(i) + 1) * hb, 0, last_hblk), 0))]


def pool_layer(xs, g, shift, scale, gate, w_pool, pool_scale, n_ctx, seq, dec_seq):
    split = isinstance(xs, tuple)
    tp, hl = POOL_TILE, POOL_HALO
    if split:
        d = xs[0].shape[1]
        t = xs[0].shape[0] + xs[1].shape[0]
        x_specs = (_halo_specs(tp, hl, d, 0, xs[0].shape[0])
                   + _halo_specs(tp, hl, d, n_ctx // tp, xs[1].shape[0]))
        x_args = (xs[0],) * 3 + (xs[1],) * 3
    else:
        t, d = xs.shape
        x_specs = _halo_specs(tp, hl, d, 0, t)
        x_args = (xs,) * 3
    cidx = _cvec_index(n_ctx, dec_seq, tp)
    mod_spec = pl.BlockSpec((None, 1, d), lambda i: (cidx(i), 0, 0))
    row_spec = pl.BlockSpec((1, d), lambda i: (0, 0))
    kern = functools.partial(_pool_kernel, n_ctx_tiles=n_ctx // tp, ctx_seq_tiles=seq // tp,
                             lat_seq_tiles=dec_seq // tp, split_input=split)
    return pl.pallas_call(
        kern,
        grid=(t // tp,),
        in_specs=x_specs + [row_spec, mod_spec, mod_spec, mod_spec,
                            pl.BlockSpec(w_pool.shape, lambda i: (0, 0, 0)), row_spec],
        out_specs=pl.BlockSpec((tp, d), lambda i: (i, 0)),
        out_shape=jax.ShapeDtypeStruct((t, d), F32),
        scratch_shapes=[pltpu.VMEM((tp + 2 * hl, d), F32)],
        compiler_params=_params("arbitrary"),
        name="pool_mixer",
    )(*x_args, g, shift, scale, gate, w_pool, pool_scale)


def _route(sel, scores):
    e, tm = sel.shape
    row = lax.broadcasted_iota(jnp.int32, (e, tm), 0)
    best = jnp.zeros((1, tm), jnp.int32)
    best_sc = None
    for gidx in range(N_EXPERT_GROUPS):
        r = [sel[gidx * EXPERTS_PER_GROUP + k:gidx * EXPERTS_PER_GROUP + k + 1, :]
             for k in range(EXPERTS_PER_GROUP)]
        top2 = None
        for a in range(EXPERTS_PER_GROUP):
            for b in range(a + 1, EXPERTS_PER_GROUP):
                s = r[a] + r[b]
                top2 = s if top2 is None else jnp.maximum(top2, s)
        if best_sc is None:
            best_sc = top2
        else:
            better = top2 > best_sc
            best = jnp.where(better, gidx, best)
            best_sc = jnp.where(better, top2, best_sc)
    neg = -jnp.inf
    masked = jnp.where(row // EXPERTS_PER_GROUP == best, sel, neg)
    m1 = jnp.max(masked, axis=0, keepdims=True)
    i1 = jnp.min(jnp.where(masked == m1, row, e), axis=0, keepdims=True)
    masked2 = jnp.where(row == i1, neg, masked)
    m2 = jnp.max(masked2, axis=0, keepdims=True)
    i2 = jnp.min(jnp.where(masked2 == m2, row, e), axis=0, keepdims=True)
    hot1 = row == i1
    hot2 = row == i2
    w1 = jnp.sum(jnp.where(hot1, scores, 0.0), axis=0, keepdims=True)
    w2 = jnp.sum(jnp.where(hot2, scores, 0.0), axis=0, keepdims=True)
    tot = w1 + w2
    return best, jnp.where(hot1, w1 / tot, 0.0) + jnp.where(hot2, w2 / tot, 0.0)


def _split_bf16(a, parts):
    out = []
    for _ in range(parts):
        p = a.astype(BF16)
        out.append(p)
        a = a - p.astype(F32)
    return out


def _pad_rows(a, rows):
    return jnp.concatenate([a, jnp.zeros((rows - a.shape[0], a.shape[1]), a.dtype)], axis=0)


def _moe_kernel(x_ref, g_ref, sh_ref, sc_ref, gt_ref, wr_ref, br_ref, wgf_ref, wuf_ref, wdf_ref,
                gf_ref, *rest, final_norm, n_ctx_tiles, n_experts):
    out_refs, (wg_ref, wu_ref, wd_ref), scratch = rest[:-6], rest[-6:-3], rest[-3:]
    i = pl.program_id(0)

    @pl.when(i < n_experts)
    def _():
        wg_ref[i] = wgf_ref[...].astype(BF16)
        wu_ref[i] = wuf_ref[...].astype(BF16)
        wd_ref[i] = wdf_ref[...].astype(BF16)

    @pl.when(i >= n_experts)
    def _():
        _moe_tile(x_ref, g_ref, sh_ref, sc_ref, gt_ref, wr_ref, br_ref, wg_ref, wu_ref, wd_ref,
                  gf_ref, out_refs, scratch, i - n_experts, final_norm, n_ctx_tiles)


def _moe_tile(x_ref, g_ref, sh_ref, sc_ref, gt_ref, wr_ref, br_ref, wg_ref, wu_ref, wd_ref,
              gf_ref, out_refs, scratch, tile, final_norm, n_ctx_tiles):
    hp_scr, cw_scr, yp_scr = scratch
    tr, d = x_ref.shape
    trp = hp_scr.shape[0]
    ng, eg = N_EXPERT_GROUPS, EXPERTS_PER_GROUP
    x = x_ref[...]
    h = _modulated(x, g_ref[...], sh_ref[...], sc_ref[...])
    hb = h.astype(BF16)

    h_lo = (h - hb.astype(F32)).astype(BF16)
    wr = wr_ref[...]
    lg = jnp.dot(hb, wr, preferred_element_type=F32)
    logits = lg[:, :LANES] + lg[:, LANES:] + jnp.dot(h_lo, wr[:, :LANES], preferred_element_type=F32)
    scores = jax.nn.sigmoid(logits.T[:N_EXPERTS, :])
    best, comb_t = _route(scores + br_ref[...], scores)

    grp = lax.broadcasted_iota(jnp.int32, (SUBLANES, tr), 0)
    hot_t = (grp == best).astype(F32)
    cw_t = hot_t[0:1, :] * comb_t[0:eg, :]
    for gi in range(1, ng):
        cw_t = cw_t + hot_t[gi:gi + 1, :] * comb_t[gi * eg:(gi + 1) * eg, :]
    cw_c = _pad_rows(cw_t, LANES).T

    ia = lax.broadcasted_iota(jnp.int32, (tr, tr), 0)
    ib = lax.broadcasted_iota(jnp.int32, (tr, tr), 1)
    before = jnp.where(ia < ib, 1.0, 0.0).astype(BF16)
    rank_t = jnp.dot(hot_t.astype(BF16), before, preferred_element_type=F32)

    starts, counts = [], []
    off = jnp.int32(0)
    for gi in range(ng):
        n = jnp.sum(hot_t[gi:gi + 1, :]).astype(jnp.int32)
        n = ((n + MOE_SEG_ALIGN - 1) // MOE_SEG_ALIGN) * MOE_SEG_ALIGN
        starts.append(off)
        counts.append(n)
        off = off + n

    pos_t = hot_t[0:1, :] * (rank_t[0:1, :] + starts[0].astype(F32))
    for gi in range(1, ng):
        pos_t = pos_t + hot_t[gi:gi + 1, :] * (rank_t[gi:gi + 1, :] + starts[gi].astype(F32))
    pos_c = _pad_rows(pos_t, LANES).T[:, 0:1]
    used = tr + ng * MOE_SEG_ALIGN
    perm = jnp.where(lax.broadcasted_iota(jnp.int32, (used, tr), 0) == pos_t.astype(jnp.int32),
                     1.0, 0.0).astype(BF16)
    perm_t = jnp.where(lax.broadcasted_iota(jnp.int32, (tr, used), 1) == pos_c.astype(jnp.int32),
                       1.0, 0.0).astype(BF16)

    hp_scr[pl.ds(0, used), :] = jnp.dot(perm, hb, preferred_element_type=F32).astype(BF16)
    hp_scr[pl.ds(used, trp - used), :] = jnp.zeros((trp - used, d), BF16)
    cw_pair = jnp.dot(perm, jnp.concatenate(_split_bf16(cw_c, 2), axis=1), preferred_element_type=F32)
    cw_scr[pl.ds(0, used), :] = cw_pair[:, :LANES] + cw_pair[:, LANES:]
    cw_scr[pl.ds(used, trp - used), :] = jnp.zeros((trp - used, LANES), F32)
    yp_scr[...] = jnp.zeros_like(yp_scr)

    def expert_block(gi, row0, sb):
        rows = pl.ds(pl.multiple_of(row0, MOE_SEG_ALIGN), sb)
        hblk = hp_scr[rows, :]
        cwb = cw_scr[rows, :]
        y = None
        for k in range(eg):
            e = gi * eg + k
            gate = jnp.dot(hblk, wg_ref[e], preferred_element_type=F32)
            up = jnp.dot(hblk, wu_ref[e], preferred_element_type=F32)
            hid = (_silu(gate) * up * cwb[:, k:k + 1]).astype(BF16)
            t = jnp.dot(hid, wd_ref[e], preferred_element_type=F32)
            y = t if y is None else y + t
        yp_scr[rows, :] = y

    sb, wide = MOE_ROW_BLOCK, MOE_WIDE_BLOCK
    for gi in range(ng):
        use_wide = jnp.logical_and(counts[gi] > sb, counts[gi] <= wide)
        pl.when(use_wide)(functools.partial(expert_block, gi, starts[gi], wide))
        n_blocks = jnp.where(use_wide, 0, (counts[gi] + sb - 1) // sb)

        def block(b, carry, gi=gi):
            expert_block(gi, starts[gi] + b * sb, sb)
            return carry
        lax.fori_loop(0, n_blocks, block, 0)

    y_hi, y_lo = _split_bf16(yp_scr[pl.ds(0, used), :], 2)
    moe = (jnp.dot(perm_t, y_hi, preferred_element_type=F32)
           + jnp.dot(perm_t, y_lo, preferred_element_type=F32))
    out = x + gt_ref[...] * moe
    if final_norm:
        out = _rms(out, gf_ref[...])
    if n_ctx_tiles is None:
        out_refs[0][...] = out
    else:
        is_ctx = tile < n_ctx_tiles

        @pl.when(is_ctx)
        def _():
            out_refs[0][...] = out

        @pl.when(jnp.logical_not(is_ctx))
        def _():
            out_refs[1][...] = out


def moe_layer(x, g, shift, scale, gate, w_router_split, b_router_col, wg, wu, wd, layer, g_final,
              n_ctx, dec_seq, final_norm, split_output=False, tm=MOE_TILE):
    t, d = x.shape
    _, n_e, _, f = wg.shape
    trp = tm + N_EXPERT_GROUPS * MOE_SEG_ALIGN + max(MOE_ROW_BLOCK, MOE_WIDE_BLOCK)
    tile = lambda i: jnp.maximum(i - n_e, 0)
    cidx = _cvec_index(n_ctx, dec_seq, tm)
    mod_spec = pl.BlockSpec((None, 1, d), lambda i: (cidx(tile(i)), 0, 0))
    row_spec = pl.BlockSpec((1, d), lambda i: (0, 0))
    expert = lambda i: jnp.minimum(i, n_e - 1)
    nct = n_ctx // tm
    if split_output:
        out_specs = [pl.BlockSpec((tm, d), lambda i: (jnp.minimum(tile(i), nct - 1), 0)),
                     pl.BlockSpec((tm, d), lambda i: (jnp.maximum(tile(i) - nct, 0), 0))]
        out_shape = [jax.ShapeDtypeStruct((n_ctx, d), F32), jax.ShapeDtypeStruct((t - n_ctx, d), F32)]
    else:
        out_specs = pl.BlockSpec((tm, d), lambda i: (tile(i), 0))
        out_shape = jax.ShapeDtypeStruct((t, d), F32)
    return pl.pallas_call(
        functools.partial(_moe_kernel, final_norm=final_norm,
                          n_ctx_tiles=nct if split_output else None, n_experts=n_e),
        grid=(n_e + t // tm,),
        in_specs=[pl.BlockSpec((tm, d), lambda i: (tile(i), 0)),
                  row_spec, mod_spec, mod_spec, mod_spec,
                  pl.BlockSpec(w_router_split.shape, lambda i: (0, 0)),
                  pl.BlockSpec(b_router_col.shape, lambda i: (0, 0)),
                  pl.BlockSpec((None, None, d, f), lambda i: (layer, expert(i), 0, 0)),
                  pl.BlockSpec((None, None, d, f), lambda i: (layer, expert(i), 0, 0)),
                  pl.BlockSpec((None, None, f, d), lambda i: (layer, expert(i), 0, 0)),
                  row_spec],
        out_specs=out_specs,
        out_shape=out_shape,
        scratch_shapes=[pltpu.VMEM((n_e, d, f), BF16), pltpu.VMEM((n_e, d, f), BF16),
                        pltpu.VMEM((n_e, f, d), BF16),
                        pltpu.VMEM((trp, d), BF16), pltpu.VMEM((trp, LANES), F32),
                        pltpu.VMEM((trp, d), F32)],
        compiler_params=_params("arbitrary"),
        name="moe",
    )(x, g, shift, scale, gate, w_router_split, b_router_col, wg, wu, wd, g_final)


def _per_stream(n_ctx_tiles, ctx_ref, lat_ref, fn):
    i = pl.program_id(0)

    @pl.when(i < n_ctx_tiles)
    def _():
        fn(ctx_ref[...])

    @pl.when(i >= n_ctx_tiles)
    def _():
        fn(lat_ref[...])


def _stream_specs(tm, k, n_ctx_tiles):
    return [pl.BlockSpec((tm, k), lambda i: (jnp.minimum(i, n_ctx_tiles - 1), 0)),
            pl.BlockSpec((tm, k), lambda i: (jnp.maximum(i - n_ctx_tiles, 0), 0))]


def _resid_proj_kernel(x_ref, ac_ref, al_ref, w_ref, gt_ref, o_ref, *, n_ctx_tiles):
    def run(a):
        o_ref[...] = x_ref[...] + gt_ref[...] * _bdot(a, w_ref[...])
    _per_stream(n_ctx_tiles, ac_ref, al_ref, run)


def _mlstm_out_kernel(x_ref, hc_ref, hl_ref, og_ref, hg_ref, w_ref, gt_ref, o_ref, *, n_ctx_tiles):
    def run(hs):
        a = jax.nn.sigmoid(og_ref[...]) * (hs * hg_ref[...])
        o_ref[...] = x_ref[...] + gt_ref[...] * _bdot(a, w_ref[...])
    _per_stream(n_ctx_tiles, hc_ref, hl_ref, run)


def resid_proj(x, a_ctx, a_lat, w, gate, n_ctx, dec_seq, tm=512):
    t, d = x.shape
    k = a_ctx.shape[1]
    cidx = _cvec_index(n_ctx, dec_seq, tm)
    nct = n_ctx // tm
    return pl.pallas_call(
        functools.partial(_resid_proj_kernel, n_ctx_tiles=nct),
        grid=(t // tm,),
        in_specs=[pl.BlockSpec((tm, d), lambda i: (i, 0))] + _stream_specs(tm, k, nct) + [
            pl.BlockSpec((k, d), lambda i: (0, 0)),
            pl.BlockSpec((None, 1, d), lambda i: (cidx(i), 0, 0))],
        out_specs=pl.BlockSpec((tm, d), lambda i: (i, 0)),
        out_shape=jax.ShapeDtypeStruct((t, d), F32),
        compiler_params=_params("arbitrary"),
        name="resid_proj",
    )(x, a_ctx, a_lat, w, gate)


def mlstm_out(x, hs_ctx, hs_lat, o_gate, head_g, w, gate, n_ctx, dec_seq, tm=512):
    t, d = x.shape
    k = hs_ctx.shape[1]
    cidx = _cvec_index(n_ctx, dec_seq, tm)
    nct = n_ctx // tm
    return pl.pallas_call(
        functools.partial(_mlstm_out_kernel, n_ctx_tiles=nct),
        grid=(t // tm,),
        in_specs=[pl.BlockSpec((tm, d), lambda i: (i, 0))] + _stream_specs(tm, k, nct) + [
            pl.BlockSpec((tm, k), lambda i: (i, 0)),
            pl.BlockSpec((1, k), lambda i: (0, 0)),
            pl.BlockSpec((k, d), lambda i: (0, 0)),
            pl.BlockSpec((None, 1, d), lambda i: (cidx(i), 0, 0))],
        out_specs=pl.BlockSpec((tm, d), lambda i: (i, 0)),
        out_shape=jax.ShapeDtypeStruct((t, d), F32),
        compiler_params=_params("arbitrary"),
        name="mlstm_out",
    )(x, hs_ctx, hs_lat, o_gate, head_g, w, gate)


def _mlstm_proj_kernel(x_ref, g_ref, sh_ref, sc_ref, wqk_ref, wv_ref, wo_ref, wgt_ref, bg_ref,
                       wgtt_ref, bgt_ref, qs_ref, qkv_ref, o_ref, gc_ref, gr_ref):
    h = _modulated(x_ref[...], g_ref[...], sh_ref[...], sc_ref[...]).astype(BF16)
    nqk = wqk_ref.shape[1]
    qkv_ref[:, :nqk] = (jnp.dot(h, wqk_ref[...].astype(BF16), preferred_element_type=F32)
                        * qs_ref[...]).astype(BF16)
    qkv_ref[:, nqk:] = jnp.dot(h, wv_ref[...].astype(BF16),
                               preferred_element_type=F32).astype(BF16)
    o_ref[...] = jnp.dot(h, wo_ref[...].astype(BF16), preferred_element_type=F32)
    gc_ref[...] = jnp.dot(h, wgt_ref[...].astype(BF16), preferred_element_type=F32) + bg_ref[...]
    gr_ref[...] = _bdot_nt(wgtt_ref[...], h) + bgt_ref[...]


def mlstm_proj(x, g, shift, scale, w_in_all, layer, b_gate, n_ctx, dec_seq, tm=512):
    t, d = x.shape
    hh = MLSTM_HEADS
    hv = d
    hk = hv // 2
    ng = 4 * hh
    w_g = w_in_all[layer, :, 2 * hk + 2 * hv:]
    w_g_pad = jnp.pad(w_g, ((0, 0), (0, LANES - ng)))
    b_pad = jnp.pad(b_gate.reshape(1, ng), ((0, 0), (0, LANES - ng)))
    w_g_t = w_g.T
    b_t = b_gate.reshape(ng, 1)
    dk = hk // hh
    q_scale = jnp.concatenate([jnp.full((1, hk), dk ** -0.5, F32), jnp.ones((1, hk), F32)], axis=1)
    cidx = _cvec_index(n_ctx, dec_seq, tm)
    mod_spec = pl.BlockSpec((None, 1, d), lambda i: (cidx(i), 0, 0))
    full = lambda a: pl.BlockSpec(a.shape, lambda i: (0,) * a.ndim)
    assert 2 * hk == hv
    w_col = lambda n: pl.BlockSpec((None, d, hv), lambda i: (layer, 0, n))
    return pl.pallas_call(
        _mlstm_proj_kernel,
        grid=(t // tm,),
        in_specs=[pl.BlockSpec((tm, d), lambda i: (i, 0)), full(g), mod_spec, mod_spec,
                  w_col(0), w_col(1), w_col(2), full(w_g_pad), full(b_pad), full(w_g_t),
                  full(b_t), full(q_scale)],
        out_specs=[pl.BlockSpec((tm, 2 * hk + hv), lambda i: (i, 0)),
                   pl.BlockSpec((tm, hv), lambda i: (i, 0)),
                   pl.BlockSpec((tm, LANES), lambda i: (i, 0)),
                   pl.BlockSpec((ng, tm), lambda i: (0, i))],
        out_shape=[jax.ShapeDtypeStruct((t, 2 * hk + hv), BF16),
                   jax.ShapeDtypeStruct((t, hv), F32),
                   jax.ShapeDtypeStruct((t, LANES), F32),
                   jax.ShapeDtypeStruct((ng, t), F32)],
        compiler_params=_params("arbitrary"),
        name="mlstm_proj",
    )(x, g, shift, scale, w_in_all, w_in_all, w_in_all, w_g_pad, b_pad, w_g_t, b_t, q_scale)


def _log_sigmoid(x):
    return jnp.minimum(x, 0.0) - jnp.log(1.0 + jnp.exp(-jnp.abs(x)))


def _mlstm_chunk(q, k, v, i_col, f_col, i_row, f_row, c_st, n_st, m_st, rev):
    l = q.shape[0]
    lf_col = _log_sigmoid(f_col)
    lf_row = _log_sigmoid(f_row)
    tt = lax.broadcasted_iota(jnp.int32, (l, l), 0)
    ss = lax.broadcasted_iota(jnp.int32, (l, l), 1)
    causal = (ss >= tt) if rev else (ss <= tt)
    b_col = jnp.sum(jnp.where(causal, lf_row, 0.0), axis=1, keepdims=True)
    feeds = (tt >= ss) if rev else (tt <= ss)
    b_row = jnp.sum(jnp.where(feeds, lf_col, 0.0), axis=0, keepdims=True)
    log_d = jnp.where(causal, b_col - b_row + i_row, -jnp.inf)
    log_state = b_col + m_st
    m_t = jnp.maximum(log_state, jnp.max(log_d, axis=1, keepdims=True))
    dw = jnp.exp(log_d - m_t)
    sw = jnp.exp(log_state - m_t)
    a = _bdot_nt(q, k) * dw
    num = sw * _bdot(q, c_st) + _bdot(a, v)
    qn = jnp.sum(q.astype(F32) * n_st, axis=1, keepdims=True)
    den = sw * qn + jnp.sum(a, axis=1, keepdims=True)
    h = num / jnp.maximum(jnp.abs(den), jnp.exp(-m_t))
    b_last = b_col[0:1, :] if rev else b_col[l - 1:l, :]
    log_k = b_last - b_col + i_col
    m_new = jnp.maximum(b_last + m_st, jnp.max(log_k, axis=0, keepdims=True))
    kw = jnp.exp(log_k - m_new)
    decay = jnp.exp(b_last + m_st - m_new)
    kwk = kw * k.astype(F32)
    c_new = decay * c_st + _bdot_tn(kwk, v)
    n_new = decay * n_st + jnp.sum(kwk, axis=0, keepdims=True)
    return h, c_new, n_new, m_new


def _mlstm_scan_kernel(*refs, n_chunks, zero_init):
    if zero_init:
        q_ref, k_ref, v_ref, gc_ref, gr_ref, hs_ref, c_ref, n_ref, m_ref, hb_scr = refs
        c_ref[...] = jnp.zeros_like(c_ref)
        n_ref[...] = jnp.zeros_like(n_ref)
        m_ref[...] = jnp.zeros_like(m_ref)
    else:
        (q_ref, k_ref, v_ref, gc_ref, gr_ref, c0_ref, n0_ref, m0_ref,
         hs_ref, c_ref, n_ref, m_ref, hb_scr) = refs
        c_ref[...] = c0_ref[...]
        n_ref[...] = n0_ref[...]
        m_ref[...] = m0_ref[...]
    l = MLSTM_CHUNK
    hh = MLSTM_HEADS
    dk = q_ref.shape[1] // hh
    dv = v_ref.shape[1] // hh

    def body(c, carry):
        cr = n_chunks - 1 - c
        rows = (pl.ds(pl.multiple_of(c * l, l), l), pl.ds(pl.multiple_of(cr * l, l), l))
        grs = (gr_ref[c], gr_ref[cr])
        outs, states = ([], []), []
        for d in range(2):
            q, k, v, gc, gr = q_ref[rows[d], :], k_ref[rows[d], :], v_ref[rows[d], :], gc_ref[rows[d], :], grs[d]
            for h in range(hh):
                ci, cf = 2 * d * hh + h, (2 * d + 1) * hh + h
                o, *st = _mlstm_chunk(q[:, h * dk:(h + 1) * dk], k[:, h * dk:(h + 1) * dk],
                                      v[:, h * dv:(h + 1) * dv], gc[:, ci:ci + 1], gc[:, cf:cf + 1],
                                      gr[ci:ci + 1, :], gr[cf:cf + 1, :],
                                      c_ref[d, h], n_ref[d, h], m_ref[d, h], d == 1)
                outs[d].append(o)
                states.append((d, h, st))
        hs_ref[rows[0], :] = jnp.concatenate(outs[0], axis=1)
        hb_scr[rows[1], :] = jnp.concatenate(outs[1], axis=1)
        for d, h, (c_new, n_new, m_new) in states:
            c_ref[d, h], n_ref[d, h], m_ref[d, h] = c_new, n_new, m_new
        return carry

    lax.fori_loop(0, n_chunks, body, 0)
    for h in range(hh):
        cols = pl.ds(h * dv, dv)
        hs = hs_ref[:, cols] + hb_scr[:, cols]
        hs_ref[:, cols] = hs * lax.rsqrt(jnp.mean(hs * hs, axis=-1, keepdims=True) + NORM_EPS)


def mlstm_scan(qkv, gcol, grow, init, row_off, n_seq, seq_len):
    hh = MLSTM_HEADS
    hv = qkv.shape[1] // 2
    dv = hv // hh
    dk = dv // 2
    l = MLSTM_CHUNK
    nc = seq_len // l
    ob = row_off // seq_len
    kern = functools.partial(_mlstm_scan_kernel, n_chunks=nc, zero_init=init is None)
    st = lambda *tail: pl.BlockSpec((None, 2, hh) + tail, lambda s: (s, 0, 0) + (0,) * len(tail))
    states = [st(dk, dv), st(1, dk), st(1, 1)]
    return pl.pallas_call(
        kern,
        grid=(n_seq,),
        in_specs=[pl.BlockSpec((seq_len, hh * dk), lambda s: (ob + s, 0)),
                  pl.BlockSpec((seq_len, hh * dk), lambda s: (ob + s, 1)),
                  pl.BlockSpec((seq_len, hv), lambda s: (ob + s, 1)),
                  pl.BlockSpec((seq_len, LANES), lambda s: (ob + s, 0)),
                  pl.BlockSpec((nc, 4 * hh, l), lambda s: (ob + s, 0, 0))]
                 + ([] if init is None else states),
        out_specs=[pl.BlockSpec((seq_len, hv), lambda s: (s, 0))] + states,
        out_shape=[jax.ShapeDtypeStruct((n_seq * seq_len, hv), F32),
                   jax.ShapeDtypeStruct((n_seq, 2, hh, dk, dv), F32),
                   jax.ShapeDtypeStruct((n_seq, 2, hh, 1, dk), F32),
                   jax.ShapeDtypeStruct((n_seq, 2, hh, 1, 1), F32)],
        scratch_shapes=[pltpu.VMEM((seq_len, hv), F32)],
        compiler_params=_params("arbitrary"),
        name="mlstm_scan",
    )(qkv, qkv, qkv, gcol, grow, *(() if init is None else init))


def _mla_proj_kernel(x_ref, g_ref, sh_ref, sc_ref, win_ref, qg_ref, kvg_ref, wqb_ref, cos_ref,
                     sin_ref, q_ref, ckv_ref, kpe_ref, *, q_lora, kv_lora, rope, n_heads):
    h = _modulated(x_ref[...], g_ref[...], sh_ref[...], sc_ref[...])
    proj = _bdot(h, win_ref[...])
    q_lat = proj[:, :q_lora]
    ckv_ref[...] = _rms(proj[:, q_lora:q_lora + kv_lora], kvg_ref[...])
    cos, sin = cos_ref[...], sin_ref[...]
    kpe = proj[:, q_lora + kv_lora:q_lora + kv_lora + rope]
    kpe_rot = proj[:, q_lora + kv_lora + rope:q_lora + kv_lora + 2 * rope]
    kpe_ref[...] = kpe * cos[:, :rope] + kpe_rot * sin[:, :rope]
    q = _bdot(_rms(q_lat, qg_ref[...]), wqb_ref[...])
    n = n_heads * LANES
    pieces = []
    for hd in range(n_heads):
        c = slice(hd * LANES, (hd + 1) * LANES)
        pieces.append(q[:, c])
        pieces.append(q[:, n:2 * n][:, c] * cos + q[:, 2 * n:][:, c] * sin)
    q_ref[...] = jnp.concatenate(pieces, axis=1).astype(BF16)


def _rot_cols(w):
    n = w.shape[1]
    j = np.arange(n)
    quarter = MLA_ROPE // 4
    first = (j % (2 * quarter)) < quarter
    src = np.where(first, j + quarter, j - quarter)
    sign = np.where(first, -1.0, 1.0).astype(np.float32)
    return w[:, src] * sign


def _rope_tables(n_ctx, dec_batch, dec_seq):
    quarter = MLA_ROPE // 4
    freq = np.power(np.float32(ROPE_BASE), -np.arange(quarter, dtype=np.float32) / np.float32(quarter))
    pos = np.arange(dec_seq)
    ang_r = (pos // GRID_W).astype(np.float32)[:, None] * freq[None, :]
    ang_c = (pos % GRID_W).astype(np.float32)[:, None] * freq[None, :]
    ang = np.concatenate([ang_r, ang_r, ang_c, ang_c], axis=1).astype(np.float32)
    cos = np.concatenate([np.ones((n_ctx, MLA_ROPE), np.float32)] + [np.cos(ang)] * dec_batch, axis=0)
    sin = np.concatenate([np.zeros((n_ctx, MLA_ROPE), np.float32)] + [np.sin(ang)] * dec_batch, axis=0)
    reps = LANES // MLA_ROPE
    return (jnp.asarray(np.tile(cos, (1, reps)), F32), jnp.asarray(np.tile(sin, (1, reps)), F32))


def mla_proj(x, g, shift, scale, w_in, q_g, kv_g, w_qb, cos, sin, n_ctx, dec_seq, tm=512):
    t, d = x.shape
    hh, nope, rope = MLA_HEADS, MLA_NOPE, MLA_ROPE
    q_lora = q_g.shape[1]
    kv_lora = kv_g.shape[1]
    w_in_ext = jnp.concatenate([w_in, _rot_cols(w_in[:, q_lora + kv_lora:])], axis=1)
    w3 = w_qb.reshape(q_lora, hh, nope + rope)
    w_qn = w3[:, :, :nope].reshape(q_lora, hh * nope)
    w_qp = w3[:, :, nope:].reshape(q_lora, hh * rope)
    assert nope == LANES and rope <= LANES
    lane_pad = lambda w: jnp.pad(w.reshape(q_lora, hh, rope),
                                 ((0, 0), (0, 0), (0, LANES - rope))).reshape(q_lora, hh * LANES)
    w_qb_ext = jnp.concatenate([w_qn, lane_pad(w_qp), lane_pad(_rot_cols(w_qp))], axis=1)
    cidx = _cvec_index(n_ctx, dec_seq, tm)
    mod_spec = pl.BlockSpec((None, 1, d), lambda i: (cidx(i), 0, 0))
    full = lambda a: pl.BlockSpec(a.shape, lambda i: (0,) * a.ndim)
    kern = functools.partial(_mla_proj_kernel, q_lora=q_lora, kv_lora=kv_lora, rope=rope, n_heads=hh)
    return pl.pallas_call(
        kern,
        grid=(t // tm,),
        in_specs=[pl.BlockSpec((tm, d), lambda i: (i, 0)), full(g), mod_spec, mod_spec,
                  full(w_in_ext), full(q_g), full(kv_g), full(w_qb_ext),
                  pl.BlockSpec((tm, LANES), lambda i: (i, 0)),
                  pl.BlockSpec((tm, LANES), lambda i: (i, 0))],
        out_specs=[pl.BlockSpec((tm, 2 * hh * LANES), lambda i: (i, 0)),
                   pl.BlockSpec((tm, kv_lora), lambda i: (i, 0)),
                   pl.BlockSpec((tm, rope), lambda i: (i, 0))],
        out_shape=[jax.ShapeDtypeStruct((t, 2 * hh * LANES), BF16),
                   jax.ShapeDtypeStruct((t, kv_lora), F32),
                   jax.ShapeDtypeStruct((t, rope), F32)],
        compiler_params=_params("arbitrary"),
        name="mla_proj",
    )(x, g, shift, scale, w_in_ext, q_g, kv_g, w_qb_ext, cos, sin)


def _mla_kv_kernel(ckv_ref, kp_ref, w_ref, k_ref, v_ref):
    kv = _bdot(ckv_ref[...], w_ref[...])
    n = v_ref.shape[1]
    kp = kp_ref[...]
    pieces = []
    for hd in range(n // LANES):
        pieces += [kv[:, hd * LANES:(hd + 1) * LANES].astype(BF16), kp]
    k_ref[...] = jnp.concatenate(pieces, axis=1)
    v_ref[...] = kv[:, n:].astype(BF16)


def mla_kv(ckv_all, kp_pad, w_kvb, tm=512):
    r, kv_lora = ckv_all.shape
    hh, nope, vd = MLA_HEADS, MLA_NOPE, MLA_V
    w3 = w_kvb.reshape(kv_lora, hh, nope + vd)
    w_perm = jnp.concatenate([w3[:, :, :nope].reshape(kv_lora, hh * nope),
                              w3[:, :, nope:].reshape(kv_lora, hh * vd)], axis=1)
    return pl.pallas_call(
        _mla_kv_kernel,
        grid=(r // tm,),
        in_specs=[pl.BlockSpec((tm, kv_lora), lambda i: (i, 0)),
                  pl.BlockSpec((tm, LANES), lambda i: (i, 0)),
                  pl.BlockSpec(w_perm.shape, lambda i: (0, 0))],
        out_specs=[pl.BlockSpec((tm, 2 * hh * LANES), lambda i: (i, 0)),
                   pl.BlockSpec((tm, hh * vd), lambda i: (i, 0))],
        out_shape=[jax.ShapeDtypeStruct((r, 2 * hh * LANES), BF16),
                   jax.ShapeDtypeStruct((r, hh * vd), BF16)],
        compiler_params=_params("arbitrary"),
        name="mla_kv",
    )(ckv_all, kp_pad, w_perm)


def _attn_kernel(q_ref, k_ref, v_ref, o_ref, *, scale):
    c = scale * np.log2(np.e)
    for h in range(MLA_HEADS):
        hk = slice(2 * h * LANES, 2 * (h + 1) * LANES)
        s = _bdot_nt(q_ref[:, hk], k_ref[:, hk])
        e = jnp.exp2((s - jnp.max(s, axis=-1, keepdims=True)) * c)
        o = _bdot(e, v_ref[:, h * MLA_V:(h + 1) * MLA_V]) / jnp.sum(e, axis=-1, keepdims=True)
        o_ref[:, h * MLA_V:(h + 1) * MLA_V] = o.astype(BF16)


def mla_attention(q, k, v, q_row_off, k_row_off, n_seq, q_len, k_len):
    tq = ATTN_Q_BLOCK
    qb = q_len // tq
    q0 = q_row_off // tq
    k0 = k_row_off // k_len
    dq, dv = q.shape[1], v.shape[1]
    kern = functools.partial(_attn_kernel, scale=(MLA_NOPE + MLA_ROPE) ** -0.5)
    return pl.pallas_call(
        kern,
        grid=(n_seq, qb),
        in_specs=[pl.BlockSpec((tq, dq), lambda s, j: (q0 + s * qb + j, 0)),
                  pl.BlockSpec((k_len, dq), lambda s, j: (k0 + s, 0)),
                  pl.BlockSpec((k_len, dv), lambda s, j: (k0 + s, 0))],
        out_specs=pl.BlockSpec((tq, dv), lambda s, j: (s * qb + j, 0)),
        out_shape=jax.ShapeDtypeStruct((n_seq * q_len, dv), BF16),
        compiler_params=_params("arbitrary", "arbitrary"),
        name="mla_attention",
    )(q, k, v)


def kernel(x_prompt, x_sample, state_mlstm_C, state_mlstm_n, state_mlstm_m, cache_mla_ckv,
           cache_mla_kpe, c, c_ctx, w_ada, b_ada, norm_mix, norm_ffn, norm_final, w_pool,
           pool_scale, w_mlstm_in, b_mlstm_gate, mlstm_head_g, w_mlstm_out, w_mla_in, mla_q_g,
           mla_kv_g, w_mla_qb, w_mla_kvb, w_mla_out, w_router, b_router, w_exp_gate, w_exp_up,
           w_exp_down):
    batch, seq, d = x_prompt.shape
    dec_batch, dec_seq, _ = x_sample.shape
    depth = w_ada.shape[0]
    n_ctx = batch * seq
    n_lat = dec_batch * dec_seq
    hh = MLSTM_HEADS
    past = cache_mla_ckv.shape[2]

    x = (x_prompt.reshape(n_ctx, d), x_sample.reshape(n_lat, d))

    n_cv = 1 + dec_batch
    cvecs = jnp.concatenate([c_ctx[None, :], c, jnp.zeros((SUBLANES - n_cv % SUBLANES, d), F32)], axis=0)
    mod = ada_mod_all(cvecs, w_ada, b_ada).reshape(depth, cvecs.shape[0], 6, 1, d)

    w_router_pad = jnp.pad(w_router, ((0, 0), (0, LANES - N_EXPERTS)))
    w_router_hi = w_router_pad.astype(BF16)
    w_router_lo = (w_router_pad - w_router_hi.astype(F32)).astype(BF16)
    w_router_split = jnp.concatenate([w_router_hi, w_router_lo], axis=1)
    b_router_col = b_router.reshape(N_EXPERTS, 1)
    g_final = norm_final.reshape(1, d)
    row = lambda a: a.reshape(1, -1)

    outs = {}
    for i in range(depth):
        kind, j = i % 3, i // 3
        m = [mod[i, :n_cv, k] for k in range(6)]
        g_mix = row(norm_mix[i])
        if kind == 0:
            x = pool_layer(x, g_mix, m[0], m[1], m[2], w_pool[j], row(pool_scale[j]),
                           n_ctx, seq, dec_seq)
        elif kind == 1:
            qkv, o_gate, gcol, grow = mlstm_proj(x, g_mix, m[0], m[1], w_mlstm_in, j,
                                                 b_mlstm_gate[j], n_ctx, dec_seq)
            t = n_ctx + n_lat
            l = MLSTM_CHUNK
            grow_c = grow.reshape(4 * hh, t // l, l).transpose(1, 0, 2)
            hs_c, c_new, n_new, m_new = mlstm_scan(qkv, gcol, grow_c, None, 0, batch, seq)
            init = (state_mlstm_C[:, j], state_mlstm_n[:, j][:, :, :, None, :],
                    state_mlstm_m[:, j][:, :, :, None, None])
            hs_l, _, _, _ = mlstm_scan(qkv, gcol, grow_c, init, n_ctx, dec_batch, dec_seq)
            outs["C"] = c_new[:, None]
            outs["n"] = n_new[:, None, :, :, 0, :]
            outs["m"] = m_new[:, None, :, :, 0, 0]
            x = mlstm_out(x, hs_c, hs_l, o_gate, row(mlstm_head_g[j]), w_mlstm_out[j], m[2],
                          n_ctx, dec_seq)
        else:
            cos, sin = _rope_tables(n_ctx, dec_batch, dec_seq)
            q_cat, ckv, kpe = mla_proj(x, g_mix, m[0], m[1], w_mla_in[j], row(mla_q_g[j]),
                                       row(mla_kv_g[j]), w_mla_qb[j], cos, sin, n_ctx, dec_seq)
            lat_parts_c, lat_parts_p = [], []
            for b in range(dec_batch):
                lo = n_ctx + b * dec_seq
                lat_parts_c += [cache_mla_ckv[b, j], ckv[lo:lo + dec_seq]]
                lat_parts_p += [cache_mla_kpe[b, j], kpe[lo:lo + dec_seq]]
            ckv_all = jnp.concatenate(lat_parts_c + [ckv[:n_ctx]], axis=0)
            kp_all = jnp.concatenate(lat_parts_p + [kpe[:n_ctx]], axis=0).astype(BF16)
            kp_pad = jnp.pad(kp_all, ((0, 0), (0, LANES - kp_all.shape[1])))
            k_cat, v = mla_kv(ckv_all, kp_pad, w_mla_kvb[j])
            k_lat = past + dec_seq
            o_c = mla_attention(q_cat, k_cat, v, 0, dec_batch * k_lat, batch, seq, seq)
            o_l = mla_attention(q_cat, k_cat, v, n_ctx, 0, dec_batch, dec_seq, k_lat)
            outs["ckv"] = ckv[:n_ctx].reshape(batch, 1, seq, -1)
            outs["kpe"] = kpe[:n_ctx].reshape(batch, 1, seq, -1)
            x = resid_proj(x, o_c, o_l, w_mla_out[j], m[2], n_ctx, dec_seq)
        x = moe_layer(x, row(norm_ffn[i]), m[3], m[4], m[5], w_router_split, b_router_col,
                      w_exp_gate, w_exp_up, w_exp_down, i, g_final, n_ctx, dec_seq,
                      final_norm=(i == depth - 1), split_output=(i == depth - 1))

    y_prompt = x[0].reshape(batch, seq, d)
    y_sample = x[1].reshape(dec_batch, dec_seq, d)
    return (y_prompt, y_sample, outs["C"], outs["n"], outs["m"], outs["ckv"], outs["kpe"])
```

```python
import functools

import numpy as np
import jax
import jax.numpy as jnp
from jax import lax
from jax.experimental import pallas as pl
from jax.experimental.pallas import tpu as pltpu

F32 = jnp.float32
BF16 = jnp.bfloat16

NORM_EPS = 1e-6
GRID_W = 64
POOL_WINDOWS = (2, 4, 8, 16)
MLSTM_HEADS = 4
MLSTM_CHUNK = 256
MLA_HEADS = 8
MLA_NOPE = 128
MLA_ROPE = 64
MLA_V = 128
ROPE_BASE = 10000.0
N_EXPERTS = 16
N_EXPERT_GROUPS = 4
EXPERTS_PER_GROUP = N_EXPERTS // N_EXPERT_GROUPS

LANES = 128
SUBLANES = 8
VMEM_LIMIT = 56 * 1024 * 1024
POOL_TILE = 256
POOL_HALO = 8
ATTN_Q_BLOCK = 256
MOE_TILE = 512
MOE_SEG_ALIGN = 16
MOE_ROW_BLOCK = 160
MOE_WIDE_BLOCK = 224


def _params(*sem):
    return pltpu.CompilerParams(dimension_semantics=sem, vmem_limit_bytes=VMEM_LIMIT)


def _rms(x, g):
    return x * lax.rsqrt(jnp.mean(x * x, axis=-1, keepdims=True) + NORM_EPS) * g


def _modulated(x, g, shift, scale):
    return _rms(x, g) * (1.0 + scale) + shift


def _silu(x):
    return x * jax.nn.sigmoid(x)


def _bdot(a, b):
    return jnp.dot(a.astype(BF16), b.astype(BF16), preferred_element_type=F32)


def _bdot_nt(a, b):
    return lax.dot_general(a.astype(BF16), b.astype(BF16), (((1,), (1,)), ((), ())),
                           preferred_element_type=F32)


def _bdot_tn(a, b):
    return lax.dot_general(a.astype(BF16), b.astype(BF16), (((0,), (0,)), ((), ())),
                           preferred_element_type=F32)


def _cvec_index(n_ctx, dec_seq, tm):
    def idx(i):
        r = i * tm
        return jnp.where(r < n_ctx, 0, (r - n_ctx) // dec_seq + 1)
    return idx


def _ada_kernel(c_ref, w_ref, b_ref, o_ref):
    o_ref[...] = _bdot(_silu(c_ref[...]), w_ref[...]) + b_ref[...]


def ada_mod_all(cvecs, w_ada, b_ada, tn=1536):
    depth, d, n6 = w_ada.shape
    rows = cvecs.shape[0]
    return pl.pallas_call(
        _ada_kernel,
        grid=(depth, n6 // tn),
        in_specs=[pl.BlockSpec((rows, d), lambda l, n: (0, 0)),
                  pl.BlockSpec((None, d, tn), lambda l, n: (l, 0, n)),
                  pl.BlockSpec((None, 1, tn), lambda l, n: (l, 0, n))],
        out_specs=pl.BlockSpec((None, rows, tn), lambda l, n: (l, 0, n)),
        out_shape=jax.ShapeDtypeStruct((depth, rows, n6), F32),
        compiler_params=_params("arbitrary", "arbitrary"),
        name="ada_mod",
    )(cvecs, w_ada, b_ada.reshape(depth, 1, n6))


def _pool_kernel(*refs, n_ctx_tiles, ctx_seq_tiles, lat_seq_tiles, split_input):
    n_x = 6 if split_input else 3
    x_refs = refs[:n_x]
    g_ref, sh_ref, sc_ref, gt_ref, wp_ref, ps_ref, o_ref, buf_ref = refs[n_x:]
    i = pl.program_id(0)
    is_ctx = i < n_ctx_tiles
    j = jnp.where(is_ctx, i % ctx_seq_tiles, (i - n_ctx_tiles) % lat_seq_tiles)
    nt = jnp.where(is_ctx, ctx_seq_tiles, lat_seq_tiles)
    g, sh, sc = g_ref[...], sh_ref[...], sc_ref[...]
    tp, hl = POOL_TILE, POOL_HALO
    gw = o_ref.shape[1] // len(POOL_WINDOWS)

    def fill(xc_ref, xp_ref, xn_ref):
        buf_ref[pl.ds(0, hl), :] = jnp.where(j == 0, 0.0, _modulated(xp_ref[...], g, sh, sc))
        buf_ref[pl.ds(hl, tp), :] = _modulated(xc_ref[...], g, sh, sc)
        buf_ref[pl.ds(hl + tp, hl), :] = jnp.where(j == nt - 1, 0.0,
                                                    _modulated(xn_ref[...], g, sh, sc))
        o_ref[...] = xc_ref[...]

    if split_input:
        pl.when(is_ctx)(lambda: fill(*x_refs[:3]))
        pl.when(jnp.logical_not(is_ctx))(lambda: fill(*x_refs[3:]))
    else:
        fill(*x_refs)

    pos = j * tp + lax.broadcasted_iota(jnp.int32, (tp, 1), 0)
    seq_len = nt * tp
    for gi, w in enumerate(POOL_WINDOWS):
        cols = pl.ds(gi * gw, gw)
        acc = buf_ref[pl.ds(hl - w // 2, tp), cols]
        for d in range(-w // 2 + 1, w // 2):
            acc = acc + buf_ref[pl.ds(hl + d, tp), cols]
        cnt = jnp.minimum(pos + w // 2, seq_len) - jnp.maximum(pos - w // 2, 0)
        pooled = acc / cnt.astype(F32) - buf_ref[pl.ds(hl, tp), cols]
        y = _bdot(pooled, wp_ref[gi]) * ps_ref[:, cols]
        o_ref[:, cols] = o_ref[:, cols] + gt_ref[:, cols] * y


def _halo_specs(tp, hl, d, tile_off, n_rows):
    hb = tp // hl
    last_tile, last_hblk = n_rows // tp - 1, n_rows // hl - 1
    tile = lambda i: jnp.clip(i - tile_off, 0, last_tile)
    return [pl.BlockSpec((tp, d), lambda i: (tile(i), 0)),
            pl.BlockSpec((hl, d), lambda i: (jnp.clip(tile(i) * hb - 1, 0, last_hblk), 0)),
            pl.BlockSpec((hl, d), lambda i: (jnp.clip((tile(i) + 1) * hb, 0, last_hblk), 0))]


def pool_layer(xs, g, shift, scale, gate, w_pool, pool_scale, n_ctx, seq, dec_seq):
    split = isinstance(xs, tuple)
    tp, hl = POOL_TILE, POOL_HALO
    if split:
        d = xs[0].shape[1]
        t = xs[0].shape[0] + xs[1].shape[0]
        x_specs = (_halo_specs(tp, hl, d, 0, xs[0].shape[0])
                   + _halo_specs(tp, hl, d, n_ctx // tp, xs[1].shape[0]))
        x_args = (xs[0],) * 3 + (xs[1],) * 3
    else:
        t, d = xs.shape
        x_specs = _halo_specs(tp, hl, d, 0, t)
        x_args = (xs,) * 3
    cidx = _cvec_index(n_ctx, dec_seq, tp)
    mod_spec = pl.BlockSpec((None, 1, d), lambda i: (cidx(i), 0, 0))
    row_spec = pl.BlockSpec((1, d), lambda i: (0, 0))
    kern = functools.partial(_pool_kernel, n_ctx_tiles=n_ctx // tp, ctx_seq_tiles=seq // tp,
                             lat_seq_tiles=dec_seq // tp, split_input=split)
    return pl.pallas_call(
        kern,
        grid=(t // tp,),
        in_specs=x_specs + [row_spec, mod_spec, mod_spec, mod_spec,
                            pl.BlockSpec(w_pool.shape, lambda i: (0, 0, 0)), row_spec],
        out_specs=pl.BlockSpec((tp, d), lambda i: (i, 0)),
        out_shape=jax.ShapeDtypeStruct((t, d), F32),
        scratch_shapes=[pltpu.VMEM((tp + 2 * hl, d), F32)],
        compiler_params=_params("arbitrary"),
        name="pool_mixer",
    )(*x_args, g, shift, scale, gate, w_pool, pool_scale)


def _route(sel, scores):
    e, tm = sel.shape
    row = lax.broadcasted_iota(jnp.int32, (e, tm), 0)
    best = jnp.zeros((1, tm), jnp.int32)
    best_sc = None
    for gidx in range(N_EXPERT_GROUPS):
        r = [sel[gidx * EXPERTS_PER_GROUP + k:gidx * EXPERTS_PER_GROUP + k + 1, :]
             for k in range(EXPERTS_PER_GROUP)]
        top2 = None
        for a in range(EXPERTS_PER_GROUP):
            for b in range(a + 1, EXPERTS_PER_GROUP):
                s = r[a] + r[b]
                top2 = s if top2 is None else jnp.maximum(top2, s)
        if best_sc is None:
            best_sc = top2
        else:
            better = top2 > best_sc
            best = jnp.where(better, gidx, best)
            best_sc = jnp.where(better, top2, best_sc)
    neg = -jnp.inf
    masked = jnp.where(row // EXPERTS_PER_GROUP == best, sel, neg)
    m1 = jnp.max(masked, axis=0, keepdims=True)
    i1 = jnp.min(jnp.where(masked == m1, row, e), axis=0, keepdims=True)
    masked2 = jnp.where(row == i1, neg, masked)
    m2 = jnp.max(masked2, axis=0, keepdims=True)
    i2 = jnp.min(jnp.where(masked2 == m2, row, e), axis=0, keepdims=True)
    hot1 = row == i1
    hot2 = row == i2
    w1 = jnp.sum(jnp.where(hot1, scores, 0.0), axis=0, keepdims=True)
    w2 = jnp.sum(jnp.where(hot2, scores, 0.0), axis=0, keepdims=True)
    tot = w1 + w2
    return best, jnp.where(hot1, w1 / tot, 0.0) + jnp.where(hot2, w2 / tot, 0.0)


def _split_bf16(a, parts):
    out = []
    for _ in range(parts):
        p = a.astype(BF16)
        out.append(p)
        a = a - p.astype(F32)
    return out


def _pad_rows(a, rows):
    return jnp.concatenate([a, jnp.zeros((rows - a.shape[0], a.shape[1]), a.dtype)], axis=0)


def _moe_kernel(x_ref, g_ref, sh_ref, sc_ref, gt_ref, wr_ref, br_ref, wgf_ref, wuf_ref, wdf_ref,
                gf_ref, *rest, final_norm, n_ctx_tiles, n_experts):
    out_refs, (wg_ref, wu_ref, wd_ref), scratch = rest[:-6], rest[-6:-3], rest[-3:]
    i = pl.program_id(0)

    @pl.when(i < n_experts)
    def _():
        wg_ref[i] = wgf_ref[...].astype(BF16)
        wu_ref[i] = wuf_ref[...].astype(BF16)
        wd_ref[i] = wdf_ref[...].astype(BF16)

    @pl.when(i >= n_experts)
    def _():
        _moe_tile(x_ref, g_ref, sh_ref, sc_ref, gt_ref, wr_ref, br_ref, wg_ref, wu_ref, wd_ref,
                  gf_ref, out_refs, scratch, i - n_experts, final_norm, n_ctx_tiles)


def _moe_tile(x_ref, g_ref, sh_ref, sc_ref, gt_ref, wr_ref, br_ref, wg_ref, wu_ref, wd_ref,
              gf_ref, out_refs, scratch, tile, final_norm, n_ctx_tiles):
    hp_scr, cw_scr, yp_scr = scratch
    tr, d = x_ref.shape
    trp = hp_scr.shape[0]
    ng, eg = N_EXPERT_GROUPS, EXPERTS_PER_GROUP
    x = x_ref[...]
    h = _modulated(x, g_ref[...], sh_ref[...], sc_ref[...])
    hb = h.astype(BF16)

    h_lo = (h - hb.astype(F32)).astype(BF16)
    wr = wr_ref[...]
    lg = jnp.dot(hb, wr, preferred_element_type=F32)
    logits = lg[:, :LANES] + lg[:, LANES:] + jnp.dot(h_lo, wr[:, :LANES], preferred_element_type=F32)
    scores = jax.nn.sigmoid(logits.T[:N_EXPERTS, :])
    best, comb_t = _route(scores + br_ref[...], scores)

    grp = lax.broadcasted_iota(jnp.int32, (SUBLANES, tr), 0)
    hot_t = (grp == best).astype(F32)
    cw_t = hot_t[0:1, :] * comb_t[0:eg, :]
    for gi in range(1, ng):
        cw_t = cw_t + hot_t[gi:gi + 1, :] * comb_t[gi * eg:(gi + 1) * eg, :]
    cw_c = _pad_rows(cw_t, LANES).T

    ia = lax.broadcasted_iota(jnp.int32, (tr, tr), 0)
    ib = lax.broadcasted_iota(jnp.int32, (tr, tr), 1)
    before = jnp.where(ia < ib, 1.0, 0.0).astype(BF16)
    rank_t = jnp.dot(hot_t.astype(BF16), before, preferred_element_type=F32)

    starts, counts = [], []
    off = jnp.int32(0)
    for gi in range(ng):
        n = jnp.sum(hot_t[gi:gi + 1, :]).astype(jnp.int32)
        n = ((n + MOE_SEG_ALIGN - 1) // MOE_SEG_ALIGN) * MOE_SEG_ALIGN
        starts.append(off)
        counts.append(n)
        off = off + n

    pos_t = hot_t[0:1, :] * (rank_t[0:1, :] + starts[0].astype(F32))
    for gi in range(1, ng):
        pos_t = pos_t + hot_t[gi:gi + 1, :] * (rank_t[gi:gi + 1, :] + starts[gi].astype(F32))
    pos_c = _pad_rows(pos_t, LANES).T[:, 0:1]
    used = tr + ng * MOE_SEG_ALIGN
    perm = jnp.where(lax.broadcasted_iota(jnp.int32, (used, tr), 0) == pos_t.astype(jnp.int32),
                     1.0, 0.0).astype(BF16)
    perm_t = jnp.where(lax.broadcasted_iota(jnp.int32, (tr, used), 1) == pos_c.astype(jnp.int32),
                       1.0, 0.0).astype(BF16)

    hp_scr[pl.ds(0, used), :] = jnp.dot(perm, hb, preferred_element_type=F32).astype(BF16)
    hp_scr[pl.ds(used, trp - used), :] = jnp.zeros((trp - used, d), BF16)
    cw_pair = jnp.dot(perm, jnp.concatenate(_split_bf16(cw_c, 2), axis=1), preferred_element_type=F32)
    cw_scr[pl.ds(0, used), :] = cw_pair[:, :LANES] + cw_pair[:, LANES:]
    cw_scr[pl.ds(used, trp - used), :] = jnp.zeros((trp - used, LANES), F32)
    yp_scr[...] = jnp.zeros_like(yp_scr)

    def expert_block(gi, row0, sb):
        rows = pl.ds(pl.multiple_of(row0, MOE_SEG_ALIGN), sb)
        hblk = hp_scr[rows, :]
        cwb = cw_scr[rows, :]
        y = None
        for k in range(eg):
            e = gi * eg + k
            gate = jnp.dot(hblk, wg_ref[e], preferred_element_type=F32)
            up = jnp.dot(hblk, wu_ref[e], preferred_element_type=F32)
            hid = (_silu(gate) * up * cwb[:, k:k + 1]).astype(BF16)
            t = jnp.dot(hid, wd_ref[e], preferred_element_type=F32)
            y = t if y is None else y + t
        yp_scr[rows, :] = y

    sb, wide = MOE_ROW_BLOCK, MOE_WIDE_BLOCK
    for gi in range(ng):
        use_wide = jnp.logical_and(counts[gi] > sb, counts[gi] <= wide)
        pl.when(use_wide)(functools.partial(expert_block, gi, starts[gi], wide))
        n_blocks = jnp.where(use_wide, 0, (counts[gi] + sb - 1) // sb)

        def block(b, carry, gi=gi):
            expert_block(gi, starts[gi] + b * sb, sb)
            return carry
        lax.fori_loop(0, n_blocks, block, 0)

    y_hi, y_lo = _split_bf16(yp_scr[pl.ds(0, used), :], 2)
    moe = (jnp.dot(perm_t, y_hi, preferred_element_type=F32)
           + jnp.dot(perm_t, y_lo, preferred_element_type=F32))
    out = x + gt_ref[...] * moe
    if final_norm:
        out = _rms(out, gf_ref[...])
    if n_ctx_tiles is None:
        out_refs[0][...] = out
    else:
        is_ctx = tile < n_ctx_tiles

        @pl.when(is_ctx)
        def _():
            out_refs[0][...] = out

        @pl.when(jnp.logical_not(is_ctx))
        def _():
            out_refs[1][...] = out


def moe_layer(x, g, shift, scale, gate, w_router_split, b_router_col, wg, wu, wd, layer, g_final,
              n_ctx, dec_seq, final_norm, split_output=False, tm=MOE_TILE):
    t, d = x.shape
    _, n_e, _, f = wg.shape
    trp = tm + N_EXPERT_GROUPS * MOE_SEG_ALIGN + max(MOE_ROW_BLOCK, MOE_WIDE_BLOCK)
    tile = lambda i: jnp.maximum(i - n_e, 0)
    cidx = _cvec_index(n_ctx, dec_seq, tm)
    mod_spec = pl.BlockSpec((None, 1, d), lambda i: (cidx(tile(i)), 0, 0))
    row_spec = pl.BlockSpec((1, d), lambda i: (0, 0))
    expert = lambda i: jnp.minimum(i, n_e - 1)
    nct = n_ctx // tm
    if split_output:
        out_specs = [pl.BlockSpec((tm, d), lambda i: (jnp.minimum(tile(i), nct - 1), 0)),
                     pl.BlockSpec((tm, d), lambda i: (jnp.maximum(tile(i) - nct, 0), 0))]
        out_shape = [jax.ShapeDtypeStruct((n_ctx, d), F32), jax.ShapeDtypeStruct((t - n_ctx, d), F32)]
    else:
        out_specs = pl.BlockSpec((tm, d), lambda i: (tile(i), 0))
        out_shape = jax.ShapeDtypeStruct((t, d), F32)
    return pl.pallas_call(
        functools.partial(_moe_kernel, final_norm=final_norm,
                          n_ctx_tiles=nct if split_output else None, n_experts=n_e),
        grid=(n_e + t // tm,),
        in_specs=[pl.BlockSpec((tm, d), lambda i: (tile(i), 0)),
                  row_spec, mod_spec, mod_spec, mod_spec,
                  pl.BlockSpec(w_router_split.shape, lambda i: (0, 0)),
                  pl.BlockSpec(b_router_col.shape, lambda i: (0, 0)),
                  pl.BlockSpec((None, None, d, f), lambda i: (layer, expert(i), 0, 0)),
                  pl.BlockSpec((None, None, d, f), lambda i: (layer, expert(i), 0, 0)),
                  pl.BlockSpec((None, None, f, d), lambda i: (layer, expert(i), 0, 0)),
                  row_spec],
        out_specs=out_specs,
        out_shape=out_shape,
        scratch_shapes=[pltpu.VMEM((n_e, d, f), BF16), pltpu.VMEM((n_e, d, f), BF16),
                        pltpu.VMEM((n_e, f, d), BF16),
                        pltpu.VMEM((trp, d), BF16), pltpu.VMEM((trp, LANES), F32),
                        pltpu.VMEM((trp, d), F32)],
        compiler_params=_params("arbitrary"),
        name="moe",
    )(x, g, shift, scale, gate, w_router_split, b_router_col, wg, wu, wd, g_final)


def _per_stream(n_ctx_tiles, ctx_ref, lat_ref, fn):
    i = pl.program_id(0)

    @pl.when(i < n_ctx_tiles)
    def _():
        fn(ctx_ref[...])

    @pl.when(i >= n_ctx_tiles)
    def _():
        fn(lat_ref[...])


def _stream_specs(tm, k, n_ctx_tiles):
    return [pl.BlockSpec((tm, k), lambda i: (jnp.minimum(i, n_ctx_tiles - 1), 0)),
            pl.BlockSpec((tm, k), lambda i: (jnp.maximum(i - n_ctx_tiles, 0), 0))]


def _resid_proj_kernel(x_ref, ac_ref, al_ref, w_ref, gt_ref, o_ref, *, n_ctx_tiles):
    def run(a):
        o_ref[...] = x_ref[...] + gt_ref[...] * _bdot(a, w_ref[...])
    _per_stream(n_ctx_tiles, ac_ref, al_ref, run)


def _mlstm_out_kernel(x_ref, hc_ref, hl_ref, og_ref, hg_ref, w_ref, gt_ref, o_ref, *, n_ctx_tiles):
    def run(hs):
        a = jax.nn.sigmoid(og_ref[...]) * (hs * hg_ref[...])
        o_ref[...] = x_ref[...] + gt_ref[...] * _bdot(a, w_ref[...])
    _per_stream(n_ctx_tiles, hc_ref, hl_ref, run)


def resid_proj(x, a_ctx, a_lat, w, gate, n_ctx, dec_seq, tm=512):
    t, d = x.shape
    k = a_ctx.shape[1]
    cidx = _cvec_index(n_ctx, dec_seq, tm)
    nct = n_ctx // tm
    return pl.pallas_call(
        functools.partial(_resid_proj_kernel, n_ctx_tiles=nct),
        grid=(t // tm,),
        in_specs=[pl.BlockSpec((tm, d), lambda i: (i, 0))] + _stream_specs(tm, k, nct) + [
            pl.BlockSpec((k, d), lambda i: (0, 0)),
            pl.BlockSpec((None, 1, d), lambda i: (cidx(i), 0, 0))],
        out_specs=pl.BlockSpec((tm, d), lambda i: (i, 0)),
        out_shape=jax.ShapeDtypeStruct((t, d), F32),
        compiler_params=_params("arbitrary"),
        name="resid_proj",
    )(x, a_ctx, a_lat, w, gate)


def mlstm_out(x, hs_ctx, hs_lat, o_gate, head_g, w, gate, n_ctx, dec_seq, tm=512):
    t, d = x.shape
    k = hs_ctx.shape[1]
    cidx = _cvec_index(n_ctx, dec_seq, tm)
    nct = n_ctx // tm
    return pl.pallas_call(
        functools.partial(_mlstm_out_kernel, n_ctx_tiles=nct),
        grid=(t // tm,),
        in_specs=[pl.BlockSpec((tm, d), lambda i: (i, 0))] + _stream_specs(tm, k, nct) + [
            pl.BlockSpec((tm, k), lambda i: (i, 0)),
            pl.BlockSpec((1, k), lambda i: (0, 0)),
            pl.BlockSpec((k, d), lambda i: (0, 0)),
            pl.BlockSpec((None, 1, d), lambda i: (cidx(i), 0, 0))],
        out_specs=pl.BlockSpec((tm, d), lambda i: (i, 0)),
        out_shape=jax.ShapeDtypeStruct((t, d), F32),
        compiler_params=_params("arbitrary"),
        name="mlstm_out",
    )(x, hs_ctx, hs_lat, o_gate, head_g, w, gate)


def _mlstm_proj_kernel(x_ref, g_ref, sh_ref, sc_ref, wqk_ref, wv_ref, wo_ref, wgt_ref, bg_ref,
                       wgtt_ref, bgt_ref, qs_ref, qkv_ref, o_ref, gc_ref, gr_ref):
    h = _modulated(x_ref[...], g_ref[...], sh_ref[...], sc_ref[...]).astype(BF16)
    nqk = wqk_ref.shape[1]
    qkv_ref[:, :nqk] = (jnp.dot(h, wqk_ref[...].astype(BF16), preferred_element_type=F32)
                        * qs_ref[...]).astype(BF16)
    qkv_ref[:, nqk:] = jnp.dot(h, wv_ref[...].astype(BF16),
                               preferred_element_type=F32).astype(BF16)
    o_ref[...] = jnp.dot(h, wo_ref[...].astype(BF16), preferred_element_type=F32)
    gc_ref[...] = jnp.dot(h, wgt_ref[...].astype(BF16), preferred_element_type=F32) + bg_ref[...]
    gr_ref[...] = _bdot_nt(wgtt_ref[...], h) + bgt_ref[...]


def mlstm_proj(x, g, shift, scale, w_in_all, layer, b_gate, n_ctx, dec_seq, tm=512):
    t, d = x.shape
    hh = MLSTM_HEADS
    hv = d
    hk = hv // 2
    ng = 4 * hh
    w_g = w_in_all[layer, :, 2 * hk + 2 * hv:]
    w_g_pad = jnp.pad(w_g, ((0, 0), (0, LANES - ng)))
    b_pad = jnp.pad(b_gate.reshape(1, ng), ((0, 0), (0, LANES - ng)))
    w_g_t = w_g.T
    b_t = b_gate.reshape(ng, 1)
    dk = hk // hh
    q_scale = jnp.concatenate([jnp.full((1, hk), dk ** -0.5, F32), jnp.ones((1, hk), F32)], axis=1)
    cidx = _cvec_index(n_ctx, dec_seq, tm)
    mod_spec = pl.BlockSpec((None, 1, d), lambda i: (cidx(i), 0, 0))
    full = lambda a: pl.BlockSpec(a.shape, lambda i: (0,) * a.ndim)
    assert 2 * hk == hv
    w_col = lambda n: pl.BlockSpec((None, d, hv), lambda i: (layer, 0, n))
    return pl.pallas_call(
        _mlstm_proj_kernel,
        grid=(t // tm,),
        in_specs=[pl.BlockSpec((tm, d), lambda i: (i, 0)), full(g), mod_spec, mod_spec,
                  w_col(0), w_col(1), w_col(2), full(w_g_pad), full(b_pad), full(w_g_t),
                  full(b_t), full(q_scale)],
        out_specs=[pl.BlockSpec((tm, 2 * hk + hv), lambda i: (i, 0)),
                   pl.BlockSpec((tm, hv), lambda i: (i, 0)),
                   pl.BlockSpec((tm, LANES), lambda i: (i, 0)),
                   pl.BlockSpec((ng, tm), lambda i: (0, i))],
        out_shape=[jax.ShapeDtypeStruct((t, 2 * hk + hv), BF16),
                   jax.ShapeDtypeStruct((t, hv), F32),
                   jax.ShapeDtypeStruct((t, LANES), F32),
                   jax.ShapeDtypeStruct((ng, t), F32)],
        compiler_params=_params("arbitrary"),
        name="mlstm_proj",
    )(x, g, shift, scale, w_in_all, w_in_all, w_in_all, w_g_pad, b_pad, w_g_t, b_t, q_scale)


def _log_sigmoid(x):
    return jnp.minimum(x, 0.0) - jnp.log(1.0 + jnp.exp(-jnp.abs(x)))


def _gate_cumsums(gc, gr, causal_bf, feeds_bf):
    bc = br = None
    for part in _split_bf16(_log_sigmoid(gc), 3):
        t = jnp.dot(causal_bf, part, preferred_element_type=F32)
        bc = t if bc is None else bc + t
    for part in _split_bf16(_log_sigmoid(gr), 3):
        t = jnp.dot(part, feeds_bf, preferred_element_type=F32)
        br = t if br is None else br + t
    return bc, br


def _mlstm_chunk(q, k, v, i_col, b_col, i_row, b_row, c_st, n_st, m_st, causal, rev):
    l = q.shape[0]
    g_row = i_row - b_row
    log_w = jnp.where(causal, g_row, -jnp.inf)
    c_t = jnp.maximum(m_st, jnp.max(log_w, axis=1, keepdims=True))
    dw = jnp.exp(log_w - c_t)
    sw = jnp.exp(m_st - c_t)
    a = _bdot_nt(q, k) * dw
    a_hi, a_lo = _split_bf16(a, 2)
    ones = jnp.ones((l, LANES), BF16)
    row_sum = (jnp.dot(a_hi, ones, preferred_element_type=F32)
               + jnp.dot(a_lo, ones, preferred_element_type=F32))[:, 0:1]
    n_hi, n_lo = _split_bf16(jnp.broadcast_to(n_st, (LANES, n_st.shape[1])), 2)
    qn = (_bdot_nt(q, n_hi) + _bdot_nt(q, n_lo))[:, 0:1]
    den = sw * qn + row_sum
    inv = 1.0 / jnp.maximum(jnp.abs(den), jnp.exp(-(b_col + c_t)))
    h = (sw * inv) * _bdot(q, c_st) + inv * jnp.dot(a_hi, v, preferred_element_type=F32)
    b_last = b_col[0:1, :] if rev else b_col[l - 1:l, :]
    log_k = b_last - b_col + i_col
    m_new = jnp.maximum(b_last + m_st, jnp.max(log_k, axis=0, keepdims=True))
    kw = jnp.exp(log_k - m_new)
    decay = jnp.exp(b_last + m_st - m_new)
    kwk = kw * k.astype(F32)
    c_new = decay * c_st + _bdot_tn(kwk, v)
    n_new = decay * n_st + jnp.sum(kwk, axis=0, keepdims=True)
    return h, c_new, n_new, m_new


def _mlstm_scan_kernel(*refs, n_chunks, zero_init):
    if zero_init:
        q_ref, k_ref, v_ref, gc_ref, gr_ref, hs_ref, c_ref, n_ref, m_ref, hb_scr = refs
        c_ref[...] = jnp.zeros_like(c_ref)
        n_ref[...] = jnp.zeros_like(n_ref)
        m_ref[...] = jnp.zeros_like(m_ref)
    else:
        (q_ref, k_ref, v_ref, gc_ref, gr_ref, c0_ref, n0_ref, m0_ref,
         hs_ref, c_ref, n_ref, m_ref, hb_scr) = refs
        c_ref[...] = c0_ref[...]
        n_ref[...] = n0_ref[...]
        m_ref[...] = m0_ref[...]
    l = MLSTM_CHUNK
    hh = MLSTM_HEADS
    dk = q_ref.shape[1] // hh
    dv = v_ref.shape[1] // hh

    tt = lax.broadcasted_iota(jnp.int32, (l, l), 0)
    ss = lax.broadcasted_iota(jnp.int32, (l, l), 1)
    masks = (ss <= tt, ss >= tt)
    masks_bf = tuple(jnp.where(m, 1.0, 0.0).astype(BF16) for m in masks)

    def body(c, carry):
        cr = n_chunks - 1 - c
        rows = (pl.ds(pl.multiple_of(c * l, l), l), pl.ds(pl.multiple_of(cr * l, l), l))
        grs = (gr_ref[c], gr_ref[cr])
        outs, states = ([], []), []
        for d in range(2):
            q, k, v, gc, gr = q_ref[rows[d], :], k_ref[rows[d], :], v_ref[rows[d], :], gc_ref[rows[d], :], grs[d]
            bc, br = _gate_cumsums(gc, gr, masks_bf[d], masks_bf[1 - d])
            for h in range(hh):
                ci, cf = 2 * d * hh + h, (2 * d + 1) * hh + h
                o, *st = _mlstm_chunk(q[:, h * dk:(h + 1) * dk], k[:, h * dk:(h + 1) * dk],
                                      v[:, h * dv:(h + 1) * dv], gc[:, ci:ci + 1], bc[:, cf:cf + 1],
                                      gr[ci:ci + 1, :], br[cf:cf + 1, :],
                                      c_ref[d, h], n_ref[d, h], m_ref[d, h], masks[d], d == 1)
                outs[d].append(o)
                states.append((d, h, st))
        hs_ref[rows[0], :] = jnp.concatenate(outs[0], axis=1)
        hb_scr[rows[1], :] = jnp.concatenate(outs[1], axis=1)
        for d, h, (c_new, n_new, m_new) in states:
            c_ref[d, h], n_ref[d, h], m_ref[d, h] = c_new, n_new, m_new
        return carry

    lax.fori_loop(0, n_chunks, body, 0)
    for h in range(hh):
        cols = pl.ds(h * dv, dv)
        hs = hs_ref[:, cols] + hb_scr[:, cols]
        hs_ref[:, cols] = hs * lax.rsqrt(jnp.mean(hs * hs, axis=-1, keepdims=True) + NORM_EPS)


def mlstm_scan(qkv, gcol, grow, init, row_off, n_seq, seq_len):
    hh = MLSTM_HEADS
    hv = qkv.shape[1] // 2
    dv = hv // hh
    dk = dv // 2
    l = MLSTM_CHUNK
    nc = seq_len // l
    ob = row_off // seq_len
    kern = functools.partial(_mlstm_scan_kernel, n_chunks=nc, zero_init=init is None)
    st = lambda *tail: pl.BlockSpec((None, 2, hh) + tail, lambda s: (s, 0, 0) + (0,) * len(tail))
    states = [st(dk, dv), st(1, dk), st(1, 1)]
    return pl.pallas_call(
        kern,
        grid=(n_seq,),
        in_specs=[pl.BlockSpec((seq_len, hh * dk), lambda s: (ob + s, 0)),
                  pl.BlockSpec((seq_len, hh * dk), lambda s: (ob + s, 1)),
                  pl.BlockSpec((seq_len, hv), lambda s: (ob + s, 1)),
                  pl.BlockSpec((seq_len, LANES), lambda s: (ob + s, 0)),
                  pl.BlockSpec((nc, 4 * hh, l), lambda s: (ob + s, 0, 0))]
                 + ([] if init is None else states),
        out_specs=[pl.BlockSpec((seq_len, hv), lambda s: (s, 0))] + states,
        out_shape=[jax.ShapeDtypeStruct((n_seq * seq_len, hv), F32),
                   jax.ShapeDtypeStruct((n_seq, 2, hh, dk, dv), F32),
                   jax.ShapeDtypeStruct((n_seq, 2, hh, 1, dk), F32),
                   jax.ShapeDtypeStruct((n_seq, 2, hh, 1, 1), F32)],
        scratch_shapes=[pltpu.VMEM((seq_len, hv), F32)],
        compiler_params=_params("arbitrary"),
        name="mlstm_scan",
    )(qkv, qkv, qkv, gcol, grow, *(() if init is None else init))


def _mla_proj_kernel(x_ref, g_ref, sh_ref, sc_ref, win_ref, qg_ref, kvg_ref, wqb_ref, cos_ref,
                     sin_ref, q_ref, ckv_ref, kpe_ref, *, q_lora, kv_lora, rope, n_heads):
    h = _modulated(x_ref[...], g_ref[...], sh_ref[...], sc_ref[...])
    proj = _bdot(h, win_ref[...])
    q_lat = proj[:, :q_lora]
    ckv_ref[...] = _rms(proj[:, q_lora:q_lora + kv_lora], kvg_ref[...])
    cos, sin = cos_ref[...], sin_ref[...]
    kpe = proj[:, q_lora + kv_lora:q_lora + kv_lora + rope]
    kpe_rot = proj[:, q_lora + kv_lora + rope:q_lora + kv_lora + 2 * rope]
    kpe_ref[...] = kpe * cos[:, :rope] + kpe_rot * sin[:, :rope]
    q = _bdot(_rms(q_lat, qg_ref[...]), wqb_ref[...])
    n = n_heads * LANES
    pieces = []
    for hd in range(n_heads):
        c = slice(hd * LANES, (hd + 1) * LANES)
        pieces.append(q[:, c])
        pieces.append(q[:, n:2 * n][:, c] * cos + q[:, 2 * n:][:, c] * sin)
    q_ref[...] = jnp.concatenate(pieces, axis=1).astype(BF16)


def _rot_cols(w):
    n = w.shape[1]
    j = np.arange(n)
    quarter = MLA_ROPE // 4
    first = (j % (2 * quarter)) < quarter
    src = np.where(first, j + quarter, j - quarter)
    sign = np.where(first, -1.0, 1.0).astype(np.float32)
    return w[:, src] * sign


def _rope_tables(n_ctx, dec_batch, dec_seq):
    quarter = MLA_ROPE // 4
    freq = np.power(np.float32(ROPE_BASE), -np.arange(quarter, dtype=np.float32) / np.float32(quarter))
    pos = np.arange(dec_seq)
    ang_r = (pos // GRID_W).astype(np.float32)[:, None] * freq[None, :]
    ang_c = (pos % GRID_W).astype(np.float32)[:, None] * freq[None, :]
    ang = np.concatenate([ang_r, ang_r, ang_c, ang_c], axis=1).astype(np.float32)
    cos = np.concatenate([np.ones((n_ctx, MLA_ROPE), np.float32)] + [np.cos(ang)] * dec_batch, axis=0)
    sin = np.concatenate([np.zeros((n_ctx, MLA_ROPE), np.float32)] + [np.sin(ang)] * dec_batch, axis=0)
    reps = LANES // MLA_ROPE
    return (jnp.asarray(np.tile(cos, (1, reps)), F32), jnp.asarray(np.tile(sin, (1, reps)), F32))


def mla_proj(x, g, shift, scale, w_in, q_g, kv_g, w_qb, cos, sin, n_ctx, dec_seq, tm=512):
    t, d = x.shape
    hh, nope, rope = MLA_HEADS, MLA_NOPE, MLA_ROPE
    q_lora = q_g.shape[1]
    kv_lora = kv_g.shape[1]
    w_in_ext = jnp.concatenate([w_in, _rot_cols(w_in[:, q_lora + kv_lora:])], axis=1)
    w3 = w_qb.reshape(q_lora, hh, nope + rope)
    w_qn = w3[:, :, :nope].reshape(q_lora, hh * nope)
    w_qp = w3[:, :, nope:].reshape(q_lora, hh * rope)
    assert nope == LANES and rope <= LANES
    lane_pad = lambda w: jnp.pad(w.reshape(q_lora, hh, rope),
                                 ((0, 0), (0, 0), (0, LANES - rope))).reshape(q_lora, hh * LANES)
    w_qb_ext = jnp.concatenate([w_qn, lane_pad(w_qp), lane_pad(_rot_cols(w_qp))], axis=1)
    cidx = _cvec_index(n_ctx, dec_seq, tm)
    mod_spec = pl.BlockSpec((None, 1, d), lambda i: (cidx(i), 0, 0))
    full = lambda a: pl.BlockSpec(a.shape, lambda i: (0,) * a.ndim)
    kern = functools.partial(_mla_proj_kernel, q_lora=q_lora, kv_lora=kv_lora, rope=rope, n_heads=hh)
    return pl.pallas_call(
        kern,
        grid=(t // tm,),
        in_specs=[pl.BlockSpec((tm, d), lambda i: (i, 0)), full(g), mod_spec, mod_spec,
                  full(w_in_ext), full(q_g), full(kv_g), full(w_qb_ext),
                  pl.BlockSpec((tm, LANES), lambda i: (i, 0)),
                  pl.BlockSpec((tm, LANES), lambda i: (i, 0))],
        out_specs=[pl.BlockSpec((tm, 2 * hh * LANES), lambda i: (i, 0)),
                   pl.BlockSpec((tm, kv_lora), lambda i: (i, 0)),
                   pl.BlockSpec((tm, rope), lambda i: (i, 0))],
        out_shape=[jax.ShapeDtypeStruct((t, 2 * hh * LANES), BF16),
                   jax.ShapeDtypeStruct((t, kv_lora), F32),
                   jax.ShapeDtypeStruct((t, rope), F32)],
        compiler_params=_params("arbitrary"),
        name="mla_proj",
    )(x, g, shift, scale, w_in_ext, q_g, kv_g, w_qb_ext, cos, sin)


def _mla_kv_kernel(ckv_ref, kp_ref, w_ref, k_ref, v_ref):
    kv = _bdot(ckv_ref[...], w_ref[...])
    n = v_ref.shape[1]
    kp = kp_ref[...]
    pieces = []
    for hd in range(n // LANES):
        pieces += [kv[:, hd * LANES:(hd + 1) * LANES].astype(BF16), kp]
    k_ref[...] = jnp.concatenate(pieces, axis=1)
    v_ref[...] = kv[:, n:].astype(BF16)


def mla_kv(ckv_all, kp_pad, w_kvb, tm=512):
    r, kv_lora = ckv_all.shape
    hh, nope, vd = MLA_HEADS, MLA_NOPE, MLA_V
    w3 = w_kvb.reshape(kv_lora, hh, nope + vd)
    w_perm = jnp.concatenate([w3[:, :, :nope].reshape(kv_lora, hh * nope),
                              w3[:, :, nope:].reshape(kv_lora, hh * vd)], axis=1)
    return pl.pallas_call(
        _mla_kv_kernel,
        grid=(r // tm,),
        in_specs=[pl.BlockSpec((tm, kv_lora), lambda i: (i, 0)),
                  pl.BlockSpec((tm, LANES), lambda i: (i, 0)),
                  pl.BlockSpec(w_perm.shape, lambda i: (0, 0))],
        out_specs=[pl.BlockSpec((tm, 2 * hh * LANES), lambda i: (i, 0)),
                   pl.BlockSpec((tm, hh * vd), lambda i: (i, 0))],
        out_shape=[jax.ShapeDtypeStruct((r, 2 * hh * LANES), BF16),
                   jax.ShapeDtypeStruct((r, hh * vd), BF16)],
        compiler_params=_params("arbitrary"),
        name="mla_kv",
    )(ckv_all, kp_pad, w_perm)


def _attn_kernel(q_ref, k_ref, v_ref, o_ref, *, scale):
    c = scale * np.log2(np.e)
    for h in range(MLA_HEADS):
        hk = slice(2 * h * LANES, 2 * (h + 1) * LANES)
        s = _bdot_nt(q_ref[:, hk], k_ref[:, hk])
        e = jnp.exp2((s - jnp.max(s, axis=-1, keepdims=True)) * c)
        o = _bdot(e, v_ref[:, h * MLA_V:(h + 1) * MLA_V]) / jnp.sum(e, axis=-1, keepdims=True)
        o_ref[:, h * MLA_V:(h + 1) * MLA_V] = o.astype(BF16)


def mla_attention(q, k, v, q_row_off, k_row_off, n_seq, q_len, k_len):
    tq = ATTN_Q_BLOCK
    qb = q_len // tq
    q0 = q_row_off // tq
    k0 = k_row_off // k_len
    dq, dv = q.shape[1], v.shape[1]
    kern = functools.partial(_attn_kernel, scale=(MLA_NOPE + MLA_ROPE) ** -0.5)
    return pl.pallas_call(
        kern,
        grid=(n_seq, qb),
        in_specs=[pl.BlockSpec((tq, dq), lambda s, j: (q0 + s * qb + j, 0)),
                  pl.BlockSpec((k_len, dq), lambda s, j: (k0 + s, 0)),
                  pl.BlockSpec((k_len, dv), lambda s, j: (k0 + s, 0))],
        out_specs=pl.BlockSpec((tq, dv), lambda s, j: (s * qb + j, 0)),
        out_shape=jax.ShapeDtypeStruct((n_seq * q_len, dv), BF16),
        compiler_params=_params("arbitrary", "arbitrary"),
        name="mla_attention",
    )(q, k, v)


def kernel(x_prompt, x_sample, state_mlstm_C, state_mlstm_n, state_mlstm_m, cache_mla_ckv,
           cache_mla_kpe, c, c_ctx, w_ada, b_ada, norm_mix, norm_ffn, norm_final, w_pool,
           pool_scale, w_mlstm_in, b_mlstm_gate, mlstm_head_g, w_mlstm_out, w_mla_in, mla_q_g,
           mla_kv_g, w_mla_qb, w_mla_kvb, w_mla_out, w_router, b_router, w_exp_gate, w_exp_up,
           w_exp_down):
    batch, seq, d = x_prompt.shape
    dec_batch, dec_seq, _ = x_sample.shape
    depth = w_ada.shape[0]
    n_ctx = batch * seq
    n_lat = dec_batch * dec_seq
    hh = MLSTM_HEADS
    past = cache_mla_ckv.shape[2]

    x = (x_prompt.reshape(n_ctx, d), x_sample.reshape(n_lat, d))

    n_cv = 1 + dec_batch
    cvecs = jnp.concatenate([c_ctx[None, :], c, jnp.zeros((SUBLANES - n_cv % SUBLANES, d), F32)], axis=0)
    mod = ada_mod_all(cvecs, w_ada, b_ada).reshape(depth, cvecs.shape[0], 6, 1, d)

    w_router_pad = jnp.pad(w_router, ((0, 0), (0, LANES - N_EXPERTS)))
    w_router_hi = w_router_pad.astype(BF16)
    w_router_lo = (w_router_pad - w_router_hi.astype(F32)).astype(BF16)
    w_router_split = jnp.concatenate([w_router_hi, w_router_lo], axis=1)
    b_router_col = b_router.reshape(N_EXPERTS, 1)
    g_final = norm_final.reshape(1, d)
    row = lambda a: a.reshape(1, -1)

    outs = {}
    for i in range(depth):
        kind, j = i % 3, i // 3
        m = [mod[i, :n_cv, k] for k in range(6)]
        g_mix = row(norm_mix[i])
        if kind == 0:
            x = pool_layer(x, g_mix, m[0], m[1], m[2], w_pool[j], row(pool_scale[j]),
                           n_ctx, seq, dec_seq)
        elif kind == 1:
            qkv, o_gate, gcol, grow = mlstm_proj(x, g_mix, m[0], m[1], w_mlstm_in, j,
                                                 b_mlstm_gate[j], n_ctx, dec_seq)
            t = n_ctx + n_lat
            l = MLSTM_CHUNK
            grow_c = grow.reshape(4 * hh, t // l, l).transpose(1, 0, 2)
            hs_c, c_new, n_new, m_new = mlstm_scan(qkv, gcol, grow_c, None, 0, batch, seq)
            init = (state_mlstm_C[:, j], state_mlstm_n[:, j][:, :, :, None, :],
                    state_mlstm_m[:, j][:, :, :, None, None])
            hs_l, _, _, _ = mlstm_scan(qkv, gcol, grow_c, init, n_ctx, dec_batch, dec_seq)
            outs["C"] = c_new[:, None]
            outs["n"] = n_new[:, None, :, :, 0, :]
            outs["m"] = m_new[:, None, :, :, 0, 0]
            x = mlstm_out(x, hs_c, hs_l, o_gate, row(mlstm_head_g[j]), w_mlstm_out[j], m[2],
                          n_ctx, dec_seq)
        else:
            cos, sin = _rope_tables(n_ctx, dec_batch, dec_seq)
            q_cat, ckv, kpe = mla_proj(x, g_mix, m[0], m[1], w_mla_in[j], row(mla_q_g[j]),
                                       row(mla_kv_g[j]), w_mla_qb[j], cos, sin, n_ctx, dec_seq)
            lat_parts_c, lat_parts_p = [], []
            for b in range(dec_batch):
                lo = n_ctx + b * dec_seq
                lat_parts_c += [cache_mla_ckv[b, j], ckv[lo:lo + dec_seq]]
                lat_parts_p += [cache_mla_kpe[b, j], kpe[lo:lo + dec_seq]]
            ckv_all = jnp.concatenate(lat_parts_c + [ckv[:n_ctx]], axis=0)
            kp_all = jnp.concatenate(lat_parts_p + [kpe[:n_ctx]], axis=0).astype(BF16)
            kp_pad = jnp.pad(kp_all, ((0, 0), (0, LANES - kp_all.shape[1])))
            k_cat, v = mla_kv(ckv_all, kp_pad, w_mla_kvb[j])
            k_lat = past + dec_seq
            o_c = mla_attention(q_cat, k_cat, v, 0, dec_batch * k_lat, batch, seq, seq)
            o_l = mla_attention(q_cat, k_cat, v, n_ctx, 0, dec_batch, dec_seq, k_lat)
            outs["ckv"] = ckv[:n_ctx].reshape(batch, 1, seq, -1)
            outs["kpe"] = kpe[:n_ctx].reshape(batch, 1, seq, -1)
            x = resid_proj(x, o_c, o_l, w_mla_out[j], m[2], n_ctx, dec_seq)
        x = moe_layer(x, row(norm_ffn[i]), m[3], m[4], m[5], w_router_split, b_router_col,
                      w_exp_gate, w_exp_up, w_exp_down, i, g_final, n_ctx, dec_seq,
                      final_norm=(i == depth - 1), split_output=(i == depth - 1))

    y_prompt = x[0].reshape(batch, seq, d)
    y_sample = x[1].reshape(dec_batch, dec_seq, d)
    return (y_prompt, y_sample, outs["C"], outs["n"], outs["m"], outs["ckv"], outs["kpe"])
```

```python
import functools

import numpy as np
import jax
import jax.numpy as jnp
from jax import lax
from jax.experimental import pallas as pl
from jax.experimental.pallas import tpu as pltpu

F32 = jnp.float32
BF16 = jnp.bfloat16

NORM_EPS = 1e-6
GRID_W = 64
POOL_WINDOWS = (2, 4, 8, 16)
MLSTM_HEADS = 4
MLSTM_CHUNK = 256
MLA_HEADS = 8
MLA_NOPE = 128
MLA_ROPE = 64
MLA_V = 128
ROPE_BASE = 10000.0
N_EXPERTS = 16
N_EXPERT_GROUPS = 4
EXPERTS_PER_GROUP = N_EXPERTS // N_EXPERT_GROUPS

LANES = 128
SUBLANES = 8
VMEM_LIMIT = 56 * 1024 * 1024
POOL_TILE = 256
POOL_HALO = 8
ATTN_Q_BLOCK = 256
MOE_TILE = 512
MOE_SEG_ALIGN = 16
MOE_ROW_BLOCK = 160
MOE_WIDE_BLOCK = 224


def _params(*sem):
    return pltpu.CompilerParams(dimension_semantics=sem, vmem_limit_bytes=VMEM_LIMIT)


def _rms(x, g):
    return x * lax.rsqrt(jnp.mean(x * x, axis=-1, keepdims=True) + NORM_EPS) * g


def _modulated(x, g, shift, scale):
    return _rms(x, g) * (1.0 + scale) + shift


def _silu(x):
    return x * jax.nn.sigmoid(x)


def _bdot(a, b):
    return jnp.dot(a.astype(BF16), b.astype(BF16), preferred_element_type=F32)


def _bdot_nt(a, b):
    return lax.dot_general(a.astype(BF16), b.astype(BF16), (((1,), (1,)), ((), ())),
                           preferred_element_type=F32)


def _bdot_tn(a, b):
    return lax.dot_general(a.astype(BF16), b.astype(BF16), (((0,), (0,)), ((), ())),
                           preferred_element_type=F32)


def _cvec_index(n_ctx, dec_seq, tm):
    def idx(i):
        r = i * tm
        return jnp.where(r < n_ctx, 0, (r - n_ctx) // dec_seq + 1)
    return idx


def _ada_kernel(c_ref, w_ref, b_ref, o_ref):
    o_ref[...] = _bdot(_silu(c_ref[...]), w_ref[...]) + b_ref[...]


def ada_mod_all(cvecs, w_ada, b_ada, tn=1536):
    depth, d, n6 = w_ada.shape
    rows = cvecs.shape[0]
    return pl.pallas_call(
        _ada_kernel,
        grid=(depth, n6 // tn),
        in_specs=[pl.BlockSpec((rows, d), lambda l, n: (0, 0)),
                  pl.BlockSpec((None, d, tn), lambda l, n: (l, 0, n)),
                  pl.BlockSpec((None, 1, tn), lambda l, n: (l, 0, n))],
        out_specs=pl.BlockSpec((None, rows, tn), lambda l, n: (l, 0, n)),
        out_shape=jax.ShapeDtypeStruct((depth, rows, n6), F32),
        compiler_params=_params("arbitrary", "arbitrary"),
        name="ada_mod",
    )(cvecs, w_ada, b_ada.reshape(depth, 1, n6))


def _pool_kernel(*refs, n_ctx_tiles, ctx_seq_tiles, lat_seq_tiles, split_input):
    n_x = 6 if split_input else 3
    x_refs = refs[:n_x]
    g_ref, sh_ref, sc_ref, gt_ref, wp_ref, ps_ref, o_ref, buf_ref = refs[n_x:]
    i = pl.program_id(0)
    is_ctx = i < n_ctx_tiles
    j = jnp.where(is_ctx, i % ctx_seq_tiles, (i - n_ctx_tiles) % lat_seq_tiles)
    nt = jnp.where(is_ctx, ctx_seq_tiles, lat_seq_tiles)
    g, sh, sc = g_ref[...], sh_ref[...], sc_ref[...]
    tp, hl = POOL_TILE, POOL_HALO
    gw = o_ref.shape[1] // len(POOL_WINDOWS)

    def fill(xc_ref, xp_ref, xn_ref):
        buf_ref[pl.ds(0, hl), :] = jnp.where(j == 0, 0.0, _modulated(xp_ref[...], g, sh, sc))
        buf_ref[pl.ds(hl, tp), :] = _modulated(xc_ref[...], g, sh, sc)
        buf_ref[pl.ds(hl + tp, hl), :] = jnp.where(j == nt - 1, 0.0,
                                                    _modulated(xn_ref[...], g, sh, sc))
        o_ref[...] = xc_ref[...]

    if split_input:
        pl.when(is_ctx)(lambda: fill(*x_refs[:3]))
        pl.when(jnp.logical_not(is_ctx))(lambda: fill(*x_refs[3:]))
    else:
        fill(*x_refs)

    pos = j * tp + lax.broadcasted_iota(jnp.int32, (tp, 1), 0)
    seq_len = nt * tp
    for gi, w in enumerate(POOL_WINDOWS):
        cols = pl.ds(gi * gw, gw)
        acc = buf_ref[pl.ds(hl - w // 2, tp), cols]
        for d in range(-w // 2 + 1, w // 2):
            acc = acc + buf_ref[pl.ds(hl + d, tp), cols]
        cnt = jnp.minimum(pos + w // 2, seq_len) - jnp.maximum(pos - w // 2, 0)
        pooled = acc / cnt.astype(F32) - buf_ref[pl.ds(hl, tp), cols]
        y = _bdot(pooled, wp_ref[gi]) * ps_ref[:, cols]
        o_ref[:, cols] = o_ref[:, cols] + gt_ref[:, cols] * y


def _halo_specs(tp, hl, d, tile_off, n_rows):
    hb = tp // hl
    last_tile, last_hblk = n_rows // tp - 1, n_rows // hl - 1
    tile = lambda i: jnp.clip(i - tile_off, 0, last_tile)
    return [pl.BlockSpec((tp, d), lambda i: (tile(i), 0)),
            pl.BlockSpec((hl, d), lambda i: (jnp.clip(tile(i) * hb - 1, 0, last_hblk), 0)),
            pl.BlockSpec((hl, d), lambda i: (jnp.clip((tile(i) + 1) * hb, 0, last_hblk), 0))]


def pool_layer(xs, g, shift, scale, gate, w_pool, pool_scale, n_ctx, seq, dec_seq):
    split = isinstance(xs, tuple)
    tp, hl = POOL_TILE, POOL_HALO
    if split:
        d = xs[0].shape[1]
        t = xs[0].shape[0] + xs[1].shape[0]
        x_specs = (_halo_specs(tp, hl, d, 0, xs[0].shape[0])
                   + _halo_specs(tp, hl, d, n_ctx // tp, xs[1].shape[0]))
        x_args = (xs[0],) * 3 + (xs[1],) * 3
    else:
        t, d = xs.shape
        x_specs = _halo_specs(tp, hl, d, 0, t)
        x_args = (xs,) * 3
    cidx = _cvec_index(n_ctx, dec_seq, tp)
    mod_spec = pl.BlockSpec((None, 1, d), lambda i: (cidx(i), 0, 0))
    row_spec = pl.BlockSpec((1, d), lambda i: (0, 0))
    kern = functools.partial(_pool_kernel, n_ctx_tiles=n_ctx // tp, ctx_seq_tiles=seq // tp,
                             lat_seq_tiles=dec_seq // tp, split_input=split)
    return pl.pallas_call(
        kern,
        grid=(t // tp,),
        in_specs=x_specs + [row_spec, mod_spec, mod_spec, mod_spec,
                            pl.BlockSpec(w_pool.shape, lambda i: (0, 0, 0)), row_spec],
        out_specs=pl.BlockSpec((tp, d), lambda i: (i, 0)),
        out_shape=jax.ShapeDtypeStruct((t, d), F32),
        scratch_shapes=[pltpu.VMEM((tp + 2 * hl, d), F32)],
        compiler_params=_params("arbitrary"),
        name="pool_mixer",
    )(*x_args, g, shift, scale, gate, w_pool, pool_scale)


def _route(sel, scores):
    e, tm = sel.shape
    row = lax.broadcasted_iota(jnp.int32, (e, tm), 0)
    best = jnp.zeros((1, tm), jnp.int32)
    best_sc = None
    for gidx in range(N_EXPERT_GROUPS):
        r = [sel[gidx * EXPERTS_PER_GROUP + k:gidx * EXPERTS_PER_GROUP + k + 1, :]
             for k in range(EXPERTS_PER_GROUP)]
        top2 = None
        for a in range(EXPERTS_PER_GROUP):
            for b in range(a + 1, EXPERTS_PER_GROUP):
                s = r[a] + r[b]
                top2 = s if top2 is None else jnp.maximum(top2, s)
        if best_sc is None:
            best_sc = top2
        else:
            better = top2 > best_sc
            best = jnp.where(better, gidx, best)
            best_sc = jnp.where(better, top2, best_sc)
    neg = -jnp.inf
    masked = jnp.where(row // EXPERTS_PER_GROUP == best, sel, neg)
    m1 = jnp.max(masked, axis=0, keepdims=True)
    i1 = jnp.min(jnp.where(masked == m1, row, e), axis=0, keepdims=True)
    masked2 = jnp.where(row == i1, neg, masked)
    m2 = jnp.max(masked2, axis=0, keepdims=True)
    i2 = jnp.min(jnp.where(masked2 == m2, row, e), axis=0, keepdims=True)
    hot1 = row == i1
    hot2 = row == i2
    w1 = jnp.sum(jnp.where(hot1, scores, 0.0), axis=0, keepdims=True)
    w2 = jnp.sum(jnp.where(hot2, scores, 0.0), axis=0, keepdims=True)
    tot = w1 + w2
    return best, jnp.where(hot1, w1 / tot, 0.0) + jnp.where(hot2, w2 / tot, 0.0)


def _split_bf16(a, parts):
    out = []
    for _ in range(parts):
        p = a.astype(BF16)
        out.append(p)
        a = a - p.astype(F32)
    return out


def _pad_rows(a, rows):
    return jnp.concatenate([a, jnp.zeros((rows - a.shape[0], a.shape[1]), a.dtype)], axis=0)


def _moe_kernel(x_ref, g_ref, sh_ref, sc_ref, gt_ref, wr_ref, br_ref, wgf_ref, wuf_ref, wdf_ref,
                gf_ref, *rest, final_norm, n_ctx_tiles, n_experts):
    out_refs, (wgu_ref, wd_ref), scratch = rest[:-5], rest[-5:-3], rest[-3:]
    i = pl.program_id(0)
    f = wgf_ref.shape[1]
    for e in range(n_experts):
        gi, k = divmod(e, EXPERTS_PER_GROUP)

        @pl.when(i == e)
        def _(gi=gi, k=k):
            wgu_ref[gi, :, pl.ds(2 * k * f, f)] = wgf_ref[...].astype(BF16)
            wgu_ref[gi, :, pl.ds((2 * k + 1) * f, f)] = wuf_ref[...].astype(BF16)
            wd_ref[gi, pl.ds(k * f, f), :] = wdf_ref[...].astype(BF16)

    @pl.when(i >= n_experts)
    def _():
        _moe_tile(x_ref, g_ref, sh_ref, sc_ref, gt_ref, wr_ref, br_ref, wgu_ref, wd_ref,
                  gf_ref, out_refs, scratch, i - n_experts, final_norm, n_ctx_tiles)


def _moe_tile(x_ref, g_ref, sh_ref, sc_ref, gt_ref, wr_ref, br_ref, wgu_ref, wd_ref,
              gf_ref, out_refs, scratch, tile, final_norm, n_ctx_tiles):
    hp_scr, cw_scr, yp_scr = scratch
    tr, d = x_ref.shape
    trp = hp_scr.shape[0]
    ng, eg = N_EXPERT_GROUPS, EXPERTS_PER_GROUP
    x = x_ref[...]
    h = _modulated(x, g_ref[...], sh_ref[...], sc_ref[...])
    hb = h.astype(BF16)

    h_lo = (h - hb.astype(F32)).astype(BF16)
    wr = wr_ref[...]
    lg = jnp.dot(hb, wr, preferred_element_type=F32)
    logits = lg[:, :LANES] + lg[:, LANES:] + jnp.dot(h_lo, wr[:, :LANES], preferred_element_type=F32)
    scores = jax.nn.sigmoid(logits.T[:N_EXPERTS, :])
    best, comb_t = _route(scores + br_ref[...], scores)

    grp = lax.broadcasted_iota(jnp.int32, (SUBLANES, tr), 0)
    hot_t = (grp == best).astype(F32)
    cw_t = hot_t[0:1, :] * comb_t[0:eg, :]
    for gi in range(1, ng):
        cw_t = cw_t + hot_t[gi:gi + 1, :] * comb_t[gi * eg:(gi + 1) * eg, :]
    cw_c = _pad_rows(cw_t, LANES).T

    ia = lax.broadcasted_iota(jnp.int32, (tr, tr), 0)
    ib = lax.broadcasted_iota(jnp.int32, (tr, tr), 1)
    before = jnp.where(ia < ib, 1.0, 0.0).astype(BF16)
    rank_t = jnp.dot(hot_t.astype(BF16), before, preferred_element_type=F32)

    starts, counts = [], []
    off = jnp.int32(0)
    for gi in range(ng):
        n = jnp.sum(hot_t[gi:gi + 1, :]).astype(jnp.int32)
        n = ((n + MOE_SEG_ALIGN - 1) // MOE_SEG_ALIGN) * MOE_SEG_ALIGN
        starts.append(off)
        counts.append(n)
        off = off + n

    pos_t = hot_t[0:1, :] * (rank_t[0:1, :] + starts[0].astype(F32))
    for gi in range(1, ng):
        pos_t = pos_t + hot_t[gi:gi + 1, :] * (rank_t[gi:gi + 1, :] + starts[gi].astype(F32))
    pos_c = _pad_rows(pos_t, LANES).T[:, 0:1]
    used = tr + ng * MOE_SEG_ALIGN
    perm = jnp.where(lax.broadcasted_iota(jnp.int32, (used, tr), 0) == pos_t.astype(jnp.int32),
                     1.0, 0.0).astype(BF16)
    perm_t = jnp.where(lax.broadcasted_iota(jnp.int32, (tr, used), 1) == pos_c.astype(jnp.int32),
                       1.0, 0.0).astype(BF16)

    hp_scr[pl.ds(0, used), :] = jnp.dot(perm, hb, preferred_element_type=F32).astype(BF16)
    hp_scr[pl.ds(used, trp - used), :] = jnp.zeros((trp - used, d), BF16)
    cw_pair = jnp.dot(perm, jnp.concatenate(_split_bf16(cw_c, 2), axis=1), preferred_element_type=F32)
    cw_scr[pl.ds(0, used), :] = cw_pair[:, :LANES] + cw_pair[:, LANES:]
    cw_scr[pl.ds(used, trp - used), :] = jnp.zeros((trp - used, LANES), F32)
    yp_scr[...] = jnp.zeros_like(yp_scr)

    f = wd_ref.shape[1] // eg

    def expert_block(gi, row0, n_rows):
        rows = pl.ds(pl.multiple_of(row0, MOE_SEG_ALIGN), n_rows)
        cwb = cw_scr[rows, :]
        gu = jnp.dot(hp_scr[rows, :], wgu_ref[gi], preferred_element_type=F32)
        hid = [(_silu(gu[:, 2 * k * f:(2 * k + 1) * f]) * gu[:, (2 * k + 1) * f:(2 * k + 2) * f]
                * cwb[:, k:k + 1]).astype(BF16) for k in range(eg)]
        yp_scr[rows, :] = jnp.dot(jnp.concatenate(hid, axis=1), wd_ref[gi],
                                  preferred_element_type=F32)

    sb, wide = MOE_ROW_BLOCK, MOE_WIDE_BLOCK
    for gi in range(ng):
        use_wide = jnp.logical_and(counts[gi] > sb, counts[gi] <= wide)
        pl.when(use_wide)(functools.partial(expert_block, gi, starts[gi], wide))
        n_blocks = jnp.where(use_wide, 0, (counts[gi] + sb - 1) // sb)

        def block(b, carry, gi=gi):
            expert_block(gi, starts[gi] + b * sb, sb)
            return carry
        lax.fori_loop(0, n_blocks, block, 0)

    y_hi, y_lo = _split_bf16(yp_scr[pl.ds(0, used), :], 2)
    moe = (jnp.dot(perm_t, y_hi, preferred_element_type=F32)
           + jnp.dot(perm_t, y_lo, preferred_element_type=F32))
    out = x + gt_ref[...] * moe
    if final_norm:
        out = _rms(out, gf_ref[...])
    if n_ctx_tiles is None:
        out_refs[0][...] = out
    else:
        is_ctx = tile < n_ctx_tiles

        @pl.when(is_ctx)
        def _():
            out_refs[0][...] = out

        @pl.when(jnp.logical_not(is_ctx))
        def _():
            out_refs[1][...] = out


def moe_layer(x, g, shift, scale, gate, w_router_split, b_router_col, wg, wu, wd, layer, g_final,
              n_ctx, dec_seq, final_norm, split_output=False, tm=MOE_TILE):
    t, d = x.shape
    _, n_e, _, f = wg.shape
    trp = tm + N_EXPERT_GROUPS * MOE_SEG_ALIGN + max(MOE_ROW_BLOCK, MOE_WIDE_BLOCK)
    tile = lambda i: jnp.maximum(i - n_e, 0)
    cidx = _cvec_index(n_ctx, dec_seq, tm)
    mod_spec = pl.BlockSpec((None, 1, d), lambda i: (cidx(tile(i)), 0, 0))
    row_spec = pl.BlockSpec((1, d), lambda i: (0, 0))
    expert = lambda i: jnp.minimum(i, n_e - 1)
    nct = n_ctx // tm
    if split_output:
        out_specs = [pl.BlockSpec((tm, d), lambda i: (jnp.minimum(tile(i), nct - 1), 0)),
                     pl.BlockSpec((tm, d), lambda i: (jnp.maximum(tile(i) - nct, 0), 0))]
        out_shape = [jax.ShapeDtypeStruct((n_ctx, d), F32), jax.ShapeDtypeStruct((t - n_ctx, d), F32)]
    else:
        out_specs = pl.BlockSpec((tm, d), lambda i: (tile(i), 0))
        out_shape = jax.ShapeDtypeStruct((t, d), F32)
    return pl.pallas_call(
        functools.partial(_moe_kernel, final_norm=final_norm,
                          n_ctx_tiles=nct if split_output else None, n_experts=n_e),
        grid=(n_e + t // tm,),
        in_specs=[pl.BlockSpec((tm, d), lambda i: (tile(i), 0)),
                  row_spec, mod_spec, mod_spec, mod_spec,
                  pl.BlockSpec(w_router_split.shape, lambda i: (0, 0)),
                  pl.BlockSpec(b_router_col.shape, lambda i: (0, 0)),
                  pl.BlockSpec((None, None, d, f), lambda i: (layer, expert(i), 0, 0)),
                  pl.BlockSpec((None, None, d, f), lambda i: (layer, expert(i), 0, 0)),
                  pl.BlockSpec((None, None, f, d), lambda i: (layer, expert(i), 0, 0)),
                  row_spec],
        out_specs=out_specs,
        out_shape=out_shape,
        scratch_shapes=[pltpu.VMEM((N_EXPERT_GROUPS, d, 2 * EXPERTS_PER_GROUP * f), BF16),
                        pltpu.VMEM((N_EXPERT_GROUPS, EXPERTS_PER_GROUP * f, d), BF16),
                        pltpu.VMEM((trp, d), BF16), pltpu.VMEM((trp, LANES), F32),
                        pltpu.VMEM((trp, d), F32)],
        compiler_params=_params("arbitrary"),
        name="moe",
    )(x, g, shift, scale, gate, w_router_split, b_router_col, wg, wu, wd, g_final)


def _per_stream(n_ctx_tiles, ctx_ref, lat_ref, fn):
    i = pl.program_id(0)

    @pl.when(i < n_ctx_tiles)
    def _():
        fn(ctx_ref[...])

    @pl.when(i >= n_ctx_tiles)
    def _():
        fn(lat_ref[...])


def _stream_specs(tm, k, n_ctx_tiles):
    return [pl.BlockSpec((tm, k), lambda i: (jnp.minimum(i, n_ctx_tiles - 1), 0)),
            pl.BlockSpec((tm, k), lambda i: (jnp.maximum(i - n_ctx_tiles, 0), 0))]


def _resid_proj_kernel(x_ref, ac_ref, al_ref, w_ref, gt_ref, o_ref, *, n_ctx_tiles):
    def run(a):
        o_ref[...] = x_ref[...] + gt_ref[...] * _bdot(a, w_ref[...])
    _per_stream(n_ctx_tiles, ac_ref, al_ref, run)


def _mlstm_out_kernel(x_ref, hc_ref, hl_ref, og_ref, hg_ref, w_ref, gt_ref, o_ref, *, n_ctx_tiles):
    def run(hs):
        a = jax.nn.sigmoid(og_ref[...]) * (hs * hg_ref[...])
        o_ref[...] = x_ref[...] + gt_ref[...] * _bdot(a, w_ref[...])
    _per_stream(n_ctx_tiles, hc_ref, hl_ref, run)


def resid_proj(x, a_ctx, a_lat, w, gate, n_ctx, dec_seq, tm=512):
    t, d = x.shape
    k = a_ctx.shape[1]
    cidx = _cvec_index(n_ctx, dec_seq, tm)
    nct = n_ctx // tm
    return pl.pallas_call(
        functools.partial(_resid_proj_kernel, n_ctx_tiles=nct),
        grid=(t // tm,),
        in_specs=[pl.BlockSpec((tm, d), lambda i: (i, 0))] + _stream_specs(tm, k, nct) + [
            pl.BlockSpec((k, d), lambda i: (0, 0)),
            pl.BlockSpec((None, 1, d), lambda i: (cidx(i), 0, 0))],
        out_specs=pl.BlockSpec((tm, d), lambda i: (i, 0)),
        out_shape=jax.ShapeDtypeStruct((t, d), F32),
        compiler_params=_params("arbitrary"),
        name="resid_proj",
    )(x, a_ctx, a_lat, w, gate)


def mlstm_out(x, hs_ctx, hs_lat, o_gate, head_g, w, gate, n_ctx, dec_seq, tm=512):
    t, d = x.shape
    k = hs_ctx.shape[1]
    cidx = _cvec_index(n_ctx, dec_seq, tm)
    nct = n_ctx // tm
    return pl.pallas_call(
        functools.partial(_mlstm_out_kernel, n_ctx_tiles=nct),
        grid=(t // tm,),
        in_specs=[pl.BlockSpec((tm, d), lambda i: (i, 0))] + _stream_specs(tm, k, nct) + [
            pl.BlockSpec((tm, k), lambda i: (i, 0)),
            pl.BlockSpec((1, k), lambda i: (0, 0)),
            pl.BlockSpec((k, d), lambda i: (0, 0)),
            pl.BlockSpec((None, 1, d), lambda i: (cidx(i), 0, 0))],
        out_specs=pl.BlockSpec((tm, d), lambda i: (i, 0)),
        out_shape=jax.ShapeDtypeStruct((t, d), F32),
        compiler_params=_params("arbitrary"),
        name="mlstm_out",
    )(x, hs_ctx, hs_lat, o_gate, head_g, w, gate)


def _mlstm_proj_kernel(x_ref, g_ref, sh_ref, sc_ref, wqk_ref, wv_ref, wo_ref, wgt_ref, bg_ref,
                       wgtt_ref, bgt_ref, qs_ref, qkv_ref, o_ref, gc_ref, gr_ref):
    h = _modulated(x_ref[...], g_ref[...], sh_ref[...], sc_ref[...]).astype(BF16)
    nqk = wqk_ref.shape[1]
    qkv_ref[:, :nqk] = (jnp.dot(h, wqk_ref[...].astype(BF16), preferred_element_type=F32)
                        * qs_ref[...]).astype(BF16)
    qkv_ref[:, nqk:] = jnp.dot(h, wv_ref[...].astype(BF16),
                               preferred_element_type=F32).astype(BF16)
    o_ref[...] = jnp.dot(h, wo_ref[...].astype(BF16), preferred_element_type=F32)
    gc_ref[...] = jnp.dot(h, wgt_ref[...].astype(BF16), preferred_element_type=F32) + bg_ref[...]
    gr_ref[...] = _bdot_nt(wgtt_ref[...], h) + bgt_ref[...]


def mlstm_proj(x, g, shift, scale, w_in_all, layer, b_gate, n_ctx, dec_seq, tm=512):
    t, d = x.shape
    hh = MLSTM_HEADS
    hv = d
    hk = hv // 2
    ng = 4 * hh
    w_g = w_in_all[layer, :, 2 * hk + 2 * hv:]
    w_g_pad = jnp.pad(w_g, ((0, 0), (0, LANES - ng)))
    b_pad = jnp.pad(b_gate.reshape(1, ng), ((0, 0), (0, LANES - ng)))
    w_g_t = w_g.T
    b_t = b_gate.reshape(ng, 1)
    dk = hk // hh
    q_scale = jnp.concatenate([jnp.full((1, hk), dk ** -0.5, F32), jnp.ones((1, hk), F32)], axis=1)
    cidx = _cvec_index(n_ctx, dec_seq, tm)
    mod_spec = pl.BlockSpec((None, 1, d), lambda i: (cidx(i), 0, 0))
    full = lambda a: pl.BlockSpec(a.shape, lambda i: (0,) * a.ndim)
    assert 2 * hk == hv
    w_col = lambda n: pl.BlockSpec((None, d, hv), lambda i: (layer, 0, n))
    return pl.pallas_call(
        _mlstm_proj_kernel,
        grid=(t // tm,),
        in_specs=[pl.BlockSpec((tm, d), lambda i: (i, 0)), full(g), mod_spec, mod_spec,
                  w_col(0), w_col(1), w_col(2), full(w_g_pad), full(b_pad), full(w_g_t),
                  full(b_t), full(q_scale)],
        out_specs=[pl.BlockSpec((tm, 2 * hk + hv), lambda i: (i, 0)),
                   pl.BlockSpec((tm, hv), lambda i: (i, 0)),
                   pl.BlockSpec((tm, LANES), lambda i: (i, 0)),
                   pl.BlockSpec((ng, tm), lambda i: (0, i))],
        out_shape=[jax.ShapeDtypeStruct((t, 2 * hk + hv), BF16),
                   jax.ShapeDtypeStruct((t, hv), F32),
                   jax.ShapeDtypeStruct((t, LANES), F32),
                   jax.ShapeDtypeStruct((ng, t), F32)],
        compiler_params=_params("arbitrary"),
        name="mlstm_proj",
    )(x, g, shift, scale, w_in_all, w_in_all, w_in_all, w_g_pad, b_pad, w_g_t, b_t, q_scale)


def _log_sigmoid(x):
    return jnp.minimum(x, 0.0) - jnp.log(1.0 + jnp.exp(-jnp.abs(x)))


def _gate_cumsums(gc, gr, causal_bf, feeds_bf):
    bc = br = None
    for part in _split_bf16(_log_sigmoid(gc), 3):
        t = jnp.dot(causal_bf, part, preferred_element_type=F32)
        bc = t if bc is None else bc + t
    for part in _split_bf16(_log_sigmoid(gr), 3):
        t = jnp.dot(part, feeds_bf, preferred_element_type=F32)
        br = t if br is None else br + t
    return bc, br


def _mlstm_chunk(q, k, v, i_col, b_col, i_row, b_row, c_st, n_st, m_st, causal, rev):
    l = q.shape[0]
    g_row = i_row - b_row
    log_w = jnp.where(causal, g_row, -jnp.inf)
    c_t = jnp.maximum(m_st, jnp.max(log_w, axis=1, keepdims=True))
    dw = jnp.exp(log_w - c_t)
    sw = jnp.exp(m_st - c_t)
    a = _bdot_nt(q, k) * dw
    a_hi, a_lo = _split_bf16(a, 2)
    ones = jnp.ones((l, LANES), BF16)
    row_sum = (jnp.dot(a_hi, ones, preferred_element_type=F32)
               + jnp.dot(a_lo, ones, preferred_element_type=F32))[:, 0:1]
    n_hi, n_lo = _split_bf16(jnp.broadcast_to(n_st, (LANES, n_st.shape[1])), 2)
    qn = (_bdot_nt(q, n_hi) + _bdot_nt(q, n_lo))[:, 0:1]
    den = sw * qn + row_sum
    inv = 1.0 / jnp.maximum(jnp.abs(den), jnp.exp(-(b_col + c_t)))
    h = (sw * inv) * _bdot(q, c_st) + inv * jnp.dot(a_hi, v, preferred_element_type=F32)
    b_last = b_col[0:1, :] if rev else b_col[l - 1:l, :]
    log_k = b_last - b_col + i_col
    m_new = jnp.maximum(b_last + m_st, jnp.max(log_k, axis=0, keepdims=True))
    kw = jnp.exp(log_k - m_new)
    decay = jnp.exp(b_last + m_st - m_new)
    kwk = kw * k.astype(F32)
    c_new = decay * c_st + _bdot_tn(kwk, v)
    n_new = decay * n_st + jnp.sum(kwk, axis=0, keepdims=True)
    return h, c_new, n_new, m_new


def _mlstm_scan_kernel(*refs, n_chunks, zero_init):
    if zero_init:
        q_ref, k_ref, v_ref, gc_ref, gr_ref, hs_ref, c_ref, n_ref, m_ref, hb_scr = refs
        c_ref[...] = jnp.zeros_like(c_ref)
        n_ref[...] = jnp.zeros_like(n_ref)
        m_ref[...] = jnp.zeros_like(m_ref)
    else:
        (q_ref, k_ref, v_ref, gc_ref, gr_ref, c0_ref, n0_ref, m0_ref,
         hs_ref, c_ref, n_ref, m_ref, hb_scr) = refs
        c_ref[...] = c0_ref[...]
        n_ref[...] = n0_ref[...]
        m_ref[...] = m0_ref[...]
    l = MLSTM_CHUNK
    hh = MLSTM_HEADS
    dk = q_ref.shape[1] // hh
    dv = v_ref.shape[1] // hh

    tt = lax.broadcasted_iota(jnp.int32, (l, l), 0)
    ss = lax.broadcasted_iota(jnp.int32, (l, l), 1)
    masks = (ss <= tt, ss >= tt)
    masks_bf = tuple(jnp.where(m, 1.0, 0.0).astype(BF16) for m in masks)

    def body(c, carry):
        cr = n_chunks - 1 - c
        rows = (pl.ds(pl.multiple_of(c * l, l), l), pl.ds(pl.multiple_of(cr * l, l), l))
        grs = (gr_ref[c], gr_ref[cr])
        outs, states = ([], []), []
        for d in range(2):
            q, k, v, gc, gr = q_ref[rows[d], :], k_ref[rows[d], :], v_ref[rows[d], :], gc_ref[rows[d], :], grs[d]
            bc, br = _gate_cumsums(gc, gr, masks_bf[d], masks_bf[1 - d])
            for h in range(hh):
                ci, cf = 2 * d * hh + h, (2 * d + 1) * hh + h
                o, *st = _mlstm_chunk(q[:, h * dk:(h + 1) * dk], k[:, h * dk:(h + 1) * dk],
                                      v[:, h * dv:(h + 1) * dv], gc[:, ci:ci + 1], bc[:, cf:cf + 1],
                                      gr[ci:ci + 1, :], br[cf:cf + 1, :],
                                      c_ref[d, h], n_ref[d, h], m_ref[d, h], masks[d], d == 1)
                outs[d].append(o)
                states.append((d, h, st))
        hs_ref[rows[0], :] = jnp.concatenate(outs[0], axis=1)
        hb_scr[rows[1], :] = jnp.concatenate(outs[1], axis=1)
        for d, h, (c_new, n_new, m_new) in states:
            c_ref[d, h], n_ref[d, h], m_ref[d, h] = c_new, n_new, m_new
        return carry

    lax.fori_loop(0, n_chunks, body, 0)
    for h in range(hh):
        cols = pl.ds(h * dv, dv)
        hs = hs_ref[:, cols] + hb_scr[:, cols]
        hs_ref[:, cols] = hs * lax.rsqrt(jnp.mean(hs * hs, axis=-1, keepdims=True) + NORM_EPS)


def mlstm_scan(qkv, gcol, grow, init, row_off, n_seq, seq_len):
    hh = MLSTM_HEADS
    hv = qkv.shape[1] // 2
    dv = hv // hh
    dk = dv // 2
    l = MLSTM_CHUNK
    nc = seq_len // l
    ob = row_off // seq_len
    kern = functools.partial(_mlstm_scan_kernel, n_chunks=nc, zero_init=init is None)
    st = lambda *tail: pl.BlockSpec((None, 2, hh) + tail, lambda s: (s, 0, 0) + (0,) * len(tail))
    states = [st(dk, dv), st(1, dk), st(1, 1)]
    return pl.pallas_call(
        kern,
        grid=(n_seq,),
        in_specs=[pl.BlockSpec((seq_len, hh * dk), lambda s: (ob + s, 0)),
                  pl.BlockSpec((seq_len, hh * dk), lambda s: (ob + s, 1)),
                  pl.BlockSpec((seq_len, hv), lambda s: (ob + s, 1)),
                  pl.BlockSpec((seq_len, LANES), lambda s: (ob + s, 0)),
                  pl.BlockSpec((nc, 4 * hh, l), lambda s: (ob + s, 0, 0))]
                 + ([] if init is None else states),
        out_specs=[pl.BlockSpec((seq_len, hv), lambda s: (s, 0))] + states,
        out_shape=[jax.ShapeDtypeStruct((n_seq * seq_len, hv), F32),
                   jax.ShapeDtypeStruct((n_seq, 2, hh, dk, dv), F32),
                   jax.ShapeDtypeStruct((n_seq, 2, hh, 1, dk), F32),
                   jax.ShapeDtypeStruct((n_seq, 2, hh, 1, 1), F32)],
        scratch_shapes=[pltpu.VMEM((seq_len, hv), F32)],
        compiler_params=_params("arbitrary"),
        name="mlstm_scan",
    )(qkv, qkv, qkv, gcol, grow, *(() if init is None else init))


def _mla_proj_kernel(x_ref, g_ref, sh_ref, sc_ref, win_ref, qg_ref, kvg_ref, wqb_ref, cos_ref,
                     sin_ref, q_ref, ckv_ref, kpe_ref, *, q_lora, kv_lora, rope, n_heads):
    h = _modulated(x_ref[...], g_ref[...], sh_ref[...], sc_ref[...])
    proj = _bdot(h, win_ref[...])
    q_lat = proj[:, :q_lora]
    ckv_ref[...] = _rms(proj[:, q_lora:q_lora + kv_lora], kvg_ref[...])
    cos, sin = cos_ref[...], sin_ref[...]
    kpe = proj[:, q_lora + kv_lora:q_lora + kv_lora + rope]
    kpe_rot = proj[:, q_lora + kv_lora + rope:q_lora + kv_lora + 2 * rope]
    kpe_ref[...] = kpe * cos[:, :rope] + kpe_rot * sin[:, :rope]
    q = _bdot(_rms(q_lat, qg_ref[...]), wqb_ref[...])
    n = n_heads * LANES
    pieces = []
    for hd in range(n_heads):
        c = slice(hd * LANES, (hd + 1) * LANES)
        pieces.append(q[:, c])
        pieces.append(q[:, n:2 * n][:, c] * cos + q[:, 2 * n:][:, c] * sin)
    q_ref[...] = jnp.concatenate(pieces, axis=1).astype(BF16)


def _rot_cols(w):
    n = w.shape[1]
    j = np.arange(n)
    quarter = MLA_ROPE // 4
    first = (j % (2 * quarter)) < quarter
    src = np.where(first, j + quarter, j - quarter)
    sign = np.where(first, -1.0, 1.0).astype(np.float32)
    return w[:, src] * sign


def _rope_tables(n_ctx, dec_batch, dec_seq):
    quarter = MLA_ROPE // 4
    freq = np.power(np.float32(ROPE_BASE), -np.arange(quarter, dtype=np.float32) / np.float32(quarter))
    pos = np.arange(dec_seq)
    ang_r = (pos // GRID_W).astype(np.float32)[:, None] * freq[None, :]
    ang_c = (pos % GRID_W).astype(np.float32)[:, None] * freq[None, :]
    ang = np.concatenate([ang_r, ang_r, ang_c, ang_c], axis=1).astype(np.float32)
    cos = np.concatenate([np.ones((n_ctx, MLA_ROPE), np.float32)] + [np.cos(ang)] * dec_batch, axis=0)
    sin = np.concatenate([np.zeros((n_ctx, MLA_ROPE), np.float32)] + [np.sin(ang)] * dec_batch, axis=0)
    reps = LANES // MLA_ROPE
    return (jnp.asarray(np.tile(cos, (1, reps)), F32), jnp.asarray(np.tile(sin, (1, reps)), F32))


def mla_proj(x, g, shift, scale, w_in, q_g, kv_g, w_qb, cos, sin, n_ctx, dec_seq, tm=512):
    t, d = x.shape
    hh, nope, rope = MLA_HEADS, MLA_NOPE, MLA_ROPE
    q_lora = q_g.shape[1]
    kv_lora = kv_g.shape[1]
    w_in_ext = jnp.concatenate([w_in, _rot_cols(w_in[:, q_lora + kv_lora:])], axis=1)
    w3 = w_qb.reshape(q_lora, hh, nope + rope)
    w_qn = w3[:, :, :nope].reshape(q_lora, hh * nope)
    w_qp = w3[:, :, nope:].reshape(q_lora, hh * rope)
    assert nope == LANES and rope <= LANES
    lane_pad = lambda w: jnp.pad(w.reshape(q_lora, hh, rope),
                                 ((0, 0), (0, 0), (0, LANES - rope))).reshape(q_lora, hh * LANES)
    w_qb_ext = jnp.concatenate([w_qn, lane_pad(w_qp), lane_pad(_rot_cols(w_qp))], axis=1)
    cidx = _cvec_index(n_ctx, dec_seq, tm)
    mod_spec = pl.BlockSpec((None, 1, d), lambda i: (cidx(i), 0, 0))
    full = lambda a: pl.BlockSpec(a.shape, lambda i: (0,) * a.ndim)
    kern = functools.partial(_mla_proj_kernel, q_lora=q_lora, kv_lora=kv_lora, rope=rope, n_heads=hh)
    return pl.pallas_call(
        kern,
        grid=(t // tm,),
        in_specs=[pl.BlockSpec((tm, d), lambda i: (i, 0)), full(g), mod_spec, mod_spec,
                  full(w_in_ext), full(q_g), full(kv_g), full(w_qb_ext),
                  pl.BlockSpec((tm, LANES), lambda i: (i, 0)),
                  pl.BlockSpec((tm, LANES), lambda i: (i, 0))],
        out_specs=[pl.BlockSpec((tm, 2 * hh * LANES), lambda i: (i, 0)),
                   pl.BlockSpec((tm, kv_lora), lambda i: (i, 0)),
                   pl.BlockSpec((tm, rope), lambda i: (i, 0))],
        out_shape=[jax.ShapeDtypeStruct((t, 2 * hh * LANES), BF16),
                   jax.ShapeDtypeStruct((t, kv_lora), F32),
                   jax.ShapeDtypeStruct((t, rope), F32)],
        compiler_params=_params("arbitrary"),
        name="mla_proj",
    )(x, g, shift, scale, w_in_ext, q_g, kv_g, w_qb_ext, cos, sin)


def _mla_kv_kernel(ckv_ref, kp_ref, w_ref, k_ref, v_ref):
    kv = _bdot(ckv_ref[...], w_ref[...])
    n = v_ref.shape[1]
    kp = kp_ref[...]
    pieces = []
    for hd in range(n // LANES):
        pieces += [kv[:, hd * LANES:(hd + 1) * LANES].astype(BF16), kp]
    k_ref[...] = jnp.concatenate(pieces, axis=1)
    v_ref[...] = kv[:, n:].astype(BF16)


def mla_kv(ckv_all, kp_pad, w_kvb, tm=512):
    r, kv_lora = ckv_all.shape
    hh, nope, vd = MLA_HEADS, MLA_NOPE, MLA_V
    w3 = w_kvb.reshape(kv_lora, hh, nope + vd)
    w_perm = jnp.concatenate([w3[:, :, :nope].reshape(kv_lora, hh * nope),
                              w3[:, :, nope:].reshape(kv_lora, hh * vd)], axis=1)
    return pl.pallas_call(
        _mla_kv_kernel,
        grid=(r // tm,),
        in_specs=[pl.BlockSpec((tm, kv_lora), lambda i: (i, 0)),
                  pl.BlockSpec((tm, LANES), lambda i: (i, 0)),
                  pl.BlockSpec(w_perm.shape, lambda i: (0, 0))],
        out_specs=[pl.BlockSpec((tm, 2 * hh * LANES), lambda i: (i, 0)),
                   pl.BlockSpec((tm, hh * vd), lambda i: (i, 0))],
        out_shape=[jax.ShapeDtypeStruct((r, 2 * hh * LANES), BF16),
                   jax.ShapeDtypeStruct((r, hh * vd), BF16)],
        compiler_params=_params("arbitrary"),
        name="mla_kv",
    )(ckv_all, kp_pad, w_perm)


def _attn_kernel(q_ref, k_ref, v_ref, o_ref, *, scale):
    c = scale * np.log2(np.e)
    for h in range(MLA_HEADS):
        hk = slice(2 * h * LANES, 2 * (h + 1) * LANES)
        s = _bdot_nt(q_ref[:, hk], k_ref[:, hk])
        e = jnp.exp2((s - jnp.max(s, axis=-1, keepdims=True)) * c)
        o = _bdot(e, v_ref[:, h * MLA_V:(h + 1) * MLA_V]) / jnp.sum(e, axis=-1, keepdims=True)
        o_ref[:, h * MLA_V:(h + 1) * MLA_V] = o.astype(BF16)


def mla_attention(q, k, v, q_row_off, k_row_off, n_seq, q_len, k_len):
    tq = ATTN_Q_BLOCK
    qb = q_len // tq
    q0 = q_row_off // tq
    k0 = k_row_off // k_len
    dq, dv = q.shape[1], v.shape[1]
    kern = functools.partial(_attn_kernel, scale=(MLA_NOPE + MLA_ROPE) ** -0.5)
    return pl.pallas_call(
        kern,
        grid=(n_seq, qb),
        in_specs=[pl.BlockSpec((tq, dq), lambda s, j: (q0 + s * qb + j, 0)),
                  pl.BlockSpec((k_len, dq), lambda s, j: (k0 + s, 0)),
                  pl.BlockSpec((k_len, dv), lambda s, j: (k0 + s, 0))],
        out_specs=pl.BlockSpec((tq, dv), lambda s, j: (s * qb + j, 0)),
        out_shape=jax.ShapeDtypeStruct((n_seq * q_len, dv), BF16),
        compiler_params=_params("arbitrary", "arbitrary"),
        name="mla_attention",
    )(q, k, v)


def kernel(x_prompt, x_sample, state_mlstm_C, state_mlstm_n, state_mlstm_m, cache_mla_ckv,
           cache_mla_kpe, c, c_ctx, w_ada, b_ada, norm_mix, norm_ffn, norm_final, w_pool,
           pool_scale, w_mlstm_in, b_mlstm_gate, mlstm_head_g, w_mlstm_out, w_mla_in, mla_q_g,
           mla_kv_g, w_mla_qb, w_mla_kvb, w_mla_out, w_router, b_router, w_exp_gate, w_exp_up,
           w_exp_down):
    batch, seq, d = x_prompt.shape
    dec_batch, dec_seq, _ = x_sample.shape
    depth = w_ada.shape[0]
    n_ctx = batch * seq
    n_lat = dec_batch * dec_seq
    hh = MLSTM_HEADS
    past = cache_mla_ckv.shape[2]

    x = (x_prompt.reshape(n_ctx, d), x_sample.reshape(n_lat, d))

    n_cv = 1 + dec_batch
    cvecs = jnp.concatenate([c_ctx[None, :], c, jnp.zeros((SUBLANES - n_cv % SUBLANES, d), F32)], axis=0)
    mod = ada_mod_all(cvecs, w_ada, b_ada).reshape(depth, cvecs.shape[0], 6, 1, d)

    w_router_pad = jnp.pad(w_router, ((0, 0), (0, LANES - N_EXPERTS)))
    w_router_hi = w_router_pad.astype(BF16)
    w_router_lo = (w_router_pad - w_router_hi.astype(F32)).astype(BF16)
    w_router_split = jnp.concatenate([w_router_hi, w_router_lo], axis=1)
    b_router_col = b_router.reshape(N_EXPERTS, 1)
    g_final = norm_final.reshape(1, d)
    row = lambda a: a.reshape(1, -1)

    outs = {}
    for i in range(depth):
        kind, j = i % 3, i // 3
        m = [mod[i, :n_cv, k] for k in range(6)]
        g_mix = row(norm_mix[i])
        if kind == 0:
            x = pool_layer(x, g_mix, m[0], m[1], m[2], w_pool[j], row(pool_scale[j]),
                           n_ctx, seq, dec_seq)
        elif kind == 1:
            qkv, o_gate, gcol, grow = mlstm_proj(x, g_mix, m[0], m[1], w_mlstm_in, j,
                                                 b_mlstm_gate[j], n_ctx, dec_seq)
            t = n_ctx + n_lat
            l = MLSTM_CHUNK
            grow_c = grow.reshape(4 * hh, t // l, l).transpose(1, 0, 2)
            hs_c, c_new, n_new, m_new = mlstm_scan(qkv, gcol, grow_c, None, 0, batch, seq)
            init = (state_mlstm_C[:, j], state_mlstm_n[:, j][:, :, :, None, :],
                    state_mlstm_m[:, j][:, :, :, None, None])
            hs_l, _, _, _ = mlstm_scan(qkv, gcol, grow_c, init, n_ctx, dec_batch, dec_seq)
            outs["C"] = c_new[:, None]
            outs["n"] = n_new[:, None, :, :, 0, :]
            outs["m"] = m_new[:, None, :, :, 0, 0]
            x = mlstm_out(x, hs_c, hs_l, o_gate, row(mlstm_head_g[j]), w_mlstm_out[j], m[2],
                          n_ctx, dec_seq)
        else:
            cos, sin = _rope_tables(n_ctx, dec_batch, dec_seq)
            q_cat, ckv, kpe = mla_proj(x, g_mix, m[0], m[1], w_mla_in[j], row(mla_q_g[j]),
                                       row(mla_kv_g[j]), w_mla_qb[j], cos, sin, n_ctx, dec_seq)
            lat_parts_c, lat_parts_p = [], []
            for b in range(dec_batch):
                lo = n_ctx + b * dec_seq
                lat_parts_c += [cache_mla_ckv[b, j], ckv[lo:lo + dec_seq]]
                lat_parts_p += [cache_mla_kpe[b, j], kpe[lo:lo + dec_seq]]
            ckv_all = jnp.concatenate(lat_parts_c + [ckv[:n_ctx]], axis=0)
            kp_all = jnp.concatenate(lat_parts_p + [kpe[:n_ctx]], axis=0).astype(BF16)
            kp_pad = jnp.pad(kp_all, ((0, 0), (0, LANES - kp_all.shape[1])))
            k_cat, v = mla_kv(ckv_all, kp_pad, w_mla_kvb[j])
            k_lat = past + dec_seq
            o_c = mla_attention(q_cat, k_cat, v, 0, dec_batch * k_lat, batch, seq, seq)
            o_l = mla_attention(q_cat, k_cat, v, n_ctx, 0, dec_batch, dec_seq, k_lat)
            outs["ckv"] = ckv[:n_ctx].reshape(batch, 1, seq, -1)
            outs["kpe"] = kpe[:n_ctx].reshape(batch, 1, seq, -1)
            x = resid_proj(x, o_c, o_l, w_mla_out[j], m[2], n_ctx, dec_seq)
        x = moe_layer(x, row(norm_ffn[i]), m[3], m[4], m[5], w_router_split, b_router_col,
                      w_exp_gate, w_exp_up, w_exp_down, i, g_final, n_ctx, dec_seq,
                      final_norm=(i == depth - 1), split_output=(i == depth - 1))

    y_prompt = x[0].reshape(batch, seq, d)
    y_sample = x[1].reshape(dec_batch, dec_seq, d)
    return (y_prompt, y_sample, outs["C"], outs["n"], outs["m"], outs["ckv"], outs["kpe"])
```

```python
import functools

import numpy as np
import jax
import jax.numpy as jnp
from jax import lax
from jax.experimental import pallas as pl
from jax.experimental.pallas import tpu as pltpu

F32 = jnp.float32
BF16 = jnp.bfloat16

NORM_EPS = 1e-6
GRID_W = 64
POOL_WINDOWS = (2, 4, 8, 16)
MLSTM_HEADS = 4
MLSTM_CHUNK = 256
MLA_HEADS = 8
MLA_NOPE = 128
MLA_ROPE = 64
MLA_V = 128
ROPE_BASE = 10000.0
N_EXPERTS = 16
N_EXPERT_GROUPS = 4
EXPERTS_PER_GROUP = N_EXPERTS // N_EXPERT_GROUPS

LANES = 128
SUBLANES = 8
VMEM_LIMIT = 56 * 1024 * 1024
POOL_TILE = 256
POOL_HALO = 8
ATTN_Q_BLOCK = 256
MOE_TILE = 512
MOE_SEG_ALIGN = 16
MOE_ROW_BLOCK = 160
MOE_WIDE_BLOCK = 224


def _params(*sem):
    return pltpu.CompilerParams(dimension_semantics=sem, vmem_limit_bytes=VMEM_LIMIT)


def _rms(x, g):
    return x * lax.rsqrt(jnp.mean(x * x, axis=-1, keepdims=True) + NORM_EPS) * g


def _modulated(x, g, shift, scale):
    return _rms(x, g) * (1.0 + scale) + shift


def _silu(x):
    return x * jax.nn.sigmoid(x)


def _bdot(a, b):
    return jnp.dot(a.astype(BF16), b.astype(BF16), preferred_element_type=F32)


def _bdot_nt(a, b):
    return lax.dot_general(a.astype(BF16), b.astype(BF16), (((1,), (1,)), ((), ())),
                           preferred_element_type=F32)


def _bdot_tn(a, b):
    return lax.dot_general(a.astype(BF16), b.astype(BF16), (((0,), (0,)), ((), ())),
                           preferred_element_type=F32)


def _cvec_index(n_ctx, dec_seq, tm):
    def idx(i):
        r = i * tm
        return jnp.where(r < n_ctx, 0, (r - n_ctx) // dec_seq + 1)
    return idx


def _ada_kernel(c_ref, w_ref, b_ref, o_ref):
    o_ref[...] = _bdot(_silu(c_ref[...]), w_ref[...]) + b_ref[...]


def ada_mod_all(cvecs, w_ada, b_ada, tn=1536):
    depth, d, n6 = w_ada.shape
    rows = cvecs.shape[0]
    return pl.pallas_call(
        _ada_kernel,
        grid=(depth, n6 // tn),
        in_specs=[pl.BlockSpec((rows, d), lambda l, n: (0, 0)),
                  pl.BlockSpec((None, d, tn), lambda l, n: (l, 0, n)),
                  pl.BlockSpec((None, 1, tn), lambda l, n: (l, 0, n))],
        out_specs=pl.BlockSpec((None, rows, tn), lambda l, n: (l, 0, n)),
        out_shape=jax.ShapeDtypeStruct((depth, rows, n6), F32),
        compiler_params=_params("arbitrary", "arbitrary"),
        name="ada_mod",
    )(cvecs, w_ada, b_ada.reshape(depth, 1, n6))


def _pool_kernel(*refs, n_ctx_tiles, ctx_seq_tiles, lat_seq_tiles, split_input):
    n_x = 6 if split_input else 3
    x_refs = refs[:n_x]
    g_ref, sh_ref, sc_ref, gt_ref, wp_ref, ps_ref, o_ref, buf_ref = refs[n_x:]
    i = pl.program_id(0)
    is_ctx = i < n_ctx_tiles
    j = jnp.where(is_ctx, i % ctx_seq_tiles, (i - n_ctx_tiles) % lat_seq_tiles)
    nt = jnp.where(is_ctx, ctx_seq_tiles, lat_seq_tiles)
    g, sh, sc = g_ref[...], sh_ref[...], sc_ref[...]
    tp, hl = POOL_TILE, POOL_HALO
    gw = o_ref.shape[1] // len(POOL_WINDOWS)

    def fill(xc_ref, xp_ref, xn_ref):
        buf_ref[pl.ds(0, hl), :] = jnp.where(j == 0, 0.0, _modulated(xp_ref[...], g, sh, sc))
        buf_ref[pl.ds(hl, tp), :] = _modulated(xc_ref[...], g, sh, sc)
        buf_ref[pl.ds(hl + tp, hl), :] = jnp.where(j == nt - 1, 0.0,
                                                    _modulated(xn_ref[...], g, sh, sc))
        o_ref[...] = xc_ref[...]

    if split_input:
        pl.when(is_ctx)(lambda: fill(*x_refs[:3]))
        pl.when(jnp.logical_not(is_ctx))(lambda: fill(*x_refs[3:]))
    else:
        fill(*x_refs)

    pos = j * tp + lax.broadcasted_iota(jnp.int32, (tp, 1), 0)
    seq_len = nt * tp
    for gi, w in enumerate(POOL_WINDOWS):
        cols = pl.ds(gi * gw, gw)
        acc = buf_ref[pl.ds(hl - w // 2, tp), cols]
        for d in range(-w // 2 + 1, w // 2):
            acc = acc + buf_ref[pl.ds(hl + d, tp), cols]
        cnt = jnp.minimum(pos + w // 2, seq_len) - jnp.maximum(pos - w // 2, 0)
        pooled = acc / cnt.astype(F32) - buf_ref[pl.ds(hl, tp), cols]
        y = _bdot(pooled, wp_ref[gi]) * ps_ref[:, cols]
        o_ref[:, cols] = o_ref[:, cols] + gt_ref[:, cols] * y


def _halo_specs(tp, hl, d, tile_off, n_rows):
    hb = tp // hl
    last_tile, last_hblk = n_rows // tp - 1, n_rows // hl - 1
    tile = lambda i: jnp.clip(i - tile_off, 0, last_tile)
    return [pl.BlockSpec((tp, d), lambda i: (tile(i), 0)),
            pl.BlockSpec((hl, d), lambda i: (jnp.clip(tile(i) * hb - 1, 0, last_hblk), 0)),
            pl.BlockSpec((hl, d), lambda i: (jnp.clip((tile(i) + 1) * hb, 0, last_hblk), 0))]


def pool_layer(xs, g, shift, scale, gate, w_pool, pool_scale, n_ctx, seq, dec_seq):
    split = isinstance(xs, tuple)
    tp, hl = POOL_TILE, POOL_HALO
    if split:
        d = xs[0].shape[1]
        t = xs[0].shape[0] + xs[1].shape[0]
        x_specs = (_halo_specs(tp, hl, d, 0, xs[0].shape[0])
                   + _halo_specs(tp, hl, d, n_ctx // tp, xs[1].shape[0]))
        x_args = (xs[0],) * 3 + (xs[1],) * 3
    else:
        t, d = xs.shape
        x_specs = _halo_specs(tp, hl, d, 0, t)
        x_args = (xs,) * 3
    cidx = _cvec_index(n_ctx, dec_seq, tp)
    mod_spec = pl.BlockSpec((None, 1, d), lambda i: (cidx(i), 0, 0))
    row_spec = pl.BlockSpec((1, d), lambda i: (0, 0))
    kern = functools.partial(_pool_kernel, n_ctx_tiles=n_ctx // tp, ctx_seq_tiles=seq // tp,
                             lat_seq_tiles=dec_seq // tp, split_input=split)
    return pl.pallas_call(
        kern,
        grid=(t // tp,),
        in_specs=x_specs + [row_spec, mod_spec, mod_spec, mod_spec,
                            pl.BlockSpec(w_pool.shape, lambda i: (0, 0, 0)), row_spec],
        out_specs=pl.BlockSpec((tp, d), lambda i: (i, 0)),
        out_shape=jax.ShapeDtypeStruct((t, d), F32),
        scratch_shapes=[pltpu.VMEM((tp + 2 * hl, d), F32)],
        compiler_params=_params("arbitrary"),
        name="pool_mixer",
    )(*x_args, g, shift, scale, gate, w_pool, pool_scale)


def _route(sel, scores):
    e, tm = sel.shape
    row = lax.broadcasted_iota(jnp.int32, (e, tm), 0)
    best = jnp.zeros((1, tm), jnp.int32)
    best_sc = None
    for gidx in range(N_EXPERT_GROUPS):
        r = [sel[gidx * EXPERTS_PER_GROUP + k:gidx * EXPERTS_PER_GROUP + k + 1, :]
             for k in range(EXPERTS_PER_GROUP)]
        top2 = None
        for a in range(EXPERTS_PER_GROUP):
            for b in range(a + 1, EXPERTS_PER_GROUP):
                s = r[a] + r[b]
                top2 = s if top2 is None else jnp.maximum(top2, s)
        if best_sc is None:
            best_sc = top2
        else:
            better = top2 > best_sc
            best = jnp.where(better, gidx, best)
            best_sc = jnp.where(better, top2, best_sc)
    neg = -jnp.inf
    masked = jnp.where(row // EXPERTS_PER_GROUP == best, sel, neg)
    m1 = jnp.max(masked, axis=0, keepdims=True)
    i1 = jnp.min(jnp.where(masked == m1, row, e), axis=0, keepdims=True)
    masked2 = jnp.where(row == i1, neg, masked)
    m2 = jnp.max(masked2, axis=0, keepdims=True)
    i2 = jnp.min(jnp.where(masked2 == m2, row, e), axis=0, keepdims=True)
    hot1 = row == i1
    hot2 = row == i2
    w1 = jnp.sum(jnp.where(hot1, scores, 0.0), axis=0, keepdims=True)
    w2 = jnp.sum(jnp.where(hot2, scores, 0.0), axis=0, keepdims=True)
    tot = w1 + w2
    return best, jnp.where(hot1, w1 / tot, 0.0) + jnp.where(hot2, w2 / tot, 0.0)


def _split_bf16(a, parts):
    out = []
    for _ in range(parts):
        p = a.astype(BF16)
        out.append(p)
        a = a - p.astype(F32)
    return out


def _pad_rows(a, rows):
    return jnp.concatenate([a, jnp.zeros((rows - a.shape[0], a.shape[1]), a.dtype)], axis=0)


def _moe_kernel(x_ref, g_ref, sh_ref, sc_ref, gt_ref, wr_ref, br_ref, wgf_ref, wuf_ref, wdf_ref,
                gf_ref, *rest, final_norm, n_ctx_tiles, n_experts):
    out_refs, (wgu_ref, wd_ref), scratch = rest[:-5], rest[-5:-3], rest[-3:]
    i = pl.program_id(0)
    f = wgf_ref.shape[1]
    for e in range(n_experts):
        gi, k = divmod(e, EXPERTS_PER_GROUP)

        @pl.when(i == e)
        def _(gi=gi, k=k):
            wgu_ref[gi, :, pl.ds(2 * k * f, f)] = wgf_ref[...].astype(BF16)
            wgu_ref[gi, :, pl.ds((2 * k + 1) * f, f)] = wuf_ref[...].astype(BF16)
            wd_ref[gi, pl.ds(k * f, f), :] = wdf_ref[...].astype(BF16)

    @pl.when(i >= n_experts)
    def _():
        _moe_tile(x_ref, g_ref, sh_ref, sc_ref, gt_ref, wr_ref, br_ref, wgu_ref, wd_ref,
                  gf_ref, out_refs, scratch, i - n_experts, final_norm, n_ctx_tiles)


def _moe_tile(x_ref, g_ref, sh_ref, sc_ref, gt_ref, wr_ref, br_ref, wgu_ref, wd_ref,
              gf_ref, out_refs, scratch, tile, final_norm, n_ctx_tiles):
    hp_scr, cw_scr, yp_scr = scratch
    tr, d = x_ref.shape
    trp = hp_scr.shape[0]
    ng, eg = N_EXPERT_GROUPS, EXPERTS_PER_GROUP
    x = x_ref[...]
    h = _modulated(x, g_ref[...], sh_ref[...], sc_ref[...])
    hb = h.astype(BF16)

    h_lo = (h - hb.astype(F32)).astype(BF16)
    wr = wr_ref[...]
    lg = jnp.dot(hb, wr, preferred_element_type=F32)
    logits = lg[:, :LANES] + lg[:, LANES:] + jnp.dot(h_lo, wr[:, :LANES], preferred_element_type=F32)
    scores = jax.nn.sigmoid(logits.T[:N_EXPERTS, :])
    best, comb_t = _route(scores + br_ref[...], scores)

    grp = lax.broadcasted_iota(jnp.int32, (SUBLANES, tr), 0)
    hot_t = (grp == best).astype(F32)
    cw_t = hot_t[0:1, :] * comb_t[0:eg, :]
    for gi in range(1, ng):
        cw_t = cw_t + hot_t[gi:gi + 1, :] * comb_t[gi * eg:(gi + 1) * eg, :]
    cw_c = _pad_rows(cw_t, LANES).T

    ia = lax.broadcasted_iota(jnp.int32, (tr, tr), 0)
    ib = lax.broadcasted_iota(jnp.int32, (tr, tr), 1)
    before = jnp.where(ia < ib, 1.0, 0.0).astype(BF16)
    rank_t = jnp.dot(hot_t.astype(BF16), before, preferred_element_type=F32)

    starts, counts = [], []
    off = jnp.int32(0)
    for gi in range(ng):
        n = jnp.sum(hot_t[gi:gi + 1, :]).astype(jnp.int32)
        n = ((n + MOE_SEG_ALIGN - 1) // MOE_SEG_ALIGN) * MOE_SEG_ALIGN
        starts.append(off)
        counts.append(n)
        off = off + n

    pos_t = hot_t[0:1, :] * (rank_t[0:1, :] + starts[0].astype(F32))
    for gi in range(1, ng):
        pos_t = pos_t + hot_t[gi:gi + 1, :] * (rank_t[gi:gi + 1, :] + starts[gi].astype(F32))
    pos_c = _pad_rows(pos_t, LANES).T[:, 0:1]
    used = tr + ng * MOE_SEG_ALIGN
    perm = jnp.where(lax.broadcasted_iota(jnp.int32, (used, tr), 0) == pos_t.astype(jnp.int32),
                     1.0, 0.0).astype(BF16)
    perm_t = jnp.where(lax.broadcasted_iota(jnp.int32, (tr, used), 1) == pos_c.astype(jnp.int32),
                       1.0, 0.0).astype(BF16)

    hp_scr[pl.ds(0, used), :] = jnp.dot(perm, hb, preferred_element_type=F32).astype(BF16)
    hp_scr[pl.ds(used, trp - used), :] = jnp.zeros((trp - used, d), BF16)
    cw_pair = jnp.dot(perm, jnp.concatenate(_split_bf16(cw_c, 2), axis=1), preferred_element_type=F32)
    cw_scr[pl.ds(0, used), :] = cw_pair[:, :LANES] + cw_pair[:, LANES:]
    cw_scr[pl.ds(used, trp - used), :] = jnp.zeros((trp - used, LANES), F32)
    yp_scr[...] = jnp.zeros_like(yp_scr)

    f = wd_ref.shape[1] // eg

    def expert_block(gi, row0, n_rows):
        rows = pl.ds(pl.multiple_of(row0, MOE_SEG_ALIGN), n_rows)
        cwb = cw_scr[rows, :]
        gu = jnp.dot(hp_scr[rows, :], wgu_ref[gi], preferred_element_type=F32)
        hid = [(_silu(gu[:, 2 * k * f:(2 * k + 1) * f]) * gu[:, (2 * k + 1) * f:(2 * k + 2) * f]
                * cwb[:, k:k + 1]).astype(BF16) for k in range(eg)]
        yp_scr[rows, :] = jnp.dot(jnp.concatenate(hid, axis=1), wd_ref[gi],
                                  preferred_element_type=F32)

    sb, wide = MOE_ROW_BLOCK, MOE_WIDE_BLOCK
    for gi in range(ng):
        use_wide = jnp.logical_and(counts[gi] > sb, counts[gi] <= wide)
        pl.when(use_wide)(functools.partial(expert_block, gi, starts[gi], wide))
        n_blocks = jnp.where(use_wide, 0, (counts[gi] + sb - 1) // sb)

        def block(b, carry, gi=gi):
            expert_block(gi, starts[gi] + b * sb, sb)
            return carry
        lax.fori_loop(0, n_blocks, block, 0)

    y_hi, y_lo = _split_bf16(yp_scr[pl.ds(0, used), :], 2)
    moe = (jnp.dot(perm_t, y_hi, preferred_element_type=F32)
           + jnp.dot(perm_t, y_lo, preferred_element_type=F32))
    out = x + gt_ref[...] * moe
    if final_norm:
        out = _rms(out, gf_ref[...])
    if n_ctx_tiles is None:
        out_refs[0][...] = out
    else:
        is_ctx = tile < n_ctx_tiles

        @pl.when(is_ctx)
        def _():
            out_refs[0][...] = out

        @pl.when(jnp.logical_not(is_ctx))
        def _():
            out_refs[1][...] = out


def moe_layer(x, g, shift, scale, gate, w_router_split, b_router_col, wg, wu, wd, layer, g_final,
              n_ctx, dec_seq, final_norm, split_output=False, tm=MOE_TILE):
    t, d = x.shape
    _, n_e, _, f = wg.shape
    trp = tm + N_EXPERT_GROUPS * MOE_SEG_ALIGN + max(MOE_ROW_BLOCK, MOE_WIDE_BLOCK)
    tile = lambda i: jnp.maximum(i - n_e, 0)
    cidx = _cvec_index(n_ctx, dec_seq, tm)
    mod_spec = pl.BlockSpec((None, 1, d), lambda i: (cidx(tile(i)), 0, 0))
    row_spec = pl.BlockSpec((1, d), lambda i: (0, 0))
    expert = lambda i: jnp.minimum(i, n_e - 1)
    nct = n_ctx // tm
    if split_output:
        out_specs = [pl.BlockSpec((tm, d), lambda i: (jnp.minimum(tile(i), nct - 1), 0)),
                     pl.BlockSpec((tm, d), lambda i: (jnp.maximum(tile(i) - nct, 0), 0))]
        out_shape = [jax.ShapeDtypeStruct((n_ctx, d), F32), jax.ShapeDtypeStruct((t - n_ctx, d), F32)]
    else:
        out_specs = pl.BlockSpec((tm, d), lambda i: (tile(i), 0))
        out_shape = jax.ShapeDtypeStruct((t, d), F32)
    return pl.pallas_call(
        functools.partial(_moe_kernel, final_norm=final_norm,
                          n_ctx_tiles=nct if split_output else None, n_experts=n_e),
        grid=(n_e + t // tm,),
        in_specs=[pl.BlockSpec((tm, d), lambda i: (tile(i), 0)),
                  row_spec, mod_spec, mod_spec, mod_spec,
                  pl.BlockSpec(w_router_split.shape, lambda i: (0, 0)),
                  pl.BlockSpec(b_router_col.shape, lambda i: (0, 0)),
                  pl.BlockSpec((None, None, d, f), lambda i: (layer, expert(i), 0, 0)),
                  pl.BlockSpec((None, None, d, f), lambda i: (layer, expert(i), 0, 0)),
                  pl.BlockSpec((None, None, f, d), lambda i: (layer, expert(i), 0, 0)),
                  row_spec],
        out_specs=out_specs,
        out_shape=out_shape,
        scratch_shapes=[pltpu.VMEM((N_EXPERT_GROUPS, d, 2 * EXPERTS_PER_GROUP * f), BF16),
                        pltpu.VMEM((N_EXPERT_GROUPS, EXPERTS_PER_GROUP * f, d), BF16),
                        pltpu.VMEM((trp, d), BF16), pltpu.VMEM((trp, LANES), F32),
                        pltpu.VMEM((trp, d), F32)],
        compiler_params=_params("arbitrary"),
        name="moe",
    )(x, g, shift, scale, gate, w_router_split, b_router_col, wg, wu, wd, g_final)


def _per_stream(n_ctx_tiles, ctx_ref, lat_ref, fn):
    i = pl.program_id(0)

    @pl.when(i < n_ctx_tiles)
    def _():
        fn(ctx_ref[...])

    @pl.when(i >= n_ctx_tiles)
    def _():
        fn(lat_ref[...])


def _stream_specs(tm, k, n_ctx_tiles):
    return [pl.BlockSpec((tm, k), lambda i: (jnp.minimum(i, n_ctx_tiles - 1), 0)),
            pl.BlockSpec((tm, k), lambda i: (jnp.maximum(i - n_ctx_tiles, 0), 0))]


def _resid_proj_kernel(x_ref, ac_ref, al_ref, w_ref, gt_ref, o_ref, *, n_ctx_tiles):
    def run(a):
        o_ref[...] = x_ref[...] + gt_ref[...] * _bdot(a, w_ref[...])
    _per_stream(n_ctx_tiles, ac_ref, al_ref, run)


def _mlstm_out_kernel(x_ref, hc_ref, hl_ref, og_ref, hg_ref, w_ref, gt_ref, o_ref, *, n_ctx_tiles):
    def run(hs):
        a = jax.nn.sigmoid(og_ref[...]) * (hs * hg_ref[...])
        o_ref[...] = x_ref[...] + gt_ref[...] * _bdot(a, w_ref[...])
    _per_stream(n_ctx_tiles, hc_ref, hl_ref, run)


def resid_proj(x, a_ctx, a_lat, w, gate, n_ctx, dec_seq, tm=512):
    t, d = x.shape
    k = a_ctx.shape[1]
    cidx = _cvec_index(n_ctx, dec_seq, tm)
    nct = n_ctx // tm
    return pl.pallas_call(
        functools.partial(_resid_proj_kernel, n_ctx_tiles=nct),
        grid=(t // tm,),
        in_specs=[pl.BlockSpec((tm, d), lambda i: (i, 0))] + _stream_specs(tm, k, nct) + [
            pl.BlockSpec((k, d), lambda i: (0, 0)),
            pl.BlockSpec((None, 1, d), lambda i: (cidx(i), 0, 0))],
        out_specs=pl.BlockSpec((tm, d), lambda i: (i, 0)),
        out_shape=jax.ShapeDtypeStruct((t, d), F32),
        compiler_params=_params("arbitrary"),
        name="resid_proj",
    )(x, a_ctx, a_lat, w, gate)


def mlstm_out(x, hs_ctx, hs_lat, o_gate, head_g, w, gate, n_ctx, dec_seq, tm=512):
    t, d = x.shape
    k = hs_ctx.shape[1]
    cidx = _cvec_index(n_ctx, dec_seq, tm)
    nct = n_ctx // tm
    return pl.pallas_call(
        functools.partial(_mlstm_out_kernel, n_ctx_tiles=nct),
        grid=(t // tm,),
        in_specs=[pl.BlockSpec((tm, d), lambda i: (i, 0))] + _stream_specs(tm, k, nct) + [
            pl.BlockSpec((tm, k), lambda i: (i, 0)),
            pl.BlockSpec((1, k), lambda i: (0, 0)),
            pl.BlockSpec((k, d), lambda i: (0, 0)),
            pl.BlockSpec((None, 1, d), lambda i: (cidx(i), 0, 0))],
        out_specs=pl.BlockSpec((tm, d), lambda i: (i, 0)),
        out_shape=jax.ShapeDtypeStruct((t, d), F32),
        compiler_params=_params("arbitrary"),
        name="mlstm_out",
    )(x, hs_ctx, hs_lat, o_gate, head_g, w, gate)


def _mlstm_proj_kernel(x_ref, g_ref, sh_ref, sc_ref, wqk_ref, wv_ref, wo_ref, wgt_ref, bg_ref,
                       wgtt_ref, bgt_ref, qs_ref, qkv_ref, o_ref, gc_ref, gr_ref):
    h = _modulated(x_ref[...], g_ref[...], sh_ref[...], sc_ref[...]).astype(BF16)
    nqk = wqk_ref.shape[1]
    qkv_ref[:, :nqk] = (jnp.dot(h, wqk_ref[...].astype(BF16), preferred_element_type=F32)
                        * qs_ref[...]).astype(BF16)
    qkv_ref[:, nqk:] = jnp.dot(h, wv_ref[...].astype(BF16),
                               preferred_element_type=F32).astype(BF16)
    o_ref[...] = jnp.dot(h, wo_ref[...].astype(BF16), preferred_element_type=F32)
    gc_ref[...] = jnp.dot(h, wgt_ref[...].astype(BF16), preferred_element_type=F32) + bg_ref[...]
    gr_ref[...] = _bdot_nt(wgtt_ref[...], h) + bgt_ref[...]


def mlstm_proj(x, g, shift, scale, w_in_all, layer, b_gate, n_ctx, dec_seq, tm=512):
    t, d = x.shape
    hh = MLSTM_HEADS
    hv = d
    hk = hv // 2
    ng = 4 * hh
    w_g = w_in_all[layer, :, 2 * hk + 2 * hv:]
    w_g_pad = jnp.pad(w_g, ((0, 0), (0, LANES - ng)))
    b_pad = jnp.pad(b_gate.reshape(1, ng), ((0, 0), (0, LANES - ng)))
    w_g_t = w_g.T
    b_t = b_gate.reshape(ng, 1)
    dk = hk // hh
    q_scale = jnp.concatenate([jnp.full((1, hk), dk ** -0.5, F32), jnp.ones((1, hk), F32)], axis=1)
    cidx = _cvec_index(n_ctx, dec_seq, tm)
    mod_spec = pl.BlockSpec((None, 1, d), lambda i: (cidx(i), 0, 0))
    full = lambda a: pl.BlockSpec(a.shape, lambda i: (0,) * a.ndim)
    assert 2 * hk == hv
    w_col = lambda n: pl.BlockSpec((None, d, hv), lambda i: (layer, 0, n))
    return pl.pallas_call(
        _mlstm_proj_kernel,
        grid=(t // tm,),
        in_specs=[pl.BlockSpec((tm, d), lambda i: (i, 0)), full(g), mod_spec, mod_spec,
                  w_col(0), w_col(1), w_col(2), full(w_g_pad), full(b_pad), full(w_g_t),
                  full(b_t), full(q_scale)],
        out_specs=[pl.BlockSpec((tm, 2 * hk + hv), lambda i: (i, 0)),
                   pl.BlockSpec((tm, hv), lambda i: (i, 0)),
                   pl.BlockSpec((tm, LANES), lambda i: (i, 0)),
                   pl.BlockSpec((ng, tm), lambda i: (0, i))],
        out_shape=[jax.ShapeDtypeStruct((t, 2 * hk + hv), BF16),
                   jax.ShapeDtypeStruct((t, hv), F32),
                   jax.ShapeDtypeStruct((t, LANES), F32),
                   jax.ShapeDtypeStruct((ng, t), F32)],
        compiler_params=_params("arbitrary"),
        name="mlstm_proj",
    )(x, g, shift, scale, w_in_all, w_in_all, w_in_all, w_g_pad, b_pad, w_g_t, b_t, q_scale)


def _log_sigmoid(x):
    return jnp.minimum(x, 0.0) - jnp.log(1.0 + jnp.exp(-jnp.abs(x)))


def _gate_cumsums(gc, gr, causal_bf, feeds_bf):
    bc = br = None
    for part in _split_bf16(_log_sigmoid(gc), 3):
        t = jnp.dot(causal_bf, part, preferred_element_type=F32)
        bc = t if bc is None else bc + t
    for part in _split_bf16(_log_sigmoid(gr), 3):
        t = jnp.dot(part, feeds_bf, preferred_element_type=F32)
        br = t if br is None else br + t
    return bc, br


def _mlstm_chunk(q, k, v, i_col, b_col, i_row, b_row, c_st, n_st, m_st, causal, rev):
    l = q.shape[0]
    zero_state = c_st is None
    if zero_state:
        m_st = 0.0
    g_row = i_row - b_row
    log_w = jnp.where(causal, g_row, -jnp.inf)
    c_t = jnp.maximum(m_st, jnp.max(log_w, axis=1, keepdims=True))
    dw = jnp.exp(log_w - c_t)
    a = _bdot_nt(q, k) * dw
    a_hi, a_lo = _split_bf16(a, 2)
    ones = jnp.ones((l, LANES), BF16)
    den = (jnp.dot(a_hi, ones, preferred_element_type=F32)
           + jnp.dot(a_lo, ones, preferred_element_type=F32))[:, 0:1]
    if not zero_state:
        sw = jnp.exp(m_st - c_t)
        n_hi, n_lo = _split_bf16(jnp.broadcast_to(n_st, (LANES, n_st.shape[1])), 2)
        den = den + sw * (_bdot_nt(q, n_hi) + _bdot_nt(q, n_lo))[:, 0:1]
    inv = 1.0 / jnp.maximum(jnp.abs(den), jnp.exp(-(b_col + c_t)))
    h = inv * jnp.dot(a_hi, v, preferred_element_type=F32)
    if not zero_state:
        h = h + (sw * inv) * _bdot(q, c_st)
    b_last = b_col[0:1, :] if rev else b_col[l - 1:l, :]
    log_k = b_last - b_col + i_col
    m_new = jnp.maximum(b_last + m_st, jnp.max(log_k, axis=0, keepdims=True))
    kw = jnp.exp(log_k - m_new)
    kwk = kw * k.astype(F32)
    c_new = _bdot_tn(kwk, v)
    n_new = jnp.sum(kwk, axis=0, keepdims=True)
    if not zero_state:
        decay = jnp.exp(b_last + m_st - m_new)
        c_new = decay * c_st + c_new
        n_new = decay * n_st + n_new
    return h, c_new, n_new, m_new


def _mlstm_scan_kernel(*refs, n_chunks, zero_init):
    single = zero_init and n_chunks == 1
    if zero_init:
        q_ref, k_ref, v_ref, gc_ref, gr_ref, hs_ref, c_ref, n_ref, m_ref, hb_scr = refs
        if not single:
            c_ref[...] = jnp.zeros_like(c_ref)
            n_ref[...] = jnp.zeros_like(n_ref)
            m_ref[...] = jnp.zeros_like(m_ref)
    else:
        (q_ref, k_ref, v_ref, gc_ref, gr_ref, c0_ref, n0_ref, m0_ref,
         hs_ref, c_ref, n_ref, m_ref, hb_scr) = refs
        c_ref[...] = c0_ref[...]
        n_ref[...] = n0_ref[...]
        m_ref[...] = m0_ref[...]
    l = MLSTM_CHUNK
    hh = MLSTM_HEADS
    dk = q_ref.shape[1] // hh
    dv = v_ref.shape[1] // hh

    tt = lax.broadcasted_iota(jnp.int32, (l, l), 0)
    ss = lax.broadcasted_iota(jnp.int32, (l, l), 1)
    masks = (ss <= tt, ss >= tt)
    masks_bf = tuple(jnp.where(m, 1.0, 0.0).astype(BF16) for m in masks)

    def body(c, carry):
        cr = n_chunks - 1 - c
        start = (lambda j: j * l) if single else (lambda j: pl.multiple_of(j * l, l))
        rows = (pl.ds(start(c), l), pl.ds(start(cr), l))
        grs = (gr_ref[c], gr_ref[cr])
        outs, states = ([], []), []
        for d in range(2):
            q, k, v, gc, gr = q_ref[rows[d], :], k_ref[rows[d], :], v_ref[rows[d], :], gc_ref[rows[d], :], grs[d]
            bc, br = _gate_cumsums(gc, gr, masks_bf[d], masks_bf[1 - d])
            for h in range(hh):
                ci, cf = 2 * d * hh + h, (2 * d + 1) * hh + h
                state = (None,) * 3 if single else (c_ref[d, h], n_ref[d, h], m_ref[d, h])
                o, *st = _mlstm_chunk(q[:, h * dk:(h + 1) * dk], k[:, h * dk:(h + 1) * dk],
                                      v[:, h * dv:(h + 1) * dv], gc[:, ci:ci + 1], bc[:, cf:cf + 1],
                                      gr[ci:ci + 1, :], br[cf:cf + 1, :], *state, masks[d], d == 1)
                outs[d].append(o)
                states.append((d, h, st))
        hs_ref[rows[0], :] = jnp.concatenate(outs[0], axis=1)
        hb_scr[rows[1], :] = jnp.concatenate(outs[1], axis=1)
        for d, h, (c_new, n_new, m_new) in states:
            c_ref[d, h], n_ref[d, h], m_ref[d, h] = c_new, n_new, m_new
        return carry

    if single:
        body(0, 0)
    else:
        lax.fori_loop(0, n_chunks, body, 0)
    for h in range(hh):
        cols = pl.ds(h * dv, dv)
        hs = hs_ref[:, cols] + hb_scr[:, cols]
        hs_ref[:, cols] = hs * lax.rsqrt(jnp.mean(hs * hs, axis=-1, keepdims=True) + NORM_EPS)


def mlstm_scan(qkv, gcol, grow, init, row_off, n_seq, seq_len):
    hh = MLSTM_HEADS
    hv = qkv.shape[1] // 2
    dv = hv // hh
    dk = dv // 2
    l = MLSTM_CHUNK
    nc = seq_len // l
    ob = row_off // seq_len
    kern = functools.partial(_mlstm_scan_kernel, n_chunks=nc, zero_init=init is None)
    st = lambda *tail: pl.BlockSpec((None, 2, hh) + tail, lambda s: (s, 0, 0) + (0,) * len(tail))
    states = [st(dk, dv), st(1, dk), st(1, 1)]
    return pl.pallas_call(
        kern,
        grid=(n_seq,),
        in_specs=[pl.BlockSpec((seq_len, hh * dk), lambda s: (ob + s, 0)),
                  pl.BlockSpec((seq_len, hh * dk), lambda s: (ob + s, 1)),
                  pl.BlockSpec((seq_len, hv), lambda s: (ob + s, 1)),
                  pl.BlockSpec((seq_len, LANES), lambda s: (ob + s, 0)),
                  pl.BlockSpec((nc, 4 * hh, l), lambda s: (ob + s, 0, 0))]
                 + ([] if init is None else states),
        out_specs=[pl.BlockSpec((seq_len, hv), lambda s: (s, 0))] + states,
        out_shape=[jax.ShapeDtypeStruct((n_seq * seq_len, hv), F32),
                   jax.ShapeDtypeStruct((n_seq, 2, hh, dk, dv), F32),
                   jax.ShapeDtypeStruct((n_seq, 2, hh, 1, dk), F32),
                   jax.ShapeDtypeStruct((n_seq, 2, hh, 1, 1), F32)],
        scratch_shapes=[pltpu.VMEM((seq_len, hv), F32)],
        compiler_params=_params("arbitrary"),
        name="mlstm_scan",
    )(qkv, qkv, qkv, gcol, grow, *(() if init is None else init))


def _mla_proj_kernel(x_ref, g_ref, sh_ref, sc_ref, win_ref, qg_ref, kvg_ref, wqb_ref, cos_ref,
                     sin_ref, q_ref, ckv_ref, kpe_ref, *, q_lora, kv_lora, rope, n_heads):
    h = _modulated(x_ref[...], g_ref[...], sh_ref[...], sc_ref[...])
    proj = _bdot(h, win_ref[...])
    q_lat = proj[:, :q_lora]
    ckv_ref[...] = _rms(proj[:, q_lora:q_lora + kv_lora], kvg_ref[...])
    cos, sin = cos_ref[...], sin_ref[...]
    kpe = proj[:, q_lora + kv_lora:q_lora + kv_lora + rope]
    kpe_rot = proj[:, q_lora + kv_lora + rope:q_lora + kv_lora + 2 * rope]
    kpe_ref[...] = kpe * cos[:, :rope] + kpe_rot * sin[:, :rope]
    q = _bdot(_rms(q_lat, qg_ref[...]), wqb_ref[...])
    n = n_heads * LANES
    pieces = []
    for hd in range(n_heads):
        c = slice(hd * LANES, (hd + 1) * LANES)
        pieces.append(q[:, c])
        pieces.append(q[:, n:2 * n][:, c] * cos + q[:, 2 * n:][:, c] * sin)
    q_ref[...] = jnp.concatenate(pieces, axis=1).astype(BF16)


def _rot_cols(w):
    n = w.shape[1]
    j = np.arange(n)
    quarter = MLA_ROPE // 4
    first = (j % (2 * quarter)) < quarter
    src = np.where(first, j + quarter, j - quarter)
    sign = np.where(first, -1.0, 1.0).astype(np.float32)
    return w[:, src] * sign


def _rope_tables(n_ctx, dec_batch, dec_seq):
    quarter = MLA_ROPE // 4
    freq = np.power(np.float32(ROPE_BASE), -np.arange(quarter, dtype=np.float32) / np.float32(quarter))
    pos = np.arange(dec_seq)
    ang_r = (pos // GRID_W).astype(np.float32)[:, None] * freq[None, :]
    ang_c = (pos % GRID_W).astype(np.float32)[:, None] * freq[None, :]
    ang = np.concatenate([ang_r, ang_r, ang_c, ang_c], axis=1).astype(np.float32)
    cos = np.concatenate([np.ones((n_ctx, MLA_ROPE), np.float32)] + [np.cos(ang)] * dec_batch, axis=0)
    sin = np.concatenate([np.zeros((n_ctx, MLA_ROPE), np.float32)] + [np.sin(ang)] * dec_batch, axis=0)
    reps = LANES // MLA_ROPE
    return (jnp.asarray(np.tile(cos, (1, reps)), F32), jnp.asarray(np.tile(sin, (1, reps)), F32))


def mla_proj(x, g, shift, scale, w_in, q_g, kv_g, w_qb, cos, sin, n_ctx, dec_seq, tm=512):
    t, d = x.shape
    hh, nope, rope = MLA_HEADS, MLA_NOPE, MLA_ROPE
    q_lora = q_g.shape[1]
    kv_lora = kv_g.shape[1]
    w_in_ext = jnp.concatenate([w_in, _rot_cols(w_in[:, q_lora + kv_lora:])], axis=1)
    w3 = w_qb.reshape(q_lora, hh, nope + rope)
    w_qn = w3[:, :, :nope].reshape(q_lora, hh * nope)
    w_qp = w3[:, :, nope:].reshape(q_lora, hh * rope)
    assert nope == LANES and rope <= LANES
    lane_pad = lambda w: jnp.pad(w.reshape(q_lora, hh, rope),
                                 ((0, 0), (0, 0), (0, LANES - rope))).reshape(q_lora, hh * LANES)
    w_qb_ext = jnp.concatenate([w_qn, lane_pad(w_qp), lane_pad(_rot_cols(w_qp))], axis=1)
    cidx = _cvec_index(n_ctx, dec_seq, tm)
    mod_spec = pl.BlockSpec((None, 1, d), lambda i: (cidx(i), 0, 0))
    full = lambda a: pl.BlockSpec(a.shape, lambda i: (0,) * a.ndim)
    kern = functools.partial(_mla_proj_kernel, q_lora=q_lora, kv_lora=kv_lora, rope=rope, n_heads=hh)
    return pl.pallas_call(
        kern,
        grid=(t // tm,),
        in_specs=[pl.BlockSpec((tm, d), lambda i: (i, 0)), full(g), mod_spec, mod_spec,
                  full(w_in_ext), full(q_g), full(kv_g), full(w_qb_ext),
                  pl.BlockSpec((tm, LANES), lambda i: (i, 0)),
                  pl.BlockSpec((tm, LANES), lambda i: (i, 0))],
        out_specs=[pl.BlockSpec((tm, 2 * hh * LANES), lambda i: (i, 0)),
                   pl.BlockSpec((tm, kv_lora), lambda i: (i, 0)),
                   pl.BlockSpec((tm, rope), lambda i: (i, 0))],
        out_shape=[jax.ShapeDtypeStruct((t, 2 * hh * LANES), BF16),
                   jax.ShapeDtypeStruct((t, kv_lora), F32),
                   jax.ShapeDtypeStruct((t, rope), F32)],
        compiler_params=_params("arbitrary"),
        name="mla_proj",
    )(x, g, shift, scale, w_in_ext, q_g, kv_g, w_qb_ext, cos, sin)


def _mla_kv_kernel(ckv_ref, kp_ref, w_ref, k_ref, v_ref):
    kv = _bdot(ckv_ref[...], w_ref[...])
    n = v_ref.shape[1]
    kp = kp_ref[...]
    pieces = []
    for hd in range(n // LANES):
        pieces += [kv[:, hd * LANES:(hd + 1) * LANES].astype(BF16), kp]
    k_ref[...] = jnp.concatenate(pieces, axis=1)
    v_ref[...] = kv[:, n:].astype(BF16)


def mla_kv(ckv_all, kp_pad, w_kvb, tm=512):
    r, kv_lora = ckv_all.shape
    hh, nope, vd = MLA_HEADS, MLA_NOPE, MLA_V
    w3 = w_kvb.reshape(kv_lora, hh, nope + vd)
    w_perm = jnp.concatenate([w3[:, :, :nope].reshape(kv_lora, hh * nope),
                              w3[:, :, nope:].reshape(kv_lora, hh * vd)], axis=1)
    return pl.pallas_call(
        _mla_kv_kernel,
        grid=(r // tm,),
        in_specs=[pl.BlockSpec((tm, kv_lora), lambda i: (i, 0)),
                  pl.BlockSpec((tm, LANES), lambda i: (i, 0)),
                  pl.BlockSpec(w_perm.shape, lambda i: (0, 0))],
        out_specs=[pl.BlockSpec((tm, 2 * hh * LANES), lambda i: (i, 0)),
                   pl.BlockSpec((tm, hh * vd), lambda i: (i, 0))],
        out_shape=[jax.ShapeDtypeStruct((r, 2 * hh * LANES), BF16),
                   jax.ShapeDtypeStruct((r, hh * vd), BF16)],
        compiler_params=_params("arbitrary"),
        name="mla_kv",
    )(ckv_all, kp_pad, w_perm)


def _attn_kernel(q_ref, k_ref, v_ref, o_ref, *, scale):
    c = scale * np.log2(np.e)
    for h in range(MLA_HEADS):
        hk = slice(2 * h * LANES, 2 * (h + 1) * LANES)
        s = _bdot_nt(q_ref[:, hk], k_ref[:, hk])
        e = jnp.exp2((s - jnp.max(s, axis=-1, keepdims=True)) * c)
        o = _bdot(e, v_ref[:, h * MLA_V:(h + 1) * MLA_V]) / jnp.sum(e, axis=-1, keepdims=True)
        o_ref[:, h * MLA_V:(h + 1) * MLA_V] = o.astype(BF16)


def mla_attention(q, k, v, q_row_off, k_row_off, n_seq, q_len, k_len):
    tq = min(ATTN_Q_BLOCK, q_len)
    qb = q_len // tq
    q0 = q_row_off // tq
    k0 = k_row_off // k_len
    dq, dv = q.shape[1], v.shape[1]
    kern = functools.partial(_attn_kernel, scale=(MLA_NOPE + MLA_ROPE) ** -0.5)
    return pl.pallas_call(
        kern,
        grid=(n_seq, qb),
        in_specs=[pl.BlockSpec((tq, dq), lambda s, j: (q0 + s * qb + j, 0)),
                  pl.BlockSpec((k_len, dq), lambda s, j: (k0 + s, 0)),
                  pl.BlockSpec((k_len, dv), lambda s, j: (k0 + s, 0))],
        out_specs=pl.BlockSpec((tq, dv), lambda s, j: (s * qb + j, 0)),
        out_shape=jax.ShapeDtypeStruct((n_seq * q_len, dv), BF16),
        compiler_params=_params("arbitrary", "arbitrary"),
        name="mla_attention",
    )(q, k, v)


def kernel(x_prompt, x_sample, state_mlstm_C, state_mlstm_n, state_mlstm_m, cache_mla_ckv,
           cache_mla_kpe, c, c_ctx, w_ada, b_ada, norm_mix, norm_ffn, norm_final, w_pool,
           pool_scale, w_mlstm_in, b_mlstm_gate, mlstm_head_g, w_mlstm_out, w_mla_in, mla_q_g,
           mla_kv_g, w_mla_qb, w_mla_kvb, w_mla_out, w_router, b_router, w_exp_gate, w_exp_up,
           w_exp_down):
    batch, seq, d = x_prompt.shape
    dec_batch, dec_seq, _ = x_sample.shape
    depth = w_ada.shape[0]
    n_ctx = batch * seq
    n_lat = dec_batch * dec_seq
    hh = MLSTM_HEADS
    past = cache_mla_ckv.shape[2]

    x = (x_prompt.reshape(n_ctx, d), x_sample.reshape(n_lat, d))

    n_cv = 1 + dec_batch
    cvecs = jnp.concatenate([c_ctx[None, :], c, jnp.zeros((SUBLANES - n_cv % SUBLANES, d), F32)], axis=0)
    mod = ada_mod_all(cvecs, w_ada, b_ada).reshape(depth, cvecs.shape[0], 6, 1, d)

    w_router_pad = jnp.pad(w_router, ((0, 0), (0, LANES - N_EXPERTS)))
    w_router_hi = w_router_pad.astype(BF16)
    w_router_lo = (w_router_pad - w_router_hi.astype(F32)).astype(BF16)
    w_router_split = jnp.concatenate([w_router_hi, w_router_lo], axis=1)
    b_router_col = b_router.reshape(N_EXPERTS, 1)
    g_final = norm_final.reshape(1, d)
    row = lambda a: a.reshape(1, -1)

    outs = {}
    for i in range(depth):
        kind, j = i % 3, i // 3
        m = [mod[i, :n_cv, k] for k in range(6)]
        g_mix = row(norm_mix[i])
        if kind == 0:
            x = pool_layer(x, g_mix, m[0], m[1], m[2], w_pool[j], row(pool_scale[j]),
                           n_ctx, seq, dec_seq)
        elif kind == 1:
            qkv, o_gate, gcol, grow = mlstm_proj(x, g_mix, m[0], m[1], w_mlstm_in, j,
                                                 b_mlstm_gate[j], n_ctx, dec_seq)
            t = n_ctx + n_lat
            l = MLSTM_CHUNK
            grow_c = grow.reshape(4 * hh, t // l, l).transpose(1, 0, 2)
            hs_c, c_new, n_new, m_new = mlstm_scan(qkv, gcol, grow_c, None, 0, batch, seq)
            init = (state_mlstm_C[:, j], state_mlstm_n[:, j][:, :, :, None, :],
                    state_mlstm_m[:, j][:, :, :, None, None])
            hs_l, _, _, _ = mlstm_scan(qkv, gcol, grow_c, init, n_ctx, dec_batch, dec_seq)
            outs["C"] = c_new[:, None]
            outs["n"] = n_new[:, None, :, :, 0, :]
            outs["m"] = m_new[:, None, :, :, 0, 0]
            x = mlstm_out(x, hs_c, hs_l, o_gate, row(mlstm_head_g[j]), w_mlstm_out[j], m[2],
                          n_ctx, dec_seq)
        else:
            cos, sin = _rope_tables(n_ctx, dec_batch, dec_seq)
            q_cat, ckv, kpe = mla_proj(x, g_mix, m[0], m[1], w_mla_in[j], row(mla_q_g[j]),
                                       row(mla_kv_g[j]), w_mla_qb[j], cos, sin, n_ctx, dec_seq)
            lat_parts_c, lat_parts_p = [], []
            for b in range(dec_batch):
                lo = n_ctx + b * dec_seq
                lat_parts_c += [cache_mla_ckv[b, j], ckv[lo:lo + dec_seq]]
                lat_parts_p += [cache_mla_kpe[b, j], kpe[lo:lo + dec_seq]]
            ckv_all = jnp.concatenate(lat_parts_c + [ckv[:n_ctx]], axis=0)
            kp_all = jnp.concatenate(lat_parts_p + [kpe[:n_ctx]], axis=0).astype(BF16)
            kp_pad = jnp.pad(kp_all, ((0, 0), (0, LANES - kp_all.shape[1])))
            k_cat, v = mla_kv(ckv_all, kp_pad, w_mla_kvb[j])
            k_lat = past + dec_seq
            o_c = mla_attention(q_cat, k_cat, v, 0, dec_batch * k_lat, batch, seq, seq)
            o_l = mla_attention(q_cat, k_cat, v, n_ctx, 0, dec_batch, dec_seq, k_lat)
            outs["ckv"] = ckv[:n_ctx].reshape(batch, 1, seq, -1)
            outs["kpe"] = kpe[:n_ctx].reshape(batch, 1, seq, -1)
            x = resid_proj(x, o_c, o_l, w_mla_out[j], m[2], n_ctx, dec_seq)
        x = moe_layer(x, row(norm_ffn[i]), m[3], m[4], m[5], w_router_split, b_router_col,
                      w_exp_gate, w_exp_up, w_exp_down, i, g_final, n_ctx, dec_seq,
                      final_norm=(i == depth - 1), split_output=(i == depth - 1))

    y_prompt = x[0].reshape(batch, seq, d)
    y_sample = x[1].reshape(dec_batch, dec_seq, d)
    return (y_prompt, y_sample, outs["C"], outs["n"], outs["m"], outs["ckv"], outs["kpe"])
```

```python
import functools

import numpy as np
import jax
import jax.numpy as jnp
from jax import lax
from jax.experimental import pallas as pl
from jax.experimental.pallas import tpu as pltpu

F32 = jnp.float32
BF16 = jnp.bfloat16

NORM_EPS = 1e-6
GRID_W = 64
POOL_WINDOWS = (2, 4, 8, 16)
MLSTM_HEADS = 4
MLSTM_CHUNK = 256
MLA_HEADS = 8
MLA_NOPE = 128
MLA_ROPE = 64
MLA_V = 128
ROPE_BASE = 10000.0
N_EXPERTS = 16
N_EXPERT_GROUPS = 4
EXPERTS_PER_GROUP = N_EXPERTS // N_EXPERT_GROUPS

LANES = 128
SUBLANES = 8
VMEM_LIMIT = 56 * 1024 * 1024
POOL_TILE = 256
POOL_HALO = 8
ATTN_Q_BLOCK = 256
MOE_TILE = 512
MOE_SEG_ALIGN = 16
MOE_ROW_BLOCK = 160
MOE_WIDE_BLOCK = 224


def _params(*sem):
    return pltpu.CompilerParams(dimension_semantics=sem, vmem_limit_bytes=VMEM_LIMIT)


def _rms(x, g):
    return x * lax.rsqrt(jnp.mean(x * x, axis=-1, keepdims=True) + NORM_EPS) * g


def _modulated(x, g, shift, scale):
    return _rms(x, g) * (1.0 + scale) + shift


def _silu(x):
    return x * jax.nn.sigmoid(x)


def _bdot(a, b):
    return jnp.dot(a.astype(BF16), b.astype(BF16), preferred_element_type=F32)


def _bdot_nt(a, b):
    return lax.dot_general(a.astype(BF16), b.astype(BF16), (((1,), (1,)), ((), ())),
                           preferred_element_type=F32)


def _bdot_tn(a, b):
    return lax.dot_general(a.astype(BF16), b.astype(BF16), (((0,), (0,)), ((), ())),
                           preferred_element_type=F32)


def _cvec_index(n_ctx, dec_seq, tm):
    def idx(i):
        r = i * tm
        return jnp.where(r < n_ctx, 0, (r - n_ctx) // dec_seq + 1)
    return idx


def _ada_kernel(c_ref, w_ref, b_ref, o_ref):
    o_ref[...] = _bdot(_silu(c_ref[...]), w_ref[...]) + b_ref[...]


def ada_mod_all(cvecs, w_ada, b_ada, tn=1536):
    depth, d, n6 = w_ada.shape
    rows = cvecs.shape[0]
    return pl.pallas_call(
        _ada_kernel,
        grid=(depth, n6 // tn),
        in_specs=[pl.BlockSpec((rows, d), lambda l, n: (0, 0)),
                  pl.BlockSpec((None, d, tn), lambda l, n: (l, 0, n)),
                  pl.BlockSpec((None, 1, tn), lambda l, n: (l, 0, n))],
        out_specs=pl.BlockSpec((None, rows, tn), lambda l, n: (l, 0, n)),
        out_shape=jax.ShapeDtypeStruct((depth, rows, n6), F32),
        compiler_params=_params("arbitrary", "arbitrary"),
        name="ada_mod",
    )(cvecs, w_ada, b_ada.reshape(depth, 1, n6))


def _pool_kernel(*refs, n_ctx_tiles, ctx_seq_tiles, lat_seq_tiles, split_input):
    n_x = 6 if split_input else 3
    x_refs = refs[:n_x]
    g_ref, sh_ref, sc_ref, gt_ref, wp_ref, ps_ref, o_ref, buf_ref, *lvl_refs = refs[n_x:]
    i = pl.program_id(0)
    is_ctx = i < n_ctx_tiles
    j = jnp.where(is_ctx, i % ctx_seq_tiles, (i - n_ctx_tiles) % lat_seq_tiles)
    nt = jnp.where(is_ctx, ctx_seq_tiles, lat_seq_tiles)
    g, sh, sc = g_ref[...], sh_ref[...], sc_ref[...]
    tp, hl = POOL_TILE, POOL_HALO
    gw = o_ref.shape[1] // len(POOL_WINDOWS)

    def fill(xc_ref, xp_ref, xn_ref):
        buf_ref[pl.ds(0, hl), :] = jnp.where(j == 0, 0.0, _modulated(xp_ref[...], g, sh, sc))
        buf_ref[pl.ds(hl, tp), :] = _modulated(xc_ref[...], g, sh, sc)
        buf_ref[pl.ds(hl + tp, hl), :] = jnp.where(j == nt - 1, 0.0,
                                                    _modulated(xn_ref[...], g, sh, sc))
        o_ref[...] = xc_ref[...]

    if split_input:
        pl.when(is_ctx)(lambda: fill(*x_refs[:3]))
        pl.when(jnp.logical_not(is_ctx))(lambda: fill(*x_refs[3:]))
    else:
        fill(*x_refs)

    rows = tp + 2 * hl
    buf_ref[pl.ds(rows, hl), :] = jnp.zeros((hl, buf_ref.shape[1]), F32)
    for ref in lvl_refs:
        ref[pl.ds(rows, hl), :] = jnp.zeros((hl, gw), F32)
    pos = j * tp + lax.broadcasted_iota(jnp.int32, (tp, 1), 0)
    seq_len = nt * tp
    for gi, w in enumerate(POOL_WINDOWS):
        cols = pl.ds(gi * gw, gw)
        src, src_cols = buf_ref, cols
        for lvl in range(1, w.bit_length() - 1):
            step = 1 << (lvl - 1)
            dst = lvl_refs[lvl - 1]
            dst[pl.ds(0, rows), :] = src[pl.ds(0, rows), src_cols] + src[pl.ds(step, rows), src_cols]
            src, src_cols = dst, pl.ds(0, gw)
        acc = src[pl.ds(hl - w // 2, tp), src_cols] + src[pl.ds(hl, tp), src_cols]
        cnt = jnp.minimum(pos + w // 2, seq_len) - jnp.maximum(pos - w // 2, 0)
        pooled = acc / cnt.astype(F32) - buf_ref[pl.ds(hl, tp), cols]
        y = _bdot(pooled, wp_ref[gi]) * ps_ref[:, cols]
        o_ref[:, cols] = o_ref[:, cols] + gt_ref[:, cols] * y


def _halo_specs(tp, hl, d, tile_off, n_rows):
    hb = tp // hl
    last_tile, last_hblk = n_rows // tp - 1, n_rows // hl - 1
    tile = lambda i: jnp.clip(i - tile_off, 0, last_tile)
    return [pl.BlockSpec((tp, d), lambda i: (tile(i), 0)),
            pl.BlockSpec((hl, d), lambda i: (jnp.clip(tile(i) * hb - 1, 0, last_hblk), 0)),
            pl.BlockSpec((hl, d), lambda i: (jnp.clip((tile(i) + 1) * hb, 0, last_hblk), 0))]


def pool_layer(xs, g, shift, scale, gate, w_pool, pool_scale, n_ctx, seq, dec_seq):
    split = isinstance(xs, tuple)
    tp, hl = POOL_TILE, POOL_HALO
    if split:
        d = xs[0].shape[1]
        t = xs[0].shape[0] + xs[1].shape[0]
        x_specs = (_halo_specs(tp, hl, d, 0, xs[0].shape[0])
                   + _halo_specs(tp, hl, d, n_ctx // tp, xs[1].shape[0]))
        x_args = (xs[0],) * 3 + (xs[1],) * 3
    else:
        t, d = xs.shape
        x_specs = _halo_specs(tp, hl, d, 0, t)
        x_args = (xs,) * 3
    cidx = _cvec_index(n_ctx, dec_seq, tp)
    mod_spec = pl.BlockSpec((None, 1, d), lambda i: (cidx(i), 0, 0))
    row_spec = pl.BlockSpec((1, d), lambda i: (0, 0))
    assert all(w & (w - 1) == 0 and w // 2 <= hl for w in POOL_WINDOWS)
    n_levels = max(POOL_WINDOWS).bit_length() - 2
    kern = functools.partial(_pool_kernel, n_ctx_tiles=n_ctx // tp, ctx_seq_tiles=seq // tp,
                             lat_seq_tiles=dec_seq // tp, split_input=split)
    return pl.pallas_call(
        kern,
        grid=(t // tp,),
        in_specs=x_specs + [row_spec, mod_spec, mod_spec, mod_spec,
                            pl.BlockSpec(w_pool.shape, lambda i: (0, 0, 0)), row_spec],
        out_specs=pl.BlockSpec((tp, d), lambda i: (i, 0)),
        out_shape=jax.ShapeDtypeStruct((t, d), F32),
        scratch_shapes=[pltpu.VMEM((tp + 3 * hl, d), F32)]
        + [pltpu.VMEM((tp + 3 * hl, d // len(POOL_WINDOWS)), F32)] * n_levels,
        compiler_params=_params("arbitrary"),
        name="pool_mixer",
    )(*x_args, g, shift, scale, gate, w_pool, pool_scale)


def _route(sel, scores):
    e, tm = sel.shape
    row = lax.broadcasted_iota(jnp.int32, (e, tm), 0)
    best = jnp.zeros((1, tm), jnp.int32)
    best_sc = None
    for gidx in range(N_EXPERT_GROUPS):
        r = [sel[gidx * EXPERTS_PER_GROUP + k:gidx * EXPERTS_PER_GROUP + k + 1, :]
             for k in range(EXPERTS_PER_GROUP)]
        top2 = None
        for a in range(EXPERTS_PER_GROUP):
            for b in range(a + 1, EXPERTS_PER_GROUP):
                s = r[a] + r[b]
                top2 = s if top2 is None else jnp.maximum(top2, s)
        if best_sc is None:
            best_sc = top2
        else:
            better = top2 > best_sc
            best = jnp.where(better, gidx, best)
            best_sc = jnp.where(better, top2, best_sc)
    neg = -jnp.inf
    masked = jnp.where(row // EXPERTS_PER_GROUP == best, sel, neg)
    m1 = jnp.max(masked, axis=0, keepdims=True)
    i1 = jnp.min(jnp.where(masked == m1, row, e), axis=0, keepdims=True)
    masked2 = jnp.where(row == i1, neg, masked)
    m2 = jnp.max(masked2, axis=0, keepdims=True)
    i2 = jnp.min(jnp.where(masked2 == m2, row, e), axis=0, keepdims=True)
    hot1 = row == i1
    hot2 = row == i2
    w1 = jnp.sum(jnp.where(hot1, scores, 0.0), axis=0, keepdims=True)
    w2 = jnp.sum(jnp.where(hot2, scores, 0.0), axis=0, keepdims=True)
    tot = w1 + w2
    return best, jnp.where(hot1, w1 / tot, 0.0) + jnp.where(hot2, w2 / tot, 0.0)


def _split_bf16(a, parts):
    out = []
    for _ in range(parts):
        p = a.astype(BF16)
        out.append(p)
        a = a - p.astype(F32)
    return out


def _pad_rows(a, rows):
    return jnp.concatenate([a, jnp.zeros((rows - a.shape[0], a.shape[1]), a.dtype)], axis=0)


def _moe_kernel(x_ref, g_ref, sh_ref, sc_ref, gt_ref, wr_ref, br_ref, wgf_ref, wuf_ref, wdf_ref,
                gf_ref, *rest, final_norm, n_ctx_tiles, n_experts):
    out_refs, (wgu_ref, wd_ref), scratch = rest[:-5], rest[-5:-3], rest[-3:]
    i = pl.program_id(0)
    f = wgf_ref.shape[1]
    for e in range(n_experts):
        gi, k = divmod(e, EXPERTS_PER_GROUP)

        @pl.when(i == e)
        def _(gi=gi, k=k):
            wgu_ref[gi, :, pl.ds(2 * k * f, f)] = wgf_ref[...].astype(BF16)
            wgu_ref[gi, :, pl.ds((2 * k + 1) * f, f)] = wuf_ref[...].astype(BF16)
            wd_ref[gi, pl.ds(k * f, f), :] = wdf_ref[...].astype(BF16)

    @pl.when(i >= n_experts)
    def _():
        _moe_tile(x_ref, g_ref, sh_ref, sc_ref, gt_ref, wr_ref, br_ref, wgu_ref, wd_ref,
                  gf_ref, out_refs, scratch, i - n_experts, final_norm, n_ctx_tiles)


def _moe_tile(x_ref, g_ref, sh_ref, sc_ref, gt_ref, wr_ref, br_ref, wgu_ref, wd_ref,
              gf_ref, out_refs, scratch, tile, final_norm, n_ctx_tiles):
    hp_scr, cw_scr, yp_scr = scratch
    tr, d = x_ref.shape
    trp = hp_scr.shape[0]
    ng, eg = N_EXPERT_GROUPS, EXPERTS_PER_GROUP
    x = x_ref[...]
    h = _modulated(x, g_ref[...], sh_ref[...], sc_ref[...])
    hb = h.astype(BF16)

    h_lo = (h - hb.astype(F32)).astype(BF16)
    wr = wr_ref[...]
    lg = jnp.dot(hb, wr, preferred_element_type=F32)
    logits = lg[:, :LANES] + lg[:, LANES:] + jnp.dot(h_lo, wr[:, :LANES], preferred_element_type=F32)
    scores = jax.nn.sigmoid(logits.T[:N_EXPERTS, :])
    best, comb_t = _route(scores + br_ref[...], scores)

    grp = lax.broadcasted_iota(jnp.int32, (SUBLANES, tr), 0)
    hot_t = (grp == best).astype(F32)
    cw_t = hot_t[0:1, :] * comb_t[0:eg, :]
    for gi in range(1, ng):
        cw_t = cw_t + hot_t[gi:gi + 1, :] * comb_t[gi * eg:(gi + 1) * eg, :]
    cw_c = _pad_rows(cw_t, LANES).T

    ia = lax.broadcasted_iota(jnp.int32, (tr, tr), 0)
    ib = lax.broadcasted_iota(jnp.int32, (tr, tr), 1)
    before = jnp.where(ia < ib, 1.0, 0.0).astype(BF16)
    rank_t = jnp.dot(hot_t.astype(BF16), before, preferred_element_type=F32)

    starts, counts = [], []
    off = jnp.int32(0)
    for gi in range(ng):
        n = jnp.sum(hot_t[gi:gi + 1, :]).astype(jnp.int32)
        n = ((n + MOE_SEG_ALIGN - 1) // MOE_SEG_ALIGN) * MOE_SEG_ALIGN
        starts.append(off)
        counts.append(n)
        off = off + n

    pos_t = hot_t[0:1, :] * (rank_t[0:1, :] + starts[0].astype(F32))
    for gi in range(1, ng):
        pos_t = pos_t + hot_t[gi:gi + 1, :] * (rank_t[gi:gi + 1, :] + starts[gi].astype(F32))
    pos_c = _pad_rows(pos_t, LANES).T[:, 0:1]
    used = tr + ng * MOE_SEG_ALIGN
    perm = jnp.where(lax.broadcasted_iota(jnp.int32, (used, tr), 0) == pos_t.astype(jnp.int32),
                     1.0, 0.0).astype(BF16)
    perm_t = jnp.where(lax.broadcasted_iota(jnp.int32, (tr, used), 1) == pos_c.astype(jnp.int32),
                       1.0, 0.0).astype(BF16)

    hp_scr[pl.ds(0, used), :] = jnp.dot(perm, hb, preferred_element_type=F32).astype(BF16)
    hp_scr[pl.ds(used, trp - used), :] = jnp.zeros((trp - used, d), BF16)
    cw_pair = jnp.dot(perm, jnp.concatenate(_split_bf16(cw_c, 2), axis=1), preferred_element_type=F32)
    cw_scr[pl.ds(0, used), :] = cw_pair[:, :LANES] + cw_pair[:, LANES:]
    cw_scr[pl.ds(used, trp - used), :] = jnp.zeros((trp - used, LANES), F32)
    yp_scr[...] = jnp.zeros_like(yp_scr)

    f = wd_ref.shape[1] // eg

    def expert_block(gi, row0, n_rows):
        rows = pl.ds(pl.multiple_of(row0, MOE_SEG_ALIGN), n_rows)
        cwb = cw_scr[rows, :]
        gu = jnp.dot(hp_scr[rows, :], wgu_ref[gi], preferred_element_type=F32)
        hid = [(_silu(gu[:, 2 * k * f:(2 * k + 1) * f]) * gu[:, (2 * k + 1) * f:(2 * k + 2) * f]
                * cwb[:, k:k + 1]).astype(BF16) for k in range(eg)]
        yp_scr[rows, :] = jnp.dot(jnp.concatenate(hid, axis=1), wd_ref[gi],
                                  preferred_element_type=F32)

    sb, wide = MOE_ROW_BLOCK, MOE_WIDE_BLOCK
    for gi in range(ng):
        use_wide = jnp.logical_and(counts[gi] > sb, counts[gi] <= wide)
        pl.when(use_wide)(functools.partial(expert_block, gi, starts[gi], wide))
        n_blocks = jnp.where(use_wide, 0, (counts[gi] + sb - 1) // sb)

        def block(b, carry, gi=gi):
            expert_block(gi, starts[gi] + b * sb, sb)
            return carry
        lax.fori_loop(0, n_blocks, block, 0)

    y_hi, y_lo = _split_bf16(yp_scr[pl.ds(0, used), :], 2)
    moe = (jnp.dot(perm_t, y_hi, preferred_element_type=F32)
           + jnp.dot(perm_t, y_lo, preferred_element_type=F32))
    out = x + gt_ref[...] * moe
    if final_norm:
        out = _rms(out, gf_ref[...])
    if n_ctx_tiles is None:
        out_refs[0][...] = out
    else:
        is_ctx = tile < n_ctx_tiles

        @pl.when(is_ctx)
        def _():
            out_refs[0][...] = out

        @pl.when(jnp.logical_not(is_ctx))
        def _():
            out_refs[1][...] = out


def moe_layer(x, g, shift, scale, gate, w_router_split, b_router_col, wg, wu, wd, layer, g_final,
              n_ctx, dec_seq, final_norm, split_output=False, tm=MOE_TILE):
    t, d = x.shape
    _, n_e, _, f = wg.shape
    trp = tm + N_EXPERT_GROUPS * MOE_SEG_ALIGN + max(MOE_ROW_BLOCK, MOE_WIDE_BLOCK)
    tile = lambda i: jnp.maximum(i - n_e, 0)
    cidx = _cvec_index(n_ctx, dec_seq, tm)
    mod_spec = pl.BlockSpec((None, 1, d), lambda i: (cidx(tile(i)), 0, 0))
    row_spec = pl.BlockSpec((1, d), lambda i: (0, 0))
    expert = lambda i: jnp.minimum(i, n_e - 1)
    nct = n_ctx // tm
    if split_output:
        out_specs = [pl.BlockSpec((tm, d), lambda i: (jnp.minimum(tile(i), nct - 1), 0)),
                     pl.BlockSpec((tm, d), lambda i: (jnp.maximum(tile(i) - nct, 0), 0))]
        out_shape = [jax.ShapeDtypeStruct((n_ctx, d), F32), jax.ShapeDtypeStruct((t - n_ctx, d), F32)]
    else:
        out_specs = pl.BlockSpec((tm, d), lambda i: (tile(i), 0))
        out_shape = jax.ShapeDtypeStruct((t, d), F32)
    return pl.pallas_call(
        functools.partial(_moe_kernel, final_norm=final_norm,
                          n_ctx_tiles=nct if split_output else None, n_experts=n_e),
        grid=(n_e + t // tm,),
        in_specs=[pl.BlockSpec((tm, d), lambda i: (tile(i), 0)),
                  row_spec, mod_spec, mod_spec, mod_spec,
                  pl.BlockSpec(w_router_split.shape, lambda i: (0, 0)),
                  pl.BlockSpec(b_router_col.shape, lambda i: (0, 0)),
                  pl.BlockSpec((None, None, d, f), lambda i: (layer, expert(i), 0, 0)),
                  pl.BlockSpec((None, None, d, f), lambda i: (layer, expert(i), 0, 0)),
                  pl.BlockSpec((None, None, f, d), lambda i: (layer, expert(i), 0, 0)),
                  row_spec],
        out_specs=out_specs,
        out_shape=out_shape,
        scratch_shapes=[pltpu.VMEM((N_EXPERT_GROUPS, d, 2 * EXPERTS_PER_GROUP * f), BF16),
                        pltpu.VMEM((N_EXPERT_GROUPS, EXPERTS_PER_GROUP * f, d), BF16),
                        pltpu.VMEM((trp, d), BF16), pltpu.VMEM((trp, LANES), F32),
                        pltpu.VMEM((trp, d), F32)],
        compiler_params=_params("arbitrary"),
        name="moe",
    )(x, g, shift, scale, gate, w_router_split, b_router_col, wg, wu, wd, g_final)


def _per_stream(n_ctx_tiles, ctx_ref, lat_ref, fn):
    i = pl.program_id(0)

    @pl.when(i < n_ctx_tiles)
    def _():
        fn(ctx_ref[...])

    @pl.when(i >= n_ctx_tiles)
    def _():
        fn(lat_ref[...])


def _stream_specs(tm, k, n_ctx_tiles):
    return [pl.BlockSpec((tm, k), lambda i: (jnp.minimum(i, n_ctx_tiles - 1), 0)),
            pl.BlockSpec((tm, k), lambda i: (jnp.maximum(i - n_ctx_tiles, 0), 0))]


def _resid_proj_kernel(x_ref, ac_ref, al_ref, w_ref, gt_ref, o_ref, *, n_ctx_tiles):
    def run(a):
        o_ref[...] = x_ref[...] + gt_ref[...] * _bdot(a, w_ref[...])
    _per_stream(n_ctx_tiles, ac_ref, al_ref, run)


def _mlstm_out_kernel(x_ref, hc_ref, hl_ref, og_ref, hg_ref, w_ref, gt_ref, o_ref, *, n_ctx_tiles):
    def run(hs):
        a = jax.nn.sigmoid(og_ref[...].astype(F32)) * (hs.astype(F32) * hg_ref[...])
        o_ref[...] = x_ref[...] + gt_ref[...] * _bdot(a, w_ref[...])
    _per_stream(n_ctx_tiles, hc_ref, hl_ref, run)


def resid_proj(x, a_ctx, a_lat, w, gate, n_ctx, dec_seq, tm=512):
    t, d = x.shape
    k = a_ctx.shape[1]
    cidx = _cvec_index(n_ctx, dec_seq, tm)
    nct = n_ctx // tm
    return pl.pallas_call(
        functools.partial(_resid_proj_kernel, n_ctx_tiles=nct),
        grid=(t // tm,),
        in_specs=[pl.BlockSpec((tm, d), lambda i: (i, 0))] + _stream_specs(tm, k, nct) + [
            pl.BlockSpec((k, d), lambda i: (0, 0)),
            pl.BlockSpec((None, 1, d), lambda i: (cidx(i), 0, 0))],
        out_specs=pl.BlockSpec((tm, d), lambda i: (i, 0)),
        out_shape=jax.ShapeDtypeStruct((t, d), F32),
        compiler_params=_params("arbitrary"),
        name="resid_proj",
    )(x, a_ctx, a_lat, w, gate)


def mlstm_out(x, hs_ctx, hs_lat, o_gate, head_g, w, gate, n_ctx, dec_seq, tm=512):
    t, d = x.shape
    k = hs_ctx.shape[1]
    cidx = _cvec_index(n_ctx, dec_seq, tm)
    nct = n_ctx // tm
    return pl.pallas_call(
        functools.partial(_mlstm_out_kernel, n_ctx_tiles=nct),
        grid=(t // tm,),
        in_specs=[pl.BlockSpec((tm, d), lambda i: (i, 0))] + _stream_specs(tm, k, nct) + [
            pl.BlockSpec((tm, k), lambda i: (i, 0)),
            pl.BlockSpec((1, k), lambda i: (0, 0)),
            pl.BlockSpec((k, d), lambda i: (0, 0)),
            pl.BlockSpec((None, 1, d), lambda i: (cidx(i), 0, 0))],
        out_specs=pl.BlockSpec((tm, d), lambda i: (i, 0)),
        out_shape=jax.ShapeDtypeStruct((t, d), F32),
        compiler_params=_params("arbitrary"),
        name="mlstm_out",
    )(x, hs_ctx, hs_lat, o_gate, head_g, w, gate)


def _mlstm_proj_kernel(x_ref, g_ref, sh_ref, sc_ref, wqk_ref, wv_ref, wo_ref, wgt_ref, bg_ref,
                       wgtt_ref, bgt_ref, qs_ref, qkv_ref, o_ref, gc_ref, gr_ref):
    h = _modulated(x_ref[...], g_ref[...], sh_ref[...], sc_ref[...]).astype(BF16)
    nqk = wqk_ref.shape[1]
    qkv_ref[:, :nqk] = (jnp.dot(h, wqk_ref[...].astype(BF16), preferred_element_type=F32)
                        * qs_ref[...]).astype(BF16)
    qkv_ref[:, nqk:] = jnp.dot(h, wv_ref[...].astype(BF16),
                               preferred_element_type=F32).astype(BF16)
    o_ref[...] = jnp.dot(h, wo_ref[...].astype(BF16), preferred_element_type=F32).astype(BF16)
    gc_ref[...] = jnp.dot(h, wgt_ref[...].astype(BF16), preferred_element_type=F32) + bg_ref[...]
    gr_ref[...] = _bdot_nt(wgtt_ref[...], h) + bgt_ref[...]


def mlstm_proj(x, g, shift, scale, w_in_all, layer, b_gate, n_ctx, dec_seq, tm=512):
    t, d = x.shape
    hh = MLSTM_HEADS
    hv = d
    hk = hv // 2
    ng = 4 * hh
    w_g = w_in_all[layer, :, 2 * hk + 2 * hv:]
    w_g_pad = jnp.pad(w_g, ((0, 0), (0, LANES - ng)))
    b_pad = jnp.pad(b_gate.reshape(1, ng), ((0, 0), (0, LANES - ng)))
    w_g_t = w_g.T
    b_t = b_gate.reshape(ng, 1)
    dk = hk // hh
    q_scale = jnp.concatenate([jnp.full((1, hk), dk ** -0.5, F32), jnp.ones((1, hk), F32)], axis=1)
    cidx = _cvec_index(n_ctx, dec_seq, tm)
    mod_spec = pl.BlockSpec((None, 1, d), lambda i: (cidx(i), 0, 0))
    full = lambda a: pl.BlockSpec(a.shape, lambda i: (0,) * a.ndim)
    assert 2 * hk == hv
    w_col = lambda n: pl.BlockSpec((None, d, hv), lambda i: (layer, 0, n))
    return pl.pallas_call(
        _mlstm_proj_kernel,
        grid=(t // tm,),
        in_specs=[pl.BlockSpec((tm, d), lambda i: (i, 0)), full(g), mod_spec, mod_spec,
                  w_col(0), w_col(1), w_col(2), full(w_g_pad), full(b_pad), full(w_g_t),
                  full(b_t), full(q_scale)],
        out_specs=[pl.BlockSpec((tm, 2 * hk + hv), lambda i: (i, 0)),
                   pl.BlockSpec((tm, hv), lambda i: (i, 0)),
                   pl.BlockSpec((tm, LANES), lambda i: (i, 0)),
                   pl.BlockSpec((ng, tm), lambda i: (0, i))],
        out_shape=[jax.ShapeDtypeStruct((t, 2 * hk + hv), BF16),
                   jax.ShapeDtypeStruct((t, hv), BF16),
                   jax.ShapeDtypeStruct((t, LANES), F32),
                   jax.ShapeDtypeStruct((ng, t), F32)],
        compiler_params=_params("arbitrary"),
        name="mlstm_proj",
    )(x, g, shift, scale, w_in_all, w_in_all, w_in_all, w_g_pad, b_pad, w_g_t, b_t, q_scale)


def _log_sigmoid(x):
    return jnp.minimum(x, 0.0) - jnp.log(1.0 + jnp.exp(-jnp.abs(x)))


def _gate_cumsums(gc, gr, causal_bf, feeds_bf):
    bc = br = None
    for part in _split_bf16(_log_sigmoid(gc), 3):
        t = jnp.dot(causal_bf, part, preferred_element_type=F32)
        bc = t if bc is None else bc + t
    for part in _split_bf16(_log_sigmoid(gr), 3):
        t = jnp.dot(part, feeds_bf, preferred_element_type=F32)
        br = t if br is None else br + t
    return bc, br


def _mlstm_chunk(q, k, v, i_col, b_col, i_row, b_row, c_st, n_st, m_st, causal, rev):
    l = q.shape[0]
    zero_state = c_st is None
    if zero_state:
        m_st = 0.0
    g_row = i_row - b_row
    log_w = jnp.where(causal, g_row, -jnp.inf)
    c_t = jnp.maximum(m_st, jnp.max(log_w, axis=1, keepdims=True))
    dw = jnp.exp(log_w - c_t)
    a = _bdot_nt(q, k) * dw
    a_hi, a_lo = _split_bf16(a, 2)
    ones = jnp.ones((l, LANES), BF16)
    den = (jnp.dot(a_hi, ones, preferred_element_type=F32)
           + jnp.dot(a_lo, ones, preferred_element_type=F32))[:, 0:1]
    if not zero_state:
        sw = jnp.exp(m_st - c_t)
        n_hi, n_lo = _split_bf16(jnp.broadcast_to(n_st, (LANES, n_st.shape[1])), 2)
        den = den + sw * (_bdot_nt(q, n_hi) + _bdot_nt(q, n_lo))[:, 0:1]
    inv = 1.0 / jnp.maximum(jnp.abs(den), jnp.exp(-(b_col + c_t)))
    h = inv * jnp.dot(a_hi, v, preferred_element_type=F32)
    if not zero_state:
        h = h + (sw * inv) * _bdot(q, c_st)
    b_last = b_col[0:1, :] if rev else b_col[l - 1:l, :]
    log_k = b_last - b_col + i_col
    m_new = jnp.maximum(b_last + m_st, jnp.max(log_k, axis=0, keepdims=True))
    kw = jnp.exp(log_k - m_new)
    kwk = kw * k.astype(F32)
    c_new = _bdot_tn(kwk, v)
    n_new = jnp.sum(kwk, axis=0, keepdims=True)
    if not zero_state:
        decay = jnp.exp(b_last + m_st - m_new)
        c_new = decay * c_st + c_new
        n_new = decay * n_st + n_new
    return h, c_new, n_new, m_new


def _mlstm_scan_kernel(*refs, n_chunks, zero_init):
    single = zero_init and n_chunks == 1
    if zero_init:
        q_ref, k_ref, v_ref, gc_ref, gr_ref, hs_ref, c_ref, n_ref, m_ref, hf_scr, hb_scr = refs
        if not single:
            c_ref[...] = jnp.zeros_like(c_ref)
            n_ref[...] = jnp.zeros_like(n_ref)
            m_ref[...] = jnp.zeros_like(m_ref)
    else:
        (q_ref, k_ref, v_ref, gc_ref, gr_ref, c0_ref, n0_ref, m0_ref,
         hs_ref, c_ref, n_ref, m_ref, hf_scr, hb_scr) = refs
        c_ref[...] = c0_ref[...]
        n_ref[...] = n0_ref[...]
        m_ref[...] = m0_ref[...]
    l = MLSTM_CHUNK
    hh = MLSTM_HEADS
    dk = q_ref.shape[1] // hh
    dv = v_ref.shape[1] // hh

    tt = lax.broadcasted_iota(jnp.int32, (l, l), 0)
    ss = lax.broadcasted_iota(jnp.int32, (l, l), 1)
    masks = (ss <= tt, ss >= tt)
    masks_bf = tuple(jnp.where(m, 1.0, 0.0).astype(BF16) for m in masks)

    def body(c, carry):
        cr = n_chunks - 1 - c
        start = (lambda j: j * l) if single else (lambda j: pl.multiple_of(j * l, l))
        rows = (pl.ds(start(c), l), pl.ds(start(cr), l))
        grs = (gr_ref[c], gr_ref[cr])
        outs, states = ([], []), []
        for d in range(2):
            q, k, v, gc, gr = q_ref[rows[d], :], k_ref[rows[d], :], v_ref[rows[d], :], gc_ref[rows[d], :], grs[d]
            bc, br = _gate_cumsums(gc, gr, masks_bf[d], masks_bf[1 - d])
            for h in range(hh):
                ci, cf = 2 * d * hh + h, (2 * d + 1) * hh + h
                state = (None,) * 3 if single else (c_ref[d, h], n_ref[d, h], m_ref[d, h])
                o, *st = _mlstm_chunk(q[:, h * dk:(h + 1) * dk], k[:, h * dk:(h + 1) * dk],
                                      v[:, h * dv:(h + 1) * dv], gc[:, ci:ci + 1], bc[:, cf:cf + 1],
                                      gr[ci:ci + 1, :], br[cf:cf + 1, :], *state, masks[d], d == 1)
                outs[d].append(o)
                states.append((d, h, st))
        hf_scr[rows[0], :] = jnp.concatenate(outs[0], axis=1)
        hb_scr[rows[1], :] = jnp.concatenate(outs[1], axis=1)
        for d, h, (c_new, n_new, m_new) in states:
            c_ref[d, h], n_ref[d, h], m_ref[d, h] = c_new, n_new, m_new
        return carry

    if single:
        body(0, 0)
    else:
        lax.fori_loop(0, n_chunks, body, 0)
    for h in range(hh):
        cols = pl.ds(h * dv, dv)
        hs = hf_scr[:, cols] + hb_scr[:, cols]
        hs_ref[:, cols] = (hs * lax.rsqrt(jnp.mean(hs * hs, axis=-1, keepdims=True)
                                          + NORM_EPS)).astype(BF16)


def mlstm_scan(qkv, gcol, grow, init, row_off, n_seq, seq_len):
    hh = MLSTM_HEADS
    hv = qkv.shape[1] // 2
    dv = hv // hh
    dk = dv // 2
    l = MLSTM_CHUNK
    nc = seq_len // l
    ob = row_off // seq_len
    kern = functools.partial(_mlstm_scan_kernel, n_chunks=nc, zero_init=init is None)
    st = lambda *tail: pl.BlockSpec((None, 2, hh) + tail, lambda s: (s, 0, 0) + (0,) * len(tail))
    states = [st(dk, dv), st(1, dk), st(1, 1)]
    return pl.pallas_call(
        kern,
        grid=(n_seq,),
        in_specs=[pl.BlockSpec((seq_len, hh * dk), lambda s: (ob + s, 0)),
                  pl.BlockSpec((seq_len, hh * dk), lambda s: (ob + s, 1)),
                  pl.BlockSpec((seq_len, hv), lambda s: (ob + s, 1)),
                  pl.BlockSpec((seq_len, LANES), lambda s: (ob + s, 0)),
                  pl.BlockSpec((nc, 4 * hh, l), lambda s: (ob + s, 0, 0))]
                 + ([] if init is None else states),
        out_specs=[pl.BlockSpec((seq_len, hv), lambda s: (s, 0))] + states,
        out_shape=[jax.ShapeDtypeStruct((n_seq * seq_len, hv), BF16),
                   jax.ShapeDtypeStruct((n_seq, 2, hh, dk, dv), F32),
                   jax.ShapeDtypeStruct((n_seq, 2, hh, 1, dk), F32),
                   jax.ShapeDtypeStruct((n_seq, 2, hh, 1, 1), F32)],
        scratch_shapes=[pltpu.VMEM((seq_len, hv), F32), pltpu.VMEM((seq_len, hv), F32)],
        compiler_params=_params("arbitrary"),
        name="mlstm_scan",
    )(qkv, qkv, qkv, gcol, grow, *(() if init is None else init))


def _mla_proj_kernel(x_ref, g_ref, sh_ref, sc_ref, win_ref, qg_ref, kvg_ref, wqb_ref, cos_ref,
                     sin_ref, q_ref, ckv_ref, kpe_ref, *, q_lora, kv_lora, rope, n_heads):
    h = _modulated(x_ref[...], g_ref[...], sh_ref[...], sc_ref[...])
    proj = _bdot(h, win_ref[...])
    q_lat = proj[:, :q_lora]
    ckv_ref[...] = _rms(proj[:, q_lora:q_lora + kv_lora], kvg_ref[...])
    cos, sin = cos_ref[...], sin_ref[...]
    kpe = proj[:, q_lora + kv_lora:q_lora + kv_lora + rope]
    kpe_rot = proj[:, q_lora + kv_lora + rope:q_lora + kv_lora + 2 * rope]
    kpe_ref[...] = kpe * cos[:, :rope] + kpe_rot * sin[:, :rope]
    q = _bdot(_rms(q_lat, qg_ref[...]), wqb_ref[...])
    n = n_heads * LANES
    pieces = []
    for hd in range(n_heads):
        c = slice(hd * LANES, (hd + 1) * LANES)
        pieces.append(q[:, c])
        pieces.append(q[:, n:2 * n][:, c] * cos + q[:, 2 * n:][:, c] * sin)
    q_ref[...] = jnp.concatenate(pieces, axis=1).astype(BF16)


def _rot_cols(w):
    n = w.shape[1]
    j = np.arange(n)
    quarter = MLA_ROPE // 4
    first = (j % (2 * quarter)) < quarter
    src = np.where(first, j + quarter, j - quarter)
    sign = np.where(first, -1.0, 1.0).astype(np.float32)
    return w[:, src] * sign


def _rope_tables(n_ctx, dec_batch, dec_seq):
    quarter = MLA_ROPE // 4
    freq = np.power(np.float32(ROPE_BASE), -np.arange(quarter, dtype=np.float32) / np.float32(quarter))
    pos = np.arange(dec_seq)
    ang_r = (pos // GRID_W).astype(np.float32)[:, None] * freq[None, :]
    ang_c = (pos % GRID_W).astype(np.float32)[:, None] * freq[None, :]
    ang = np.concatenate([ang_r, ang_r, ang_c, ang_c], axis=1).astype(np.float32)
    cos = np.concatenate([np.ones((n_ctx, MLA_ROPE), np.float32)] + [np.cos(ang)] * dec_batch, axis=0)
    sin = np.concatenate([np.zeros((n_ctx, MLA_ROPE), np.float32)] + [np.sin(ang)] * dec_batch, axis=0)
    reps = LANES // MLA_ROPE
    return (jnp.asarray(np.tile(cos, (1, reps)), F32), jnp.asarray(np.tile(sin, (1, reps)), F32))


def mla_proj(x, g, shift, scale, w_in, q_g, kv_g, w_qb, cos, sin, n_ctx, dec_seq, tm=512):
    t, d = x.shape
    hh, nope, rope = MLA_HEADS, MLA_NOPE, MLA_ROPE
    q_lora = q_g.shape[1]
    kv_lora = kv_g.shape[1]
    w_in_ext = jnp.concatenate([w_in, _rot_cols(w_in[:, q_lora + kv_lora:])], axis=1)
    w3 = w_qb.reshape(q_lora, hh, nope + rope)
    w_qn = w3[:, :, :nope].reshape(q_lora, hh * nope)
    w_qp = w3[:, :, nope:].reshape(q_lora, hh * rope)
    assert nope == LANES and rope <= LANES
    lane_pad = lambda w: jnp.pad(w.reshape(q_lora, hh, rope),
                                 ((0, 0), (0, 0), (0, LANES - rope))).reshape(q_lora, hh * LANES)
    w_qb_ext = jnp.concatenate([w_qn, lane_pad(w_qp), lane_pad(_rot_cols(w_qp))], axis=1)
    cidx = _cvec_index(n_ctx, dec_seq, tm)
    mod_spec = pl.BlockSpec((None, 1, d), lambda i: (cidx(i), 0, 0))
    full = lambda a: pl.BlockSpec(a.shape, lambda i: (0,) * a.ndim)
    kern = functools.partial(_mla_proj_kernel, q_lora=q_lora, kv_lora=kv_lora, rope=rope, n_heads=hh)
    return pl.pallas_call(
        kern,
        grid=(t // tm,),
        in_specs=[pl.BlockSpec((tm, d), lambda i: (i, 0)), full(g), mod_spec, mod_spec,
                  full(w_in_ext), full(q_g), full(kv_g), full(w_qb_ext),
                  pl.BlockSpec((tm, LANES), lambda i: (i, 0)),
                  pl.BlockSpec((tm, LANES), lambda i: (i, 0))],
        out_specs=[pl.BlockSpec((tm, 2 * hh * LANES), lambda i: (i, 0)),
                   pl.BlockSpec((tm, kv_lora), lambda i: (i, 0)),
                   pl.BlockSpec((tm, rope), lambda i: (i, 0))],
        out_shape=[jax.ShapeDtypeStruct((t, 2 * hh * LANES), BF16),
                   jax.ShapeDtypeStruct((t, kv_lora), F32),
                   jax.ShapeDtypeStruct((t, rope), F32)],
        compiler_params=_params("arbitrary"),
        name="mla_proj",
    )(x, g, shift, scale, w_in_ext, q_g, kv_g, w_qb_ext, cos, sin)


def _mla_kv_kernel(ckv_ref, kp_ref, w_ref, k_ref, v_ref):
    kv = _bdot(ckv_ref[...], w_ref[...])
    n = v_ref.shape[1]
    kp = kp_ref[...]
    pieces = []
    for hd in range(n // LANES):
        pieces += [kv[:, hd * LANES:(hd + 1) * LANES].astype(BF16), kp]
    k_ref[...] = jnp.concatenate(pieces, axis=1)
    v_ref[...] = kv[:, n:].astype(BF16)


def mla_kv(ckv_all, kp_pad, w_kvb, tm=512):
    r, kv_lora = ckv_all.shape
    hh, nope, vd = MLA_HEADS, MLA_NOPE, MLA_V
    w3 = w_kvb.reshape(kv_lora, hh, nope + vd)
    w_perm = jnp.concatenate([w3[:, :, :nope].reshape(kv_lora, hh * nope),
                              w3[:, :, nope:].reshape(kv_lora, hh * vd)], axis=1)
    return pl.pallas_call(
        _mla_kv_kernel,
        grid=(r // tm,),
        in_specs=[pl.BlockSpec((tm, kv_lora), lambda i: (i, 0)),
                  pl.BlockSpec((tm, LANES), lambda i: (i, 0)),
                  pl.BlockSpec(w_perm.shape, lambda i: (0, 0))],
        out_specs=[pl.BlockSpec((tm, 2 * hh * LANES), lambda i: (i, 0)),
                   pl.BlockSpec((tm, hh * vd), lambda i: (i, 0))],
        out_shape=[jax.ShapeDtypeStruct((r, 2 * hh * LANES), BF16),
                   jax.ShapeDtypeStruct((r, hh * vd), BF16)],
        compiler_params=_params("arbitrary"),
        name="mla_kv",
    )(ckv_all, kp_pad, w_perm)


def _attn_kernel(q_ref, k_ref, v_ref, o_ref, *, scale):
    c = scale * np.log2(np.e)
    for h in range(MLA_HEADS):
        hk = slice(2 * h * LANES, 2 * (h + 1) * LANES)
        s = _bdot_nt(q_ref[:, hk], k_ref[:, hk])
        e = jnp.exp2((s - jnp.max(s, axis=-1, keepdims=True)) * c)
        o = _bdot(e, v_ref[:, h * MLA_V:(h + 1) * MLA_V]) / jnp.sum(e, axis=-1, keepdims=True)
        o_ref[:, h * MLA_V:(h + 1) * MLA_V] = o.astype(BF16)


def mla_attention(q, k, v, q_row_off, k_row_off, n_seq, q_len, k_len):
    tq = min(ATTN_Q_BLOCK, q_len)
    qb = q_len // tq
    q0 = q_row_off // tq
    k0 = k_row_off // k_len
    dq, dv = q.shape[1], v.shape[1]
    kern = functools.partial(_attn_kernel, scale=(MLA_NOPE + MLA_ROPE) ** -0.5)
    return pl.pallas_call(
        kern,
        grid=(n_seq, qb),
        in_specs=[pl.BlockSpec((tq, dq), lambda s, j: (q0 + s * qb + j, 0)),
                  pl.BlockSpec((k_len, dq), lambda s, j: (k0 + s, 0)),
                  pl.BlockSpec((k_len, dv), lambda s, j: (k0 + s, 0))],
        out_specs=pl.BlockSpec((tq, dv), lambda s, j: (s * qb + j, 0)),
        out_shape=jax.ShapeDtypeStruct((n_seq * q_len, dv), BF16),
        compiler_params=_params("arbitrary", "arbitrary"),
        name="mla_attention",
    )(q, k, v)


def kernel(x_prompt, x_sample, state_mlstm_C, state_mlstm_n, state_mlstm_m, cache_mla_ckv,
           cache_mla_kpe, c, c_ctx, w_ada, b_ada, norm_mix, norm_ffn, norm_final, w_pool,
           pool_scale, w_mlstm_in, b_mlstm_gate, mlstm_head_g, w_mlstm_out, w_mla_in, mla_q_g,
           mla_kv_g, w_mla_qb, w_mla_kvb, w_mla_out, w_router, b_router, w_exp_gate, w_exp_up,
           w_exp_down):
    batch, seq, d = x_prompt.shape
    dec_batch, dec_seq, _ = x_sample.shape
    depth = w_ada.shape[0]
    n_ctx = batch * seq
    n_lat = dec_batch * dec_seq
    hh = MLSTM_HEADS
    past = cache_mla_ckv.shape[2]

    x = (x_prompt.reshape(n_ctx, d), x_sample.reshape(n_lat, d))

    n_cv = 1 + dec_batch
    cvecs = jnp.concatenate([c_ctx[None, :], c, jnp.zeros((SUBLANES - n_cv % SUBLANES, d), F32)], axis=0)
    mod = ada_mod_all(cvecs, w_ada, b_ada).reshape(depth, cvecs.shape[0], 6, 1, d)

    w_router_pad = jnp.pad(w_router, ((0, 0), (0, LANES - N_EXPERTS)))
    w_router_hi = w_router_pad.astype(BF16)
    w_router_lo = (w_router_pad - w_router_hi.astype(F32)).astype(BF16)
    w_router_split = jnp.concatenate([w_router_hi, w_router_lo], axis=1)
    b_router_col = b_router.reshape(N_EXPERTS, 1)
    g_final = norm_final.reshape(1, d)
    row = lambda a: a.reshape(1, -1)

    outs = {}
    for i in range(depth):
        kind, j = i % 3, i // 3
        m = [mod[i, :n_cv, k] for k in range(6)]
        g_mix = row(norm_mix[i])
        if kind == 0:
            x = pool_layer(x, g_mix, m[0], m[1], m[2], w_pool[j], row(pool_scale[j]),
                           n_ctx, seq, dec_seq)
        elif kind == 1:
            qkv, o_gate, gcol, grow = mlstm_proj(x, g_mix, m[0], m[1], w_mlstm_in, j,
                                                 b_mlstm_gate[j], n_ctx, dec_seq)
            t = n_ctx + n_lat
            l = MLSTM_CHUNK
            grow_c = grow.reshape(4 * hh, t // l, l).transpose(1, 0, 2)
            hs_c, c_new, n_new, m_new = mlstm_scan(qkv, gcol, grow_c, None, 0, batch, seq)
            init = (state_mlstm_C[:, j], state_mlstm_n[:, j][:, :, :, None, :],
                    state_mlstm_m[:, j][:, :, :, None, None])
            hs_l, _, _, _ = mlstm_scan(qkv, gcol, grow_c, init, n_ctx, dec_batch, dec_seq)
            outs["C"] = c_new[:, None]
            outs["n"] = n_new[:, None, :, :, 0, :]
            outs["m"] = m_new[:, None, :, :, 0, 0]
            x = mlstm_out(x, hs_c, hs_l, o_gate, row(mlstm_head_g[j]), w_mlstm_out[j], m[2],
                          n_ctx, dec_seq)
        else:
            cos, sin = _rope_tables(n_ctx, dec_batch, dec_seq)
            q_cat, ckv, kpe = mla_proj(x, g_mix, m[0], m[1], w_mla_in[j], row(mla_q_g[j]),
                                       row(mla_kv_g[j]), w_mla_qb[j], cos, sin, n_ctx, dec_seq)
            lat_parts_c, lat_parts_p = [], []
            for b in range(dec_batch):
                lo = n_ctx + b * dec_seq
                lat_parts_c += [cache_mla_ckv[b, j], ckv[lo:lo + dec_seq]]
                lat_parts_p += [cache_mla_kpe[b, j], kpe[lo:lo + dec_seq]]
            ckv_all = jnp.concatenate(lat_parts_c + [ckv[:n_ctx]], axis=0)
            kp_all = jnp.concatenate(lat_parts_p + [kpe[:n_ctx]], axis=0).astype(BF16)
            kp_pad = jnp.pad(kp_all, ((0, 0), (0, LANES - kp_all.shape[1])))
            k_cat, v = mla_kv(ckv_all, kp_pad, w_mla_kvb[j])
            k_lat = past + dec_seq
            o_c = mla_attention(q_cat, k_cat, v, 0, dec_batch * k_lat, batch, seq, seq)
            o_l = mla_attention(q_cat, k_cat, v, n_ctx, 0, dec_batch, dec_seq, k_lat)
            outs["ckv"] = ckv[:n_ctx].reshape(batch, 1, seq, -1)
            outs["kpe"] = kpe[:n_ctx].reshape(batch, 1, seq, -1)
            x = resid_proj(x, o_c, o_l, w_mla_out[j], m[2], n_ctx, dec_seq)
        x = moe_layer(x, row(norm_ffn[i]), m[3], m[4], m[5], w_router_split, b_router_col,
                      w_exp_gate, w_exp_up, w_exp_down, i, g_final, n_ctx, dec_seq,
                      final_norm=(i == depth - 1), split_output=(i == depth - 1))

    y_prompt = x[0].reshape(batch, seq, d)
    y_sample = x[1].reshape(dec_batch, dec_seq, d)
    return (y_prompt, y_sample, outs["C"], outs["n"], outs["m"], outs["ckv"], outs["kpe"])
```

```python
import functools

import numpy as np
import jax
import jax.numpy as jnp
from jax import lax
from jax.experimental import pallas as pl
from jax.experimental.pallas import tpu as pltpu

F32 = jnp.float32
BF16 = jnp.bfloat16

NORM_EPS = 1e-6
GRID_W = 64
POOL_WINDOWS = (2, 4, 8, 16)
MLSTM_HEADS = 4
MLSTM_CHUNK = 256
MLA_HEADS = 8
MLA_NOPE = 128
MLA_ROPE = 64
MLA_V = 128
ROPE_BASE = 10000.0
N_EXPERTS = 16
N_EXPERT_GROUPS = 4
EXPERTS_PER_GROUP = N_EXPERTS // N_EXPERT_GROUPS

LANES = 128
SUBLANES = 8
VMEM_LIMIT = 56 * 1024 * 1024
POOL_TILE = 256
POOL_HALO = 8
ATTN_Q_BLOCK = 256
MOE_TILE = 512
MOE_SEG_ALIGN = 16
MOE_ROW_BLOCK = 160
MOE_WIDE_BLOCK = 224


def _params(*sem):
    return pltpu.CompilerParams(dimension_semantics=sem, vmem_limit_bytes=VMEM_LIMIT)


def _rms(x, g):
    return x * lax.rsqrt(jnp.mean(x * x, axis=-1, keepdims=True) + NORM_EPS) * g


def _modulated(x, g, shift, scale):
    return _rms(x, g) * (1.0 + scale) + shift


def _silu(x):
    return x * jax.nn.sigmoid(x)


def _bdot(a, b):
    return jnp.dot(a.astype(BF16), b.astype(BF16), preferred_element_type=F32)


def _bdot_nt(a, b):
    return lax.dot_general(a.astype(BF16), b.astype(BF16), (((1,), (1,)), ((), ())),
                           preferred_element_type=F32)


def _bdot_tn(a, b):
    return lax.dot_general(a.astype(BF16), b.astype(BF16), (((0,), (0,)), ((), ())),
                           preferred_element_type=F32)


def _cvec_index(n_ctx, dec_seq, tm):
    def idx(i):
        r = i * tm
        return jnp.where(r < n_ctx, 0, (r - n_ctx) // dec_seq + 1)
    return idx


def _ada_kernel(c_ref, w_ref, b_ref, o_ref):
    o_ref[...] = _bdot(_silu(c_ref[...]), w_ref[...]) + b_ref[...]


def ada_mod_all(cvecs, w_ada, b_ada, tn=1536):
    depth, d, n6 = w_ada.shape
    rows = cvecs.shape[0]
    return pl.pallas_call(
        _ada_kernel,
        grid=(depth, n6 // tn),
        in_specs=[pl.BlockSpec((rows, d), lambda l, n: (0, 0)),
                  pl.BlockSpec((None, d, tn), lambda l, n: (l, 0, n)),
                  pl.BlockSpec((None, 1, tn), lambda l, n: (l, 0, n))],
        out_specs=pl.BlockSpec((None, rows, tn), lambda l, n: (l, 0, n)),
        out_shape=jax.ShapeDtypeStruct((depth, rows, n6), F32),
        compiler_params=_params("arbitrary", "arbitrary"),
        name="ada_mod",
    )(cvecs, w_ada, b_ada.reshape(depth, 1, n6))


def _pool_kernel(*refs, n_ctx_tiles, ctx_seq_tiles, lat_seq_tiles, split_input):
    n_x = 6 if split_input else 3
    x_refs = refs[:n_x]
    g_ref, sh_ref, sc_ref, gt_ref, wp_ref, ps_ref, o_ref, buf_ref, *lvl_refs = refs[n_x:]
    i = pl.program_id(0)
    is_ctx = i < n_ctx_tiles
    j = jnp.where(is_ctx, i % ctx_seq_tiles, (i - n_ctx_tiles) % lat_seq_tiles)
    nt = jnp.where(is_ctx, ctx_seq_tiles, lat_seq_tiles)
    g, sh, sc = g_ref[...], sh_ref[...], sc_ref[...]
    tp, hl = POOL_TILE, POOL_HALO
    gw = o_ref.shape[1] // len(POOL_WINDOWS)

    def fill(xc_ref, xp_ref, xn_ref):
        buf_ref[pl.ds(0, hl), :] = jnp.where(j == 0, 0.0, _modulated(xp_ref[...], g, sh, sc))
        buf_ref[pl.ds(hl, tp), :] = _modulated(xc_ref[...], g, sh, sc)
        buf_ref[pl.ds(hl + tp, hl), :] = jnp.where(j == nt - 1, 0.0,
                                                    _modulated(xn_ref[...], g, sh, sc))
        o_ref[...] = xc_ref[...]

    if split_input:
        pl.when(is_ctx)(lambda: fill(*x_refs[:3]))
        pl.when(jnp.logical_not(is_ctx))(lambda: fill(*x_refs[3:]))
    else:
        fill(*x_refs)

    rows = tp + 2 * hl
    buf_ref[pl.ds(rows, hl), :] = jnp.zeros((hl, buf_ref.shape[1]), F32)
    for ref in lvl_refs:
        ref[pl.ds(rows, hl), :] = jnp.zeros((hl, gw), F32)
    pos = j * tp + lax.broadcasted_iota(jnp.int32, (tp, 1), 0)
    seq_len = nt * tp
    for gi, w in enumerate(POOL_WINDOWS):
        cols = pl.ds(gi * gw, gw)
        src, src_cols = buf_ref, cols
        for lvl in range(1, w.bit_length() - 1):
            step = 1 << (lvl - 1)
            dst = lvl_refs[lvl - 1]
            dst[pl.ds(0, rows), :] = src[pl.ds(0, rows), src_cols] + src[pl.ds(step, rows), src_cols]
            src, src_cols = dst, pl.ds(0, gw)
        acc = src[pl.ds(hl - w // 2, tp), src_cols] + src[pl.ds(hl, tp), src_cols]
        cnt = jnp.minimum(pos + w // 2, seq_len) - jnp.maximum(pos - w // 2, 0)
        pooled = acc / cnt.astype(F32) - buf_ref[pl.ds(hl, tp), cols]
        y = _bdot(pooled, wp_ref[gi]) * ps_ref[:, cols]
        o_ref[:, cols] = o_ref[:, cols] + gt_ref[:, cols] * y


def _halo_specs(tp, hl, d, tile_off, n_rows):
    hb = tp // hl
    last_tile, last_hblk = n_rows // tp - 1, n_rows // hl - 1
    tile = lambda i: jnp.clip(i - tile_off, 0, last_tile)
    return [pl.BlockSpec((tp, d), lambda i: (tile(i), 0)),
            pl.BlockSpec((hl, d), lambda i: (jnp.clip(tile(i) * hb - 1, 0, last_hblk), 0)),
            pl.BlockSpec((hl, d), lambda i: (jnp.clip((tile(i) + 1) * hb, 0, last_hblk), 0))]


def pool_layer(xs, g, shift, scale, gate, w_pool, pool_scale, n_ctx, seq, dec_seq):
    split = isinstance(xs, tuple)
    tp, hl = POOL_TILE, POOL_HALO
    if split:
        d = xs[0].shape[1]
        t = xs[0].shape[0] + xs[1].shape[0]
        x_specs = (_halo_specs(tp, hl, d, 0, xs[0].shape[0])
                   + _halo_specs(tp, hl, d, n_ctx // tp, xs[1].shape[0]))
        x_args = (xs[0],) * 3 + (xs[1],) * 3
    else:
        t, d = xs.shape
        x_specs = _halo_specs(tp, hl, d, 0, t)
        x_args = (xs,) * 3
    cidx = _cvec_index(n_ctx, dec_seq, tp)
    mod_spec = pl.BlockSpec((None, 1, d), lambda i: (cidx(i), 0, 0))
    row_spec = pl.BlockSpec((1, d), lambda i: (0, 0))
    assert all(w & (w - 1) == 0 and w // 2 <= hl for w in POOL_WINDOWS)
    n_levels = max(POOL_WINDOWS).bit_length() - 2
    kern = functools.partial(_pool_kernel, n_ctx_tiles=n_ctx // tp, ctx_seq_tiles=seq // tp,
                             lat_seq_tiles=dec_seq // tp, split_input=split)
    return pl.pallas_call(
        kern,
        grid=(t // tp,),
        in_specs=x_specs + [row_spec, mod_spec, mod_spec, mod_spec,
                            pl.BlockSpec(w_pool.shape, lambda i: (0, 0, 0)), row_spec],
        out_specs=pl.BlockSpec((tp, d), lambda i: (i, 0)),
        out_shape=jax.ShapeDtypeStruct((t, d), F32),
        scratch_shapes=[pltpu.VMEM((tp + 3 * hl, d), F32)]
        + [pltpu.VMEM((tp + 3 * hl, d // len(POOL_WINDOWS)), F32)] * n_levels,
        compiler_params=_params("arbitrary"),
        name="pool_mixer",
    )(*x_args, g, shift, scale, gate, w_pool, pool_scale)


def _route(sel, scores):
    e, tm = sel.shape
    row = lax.broadcasted_iota(jnp.int32, (e, tm), 0)
    best = jnp.zeros((1, tm), jnp.int32)
    best_sc = None
    for gidx in range(N_EXPERT_GROUPS):
        r = [sel[gidx * EXPERTS_PER_GROUP + k:gidx * EXPERTS_PER_GROUP + k + 1, :]
             for k in range(EXPERTS_PER_GROUP)]
        top2 = None
        for a in range(EXPERTS_PER_GROUP):
            for b in range(a + 1, EXPERTS_PER_GROUP):
                s = r[a] + r[b]
                top2 = s if top2 is None else jnp.maximum(top2, s)
        if best_sc is None:
            best_sc = top2
        else:
            better = top2 > best_sc
            best = jnp.where(better, gidx, best)
            best_sc = jnp.where(better, top2, best_sc)
    neg = -jnp.inf
    masked = jnp.where(row // EXPERTS_PER_GROUP == best, sel, neg)
    m1 = jnp.max(masked, axis=0, keepdims=True)
    i1 = jnp.min(jnp.where(masked == m1, row, e), axis=0, keepdims=True)
    masked2 = jnp.where(row == i1, neg, masked)
    m2 = jnp.max(masked2, axis=0, keepdims=True)
    i2 = jnp.min(jnp.where(masked2 == m2, row, e), axis=0, keepdims=True)
    hot1 = row == i1
    hot2 = row == i2
    w1 = jnp.sum(jnp.where(hot1, scores, 0.0), axis=0, keepdims=True)
    w2 = jnp.sum(jnp.where(hot2, scores, 0.0), axis=0, keepdims=True)
    tot = w1 + w2
    return best, jnp.where(hot1, w1 / tot, 0.0) + jnp.where(hot2, w2 / tot, 0.0)


def _split_bf16(a, parts):
    out = []
    for _ in range(parts):
        p = a.astype(BF16)
        out.append(p)
        a = a - p.astype(F32)
    return out


def _pad_rows(a, rows):
    return jnp.concatenate([a, jnp.zeros((rows - a.shape[0], a.shape[1]), a.dtype)], axis=0)


def _moe_kernel(x_ref, g_ref, sh_ref, sc_ref, gt_ref, wr_ref, br_ref, wgf_ref, wuf_ref, wdf_ref,
                gf_ref, *rest, final_norm, n_ctx_tiles, n_experts):
    out_refs, (wgu_ref, wd_ref), scratch = rest[:-5], rest[-5:-3], rest[-3:]
    i = pl.program_id(0)
    f = wgf_ref.shape[1]
    for e in range(n_experts):
        gi, k = divmod(e, EXPERTS_PER_GROUP)

        @pl.when(i == e)
        def _(gi=gi, k=k):
            wgu_ref[gi, :, pl.ds(2 * k * f, f)] = wgf_ref[...].astype(BF16)
            wgu_ref[gi, :, pl.ds((2 * k + 1) * f, f)] = wuf_ref[...].astype(BF16)
            wd_ref[gi, pl.ds(k * f, f), :] = wdf_ref[...].astype(BF16)

    @pl.when(i >= n_experts)
    def _():
        _moe_tile(x_ref, g_ref, sh_ref, sc_ref, gt_ref, wr_ref, br_ref, wgu_ref, wd_ref,
                  gf_ref, out_refs, scratch, i - n_experts, final_norm, n_ctx_tiles)


def _moe_tile(x_ref, g_ref, sh_ref, sc_ref, gt_ref, wr_ref, br_ref, wgu_ref, wd_ref,
              gf_ref, out_refs, scratch, tile, final_norm, n_ctx_tiles):
    hp_scr, cw_scr, yp_scr = scratch
    tr, d = x_ref.shape
    trp = hp_scr.shape[0]
    ng, eg = N_EXPERT_GROUPS, EXPERTS_PER_GROUP
    x = x_ref[...]
    h = _modulated(x, g_ref[...], sh_ref[...], sc_ref[...])
    hb = h.astype(BF16)

    h_lo = (h - hb.astype(F32)).astype(BF16)
    wr = wr_ref[...]
    lg = jnp.dot(hb, wr, preferred_element_type=F32)
    logits = lg[:, :LANES] + lg[:, LANES:] + jnp.dot(h_lo, wr[:, :LANES], preferred_element_type=F32)
    scores = jax.nn.sigmoid(logits.T[:N_EXPERTS, :])
    best, comb_t = _route(scores + br_ref[...], scores)

    grp = lax.broadcasted_iota(jnp.int32, (SUBLANES, tr), 0)
    hot_t = (grp == best).astype(F32)
    cw_t = hot_t[0:1, :] * comb_t[0:eg, :]
    for gi in range(1, ng):
        cw_t = cw_t + hot_t[gi:gi + 1, :] * comb_t[gi * eg:(gi + 1) * eg, :]
    cw_c = _pad_rows(cw_t, LANES).T

    ia = lax.broadcasted_iota(jnp.int32, (tr, tr), 0)
    ib = lax.broadcasted_iota(jnp.int32, (tr, tr), 1)
    before = jnp.where(ia < ib, 1.0, 0.0).astype(BF16)
    rank_t = jnp.dot(hot_t.astype(BF16), before, preferred_element_type=F32)

    starts, counts = [], []
    off = jnp.int32(0)
    for gi in range(ng):
        n = jnp.sum(hot_t[gi:gi + 1, :]).astype(jnp.int32)
        n = ((n + MOE_SEG_ALIGN - 1) // MOE_SEG_ALIGN) * MOE_SEG_ALIGN
        starts.append(off)
        counts.append(n)
        off = off + n

    pos_t = hot_t[0:1, :] * (rank_t[0:1, :] + starts[0].astype(F32))
    for gi in range(1, ng):
        pos_t = pos_t + hot_t[gi:gi + 1, :] * (rank_t[gi:gi + 1, :] + starts[gi].astype(F32))
    pos_c = _pad_rows(pos_t, LANES).T[:, 0:1]
    used = tr + ng * MOE_SEG_ALIGN
    perm = jnp.where(lax.broadcasted_iota(jnp.int32, (used, tr), 0) == pos_t.astype(jnp.int32),
                     1.0, 0.0).astype(BF16)
    perm_t = jnp.where(lax.broadcasted_iota(jnp.int32, (tr, used), 1) == pos_c.astype(jnp.int32),
                       1.0, 0.0).astype(BF16)

    hp_scr[pl.ds(0, used), :] = jnp.dot(perm, hb, preferred_element_type=F32).astype(BF16)
    hp_scr[pl.ds(used, trp - used), :] = jnp.zeros((trp - used, d), BF16)
    cw_pair = jnp.dot(perm, jnp.concatenate(_split_bf16(cw_c, 2), axis=1), preferred_element_type=F32)
    cw_scr[pl.ds(0, used), :] = cw_pair[:, :LANES] + cw_pair[:, LANES:]
    cw_scr[pl.ds(used, trp - used), :] = jnp.zeros((trp - used, LANES), F32)
    yp_scr[...] = jnp.zeros_like(yp_scr)

    f = wd_ref.shape[1] // eg

    def expert_block(gi, row0, n_rows):
        rows = pl.ds(pl.multiple_of(row0, MOE_SEG_ALIGN), n_rows)
        cwb = cw_scr[rows, :]
        gu = jnp.dot(hp_scr[rows, :], wgu_ref[gi], preferred_element_type=F32)
        hid = [(_silu(gu[:, 2 * k * f:(2 * k + 1) * f]) * gu[:, (2 * k + 1) * f:(2 * k + 2) * f]
                * cwb[:, k:k + 1]).astype(BF16) for k in range(eg)]
        yp_scr[rows, :] = jnp.dot(jnp.concatenate(hid, axis=1), wd_ref[gi],
                                  preferred_element_type=F32)

    sb, wide = MOE_ROW_BLOCK, MOE_WIDE_BLOCK
    for gi in range(ng):
        use_wide = jnp.logical_and(counts[gi] > sb, counts[gi] <= wide)
        pl.when(use_wide)(functools.partial(expert_block, gi, starts[gi], wide))
        n_blocks = jnp.where(use_wide, 0, (counts[gi] + sb - 1) // sb)

        def block(b, carry, gi=gi):
            expert_block(gi, starts[gi] + b * sb, sb)
            return carry
        lax.fori_loop(0, n_blocks, block, 0)

    y_hi, y_lo = _split_bf16(yp_scr[pl.ds(0, used), :], 2)
    moe = (jnp.dot(perm_t, y_hi, preferred_element_type=F32)
           + jnp.dot(perm_t, y_lo, preferred_element_type=F32))
    out = x + gt_ref[...] * moe
    if final_norm:
        out = _rms(out, gf_ref[...])
    if n_ctx_tiles is None:
        out_refs[0][...] = out
    else:
        is_ctx = tile < n_ctx_tiles

        @pl.when(is_ctx)
        def _():
            out_refs[0][...] = out

        @pl.when(jnp.logical_not(is_ctx))
        def _():
            out_refs[1][...] = out


def moe_layer(x, g, shift, scale, gate, w_router_split, b_router_col, wg, wu, wd, layer, g_final,
              n_ctx, dec_seq, final_norm, split_output=False, tm=MOE_TILE):
    t, d = x.shape
    _, n_e, _, f = wg.shape
    trp = tm + N_EXPERT_GROUPS * MOE_SEG_ALIGN + max(MOE_ROW_BLOCK, MOE_WIDE_BLOCK)
    tile = lambda i: jnp.maximum(i - n_e, 0)
    cidx = _cvec_index(n_ctx, dec_seq, tm)
    mod_spec = pl.BlockSpec((None, 1, d), lambda i: (cidx(tile(i)), 0, 0))
    row_spec = pl.BlockSpec((1, d), lambda i: (0, 0))
    expert = lambda i: jnp.minimum(i, n_e - 1)
    nct = n_ctx // tm
    if split_output:
        out_specs = [pl.BlockSpec((tm, d), lambda i: (jnp.minimum(tile(i), nct - 1), 0)),
                     pl.BlockSpec((tm, d), lambda i: (jnp.maximum(tile(i) - nct, 0), 0))]
        out_shape = [jax.ShapeDtypeStruct((n_ctx, d), F32), jax.ShapeDtypeStruct((t - n_ctx, d), F32)]
    else:
        out_specs = pl.BlockSpec((tm, d), lambda i: (tile(i), 0))
        out_shape = jax.ShapeDtypeStruct((t, d), F32)
    return pl.pallas_call(
        functools.partial(_moe_kernel, final_norm=final_norm,
                          n_ctx_tiles=nct if split_output else None, n_experts=n_e),
        grid=(n_e + t // tm,),
        in_specs=[pl.BlockSpec((tm, d), lambda i: (tile(i), 0)),
                  row_spec, mod_spec, mod_spec, mod_spec,
                  pl.BlockSpec(w_router_split.shape, lambda i: (0, 0)),
                  pl.BlockSpec(b_router_col.shape, lambda i: (0, 0)),
                  pl.BlockSpec((None, None, d, f), lambda i: (layer, expert(i), 0, 0)),
                  pl.BlockSpec((None, None, d, f), lambda i: (layer, expert(i), 0, 0)),
                  pl.BlockSpec((None, None, f, d), lambda i: (layer, expert(i), 0, 0)),
                  row_spec],
        out_specs=out_specs,
        out_shape=out_shape,
        scratch_shapes=[pltpu.VMEM((N_EXPERT_GROUPS, d, 2 * EXPERTS_PER_GROUP * f), BF16),
                        pltpu.VMEM((N_EXPERT_GROUPS, EXPERTS_PER_GROUP * f, d), BF16),
                        pltpu.VMEM((trp, d), BF16), pltpu.VMEM((trp, LANES), F32),
                        pltpu.VMEM((trp, d), F32)],
        compiler_params=_params("arbitrary"),
        name="moe",
    )(x, g, shift, scale, gate, w_router_split, b_router_col, wg, wu, wd, g_final)


def _per_stream(n_ctx_tiles, ctx_ref, lat_ref, fn):
    i = pl.program_id(0)

    @pl.when(i < n_ctx_tiles)
    def _():
        fn(ctx_ref[...])

    @pl.when(i >= n_ctx_tiles)
    def _():
        fn(lat_ref[...])


def _stream_specs(tm, k, n_ctx_tiles):
    return [pl.BlockSpec((tm, k), lambda i: (jnp.minimum(i, n_ctx_tiles - 1), 0)),
            pl.BlockSpec((tm, k), lambda i: (jnp.maximum(i - n_ctx_tiles, 0), 0))]


def _resid_proj_kernel(x_ref, ac_ref, al_ref, w_ref, gt_ref, o_ref, *, n_ctx_tiles):
    def run(a):
        o_ref[...] = x_ref[...] + gt_ref[...] * _bdot(a, w_ref[...])
    _per_stream(n_ctx_tiles, ac_ref, al_ref, run)


def _mlstm_out_kernel(x_ref, hc_ref, hl_ref, og_ref, hg_ref, w_ref, gt_ref, o_ref, *, n_ctx_tiles):
    def run(hs):
        a = jax.nn.sigmoid(og_ref[...].astype(F32)) * (hs.astype(F32) * hg_ref[...])
        o_ref[...] = x_ref[...] + gt_ref[...] * _bdot(a, w_ref[...])
    _per_stream(n_ctx_tiles, hc_ref, hl_ref, run)


def resid_proj(x, a_ctx, a_lat, w, gate, n_ctx, dec_seq, tm=512):
    t, d = x.shape
    k = a_ctx.shape[1]
    cidx = _cvec_index(n_ctx, dec_seq, tm)
    nct = n_ctx // tm
    return pl.pallas_call(
        functools.partial(_resid_proj_kernel, n_ctx_tiles=nct),
        grid=(t // tm,),
        in_specs=[pl.BlockSpec((tm, d), lambda i: (i, 0))] + _stream_specs(tm, k, nct) + [
            pl.BlockSpec((k, d), lambda i: (0, 0)),
            pl.BlockSpec((None, 1, d), lambda i: (cidx(i), 0, 0))],
        out_specs=pl.BlockSpec((tm, d), lambda i: (i, 0)),
        out_shape=jax.ShapeDtypeStruct((t, d), F32),
        compiler_params=_params("arbitrary"),
        name="resid_proj",
    )(x, a_ctx, a_lat, w, gate)


def mlstm_out(x, hs_ctx, hs_lat, o_gate, head_g, w, gate, n_ctx, dec_seq, tm=512):
    t, d = x.shape
    k = hs_ctx.shape[1]
    cidx = _cvec_index(n_ctx, dec_seq, tm)
    nct = n_ctx // tm
    return pl.pallas_call(
        functools.partial(_mlstm_out_kernel, n_ctx_tiles=nct),
        grid=(t // tm,),
        in_specs=[pl.BlockSpec((tm, d), lambda i: (i, 0))] + _stream_specs(tm, k, nct) + [
            pl.BlockSpec((tm, k), lambda i: (i, 0)),
            pl.BlockSpec((1, k), lambda i: (0, 0)),
            pl.BlockSpec((k, d), lambda i: (0, 0)),
            pl.BlockSpec((None, 1, d), lambda i: (cidx(i), 0, 0))],
        out_specs=pl.BlockSpec((tm, d), lambda i: (i, 0)),
        out_shape=jax.ShapeDtypeStruct((t, d), F32),
        compiler_params=_params("arbitrary"),
        name="mlstm_out",
    )(x, hs_ctx, hs_lat, o_gate, head_g, w, gate)


def _mlstm_proj_kernel(x_ref, g_ref, sh_ref, sc_ref, wqk_ref, wv_ref, wo_ref, wgt_ref, bg_ref,
                       qs_ref, qkv_ref, o_ref, gc_ref, gr_ref):
    h = _modulated(x_ref[...], g_ref[...], sh_ref[...], sc_ref[...]).astype(BF16)
    nqk = wqk_ref.shape[0]
    qkv_ref[:, :nqk] = (_bdot_nt(h, wqk_ref[...]) * qs_ref[...]).astype(BF16)
    qkv_ref[:, nqk:] = _bdot_nt(h, wv_ref[...]).astype(BF16)
    o_ref[...] = _bdot_nt(h, wo_ref[...]).astype(BF16)
    gates = _bdot_nt(h, wgt_ref[...]) + bg_ref[...]
    gc_ref[...] = gates
    gr_ref[...] = gates.T[:gr_ref.shape[0], :]


def mlstm_proj(x, g, shift, scale, w_in_all, layer, b_gate, n_ctx, dec_seq, tm=512):
    t, d = x.shape
    hh = MLSTM_HEADS
    hv = d
    hk = hv // 2
    ng = 4 * hh
    w_t_all = jnp.swapaxes(w_in_all, 1, 2)
    w_g_pad = jnp.pad(w_t_all[layer, 2 * hk + 2 * hv:, :], ((0, LANES - ng), (0, 0)))
    b_pad = jnp.pad(b_gate.reshape(1, ng), ((0, 0), (0, LANES - ng)))
    dk = hk // hh
    q_scale = jnp.concatenate([jnp.full((1, hk), dk ** -0.5, F32), jnp.ones((1, hk), F32)], axis=1)
    cidx = _cvec_index(n_ctx, dec_seq, tm)
    mod_spec = pl.BlockSpec((None, 1, d), lambda i: (cidx(i), 0, 0))
    full = lambda a: pl.BlockSpec(a.shape, lambda i: (0,) * a.ndim)
    assert 2 * hk == hv
    w_col = lambda n: pl.BlockSpec((None, hv, d), lambda i: (layer, n, 0))
    return pl.pallas_call(
        _mlstm_proj_kernel,
        grid=(t // tm,),
        in_specs=[pl.BlockSpec((tm, d), lambda i: (i, 0)), full(g), mod_spec, mod_spec,
                  w_col(0), w_col(1), w_col(2), full(w_g_pad), full(b_pad), full(q_scale)],
        out_specs=[pl.BlockSpec((tm, 2 * hk + hv), lambda i: (i, 0)),
                   pl.BlockSpec((tm, hv), lambda i: (i, 0)),
                   pl.BlockSpec((tm, LANES), lambda i: (i, 0)),
                   pl.BlockSpec((ng, tm), lambda i: (0, i))],
        out_shape=[jax.ShapeDtypeStruct((t, 2 * hk + hv), BF16),
                   jax.ShapeDtypeStruct((t, hv), BF16),
                   jax.ShapeDtypeStruct((t, LANES), F32),
                   jax.ShapeDtypeStruct((ng, t), F32)],
        compiler_params=_params("arbitrary"),
        name="mlstm_proj",
    )(x, g, shift, scale, w_t_all, w_t_all, w_t_all, w_g_pad, b_pad, q_scale)


def _log_sigmoid(x):
    return jnp.minimum(x, 0.0) - jnp.log(1.0 + jnp.exp(-jnp.abs(x)))


def _gate_cumsums(gc, gr, causal_bf, feeds_bf):
    bc = br = None
    for part in _split_bf16(_log_sigmoid(gc), 3):
        t = jnp.dot(causal_bf, part, preferred_element_type=F32)
        bc = t if bc is None else bc + t
    for part in _split_bf16(_log_sigmoid(gr), 3):
        t = jnp.dot(part, feeds_bf, preferred_element_type=F32)
        br = t if br is None else br + t
    return bc, br


def _mlstm_chunk(q, k, v, i_col, b_col, i_row, b_row, c_st, n_st, m_st, causal, rev):
    l = q.shape[0]
    zero_state = c_st is None
    if zero_state:
        m_st = 0.0
    g_row = i_row - b_row
    log_w = jnp.where(causal, g_row, -jnp.inf)
    c_t = jnp.maximum(m_st, jnp.max(log_w, axis=1, keepdims=True))
    dw = jnp.exp(log_w - c_t)
    a = _bdot_nt(q, k) * dw
    a_hi, a_lo = _split_bf16(a, 2)
    ones = jnp.ones((l, LANES), BF16)
    den = (jnp.dot(a_hi, ones, preferred_element_type=F32)
           + jnp.dot(a_lo, ones, preferred_element_type=F32))[:, 0:1]
    if not zero_state:
        sw = jnp.exp(m_st - c_t)
        n_hi, n_lo = _split_bf16(jnp.broadcast_to(n_st, (LANES, n_st.shape[1])), 2)
        den = den + sw * (_bdot_nt(q, n_hi) + _bdot_nt(q, n_lo))[:, 0:1]
    inv = 1.0 / jnp.maximum(jnp.abs(den), jnp.exp(-(b_col + c_t)))
    h = inv * jnp.dot(a_hi, v, preferred_element_type=F32)
    if not zero_state:
        h = h + (sw * inv) * _bdot(q, c_st)
    b_last = b_col[0:1, :] if rev else b_col[l - 1:l, :]
    log_k = b_last - b_col + i_col
    m_new = jnp.maximum(b_last + m_st, jnp.max(log_k, axis=0, keepdims=True))
    kw = jnp.exp(log_k - m_new)
    kwk = kw * k.astype(F32)
    c_new = _bdot_tn(kwk, v)
    n_new = jnp.sum(kwk, axis=0, keepdims=True)
    if not zero_state:
        decay = jnp.exp(b_last + m_st - m_new)
        c_new = decay * c_st + c_new
        n_new = decay * n_st + n_new
    return h, c_new, n_new, m_new


def _mlstm_scan_kernel(*refs, n_chunks, zero_init):
    single = zero_init and n_chunks == 1
    if zero_init:
        q_ref, k_ref, v_ref, gc_ref, gr_ref, hs_ref, c_ref, n_ref, m_ref, hf_scr, hb_scr = refs
        if not single:
            c_ref[...] = jnp.zeros_like(c_ref)
            n_ref[...] = jnp.zeros_like(n_ref)
            m_ref[...] = jnp.zeros_like(m_ref)
    else:
        (q_ref, k_ref, v_ref, gc_ref, gr_ref, c0_ref, n0_ref, m0_ref,
         hs_ref, c_ref, n_ref, m_ref, hf_scr, hb_scr) = refs
        c_ref[...] = c0_ref[...]
        n_ref[...] = n0_ref[...]
        m_ref[...] = m0_ref[...]
    l = MLSTM_CHUNK
    hh = MLSTM_HEADS
    dk = q_ref.shape[1] // hh
    dv = v_ref.shape[1] // hh

    tt = lax.broadcasted_iota(jnp.int32, (l, l), 0)
    ss = lax.broadcasted_iota(jnp.int32, (l, l), 1)
    masks = (ss <= tt, ss >= tt)
    masks_bf = tuple(jnp.where(m, 1.0, 0.0).astype(BF16) for m in masks)

    def body(c, carry):
        cr = n_chunks - 1 - c
        start = (lambda j: j * l) if single else (lambda j: pl.multiple_of(j * l, l))
        rows = (pl.ds(start(c), l), pl.ds(start(cr), l))
        grs = (gr_ref[c], gr_ref[cr])
        outs, states = ([], []), []
        for d in range(2):
            q, k, v, gc, gr = q_ref[rows[d], :], k_ref[rows[d], :], v_ref[rows[d], :], gc_ref[rows[d], :], grs[d]
            bc, br = _gate_cumsums(gc, gr, masks_bf[d], masks_bf[1 - d])
            for h in range(hh):
                ci, cf = 2 * d * hh + h, (2 * d + 1) * hh + h
                state = (None,) * 3 if single else (c_ref[d, h], n_ref[d, h], m_ref[d, h])
                o, *st = _mlstm_chunk(q[:, h * dk:(h + 1) * dk], k[:, h * dk:(h + 1) * dk],
                                      v[:, h * dv:(h + 1) * dv], gc[:, ci:ci + 1], bc[:, cf:cf + 1],
                                      gr[ci:ci + 1, :], br[cf:cf + 1, :], *state, masks[d], d == 1)
                outs[d].append(o)
                states.append((d, h, st))
        hf_scr[rows[0], :] = jnp.concatenate(outs[0], axis=1)
        hb_scr[rows[1], :] = jnp.concatenate(outs[1], axis=1)
        for d, h, (c_new, n_new, m_new) in states:
            c_ref[d, h], n_ref[d, h], m_ref[d, h] = c_new, n_new, m_new
        return carry

    if single:
        body(0, 0)
    else:
        lax.fori_loop(0, n_chunks, body, 0)
    for h in range(hh):
        cols = pl.ds(h * dv, dv)
        hs = hf_scr[:, cols] + hb_scr[:, cols]
        hs_ref[:, cols] = (hs * lax.rsqrt(jnp.mean(hs * hs, axis=-1, keepdims=True)
                                          + NORM_EPS)).astype(BF16)


def mlstm_scan(qkv, gcol, grow, init, row_off, n_seq, seq_len):
    hh = MLSTM_HEADS
    hv = qkv.shape[1] // 2
    dv = hv // hh
    dk = dv // 2
    l = MLSTM_CHUNK
    nc = seq_len // l
    ob = row_off // seq_len
    kern = functools.partial(_mlstm_scan_kernel, n_chunks=nc, zero_init=init is None)
    st = lambda *tail: pl.BlockSpec((None, 2, hh) + tail, lambda s: (s, 0, 0) + (0,) * len(tail))
    states = [st(dk, dv), st(1, dk), st(1, 1)]
    return pl.pallas_call(
        kern,
        grid=(n_seq,),
        in_specs=[pl.BlockSpec((seq_len, hh * dk), lambda s: (ob + s, 0)),
                  pl.BlockSpec((seq_len, hh * dk), lambda s: (ob + s, 1)),
                  pl.BlockSpec((seq_len, hv), lambda s: (ob + s, 1)),
                  pl.BlockSpec((seq_len, LANES), lambda s: (ob + s, 0)),
                  pl.BlockSpec((nc, 4 * hh, l), lambda s: (ob + s, 0, 0))]
                 + ([] if init is None else states),
        out_specs=[pl.BlockSpec((seq_len, hv), lambda s: (s, 0))] + states,
        out_shape=[jax.ShapeDtypeStruct((n_seq * seq_len, hv), BF16),
                   jax.ShapeDtypeStruct((n_seq, 2, hh, dk, dv), F32),
                   jax.ShapeDtypeStruct((n_seq, 2, hh, 1, dk), F32),
                   jax.ShapeDtypeStruct((n_seq, 2, hh, 1, 1), F32)],
        scratch_shapes=[pltpu.VMEM((seq_len, hv), F32), pltpu.VMEM((seq_len, hv), F32)],
        compiler_params=_params("arbitrary"),
        name="mlstm_scan",
    )(qkv, qkv, qkv, gcol, grow, *(() if init is None else init))


def _mla_proj_kernel(x_ref, g_ref, sh_ref, sc_ref, win_ref, qg_ref, kvg_ref, wqb_ref, cos_ref,
                     sin_ref, q_ref, ckv_ref, kpe_ref, *, q_lora, kv_lora, rope, n_heads):
    h = _modulated(x_ref[...], g_ref[...], sh_ref[...], sc_ref[...])
    proj = _bdot_nt(h, win_ref[...])
    q_lat = proj[:, :q_lora]
    ckv_ref[...] = _rms(proj[:, q_lora:q_lora + kv_lora], kvg_ref[...])
    cos, sin = cos_ref[...], sin_ref[...]
    kpe = proj[:, q_lora + kv_lora:q_lora + kv_lora + rope]
    kpe_rot = proj[:, q_lora + kv_lora + rope:q_lora + kv_lora + 2 * rope]
    kpe_ref[...] = kpe * cos[:, :rope] + kpe_rot * sin[:, :rope]
    q = _bdot_nt(_rms(q_lat, qg_ref[...]), wqb_ref[...])
    n = n_heads * LANES
    pieces = []
    for hd in range(n_heads):
        c = slice(hd * LANES, (hd + 1) * LANES)
        pieces.append(q[:, c])
        pieces.append(q[:, n:2 * n][:, c] * cos + q[:, 2 * n:][:, c] * sin)
    q_ref[...] = jnp.concatenate(pieces, axis=1).astype(BF16)


def _rot_rows(wt):
    n, k = wt.shape
    quarter = MLA_ROPE // 4
    w4 = wt.reshape(n // (2 * quarter), 2, quarter, k)
    return jnp.concatenate([-w4[:, 1:2], w4[:, 0:1]], axis=1).reshape(n, k)


def _rope_tables(n_ctx, dec_batch, dec_seq):
    quarter = MLA_ROPE // 4
    freq = np.power(np.float32(ROPE_BASE), -np.arange(quarter, dtype=np.float32) / np.float32(quarter))
    pos = np.arange(dec_seq)
    ang_r = (pos // GRID_W).astype(np.float32)[:, None] * freq[None, :]
    ang_c = (pos % GRID_W).astype(np.float32)[:, None] * freq[None, :]
    ang = np.concatenate([ang_r, ang_r, ang_c, ang_c], axis=1).astype(np.float32)
    cos = np.concatenate([np.ones((n_ctx, MLA_ROPE), np.float32)] + [np.cos(ang)] * dec_batch, axis=0)
    sin = np.concatenate([np.zeros((n_ctx, MLA_ROPE), np.float32)] + [np.sin(ang)] * dec_batch, axis=0)
    reps = LANES // MLA_ROPE
    return (jnp.asarray(np.tile(cos, (1, reps)), F32), jnp.asarray(np.tile(sin, (1, reps)), F32))


def mla_proj(x, g, shift, scale, w_in, q_g, kv_g, w_qb, cos, sin, n_ctx, dec_seq, tm=512):
    t, d = x.shape
    hh, nope, rope = MLA_HEADS, MLA_NOPE, MLA_ROPE
    q_lora = q_g.shape[1]
    kv_lora = kv_g.shape[1]
    w_in_t = w_in.T
    w_in_ext = jnp.concatenate([w_in_t, _rot_rows(w_in_t[q_lora + kv_lora:])], axis=0)
    w3 = w_qb.T.reshape(hh, nope + rope, q_lora)
    w_qn = w3[:, :nope].reshape(hh * nope, q_lora)
    w_qp = w3[:, nope:].reshape(hh * rope, q_lora)
    assert nope == LANES and rope <= LANES
    head_pad = lambda w: jnp.pad(w.reshape(hh, rope, q_lora),
                                 ((0, 0), (0, LANES - rope), (0, 0))).reshape(hh * LANES, q_lora)
    w_qb_ext = jnp.concatenate([w_qn, head_pad(w_qp), head_pad(_rot_rows(w_qp))], axis=0)
    cidx = _cvec_index(n_ctx, dec_seq, tm)
    mod_spec = pl.BlockSpec((None, 1, d), lambda i: (cidx(i), 0, 0))
    full = lambda a: pl.BlockSpec(a.shape, lambda i: (0,) * a.ndim)
    kern = functools.partial(_mla_proj_kernel, q_lora=q_lora, kv_lora=kv_lora, rope=rope, n_heads=hh)
    return pl.pallas_call(
        kern,
        grid=(t // tm,),
        in_specs=[pl.BlockSpec((tm, d), lambda i: (i, 0)), full(g), mod_spec, mod_spec,
                  full(w_in_ext), full(q_g), full(kv_g), full(w_qb_ext),
                  pl.BlockSpec((tm, LANES), lambda i: (i, 0)),
                  pl.BlockSpec((tm, LANES), lambda i: (i, 0))],
        out_specs=[pl.BlockSpec((tm, 2 * hh * LANES), lambda i: (i, 0)),
                   pl.BlockSpec((tm, kv_lora), lambda i: (i, 0)),
                   pl.BlockSpec((tm, rope), lambda i: (i, 0))],
        out_shape=[jax.ShapeDtypeStruct((t, 2 * hh * LANES), BF16),
                   jax.ShapeDtypeStruct((t, kv_lora), F32),
                   jax.ShapeDtypeStruct((t, rope), F32)],
        compiler_params=_params("arbitrary"),
        name="mla_proj",
    )(x, g, shift, scale, w_in_ext, q_g, kv_g, w_qb_ext, cos, sin)


def _mla_kv_kernel(ckv_ref, kp_ref, w_ref, k_ref, v_ref):
    kv = _bdot(ckv_ref[...], w_ref[...])
    n = v_ref.shape[1]
    kp = kp_ref[...]
    pieces = []
    for hd in range(n // LANES):
        pieces += [kv[:, hd * LANES:(hd + 1) * LANES].astype(BF16), kp]
    k_ref[...] = jnp.concatenate(pieces, axis=1)
    v_ref[...] = kv[:, n:].astype(BF16)


def mla_kv(ckv_all, kp_pad, w_kvb, tm=512):
    r, kv_lora = ckv_all.shape
    hh, nope, vd = MLA_HEADS, MLA_NOPE, MLA_V
    w3 = w_kvb.reshape(kv_lora, hh, nope + vd)
    w_perm = jnp.concatenate([w3[:, :, :nope].reshape(kv_lora, hh * nope),
                              w3[:, :, nope:].reshape(kv_lora, hh * vd)], axis=1)
    return pl.pallas_call(
        _mla_kv_kernel,
        grid=(r // tm,),
        in_specs=[pl.BlockSpec((tm, kv_lora), lambda i: (i, 0)),
                  pl.BlockSpec((tm, LANES), lambda i: (i, 0)),
                  pl.BlockSpec(w_perm.shape, lambda i: (0, 0))],
        out_specs=[pl.BlockSpec((tm, 2 * hh * LANES), lambda i: (i, 0)),
                   pl.BlockSpec((tm, hh * vd), lambda i: (i, 0))],
        out_shape=[jax.ShapeDtypeStruct((r, 2 * hh * LANES), BF16),
                   jax.ShapeDtypeStruct((r, hh * vd), BF16)],
        compiler_params=_params("arbitrary"),
        name="mla_kv",
    )(ckv_all, kp_pad, w_perm)


def _attn_kernel(q_ref, k_ref, v_ref, o_ref, *, scale):
    c = scale * np.log2(np.e)
    for h in range(MLA_HEADS):
        hk = slice(2 * h * LANES, 2 * (h + 1) * LANES)
        s = _bdot_nt(q_ref[:, hk], k_ref[:, hk])
        e = jnp.exp2((s - jnp.max(s, axis=-1, keepdims=True)) * c)
        o = _bdot(e, v_ref[:, h * MLA_V:(h + 1) * MLA_V]) / jnp.sum(e, axis=-1, keepdims=True)
        o_ref[:, h * MLA_V:(h + 1) * MLA_V] = o.astype(BF16)


def mla_attention(q, k, v, q_row_off, k_row_off, n_seq, q_len, k_len):
    tq = min(ATTN_Q_BLOCK, q_len)
    qb = q_len // tq
    q0 = q_row_off // tq
    k0 = k_row_off // k_len
    dq, dv = q.shape[1], v.shape[1]
    kern = functools.partial(_attn_kernel, scale=(MLA_NOPE + MLA_ROPE) ** -0.5)
    return pl.pallas_call(
        kern,
        grid=(n_seq, qb),
        in_specs=[pl.BlockSpec((tq, dq), lambda s, j: (q0 + s * qb + j, 0)),
                  pl.BlockSpec((k_len, dq), lambda s, j: (k0 + s, 0)),
                  pl.BlockSpec((k_len, dv), lambda s, j: (k0 + s, 0))],
        out_specs=pl.BlockSpec((tq, dv), lambda s, j: (s * qb + j, 0)),
        out_shape=jax.ShapeDtypeStruct((n_seq * q_len, dv), BF16),
        compiler_params=_params("arbitrary", "arbitrary"),
        name="mla_attention",
    )(q, k, v)


def kernel(x_prompt, x_sample, state_mlstm_C, state_mlstm_n, state_mlstm_m, cache_mla_ckv,
           cache_mla_kpe, c, c_ctx, w_ada, b_ada, norm_mix, norm_ffn, norm_final, w_pool,
           pool_scale, w_mlstm_in, b_mlstm_gate, mlstm_head_g, w_mlstm_out, w_mla_in, mla_q_g,
           mla_kv_g, w_mla_qb, w_mla_kvb, w_mla_out, w_router, b_router, w_exp_gate, w_exp_up,
           w_exp_down):
    batch, seq, d = x_prompt.shape
    dec_batch, dec_seq, _ = x_sample.shape
    depth = w_ada.shape[0]
    n_ctx = batch * seq
    n_lat = dec_batch * dec_seq
    hh = MLSTM_HEADS
    past = cache_mla_ckv.shape[2]

    x = (x_prompt.reshape(n_ctx, d), x_sample.reshape(n_lat, d))

    n_cv = 1 + dec_batch
    cvecs = jnp.concatenate([c_ctx[None, :], c, jnp.zeros((SUBLANES - n_cv % SUBLANES, d), F32)], axis=0)
    mod = ada_mod_all(cvecs, w_ada, b_ada).reshape(depth, cvecs.shape[0], 6, 1, d)

    w_router_pad = jnp.pad(w_router, ((0, 0), (0, LANES - N_EXPERTS)))
    w_router_hi = w_router_pad.astype(BF16)
    w_router_lo = (w_router_pad - w_router_hi.astype(F32)).astype(BF16)
    w_router_split = jnp.concatenate([w_router_hi, w_router_lo], axis=1)
    b_router_col = b_router.reshape(N_EXPERTS, 1)
    g_final = norm_final.reshape(1, d)
    row = lambda a: a.reshape(1, -1)

    outs = {}
    for i in range(depth):
        kind, j = i % 3, i // 3
        m = [mod[i, :n_cv, k] for k in range(6)]
        g_mix = row(norm_mix[i])
        if kind == 0:
            x = pool_layer(x, g_mix, m[0], m[1], m[2], w_pool[j], row(pool_scale[j]),
                           n_ctx, seq, dec_seq)
        elif kind == 1:
            qkv, o_gate, gcol, grow = mlstm_proj(x, g_mix, m[0], m[1], w_mlstm_in, j,
                                                 b_mlstm_gate[j], n_ctx, dec_seq)
            t = n_ctx + n_lat
            l = MLSTM_CHUNK
            grow_c = grow.reshape(4 * hh, t // l, l).transpose(1, 0, 2)
            hs_c, c_new, n_new, m_new = mlstm_scan(qkv, gcol, grow_c, None, 0, batch, seq)
            init = (state_mlstm_C[:, j], state_mlstm_n[:, j][:, :, :, None, :],
                    state_mlstm_m[:, j][:, :, :, None, None])
            hs_l, _, _, _ = mlstm_scan(qkv, gcol, grow_c, init, n_ctx, dec_batch, dec_seq)
            outs["C"] = c_new[:, None]
            outs["n"] = n_new[:, None, :, :, 0, :]
            outs["m"] = m_new[:, None, :, :, 0, 0]
            x = mlstm_out(x, hs_c, hs_l, o_gate, row(mlstm_head_g[j]), w_mlstm_out[j], m[2],
                          n_ctx, dec_seq)
        else:
            cos, sin = _rope_tables(n_ctx, dec_batch, dec_seq)
            q_cat, ckv, kpe = mla_proj(x, g_mix, m[0], m[1], w_mla_in[j], row(mla_q_g[j]),
                                       row(mla_kv_g[j]), w_mla_qb[j], cos, sin, n_ctx, dec_seq)
            lat_parts_c, lat_parts_p = [], []
            for b in range(dec_batch):
                lo = n_ctx + b * dec_seq
                lat_parts_c += [cache_mla_ckv[b, j], ckv[lo:lo + dec_seq]]
                lat_parts_p += [cache_mla_kpe[b, j], kpe[lo:lo + dec_seq]]
            ckv_all = jnp.concatenate(lat_parts_c + [ckv[:n_ctx]], axis=0)
            kp_all = jnp.concatenate(lat_parts_p + [kpe[:n_ctx]], axis=0).astype(BF16)
            kp_pad = jnp.pad(kp_all, ((0, 0), (0, LANES - kp_all.shape[1])))
            k_cat, v = mla_kv(ckv_all, kp_pad, w_mla_kvb[j])
            k_lat = past + dec_seq
            o_c = mla_attention(q_cat, k_cat, v, 0, dec_batch * k_lat, batch, seq, seq)
            o_l = mla_attention(q_cat, k_cat, v, n_ctx, 0, dec_batch, dec_seq, k_lat)
            outs["ckv"] = ckv[:n_ctx].reshape(batch, 1, seq, -1)
            outs["kpe"] = kpe[:n_ctx].reshape(batch, 1, seq, -1)
            x = resid_proj(x, o_c, o_l, w_mla_out[j], m[2], n_ctx, dec_seq)
        x = moe_layer(x, row(norm_ffn[i]), m[3], m[4], m[5], w_router_split, b_router_col,
                      w_exp_gate, w_exp_up, w_exp_down, i, g_final, n_ctx, dec_seq,
                      final_norm=(i == depth - 1), split_output=(i == depth - 1))

    y_prompt = x[0].reshape(batch, seq, d)
    y_sample = x[1].reshape(dec_batch, dec_seq, d)
    return (y_prompt, y_sample, outs["C"], outs["n"], outs["m"], outs["ckv"], outs["kpe"])
```

```python
import functools

import numpy as np
import jax
import jax.numpy as jnp
from jax import lax
from jax.experimental import pallas as pl
from jax.experimental.pallas import tpu as pltpu

F32 = jnp.float32
BF16 = jnp.bfloat16

NORM_EPS = 1e-6
GRID_W = 64
POOL_WINDOWS = (2, 4, 8, 16)
MLSTM_HEADS = 4
MLSTM_CHUNK = 256
MLA_HEADS = 8
MLA_NOPE = 128
MLA_ROPE = 64
MLA_V = 128
ROPE_BASE = 10000.0
N_EXPERTS = 16
N_EXPERT_GROUPS = 4
EXPERTS_PER_GROUP = N_EXPERTS // N_EXPERT_GROUPS

LANES = 128
SUBLANES = 8
VMEM_LIMIT = 56 * 1024 * 1024
POOL_TILE = 256
POOL_HALO = 8
ATTN_Q_BLOCK = 256
MOE_TILE = 512
MOE_SEG_ALIGN = 16
MOE_ROW_BLOCK = 160
MOE_WIDE_BLOCK = 224


def _params(*sem):
    return pltpu.CompilerParams(dimension_semantics=sem, vmem_limit_bytes=VMEM_LIMIT)


def _rms(x, g):
    return x * lax.rsqrt(jnp.mean(x * x, axis=-1, keepdims=True) + NORM_EPS) * g


def _modulated(x, g, shift, scale):
    return _rms(x, g) * (1.0 + scale) + shift


def _silu(x):
    return x * jax.nn.sigmoid(x)


def _bdot(a, b):
    return jnp.dot(a.astype(BF16), b.astype(BF16), preferred_element_type=F32)


def _bdot_nt(a, b):
    return lax.dot_general(a.astype(BF16), b.astype(BF16), (((1,), (1,)), ((), ())),
                           preferred_element_type=F32)


def _bdot_tn(a, b):
    return lax.dot_general(a.astype(BF16), b.astype(BF16), (((0,), (0,)), ((), ())),
                           preferred_element_type=F32)


def _cvec_index(n_ctx, dec_seq, tm):
    def idx(i):
        r = i * tm
        return jnp.where(r < n_ctx, 0, (r - n_ctx) // dec_seq + 1)
    return idx


def _ada_kernel(c_ref, w_ref, b_ref, o_ref):
    o_ref[...] = _bdot(_silu(c_ref[...]), w_ref[...]) + b_ref[...]


def ada_mod_all(cvecs, w_ada, b_ada, tn=1536):
    depth, d, n6 = w_ada.shape
    rows = cvecs.shape[0]
    return pl.pallas_call(
        _ada_kernel,
        grid=(depth, n6 // tn),
        in_specs=[pl.BlockSpec((rows, d), lambda l, n: (0, 0)),
                  pl.BlockSpec((None, d, tn), lambda l, n: (l, 0, n)),
                  pl.BlockSpec((None, 1, tn), lambda l, n: (l, 0, n))],
        out_specs=pl.BlockSpec((None, rows, tn), lambda l, n: (l, 0, n)),
        out_shape=jax.ShapeDtypeStruct((depth, rows, n6), F32),
        compiler_params=_params("arbitrary", "arbitrary"),
        name="ada_mod",
    )(cvecs, w_ada, b_ada.reshape(depth, 1, n6))


def _pool_kernel(*refs, n_ctx_tiles, ctx_seq_tiles, lat_seq_tiles, split_input):
    n_x = 6 if split_input else 3
    x_refs = refs[:n_x]
    g_ref, sh_ref, sc_ref, gt_ref, wp_ref, ps_ref, o_ref, buf_ref, *lvl_refs = refs[n_x:]
    i = pl.program_id(0)
    is_ctx = i < n_ctx_tiles
    j = jnp.where(is_ctx, i % ctx_seq_tiles, (i - n_ctx_tiles) % lat_seq_tiles)
    nt = jnp.where(is_ctx, ctx_seq_tiles, lat_seq_tiles)
    g, sh, sc = g_ref[...], sh_ref[...], sc_ref[...]
    tp, hl = POOL_TILE, POOL_HALO
    gw = o_ref.shape[1] // len(POOL_WINDOWS)

    def fill(xc_ref, xp_ref, xn_ref):
        buf_ref[pl.ds(0, hl), :] = jnp.where(j == 0, 0.0, _modulated(xp_ref[...], g, sh, sc))
        buf_ref[pl.ds(hl, tp), :] = _modulated(xc_ref[...], g, sh, sc)
        buf_ref[pl.ds(hl + tp, hl), :] = jnp.where(j == nt - 1, 0.0,
                                                    _modulated(xn_ref[...], g, sh, sc))
        o_ref[...] = xc_ref[...]

    if split_input:
        pl.when(is_ctx)(lambda: fill(*x_refs[:3]))
        pl.when(jnp.logical_not(is_ctx))(lambda: fill(*x_refs[3:]))
    else:
        fill(*x_refs)

    rows = tp + 2 * hl
    buf_ref[pl.ds(rows, hl), :] = jnp.zeros((hl, buf_ref.shape[1]), F32)
    for ref in lvl_refs:
        ref[pl.ds(rows, hl), :] = jnp.zeros((hl, gw), F32)
    pos = j * tp + lax.broadcasted_iota(jnp.int32, (tp, 1), 0)
    seq_len = nt * tp
    for gi, w in enumerate(POOL_WINDOWS):
        cols = pl.ds(gi * gw, gw)
        src, src_cols = buf_ref, cols
        for lvl in range(1, w.bit_length() - 1):
            step = 1 << (lvl - 1)
            dst = lvl_refs[lvl - 1]
            dst[pl.ds(0, rows), :] = src[pl.ds(0, rows), src_cols] + src[pl.ds(step, rows), src_cols]
            src, src_cols = dst, pl.ds(0, gw)
        acc = src[pl.ds(hl - w // 2, tp), src_cols] + src[pl.ds(hl, tp), src_cols]
        cnt = jnp.minimum(pos + w // 2, seq_len) - jnp.maximum(pos - w // 2, 0)
        pooled = acc / cnt.astype(F32) - buf_ref[pl.ds(hl, tp), cols]
        y = _bdot(pooled, wp_ref[gi]) * ps_ref[:, cols]
        o_ref[:, cols] = o_ref[:, cols] + gt_ref[:, cols] * y


def _halo_specs(tp, hl, d, tile_off, n_rows):
    hb = tp // hl
    last_tile, last_hblk = n_rows // tp - 1, n_rows // hl - 1
    tile = lambda i: jnp.clip(i - tile_off, 0, last_tile)
    return [pl.BlockSpec((tp, d), lambda i: (tile(i), 0)),
            pl.BlockSpec((hl, d), lambda i: (jnp.clip(tile(i) * hb - 1, 0, last_hblk), 0)),
            pl.BlockSpec((hl, d), lambda i: (jnp.clip((tile(i) + 1) * hb, 0, last_hblk), 0))]


def pool_layer(xs, g, shift, scale, gate, w_pool, pool_scale, n_ctx, seq, dec_seq):
    split = isinstance(xs, tuple)
    tp, hl = POOL_TILE, POOL_HALO
    if split:
        d = xs[0].shape[1]
        t = xs[0].shape[0] + xs[1].shape[0]
        x_specs = (_halo_specs(tp, hl, d, 0, xs[0].shape[0])
                   + _halo_specs(tp, hl, d, n_ctx // tp, xs[1].shape[0]))
        x_args = (xs[0],) * 3 + (xs[1],) * 3
    else:
        t, d = xs.shape
        x_specs = _halo_specs(tp, hl, d, 0, t)
        x_args = (xs,) * 3
    cidx = _cvec_index(n_ctx, dec_seq, tp)
    mod_spec = pl.BlockSpec((None, 1, d), lambda i: (cidx(i), 0, 0))
    row_spec = pl.BlockSpec((1, d), lambda i: (0, 0))
    assert all(w & (w - 1) == 0 and w // 2 <= hl for w in POOL_WINDOWS)
    n_levels = max(POOL_WINDOWS).bit_length() - 2
    kern = functools.partial(_pool_kernel, n_ctx_tiles=n_ctx // tp, ctx_seq_tiles=seq // tp,
                             lat_seq_tiles=dec_seq // tp, split_input=split)
    return pl.pallas_call(
        kern,
        grid=(t // tp,),
        in_specs=x_specs + [row_spec, mod_spec, mod_spec, mod_spec,
                            pl.BlockSpec(w_pool.shape, lambda i: (0, 0, 0)), row_spec],
        out_specs=pl.BlockSpec((tp, d), lambda i: (i, 0)),
        out_shape=jax.ShapeDtypeStruct((t, d), F32),
        scratch_shapes=[pltpu.VMEM((tp + 3 * hl, d), F32)]
        + [pltpu.VMEM((tp + 3 * hl, d // len(POOL_WINDOWS)), F32)] * n_levels,
        compiler_params=_params("arbitrary"),
        name="pool_mixer",
    )(*x_args, g, shift, scale, gate, w_pool, pool_scale)


def _route(sel, scores):
    e, tm = sel.shape
    row = lax.broadcasted_iota(jnp.int32, (e, tm), 0)
    best = jnp.zeros((1, tm), jnp.int32)
    best_sc = None
    for gidx in range(N_EXPERT_GROUPS):
        r = [sel[gidx * EXPERTS_PER_GROUP + k:gidx * EXPERTS_PER_GROUP + k + 1, :]
             for k in range(EXPERTS_PER_GROUP)]
        top2 = None
        for a in range(EXPERTS_PER_GROUP):
            for b in range(a + 1, EXPERTS_PER_GROUP):
                s = r[a] + r[b]
                top2 = s if top2 is None else jnp.maximum(top2, s)
        if best_sc is None:
            best_sc = top2
        else:
            better = top2 > best_sc
            best = jnp.where(better, gidx, best)
            best_sc = jnp.where(better, top2, best_sc)
    neg = -jnp.inf
    masked = jnp.where(row // EXPERTS_PER_GROUP == best, sel, neg)
    m1 = jnp.max(masked, axis=0, keepdims=True)
    i1 = jnp.min(jnp.where(masked == m1, row, e), axis=0, keepdims=True)
    masked2 = jnp.where(row == i1, neg, masked)
    m2 = jnp.max(masked2, axis=0, keepdims=True)
    i2 = jnp.min(jnp.where(masked2 == m2, row, e), axis=0, keepdims=True)
    hot1 = row == i1
    hot2 = row == i2
    w1 = jnp.sum(jnp.where(hot1, scores, 0.0), axis=0, keepdims=True)
    w2 = jnp.sum(jnp.where(hot2, scores, 0.0), axis=0, keepdims=True)
    tot = w1 + w2
    return best, jnp.where(hot1, w1 / tot, 0.0) + jnp.where(hot2, w2 / tot, 0.0)


def _split_bf16(a, parts):
    out = []
    for _ in range(parts):
        p = a.astype(BF16)
        out.append(p)
        a = a - p.astype(F32)
    return out


def _pad_rows(a, rows):
    return jnp.concatenate([a, jnp.zeros((rows - a.shape[0], a.shape[1]), a.dtype)], axis=0)


def _moe_kernel(x_ref, g_ref, sh_ref, sc_ref, gt_ref, wr_ref, br_ref, wgf_ref, wuf_ref, wdf_ref,
                gf_ref, *rest, final_norm, n_ctx_tiles, n_experts):
    out_refs, (wgu_ref, wd_ref), scratch = rest[:-5], rest[-5:-3], rest[-3:]
    i = pl.program_id(0)
    f = wgf_ref.shape[1]
    for e in range(n_experts):
        gi, k = divmod(e, EXPERTS_PER_GROUP)

        @pl.when(i == e)
        def _(gi=gi, k=k):
            wgu_ref[gi, :, pl.ds(2 * k * f, f)] = wgf_ref[...].astype(BF16)
            wgu_ref[gi, :, pl.ds((2 * k + 1) * f, f)] = wuf_ref[...].astype(BF16)
            wd_ref[gi, pl.ds(k * f, f), :] = wdf_ref[...].astype(BF16)

    @pl.when(i >= n_experts)
    def _():
        _moe_tile(x_ref, g_ref, sh_ref, sc_ref, gt_ref, wr_ref, br_ref, wgu_ref, wd_ref,
                  gf_ref, out_refs, scratch, i - n_experts, final_norm, n_ctx_tiles)


def _moe_tile(x_ref, g_ref, sh_ref, sc_ref, gt_ref, wr_ref, br_ref, wgu_ref, wd_ref,
              gf_ref, out_refs, scratch, tile, final_norm, n_ctx_tiles):
    hp_scr, cw_scr, yp_scr = scratch
    tr, d = x_ref.shape
    trp = hp_scr.shape[0]
    ng, eg = N_EXPERT_GROUPS, EXPERTS_PER_GROUP
    x = x_ref[...]
    h = _modulated(x, g_ref[...], sh_ref[...], sc_ref[...])
    hb = h.astype(BF16)

    h_lo = (h - hb.astype(F32)).astype(BF16)
    wr = wr_ref[...]
    lg = jnp.dot(hb, wr, preferred_element_type=F32)
    logits = lg[:, :LANES] + lg[:, LANES:] + jnp.dot(h_lo, wr[:, :LANES], preferred_element_type=F32)
    scores = jax.nn.sigmoid(logits.T[:N_EXPERTS, :])
    best, comb_t = _route(scores + br_ref[...], scores)

    grp = lax.broadcasted_iota(jnp.int32, (SUBLANES, tr), 0)
    hot_t = (grp == best).astype(F32)
    cw_t = hot_t[0:1, :] * comb_t[0:eg, :]
    for gi in range(1, ng):
        cw_t = cw_t + hot_t[gi:gi + 1, :] * comb_t[gi * eg:(gi + 1) * eg, :]
    cw_c = _pad_rows(cw_t, LANES).T

    ia = lax.broadcasted_iota(jnp.int32, (tr, tr), 0)
    ib = lax.broadcasted_iota(jnp.int32, (tr, tr), 1)
    before = jnp.where(ia < ib, 1.0, 0.0).astype(BF16)
    rank_t = jnp.dot(hot_t.astype(BF16), before, preferred_element_type=F32)

    starts, counts = [], []
    off = jnp.int32(0)
    for gi in range(ng):
        n = jnp.sum(hot_t[gi:gi + 1, :]).astype(jnp.int32)
        n = ((n + MOE_SEG_ALIGN - 1) // MOE_SEG_ALIGN) * MOE_SEG_ALIGN
        starts.append(off)
        counts.append(n)
        off = off + n

    pos_t = hot_t[0:1, :] * (rank_t[0:1, :] + starts[0].astype(F32))
    for gi in range(1, ng):
        pos_t = pos_t + hot_t[gi:gi + 1, :] * (rank_t[gi:gi + 1, :] + starts[gi].astype(F32))
    pos_c = _pad_rows(pos_t, LANES).T[:, 0:1]
    used = tr + ng * MOE_SEG_ALIGN
    perm = jnp.where(lax.broadcasted_iota(jnp.int32, (used, tr), 0) == pos_t.astype(jnp.int32),
                     1.0, 0.0).astype(BF16)
    perm_t = jnp.where(lax.broadcasted_iota(jnp.int32, (tr, used), 1) == pos_c.astype(jnp.int32),
                       1.0, 0.0).astype(BF16)

    hp_scr[pl.ds(0, used), :] = jnp.dot(perm, hb, preferred_element_type=F32).astype(BF16)
    hp_scr[pl.ds(used, trp - used), :] = jnp.zeros((trp - used, d), BF16)
    cw_pair = jnp.dot(perm, jnp.concatenate(_split_bf16(cw_c, 2), axis=1), preferred_element_type=F32)
    cw_scr[pl.ds(0, used), :] = cw_pair[:, :LANES] + cw_pair[:, LANES:]
    cw_scr[pl.ds(used, trp - used), :] = jnp.zeros((trp - used, LANES), F32)
    yp_scr[...] = jnp.zeros_like(yp_scr)

    f = wd_ref.shape[1] // eg

    def expert_block(gi, row0, n_rows):
        rows = pl.ds(pl.multiple_of(row0, MOE_SEG_ALIGN), n_rows)
        cwb = cw_scr[rows, :]
        gu = jnp.dot(hp_scr[rows, :], wgu_ref[gi], preferred_element_type=F32)
        hid = [(_silu(gu[:, 2 * k * f:(2 * k + 1) * f]) * gu[:, (2 * k + 1) * f:(2 * k + 2) * f]
                * cwb[:, k:k + 1]).astype(BF16) for k in range(eg)]
        yp_scr[rows, :] = jnp.dot(jnp.concatenate(hid, axis=1), wd_ref[gi],
                                  preferred_element_type=F32)

    sb, wide = MOE_ROW_BLOCK, MOE_WIDE_BLOCK
    for gi in range(ng):
        use_wide = jnp.logical_and(counts[gi] > sb, counts[gi] <= wide)
        pl.when(use_wide)(functools.partial(expert_block, gi, starts[gi], wide))
        n_blocks = jnp.where(use_wide, 0, (counts[gi] + sb - 1) // sb)

        def block(b, carry, gi=gi):
            expert_block(gi, starts[gi] + b * sb, sb)
            return carry
        lax.fori_loop(0, n_blocks, block, 0)

    y_hi, y_lo = _split_bf16(yp_scr[pl.ds(0, used), :], 2)
    moe = (jnp.dot(perm_t, y_hi, preferred_element_type=F32)
           + jnp.dot(perm_t, y_lo, preferred_element_type=F32))
    out = x + gt_ref[...] * moe
    if final_norm:
        out = _rms(out, gf_ref[...])
    if n_ctx_tiles is None:
        out_refs[0][...] = out
    else:
        is_ctx = tile < n_ctx_tiles

        @pl.when(is_ctx)
        def _():
            out_refs[0][...] = out

        @pl.when(jnp.logical_not(is_ctx))
        def _():
            out_refs[1][...] = out


def moe_layer(x, g, shift, scale, gate, w_router_split, b_router_col, wg, wu, wd, layer, g_final,
              n_ctx, dec_seq, final_norm, split_output=False, tm=MOE_TILE):
    t, d = x.shape
    _, n_e, _, f = wg.shape
    trp = tm + N_EXPERT_GROUPS * MOE_SEG_ALIGN + max(MOE_ROW_BLOCK, MOE_WIDE_BLOCK)
    tile = lambda i: jnp.maximum(i - n_e, 0)
    cidx = _cvec_index(n_ctx, dec_seq, tm)
    mod_spec = pl.BlockSpec((None, 1, d), lambda i: (cidx(tile(i)), 0, 0))
    row_spec = pl.BlockSpec((1, d), lambda i: (0, 0))
    expert = lambda i: jnp.minimum(i, n_e - 1)
    nct = n_ctx // tm
    if split_output:
        out_specs = [pl.BlockSpec((tm, d), lambda i: (jnp.minimum(tile(i), nct - 1), 0)),
                     pl.BlockSpec((tm, d), lambda i: (jnp.maximum(tile(i) - nct, 0), 0))]
        out_shape = [jax.ShapeDtypeStruct((n_ctx, d), F32), jax.ShapeDtypeStruct((t - n_ctx, d), F32)]
    else:
        out_specs = pl.BlockSpec((tm, d), lambda i: (tile(i), 0))
        out_shape = jax.ShapeDtypeStruct((t, d), F32)
    return pl.pallas_call(
        functools.partial(_moe_kernel, final_norm=final_norm,
                          n_ctx_tiles=nct if split_output else None, n_experts=n_e),
        grid=(n_e + t // tm,),
        in_specs=[pl.BlockSpec((tm, d), lambda i: (tile(i), 0)),
                  row_spec, mod_spec, mod_spec, mod_spec,
                  pl.BlockSpec(w_router_split.shape, lambda i: (0, 0)),
                  pl.BlockSpec(b_router_col.shape, lambda i: (0, 0)),
                  pl.BlockSpec((None, None, d, f), lambda i: (layer, expert(i), 0, 0)),
                  pl.BlockSpec((None, None, d, f), lambda i: (layer, expert(i), 0, 0)),
                  pl.BlockSpec((None, None, f, d), lambda i: (layer, expert(i), 0, 0)),
                  row_spec],
        out_specs=out_specs,
        out_shape=out_shape,
        scratch_shapes=[pltpu.VMEM((N_EXPERT_GROUPS, d, 2 * EXPERTS_PER_GROUP * f), BF16),
                        pltpu.VMEM((N_EXPERT_GROUPS, EXPERTS_PER_GROUP * f, d), BF16),
                        pltpu.VMEM((trp, d), BF16), pltpu.VMEM((trp, LANES), F32),
                        pltpu.VMEM((trp, d), F32)],
        compiler_params=_params("arbitrary"),
        name="moe",
    )(x, g, shift, scale, gate, w_router_split, b_router_col, wg, wu, wd, g_final)


def _per_stream(n_ctx_tiles, ctx_ref, lat_ref, fn):
    i = pl.program_id(0)

    @pl.when(i < n_ctx_tiles)
    def _():
        fn(ctx_ref[...])

    @pl.when(i >= n_ctx_tiles)
    def _():
        fn(lat_ref[...])


def _stream_specs(tm, k, n_ctx_tiles):
    return [pl.BlockSpec((tm, k), lambda i: (jnp.minimum(i, n_ctx_tiles - 1), 0)),
            pl.BlockSpec((tm, k), lambda i: (jnp.maximum(i - n_ctx_tiles, 0), 0))]


def _resid_proj_kernel(x_ref, ac_ref, al_ref, w_ref, gt_ref, o_ref, *, n_ctx_tiles):
    def run(a):
        o_ref[...] = x_ref[...] + gt_ref[...] * _bdot(a, w_ref[...])
    _per_stream(n_ctx_tiles, ac_ref, al_ref, run)


def _mlstm_out_kernel(x_ref, hc_ref, hl_ref, og_ref, hg_ref, w_ref, gt_ref, o_ref, *, n_ctx_tiles):
    def run(hs):
        a = jax.nn.sigmoid(og_ref[...].astype(F32)) * (hs.astype(F32) * hg_ref[...])
        o_ref[...] = x_ref[...] + gt_ref[...] * _bdot(a, w_ref[...])
    _per_stream(n_ctx_tiles, hc_ref, hl_ref, run)


def resid_proj(x, a_ctx, a_lat, w, gate, n_ctx, dec_seq, tm=1024):
    t, d = x.shape
    k = a_ctx.shape[1]
    cidx = _cvec_index(n_ctx, dec_seq, tm)
    nct = n_ctx // tm
    return pl.pallas_call(
        functools.partial(_resid_proj_kernel, n_ctx_tiles=nct),
        grid=(t // tm,),
        in_specs=[pl.BlockSpec((tm, d), lambda i: (i, 0))] + _stream_specs(tm, k, nct) + [
            pl.BlockSpec((k, d), lambda i: (0, 0)),
            pl.BlockSpec((None, 1, d), lambda i: (cidx(i), 0, 0))],
        out_specs=pl.BlockSpec((tm, d), lambda i: (i, 0)),
        out_shape=jax.ShapeDtypeStruct((t, d), F32),
        compiler_params=_params("arbitrary"),
        name="resid_proj",
    )(x, a_ctx, a_lat, w, gate)


def mlstm_out(x, hs_ctx, hs_lat, o_gate, head_g, w, gate, n_ctx, dec_seq, tm=1024):
    t, d = x.shape
    k = hs_ctx.shape[1]
    cidx = _cvec_index(n_ctx, dec_seq, tm)
    nct = n_ctx // tm
    return pl.pallas_call(
        functools.partial(_mlstm_out_kernel, n_ctx_tiles=nct),
        grid=(t // tm,),
        in_specs=[pl.BlockSpec((tm, d), lambda i: (i, 0))] + _stream_specs(tm, k, nct) + [
            pl.BlockSpec((tm, k), lambda i: (i, 0)),
            pl.BlockSpec((1, k), lambda i: (0, 0)),
            pl.BlockSpec((k, d), lambda i: (0, 0)),
            pl.BlockSpec((None, 1, d), lambda i: (cidx(i), 0, 0))],
        out_specs=pl.BlockSpec((tm, d), lambda i: (i, 0)),
        out_shape=jax.ShapeDtypeStruct((t, d), F32),
        compiler_params=_params("arbitrary"),
        name="mlstm_out",
    )(x, hs_ctx, hs_lat, o_gate, head_g, w, gate)


def _mlstm_proj_kernel(x_ref, g_ref, sh_ref, sc_ref, wqk_ref, wv_ref, wo_ref, wgt_ref, bg_ref,
                       qs_ref, qkv_ref, o_ref, gc_ref, gr_ref):
    h = _modulated(x_ref[...], g_ref[...], sh_ref[...], sc_ref[...]).astype(BF16)
    nqk = wqk_ref.shape[0]
    qkv_ref[:, :nqk] = (_bdot_nt(h, wqk_ref[...]) * qs_ref[...]).astype(BF16)
    qkv_ref[:, nqk:] = _bdot_nt(h, wv_ref[...]).astype(BF16)
    o_ref[...] = _bdot_nt(h, wo_ref[...]).astype(BF16)
    gates = _bdot_nt(h, wgt_ref[...]) + bg_ref[...]
    gc_ref[...] = gates
    gr_ref[...] = gates.T[:gr_ref.shape[0], :]


def mlstm_proj(x, g, shift, scale, w_in_all, layer, b_gate, n_ctx, dec_seq, tm=512):
    t, d = x.shape
    hh = MLSTM_HEADS
    hv = d
    hk = hv // 2
    ng = 4 * hh
    w_t_all = jnp.swapaxes(w_in_all, 1, 2)
    w_g_pad = jnp.pad(w_t_all[layer, 2 * hk + 2 * hv:, :], ((0, LANES - ng), (0, 0)))
    b_pad = jnp.pad(b_gate.reshape(1, ng), ((0, 0), (0, LANES - ng)))
    dk = hk // hh
    q_scale = jnp.concatenate([jnp.full((1, hk), dk ** -0.5, F32), jnp.ones((1, hk), F32)], axis=1)
    cidx = _cvec_index(n_ctx, dec_seq, tm)
    mod_spec = pl.BlockSpec((None, 1, d), lambda i: (cidx(i), 0, 0))
    full = lambda a: pl.BlockSpec(a.shape, lambda i: (0,) * a.ndim)
    assert 2 * hk == hv
    w_col = lambda n: pl.BlockSpec((None, hv, d), lambda i: (layer, n, 0))
    return pl.pallas_call(
        _mlstm_proj_kernel,
        grid=(t // tm,),
        in_specs=[pl.BlockSpec((tm, d), lambda i: (i, 0)), full(g), mod_spec, mod_spec,
                  w_col(0), w_col(1), w_col(2), full(w_g_pad), full(b_pad), full(q_scale)],
        out_specs=[pl.BlockSpec((tm, 2 * hk + hv), lambda i: (i, 0)),
                   pl.BlockSpec((tm, hv), lambda i: (i, 0)),
                   pl.BlockSpec((tm, LANES), lambda i: (i, 0)),
                   pl.BlockSpec((ng, tm), lambda i: (0, i))],
        out_shape=[jax.ShapeDtypeStruct((t, 2 * hk + hv), BF16),
                   jax.ShapeDtypeStruct((t, hv), BF16),
                   jax.ShapeDtypeStruct((t, LANES), F32),
                   jax.ShapeDtypeStruct((ng, t), F32)],
        compiler_params=_params("arbitrary"),
        name="mlstm_proj",
    )(x, g, shift, scale, w_t_all, w_t_all, w_t_all, w_g_pad, b_pad, q_scale)


def _log_sigmoid(x):
    return jnp.minimum(x, 0.0) - jnp.log(1.0 + jnp.exp(-jnp.abs(x)))


def _gate_cumsums(gc, gr, causal_bf, feeds_bf):
    bc = br = None
    for part in _split_bf16(_log_sigmoid(gc), 3):
        t = jnp.dot(causal_bf, part, preferred_element_type=F32)
        bc = t if bc is None else bc + t
    for part in _split_bf16(_log_sigmoid(gr), 3):
        t = jnp.dot(part, feeds_bf, preferred_element_type=F32)
        br = t if br is None else br + t
    return bc, br


def _mlstm_chunk(q, k, v, i_col, b_col, i_row, b_row, c_st, n_st, m_st, causal, rev):
    l = q.shape[0]
    zero_state = c_st is None
    if zero_state:
        m_st = 0.0
    g_row = i_row - b_row
    log_w = jnp.where(causal, g_row, -jnp.inf)
    c_t = jnp.maximum(m_st, jnp.max(log_w, axis=1, keepdims=True))
    dw = jnp.exp(log_w - c_t)
    a = _bdot_nt(q, k) * dw
    a_hi, a_lo = _split_bf16(a, 2)
    ones = jnp.ones((l, LANES), BF16)
    den = (jnp.dot(a_hi, ones, preferred_element_type=F32)
           + jnp.dot(a_lo, ones, preferred_element_type=F32))[:, 0:1]
    if not zero_state:
        sw = jnp.exp(m_st - c_t)
        n_hi, n_lo = _split_bf16(jnp.broadcast_to(n_st, (LANES, n_st.shape[1])), 2)
        den = den + sw * (_bdot_nt(q, n_hi) + _bdot_nt(q, n_lo))[:, 0:1]
    inv = 1.0 / jnp.maximum(jnp.abs(den), jnp.exp(-(b_col + c_t)))
    h = inv * jnp.dot(a_hi, v, preferred_element_type=F32)
    if not zero_state:
        h = h + (sw * inv) * _bdot(q, c_st)
    b_last = b_col[0:1, :] if rev else b_col[l - 1:l, :]
    log_k = b_last - b_col + i_col
    m_new = jnp.maximum(b_last + m_st, jnp.max(log_k, axis=0, keepdims=True))
    kw = jnp.exp(log_k - m_new)
    kwk = kw * k.astype(F32)
    c_new = _bdot_tn(kwk, v)
    n_new = jnp.sum(kwk, axis=0, keepdims=True)
    if not zero_state:
        decay = jnp.exp(b_last + m_st - m_new)
        c_new = decay * c_st + c_new
        n_new = decay * n_st + n_new
    return h, c_new, n_new, m_new


def _mlstm_scan_kernel(*refs, n_chunks, zero_init):
    single = zero_init and n_chunks == 1
    if zero_init:
        q_ref, k_ref, v_ref, gc_ref, gr_ref, hs_ref, c_ref, n_ref, m_ref, hf_scr, hb_scr = refs
        if not single:
            c_ref[...] = jnp.zeros_like(c_ref)
            n_ref[...] = jnp.zeros_like(n_ref)
            m_ref[...] = jnp.zeros_like(m_ref)
    else:
        (q_ref, k_ref, v_ref, gc_ref, gr_ref, c0_ref, n0_ref, m0_ref,
         hs_ref, c_ref, n_ref, m_ref, hf_scr, hb_scr) = refs
        c_ref[...] = c0_ref[...]
        n_ref[...] = n0_ref[...]
        m_ref[...] = m0_ref[...]
    l = MLSTM_CHUNK
    hh = MLSTM_HEADS
    dk = q_ref.shape[1] // hh
    dv = v_ref.shape[1] // hh

    tt = lax.broadcasted_iota(jnp.int32, (l, l), 0)
    ss = lax.broadcasted_iota(jnp.int32, (l, l), 1)
    masks = (ss <= tt, ss >= tt)
    masks_bf = tuple(jnp.where(m, 1.0, 0.0).astype(BF16) for m in masks)

    def body(c, carry):
        cr = n_chunks - 1 - c
        start = (lambda j: j * l) if single else (lambda j: pl.multiple_of(j * l, l))
        rows = (pl.ds(start(c), l), pl.ds(start(cr), l))
        grs = (gr_ref[c], gr_ref[cr])
        outs, states = ([], []), []
        for d in range(2):
            q, k, v, gc, gr = q_ref[rows[d], :], k_ref[rows[d], :], v_ref[rows[d], :], gc_ref[rows[d], :], grs[d]
            bc, br = _gate_cumsums(gc, gr, masks_bf[d], masks_bf[1 - d])
            for h in range(hh):
                ci, cf = 2 * d * hh + h, (2 * d + 1) * hh + h
                state = (None,) * 3 if single else (c_ref[d, h], n_ref[d, h], m_ref[d, h])
                o, *st = _mlstm_chunk(q[:, h * dk:(h + 1) * dk], k[:, h * dk:(h + 1) * dk],
                                      v[:, h * dv:(h + 1) * dv], gc[:, ci:ci + 1], bc[:, cf:cf + 1],
                                      gr[ci:ci + 1, :], br[cf:cf + 1, :], *state, masks[d], d == 1)
                outs[d].append(o)
                states.append((d, h, st))
        hf_scr[rows[0], :] = jnp.concatenate(outs[0], axis=1)
        hb_scr[rows[1], :] = jnp.concatenate(outs[1], axis=1)
        for d, h, (c_new, n_new, m_new) in states:
            c_ref[d, h], n_ref[d, h], m_ref[d, h] = c_new, n_new, m_new
        return carry

    if single:
        body(0, 0)
    else:
        lax.fori_loop(0, n_chunks, body, 0)
    for h in range(hh):
        cols = pl.ds(h * dv, dv)
        hs = hf_scr[:, cols] + hb_scr[:, cols]
        hs_ref[:, cols] = (hs * lax.rsqrt(jnp.mean(hs * hs, axis=-1, keepdims=True)
                                          + NORM_EPS)).astype(BF16)


def mlstm_scan(qkv, gcol, grow, init, row_off, n_seq, seq_len):
    hh = MLSTM_HEADS
    hv = qkv.shape[1] // 2
    dv = hv // hh
    dk = dv // 2
    l = MLSTM_CHUNK
    nc = seq_len // l
    ob = row_off // seq_len
    kern = functools.partial(_mlstm_scan_kernel, n_chunks=nc, zero_init=init is None)
    st = lambda *tail: pl.BlockSpec((None, 2, hh) + tail, lambda s: (s, 0, 0) + (0,) * len(tail))
    states = [st(dk, dv), st(1, dk), st(1, 1)]
    return pl.pallas_call(
        kern,
        grid=(n_seq,),
        in_specs=[pl.BlockSpec((seq_len, hh * dk), lambda s: (ob + s, 0)),
                  pl.BlockSpec((seq_len, hh * dk), lambda s: (ob + s, 1)),
                  pl.BlockSpec((seq_len, hv), lambda s: (ob + s, 1)),
                  pl.BlockSpec((seq_len, LANES), lambda s: (ob + s, 0)),
                  pl.BlockSpec((nc, 4 * hh, l), lambda s: (ob + s, 0, 0))]
                 + ([] if init is None else states),
        out_specs=[pl.BlockSpec((seq_len, hv), lambda s: (s, 0))] + states,
        out_shape=[jax.ShapeDtypeStruct((n_seq * seq_len, hv), BF16),
                   jax.ShapeDtypeStruct((n_seq, 2, hh, dk, dv), F32),
                   jax.ShapeDtypeStruct((n_seq, 2, hh, 1, dk), F32),
                   jax.ShapeDtypeStruct((n_seq, 2, hh, 1, 1), F32)],
        scratch_shapes=[pltpu.VMEM((seq_len, hv), F32), pltpu.VMEM((seq_len, hv), F32)],
        compiler_params=_params("arbitrary"),
        name="mlstm_scan",
    )(qkv, qkv, qkv, gcol, grow, *(() if init is None else init))


def _mla_proj_kernel(x_ref, g_ref, sh_ref, sc_ref, win_ref, qg_ref, kvg_ref, wqb_ref, cos_ref,
                     sin_ref, q_ref, ckv_ref, kpe_ref, *, q_lora, kv_lora, rope, n_heads):
    h = _modulated(x_ref[...], g_ref[...], sh_ref[...], sc_ref[...])
    proj = _bdot_nt(h, win_ref[...])
    q_lat = proj[:, :q_lora]
    ckv_ref[...] = _rms(proj[:, q_lora:q_lora + kv_lora], kvg_ref[...])
    cos, sin = cos_ref[...], sin_ref[...]
    kpe = proj[:, q_lora + kv_lora:q_lora + kv_lora + rope]
    kpe_rot = proj[:, q_lora + kv_lora + rope:q_lora + kv_lora + 2 * rope]
    kpe_ref[...] = kpe * cos[:, :rope] + kpe_rot * sin[:, :rope]
    q = _bdot_nt(_rms(q_lat, qg_ref[...]), wqb_ref[...])
    n = n_heads * LANES
    pieces = []
    for hd in range(n_heads):
        c = slice(hd * LANES, (hd + 1) * LANES)
        pieces.append(q[:, c])
        pieces.append(q[:, n:2 * n][:, c] * cos + q[:, 2 * n:][:, c] * sin)
    q_ref[...] = jnp.concatenate(pieces, axis=1).astype(BF16)


def _rot_rows(wt):
    n, k = wt.shape
    quarter = MLA_ROPE // 4
    w4 = wt.reshape(n // (2 * quarter), 2, quarter, k)
    return jnp.concatenate([-w4[:, 1:2], w4[:, 0:1]], axis=1).reshape(n, k)


def _rope_tables(n_ctx, dec_batch, dec_seq):
    quarter = MLA_ROPE // 4
    freq = np.power(np.float32(ROPE_BASE), -np.arange(quarter, dtype=np.float32) / np.float32(quarter))
    pos = np.arange(dec_seq)
    ang_r = (pos // GRID_W).astype(np.float32)[:, None] * freq[None, :]
    ang_c = (pos % GRID_W).astype(np.float32)[:, None] * freq[None, :]
    ang = np.concatenate([ang_r, ang_r, ang_c, ang_c], axis=1).astype(np.float32)
    cos = np.concatenate([np.ones((n_ctx, MLA_ROPE), np.float32)] + [np.cos(ang)] * dec_batch, axis=0)
    sin = np.concatenate([np.zeros((n_ctx, MLA_ROPE), np.float32)] + [np.sin(ang)] * dec_batch, axis=0)
    reps = LANES // MLA_ROPE
    return (jnp.asarray(np.tile(cos, (1, reps)), F32), jnp.asarray(np.tile(sin, (1, reps)), F32))


def mla_proj(x, g, shift, scale, w_in, q_g, kv_g, w_qb, cos, sin, n_ctx, dec_seq, tm=512):
    t, d = x.shape
    hh, nope, rope = MLA_HEADS, MLA_NOPE, MLA_ROPE
    q_lora = q_g.shape[1]
    kv_lora = kv_g.shape[1]
    w_in_t = w_in.T
    w_in_ext = jnp.concatenate([w_in_t, _rot_rows(w_in_t[q_lora + kv_lora:])], axis=0)
    w3 = w_qb.T.reshape(hh, nope + rope, q_lora)
    w_qn = w3[:, :nope].reshape(hh * nope, q_lora)
    w_qp = w3[:, nope:].reshape(hh * rope, q_lora)
    assert nope == LANES and rope <= LANES
    head_pad = lambda w: jnp.pad(w.reshape(hh, rope, q_lora),
                                 ((0, 0), (0, LANES - rope), (0, 0))).reshape(hh * LANES, q_lora)
    w_qb_ext = jnp.concatenate([w_qn, head_pad(w_qp), head_pad(_rot_rows(w_qp))], axis=0)
    cidx = _cvec_index(n_ctx, dec_seq, tm)
    mod_spec = pl.BlockSpec((None, 1, d), lambda i: (cidx(i), 0, 0))
    full = lambda a: pl.BlockSpec(a.shape, lambda i: (0,) * a.ndim)
    kern = functools.partial(_mla_proj_kernel, q_lora=q_lora, kv_lora=kv_lora, rope=rope, n_heads=hh)
    return pl.pallas_call(
        kern,
        grid=(t // tm,),
        in_specs=[pl.BlockSpec((tm, d), lambda i: (i, 0)), full(g), mod_spec, mod_spec,
                  full(w_in_ext), full(q_g), full(kv_g), full(w_qb_ext),
                  pl.BlockSpec((tm, LANES), lambda i: (i, 0)),
                  pl.BlockSpec((tm, LANES), lambda i: (i, 0))],
        out_specs=[pl.BlockSpec((tm, 2 * hh * LANES), lambda i: (i, 0)),
                   pl.BlockSpec((tm, kv_lora), lambda i: (i, 0)),
                   pl.BlockSpec((tm, rope), lambda i: (i, 0))],
        out_shape=[jax.ShapeDtypeStruct((t, 2 * hh * LANES), BF16),
                   jax.ShapeDtypeStruct((t, kv_lora), F32),
                   jax.ShapeDtypeStruct((t, rope), F32)],
        compiler_params=_params("arbitrary"),
        name="mla_proj",
    )(x, g, shift, scale, w_in_ext, q_g, kv_g, w_qb_ext, cos, sin)


def _mla_kv_kernel(ckv_ref, kp_ref, w_ref, k_ref, v_ref):
    kv = _bdot(ckv_ref[...], w_ref[...])
    n = v_ref.shape[1]
    kp = kp_ref[...]
    pieces = []
    for hd in range(n // LANES):
        pieces += [kv[:, hd * LANES:(hd + 1) * LANES].astype(BF16), kp]
    k_ref[...] = jnp.concatenate(pieces, axis=1)
    v_ref[...] = kv[:, n:].astype(BF16)


def mla_kv(ckv_all, kp_pad, w_kvb, tm=1024):
    r, kv_lora = ckv_all.shape
    hh, nope, vd = MLA_HEADS, MLA_NOPE, MLA_V
    w3 = w_kvb.reshape(kv_lora, hh, nope + vd)
    w_perm = jnp.concatenate([w3[:, :, :nope].reshape(kv_lora, hh * nope),
                              w3[:, :, nope:].reshape(kv_lora, hh * vd)], axis=1)
    return pl.pallas_call(
        _mla_kv_kernel,
        grid=(r // tm,),
        in_specs=[pl.BlockSpec((tm, kv_lora), lambda i: (i, 0)),
                  pl.BlockSpec((tm, LANES), lambda i: (i, 0)),
                  pl.BlockSpec(w_perm.shape, lambda i: (0, 0))],
        out_specs=[pl.BlockSpec((tm, 2 * hh * LANES), lambda i: (i, 0)),
                   pl.BlockSpec((tm, hh * vd), lambda i: (i, 0))],
        out_shape=[jax.ShapeDtypeStruct((r, 2 * hh * LANES), BF16),
                   jax.ShapeDtypeStruct((r, hh * vd), BF16)],
        compiler_params=_params("arbitrary"),
        name="mla_kv",
    )(ckv_all, kp_pad, w_perm)


def _attn_kernel(q_ref, k_ref, v_ref, o_ref, *, scale):
    c = scale * np.log2(np.e)
    for h in range(MLA_HEADS):
        hk = slice(2 * h * LANES, 2 * (h + 1) * LANES)
        s = _bdot_nt(q_ref[:, hk], k_ref[:, hk])
        e = jnp.exp2((s - jnp.max(s, axis=-1, keepdims=True)) * c)
        o = _bdot(e, v_ref[:, h * MLA_V:(h + 1) * MLA_V]) / jnp.sum(e, axis=-1, keepdims=True)
        o_ref[:, h * MLA_V:(h + 1) * MLA_V] = o.astype(BF16)


def mla_attention(q, k, v, q_row_off, k_row_off, n_seq, q_len, k_len):
    tq = min(ATTN_Q_BLOCK, q_len)
    qb = q_len // tq
    q0 = q_row_off // tq
    k0 = k_row_off // k_len
    dq, dv = q.shape[1], v.shape[1]
    kern = functools.partial(_attn_kernel, scale=(MLA_NOPE + MLA_ROPE) ** -0.5)
    return pl.pallas_call(
        kern,
        grid=(n_seq, qb),
        in_specs=[pl.BlockSpec((tq, dq), lambda s, j: (q0 + s * qb + j, 0)),
                  pl.BlockSpec((k_len, dq), lambda s, j: (k0 + s, 0)),
                  pl.BlockSpec((k_len, dv), lambda s, j: (k0 + s, 0))],
        out_specs=pl.BlockSpec((tq, dv), lambda s, j: (s * qb + j, 0)),
        out_shape=jax.ShapeDtypeStruct((n_seq * q_len, dv), BF16),
        compiler_params=_params("arbitrary", "arbitrary"),
        name="mla_attention",
    )(q, k, v)


def kernel(x_prompt, x_sample, state_mlstm_C, state_mlstm_n, state_mlstm_m, cache_mla_ckv,
           cache_mla_kpe, c, c_ctx, w_ada, b_ada, norm_mix, norm_ffn, norm_final, w_pool,
           pool_scale, w_mlstm_in, b_mlstm_gate, mlstm_head_g, w_mlstm_out, w_mla_in, mla_q_g,
           mla_kv_g, w_mla_qb, w_mla_kvb, w_mla_out, w_router, b_router, w_exp_gate, w_exp_up,
           w_exp_down):
    batch, seq, d = x_prompt.shape
    dec_batch, dec_seq, _ = x_sample.shape
    depth = w_ada.shape[0]
    n_ctx = batch * seq
    n_lat = dec_batch * dec_seq
    hh = MLSTM_HEADS
    past = cache_mla_ckv.shape[2]

    x = (x_prompt.reshape(n_ctx, d), x_sample.reshape(n_lat, d))

    n_cv = 1 + dec_batch
    cvecs = jnp.concatenate([c_ctx[None, :], c, jnp.zeros((SUBLANES - n_cv % SUBLANES, d), F32)], axis=0)
    mod = ada_mod_all(cvecs, w_ada, b_ada).reshape(depth, cvecs.shape[0], 6, 1, d)

    w_router_pad = jnp.pad(w_router, ((0, 0), (0, LANES - N_EXPERTS)))
    w_router_hi = w_router_pad.astype(BF16)
    w_router_lo = (w_router_pad - w_router_hi.astype(F32)).astype(BF16)
    w_router_split = jnp.concatenate([w_router_hi, w_router_lo], axis=1)
    b_router_col = b_router.reshape(N_EXPERTS, 1)
    g_final = norm_final.reshape(1, d)
    row = lambda a: a.reshape(1, -1)

    outs = {}
    for i in range(depth):
        kind, j = i % 3, i // 3
        m = [mod[i, :n_cv, k] for k in range(6)]
        g_mix = row(norm_mix[i])
        if kind == 0:
            x = pool_layer(x, g_mix, m[0], m[1], m[2], w_pool[j], row(pool_scale[j]),
                           n_ctx, seq, dec_seq)
        elif kind == 1:
            qkv, o_gate, gcol, grow = mlstm_proj(x, g_mix, m[0], m[1], w_mlstm_in, j,
                                                 b_mlstm_gate[j], n_ctx, dec_seq)
            t = n_ctx + n_lat
            l = MLSTM_CHUNK
            grow_c = grow.reshape(4 * hh, t // l, l).transpose(1, 0, 2)
            hs_c, c_new, n_new, m_new = mlstm_scan(qkv, gcol, grow_c, None, 0, batch, seq)
            init = (state_mlstm_C[:, j], state_mlstm_n[:, j][:, :, :, None, :],
                    state_mlstm_m[:, j][:, :, :, None, None])
            hs_l, _, _, _ = mlstm_scan(qkv, gcol, grow_c, init, n_ctx, dec_batch, dec_seq)
            outs["C"] = c_new[:, None]
            outs["n"] = n_new[:, None, :, :, 0, :]
            outs["m"] = m_new[:, None, :, :, 0, 0]
            x = mlstm_out(x, hs_c, hs_l, o_gate, row(mlstm_head_g[j]), w_mlstm_out[j], m[2],
                          n_ctx, dec_seq)
        else:
            cos, sin = _rope_tables(n_ctx, dec_batch, dec_seq)
            q_cat, ckv, kpe = mla_proj(x, g_mix, m[0], m[1], w_mla_in[j], row(mla_q_g[j]),
                                       row(mla_kv_g[j]), w_mla_qb[j], cos, sin, n_ctx, dec_seq)
            lat_parts_c, lat_parts_p = [], []
            for b in range(dec_batch):
                lo = n_ctx + b * dec_seq
                lat_parts_c += [cache_mla_ckv[b, j], ckv[lo:lo + dec_seq]]
                lat_parts_p += [cache_mla_kpe[b, j], kpe[lo:lo + dec_seq]]
            ckv_all = jnp.concatenate(lat_parts_c + [ckv[:n_ctx]], axis=0)
            kp_all = jnp.concatenate(lat_parts_p + [kpe[:n_ctx]], axis=0).astype(BF16)
            kp_pad = jnp.pad(kp_all, ((0, 0), (0, LANES - kp_all.shape[1])))
            k_cat, v = mla_kv(ckv_all, kp_pad, w_mla_kvb[j])
            k_lat = past + dec_seq
            o_c = mla_attention(q_cat, k_cat, v, 0, dec_batch * k_lat, batch, seq, seq)
            o_l = mla_attention(q_cat, k_cat, v, n_ctx, 0, dec_batch, dec_seq, k_lat)
            outs["ckv"] = ckv[:n_ctx].reshape(batch, 1, seq, -1)
            outs["kpe"] = kpe[:n_ctx].reshape(batch, 1, seq, -1)
            x = resid_proj(x, o_c, o_l, w_mla_out[j], m[2], n_ctx, dec_seq)
        x = moe_layer(x, row(norm_ffn[i]), m[3], m[4], m[5], w_router_split, b_router_col,
                      w_exp_gate, w_exp_up, w_exp_down, i, g_final, n_ctx, dec_seq,
                      final_norm=(i == depth - 1), split_output=(i == depth - 1))

    y_prompt = x[0].reshape(batch, seq, d)
    y_sample = x[1].reshape(dec_batch, dec_seq, d)
    return (y_prompt, y_sample, outs["C"], outs["n"], outs["m"], outs["ckv"], outs["kpe"])
```

```python
import functools

import numpy as np
import jax
import jax.numpy as jnp
from jax import lax
from jax.experimental import pallas as pl
from jax.experimental.pallas import tpu as pltpu

F32 = jnp.float32
BF16 = jnp.bfloat16

NORM_EPS = 1e-6
GRID_W = 64
POOL_WINDOWS = (2, 4, 8, 16)
MLSTM_HEADS = 4
MLSTM_CHUNK = 256
MLA_HEADS = 8
MLA_NOPE = 128
MLA_ROPE = 64
MLA_V = 128
ROPE_BASE = 10000.0
N_EXPERTS = 16
N_EXPERT_GROUPS = 4
EXPERTS_PER_GROUP = N_EXPERTS // N_EXPERT_GROUPS

LANES = 128
SUBLANES = 8
VMEM_LIMIT = 56 * 1024 * 1024
POOL_TILE = 256
POOL_HALO = 8
ATTN_Q_BLOCK = 256
MOE_TILE = 512
MOE_SEG_ALIGN = 16
MOE_ROW_BLOCK = 160
MOE_WIDE_BLOCK = 224


def _params(*sem):
    return pltpu.CompilerParams(dimension_semantics=sem, vmem_limit_bytes=VMEM_LIMIT)


def _rms(x, g):
    return x * lax.rsqrt(jnp.mean(x * x, axis=-1, keepdims=True) + NORM_EPS) * g


def _modulated(x, g, shift, scale):
    return _rms(x, g) * (1.0 + scale) + shift


def _silu(x):
    return x * jax.nn.sigmoid(x)


def _bdot(a, b):
    return jnp.dot(a.astype(BF16), b.astype(BF16), preferred_element_type=F32)


def _bdot_nt(a, b):
    return lax.dot_general(a.astype(BF16), b.astype(BF16), (((1,), (1,)), ((), ())),
                           preferred_element_type=F32)


def _bdot_tn(a, b):
    return lax.dot_general(a.astype(BF16), b.astype(BF16), (((0,), (0,)), ((), ())),
                           preferred_element_type=F32)


def _cvec_index(n_ctx, dec_seq, tm):
    def idx(i):
        r = i * tm
        return jnp.where(r < n_ctx, 0, (r - n_ctx) // dec_seq + 1)
    return idx


def _ada_kernel(c_ref, w_ref, b_ref, o_ref):
    o_ref[...] = _bdot(_silu(c_ref[...]), w_ref[...]) + b_ref[...]


def ada_mod_all(cvecs, w_ada, b_ada, tn=1536):
    depth, d, n6 = w_ada.shape
    rows = cvecs.shape[0]
    return pl.pallas_call(
        _ada_kernel,
        grid=(depth, n6 // tn),
        in_specs=[pl.BlockSpec((rows, d), lambda l, n: (0, 0)),
                  pl.BlockSpec((None, d, tn), lambda l, n: (l, 0, n)),
                  pl.BlockSpec((None, 1, tn), lambda l, n: (l, 0, n))],
        out_specs=pl.BlockSpec((None, rows, tn), lambda l, n: (l, 0, n)),
        out_shape=jax.ShapeDtypeStruct((depth, rows, n6), F32),
        compiler_params=_params("arbitrary", "arbitrary"),
        name="ada_mod",
    )(cvecs, w_ada, b_ada.reshape(depth, 1, n6))


def _pool_kernel(*refs, n_ctx_tiles, ctx_seq_tiles, lat_seq_tiles, split_input):
    n_x = 6 if split_input else 3
    x_refs = refs[:n_x]
    g_ref, sh_ref, sc_ref, gt_ref, wp_ref, ps_ref, o_ref, buf_ref, *lvl_refs = refs[n_x:]
    i = pl.program_id(0)
    is_ctx = i < n_ctx_tiles
    j = jnp.where(is_ctx, i % ctx_seq_tiles, (i - n_ctx_tiles) % lat_seq_tiles)
    nt = jnp.where(is_ctx, ctx_seq_tiles, lat_seq_tiles)
    g, sh, sc = g_ref[...], sh_ref[...], sc_ref[...]
    tp, hl = POOL_TILE, POOL_HALO
    gw = o_ref.shape[1] // len(POOL_WINDOWS)

    def fill(xc_ref, xp_ref, xn_ref):
        buf_ref[pl.ds(0, hl), :] = jnp.where(j == 0, 0.0, _modulated(xp_ref[...], g, sh, sc))
        buf_ref[pl.ds(hl, tp), :] = _modulated(xc_ref[...], g, sh, sc)
        buf_ref[pl.ds(hl + tp, hl), :] = jnp.where(j == nt - 1, 0.0,
                                                    _modulated(xn_ref[...], g, sh, sc))
        o_ref[...] = xc_ref[...]

    if split_input:
        pl.when(is_ctx)(lambda: fill(*x_refs[:3]))
        pl.when(jnp.logical_not(is_ctx))(lambda: fill(*x_refs[3:]))
    else:
        fill(*x_refs)

    rows = tp + 2 * hl
    buf_ref[pl.ds(rows, hl), :] = jnp.zeros((hl, buf_ref.shape[1]), F32)
    for ref in lvl_refs:
        ref[pl.ds(rows, hl), :] = jnp.zeros((hl, gw), F32)
    pos = j * tp + lax.broadcasted_iota(jnp.int32, (tp, 1), 0)
    seq_len = nt * tp
    for gi, w in enumerate(POOL_WINDOWS):
        cols = pl.ds(gi * gw, gw)
        src, src_cols = buf_ref, cols
        for lvl in range(1, w.bit_length() - 1):
            step = 1 << (lvl - 1)
            dst = lvl_refs[lvl - 1]
            dst[pl.ds(0, rows), :] = src[pl.ds(0, rows), src_cols] + src[pl.ds(step, rows), src_cols]
            src, src_cols = dst, pl.ds(0, gw)
        acc = src[pl.ds(hl - w // 2, tp), src_cols] + src[pl.ds(hl, tp), src_cols]
        cnt = jnp.minimum(pos + w // 2, seq_len) - jnp.maximum(pos - w // 2, 0)
        pooled = acc / cnt.astype(F32) - buf_ref[pl.ds(hl, tp), cols]
        y = _bdot(pooled, wp_ref[gi]) * ps_ref[:, cols]
        o_ref[:, cols] = o_ref[:, cols] + gt_ref[:, cols] * y


def _halo_specs(tp, hl, d, tile_off, n_rows):
    hb = tp // hl
    last_tile, last_hblk = n_rows // tp - 1, n_rows // hl - 1
    tile = lambda i: jnp.clip(i - tile_off, 0, last_tile)
    return [pl.BlockSpec((tp, d), lambda i: (tile(i), 0)),
            pl.BlockSpec((hl, d), lambda i: (jnp.clip(tile(i) * hb - 1, 0, last_hblk), 0)),
            pl.BlockSpec((hl, d), lambda i: (jnp.clip((tile(i) + 1) * hb, 0, last_hblk), 0))]


def pool_layer(xs, g, shift, scale, gate, w_pool, pool_scale, n_ctx, seq, dec_seq):
    split = isinstance(xs, tuple)
    tp, hl = POOL_TILE, POOL_HALO
    if split:
        d = xs[0].shape[1]
        t = xs[0].shape[0] + xs[1].shape[0]
        x_specs = (_halo_specs(tp, hl, d, 0, xs[0].shape[0])
                   + _halo_specs(tp, hl, d, n_ctx // tp, xs[1].shape[0]))
        x_args = (xs[0],) * 3 + (xs[1],) * 3
    else:
        t, d = xs.shape
        x_specs = _halo_specs(tp, hl, d, 0, t)
        x_args = (xs,) * 3
    cidx = _cvec_index(n_ctx, dec_seq, tp)
    mod_spec = pl.BlockSpec((None, 1, d), lambda i: (cidx(i), 0, 0))
    row_spec = pl.BlockSpec((1, d), lambda i: (0, 0))
    assert all(w & (w - 1) == 0 and w // 2 <= hl for w in POOL_WINDOWS)
    n_levels = max(POOL_WINDOWS).bit_length() - 2
    kern = functools.partial(_pool_kernel, n_ctx_tiles=n_ctx // tp, ctx_seq_tiles=seq // tp,
                             lat_seq_tiles=dec_seq // tp, split_input=split)
    return pl.pallas_call(
        kern,
        grid=(t // tp,),
        in_specs=x_specs + [row_spec, mod_spec, mod_spec, mod_spec,
                            pl.BlockSpec(w_pool.shape, lambda i: (0, 0, 0)), row_spec],
        out_specs=pl.BlockSpec((tp, d), lambda i: (i, 0)),
        out_shape=jax.ShapeDtypeStruct((t, d), F32),
        scratch_shapes=[pltpu.VMEM((tp + 3 * hl, d), F32)]
        + [pltpu.VMEM((tp + 3 * hl, d // len(POOL_WINDOWS)), F32)] * n_levels,
        compiler_params=_params("arbitrary"),
        name="pool_mixer",
    )(*x_args, g, shift, scale, gate, w_pool, pool_scale)


def _route(sel, scores):
    e, tm = sel.shape
    row = lax.broadcasted_iota(jnp.int32, (e, tm), 0)
    best = jnp.zeros((1, tm), jnp.int32)
    best_sc = None
    for gidx in range(N_EXPERT_GROUPS):
        r = [sel[gidx * EXPERTS_PER_GROUP + k:gidx * EXPERTS_PER_GROUP + k + 1, :]
             for k in range(EXPERTS_PER_GROUP)]
        top2 = None
        for a in range(EXPERTS_PER_GROUP):
            for b in range(a + 1, EXPERTS_PER_GROUP):
                s = r[a] + r[b]
                top2 = s if top2 is None else jnp.maximum(top2, s)
        if best_sc is None:
            best_sc = top2
        else:
            better = top2 > best_sc
            best = jnp.where(better, gidx, best)
            best_sc = jnp.where(better, top2, best_sc)
    neg = -jnp.inf
    masked = jnp.where(row // EXPERTS_PER_GROUP == best, sel, neg)
    m1 = jnp.max(masked, axis=0, keepdims=True)
    i1 = jnp.min(jnp.where(masked == m1, row, e), axis=0, keepdims=True)
    masked2 = jnp.where(row == i1, neg, masked)
    m2 = jnp.max(masked2, axis=0, keepdims=True)
    i2 = jnp.min(jnp.where(masked2 == m2, row, e), axis=0, keepdims=True)
    hot1 = row == i1
    hot2 = row == i2
    w1 = jnp.sum(jnp.where(hot1, scores, 0.0), axis=0, keepdims=True)
    w2 = jnp.sum(jnp.where(hot2, scores, 0.0), axis=0, keepdims=True)
    tot = w1 + w2
    return best, jnp.where(hot1, w1 / tot, 0.0) + jnp.where(hot2, w2 / tot, 0.0)


def _split_bf16(a, parts):
    out = []
    for _ in range(parts):
        p = a.astype(BF16)
        out.append(p)
        a = a - p.astype(F32)
    return out


def _pad_rows(a, rows):
    return jnp.concatenate([a, jnp.zeros((rows - a.shape[0], a.shape[1]), a.dtype)], axis=0)


def _moe_kernel(x_ref, g_ref, sh_ref, sc_ref, gt_ref, wr_ref, br_ref, wgf_ref, wuf_ref, wdf_ref,
                gf_ref, *rest, final_norm, n_ctx_tiles, n_experts):
    out_refs, (wgu_ref, wd_ref), scratch = rest[:-5], rest[-5:-3], rest[-3:]
    i = pl.program_id(0)
    f = wgf_ref.shape[1]
    for e in range(n_experts):
        gi, k = divmod(e, EXPERTS_PER_GROUP)

        @pl.when(i == e)
        def _(gi=gi, k=k):
            wgu_ref[gi, :, pl.ds(2 * k * f, f)] = wgf_ref[...].astype(BF16)
            wgu_ref[gi, :, pl.ds((2 * k + 1) * f, f)] = wuf_ref[...].astype(BF16)
            wd_ref[gi, pl.ds(k * f, f), :] = wdf_ref[...].astype(BF16)

    @pl.when(i >= n_experts)
    def _():
        _moe_tile(x_ref, g_ref, sh_ref, sc_ref, gt_ref, wr_ref, br_ref, wgu_ref, wd_ref,
                  gf_ref, out_refs, scratch, i - n_experts, final_norm, n_ctx_tiles)


def _moe_tile(x_ref, g_ref, sh_ref, sc_ref, gt_ref, wr_ref, br_ref, wgu_ref, wd_ref,
              gf_ref, out_refs, scratch, tile, final_norm, n_ctx_tiles):
    hp_scr, cw_scr, yp_scr = scratch
    tr, d = x_ref.shape
    trp = hp_scr.shape[0]
    ng, eg = N_EXPERT_GROUPS, EXPERTS_PER_GROUP
    x = x_ref[...]
    h = _modulated(x, g_ref[...], sh_ref[...], sc_ref[...])
    hb = h.astype(BF16)

    h_lo = (h - hb.astype(F32)).astype(BF16)
    wr = wr_ref[...]
    lg = jnp.dot(hb, wr, preferred_element_type=F32)
    logits = lg[:, :LANES] + lg[:, LANES:] + jnp.dot(h_lo, wr[:, :LANES], preferred_element_type=F32)
    scores = jax.nn.sigmoid(logits.T[:N_EXPERTS, :])
    best, comb_t = _route(scores + br_ref[...], scores)

    grp = lax.broadcasted_iota(jnp.int32, (SUBLANES, tr), 0)
    hot_t = (grp == best).astype(F32)
    cw_t = hot_t[0:1, :] * comb_t[0:eg, :]
    for gi in range(1, ng):
        cw_t = cw_t + hot_t[gi:gi + 1, :] * comb_t[gi * eg:(gi + 1) * eg, :]
    cw_c = _pad_rows(cw_t, LANES).T

    ia = lax.broadcasted_iota(jnp.int32, (tr, tr), 0)
    ib = lax.broadcasted_iota(jnp.int32, (tr, tr), 1)
    before = jnp.where(ia < ib, 1.0, 0.0).astype(BF16)
    rank_t = jnp.dot(hot_t.astype(BF16), before, preferred_element_type=F32)

    starts, counts = [], []
    off = jnp.int32(0)
    for gi in range(ng):
        n = jnp.sum(hot_t[gi:gi + 1, :]).astype(jnp.int32)
        n = ((n + MOE_SEG_ALIGN - 1) // MOE_SEG_ALIGN) * MOE_SEG_ALIGN
        starts.append(off)
        counts.append(n)
        off = off + n

    pos_t = hot_t[0:1, :] * (rank_t[0:1, :] + starts[0].astype(F32))
    for gi in range(1, ng):
        pos_t = pos_t + hot_t[gi:gi + 1, :] * (rank_t[gi:gi + 1, :] + starts[gi].astype(F32))
    pos_c = _pad_rows(pos_t, LANES).T[:, 0:1]
    used = tr + ng * MOE_SEG_ALIGN
    perm = jnp.where(lax.broadcasted_iota(jnp.int32, (used, tr), 0) == pos_t.astype(jnp.int32),
                     1.0, 0.0).astype(BF16)
    perm_t = jnp.where(lax.broadcasted_iota(jnp.int32, (tr, used), 1) == pos_c.astype(jnp.int32),
                       1.0, 0.0).astype(BF16)

    hp_scr[pl.ds(0, used), :] = jnp.dot(perm, hb, preferred_element_type=F32).astype(BF16)
    hp_scr[pl.ds(used, trp - used), :] = jnp.zeros((trp - used, d), BF16)
    cw_pair = jnp.dot(perm, jnp.concatenate(_split_bf16(cw_c, 2), axis=1), preferred_element_type=F32)
    cw_scr[pl.ds(0, used), :] = cw_pair[:, :LANES] + cw_pair[:, LANES:]
    cw_scr[pl.ds(used, trp - used), :] = jnp.zeros((trp - used, LANES), F32)
    yp_scr[...] = jnp.zeros_like(yp_scr)

    f = wd_ref.shape[1] // eg

    def expert_block(gi, row0, n_rows):
        rows = pl.ds(pl.multiple_of(row0, MOE_SEG_ALIGN), n_rows)
        cwb = cw_scr[rows, :]
        gu = jnp.dot(hp_scr[rows, :], wgu_ref[gi], preferred_element_type=F32)
        hid = [(_silu(gu[:, 2 * k * f:(2 * k + 1) * f]) * gu[:, (2 * k + 1) * f:(2 * k + 2) * f]
                * cwb[:, k:k + 1]).astype(BF16) for k in range(eg)]
        yp_scr[rows, :] = jnp.dot(jnp.concatenate(hid, axis=1), wd_ref[gi],
                                  preferred_element_type=F32).astype(BF16)

    sb, wide = MOE_ROW_BLOCK, MOE_WIDE_BLOCK
    for gi in range(ng):
        use_wide = jnp.logical_and(counts[gi] > sb, counts[gi] <= wide)
        pl.when(use_wide)(functools.partial(expert_block, gi, starts[gi], wide))
        n_blocks = jnp.where(use_wide, 0, (counts[gi] + sb - 1) // sb)

        def block(b, carry, gi=gi):
            expert_block(gi, starts[gi] + b * sb, sb)
            return carry
        lax.fori_loop(0, n_blocks, block, 0)

    moe = jnp.dot(perm_t, yp_scr[pl.ds(0, used), :], preferred_element_type=F32)
    out = x + gt_ref[...] * moe
    if final_norm:
        out = _rms(out, gf_ref[...])
    if n_ctx_tiles is None:
        out_refs[0][...] = out
    else:
        is_ctx = tile < n_ctx_tiles

        @pl.when(is_ctx)
        def _():
            out_refs[0][...] = out

        @pl.when(jnp.logical_not(is_ctx))
        def _():
            out_refs[1][...] = out


def moe_layer(x, g, shift, scale, gate, w_router_split, b_router_col, wg, wu, wd, layer, g_final,
              n_ctx, dec_seq, final_norm, split_output=False, tm=MOE_TILE):
    t, d = x.shape
    _, n_e, _, f = wg.shape
    trp = tm + N_EXPERT_GROUPS * MOE_SEG_ALIGN + max(MOE_ROW_BLOCK, MOE_WIDE_BLOCK)
    tile = lambda i: jnp.maximum(i - n_e, 0)
    cidx = _cvec_index(n_ctx, dec_seq, tm)
    mod_spec = pl.BlockSpec((None, 1, d), lambda i: (cidx(tile(i)), 0, 0))
    row_spec = pl.BlockSpec((1, d), lambda i: (0, 0))
    expert = lambda i: jnp.minimum(i, n_e - 1)
    nct = n_ctx // tm
    if split_output:
        out_specs = [pl.BlockSpec((tm, d), lambda i: (jnp.minimum(tile(i), nct - 1), 0)),
                     pl.BlockSpec((tm, d), lambda i: (jnp.maximum(tile(i) - nct, 0), 0))]
        out_shape = [jax.ShapeDtypeStruct((n_ctx, d), F32), jax.ShapeDtypeStruct((t - n_ctx, d), F32)]
    else:
        out_specs = pl.BlockSpec((tm, d), lambda i: (tile(i), 0))
        out_shape = jax.ShapeDtypeStruct((t, d), F32)
    return pl.pallas_call(
        functools.partial(_moe_kernel, final_norm=final_norm,
                          n_ctx_tiles=nct if split_output else None, n_experts=n_e),
        grid=(n_e + t // tm,),
        in_specs=[pl.BlockSpec((tm, d), lambda i: (tile(i), 0)),
                  row_spec, mod_spec, mod_spec, mod_spec,
                  pl.BlockSpec(w_router_split.shape, lambda i: (0, 0)),
                  pl.BlockSpec(b_router_col.shape, lambda i: (0, 0)),
                  pl.BlockSpec((None, None, d, f), lambda i: (layer, expert(i), 0, 0)),
                  pl.BlockSpec((None, None, d, f), lambda i: (layer, expert(i), 0, 0)),
                  pl.BlockSpec((None, None, f, d), lambda i: (layer, expert(i), 0, 0)),
                  row_spec],
        out_specs=out_specs,
        out_shape=out_shape,
        scratch_shapes=[pltpu.VMEM((N_EXPERT_GROUPS, d, 2 * EXPERTS_PER_GROUP * f), BF16),
                        pltpu.VMEM((N_EXPERT_GROUPS, EXPERTS_PER_GROUP * f, d), BF16),
                        pltpu.VMEM((trp, d), BF16), pltpu.VMEM((trp, LANES), F32),
                        pltpu.VMEM((trp, d), BF16)],
        compiler_params=_params("arbitrary"),
        name="moe",
    )(x, g, shift, scale, gate, w_router_split, b_router_col, wg, wu, wd, g_final)


def _per_stream(n_ctx_tiles, ctx_ref, lat_ref, fn):
    i = pl.program_id(0)

    @pl.when(i < n_ctx_tiles)
    def _():
        fn(ctx_ref[...])

    @pl.when(i >= n_ctx_tiles)
    def _():
        fn(lat_ref[...])


def _stream_specs(tm, k, n_ctx_tiles):
    return [pl.BlockSpec((tm, k), lambda i: (jnp.minimum(i, n_ctx_tiles - 1), 0)),
            pl.BlockSpec((tm, k), lambda i: (jnp.maximum(i - n_ctx_tiles, 0), 0))]


def _resid_proj_kernel(x_ref, ac_ref, al_ref, w_ref, gt_ref, o_ref, *, n_ctx_tiles):
    def run(a):
        o_ref[...] = x_ref[...] + gt_ref[...] * _bdot(a, w_ref[...])
    _per_stream(n_ctx_tiles, ac_ref, al_ref, run)


def _mlstm_out_kernel(x_ref, hc_ref, hl_ref, og_ref, hg_ref, w_ref, gt_ref, o_ref, *, n_ctx_tiles):
    def run(hs):
        a = jax.nn.sigmoid(og_ref[...].astype(F32)) * (hs.astype(F32) * hg_ref[...])
        o_ref[...] = x_ref[...] + gt_ref[...] * _bdot(a, w_ref[...])
    _per_stream(n_ctx_tiles, hc_ref, hl_ref, run)


def resid_proj(x, a_ctx, a_lat, w, gate, n_ctx, dec_seq, tm=1024):
    t, d = x.shape
    k = a_ctx.shape[1]
    cidx = _cvec_index(n_ctx, dec_seq, tm)
    nct = n_ctx // tm
    return pl.pallas_call(
        functools.partial(_resid_proj_kernel, n_ctx_tiles=nct),
        grid=(t // tm,),
        in_specs=[pl.BlockSpec((tm, d), lambda i: (i, 0))] + _stream_specs(tm, k, nct) + [
            pl.BlockSpec((k, d), lambda i: (0, 0)),
            pl.BlockSpec((None, 1, d), lambda i: (cidx(i), 0, 0))],
        out_specs=pl.BlockSpec((tm, d), lambda i: (i, 0)),
        out_shape=jax.ShapeDtypeStruct((t, d), F32),
        compiler_params=_params("arbitrary"),
        name="resid_proj",
    )(x, a_ctx, a_lat, w, gate)


def mlstm_out(x, hs_ctx, hs_lat, o_gate, head_g, w, gate, n_ctx, dec_seq, tm=1024):
    t, d = x.shape
    k = hs_ctx.shape[1]
    cidx = _cvec_index(n_ctx, dec_seq, tm)
    nct = n_ctx // tm
    return pl.pallas_call(
        functools.partial(_mlstm_out_kernel, n_ctx_tiles=nct),
        grid=(t // tm,),
        in_specs=[pl.BlockSpec((tm, d), lambda i: (i, 0))] + _stream_specs(tm, k, nct) + [
            pl.BlockSpec((tm, k), lambda i: (i, 0)),
            pl.BlockSpec((1, k), lambda i: (0, 0)),
            pl.BlockSpec((k, d), lambda i: (0, 0)),
            pl.BlockSpec((None, 1, d), lambda i: (cidx(i), 0, 0))],
        out_specs=pl.BlockSpec((tm, d), lambda i: (i, 0)),
        out_shape=jax.ShapeDtypeStruct((t, d), F32),
        compiler_params=_params("arbitrary"),
        name="mlstm_out",
    )(x, hs_ctx, hs_lat, o_gate, head_g, w, gate)


def _mlstm_proj_kernel(x_ref, g_ref, sh_ref, sc_ref, wqk_ref, wv_ref, wo_ref, wgt_ref, bg_ref,
                       qs_ref, qkv_ref, o_ref, gc_ref, gr_ref):
    h = _modulated(x_ref[...], g_ref[...], sh_ref[...], sc_ref[...]).astype(BF16)
    nqk = wqk_ref.shape[0]
    qkv_ref[:, :nqk] = (_bdot_nt(h, wqk_ref[...]) * qs_ref[...]).astype(BF16)
    qkv_ref[:, nqk:] = _bdot_nt(h, wv_ref[...]).astype(BF16)
    o_ref[...] = _bdot_nt(h, wo_ref[...]).astype(BF16)
    gates = _bdot_nt(h, wgt_ref[...]) + bg_ref[...]
    gc_ref[...] = gates
    gr_ref[...] = gates.T[:gr_ref.shape[0], :]


def mlstm_proj(x, g, shift, scale, w_in_all, layer, b_gate, n_ctx, dec_seq, tm=512):
    t, d = x.shape
    hh = MLSTM_HEADS
    hv = d
    hk = hv // 2
    ng = 4 * hh
    w_t_all = jnp.swapaxes(w_in_all, 1, 2)
    w_g_pad = jnp.pad(w_t_all[layer, 2 * hk + 2 * hv:, :], ((0, LANES - ng), (0, 0)))
    b_pad = jnp.pad(b_gate.reshape(1, ng), ((0, 0), (0, LANES - ng)))
    dk = hk // hh
    q_scale = jnp.concatenate([jnp.full((1, hk), dk ** -0.5, F32), jnp.ones((1, hk), F32)], axis=1)
    cidx = _cvec_index(n_ctx, dec_seq, tm)
    mod_spec = pl.BlockSpec((None, 1, d), lambda i: (cidx(i), 0, 0))
    full = lambda a: pl.BlockSpec(a.shape, lambda i: (0,) * a.ndim)
    assert 2 * hk == hv
    w_col = lambda n: pl.BlockSpec((None, hv, d), lambda i: (layer, n, 0))
    return pl.pallas_call(
        _mlstm_proj_kernel,
        grid=(t // tm,),
        in_specs=[pl.BlockSpec((tm, d), lambda i: (i, 0)), full(g), mod_spec, mod_spec,
                  w_col(0), w_col(1), w_col(2), full(w_g_pad), full(b_pad), full(q_scale)],
        out_specs=[pl.BlockSpec((tm, 2 * hk + hv), lambda i: (i, 0)),
                   pl.BlockSpec((tm, hv), lambda i: (i, 0)),
                   pl.BlockSpec((tm, LANES), lambda i: (i, 0)),
                   pl.BlockSpec((ng, tm), lambda i: (0, i))],
        out_shape=[jax.ShapeDtypeStruct((t, 2 * hk + hv), BF16),
                   jax.ShapeDtypeStruct((t, hv), BF16),
                   jax.ShapeDtypeStruct((t, LANES), F32),
                   jax.ShapeDtypeStruct((ng, t), F32)],
        compiler_params=_params("arbitrary"),
        name="mlstm_proj",
    )(x, g, shift, scale, w_t_all, w_t_all, w_t_all, w_g_pad, b_pad, q_scale)


def _log_sigmoid(x):
    return jnp.minimum(x, 0.0) - jnp.log(1.0 + jnp.exp(-jnp.abs(x)))


def _gate_cumsums(gc, gr, causal_bf, feeds_bf):
    bc = br = None
    for part in _split_bf16(_log_sigmoid(gc), 3):
        t = jnp.dot(causal_bf, part, preferred_element_type=F32)
        bc = t if bc is None else bc + t
    for part in _split_bf16(_log_sigmoid(gr), 3):
        t = jnp.dot(part, feeds_bf, preferred_element_type=F32)
        br = t if br is None else br + t
    return bc, br


def _mlstm_chunk(q, k, v, i_col, b_col, i_row, b_row, c_st, n_st, m_st, causal, rev):
    l = q.shape[0]
    zero_state = c_st is None
    if zero_state:
        m_st = 0.0
    g_row = i_row - b_row
    log_w = jnp.where(causal, g_row, -jnp.inf)
    c_t = jnp.maximum(m_st, jnp.max(log_w, axis=1, keepdims=True))
    dw = jnp.exp(log_w - c_t)
    a = _bdot_nt(q, k) * dw
    a_hi, a_lo = _split_bf16(a, 2)
    ones = jnp.ones((l, LANES), BF16)
    den = (jnp.dot(a_hi, ones, preferred_element_type=F32)
           + jnp.dot(a_lo, ones, preferred_element_type=F32))[:, 0:1]
    if not zero_state:
        sw = jnp.exp(m_st - c_t)
        n_hi, n_lo = _split_bf16(jnp.broadcast_to(n_st, (LANES, n_st.shape[1])), 2)
        den = den + sw * (_bdot_nt(q, n_hi) + _bdot_nt(q, n_lo))[:, 0:1]
    inv = 1.0 / jnp.maximum(jnp.abs(den), jnp.exp(-(b_col + c_t)))
    h = inv * jnp.dot(a_hi, v, preferred_element_type=F32)
    if not zero_state:
        h = h + (sw * inv) * _bdot(q, c_st)
    b_last = b_col[0:1, :] if rev else b_col[l - 1:l, :]
    log_k = b_last - b_col + i_col
    m_new = jnp.maximum(b_last + m_st, jnp.max(log_k, axis=0, keepdims=True))
    kw = jnp.exp(log_k - m_new)
    kwk = kw * k.astype(F32)
    c_new = _bdot_tn(kwk, v)
    n_new = jnp.sum(kwk, axis=0, keepdims=True)
    if not zero_state:
        decay = jnp.exp(b_last + m_st - m_new)
        c_new = decay * c_st + c_new
        n_new = decay * n_st + n_new
    return h, c_new, n_new, m_new


def _mlstm_scan_kernel(*refs, n_chunks, zero_init):
    single = zero_init and n_chunks == 1
    if zero_init:
        q_ref, k_ref, v_ref, gc_ref, gr_ref, hs_ref, c_ref, n_ref, m_ref, hf_scr, hb_scr = refs
        if not single:
            c_ref[...] = jnp.zeros_like(c_ref)
            n_ref[...] = jnp.zeros_like(n_ref)
            m_ref[...] = jnp.zeros_like(m_ref)
    else:
        (q_ref, k_ref, v_ref, gc_ref, gr_ref, c0_ref, n0_ref, m0_ref,
         hs_ref, c_ref, n_ref, m_ref, hf_scr, hb_scr) = refs
        c_ref[...] = c0_ref[...]
        n_ref[...] = n0_ref[...]
        m_ref[...] = m0_ref[...]
    l = MLSTM_CHUNK
    hh = MLSTM_HEADS
    dk = q_ref.shape[1] // hh
    dv = v_ref.shape[1] // hh

    tt = lax.broadcasted_iota(jnp.int32, (l, l), 0)
    ss = lax.broadcasted_iota(jnp.int32, (l, l), 1)
    masks = (ss <= tt, ss >= tt)
    masks_bf = tuple(jnp.where(m, 1.0, 0.0).astype(BF16) for m in masks)

    def body(c, carry):
        cr = n_chunks - 1 - c
        start = (lambda j: j * l) if single else (lambda j: pl.multiple_of(j * l, l))
        rows = (pl.ds(start(c), l), pl.ds(start(cr), l))
        grs = (gr_ref[c], gr_ref[cr])
        outs, states = ([], []), []
        for d in range(2):
            q, k, v, gc, gr = q_ref[rows[d], :], k_ref[rows[d], :], v_ref[rows[d], :], gc_ref[rows[d], :], grs[d]
            bc, br = _gate_cumsums(gc, gr, masks_bf[d], masks_bf[1 - d])
            for h in range(hh):
                ci, cf = 2 * d * hh + h, (2 * d + 1) * hh + h
                state = (None,) * 3 if single else (c_ref[d, h], n_ref[d, h], m_ref[d, h])
                o, *st = _mlstm_chunk(q[:, h * dk:(h + 1) * dk], k[:, h * dk:(h + 1) * dk],
                                      v[:, h * dv:(h + 1) * dv], gc[:, ci:ci + 1], bc[:, cf:cf + 1],
                                      gr[ci:ci + 1, :], br[cf:cf + 1, :], *state, masks[d], d == 1)
                outs[d].append(o)
                states.append((d, h, st))
        hf_scr[rows[0], :] = jnp.concatenate(outs[0], axis=1)
        hb_scr[rows[1], :] = jnp.concatenate(outs[1], axis=1)
        for d, h, (c_new, n_new, m_new) in states:
            c_ref[d, h], n_ref[d, h], m_ref[d, h] = c_new, n_new, m_new
        return carry

    if single:
        body(0, 0)
    else:
        lax.fori_loop(0, n_chunks, body, 0)
    for h in range(hh):
        cols = pl.ds(h * dv, dv)
        hs = hf_scr[:, cols] + hb_scr[:, cols]
        hs_ref[:, cols] = (hs * lax.rsqrt(jnp.mean(hs * hs, axis=-1, keepdims=True)
                                          + NORM_EPS)).astype(BF16)


def mlstm_scan(qkv, gcol, grow, init, row_off, n_seq, seq_len):
    hh = MLSTM_HEADS
    hv = qkv.shape[1] // 2
    dv = hv // hh
    dk = dv // 2
    l = MLSTM_CHUNK
    nc = seq_len // l
    ob = row_off // seq_len
    kern = functools.partial(_mlstm_scan_kernel, n_chunks=nc, zero_init=init is None)
    st = lambda *tail: pl.BlockSpec((None, 2, hh) + tail, lambda s: (s, 0, 0) + (0,) * len(tail))
    states = [st(dk, dv), st(1, dk), st(1, 1)]
    return pl.pallas_call(
        kern,
        grid=(n_seq,),
        in_specs=[pl.BlockSpec((seq_len, hh * dk), lambda s: (ob + s, 0)),
                  pl.BlockSpec((seq_len, hh * dk), lambda s: (ob + s, 1)),
                  pl.BlockSpec((seq_len, hv), lambda s: (ob + s, 1)),
                  pl.BlockSpec((seq_len, LANES), lambda s: (ob + s, 0)),
                  pl.BlockSpec((nc, 4 * hh, l), lambda s: (ob + s, 0, 0))]
                 + ([] if init is None else states),
        out_specs=[pl.BlockSpec((seq_len, hv), lambda s: (s, 0))] + states,
        out_shape=[jax.ShapeDtypeStruct((n_seq * seq_len, hv), BF16),
                   jax.ShapeDtypeStruct((n_seq, 2, hh, dk, dv), F32),
                   jax.ShapeDtypeStruct((n_seq, 2, hh, 1, dk), F32),
                   jax.ShapeDtypeStruct((n_seq, 2, hh, 1, 1), F32)],
        scratch_shapes=[pltpu.VMEM((seq_len, hv), F32), pltpu.VMEM((seq_len, hv), F32)],
        compiler_params=_params("arbitrary"),
        name="mlstm_scan",
    )(qkv, qkv, qkv, gcol, grow, *(() if init is None else init))


def _mla_proj_kernel(x_ref, g_ref, sh_ref, sc_ref, win_ref, qg_ref, kvg_ref, wqb_ref, cos_ref,
                     sin_ref, q_ref, ckv_ref, kpe_ref, *, q_lora, kv_lora, rope, n_heads):
    h = _modulated(x_ref[...], g_ref[...], sh_ref[...], sc_ref[...])
    proj = _bdot_nt(h, win_ref[...])
    q_lat = proj[:, :q_lora]
    ckv_ref[...] = _rms(proj[:, q_lora:q_lora + kv_lora], kvg_ref[...])
    cos, sin = cos_ref[...], sin_ref[...]
    kpe = proj[:, q_lora + kv_lora:q_lora + kv_lora + rope]
    kpe_rot = proj[:, q_lora + kv_lora + rope:q_lora + kv_lora + 2 * rope]
    kpe_ref[...] = kpe * cos[:, :rope] + kpe_rot * sin[:, :rope]
    q = _bdot_nt(_rms(q_lat, qg_ref[...]), wqb_ref[...])
    n = n_heads * LANES
    pieces = []
    for hd in range(n_heads):
        c = slice(hd * LANES, (hd + 1) * LANES)
        pieces.append(q[:, c])
        pieces.append(q[:, n:2 * n][:, c] * cos + q[:, 2 * n:][:, c] * sin)
    q_ref[...] = jnp.concatenate(pieces, axis=1).astype(BF16)


def _rot_rows(wt):
    n, k = wt.shape
    quarter = MLA_ROPE // 4
    w4 = wt.reshape(n // (2 * quarter), 2, quarter, k)
    return jnp.concatenate([-w4[:, 1:2], w4[:, 0:1]], axis=1).reshape(n, k)


def _rope_tables(n_ctx, dec_batch, dec_seq):
    quarter = MLA_ROPE // 4
    freq = np.power(np.float32(ROPE_BASE), -np.arange(quarter, dtype=np.float32) / np.float32(quarter))
    pos = np.arange(dec_seq)
    ang_r = (pos // GRID_W).astype(np.float32)[:, None] * freq[None, :]
    ang_c = (pos % GRID_W).astype(np.float32)[:, None] * freq[None, :]
    ang = np.concatenate([ang_r, ang_r, ang_c, ang_c], axis=1).astype(np.float32)
    cos = np.concatenate([np.ones((n_ctx, MLA_ROPE), np.float32)] + [np.cos(ang)] * dec_batch, axis=0)
    sin = np.concatenate([np.zeros((n_ctx, MLA_ROPE), np.float32)] + [np.sin(ang)] * dec_batch, axis=0)
    reps = LANES // MLA_ROPE
    return (jnp.asarray(np.tile(cos, (1, reps)), F32), jnp.asarray(np.tile(sin, (1, reps)), F32))


def mla_proj(x, g, shift, scale, w_in, q_g, kv_g, w_qb, cos, sin, n_ctx, dec_seq, tm=512):
    t, d = x.shape
    hh, nope, rope = MLA_HEADS, MLA_NOPE, MLA_ROPE
    q_lora = q_g.shape[1]
    kv_lora = kv_g.shape[1]
    w_in_t = w_in.T
    w_in_ext = jnp.concatenate([w_in_t, _rot_rows(w_in_t[q_lora + kv_lora:])], axis=0)
    w3 = w_qb.T.reshape(hh, nope + rope, q_lora)
    w_qn = w3[:, :nope].reshape(hh * nope, q_lora)
    w_qp = w3[:, nope:].reshape(hh * rope, q_lora)
    assert nope == LANES and rope <= LANES
    head_pad = lambda w: jnp.pad(w.reshape(hh, rope, q_lora),
                                 ((0, 0), (0, LANES - rope), (0, 0))).reshape(hh * LANES, q_lora)
    w_qb_ext = jnp.concatenate([w_qn, head_pad(w_qp), head_pad(_rot_rows(w_qp))], axis=0)
    cidx = _cvec_index(n_ctx, dec_seq, tm)
    mod_spec = pl.BlockSpec((None, 1, d), lambda i: (cidx(i), 0, 0))
    full = lambda a: pl.BlockSpec(a.shape, lambda i: (0,) * a.ndim)
    kern = functools.partial(_mla_proj_kernel, q_lora=q_lora, kv_lora=kv_lora, rope=rope, n_heads=hh)
    return pl.pallas_call(
        kern,
        grid=(t // tm,),
        in_specs=[pl.BlockSpec((tm, d), lambda i: (i, 0)), full(g), mod_spec, mod_spec,
                  full(w_in_ext), full(q_g), full(kv_g), full(w_qb_ext),
                  pl.BlockSpec((tm, LANES), lambda i: (i, 0)),
                  pl.BlockSpec((tm, LANES), lambda i: (i, 0))],
        out_specs=[pl.BlockSpec((tm, 2 * hh * LANES), lambda i: (i, 0)),
                   pl.BlockSpec((tm, kv_lora), lambda i: (i, 0)),
                   pl.BlockSpec((tm, rope), lambda i: (i, 0))],
        out_shape=[jax.ShapeDtypeStruct((t, 2 * hh * LANES), BF16),
                   jax.ShapeDtypeStruct((t, kv_lora), F32),
                   jax.ShapeDtypeStruct((t, rope), F32)],
        compiler_params=_params("arbitrary"),
        name="mla_proj",
    )(x, g, shift, scale, w_in_ext, q_g, kv_g, w_qb_ext, cos, sin)


def _mla_kv_kernel(ckv_ref, kp_ref, w_ref, k_ref, v_ref):
    kv = _bdot(ckv_ref[...], w_ref[...])
    n = v_ref.shape[1]
    kp = kp_ref[...]
    pieces = []
    for hd in range(n // LANES):
        pieces += [kv[:, hd * LANES:(hd + 1) * LANES].astype(BF16), kp]
    k_ref[...] = jnp.concatenate(pieces, axis=1)
    v_ref[...] = kv[:, n:].astype(BF16)


def mla_kv(ckv_all, kp_pad, w_kvb, tm=1024):
    r, kv_lora = ckv_all.shape
    hh, nope, vd = MLA_HEADS, MLA_NOPE, MLA_V
    w3 = w_kvb.reshape(kv_lora, hh, nope + vd)
    w_perm = jnp.concatenate([w3[:, :, :nope].reshape(kv_lora, hh * nope),
                              w3[:, :, nope:].reshape(kv_lora, hh * vd)], axis=1)
    return pl.pallas_call(
        _mla_kv_kernel,
        grid=(r // tm,),
        in_specs=[pl.BlockSpec((tm, kv_lora), lambda i: (i, 0)),
                  pl.BlockSpec((tm, LANES), lambda i: (i, 0)),
                  pl.BlockSpec(w_perm.shape, lambda i: (0, 0))],
        out_specs=[pl.BlockSpec((tm, 2 * hh * LANES), lambda i: (i, 0)),
                   pl.BlockSpec((tm, hh * vd), lambda i: (i, 0))],
        out_shape=[jax.ShapeDtypeStruct((r, 2 * hh * LANES), BF16),
                   jax.ShapeDtypeStruct((r, hh * vd), BF16)],
        compiler_params=_params("arbitrary"),
        name="mla_kv",
    )(ckv_all, kp_pad, w_perm)


def _attn_kernel(q_ref, k_ref, v_ref, o_ref, *, scale):
    c = scale * np.log2(np.e)
    for h in range(MLA_HEADS):
        hk = slice(2 * h * LANES, 2 * (h + 1) * LANES)
        s = _bdot_nt(q_ref[:, hk], k_ref[:, hk])
        e = jnp.exp2((s - jnp.max(s, axis=-1, keepdims=True)) * c)
        o = _bdot(e, v_ref[:, h * MLA_V:(h + 1) * MLA_V]) / jnp.sum(e, axis=-1, keepdims=True)
        o_ref[:, h * MLA_V:(h + 1) * MLA_V] = o.astype(BF16)


def mla_attention(q, k, v, q_row_off, k_row_off, n_seq, q_len, k_len):
    tq = min(ATTN_Q_BLOCK, q_len)
    qb = q_len // tq
    q0 = q_row_off // tq
    k0 = k_row_off // k_len
    dq, dv = q.shape[1], v.shape[1]
    kern = functools.partial(_attn_kernel, scale=(MLA_NOPE + MLA_ROPE) ** -0.5)
    return pl.pallas_call(
        kern,
        grid=(n_seq, qb),
        in_specs=[pl.BlockSpec((tq, dq), lambda s, j: (q0 + s * qb + j, 0)),
                  pl.BlockSpec((k_len, dq), lambda s, j: (k0 + s, 0)),
                  pl.BlockSpec((k_len, dv), lambda s, j: (k0 + s, 0))],
        out_specs=pl.BlockSpec((tq, dv), lambda s, j: (s * qb + j, 0)),
        out_shape=jax.ShapeDtypeStruct((n_seq * q_len, dv), BF16),
        compiler_params=_params("arbitrary", "arbitrary"),
        name="mla_attention",
    )(q, k, v)


def kernel(x_prompt, x_sample, state_mlstm_C, state_mlstm_n, state_mlstm_m, cache_mla_ckv,
           cache_mla_kpe, c, c_ctx, w_ada, b_ada, norm_mix, norm_ffn, norm_final, w_pool,
           pool_scale, w_mlstm_in, b_mlstm_gate, mlstm_head_g, w_mlstm_out, w_mla_in, mla_q_g,
           mla_kv_g, w_mla_qb, w_mla_kvb, w_mla_out, w_router, b_router, w_exp_gate, w_exp_up,
           w_exp_down):
    batch, seq, d = x_prompt.shape
    dec_batch, dec_seq, _ = x_sample.shape
    depth = w_ada.shape[0]
    n_ctx = batch * seq
    n_lat = dec_batch * dec_seq
    hh = MLSTM_HEADS
    past = cache_mla_ckv.shape[2]

    x = (x_prompt.reshape(n_ctx, d), x_sample.reshape(n_lat, d))

    n_cv = 1 + dec_batch
    cvecs = jnp.concatenate([c_ctx[None, :], c, jnp.zeros((SUBLANES - n_cv % SUBLANES, d), F32)], axis=0)
    mod = ada_mod_all(cvecs, w_ada, b_ada).reshape(depth, cvecs.shape[0], 6, 1, d)

    w_router_pad = jnp.pad(w_router, ((0, 0), (0, LANES - N_EXPERTS)))
    w_router_hi = w_router_pad.astype(BF16)
    w_router_lo = (w_router_pad - w_router_hi.astype(F32)).astype(BF16)
    w_router_split = jnp.concatenate([w_router_hi, w_router_lo], axis=1)
    b_router_col = b_router.reshape(N_EXPERTS, 1)
    g_final = norm_final.reshape(1, d)
    row = lambda a: a.reshape(1, -1)

    outs = {}
    for i in range(depth):
        kind, j = i % 3, i // 3
        m = [mod[i, :n_cv, k] for k in range(6)]
        g_mix = row(norm_mix[i])
        if kind == 0:
            x = pool_layer(x, g_mix, m[0], m[1], m[2], w_pool[j], row(pool_scale[j]),
                           n_ctx, seq, dec_seq)
        elif kind == 1:
            qkv, o_gate, gcol, grow = mlstm_proj(x, g_mix, m[0], m[1], w_mlstm_in, j,
                                                 b_mlstm_gate[j], n_ctx, dec_seq)
            t = n_ctx + n_lat
            l = MLSTM_CHUNK
            grow_c = grow.reshape(4 * hh, t // l, l).transpose(1, 0, 2)
            hs_c, c_new, n_new, m_new = mlstm_scan(qkv, gcol, grow_c, None, 0, batch, seq)
            init = (state_mlstm_C[:, j], state_mlstm_n[:, j][:, :, :, None, :],
                    state_mlstm_m[:, j][:, :, :, None, None])
            hs_l, _, _, _ = mlstm_scan(qkv, gcol, grow_c, init, n_ctx, dec_batch, dec_seq)
            outs["C"] = c_new[:, None]
            outs["n"] = n_new[:, None, :, :, 0, :]
            outs["m"] = m_new[:, None, :, :, 0, 0]
            x = mlstm_out(x, hs_c, hs_l, o_gate, row(mlstm_head_g[j]), w_mlstm_out[j], m[2],
                          n_ctx, dec_seq)
        else:
            cos, sin = _rope_tables(n_ctx, dec_batch, dec_seq)
            q_cat, ckv, kpe = mla_proj(x, g_mix, m[0], m[1], w_mla_in[j], row(mla_q_g[j]),
                                       row(mla_kv_g[j]), w_mla_qb[j], cos, sin, n_ctx, dec_seq)
            lat_parts_c, lat_parts_p = [], []
            for b in range(dec_batch):
                lo = n_ctx + b * dec_seq
                lat_parts_c += [cache_mla_ckv[b, j], ckv[lo:lo + dec_seq]]
                lat_parts_p += [cache_mla_kpe[b, j], kpe[lo:lo + dec_seq]]
            ckv_all = jnp.concatenate(lat_parts_c + [ckv[:n_ctx]], axis=0)
            kp_all = jnp.concatenate(lat_parts_p + [kpe[:n_ctx]], axis=0).astype(BF16)
            kp_pad = jnp.pad(kp_all, ((0, 0), (0, LANES - kp_all.shape[1])))
            k_cat, v = mla_kv(ckv_all, kp_pad, w_mla_kvb[j])
            k_lat = past + dec_seq
            o_c = mla_attention(q_cat, k_cat, v, 0, dec_batch * k_lat, batch, seq, seq)
            o_l = mla_attention(q_cat, k_cat, v, n_ctx, 0, dec_batch, dec_seq, k_lat)
            outs["ckv"] = ckv[:n_ctx].reshape(batch, 1, seq, -1)
            outs["kpe"] = kpe[:n_ctx].reshape(batch, 1, seq, -1)
            x = resid_proj(x, o_c, o_l, w_mla_out[j], m[2], n_ctx, dec_seq)
        x = moe_layer(x, row(norm_ffn[i]), m[3], m[4], m[5], w_router_split, b_router_col,
                      w_exp_gate, w_exp_up, w_exp_down, i, g_final, n_ctx, dec_seq,
                      final_norm=(i == depth - 1), split_output=(i == depth - 1))

    y_prompt = x[0].reshape(batch, seq, d)
    y_sample = x[1].reshape(dec_batch, dec_seq, d)
    return (y_prompt, y_sample, outs["C"], outs["n"], outs["m"], outs["ckv"], outs["kpe"])
```

```python
import functools

import numpy as np
import jax
import jax.numpy as jnp
from jax import lax
from jax.experimental import pallas as pl
from jax.experimental.pallas import tpu as pltpu

F32 = jnp.float32
BF16 = jnp.bfloat16

NORM_EPS = 1e-6
GRID_W = 64
POOL_WINDOWS = (2, 4, 8, 16)
MLSTM_HEADS = 4
MLSTM_CHUNK = 256
MLA_HEADS = 8
MLA_NOPE = 128
MLA_ROPE = 64
MLA_V = 128
ROPE_BASE = 10000.0
N_EXPERTS = 16
N_EXPERT_GROUPS = 4
EXPERTS_PER_GROUP = N_EXPERTS // N_EXPERT_GROUPS

LANES = 128
SUBLANES = 8
VMEM_LIMIT = 56 * 1024 * 1024
MOE_VMEM_LIMIT = 60 * 1024 * 1024
POOL_TILE = 256
POOL_HALO = 8
ATTN_Q_BLOCK = 256
MOE_TILE = 512
MOE_SEG_ALIGN = 16
MOE_ROW_BLOCK = 160
MOE_WIDE_BLOCK = 224
MOE_LOAD_EXPERTS = 2


def _params(*sem, vmem_limit=VMEM_LIMIT):
    return pltpu.CompilerParams(dimension_semantics=sem, vmem_limit_bytes=vmem_limit)


def _rms(x, g):
    return x * lax.rsqrt(jnp.mean(x * x, axis=-1, keepdims=True) + NORM_EPS) * g


def _modulated(x, g, shift, scale):
    return _rms(x, g) * (1.0 + scale) + shift


def _silu(x):
    return x * jax.nn.sigmoid(x)


def _bdot(a, b):
    return jnp.dot(a.astype(BF16), b.astype(BF16), preferred_element_type=F32)


def _bdot_nt(a, b):
    return lax.dot_general(a.astype(BF16), b.astype(BF16), (((1,), (1,)), ((), ())),
                           preferred_element_type=F32)


def _bdot_tn(a, b):
    return lax.dot_general(a.astype(BF16), b.astype(BF16), (((0,), (0,)), ((), ())),
                           preferred_element_type=F32)


def _cvec_index(n_ctx, dec_seq, tm):
    def idx(i):
        r = i * tm
        return jnp.where(r < n_ctx, 0, (r - n_ctx) // dec_seq + 1)
    return idx


def _ada_kernel(c_ref, w_ref, b_ref, o_ref):
    o_ref[...] = _bdot(_silu(c_ref[...]), w_ref[...]) + b_ref[...]


def ada_mod_all(cvecs, w_ada, b_ada, tn=1536):
    depth, d, n6 = w_ada.shape
    rows = cvecs.shape[0]
    return pl.pallas_call(
        _ada_kernel,
        grid=(depth, n6 // tn),
        in_specs=[pl.BlockSpec((rows, d), lambda l, n: (0, 0)),
                  pl.BlockSpec((None, d, tn), lambda l, n: (l, 0, n)),
                  pl.BlockSpec((None, 1, tn), lambda l, n: (l, 0, n))],
        out_specs=pl.BlockSpec((None, rows, tn), lambda l, n: (l, 0, n)),
        out_shape=jax.ShapeDtypeStruct((depth, rows, n6), F32),
        compiler_params=_params("arbitrary", "arbitrary"),
        name="ada_mod",
    )(cvecs, w_ada, b_ada.reshape(depth, 1, n6))


def _pool_kernel(*refs, n_ctx_tiles, ctx_seq_tiles, lat_seq_tiles, split_input):
    n_x = 6 if split_input else 3
    x_refs = refs[:n_x]
    g_ref, sh_ref, sc_ref, gt_ref, wp_ref, ps_ref, o_ref, buf_ref, *lvl_refs = refs[n_x:]
    i = pl.program_id(0)
    is_ctx = i < n_ctx_tiles
    j = jnp.where(is_ctx, i % ctx_seq_tiles, (i - n_ctx_tiles) % lat_seq_tiles)
    nt = jnp.where(is_ctx, ctx_seq_tiles, lat_seq_tiles)
    g, sh, sc = g_ref[...], sh_ref[...], sc_ref[...]
    tp, hl = POOL_TILE, POOL_HALO
    gw = o_ref.shape[1] // len(POOL_WINDOWS)

    def fill(xc_ref, xp_ref, xn_ref):
        buf_ref[pl.ds(0, hl), :] = jnp.where(j == 0, 0.0, _modulated(xp_ref[...], g, sh, sc))
        buf_ref[pl.ds(hl, tp), :] = _modulated(xc_ref[...], g, sh, sc)
        buf_ref[pl.ds(hl + tp, hl), :] = jnp.where(j == nt - 1, 0.0,
                                                    _modulated(xn_ref[...], g, sh, sc))
        o_ref[...] = xc_ref[...]

    if split_input:
        pl.when(is_ctx)(lambda: fill(*x_refs[:3]))
        pl.when(jnp.logical_not(is_ctx))(lambda: fill(*x_refs[3:]))
    else:
        fill(*x_refs)

    rows = tp + 2 * hl
    buf_ref[pl.ds(rows, hl), :] = jnp.zeros((hl, buf_ref.shape[1]), F32)
    for ref in lvl_refs:
        ref[pl.ds(rows, hl), :] = jnp.zeros((hl, gw), F32)
    pos = j * tp + lax.broadcasted_iota(jnp.int32, (tp, 1), 0)
    seq_len = nt * tp
    for gi, w in enumerate(POOL_WINDOWS):
        cols = pl.ds(gi * gw, gw)
        src, src_cols = buf_ref, cols
        for lvl in range(1, w.bit_length() - 1):
            step = 1 << (lvl - 1)
            dst = lvl_refs[lvl - 1]
            dst[pl.ds(0, rows), :] = src[pl.ds(0, rows), src_cols] + src[pl.ds(step, rows), src_cols]
            src, src_cols = dst, pl.ds(0, gw)
        acc = src[pl.ds(hl - w // 2, tp), src_cols] + src[pl.ds(hl, tp), src_cols]
        cnt = jnp.minimum(pos + w // 2, seq_len) - jnp.maximum(pos - w // 2, 0)
        pooled = acc / cnt.astype(F32) - buf_ref[pl.ds(hl, tp), cols]
        y = _bdot(pooled, wp_ref[gi]) * ps_ref[:, cols]
        o_ref[:, cols] = o_ref[:, cols] + gt_ref[:, cols] * y


def _halo_specs(tp, hl, d, tile_off, n_rows):
    hb = tp // hl
    last_tile, last_hblk = n_rows // tp - 1, n_rows // hl - 1
    tile = lambda i: jnp.clip(i - tile_off, 0, last_tile)
    return [pl.BlockSpec((tp, d), lambda i: (tile(i), 0)),
            pl.BlockSpec((hl, d), lambda i: (jnp.clip(tile(i) * hb - 1, 0, last_hblk), 0)),
            pl.BlockSpec((hl, d), lambda i: (jnp.clip((tile(i) + 1) * hb, 0, last_hblk), 0))]


def pool_layer(xs, g, shift, scale, gate, w_pool, pool_scale, n_ctx, seq, dec_seq):
    split = isinstance(xs, tuple)
    tp, hl = POOL_TILE, POOL_HALO
    if split:
        d = xs[0].shape[1]
        t = xs[0].shape[0] + xs[1].shape[0]
        x_specs = (_halo_specs(tp, hl, d, 0, xs[0].shape[0])
                   + _halo_specs(tp, hl, d, n_ctx // tp, xs[1].shape[0]))
        x_args = (xs[0],) * 3 + (xs[1],) * 3
    else:
        t, d = xs.shape
        x_specs = _halo_specs(tp, hl, d, 0, t)
        x_args = (xs,) * 3
    cidx = _cvec_index(n_ctx, dec_seq, tp)
    mod_spec = pl.BlockSpec((None, 1, d), lambda i: (cidx(i), 0, 0))
    row_spec = pl.BlockSpec((1, d), lambda i: (0, 0))
    assert all(w & (w - 1) == 0 and w // 2 <= hl for w in POOL_WINDOWS)
    n_levels = max(POOL_WINDOWS).bit_length() - 2
    kern = functools.partial(_pool_kernel, n_ctx_tiles=n_ctx // tp, ctx_seq_tiles=seq // tp,
                             lat_seq_tiles=dec_seq // tp, split_input=split)
    return pl.pallas_call(
        kern,
        grid=(t // tp,),
        in_specs=x_specs + [row_spec, mod_spec, mod_spec, mod_spec,
                            pl.BlockSpec(w_pool.shape, lambda i: (0, 0, 0)), row_spec],
        out_specs=pl.BlockSpec((tp, d), lambda i: (i, 0)),
        out_shape=jax.ShapeDtypeStruct((t, d), F32),
        scratch_shapes=[pltpu.VMEM((tp + 3 * hl, d), F32)]
        + [pltpu.VMEM((tp + 3 * hl, d // len(POOL_WINDOWS)), F32)] * n_levels,
        compiler_params=_params("arbitrary"),
        name="pool_mixer",
    )(*x_args, g, shift, scale, gate, w_pool, pool_scale)


def _route(sel, scores):
    e, tm = sel.shape
    row = lax.broadcasted_iota(jnp.int32, (e, tm), 0)
    best = jnp.zeros((1, tm), jnp.int32)
    best_sc = None
    for gidx in range(N_EXPERT_GROUPS):
        r = [sel[gidx * EXPERTS_PER_GROUP + k:gidx * EXPERTS_PER_GROUP + k + 1, :]
             for k in range(EXPERTS_PER_GROUP)]
        top2 = None
        for a in range(EXPERTS_PER_GROUP):
            for b in range(a + 1, EXPERTS_PER_GROUP):
                s = r[a] + r[b]
                top2 = s if top2 is None else jnp.maximum(top2, s)
        if best_sc is None:
            best_sc = top2
        else:
            better = top2 > best_sc
            best = jnp.where(better, gidx, best)
            best_sc = jnp.where(better, top2, best_sc)
    neg = -jnp.inf
    masked = jnp.where(row // EXPERTS_PER_GROUP == best, sel, neg)
    m1 = jnp.max(masked, axis=0, keepdims=True)
    i1 = jnp.min(jnp.where(masked == m1, row, e), axis=0, keepdims=True)
    masked2 = jnp.where(row == i1, neg, masked)
    m2 = jnp.max(masked2, axis=0, keepdims=True)
    i2 = jnp.min(jnp.where(masked2 == m2, row, e), axis=0, keepdims=True)
    hot1 = row == i1
    hot2 = row == i2
    w1 = jnp.sum(jnp.where(hot1, scores, 0.0), axis=0, keepdims=True)
    w2 = jnp.sum(jnp.where(hot2, scores, 0.0), axis=0, keepdims=True)
    tot = w1 + w2
    return best, jnp.where(hot1, w1 / tot, 0.0) + jnp.where(hot2, w2 / tot, 0.0)


def _split_bf16(a, parts):
    out = []
    for _ in range(parts):
        p = a.astype(BF16)
        out.append(p)
        a = a - p.astype(F32)
    return out


def _pad_rows(a, rows):
    return jnp.concatenate([a, jnp.zeros((rows - a.shape[0], a.shape[1]), a.dtype)], axis=0)


def _moe_kernel(x_ref, g_ref, sh_ref, sc_ref, gt_ref, wr_ref, br_ref, wgf_ref, wuf_ref, wdf_ref,
                gf_ref, *rest, final_norm, n_ctx_tiles, n_experts):
    out_refs, (wgu_ref, wd_ref), scratch = rest[:-5], rest[-5:-3], rest[-3:]
    i = pl.program_id(0)
    per_step, f = wgf_ref.shape[0], wgf_ref.shape[2]
    n_load = n_experts // per_step
    for s in range(n_load):
        @pl.when(i == s)
        def _(s=s):
            for j in range(per_step):
                gi, k = divmod(s * per_step + j, EXPERTS_PER_GROUP)
                wgu_ref[gi, :, pl.ds(2 * k * f, f)] = wgf_ref[j].astype(BF16)
                wgu_ref[gi, :, pl.ds((2 * k + 1) * f, f)] = wuf_ref[j].astype(BF16)
                wd_ref[gi, pl.ds(k * f, f), :] = wdf_ref[j].astype(BF16)

    @pl.when(i >= n_load)
    def _():
        _moe_tile(x_ref, g_ref, sh_ref, sc_ref, gt_ref, wr_ref, br_ref, wgu_ref, wd_ref,
                  gf_ref, out_refs, scratch, i - n_load, final_norm, n_ctx_tiles)


def _moe_tile(x_ref, g_ref, sh_ref, sc_ref, gt_ref, wr_ref, br_ref, wgu_ref, wd_ref,
              gf_ref, out_refs, scratch, tile, final_norm, n_ctx_tiles):
    hp_scr, cw_scr, yp_scr = scratch
    tr, d = x_ref.shape
    trp = hp_scr.shape[0]
    ng, eg = N_EXPERT_GROUPS, EXPERTS_PER_GROUP
    x = x_ref[...]
    h = _modulated(x, g_ref[...], sh_ref[...], sc_ref[...])
    hb = h.astype(BF16)

    h_lo = (h - hb.astype(F32)).astype(BF16)
    wr = wr_ref[...]
    lg = jnp.dot(hb, wr, preferred_element_type=F32)
    logits = lg[:, :LANES] + lg[:, LANES:] + jnp.dot(h_lo, wr[:, :LANES], preferred_element_type=F32)
    scores = jax.nn.sigmoid(logits.T[:N_EXPERTS, :])
    best, comb_t = _route(scores + br_ref[...], scores)

    grp = lax.broadcasted_iota(jnp.int32, (SUBLANES, tr), 0)
    hot_t = (grp == best).astype(F32)
    cw_t = hot_t[0:1, :] * comb_t[0:eg, :]
    for gi in range(1, ng):
        cw_t = cw_t + hot_t[gi:gi + 1, :] * comb_t[gi * eg:(gi + 1) * eg, :]
    cw_c = _pad_rows(cw_t, LANES).T

    ia = lax.broadcasted_iota(jnp.int32, (tr, tr), 0)
    ib = lax.broadcasted_iota(jnp.int32, (tr, tr), 1)
    before = jnp.where(ia < ib, 1.0, 0.0).astype(BF16)
    rank_t = jnp.dot(hot_t.astype(BF16), before, preferred_element_type=F32)

    starts, counts = [], []
    off = jnp.int32(0)
    for gi in range(ng):
        n = jnp.sum(hot_t[gi:gi + 1, :]).astype(jnp.int32)
        n = ((n + MOE_SEG_ALIGN - 1) // MOE_SEG_ALIGN) * MOE_SEG_ALIGN
        starts.append(off)
        counts.append(n)
        off = off + n

    pos_t = hot_t[0:1, :] * (rank_t[0:1, :] + starts[0].astype(F32))
    for gi in range(1, ng):
        pos_t = pos_t + hot_t[gi:gi + 1, :] * (rank_t[gi:gi + 1, :] + starts[gi].astype(F32))
    pos_c = _pad_rows(pos_t, LANES).T[:, 0:1]
    used = tr + ng * MOE_SEG_ALIGN
    perm = jnp.where(lax.broadcasted_iota(jnp.int32, (used, tr), 0) == pos_t.astype(jnp.int32),
                     1.0, 0.0).astype(BF16)
    perm_t = jnp.where(lax.broadcasted_iota(jnp.int32, (tr, used), 1) == pos_c.astype(jnp.int32),
                       1.0, 0.0).astype(BF16)

    hp_scr[pl.ds(0, used), :] = jnp.dot(perm, hb, preferred_element_type=F32).astype(BF16)
    hp_scr[pl.ds(used, trp - used), :] = jnp.zeros((trp - used, d), BF16)
    cw_pair = jnp.dot(perm, jnp.concatenate(_split_bf16(cw_c, 2), axis=1), preferred_element_type=F32)
    cw_scr[pl.ds(0, used), :] = cw_pair[:, :LANES] + cw_pair[:, LANES:]
    cw_scr[pl.ds(used, trp - used), :] = jnp.zeros((trp - used, LANES), F32)
    yp_scr[...] = jnp.zeros_like(yp_scr)

    f = wd_ref.shape[1] // eg

    def expert_block(gi, row0, n_rows):
        rows = pl.ds(pl.multiple_of(row0, MOE_SEG_ALIGN), n_rows)
        cwb = cw_scr[rows, :]
        gu = jnp.dot(hp_scr[rows, :], wgu_ref[gi], preferred_element_type=F32)
        hid = [(_silu(gu[:, 2 * k * f:(2 * k + 1) * f]) * gu[:, (2 * k + 1) * f:(2 * k + 2) * f]
                * cwb[:, k:k + 1]).astype(BF16) for k in range(eg)]
        yp_scr[rows, :] = jnp.dot(jnp.concatenate(hid, axis=1), wd_ref[gi],
                                  preferred_element_type=F32).astype(BF16)

    sb, wide = MOE_ROW_BLOCK, MOE_WIDE_BLOCK
    for gi in range(ng):
        use_wide = jnp.logical_and(counts[gi] > sb, counts[gi] <= wide)
        pl.when(use_wide)(functools.partial(expert_block, gi, starts[gi], wide))
        n_blocks = jnp.where(use_wide, 0, (counts[gi] + sb - 1) // sb)

        def block(b, carry, gi=gi):
            expert_block(gi, starts[gi] + b * sb, sb)
            return carry
        lax.fori_loop(0, n_blocks, block, 0)

    moe = jnp.dot(perm_t, yp_scr[pl.ds(0, used), :], preferred_element_type=F32)
    out = x + gt_ref[...] * moe
    if final_norm:
        out = _rms(out, gf_ref[...])
    if n_ctx_tiles is None:
        out_refs[0][...] = out
    else:
        is_ctx = tile < n_ctx_tiles

        @pl.when(is_ctx)
        def _():
            out_refs[0][...] = out

        @pl.when(jnp.logical_not(is_ctx))
        def _():
            out_refs[1][...] = out


def moe_layer(x, g, shift, scale, gate, w_router_split, b_router_col, wg, wu, wd, layer, g_final,
              n_ctx, dec_seq, final_norm, split_output=False, tm=MOE_TILE):
    t, d = x.shape
    _, n_e, _, f = wg.shape
    trp = tm + N_EXPERT_GROUPS * MOE_SEG_ALIGN + max(MOE_ROW_BLOCK, MOE_WIDE_BLOCK)
    per_step = MOE_LOAD_EXPERTS
    n_load = n_e // per_step
    tile = lambda i: jnp.maximum(i - n_load, 0)
    cidx = _cvec_index(n_ctx, dec_seq, tm)
    mod_spec = pl.BlockSpec((None, 1, d), lambda i: (cidx(tile(i)), 0, 0))
    row_spec = pl.BlockSpec((1, d), lambda i: (0, 0))
    expert = lambda i: jnp.minimum(i, n_load - 1)
    nct = n_ctx // tm
    if split_output:
        out_specs = [pl.BlockSpec((tm, d), lambda i: (jnp.minimum(tile(i), nct - 1), 0)),
                     pl.BlockSpec((tm, d), lambda i: (jnp.maximum(tile(i) - nct, 0), 0))]
        out_shape = [jax.ShapeDtypeStruct((n_ctx, d), F32), jax.ShapeDtypeStruct((t - n_ctx, d), F32)]
    else:
        out_specs = pl.BlockSpec((tm, d), lambda i: (tile(i), 0))
        out_shape = jax.ShapeDtypeStruct((t, d), F32)
    return pl.pallas_call(
        functools.partial(_moe_kernel, final_norm=final_norm,
                          n_ctx_tiles=nct if split_output else None, n_experts=n_e),
        grid=(n_load + t // tm,),
        in_specs=[pl.BlockSpec((tm, d), lambda i: (tile(i), 0)),
                  row_spec, mod_spec, mod_spec, mod_spec,
                  pl.BlockSpec(w_router_split.shape, lambda i: (0, 0)),
                  pl.BlockSpec(b_router_col.shape, lambda i: (0, 0)),
                  pl.BlockSpec((None, per_step, d, f), lambda i: (layer, expert(i), 0, 0)),
                  pl.BlockSpec((None, per_step, d, f), lambda i: (layer, expert(i), 0, 0)),
                  pl.BlockSpec((None, per_step, f, d), lambda i: (layer, expert(i), 0, 0)),
                  row_spec],
        out_specs=out_specs,
        out_shape=out_shape,
        scratch_shapes=[pltpu.VMEM((N_EXPERT_GROUPS, d, 2 * EXPERTS_PER_GROUP * f), BF16),
                        pltpu.VMEM((N_EXPERT_GROUPS, EXPERTS_PER_GROUP * f, d), BF16),
                        pltpu.VMEM((trp, d), BF16), pltpu.VMEM((trp, LANES), F32),
                        pltpu.VMEM((trp, d), BF16)],
        compiler_params=_params("arbitrary", vmem_limit=MOE_VMEM_LIMIT),
        name="moe",
    )(x, g, shift, scale, gate, w_router_split, b_router_col, wg, wu, wd, g_final)


def _per_stream(n_ctx_tiles, ctx_ref, lat_ref, fn):
    i = pl.program_id(0)

    @pl.when(i < n_ctx_tiles)
    def _():
        fn(ctx_ref[...])

    @pl.when(i >= n_ctx_tiles)
    def _():
        fn(lat_ref[...])


def _stream_specs(tm, k, n_ctx_tiles):
    return [pl.BlockSpec((tm, k), lambda i: (jnp.minimum(i, n_ctx_tiles - 1), 0)),
            pl.BlockSpec((tm, k), lambda i: (jnp.maximum(i - n_ctx_tiles, 0), 0))]


def _resid_proj_kernel(x_ref, ac_ref, al_ref, w_ref, gt_ref, o_ref, *, n_ctx_tiles):
    def run(a):
        o_ref[...] = x_ref[...] + gt_ref[...] * _bdot(a, w_ref[...])
    _per_stream(n_ctx_tiles, ac_ref, al_ref, run)


def _mlstm_out_kernel(x_ref, hc_ref, hl_ref, og_ref, hg_ref, w_ref, gt_ref, o_ref, *, n_ctx_tiles):
    def run(hs):
        a = jax.nn.sigmoid(og_ref[...].astype(F32)) * (hs.astype(F32) * hg_ref[...])
        o_ref[...] = x_ref[...] + gt_ref[...] * _bdot(a, w_ref[...])
    _per_stream(n_ctx_tiles, hc_ref, hl_ref, run)


def resid_proj(x, a_ctx, a_lat, w, gate, n_ctx, dec_seq, tm=1024):
    t, d = x.shape
    k = a_ctx.shape[1]
    cidx = _cvec_index(n_ctx, dec_seq, tm)
    nct = n_ctx // tm
    return pl.pallas_call(
        functools.partial(_resid_proj_kernel, n_ctx_tiles=nct),
        grid=(t // tm,),
        in_specs=[pl.BlockSpec((tm, d), lambda i: (i, 0))] + _stream_specs(tm, k, nct) + [
            pl.BlockSpec((k, d), lambda i: (0, 0)),
            pl.BlockSpec((None, 1, d), lambda i: (cidx(i), 0, 0))],
        out_specs=pl.BlockSpec((tm, d), lambda i: (i, 0)),
        out_shape=jax.ShapeDtypeStruct((t, d), F32),
        compiler_params=_params("arbitrary"),
        name="resid_proj",
    )(x, a_ctx, a_lat, w, gate)


def mlstm_out(x, hs_ctx, hs_lat, o_gate, head_g, w, gate, n_ctx, dec_seq, tm=1024):
    t, d = x.shape
    k = hs_ctx.shape[1]
    cidx = _cvec_index(n_ctx, dec_seq, tm)
    nct = n_ctx // tm
    return pl.pallas_call(
        functools.partial(_mlstm_out_kernel, n_ctx_tiles=nct),
        grid=(t // tm,),
        in_specs=[pl.BlockSpec((tm, d), lambda i: (i, 0))] + _stream_specs(tm, k, nct) + [
            pl.BlockSpec((tm, k), lambda i: (i, 0)),
            pl.BlockSpec((1, k), lambda i: (0, 0)),
            pl.BlockSpec((k, d), lambda i: (0, 0)),
            pl.BlockSpec((None, 1, d), lambda i: (cidx(i), 0, 0))],
        out_specs=pl.BlockSpec((tm, d), lambda i: (i, 0)),
        out_shape=jax.ShapeDtypeStruct((t, d), F32),
        compiler_params=_params("arbitrary"),
        name="mlstm_out",
    )(x, hs_ctx, hs_lat, o_gate, head_g, w, gate)


def _mlstm_proj_kernel(x_ref, g_ref, sh_ref, sc_ref, wqk_ref, wv_ref, wo_ref, wgt_ref, bg_ref,
                       qs_ref, qkv_ref, o_ref, gc_ref, gr_ref):
    h = _modulated(x_ref[...], g_ref[...], sh_ref[...], sc_ref[...]).astype(BF16)
    nqk = wqk_ref.shape[0]
    qkv_ref[:, :nqk] = (_bdot_nt(h, wqk_ref[...]) * qs_ref[...]).astype(BF16)
    qkv_ref[:, nqk:] = _bdot_nt(h, wv_ref[...]).astype(BF16)
    o_ref[...] = _bdot_nt(h, wo_ref[...]).astype(BF16)
    gates = _bdot_nt(h, wgt_ref[...]) + bg_ref[...]
    gc_ref[...] = gates
    gr_ref[...] = gates.T[:gr_ref.shape[0], :]


def mlstm_proj(x, g, shift, scale, w_in_all, layer, b_gate, n_ctx, dec_seq, tm=512):
    t, d = x.shape
    hh = MLSTM_HEADS
    hv = d
    hk = hv // 2
    ng = 4 * hh
    w_t_all = jnp.swapaxes(w_in_all, 1, 2)
    w_g_pad = jnp.pad(w_t_all[layer, 2 * hk + 2 * hv:, :], ((0, LANES - ng), (0, 0)))
    b_pad = jnp.pad(b_gate.reshape(1, ng), ((0, 0), (0, LANES - ng)))
    dk = hk // hh
    q_scale = jnp.concatenate([jnp.full((1, hk), dk ** -0.5, F32), jnp.ones((1, hk), F32)], axis=1)
    cidx = _cvec_index(n_ctx, dec_seq, tm)
    mod_spec = pl.BlockSpec((None, 1, d), lambda i: (cidx(i), 0, 0))
    full = lambda a: pl.BlockSpec(a.shape, lambda i: (0,) * a.ndim)
    assert 2 * hk == hv
    w_col = lambda n: pl.BlockSpec((None, hv, d), lambda i: (layer, n, 0))
    return pl.pallas_call(
        _mlstm_proj_kernel,
        grid=(t // tm,),
        in_specs=[pl.BlockSpec((tm, d), lambda i: (i, 0)), full(g), mod_spec, mod_spec,
                  w_col(0), w_col(1), w_col(2), full(w_g_pad), full(b_pad), full(q_scale)],
        out_specs=[pl.BlockSpec((tm, 2 * hk + hv), lambda i: (i, 0)),
                   pl.BlockSpec((tm, hv), lambda i: (i, 0)),
                   pl.BlockSpec((tm, LANES), lambda i: (i, 0)),
                   pl.BlockSpec((ng, tm), lambda i: (0, i))],
        out_shape=[jax.ShapeDtypeStruct((t, 2 * hk + hv), BF16),
                   jax.ShapeDtypeStruct((t, hv), BF16),
                   jax.ShapeDtypeStruct((t, LANES), F32),
                   jax.ShapeDtypeStruct((ng, t), F32)],
        compiler_params=_params("arbitrary"),
        name="mlstm_proj",
    )(x, g, shift, scale, w_t_all, w_t_all, w_t_all, w_g_pad, b_pad, q_scale)


def _log_sigmoid(x):
    return jnp.minimum(x, 0.0) - jnp.log(1.0 + jnp.exp(-jnp.abs(x)))


def _gate_cumsums(gc, gr, causal_bf, feeds_bf):
    bc = br = None
    for part in _split_bf16(_log_sigmoid(gc), 3):
        t = jnp.dot(causal_bf, part, preferred_element_type=F32)
        bc = t if bc is None else bc + t
    for part in _split_bf16(_log_sigmoid(gr), 3):
        t = jnp.dot(part, feeds_bf, preferred_element_type=F32)
        br = t if br is None else br + t
    return bc, br


def _mlstm_chunk(q, k, v, i_col, b_col, i_row, b_row, c_st, n_st, m_st, causal, rev):
    l = q.shape[0]
    zero_state = c_st is None
    if zero_state:
        m_st = 0.0
    g_row = i_row - b_row
    log_w = jnp.where(causal, g_row, -jnp.inf)
    c_t = jnp.maximum(m_st, jnp.max(log_w, axis=1, keepdims=True))
    dw = jnp.exp(log_w - c_t)
    a = _bdot_nt(q, k) * dw
    a_hi, a_lo = _split_bf16(a, 2)
    ones = jnp.ones((l, LANES), BF16)
    den = (jnp.dot(a_hi, ones, preferred_element_type=F32)
           + jnp.dot(a_lo, ones, preferred_element_type=F32))[:, 0:1]
    if not zero_state:
        sw = jnp.exp(m_st - c_t)
        n_hi, n_lo = _split_bf16(jnp.broadcast_to(n_st, (LANES, n_st.shape[1])), 2)
        den = den + sw * (_bdot_nt(q, n_hi) + _bdot_nt(q, n_lo))[:, 0:1]
    inv = 1.0 / jnp.maximum(jnp.abs(den), jnp.exp(-(b_col + c_t)))
    h = inv * jnp.dot(a_hi, v, preferred_element_type=F32)
    if not zero_state:
        h = h + (sw * inv) * _bdot(q, c_st)
    b_last = b_col[0:1, :] if rev else b_col[l - 1:l, :]
    log_k = b_last - b_col + i_col
    m_new = jnp.maximum(b_last + m_st, jnp.max(log_k, axis=0, keepdims=True))
    kw = jnp.exp(log_k - m_new)
    kwk = kw * k.astype(F32)
    c_new = _bdot_tn(kwk, v)
    n_new = jnp.sum(kwk, axis=0, keepdims=True)
    if not zero_state:
        decay = jnp.exp(b_last + m_st - m_new)
        c_new = decay * c_st + c_new
        n_new = decay * n_st + n_new
    return h, c_new, n_new, m_new


def _mlstm_scan_kernel(*refs, n_chunks, zero_init):
    single = zero_init and n_chunks == 1
    if zero_init:
        q_ref, k_ref, v_ref, gc_ref, gr_ref, hs_ref, c_ref, n_ref, m_ref, hf_scr, hb_scr = refs
        if not single:
            c_ref[...] = jnp.zeros_like(c_ref)
            n_ref[...] = jnp.zeros_like(n_ref)
            m_ref[...] = jnp.zeros_like(m_ref)
    else:
        (q_ref, k_ref, v_ref, gc_ref, gr_ref, c0_ref, n0_ref, m0_ref,
         hs_ref, c_ref, n_ref, m_ref, hf_scr, hb_scr) = refs
        c_ref[...] = c0_ref[...]
        n_ref[...] = n0_ref[...]
        m_ref[...] = m0_ref[...]
    l = MLSTM_CHUNK
    hh = MLSTM_HEADS
    dk = q_ref.shape[1] // hh
    dv = v_ref.shape[1] // hh

    tt = lax.broadcasted_iota(jnp.int32, (l, l), 0)
    ss = lax.broadcasted_iota(jnp.int32, (l, l), 1)
    masks = (ss <= tt, ss >= tt)
    masks_bf = tuple(jnp.where(m, 1.0, 0.0).astype(BF16) for m in masks)

    def body(c, carry):
        cr = n_chunks - 1 - c
        start = (lambda j: j * l) if single else (lambda j: pl.multiple_of(j * l, l))
        rows = (pl.ds(start(c), l), pl.ds(start(cr), l))
        grs = (gr_ref[c], gr_ref[cr])
        outs, states = ([], []), []
        for d in range(2):
            q, k, v, gc, gr = q_ref[rows[d], :], k_ref[rows[d], :], v_ref[rows[d], :], gc_ref[rows[d], :], grs[d]
            bc, br = _gate_cumsums(gc, gr, masks_bf[d], masks_bf[1 - d])
            for h in range(hh):
                ci, cf = 2 * d * hh + h, (2 * d + 1) * hh + h
                state = (None,) * 3 if single else (c_ref[d, h], n_ref[d, h], m_ref[d, h])
                o, *st = _mlstm_chunk(q[:, h * dk:(h + 1) * dk], k[:, h * dk:(h + 1) * dk],
                                      v[:, h * dv:(h + 1) * dv], gc[:, ci:ci + 1], bc[:, cf:cf + 1],
                                      gr[ci:ci + 1, :], br[cf:cf + 1, :], *state, masks[d], d == 1)
                outs[d].append(o)
                states.append((d, h, st))
        hf_scr[rows[0], :] = jnp.concatenate(outs[0], axis=1)
        hb_scr[rows[1], :] = jnp.concatenate(outs[1], axis=1)
        for d, h, (c_new, n_new, m_new) in states:
            c_ref[d, h], n_ref[d, h], m_ref[d, h] = c_new, n_new, m_new
        return carry

    if single:
        body(0, 0)
    else:
        lax.fori_loop(0, n_chunks, body, 0)
    for h in range(hh):
        cols = pl.ds(h * dv, dv)
        hs = hf_scr[:, cols] + hb_scr[:, cols]
        hs_ref[:, cols] = (hs * lax.rsqrt(jnp.mean(hs * hs, axis=-1, keepdims=True)
                                          + NORM_EPS)).astype(BF16)


def mlstm_scan(qkv, gcol, grow, init, row_off, n_seq, seq_len):
    hh = MLSTM_HEADS
    hv = qkv.shape[1] // 2
    dv = hv // hh
    dk = dv // 2
    l = MLSTM_CHUNK
    nc = seq_len // l
    ob = row_off // seq_len
    kern = functools.partial(_mlstm_scan_kernel, n_chunks=nc, zero_init=init is None)
    st = lambda *tail: pl.BlockSpec((None, 2, hh) + tail, lambda s: (s, 0, 0) + (0,) * len(tail))
    states = [st(dk, dv), st(1, dk), st(1, 1)]
    return pl.pallas_call(
        kern,
        grid=(n_seq,),
        in_specs=[pl.BlockSpec((seq_len, hh * dk), lambda s: (ob + s, 0)),
                  pl.BlockSpec((seq_len, hh * dk), lambda s: (ob + s, 1)),
                  pl.BlockSpec((seq_len, hv), lambda s: (ob + s, 1)),
                  pl.BlockSpec((seq_len, LANES), lambda s: (ob + s, 0)),
                  pl.BlockSpec((nc, 4 * hh, l), lambda s: (ob + s, 0, 0))]
                 + ([] if init is None else states),
        out_specs=[pl.BlockSpec((seq_len, hv), lambda s: (s, 0))] + states,
        out_shape=[jax.ShapeDtypeStruct((n_seq * seq_len, hv), BF16),
                   jax.ShapeDtypeStruct((n_seq, 2, hh, dk, dv), F32),
                   jax.ShapeDtypeStruct((n_seq, 2, hh, 1, dk), F32),
                   jax.ShapeDtypeStruct((n_seq, 2, hh, 1, 1), F32)],
        scratch_shapes=[pltpu.VMEM((seq_len, hv), F32), pltpu.VMEM((seq_len, hv), F32)],
        compiler_params=_params("arbitrary"),
        name="mlstm_scan",
    )(qkv, qkv, qkv, gcol, grow, *(() if init is None else init))


def _mla_proj_kernel(x_ref, g_ref, sh_ref, sc_ref, win_ref, qg_ref, kvg_ref, wqb_ref, cos_ref,
                     sin_ref, q_ref, ckv_ref, kpe_ref, *, q_lora, kv_lora, rope, n_heads):
    h = _modulated(x_ref[...], g_ref[...], sh_ref[...], sc_ref[...])
    proj = _bdot_nt(h, win_ref[...])
    q_lat = proj[:, :q_lora]
    ckv_ref[...] = _rms(proj[:, q_lora:q_lora + kv_lora], kvg_ref[...])
    cos, sin = cos_ref[...], sin_ref[...]
    kpe = proj[:, q_lora + kv_lora:q_lora + kv_lora + rope]
    kpe_rot = proj[:, q_lora + kv_lora + rope:q_lora + kv_lora + 2 * rope]
    kpe_ref[...] = kpe * cos[:, :rope] + kpe_rot * sin[:, :rope]
    q = _bdot_nt(_rms(q_lat, qg_ref[...]), wqb_ref[...])
    n = n_heads * LANES
    pieces = []
    for hd in range(n_heads):
        c = slice(hd * LANES, (hd + 1) * LANES)
        pieces.append(q[:, c])
        pieces.append(q[:, n:2 * n][:, c] * cos + q[:, 2 * n:][:, c] * sin)
    q_ref[...] = jnp.concatenate(pieces, axis=1).astype(BF16)


def _rot_rows(wt):
    n, k = wt.shape
    quarter = MLA_ROPE // 4
    w4 = wt.reshape(n // (2 * quarter), 2, quarter, k)
    return jnp.concatenate([-w4[:, 1:2], w4[:, 0:1]], axis=1).reshape(n, k)


def _rope_tables(n_ctx, dec_batch, dec_seq):
    quarter = MLA_ROPE // 4
    freq = np.power(np.float32(ROPE_BASE), -np.arange(quarter, dtype=np.float32) / np.float32(quarter))
    pos = np.arange(dec_seq)
    ang_r = (pos // GRID_W).astype(np.float32)[:, None] * freq[None, :]
    ang_c = (pos % GRID_W).astype(np.float32)[:, None] * freq[None, :]
    ang = np.concatenate([ang_r, ang_r, ang_c, ang_c], axis=1).astype(np.float32)
    cos = np.concatenate([np.ones((n_ctx, MLA_ROPE), np.float32)] + [np.cos(ang)] * dec_batch, axis=0)
    sin = np.concatenate([np.zeros((n_ctx, MLA_ROPE), np.float32)] + [np.sin(ang)] * dec_batch, axis=0)
    reps = LANES // MLA_ROPE
    return (jnp.asarray(np.tile(cos, (1, reps)), F32), jnp.asarray(np.tile(sin, (1, reps)), F32))


def mla_proj(x, g, shift, scale, w_in, q_g, kv_g, w_qb, cos, sin, n_ctx, dec_seq, tm=512):
    t, d = x.shape
    hh, nope, rope = MLA_HEADS, MLA_NOPE, MLA_ROPE
    q_lora = q_g.shape[1]
    kv_lora = kv_g.shape[1]
    w_in_t = w_in.T
    w_in_ext = jnp.concatenate([w_in_t, _rot_rows(w_in_t[q_lora + kv_lora:])], axis=0)
    w3 = w_qb.T.reshape(hh, nope + rope, q_lora)
    w_qn = w3[:, :nope].reshape(hh * nope, q_lora)
    w_qp = w3[:, nope:].reshape(hh * rope, q_lora)
    assert nope == LANES and rope <= LANES
    head_pad = lambda w: jnp.pad(w.reshape(hh, rope, q_lora),
                                 ((0, 0), (0, LANES - rope), (0, 0))).reshape(hh * LANES, q_lora)
    w_qb_ext = jnp.concatenate([w_qn, head_pad(w_qp), head_pad(_rot_rows(w_qp))], axis=0)
    cidx = _cvec_index(n_ctx, dec_seq, tm)
    mod_spec = pl.BlockSpec((None, 1, d), lambda i: (cidx(i), 0, 0))
    full = lambda a: pl.BlockSpec(a.shape, lambda i: (0,) * a.ndim)
    kern = functools.partial(_mla_proj_kernel, q_lora=q_lora, kv_lora=kv_lora, rope=rope, n_heads=hh)
    return pl.pallas_call(
        kern,
        grid=(t // tm,),
        in_specs=[pl.BlockSpec((tm, d), lambda i: (i, 0)), full(g), mod_spec, mod_spec,
                  full(w_in_ext), full(q_g), full(kv_g), full(w_qb_ext),
                  pl.BlockSpec((tm, LANES), lambda i: (i, 0)),
                  pl.BlockSpec((tm, LANES), lambda i: (i, 0))],
        out_specs=[pl.BlockSpec((tm, 2 * hh * LANES), lambda i: (i, 0)),
                   pl.BlockSpec((tm, kv_lora), lambda i: (i, 0)),
                   pl.BlockSpec((tm, rope), lambda i: (i, 0))],
        out_shape=[jax.ShapeDtypeStruct((t, 2 * hh * LANES), BF16),
                   jax.ShapeDtypeStruct((t, kv_lora), F32),
                   jax.ShapeDtypeStruct((t, rope), F32)],
        compiler_params=_params("arbitrary"),
        name="mla_proj",
    )(x, g, shift, scale, w_in_ext, q_g, kv_g, w_qb_ext, cos, sin)


def _mla_kv_kernel(ckv_ref, kp_ref, w_ref, k_ref, v_ref):
    kv = _bdot(ckv_ref[...], w_ref[...])
    n = v_ref.shape[1]
    kp = kp_ref[...]
    pieces = []
    for hd in range(n // LANES):
        pieces += [kv[:, hd * LANES:(hd + 1) * LANES].astype(BF16), kp]
    k_ref[...] = jnp.concatenate(pieces, axis=1)
    v_ref[...] = kv[:, n:].astype(BF16)


def mla_kv(ckv_all, kp_pad, w_kvb, tm=1024):
    r, kv_lora = ckv_all.shape
    hh, nope, vd = MLA_HEADS, MLA_NOPE, MLA_V
    w3 = w_kvb.reshape(kv_lora, hh, nope + vd)
    w_perm = jnp.concatenate([w3[:, :, :nope].reshape(kv_lora, hh * nope),
                              w3[:, :, nope:].reshape(kv_lora, hh * vd)], axis=1)
    return pl.pallas_call(
        _mla_kv_kernel,
        grid=(r // tm,),
        in_specs=[pl.BlockSpec((tm, kv_lora), lambda i: (i, 0)),
                  pl.BlockSpec((tm, LANES), lambda i: (i, 0)),
                  pl.BlockSpec(w_perm.shape, lambda i: (0, 0))],
        out_specs=[pl.BlockSpec((tm, 2 * hh * LANES), lambda i: (i, 0)),
                   pl.BlockSpec((tm, hh * vd), lambda i: (i, 0))],
        out_shape=[jax.ShapeDtypeStruct((r, 2 * hh * LANES), BF16),
                   jax.ShapeDtypeStruct((r, hh * vd), BF16)],
        compiler_params=_params("arbitrary"),
        name="mla_kv",
    )(ckv_all, kp_pad, w_perm)


def _attn_kernel(q_ref, k_ref, v_ref, o_ref, *, scale):
    c = scale * np.log2(np.e)
    for h in range(MLA_HEADS):
        hk = slice(2 * h * LANES, 2 * (h + 1) * LANES)
        s = _bdot_nt(q_ref[:, hk], k_ref[:, hk])
        e = jnp.exp2((s - jnp.max(s, axis=-1, keepdims=True)) * c)
        o = _bdot(e, v_ref[:, h * MLA_V:(h + 1) * MLA_V]) / jnp.sum(e, axis=-1, keepdims=True)
        o_ref[:, h * MLA_V:(h + 1) * MLA_V] = o.astype(BF16)


def mla_attention(q, k, v, q_row_off, k_row_off, n_seq, q_len, k_len):
    tq = min(ATTN_Q_BLOCK, q_len)
    qb = q_len // tq
    q0 = q_row_off // tq
    k0 = k_row_off // k_len
    dq, dv = q.shape[1], v.shape[1]
    kern = functools.partial(_attn_kernel, scale=(MLA_NOPE + MLA_ROPE) ** -0.5)
    return pl.pallas_call(
        kern,
        grid=(n_seq, qb),
        in_specs=[pl.BlockSpec((tq, dq), lambda s, j: (q0 + s * qb + j, 0)),
                  pl.BlockSpec((k_len, dq), lambda s, j: (k0 + s, 0)),
                  pl.BlockSpec((k_len, dv), lambda s, j: (k0 + s, 0))],
        out_specs=pl.BlockSpec((tq, dv), lambda s, j: (s * qb + j, 0)),
        out_shape=jax.ShapeDtypeStruct((n_seq * q_len, dv), BF16),
        compiler_params=_params("arbitrary", "arbitrary"),
        name="mla_attention",
    )(q, k, v)


def kernel(x_prompt, x_sample, state_mlstm_C, state_mlstm_n, state_mlstm_m, cache_mla_ckv,
           cache_mla_kpe, c, c_ctx, w_ada, b_ada, norm_mix, norm_ffn, norm_final, w_pool,
           pool_scale, w_mlstm_in, b_mlstm_gate, mlstm_head_g, w_mlstm_out, w_mla_in, mla_q_g,
           mla_kv_g, w_mla_qb, w_mla_kvb, w_mla_out, w_router, b_router, w_exp_gate, w_exp_up,
           w_exp_down):
    batch, seq, d = x_prompt.shape
    dec_batch, dec_seq, _ = x_sample.shape
    depth = w_ada.shape[0]
    n_ctx = batch * seq
    n_lat = dec_batch * dec_seq
    hh = MLSTM_HEADS
    past = cache_mla_ckv.shape[2]

    x = (x_prompt.reshape(n_ctx, d), x_sample.reshape(n_lat, d))

    n_cv = 1 + dec_batch
    cvecs = jnp.concatenate([c_ctx[None, :], c, jnp.zeros((SUBLANES - n_cv % SUBLANES, d), F32)], axis=0)
    mod = ada_mod_all(cvecs, w_ada, b_ada).reshape(depth, cvecs.shape[0], 6, 1, d)

    w_router_pad = jnp.pad(w_router, ((0, 0), (0, LANES - N_EXPERTS)))
    w_router_hi = w_router_pad.astype(BF16)
    w_router_lo = (w_router_pad - w_router_hi.astype(F32)).astype(BF16)
    w_router_split = jnp.concatenate([w_router_hi, w_router_lo], axis=1)
    b_router_col = b_router.reshape(N_EXPERTS, 1)
    g_final = norm_final.reshape(1, d)
    row = lambda a: a.reshape(1, -1)

    outs = {}
    for i in range(depth):
        kind, j = i % 3, i // 3
        m = [mod[i, :n_cv, k] for k in range(6)]
        g_mix = row(norm_mix[i])
        if kind == 0:
            x = pool_layer(x, g_mix, m[0], m[1], m[2], w_pool[j], row(pool_scale[j]),
                           n_ctx, seq, dec_seq)
        elif kind == 1:
            qkv, o_gate, gcol, grow = mlstm_proj(x, g_mix, m[0], m[1], w_mlstm_in, j,
                                                 b_mlstm_gate[j], n_ctx, dec_seq)
            t = n_ctx + n_lat
            l = MLSTM_CHUNK
            grow_c = grow.reshape(4 * hh, t // l, l).transpose(1, 0, 2)
            hs_c, c_new, n_new, m_new = mlstm_scan(qkv, gcol, grow_c, None, 0, batch, seq)
            init = (state_mlstm_C[:, j], state_mlstm_n[:, j][:, :, :, None, :],
                    state_mlstm_m[:, j][:, :, :, None, None])
            hs_l, _, _, _ = mlstm_scan(qkv, gcol, grow_c, init, n_ctx, dec_batch, dec_seq)
            outs["C"] = c_new[:, None]
            outs["n"] = n_new[:, None, :, :, 0, :]
            outs["m"] = m_new[:, None, :, :, 0, 0]
            x = mlstm_out(x, hs_c, hs_l, o_gate, row(mlstm_head_g[j]), w_mlstm_out[j], m[2],
                          n_ctx, dec_seq)
        else:
            cos, sin = _rope_tables(n_ctx, dec_batch, dec_seq)
            q_cat, ckv, kpe = mla_proj(x, g_mix, m[0], m[1], w_mla_in[j], row(mla_q_g[j]),
                                       row(mla_kv_g[j]), w_mla_qb[j], cos, sin, n_ctx, dec_seq)
            lat_parts_c, lat_parts_p = [], []
            for b in range(dec_batch):
                lo = n_ctx + b * dec_seq
                lat_parts_c += [cache_mla_ckv[b, j], ckv[lo:lo + dec_seq]]
                lat_parts_p += [cache_mla_kpe[b, j], kpe[lo:lo + dec_seq]]
            ckv_all = jnp.concatenate(lat_parts_c + [ckv[:n_ctx]], axis=0)
            kp_all = jnp.concatenate(lat_parts_p + [kpe[:n_ctx]], axis=0).astype(BF16)
            kp_pad = jnp.pad(kp_all, ((0, 0), (0, LANES - kp_all.shape[1])))
            k_cat, v = mla_kv(ckv_all, kp_pad, w_mla_kvb[j])
            k_lat = past + dec_seq
            o_c = mla_attention(q_cat, k_cat, v, 0, dec_batch * k_lat, batch, seq, seq)
            o_l = mla_attention(q_cat, k_cat, v, n_ctx, 0, dec_batch, dec_seq, k_lat)
            outs["ckv"] = ckv[:n_ctx].reshape(batch, 1, seq, -1)
            outs["kpe"] = kpe[:n_ctx].reshape(batch, 1, seq, -1)
            x = resid_proj(x, o_c, o_l, w_mla_out[j], m[2], n_ctx, dec_seq)
        x = moe_layer(x, row(norm_ffn[i]), m[3], m[4], m[5], w_router_split, b_router_col,
                      w_exp_gate, w_exp_up, w_exp_down, i, g_final, n_ctx, dec_seq,
                      final_norm=(i == depth - 1), split_output=(i == depth - 1))

    y_prompt = x[0].reshape(batch, seq, d)
    y_sample = x[1].reshape(dec_batch, dec_seq, d)
    return (y_prompt, y_sample, outs["C"], outs["n"], outs["m"], outs["ckv"], outs["kpe"])
```

```python
import functools

import numpy as np
import jax
import jax.numpy as jnp
from jax import lax
from jax.experimental import pallas as pl
from jax.experimental.pallas import tpu as pltpu

F32 = jnp.float32
BF16 = jnp.bfloat16

NORM_EPS = 1e-6
GRID_W = 64
POOL_WINDOWS = (2, 4, 8, 16)
MLSTM_HEADS = 4
MLSTM_CHUNK = 256
MLA_HEADS = 8
MLA_NOPE = 128
MLA_ROPE = 64
MLA_V = 128
ROPE_BASE = 10000.0
N_EXPERTS = 16
N_EXPERT_GROUPS = 4
EXPERTS_PER_GROUP = N_EXPERTS // N_EXPERT_GROUPS

LANES = 128
SUBLANES = 8
VMEM_LIMIT = 56 * 1024 * 1024
MOE_VMEM_LIMIT = 60 * 1024 * 1024
POOL_TILE = 256
POOL_HALO = 8
ATTN_Q_BLOCK = 256
MOE_TILE = 512
MOE_SEG_ALIGN = 16
MOE_ROW_BLOCK = 160
MOE_WIDE_BLOCK = 224
MOE_LOAD_EXPERTS = 2


def _params(*sem, vmem_limit=VMEM_LIMIT):
    return pltpu.CompilerParams(dimension_semantics=sem, vmem_limit_bytes=vmem_limit)


def _rms(x, g):
    return x * lax.rsqrt(jnp.mean(x * x, axis=-1, keepdims=True) + NORM_EPS) * g


def _modulated(x, g, shift, scale):
    return _rms(x, g) * (1.0 + scale) + shift


def _silu(x):
    return x * jax.nn.sigmoid(x)


def _bdot(a, b):
    return jnp.dot(a.astype(BF16), b.astype(BF16), preferred_element_type=F32)


def _bdot_nt(a, b):
    return lax.dot_general(a.astype(BF16), b.astype(BF16), (((1,), (1,)), ((), ())),
                           preferred_element_type=F32)


def _bdot_tn(a, b):
    return lax.dot_general(a.astype(BF16), b.astype(BF16), (((0,), (0,)), ((), ())),
                           preferred_element_type=F32)


def _cvec_index(n_ctx, dec_seq, tm):
    def idx(i):
        r = i * tm
        return jnp.where(r < n_ctx, 0, (r - n_ctx) // dec_seq + 1)
    return idx


def _ada_kernel(c_ref, w_ref, b_ref, o_ref):
    o_ref[...] = _bdot(_silu(c_ref[...]), w_ref[...]) + b_ref[...]


def ada_mod_all(cvecs, w_ada, b_ada, tn=3072):
    depth, d, n6 = w_ada.shape
    rows = cvecs.shape[0]
    return pl.pallas_call(
        _ada_kernel,
        grid=(depth, n6 // tn),
        in_specs=[pl.BlockSpec((rows, d), lambda l, n: (0, 0)),
                  pl.BlockSpec((None, d, tn), lambda l, n: (l, 0, n)),
                  pl.BlockSpec((None, 1, tn), lambda l, n: (l, 0, n))],
        out_specs=pl.BlockSpec((None, rows, tn), lambda l, n: (l, 0, n)),
        out_shape=jax.ShapeDtypeStruct((depth, rows, n6), F32),
        compiler_params=_params("arbitrary", "arbitrary"),
        name="ada_mod",
    )(cvecs, w_ada, b_ada.reshape(depth, 1, n6))


def _pool_kernel(*refs, n_ctx_tiles, ctx_seq_tiles, lat_seq_tiles, split_input):
    n_x = 6 if split_input else 3
    x_refs = refs[:n_x]
    g_ref, sh_ref, sc_ref, gt_ref, wp_ref, ps_ref, o_ref, buf_ref, *lvl_refs = refs[n_x:]
    i = pl.program_id(0)
    is_ctx = i < n_ctx_tiles
    j = jnp.where(is_ctx, i % ctx_seq_tiles, (i - n_ctx_tiles) % lat_seq_tiles)
    nt = jnp.where(is_ctx, ctx_seq_tiles, lat_seq_tiles)
    g, sh, sc = g_ref[...], sh_ref[...], sc_ref[...]
    tp, hl = POOL_TILE, POOL_HALO
    gw = o_ref.shape[1] // len(POOL_WINDOWS)

    def fill(xc_ref, xp_ref, xn_ref):
        buf_ref[pl.ds(0, hl), :] = jnp.where(j == 0, 0.0, _modulated(xp_ref[...], g, sh, sc))
        buf_ref[pl.ds(hl, tp), :] = _modulated(xc_ref[...], g, sh, sc)
        buf_ref[pl.ds(hl + tp, hl), :] = jnp.where(j == nt - 1, 0.0,
                                                    _modulated(xn_ref[...], g, sh, sc))
        o_ref[...] = xc_ref[...]

    if split_input:
        pl.when(is_ctx)(lambda: fill(*x_refs[:3]))
        pl.when(jnp.logical_not(is_ctx))(lambda: fill(*x_refs[3:]))
    else:
        fill(*x_refs)

    rows = tp + 2 * hl
    buf_ref[pl.ds(rows, hl), :] = jnp.zeros((hl, buf_ref.shape[1]), F32)
    for ref in lvl_refs:
        ref[pl.ds(rows, hl), :] = jnp.zeros((hl, gw), F32)
    pos = j * tp + lax.broadcasted_iota(jnp.int32, (tp, 1), 0)
    seq_len = nt * tp
    for gi, w in enumerate(POOL_WINDOWS):
        cols = pl.ds(gi * gw, gw)
        src, src_cols = buf_ref, cols
        for lvl in range(1, w.bit_length() - 1):
            step = 1 << (lvl - 1)
            dst = lvl_refs[lvl - 1]
            dst[pl.ds(0, rows), :] = src[pl.ds(0, rows), src_cols] + src[pl.ds(step, rows), src_cols]
            src, src_cols = dst, pl.ds(0, gw)
        acc = src[pl.ds(hl - w // 2, tp), src_cols] + src[pl.ds(hl, tp), src_cols]
        cnt = jnp.minimum(pos + w // 2, seq_len) - jnp.maximum(pos - w // 2, 0)
        pooled = acc / cnt.astype(F32) - buf_ref[pl.ds(hl, tp), cols]
        y = _bdot(pooled, wp_ref[gi]) * ps_ref[:, cols]
        o_ref[:, cols] = o_ref[:, cols] + gt_ref[:, cols] * y


def _halo_specs(tp, hl, d, tile_off, n_rows):
    hb = tp // hl
    last_tile, last_hblk = n_rows // tp - 1, n_rows // hl - 1
    tile = lambda i: jnp.clip(i - tile_off, 0, last_tile)
    return [pl.BlockSpec((tp, d), lambda i: (tile(i), 0)),
            pl.BlockSpec((hl, d), lambda i: (jnp.clip(tile(i) * hb - 1, 0, last_hblk), 0)),
            pl.BlockSpec((hl, d), lambda i: (jnp.clip((tile(i) + 1) * hb, 0, last_hblk), 0))]


def pool_layer(xs, g, shift, scale, gate, w_pool, pool_scale, n_ctx, seq, dec_seq):
    split = isinstance(xs, tuple)
    tp, hl = POOL_TILE, POOL_HALO
    if split:
        d = xs[0].shape[1]
        t = xs[0].shape[0] + xs[1].shape[0]
        x_specs = (_halo_specs(tp, hl, d, 0, xs[0].shape[0])
                   + _halo_specs(tp, hl, d, n_ctx // tp, xs[1].shape[0]))
        x_args = (xs[0],) * 3 + (xs[1],) * 3
    else:
        t, d = xs.shape
        x_specs = _halo_specs(tp, hl, d, 0, t)
        x_args = (xs,) * 3
    cidx = _cvec_index(n_ctx, dec_seq, tp)
    mod_spec = pl.BlockSpec((None, 1, d), lambda i: (cidx(i), 0, 0))
    row_spec = pl.BlockSpec((1, d), lambda i: (0, 0))
    assert all(w & (w - 1) == 0 and w // 2 <= hl for w in POOL_WINDOWS)
    n_levels = max(POOL_WINDOWS).bit_length() - 2
    kern = functools.partial(_pool_kernel, n_ctx_tiles=n_ctx // tp, ctx_seq_tiles=seq // tp,
                             lat_seq_tiles=dec_seq // tp, split_input=split)
    return pl.pallas_call(
        kern,
        grid=(t // tp,),
        in_specs=x_specs + [row_spec, mod_spec, mod_spec, mod_spec,
                            pl.BlockSpec(w_pool.shape, lambda i: (0, 0, 0)), row_spec],
        out_specs=pl.BlockSpec((tp, d), lambda i: (i, 0)),
        out_shape=jax.ShapeDtypeStruct((t, d), F32),
        scratch_shapes=[pltpu.VMEM((tp + 3 * hl, d), F32)]
        + [pltpu.VMEM((tp + 3 * hl, d // len(POOL_WINDOWS)), F32)] * n_levels,
        compiler_params=_params("arbitrary"),
        name="pool_mixer",
    )(*x_args, g, shift, scale, gate, w_pool, pool_scale)


def _route(sel, scores):
    e, tm = sel.shape
    row = lax.broadcasted_iota(jnp.int32, (e, tm), 0)
    best = jnp.zeros((1, tm), jnp.int32)
    best_sc = None
    for gidx in range(N_EXPERT_GROUPS):
        r = [sel[gidx * EXPERTS_PER_GROUP + k:gidx * EXPERTS_PER_GROUP + k + 1, :]
             for k in range(EXPERTS_PER_GROUP)]
        top2 = None
        for a in range(EXPERTS_PER_GROUP):
            for b in range(a + 1, EXPERTS_PER_GROUP):
                s = r[a] + r[b]
                top2 = s if top2 is None else jnp.maximum(top2, s)
        if best_sc is None:
            best_sc = top2
        else:
            better = top2 > best_sc
            best = jnp.where(better, gidx, best)
            best_sc = jnp.where(better, top2, best_sc)
    neg = -jnp.inf
    masked = jnp.where(row // EXPERTS_PER_GROUP == best, sel, neg)
    m1 = jnp.max(masked, axis=0, keepdims=True)
    i1 = jnp.min(jnp.where(masked == m1, row, e), axis=0, keepdims=True)
    masked2 = jnp.where(row == i1, neg, masked)
    m2 = jnp.max(masked2, axis=0, keepdims=True)
    i2 = jnp.min(jnp.where(masked2 == m2, row, e), axis=0, keepdims=True)
    hot1 = row == i1
    hot2 = row == i2
    w1 = jnp.sum(jnp.where(hot1, scores, 0.0), axis=0, keepdims=True)
    w2 = jnp.sum(jnp.where(hot2, scores, 0.0), axis=0, keepdims=True)
    tot = w1 + w2
    return best, jnp.where(hot1, w1 / tot, 0.0) + jnp.where(hot2, w2 / tot, 0.0)


def _split_bf16(a, parts):
    out = []
    for _ in range(parts):
        p = a.astype(BF16)
        out.append(p)
        a = a - p.astype(F32)
    return out


def _pad_rows(a, rows):
    return jnp.concatenate([a, jnp.zeros((rows - a.shape[0], a.shape[1]), a.dtype)], axis=0)


def _moe_kernel(x_ref, g_ref, sh_ref, sc_ref, gt_ref, wr_ref, br_ref, wgf_ref, wuf_ref, wdf_ref,
                gf_ref, *rest, final_norm, n_ctx_tiles, n_experts):
    out_refs, (wgu_ref, wd_ref), scratch = rest[:-5], rest[-5:-3], rest[-3:]
    i = pl.program_id(0)
    per_step, f = wgf_ref.shape[0], wgf_ref.shape[2]
    n_load = n_experts // per_step
    for s in range(n_load):
        @pl.when(i == s)
        def _(s=s):
            for j in range(per_step):
                gi, k = divmod(s * per_step + j, EXPERTS_PER_GROUP)
                wgu_ref[gi, :, pl.ds(2 * k * f, f)] = wgf_ref[j].astype(BF16)
                wgu_ref[gi, :, pl.ds((2 * k + 1) * f, f)] = wuf_ref[j].astype(BF16)
                wd_ref[gi, pl.ds(k * f, f), :] = wdf_ref[j].astype(BF16)

    @pl.when(i >= n_load)
    def _():
        _moe_tile(x_ref, g_ref, sh_ref, sc_ref, gt_ref, wr_ref, br_ref, wgu_ref, wd_ref,
                  gf_ref, out_refs, scratch, i - n_load, final_norm, n_ctx_tiles)


def _moe_tile(x_ref, g_ref, sh_ref, sc_ref, gt_ref, wr_ref, br_ref, wgu_ref, wd_ref,
              gf_ref, out_refs, scratch, tile, final_norm, n_ctx_tiles):
    hp_scr, cw_scr, yp_scr = scratch
    tr, d = x_ref.shape
    trp = hp_scr.shape[0]
    ng, eg = N_EXPERT_GROUPS, EXPERTS_PER_GROUP
    x = x_ref[...]
    h = _modulated(x, g_ref[...], sh_ref[...], sc_ref[...])
    hb = h.astype(BF16)

    h_lo = (h - hb.astype(F32)).astype(BF16)
    wr = wr_ref[...]
    lg = jnp.dot(hb, wr, preferred_element_type=F32)
    logits = lg[:, :LANES] + lg[:, LANES:] + jnp.dot(h_lo, wr[:, :LANES], preferred_element_type=F32)
    scores = jax.nn.sigmoid(logits.T[:N_EXPERTS, :])
    best, comb_t = _route(scores + br_ref[...], scores)

    grp = lax.broadcasted_iota(jnp.int32, (SUBLANES, tr), 0)
    hot_t = (grp == best).astype(F32)
    cw_t = hot_t[0:1, :] * comb_t[0:eg, :]
    for gi in range(1, ng):
        cw_t = cw_t + hot_t[gi:gi + 1, :] * comb_t[gi * eg:(gi + 1) * eg, :]
    cw_c = _pad_rows(cw_t, LANES).T

    ia = lax.broadcasted_iota(jnp.int32, (tr, tr), 0)
    ib = lax.broadcasted_iota(jnp.int32, (tr, tr), 1)
    before = jnp.where(ia < ib, 1.0, 0.0).astype(BF16)
    rank_t = jnp.dot(hot_t.astype(BF16), before, preferred_element_type=F32)

    starts, counts = [], []
    off = jnp.int32(0)
    for gi in range(ng):
        n = jnp.sum(hot_t[gi:gi + 1, :]).astype(jnp.int32)
        n = ((n + MOE_SEG_ALIGN - 1) // MOE_SEG_ALIGN) * MOE_SEG_ALIGN
        starts.append(off)
        counts.append(n)
        off = off + n

    pos_t = hot_t[0:1, :] * (rank_t[0:1, :] + starts[0].astype(F32))
    for gi in range(1, ng):
        pos_t = pos_t + hot_t[gi:gi + 1, :] * (rank_t[gi:gi + 1, :] + starts[gi].astype(F32))
    pos_c = _pad_rows(pos_t, LANES).T[:, 0:1]
    used = tr + ng * MOE_SEG_ALIGN
    perm = jnp.where(lax.broadcasted_iota(jnp.int32, (used, tr), 0) == pos_t.astype(jnp.int32),
                     1.0, 0.0).astype(BF16)
    perm_t = jnp.where(lax.broadcasted_iota(jnp.int32, (tr, used), 1) == pos_c.astype(jnp.int32),
                       1.0, 0.0).astype(BF16)

    hp_scr[pl.ds(0, used), :] = jnp.dot(perm, hb, preferred_element_type=F32).astype(BF16)
    hp_scr[pl.ds(used, trp - used), :] = jnp.zeros((trp - used, d), BF16)
    cw_pair = jnp.dot(perm, jnp.concatenate(_split_bf16(cw_c, 2), axis=1), preferred_element_type=F32)
    cw_scr[pl.ds(0, used), :] = cw_pair[:, :LANES] + cw_pair[:, LANES:]
    cw_scr[pl.ds(used, trp - used), :] = jnp.zeros((trp - used, LANES), F32)
    yp_scr[...] = jnp.zeros_like(yp_scr)

    f = wd_ref.shape[1] // eg

    def expert_block(gi, row0, n_rows):
        rows = pl.ds(pl.multiple_of(row0, MOE_SEG_ALIGN), n_rows)
        cwb = cw_scr[rows, :]
        gu = jnp.dot(hp_scr[rows, :], wgu_ref[gi], preferred_element_type=F32)
        hid = [(_silu(gu[:, 2 * k * f:(2 * k + 1) * f]) * gu[:, (2 * k + 1) * f:(2 * k + 2) * f]
                * cwb[:, k:k + 1]).astype(BF16) for k in range(eg)]
        yp_scr[rows, :] = jnp.dot(jnp.concatenate(hid, axis=1), wd_ref[gi],
                                  preferred_element_type=F32).astype(BF16)

    sb, wide = MOE_ROW_BLOCK, MOE_WIDE_BLOCK
    for gi in range(ng):
        use_wide = jnp.logical_and(counts[gi] > sb, counts[gi] <= wide)
        pl.when(use_wide)(functools.partial(expert_block, gi, starts[gi], wide))
        n_blocks = jnp.where(use_wide, 0, (counts[gi] + sb - 1) // sb)

        def block(b, carry, gi=gi):
            expert_block(gi, starts[gi] + b * sb, sb)
            return carry
        lax.fori_loop(0, n_blocks, block, 0)

    moe = jnp.dot(perm_t, yp_scr[pl.ds(0, used), :], preferred_element_type=F32)
    out = x + gt_ref[...] * moe
    if final_norm:
        out = _rms(out, gf_ref[...])
    if n_ctx_tiles is None:
        out_refs[0][...] = out
    else:
        is_ctx = tile < n_ctx_tiles

        @pl.when(is_ctx)
        def _():
            out_refs[0][...] = out

        @pl.when(jnp.logical_not(is_ctx))
        def _():
            out_refs[1][...] = out


def moe_layer(x, g, shift, scale, gate, w_router_split, b_router_col, wg, wu, wd, layer, g_final,
              n_ctx, dec_seq, final_norm, split_output=False, tm=MOE_TILE):
    t, d = x.shape
    _, n_e, _, f = wg.shape
    trp = tm + N_EXPERT_GROUPS * MOE_SEG_ALIGN + max(MOE_ROW_BLOCK, MOE_WIDE_BLOCK)
    per_step = MOE_LOAD_EXPERTS
    n_load = n_e // per_step
    tile = lambda i: jnp.maximum(i - n_load, 0)
    cidx = _cvec_index(n_ctx, dec_seq, tm)
    mod_spec = pl.BlockSpec((None, 1, d), lambda i: (cidx(tile(i)), 0, 0))
    row_spec = pl.BlockSpec((1, d), lambda i: (0, 0))
    expert = lambda i: jnp.minimum(i, n_load - 1)
    nct = n_ctx // tm
    if split_output:
        out_specs = [pl.BlockSpec((tm, d), lambda i: (jnp.minimum(tile(i), nct - 1), 0)),
                     pl.BlockSpec((tm, d), lambda i: (jnp.maximum(tile(i) - nct, 0), 0))]
        out_shape = [jax.ShapeDtypeStruct((n_ctx, d), F32), jax.ShapeDtypeStruct((t - n_ctx, d), F32)]
    else:
        out_specs = pl.BlockSpec((tm, d), lambda i: (tile(i), 0))
        out_shape = jax.ShapeDtypeStruct((t, d), F32)
    return pl.pallas_call(
        functools.partial(_moe_kernel, final_norm=final_norm,
                          n_ctx_tiles=nct if split_output else None, n_experts=n_e),
        grid=(n_load + t // tm,),
        in_specs=[pl.BlockSpec((tm, d), lambda i: (tile(i), 0)),
                  row_spec, mod_spec, mod_spec, mod_spec,
                  pl.BlockSpec(w_router_split.shape, lambda i: (0, 0)),
                  pl.BlockSpec(b_router_col.shape, lambda i: (0, 0)),
                  pl.BlockSpec((None, per_step, d, f), lambda i: (layer, expert(i), 0, 0)),
                  pl.BlockSpec((None, per_step, d, f), lambda i: (layer, expert(i), 0, 0)),
                  pl.BlockSpec((None, per_step, f, d), lambda i: (layer, expert(i), 0, 0)),
                  row_spec],
        out_specs=out_specs,
        out_shape=out_shape,
        scratch_shapes=[pltpu.VMEM((N_EXPERT_GROUPS, d, 2 * EXPERTS_PER_GROUP * f), BF16),
                        pltpu.VMEM((N_EXPERT_GROUPS, EXPERTS_PER_GROUP * f, d), BF16),
                        pltpu.VMEM((trp, d), BF16), pltpu.VMEM((trp, LANES), F32),
                        pltpu.VMEM((trp, d), BF16)],
        compiler_params=_params("arbitrary", vmem_limit=MOE_VMEM_LIMIT),
        name="moe",
    )(x, g, shift, scale, gate, w_router_split, b_router_col, wg, wu, wd, g_final)


def _per_stream(n_ctx_tiles, ctx_ref, lat_ref, fn):
    i = pl.program_id(0)

    @pl.when(i < n_ctx_tiles)
    def _():
        fn(ctx_ref[...])

    @pl.when(i >= n_ctx_tiles)
    def _():
        fn(lat_ref[...])


def _stream_specs(tm, k, n_ctx_tiles):
    return [pl.BlockSpec((tm, k), lambda i: (jnp.minimum(i, n_ctx_tiles - 1), 0)),
            pl.BlockSpec((tm, k), lambda i: (jnp.maximum(i - n_ctx_tiles, 0), 0))]


def _resid_proj_kernel(x_ref, ac_ref, al_ref, w_ref, gt_ref, o_ref, *, n_ctx_tiles):
    def run(a):
        o_ref[...] = x_ref[...] + gt_ref[...] * _bdot(a, w_ref[...])
    _per_stream(n_ctx_tiles, ac_ref, al_ref, run)


def _mlstm_out_kernel(x_ref, hc_ref, hl_ref, og_ref, hg_ref, w_ref, gt_ref, o_ref, *, n_ctx_tiles):
    def run(hs):
        a = jax.nn.sigmoid(og_ref[...].astype(F32)) * (hs.astype(F32) * hg_ref[...])
        o_ref[...] = x_ref[...] + gt_ref[...] * _bdot(a, w_ref[...])
    _per_stream(n_ctx_tiles, hc_ref, hl_ref, run)


def resid_proj(x, a_ctx, a_lat, w, gate, n_ctx, dec_seq, tm=1024):
    t, d = x.shape
    k = a_ctx.shape[1]
    cidx = _cvec_index(n_ctx, dec_seq, tm)
    nct = n_ctx // tm
    return pl.pallas_call(
        functools.partial(_resid_proj_kernel, n_ctx_tiles=nct),
        grid=(t // tm,),
        in_specs=[pl.BlockSpec((tm, d), lambda i: (i, 0))] + _stream_specs(tm, k, nct) + [
            pl.BlockSpec((k, d), lambda i: (0, 0)),
            pl.BlockSpec((None, 1, d), lambda i: (cidx(i), 0, 0))],
        out_specs=pl.BlockSpec((tm, d), lambda i: (i, 0)),
        out_shape=jax.ShapeDtypeStruct((t, d), F32),
        compiler_params=_params("arbitrary"),
        name="resid_proj",
    )(x, a_ctx, a_lat, w, gate)


def mlstm_out(x, hs_ctx, hs_lat, o_gate, head_g, w, gate, n_ctx, dec_seq, tm=1024):
    t, d = x.shape
    k = hs_ctx.shape[1]
    cidx = _cvec_index(n_ctx, dec_seq, tm)
    nct = n_ctx // tm
    return pl.pallas_call(
        functools.partial(_mlstm_out_kernel, n_ctx_tiles=nct),
        grid=(t // tm,),
        in_specs=[pl.BlockSpec((tm, d), lambda i: (i, 0))] + _stream_specs(tm, k, nct) + [
            pl.BlockSpec((tm, k), lambda i: (i, 0)),
            pl.BlockSpec((1, k), lambda i: (0, 0)),
            pl.BlockSpec((k, d), lambda i: (0, 0)),
            pl.BlockSpec((None, 1, d), lambda i: (cidx(i), 0, 0))],
        out_specs=pl.BlockSpec((tm, d), lambda i: (i, 0)),
        out_shape=jax.ShapeDtypeStruct((t, d), F32),
        compiler_params=_params("arbitrary"),
        name="mlstm_out",
    )(x, hs_ctx, hs_lat, o_gate, head_g, w, gate)


def _mlstm_proj_kernel(x_ref, g_ref, sh_ref, sc_ref, wqk_ref, wv_ref, wo_ref, wgt_ref, bg_ref,
                       qs_ref, qkv_ref, o_ref, gc_ref, gr_ref):
    h = _modulated(x_ref[...], g_ref[...], sh_ref[...], sc_ref[...]).astype(BF16)
    nqk = wqk_ref.shape[0]
    qkv_ref[:, :nqk] = (_bdot_nt(h, wqk_ref[...]) * qs_ref[...]).astype(BF16)
    qkv_ref[:, nqk:] = _bdot_nt(h, wv_ref[...]).astype(BF16)
    o_ref[...] = _bdot_nt(h, wo_ref[...]).astype(BF16)
    gates = _bdot_nt(h, wgt_ref[...]) + bg_ref[...]
    gc_ref[...] = gates
    gr_ref[...] = gates.T[:gr_ref.shape[0], :]


def mlstm_proj(x, g, shift, scale, w_in_all, layer, b_gate, n_ctx, dec_seq, tm=512):
    t, d = x.shape
    hh = MLSTM_HEADS
    hv = d
    hk = hv // 2
    ng = 4 * hh
    w_t_all = jnp.swapaxes(w_in_all, 1, 2)
    w_g_pad = jnp.pad(w_t_all[layer, 2 * hk + 2 * hv:, :], ((0, LANES - ng), (0, 0)))
    b_pad = jnp.pad(b_gate.reshape(1, ng), ((0, 0), (0, LANES - ng)))
    dk = hk // hh
    q_scale = jnp.concatenate([jnp.full((1, hk), dk ** -0.5, F32), jnp.ones((1, hk), F32)], axis=1)
    cidx = _cvec_index(n_ctx, dec_seq, tm)
    mod_spec = pl.BlockSpec((None, 1, d), lambda i: (cidx(i), 0, 0))
    full = lambda a: pl.BlockSpec(a.shape, lambda i: (0,) * a.ndim)
    assert 2 * hk == hv
    w_col = lambda n: pl.BlockSpec((None, hv, d), lambda i: (layer, n, 0))
    return pl.pallas_call(
        _mlstm_proj_kernel,
        grid=(t // tm,),
        in_specs=[pl.BlockSpec((tm, d), lambda i: (i, 0)), full(g), mod_spec, mod_spec,
                  w_col(0), w_col(1), w_col(2), full(w_g_pad), full(b_pad), full(q_scale)],
        out_specs=[pl.BlockSpec((tm, 2 * hk + hv), lambda i: (i, 0)),
                   pl.BlockSpec((tm, hv), lambda i: (i, 0)),
                   pl.BlockSpec((tm, LANES), lambda i: (i, 0)),
                   pl.BlockSpec((ng, tm), lambda i: (0, i))],
        out_shape=[jax.ShapeDtypeStruct((t, 2 * hk + hv), BF16),
                   jax.ShapeDtypeStruct((t, hv), BF16),
                   jax.ShapeDtypeStruct((t, LANES), F32),
                   jax.ShapeDtypeStruct((ng, t), F32)],
        compiler_params=_params("arbitrary"),
        name="mlstm_proj",
    )(x, g, shift, scale, w_t_all, w_t_all, w_t_all, w_g_pad, b_pad, q_scale)


def _log_sigmoid(x):
    return jnp.minimum(x, 0.0) - jnp.log(1.0 + jnp.exp(-jnp.abs(x)))


def _gate_cumsums(gc, gr, causal_bf, feeds_bf):
    bc = br = None
    for part in _split_bf16(_log_sigmoid(gc), 3):
        t = jnp.dot(causal_bf, part, preferred_element_type=F32)
        bc = t if bc is None else bc + t
    for part in _split_bf16(_log_sigmoid(gr), 3):
        t = jnp.dot(part, feeds_bf, preferred_element_type=F32)
        br = t if br is None else br + t
    return bc, br


def _mlstm_chunk(q, k, v, i_col, b_col, i_row, b_row, c_st, n_st, m_st, causal, rev):
    l = q.shape[0]
    zero_state = c_st is None
    if zero_state:
        m_st = 0.0
    g_row = i_row - b_row
    log_w = jnp.where(causal, g_row, -jnp.inf)
    c_t = jnp.maximum(m_st, jnp.max(log_w, axis=1, keepdims=True))
    dw = jnp.exp(log_w - c_t)
    a = _bdot_nt(q, k) * dw
    a_hi, a_lo = _split_bf16(a, 2)
    ones = jnp.ones((l, LANES), BF16)
    den = (jnp.dot(a_hi, ones, preferred_element_type=F32)
           + jnp.dot(a_lo, ones, preferred_element_type=F32))[:, 0:1]
    if not zero_state:
        sw = jnp.exp(m_st - c_t)
        n_hi, n_lo = _split_bf16(jnp.broadcast_to(n_st, (LANES, n_st.shape[1])), 2)
        den = den + sw * (_bdot_nt(q, n_hi) + _bdot_nt(q, n_lo))[:, 0:1]
    inv = 1.0 / jnp.maximum(jnp.abs(den), jnp.exp(-(b_col + c_t)))
    h = inv * jnp.dot(a_hi, v, preferred_element_type=F32)
    if not zero_state:
        h = h + (sw * inv) * _bdot(q, c_st)
    b_last = b_col[0:1, :] if rev else b_col[l - 1:l, :]
    log_k = b_last - b_col + i_col
    m_new = jnp.maximum(b_last + m_st, jnp.max(log_k, axis=0, keepdims=True))
    kw = jnp.exp(log_k - m_new)
    kwk = kw * k.astype(F32)
    c_new = _bdot_tn(kwk, v)
    n_new = jnp.sum(kwk, axis=0, keepdims=True)
    if not zero_state:
        decay = jnp.exp(b_last + m_st - m_new)
        c_new = decay * c_st + c_new
        n_new = decay * n_st + n_new
    return h, c_new, n_new, m_new


def _mlstm_scan_kernel(*refs, n_chunks, zero_init):
    single = zero_init and n_chunks == 1
    if zero_init:
        q_ref, k_ref, v_ref, gc_ref, gr_ref, hs_ref, c_ref, n_ref, m_ref, hf_scr, hb_scr = refs
        if not single:
            c_ref[...] = jnp.zeros_like(c_ref)
            n_ref[...] = jnp.zeros_like(n_ref)
            m_ref[...] = jnp.zeros_like(m_ref)
    else:
        (q_ref, k_ref, v_ref, gc_ref, gr_ref, c0_ref, n0_ref, m0_ref,
         hs_ref, c_ref, n_ref, m_ref, hf_scr, hb_scr) = refs
        c_ref[...] = c0_ref[...]
        n_ref[...] = n0_ref[...]
        m_ref[...] = m0_ref[...]
    l = MLSTM_CHUNK
    hh = MLSTM_HEADS
    dk = q_ref.shape[1] // hh
    dv = v_ref.shape[1] // hh

    tt = lax.broadcasted_iota(jnp.int32, (l, l), 0)
    ss = lax.broadcasted_iota(jnp.int32, (l, l), 1)
    masks = (ss <= tt, ss >= tt)
    masks_bf = tuple(jnp.where(m, 1.0, 0.0).astype(BF16) for m in masks)

    def body(c, carry):
        cr = n_chunks - 1 - c
        start = (lambda j: j * l) if single else (lambda j: pl.multiple_of(j * l, l))
        rows = (pl.ds(start(c), l), pl.ds(start(cr), l))
        grs = (gr_ref[c], gr_ref[cr])
        outs, states = ([], []), []
        for d in range(2):
            q, k, v, gc, gr = q_ref[rows[d], :], k_ref[rows[d], :], v_ref[rows[d], :], gc_ref[rows[d], :], grs[d]
            bc, br = _gate_cumsums(gc, gr, masks_bf[d], masks_bf[1 - d])
            for h in range(hh):
                ci, cf = 2 * d * hh + h, (2 * d + 1) * hh + h
                state = (None,) * 3 if single else (c_ref[d, h], n_ref[d, h], m_ref[d, h])
                o, *st = _mlstm_chunk(q[:, h * dk:(h + 1) * dk], k[:, h * dk:(h + 1) * dk],
                                      v[:, h * dv:(h + 1) * dv], gc[:, ci:ci + 1], bc[:, cf:cf + 1],
                                      gr[ci:ci + 1, :], br[cf:cf + 1, :], *state, masks[d], d == 1)
                outs[d].append(o)
                states.append((d, h, st))
        hf_scr[rows[0], :] = jnp.concatenate(outs[0], axis=1)
        hb_scr[rows[1], :] = jnp.concatenate(outs[1], axis=1)
        for d, h, (c_new, n_new, m_new) in states:
            c_ref[d, h], n_ref[d, h], m_ref[d, h] = c_new, n_new, m_new
        return carry

    if single:
        body(0, 0)
    else:
        lax.fori_loop(0, n_chunks, body, 0)
    for h in range(hh):
        cols = pl.ds(h * dv, dv)
        hs = hf_scr[:, cols] + hb_scr[:, cols]
        hs_ref[:, cols] = (hs * lax.rsqrt(jnp.mean(hs * hs, axis=-1, keepdims=True)
                                          + NORM_EPS)).astype(BF16)


def mlstm_scan(qkv, gcol, grow, init, row_off, n_seq, seq_len):
    hh = MLSTM_HEADS
    hv = qkv.shape[1] // 2
    dv = hv // hh
    dk = dv // 2
    l = MLSTM_CHUNK
    nc = seq_len // l
    ob = row_off // seq_len
    kern = functools.partial(_mlstm_scan_kernel, n_chunks=nc, zero_init=init is None)
    st = lambda *tail: pl.BlockSpec((None, 2, hh) + tail, lambda s: (s, 0, 0) + (0,) * len(tail))
    states = [st(dk, dv), st(1, dk), st(1, 1)]
    return pl.pallas_call(
        kern,
        grid=(n_seq,),
        in_specs=[pl.BlockSpec((seq_len, hh * dk), lambda s: (ob + s, 0)),
                  pl.BlockSpec((seq_len, hh * dk), lambda s: (ob + s, 1)),
                  pl.BlockSpec((seq_len, hv), lambda s: (ob + s, 1)),
                  pl.BlockSpec((seq_len, LANES), lambda s: (ob + s, 0)),
                  pl.BlockSpec((nc, 4 * hh, l), lambda s: (ob + s, 0, 0))]
                 + ([] if init is None else states),
        out_specs=[pl.BlockSpec((seq_len, hv), lambda s: (s, 0))] + states,
        out_shape=[jax.ShapeDtypeStruct((n_seq * seq_len, hv), BF16),
                   jax.ShapeDtypeStruct((n_seq, 2, hh, dk, dv), F32),
                   jax.ShapeDtypeStruct((n_seq, 2, hh, 1, dk), F32),
                   jax.ShapeDtypeStruct((n_seq, 2, hh, 1, 1), F32)],
        scratch_shapes=[pltpu.VMEM((seq_len, hv), F32), pltpu.VMEM((seq_len, hv), F32)],
        compiler_params=_params("arbitrary"),
        name="mlstm_scan",
    )(qkv, qkv, qkv, gcol, grow, *(() if init is None else init))


def _mla_proj_kernel(x_ref, g_ref, sh_ref, sc_ref, win_ref, qg_ref, kvg_ref, wqb_ref, cos_ref,
                     sin_ref, q_ref, ckv_ref, kpe_ref, *, q_lora, kv_lora, rope, n_heads):
    h = _modulated(x_ref[...], g_ref[...], sh_ref[...], sc_ref[...])
    proj = _bdot_nt(h, win_ref[...])
    q_lat = proj[:, :q_lora]
    ckv_ref[...] = _rms(proj[:, q_lora:q_lora + kv_lora], kvg_ref[...])
    cos, sin = cos_ref[...], sin_ref[...]
    kpe = proj[:, q_lora + kv_lora:q_lora + kv_lora + rope]
    kpe_rot = proj[:, q_lora + kv_lora + rope:q_lora + kv_lora + 2 * rope]
    kpe_ref[...] = kpe * cos[:, :rope] + kpe_rot * sin[:, :rope]
    q = _bdot_nt(_rms(q_lat, qg_ref[...]), wqb_ref[...])
    n = n_heads * LANES
    pieces = []
    for hd in range(n_heads):
        c = slice(hd * LANES, (hd + 1) * LANES)
        pieces.append(q[:, c])
        pieces.append(q[:, n:2 * n][:, c] * cos + q[:, 2 * n:][:, c] * sin)
    q_ref[...] = jnp.concatenate(pieces, axis=1).astype(BF16)


def _rot_rows(wt):
    n, k = wt.shape
    quarter = MLA_ROPE // 4
    w4 = wt.reshape(n // (2 * quarter), 2, quarter, k)
    return jnp.concatenate([-w4[:, 1:2], w4[:, 0:1]], axis=1).reshape(n, k)


def _rope_tables(n_ctx, dec_batch, dec_seq):
    quarter = MLA_ROPE // 4
    freq = np.power(np.float32(ROPE_BASE), -np.arange(quarter, dtype=np.float32) / np.float32(quarter))
    pos = np.arange(dec_seq)
    ang_r = (pos // GRID_W).astype(np.float32)[:, None] * freq[None, :]
    ang_c = (pos % GRID_W).astype(np.float32)[:, None] * freq[None, :]
    ang = np.concatenate([ang_r, ang_r, ang_c, ang_c], axis=1).astype(np.float32)
    cos = np.concatenate([np.ones((n_ctx, MLA_ROPE), np.float32)] + [np.cos(ang)] * dec_batch, axis=0)
    sin = np.concatenate([np.zeros((n_ctx, MLA_ROPE), np.float32)] + [np.sin(ang)] * dec_batch, axis=0)
    reps = LANES // MLA_ROPE
    return (jnp.asarray(np.tile(cos, (1, reps)), F32), jnp.asarray(np.tile(sin, (1, reps)), F32))


def mla_proj(x, g, shift, scale, w_in, q_g, kv_g, w_qb, cos, sin, n_ctx, dec_seq, tm=512):
    t, d = x.shape
    hh, nope, rope = MLA_HEADS, MLA_NOPE, MLA_ROPE
    q_lora = q_g.shape[1]
    kv_lora = kv_g.shape[1]
    w_in_t = w_in.T
    w_in_ext = jnp.concatenate([w_in_t, _rot_rows(w_in_t[q_lora + kv_lora:])], axis=0)
    w3 = w_qb.T.reshape(hh, nope + rope, q_lora)
    w_qn = w3[:, :nope].reshape(hh * nope, q_lora)
    w_qp = w3[:, nope:].reshape(hh * rope, q_lora)
    assert nope == LANES and rope <= LANES
    head_pad = lambda w: jnp.pad(w.reshape(hh, rope, q_lora),
                                 ((0, 0), (0, LANES - rope), (0, 0))).reshape(hh * LANES, q_lora)
    w_qb_ext = jnp.concatenate([w_qn, head_pad(w_qp), head_pad(_rot_rows(w_qp))], axis=0)
    cidx = _cvec_index(n_ctx, dec_seq, tm)
    mod_spec = pl.BlockSpec((None, 1, d), lambda i: (cidx(i), 0, 0))
    full = lambda a: pl.BlockSpec(a.shape, lambda i: (0,) * a.ndim)
    kern = functools.partial(_mla_proj_kernel, q_lora=q_lora, kv_lora=kv_lora, rope=rope, n_heads=hh)
    return pl.pallas_call(
        kern,
        grid=(t // tm,),
        in_specs=[pl.BlockSpec((tm, d), lambda i: (i, 0)), full(g), mod_spec, mod_spec,
                  full(w_in_ext), full(q_g), full(kv_g), full(w_qb_ext),
                  pl.BlockSpec((tm, LANES), lambda i: (i, 0)),
                  pl.BlockSpec((tm, LANES), lambda i: (i, 0))],
        out_specs=[pl.BlockSpec((tm, 2 * hh * LANES), lambda i: (i, 0)),
                   pl.BlockSpec((tm, kv_lora), lambda i: (i, 0)),
                   pl.BlockSpec((tm, rope), lambda i: (i, 0))],
        out_shape=[jax.ShapeDtypeStruct((t, 2 * hh * LANES), BF16),
                   jax.ShapeDtypeStruct((t, kv_lora), F32),
                   jax.ShapeDtypeStruct((t, rope), F32)],
        compiler_params=_params("arbitrary"),
        name="mla_proj",
    )(x, g, shift, scale, w_in_ext, q_g, kv_g, w_qb_ext, cos, sin)


def _mla_kv_kernel(ckv_ref, kp_ref, w_ref, k_ref, v_ref):
    kv = _bdot(ckv_ref[...], w_ref[...])
    n = v_ref.shape[1]
    kp = kp_ref[...]
    pieces = []
    for hd in range(n // LANES):
        pieces += [kv[:, hd * LANES:(hd + 1) * LANES].astype(BF16), kp]
    k_ref[...] = jnp.concatenate(pieces, axis=1)
    v_ref[...] = kv[:, n:].astype(BF16)


def mla_kv(ckv_all, kp_pad, w_kvb, tm=1024):
    r, kv_lora = ckv_all.shape
    hh, nope, vd = MLA_HEADS, MLA_NOPE, MLA_V
    w3 = w_kvb.reshape(kv_lora, hh, nope + vd)
    w_perm = jnp.concatenate([w3[:, :, :nope].reshape(kv_lora, hh * nope),
                              w3[:, :, nope:].reshape(kv_lora, hh * vd)], axis=1)
    return pl.pallas_call(
        _mla_kv_kernel,
        grid=(r // tm,),
        in_specs=[pl.BlockSpec((tm, kv_lora), lambda i: (i, 0)),
                  pl.BlockSpec((tm, LANES), lambda i: (i, 0)),
                  pl.BlockSpec(w_perm.shape, lambda i: (0, 0))],
        out_specs=[pl.BlockSpec((tm, 2 * hh * LANES), lambda i: (i, 0)),
                   pl.BlockSpec((tm, hh * vd), lambda i: (i, 0))],
        out_shape=[jax.ShapeDtypeStruct((r, 2 * hh * LANES), BF16),
                   jax.ShapeDtypeStruct((r, hh * vd), BF16)],
        compiler_params=_params("arbitrary"),
        name="mla_kv",
    )(ckv_all, kp_pad, w_perm)


def _attn_kernel(q_ref, k_ref, v_ref, o_ref, *, scale):
    c = scale * np.log2(np.e)
    for h in range(MLA_HEADS):
        hk = slice(2 * h * LANES, 2 * (h + 1) * LANES)
        s = _bdot_nt(q_ref[:, hk], k_ref[:, hk])
        e = jnp.exp2((s - jnp.max(s, axis=-1, keepdims=True)) * c)
        o = _bdot(e, v_ref[:, h * MLA_V:(h + 1) * MLA_V]) / jnp.sum(e, axis=-1, keepdims=True)
        o_ref[:, h * MLA_V:(h + 1) * MLA_V] = o.astype(BF16)


def mla_attention(q, k, v, q_row_off, k_row_off, n_seq, q_len, k_len):
    tq = min(ATTN_Q_BLOCK, q_len)
    qb = q_len // tq
    q0 = q_row_off // tq
    k0 = k_row_off // k_len
    dq, dv = q.shape[1], v.shape[1]
    kern = functools.partial(_attn_kernel, scale=(MLA_NOPE + MLA_ROPE) ** -0.5)
    return pl.pallas_call(
        kern,
        grid=(n_seq, qb),
        in_specs=[pl.BlockSpec((tq, dq), lambda s, j: (q0 + s * qb + j, 0)),
                  pl.BlockSpec((k_len, dq), lambda s, j: (k0 + s, 0)),
                  pl.BlockSpec((k_len, dv), lambda s, j: (k0 + s, 0))],
        out_specs=pl.BlockSpec((tq, dv), lambda s, j: (s * qb + j, 0)),
        out_shape=jax.ShapeDtypeStruct((n_seq * q_len, dv), BF16),
        compiler_params=_params("arbitrary", "arbitrary"),
        name="mla_attention",
    )(q, k, v)


def kernel(x_prompt, x_sample, state_mlstm_C, state_mlstm_n, state_mlstm_m, cache_mla_ckv,
           cache_mla_kpe, c, c_ctx, w_ada, b_ada, norm_mix, norm_ffn, norm_final, w_pool,
           pool_scale, w_mlstm_in, b_mlstm_gate, mlstm_head_g, w_mlstm_out, w_mla_in, mla_q_g,
           mla_kv_g, w_mla_qb, w_mla_kvb, w_mla_out, w_router, b_router, w_exp_gate, w_exp_up,
           w_exp_down):
    batch, seq, d = x_prompt.shape
    dec_batch, dec_seq, _ = x_sample.shape
    depth = w_ada.shape[0]
    n_ctx = batch * seq
    n_lat = dec_batch * dec_seq
    hh = MLSTM_HEADS
    past = cache_mla_ckv.shape[2]

    x = (x_prompt.reshape(n_ctx, d), x_sample.reshape(n_lat, d))

    n_cv = 1 + dec_batch
    cvecs = jnp.concatenate([c_ctx[None, :], c, jnp.zeros((SUBLANES - n_cv % SUBLANES, d), F32)], axis=0)
    mod = ada_mod_all(cvecs, w_ada, b_ada).reshape(depth, cvecs.shape[0], 6, 1, d)

    w_router_pad = jnp.pad(w_router, ((0, 0), (0, LANES - N_EXPERTS)))
    w_router_hi = w_router_pad.astype(BF16)
    w_router_lo = (w_router_pad - w_router_hi.astype(F32)).astype(BF16)
    w_router_split = jnp.concatenate([w_router_hi, w_router_lo], axis=1)
    b_router_col = b_router.reshape(N_EXPERTS, 1)
    g_final = norm_final.reshape(1, d)
    row = lambda a: a.reshape(1, -1)

    outs = {}
    for i in range(depth):
        kind, j = i % 3, i // 3
        m = [mod[i, :n_cv, k] for k in range(6)]
        g_mix = row(norm_mix[i])
        if kind == 0:
            x = pool_layer(x, g_mix, m[0], m[1], m[2], w_pool[j], row(pool_scale[j]),
                           n_ctx, seq, dec_seq)
        elif kind == 1:
            qkv, o_gate, gcol, grow = mlstm_proj(x, g_mix, m[0], m[1], w_mlstm_in, j,
                                                 b_mlstm_gate[j], n_ctx, dec_seq)
            t = n_ctx + n_lat
            l = MLSTM_CHUNK
            grow_c = grow.reshape(4 * hh, t // l, l).transpose(1, 0, 2)
            hs_c, c_new, n_new, m_new = mlstm_scan(qkv, gcol, grow_c, None, 0, batch, seq)
            init = (state_mlstm_C[:, j], state_mlstm_n[:, j][:, :, :, None, :],
                    state_mlstm_m[:, j][:, :, :, None, None])
            hs_l, _, _, _ = mlstm_scan(qkv, gcol, grow_c, init, n_ctx, dec_batch, dec_seq)
            outs["C"] = c_new[:, None]
            outs["n"] = n_new[:, None, :, :, 0, :]
            outs["m"] = m_new[:, None, :, :, 0, 0]
            x = mlstm_out(x, hs_c, hs_l, o_gate, row(mlstm_head_g[j]), w_mlstm_out[j], m[2],
                          n_ctx, dec_seq)
        else:
            cos, sin = _rope_tables(n_ctx, dec_batch, dec_seq)
            q_cat, ckv, kpe = mla_proj(x, g_mix, m[0], m[1], w_mla_in[j], row(mla_q_g[j]),
                                       row(mla_kv_g[j]), w_mla_qb[j], cos, sin, n_ctx, dec_seq)
            lat_parts_c, lat_parts_p = [], []
            for b in range(dec_batch):
                lo = n_ctx + b * dec_seq
                lat_parts_c += [cache_mla_ckv[b, j], ckv[lo:lo + dec_seq]]
                lat_parts_p += [cache_mla_kpe[b, j], kpe[lo:lo + dec_seq]]
            ckv_all = jnp.concatenate(lat_parts_c + [ckv[:n_ctx]], axis=0)
            kp_all = jnp.concatenate(lat_parts_p + [kpe[:n_ctx]], axis=0).astype(BF16)
            kp_pad = jnp.pad(kp_all, ((0, 0), (0, LANES - kp_all.shape[1])))
            k_cat, v = mla_kv(ckv_all, kp_pad, w_mla_kvb[j])
            k_lat = past + dec_seq
            o_c = mla_attention(q_cat, k_cat, v, 0, dec_batch * k_lat, batch, seq, seq)
            o_l = mla_attention(q_cat, k_cat, v, n_ctx, 0, dec_batch, dec_seq, k_lat)
            outs["ckv"] = ckv[:n_ctx].reshape(batch, 1, seq, -1)
            outs["kpe"] = kpe[:n_ctx].reshape(batch, 1, seq, -1)
            x = resid_proj(x, o_c, o_l, w_mla_out[j], m[2], n_ctx, dec_seq)
        x = moe_layer(x, row(norm_ffn[i]), m[3], m[4], m[5], w_router_split, b_router_col,
                      w_exp_gate, w_exp_up, w_exp_down, i, g_final, n_ctx, dec_seq,
                      final_norm=(i == depth - 1), split_output=(i == depth - 1))

    y_prompt = x[0].reshape(batch, seq, d)
    y_sample = x[1].reshape(dec_batch, dec_seq, d)
    return (y_prompt, y_sample, outs["C"], outs["n"], outs["m"], outs["ckv"], outs["kpe"])
```

```python
import functools

import numpy as np
import jax
import jax.numpy as jnp
from jax import lax
from jax.experimental import pallas as pl
from jax.experimental.pallas import tpu as pltpu

F32 = jnp.float32
BF16 = jnp.bfloat16

NORM_EPS = 1e-6
GRID_W = 64
POOL_WINDOWS = (2, 4, 8, 16)
MLSTM_HEADS = 4
MLSTM_CHUNK = 256
MLA_HEADS = 8
MLA_NOPE = 128
MLA_ROPE = 64
MLA_V = 128
ROPE_BASE = 10000.0
N_EXPERTS = 16
N_EXPERT_GROUPS = 4
EXPERTS_PER_GROUP = N_EXPERTS // N_EXPERT_GROUPS

LANES = 128
SUBLANES = 8
VMEM_LIMIT = 56 * 1024 * 1024
MOE_VMEM_LIMIT = 60 * 1024 * 1024
POOL_TILE = 256
POOL_HALO = 8
ATTN_Q_BLOCK = 256
MOE_TILE = 512
MOE_SEG_ALIGN = 16
MOE_ROW_BLOCK = 176
MOE_TAIL_BLOCK = 64
MOE_LOAD_EXPERTS = 2


def _params(*sem, vmem_limit=VMEM_LIMIT):
    return pltpu.CompilerParams(dimension_semantics=sem, vmem_limit_bytes=vmem_limit)


def _rms(x, g):
    return x * lax.rsqrt(jnp.mean(x * x, axis=-1, keepdims=True) + NORM_EPS) * g


def _modulated(x, g, shift, scale):
    return _rms(x, g) * (1.0 + scale) + shift


def _silu(x):
    return x * jax.nn.sigmoid(x)


def _bdot(a, b):
    return jnp.dot(a.astype(BF16), b.astype(BF16), preferred_element_type=F32)


def _bdot_nt(a, b):
    return lax.dot_general(a.astype(BF16), b.astype(BF16), (((1,), (1,)), ((), ())),
                           preferred_element_type=F32)


def _bdot_tn(a, b):
    return lax.dot_general(a.astype(BF16), b.astype(BF16), (((0,), (0,)), ((), ())),
                           preferred_element_type=F32)


def _cvec_index(n_ctx, dec_seq, tm):
    def idx(i):
        r = i * tm
        return jnp.where(r < n_ctx, 0, (r - n_ctx) // dec_seq + 1)
    return idx


def _ada_kernel(c_ref, w_ref, b_ref, o_ref):
    o_ref[...] = _bdot(_silu(c_ref[...]), w_ref[...]) + b_ref[...]


def ada_mod_all(cvecs, w_ada, b_ada, tn=3072):
    depth, d, n6 = w_ada.shape
    rows = cvecs.shape[0]
    return pl.pallas_call(
        _ada_kernel,
        grid=(depth, n6 // tn),
        in_specs=[pl.BlockSpec((rows, d), lambda l, n: (0, 0)),
                  pl.BlockSpec((None, d, tn), lambda l, n: (l, 0, n)),
                  pl.BlockSpec((None, 1, tn), lambda l, n: (l, 0, n))],
        out_specs=pl.BlockSpec((None, rows, tn), lambda l, n: (l, 0, n)),
        out_shape=jax.ShapeDtypeStruct((depth, rows, n6), F32),
        compiler_params=_params("arbitrary", "arbitrary"),
        name="ada_mod",
    )(cvecs, w_ada, b_ada.reshape(depth, 1, n6))


def _pool_kernel(*refs, n_ctx_tiles, ctx_seq_tiles, lat_seq_tiles, split_input):
    n_x = 6 if split_input else 3
    x_refs = refs[:n_x]
    g_ref, sh_ref, sc_ref, gt_ref, wp_ref, ps_ref, o_ref, buf_ref, *lvl_refs = refs[n_x:]
    i = pl.program_id(0)
    is_ctx = i < n_ctx_tiles
    j = jnp.where(is_ctx, i % ctx_seq_tiles, (i - n_ctx_tiles) % lat_seq_tiles)
    nt = jnp.where(is_ctx, ctx_seq_tiles, lat_seq_tiles)
    g, sh, sc = g_ref[...], sh_ref[...], sc_ref[...]
    tp, hl = POOL_TILE, POOL_HALO
    gw = o_ref.shape[1] // len(POOL_WINDOWS)

    def fill(xc_ref, xp_ref, xn_ref):
        buf_ref[pl.ds(0, hl), :] = jnp.where(j == 0, 0.0, _modulated(xp_ref[...], g, sh, sc))
        buf_ref[pl.ds(hl, tp), :] = _modulated(xc_ref[...], g, sh, sc)
        buf_ref[pl.ds(hl + tp, hl), :] = jnp.where(j == nt - 1, 0.0,
                                                    _modulated(xn_ref[...], g, sh, sc))
        o_ref[...] = xc_ref[...]

    if split_input:
        pl.when(is_ctx)(lambda: fill(*x_refs[:3]))
        pl.when(jnp.logical_not(is_ctx))(lambda: fill(*x_refs[3:]))
    else:
        fill(*x_refs)

    rows = tp + 2 * hl
    buf_ref[pl.ds(rows, hl), :] = jnp.zeros((hl, buf_ref.shape[1]), F32)
    for ref in lvl_refs:
        ref[pl.ds(rows, hl), :] = jnp.zeros((hl, gw), F32)
    pos = j * tp + lax.broadcasted_iota(jnp.int32, (tp, 1), 0)
    seq_len = nt * tp
    for gi, w in enumerate(POOL_WINDOWS):
        cols = pl.ds(gi * gw, gw)
        src, src_cols = buf_ref, cols
        for lvl in range(1, w.bit_length() - 1):
            step = 1 << (lvl - 1)
            dst = lvl_refs[lvl - 1]
            dst[pl.ds(0, rows), :] = src[pl.ds(0, rows), src_cols] + src[pl.ds(step, rows), src_cols]
            src, src_cols = dst, pl.ds(0, gw)
        acc = src[pl.ds(hl - w // 2, tp), src_cols] + src[pl.ds(hl, tp), src_cols]
        cnt = jnp.minimum(pos + w // 2, seq_len) - jnp.maximum(pos - w // 2, 0)
        pooled = acc / cnt.astype(F32) - buf_ref[pl.ds(hl, tp), cols]
        y = _bdot(pooled, wp_ref[gi]) * ps_ref[:, cols]
        o_ref[:, cols] = o_ref[:, cols] + gt_ref[:, cols] * y


def _halo_specs(tp, hl, d, tile_off, n_rows):
    hb = tp // hl
    last_tile, last_hblk = n_rows // tp - 1, n_rows // hl - 1
    tile = lambda i: jnp.clip(i - tile_off, 0, last_tile)
    return [pl.BlockSpec((tp, d), lambda i: (tile(i), 0)),
            pl.BlockSpec((hl, d), lambda i: (jnp.clip(tile(i) * hb - 1, 0, last_hblk), 0)),
            pl.BlockSpec((hl, d), lambda i: (jnp.clip((tile(i) + 1) * hb, 0, last_hblk), 0))]


def pool_layer(xs, g, shift, scale, gate, w_pool, pool_scale, n_ctx, seq, dec_seq):
    split = isinstance(xs, tuple)
    tp, hl = POOL_TILE, POOL_HALO
    if split:
        d = xs[0].shape[1]
        t = xs[0].shape[0] + xs[1].shape[0]
        x_specs = (_halo_specs(tp, hl, d, 0, xs[0].shape[0])
                   + _halo_specs(tp, hl, d, n_ctx // tp, xs[1].shape[0]))
        x_args = (xs[0],) * 3 + (xs[1],) * 3
    else:
        t, d = xs.shape
        x_specs = _halo_specs(tp, hl, d, 0, t)
        x_args = (xs,) * 3
    cidx = _cvec_index(n_ctx, dec_seq, tp)
    mod_spec = pl.BlockSpec((None, 1, d), lambda i: (cidx(i), 0, 0))
    row_spec = pl.BlockSpec((1, d), lambda i: (0, 0))
    assert all(w & (w - 1) == 0 and w // 2 <= hl for w in POOL_WINDOWS)
    n_levels = max(POOL_WINDOWS).bit_length() - 2
    kern = functools.partial(_pool_kernel, n_ctx_tiles=n_ctx // tp, ctx_seq_tiles=seq // tp,
                             lat_seq_tiles=dec_seq // tp, split_input=split)
    return pl.pallas_call(
        kern,
        grid=(t // tp,),
        in_specs=x_specs + [row_spec, mod_spec, mod_spec, mod_spec,
                            pl.BlockSpec(w_pool.shape, lambda i: (0, 0, 0)), row_spec],
        out_specs=pl.BlockSpec((tp, d), lambda i: (i, 0)),
        out_shape=jax.ShapeDtypeStruct((t, d), F32),
        scratch_shapes=[pltpu.VMEM((tp + 3 * hl, d), F32)]
        + [pltpu.VMEM((tp + 3 * hl, d // len(POOL_WINDOWS)), F32)] * n_levels,
        compiler_params=_params("arbitrary"),
        name="pool_mixer",
    )(*x_args, g, shift, scale, gate, w_pool, pool_scale)


def _route(sel, scores):
    e, tm = sel.shape
    row = lax.broadcasted_iota(jnp.int32, (e, tm), 0)
    best = jnp.zeros((1, tm), jnp.int32)
    best_sc = None
    for gidx in range(N_EXPERT_GROUPS):
        r = [sel[gidx * EXPERTS_PER_GROUP + k:gidx * EXPERTS_PER_GROUP + k + 1, :]
             for k in range(EXPERTS_PER_GROUP)]
        top2 = None
        for a in range(EXPERTS_PER_GROUP):
            for b in range(a + 1, EXPERTS_PER_GROUP):
                s = r[a] + r[b]
                top2 = s if top2 is None else jnp.maximum(top2, s)
        if best_sc is None:
            best_sc = top2
        else:
            better = top2 > best_sc
            best = jnp.where(better, gidx, best)
            best_sc = jnp.where(better, top2, best_sc)
    neg = -jnp.inf
    masked = jnp.where(row // EXPERTS_PER_GROUP == best, sel, neg)
    m1 = jnp.max(masked, axis=0, keepdims=True)
    i1 = jnp.min(jnp.where(masked == m1, row, e), axis=0, keepdims=True)
    masked2 = jnp.where(row == i1, neg, masked)
    m2 = jnp.max(masked2, axis=0, keepdims=True)
    i2 = jnp.min(jnp.where(masked2 == m2, row, e), axis=0, keepdims=True)
    hot1 = row == i1
    hot2 = row == i2
    w1 = jnp.sum(jnp.where(hot1, scores, 0.0), axis=0, keepdims=True)
    w2 = jnp.sum(jnp.where(hot2, scores, 0.0), axis=0, keepdims=True)
    tot = w1 + w2
    return best, jnp.where(hot1, w1 / tot, 0.0) + jnp.where(hot2, w2 / tot, 0.0)


def _split_bf16(a, parts):
    out = []
    for _ in range(parts):
        p = a.astype(BF16)
        out.append(p)
        a = a - p.astype(F32)
    return out


def _pad_rows(a, rows):
    return jnp.concatenate([a, jnp.zeros((rows - a.shape[0], a.shape[1]), a.dtype)], axis=0)


def _moe_kernel(x_ref, g_ref, sh_ref, sc_ref, gt_ref, wr_ref, br_ref, wgf_ref, wuf_ref, wdf_ref,
                gf_ref, *rest, final_norm, n_ctx_tiles, n_experts):
    out_refs, (wgu_ref, wd_ref), scratch = rest[:-5], rest[-5:-3], rest[-3:]
    i = pl.program_id(0)
    per_step, f = wgf_ref.shape[0], wgf_ref.shape[2]
    n_load = n_experts // per_step
    for s in range(n_load):
        @pl.when(i == s)
        def _(s=s):
            for j in range(per_step):
                gi, k = divmod(s * per_step + j, EXPERTS_PER_GROUP)
                wgu_ref[gi, :, pl.ds(2 * k * f, f)] = wgf_ref[j].astype(BF16)
                wgu_ref[gi, :, pl.ds((2 * k + 1) * f, f)] = wuf_ref[j].astype(BF16)
                wd_ref[gi, pl.ds(k * f, f), :] = wdf_ref[j].astype(BF16)

    @pl.when(i >= n_load)
    def _():
        _moe_tile(x_ref, g_ref, sh_ref, sc_ref, gt_ref, wr_ref, br_ref, wgu_ref, wd_ref,
                  gf_ref, out_refs, scratch, i - n_load, final_norm, n_ctx_tiles)


def _moe_tile(x_ref, g_ref, sh_ref, sc_ref, gt_ref, wr_ref, br_ref, wgu_ref, wd_ref,
              gf_ref, out_refs, scratch, tile, final_norm, n_ctx_tiles):
    hp_scr, cw_scr, yp_scr = scratch
    tr, d = x_ref.shape
    trp = hp_scr.shape[0]
    ng, eg = N_EXPERT_GROUPS, EXPERTS_PER_GROUP
    x = x_ref[...]
    h = _modulated(x, g_ref[...], sh_ref[...], sc_ref[...])
    hb = h.astype(BF16)

    h_lo = (h - hb.astype(F32)).astype(BF16)
    wr = wr_ref[...]
    lg = jnp.dot(hb, wr, preferred_element_type=F32)
    logits = lg[:, :LANES] + lg[:, LANES:] + jnp.dot(h_lo, wr[:, :LANES], preferred_element_type=F32)
    scores = jax.nn.sigmoid(logits.T[:N_EXPERTS, :])
    best, comb_t = _route(scores + br_ref[...], scores)

    grp = lax.broadcasted_iota(jnp.int32, (SUBLANES, tr), 0)
    hot_t = (grp == best).astype(F32)
    cw_t = hot_t[0:1, :] * comb_t[0:eg, :]
    for gi in range(1, ng):
        cw_t = cw_t + hot_t[gi:gi + 1, :] * comb_t[gi * eg:(gi + 1) * eg, :]
    cw_c = _pad_rows(cw_t, LANES).T

    ia = lax.broadcasted_iota(jnp.int32, (tr, tr), 0)
    ib = lax.broadcasted_iota(jnp.int32, (tr, tr), 1)
    before = jnp.where(ia < ib, 1.0, 0.0).astype(BF16)
    rank_t = jnp.dot(hot_t.astype(BF16), before, preferred_element_type=F32)

    starts, counts = [], []
    off = jnp.int32(0)
    for gi in range(ng):
        n = jnp.sum(hot_t[gi:gi + 1, :]).astype(jnp.int32)
        n = ((n + MOE_SEG_ALIGN - 1) // MOE_SEG_ALIGN) * MOE_SEG_ALIGN
        starts.append(off)
        counts.append(n)
        off = off + n

    pos_t = hot_t[0:1, :] * (rank_t[0:1, :] + starts[0].astype(F32))
    for gi in range(1, ng):
        pos_t = pos_t + hot_t[gi:gi + 1, :] * (rank_t[gi:gi + 1, :] + starts[gi].astype(F32))
    pos_c = _pad_rows(pos_t, LANES).T[:, 0:1]
    used = tr + ng * MOE_SEG_ALIGN
    perm = jnp.where(lax.broadcasted_iota(jnp.int32, (used, tr), 0) == pos_t.astype(jnp.int32),
                     1.0, 0.0).astype(BF16)
    perm_t = jnp.where(lax.broadcasted_iota(jnp.int32, (tr, used), 1) == pos_c.astype(jnp.int32),
                       1.0, 0.0).astype(BF16)

    hp_scr[pl.ds(0, used), :] = jnp.dot(perm, hb, preferred_element_type=F32).astype(BF16)
    hp_scr[pl.ds(used, trp - used), :] = jnp.zeros((trp - used, d), BF16)
    cw_pair = jnp.dot(perm, jnp.concatenate(_split_bf16(cw_c, 2), axis=1), preferred_element_type=F32)
    cw_scr[pl.ds(0, used), :] = cw_pair[:, :LANES] + cw_pair[:, LANES:]
    cw_scr[pl.ds(used, trp - used), :] = jnp.zeros((trp - used, LANES), F32)
    yp_scr[...] = jnp.zeros_like(yp_scr)

    f = wd_ref.shape[1] // eg

    def expert_block(gi, row0, n_rows, seg_end=None):
        rows = pl.ds(pl.multiple_of(row0, MOE_SEG_ALIGN), n_rows)
        cwb = cw_scr[rows, :]
        gu = jnp.dot(hp_scr[rows, :], wgu_ref[gi], preferred_element_type=F32)
        hid = [(_silu(gu[:, 2 * k * f:(2 * k + 1) * f]) * gu[:, (2 * k + 1) * f:(2 * k + 2) * f]
                * cwb[:, k:k + 1]).astype(BF16) for k in range(eg)]
        y = jnp.dot(jnp.concatenate(hid, axis=1), wd_ref[gi], preferred_element_type=F32).astype(BF16)
        if seg_end is None:
            yp_scr[rows, :] = y
        else:
            inside = row0 + lax.broadcasted_iota(jnp.int32, (n_rows, 1), 0) < seg_end
            yp_scr[rows, :] = jnp.where(inside, y, yp_scr[rows, :])

    sb, tb = MOE_ROW_BLOCK, MOE_TAIL_BLOCK
    for gi in range(ng):
        expert_block(gi, starts[gi], sb)
    for gi in range(ng):
        def tail(b, carry, gi=gi):
            expert_block(gi, starts[gi] + sb + b * tb, tb, starts[gi] + counts[gi])
            return carry
        lax.fori_loop(0, (jnp.maximum(counts[gi] - sb, 0) + tb - 1) // tb, tail, 0)

    moe = jnp.dot(perm_t, yp_scr[pl.ds(0, used), :], preferred_element_type=F32)
    out = x + gt_ref[...] * moe
    if final_norm:
        out = _rms(out, gf_ref[...])
    if n_ctx_tiles is None:
        out_refs[0][...] = out
    else:
        is_ctx = tile < n_ctx_tiles

        @pl.when(is_ctx)
        def _():
            out_refs[0][...] = out

        @pl.when(jnp.logical_not(is_ctx))
        def _():
            out_refs[1][...] = out


def moe_layer(x, g, shift, scale, gate, w_router_split, b_router_col, wg, wu, wd, layer, g_final,
              n_ctx, dec_seq, final_norm, split_output=False, tm=MOE_TILE):
    t, d = x.shape
    _, n_e, _, f = wg.shape
    trp = tm + N_EXPERT_GROUPS * MOE_SEG_ALIGN + max(MOE_ROW_BLOCK, MOE_TAIL_BLOCK)
    per_step = MOE_LOAD_EXPERTS
    n_load = n_e // per_step
    tile = lambda i: jnp.maximum(i - n_load, 0)
    cidx = _cvec_index(n_ctx, dec_seq, tm)
    mod_spec = pl.BlockSpec((None, 1, d), lambda i: (cidx(tile(i)), 0, 0))
    row_spec = pl.BlockSpec((1, d), lambda i: (0, 0))
    expert = lambda i: jnp.minimum(i, n_load - 1)
    nct = n_ctx // tm
    if split_output:
        out_specs = [pl.BlockSpec((tm, d), lambda i: (jnp.minimum(tile(i), nct - 1), 0)),
                     pl.BlockSpec((tm, d), lambda i: (jnp.maximum(tile(i) - nct, 0), 0))]
        out_shape = [jax.ShapeDtypeStruct((n_ctx, d), F32), jax.ShapeDtypeStruct((t - n_ctx, d), F32)]
    else:
        out_specs = pl.BlockSpec((tm, d), lambda i: (tile(i), 0))
        out_shape = jax.ShapeDtypeStruct((t, d), F32)
    return pl.pallas_call(
        functools.partial(_moe_kernel, final_norm=final_norm,
                          n_ctx_tiles=nct if split_output else None, n_experts=n_e),
        grid=(n_load + t // tm,),
        in_specs=[pl.BlockSpec((tm, d), lambda i: (tile(i), 0)),
                  row_spec, mod_spec, mod_spec, mod_spec,
                  pl.BlockSpec(w_router_split.shape, lambda i: (0, 0)),
                  pl.BlockSpec(b_router_col.shape, lambda i: (0, 0)),
                  pl.BlockSpec((None, per_step, d, f), lambda i: (layer, expert(i), 0, 0)),
                  pl.BlockSpec((None, per_step, d, f), lambda i: (layer, expert(i), 0, 0)),
                  pl.BlockSpec((None, per_step, f, d), lambda i: (layer, expert(i), 0, 0)),
                  row_spec],
        out_specs=out_specs,
        out_shape=out_shape,
        scratch_shapes=[pltpu.VMEM((N_EXPERT_GROUPS, d, 2 * EXPERTS_PER_GROUP * f), BF16),
                        pltpu.VMEM((N_EXPERT_GROUPS, EXPERTS_PER_GROUP * f, d), BF16),
                        pltpu.VMEM((trp, d), BF16), pltpu.VMEM((trp, LANES), F32),
                        pltpu.VMEM((trp, d), BF16)],
        compiler_params=_params("arbitrary", vmem_limit=MOE_VMEM_LIMIT),
        name="moe",
    )(x, g, shift, scale, gate, w_router_split, b_router_col, wg, wu, wd, g_final)


def _per_stream(n_ctx_tiles, ctx_ref, lat_ref, fn):
    i = pl.program_id(0)

    @pl.when(i < n_ctx_tiles)
    def _():
        fn(ctx_ref[...])

    @pl.when(i >= n_ctx_tiles)
    def _():
        fn(lat_ref[...])


def _stream_specs(tm, k, n_ctx_tiles):
    return [pl.BlockSpec((tm, k), lambda i: (jnp.minimum(i, n_ctx_tiles - 1), 0)),
            pl.BlockSpec((tm, k), lambda i: (jnp.maximum(i - n_ctx_tiles, 0), 0))]


def _resid_proj_kernel(x_ref, ac_ref, al_ref, w_ref, gt_ref, o_ref, *, n_ctx_tiles):
    def run(a):
        o_ref[...] = x_ref[...] + gt_ref[...] * _bdot(a, w_ref[...])
    _per_stream(n_ctx_tiles, ac_ref, al_ref, run)


def _mlstm_out_kernel(x_ref, hc_ref, hl_ref, og_ref, hg_ref, w_ref, gt_ref, o_ref, *, n_ctx_tiles):
    def run(hs):
        a = jax.nn.sigmoid(og_ref[...].astype(F32)) * (hs.astype(F32) * hg_ref[...])
        o_ref[...] = x_ref[...] + gt_ref[...] * _bdot(a, w_ref[...])
    _per_stream(n_ctx_tiles, hc_ref, hl_ref, run)


def resid_proj(x, a_ctx, a_lat, w, gate, n_ctx, dec_seq, tm=1024):
    t, d = x.shape
    k = a_ctx.shape[1]
    cidx = _cvec_index(n_ctx, dec_seq, tm)
    nct = n_ctx // tm
    return pl.pallas_call(
        functools.partial(_resid_proj_kernel, n_ctx_tiles=nct),
        grid=(t // tm,),
        in_specs=[pl.BlockSpec((tm, d), lambda i: (i, 0))] + _stream_specs(tm, k, nct) + [
            pl.BlockSpec((k, d), lambda i: (0, 0)),
            pl.BlockSpec((None, 1, d), lambda i: (cidx(i), 0, 0))],
        out_specs=pl.BlockSpec((tm, d), lambda i: (i, 0)),
        out_shape=jax.ShapeDtypeStruct((t, d), F32),
        compiler_params=_params("arbitrary"),
        name="resid_proj",
    )(x, a_ctx, a_lat, w, gate)


def mlstm_out(x, hs_ctx, hs_lat, o_gate, head_g, w, gate, n_ctx, dec_seq, tm=1024):
    t, d = x.shape
    k = hs_ctx.shape[1]
    cidx = _cvec_index(n_ctx, dec_seq, tm)
    nct = n_ctx // tm
    return pl.pallas_call(
        functools.partial(_mlstm_out_kernel, n_ctx_tiles=nct),
        grid=(t // tm,),
        in_specs=[pl.BlockSpec((tm, d), lambda i: (i, 0))] + _stream_specs(tm, k, nct) + [
            pl.BlockSpec((tm, k), lambda i: (i, 0)),
            pl.BlockSpec((1, k), lambda i: (0, 0)),
            pl.BlockSpec((k, d), lambda i: (0, 0)),
            pl.BlockSpec((None, 1, d), lambda i: (cidx(i), 0, 0))],
        out_specs=pl.BlockSpec((tm, d), lambda i: (i, 0)),
        out_shape=jax.ShapeDtypeStruct((t, d), F32),
        compiler_params=_params("arbitrary"),
        name="mlstm_out",
    )(x, hs_ctx, hs_lat, o_gate, head_g, w, gate)


def _mlstm_proj_kernel(x_ref, g_ref, sh_ref, sc_ref, wqk_ref, wv_ref, wo_ref, wgt_ref, bg_ref,
                       qs_ref, qkv_ref, o_ref, gc_ref, gr_ref):
    h = _modulated(x_ref[...], g_ref[...], sh_ref[...], sc_ref[...]).astype(BF16)
    nqk = wqk_ref.shape[0]
    qkv_ref[:, :nqk] = (_bdot_nt(h, wqk_ref[...]) * qs_ref[...]).astype(BF16)
    qkv_ref[:, nqk:] = _bdot_nt(h, wv_ref[...]).astype(BF16)
    o_ref[...] = _bdot_nt(h, wo_ref[...]).astype(BF16)
    gates = _bdot_nt(h, wgt_ref[...]) + bg_ref[...]
    gc_ref[...] = gates
    gr_ref[...] = gates.T[:gr_ref.shape[0], :]


def mlstm_proj(x, g, shift, scale, w_in_all, layer, b_gate, n_ctx, dec_seq, tm=512):
    t, d = x.shape
    hh = MLSTM_HEADS
    hv = d
    hk = hv // 2
    ng = 4 * hh
    w_t_all = jnp.swapaxes(w_in_all, 1, 2)
    w_g_pad = jnp.pad(w_t_all[layer, 2 * hk + 2 * hv:, :], ((0, LANES - ng), (0, 0)))
    b_pad = jnp.pad(b_gate.reshape(1, ng), ((0, 0), (0, LANES - ng)))
    dk = hk // hh
    q_scale = jnp.concatenate([jnp.full((1, hk), dk ** -0.5, F32), jnp.ones((1, hk), F32)], axis=1)
    cidx = _cvec_index(n_ctx, dec_seq, tm)
    mod_spec = pl.BlockSpec((None, 1, d), lambda i: (cidx(i), 0, 0))
    full = lambda a: pl.BlockSpec(a.shape, lambda i: (0,) * a.ndim)
    assert 2 * hk == hv
    w_col = lambda n: pl.BlockSpec((None, hv, d), lambda i: (layer, n, 0))
    return pl.pallas_call(
        _mlstm_proj_kernel,
        grid=(t // tm,),
        in_specs=[pl.BlockSpec((tm, d), lambda i: (i, 0)), full(g), mod_spec, mod_spec,
                  w_col(0), w_col(1), w_col(2), full(w_g_pad), full(b_pad), full(q_scale)],
        out_specs=[pl.BlockSpec((tm, 2 * hk + hv), lambda i: (i, 0)),
                   pl.BlockSpec((tm, hv), lambda i: (i, 0)),
                   pl.BlockSpec((tm, LANES), lambda i: (i, 0)),
                   pl.BlockSpec((ng, tm), lambda i: (0, i))],
        out_shape=[jax.ShapeDtypeStruct((t, 2 * hk + hv), BF16),
                   jax.ShapeDtypeStruct((t, hv), BF16),
                   jax.ShapeDtypeStruct((t, LANES), F32),
                   jax.ShapeDtypeStruct((ng, t), F32)],
        compiler_params=_params("arbitrary"),
        name="mlstm_proj",
    )(x, g, shift, scale, w_t_all, w_t_all, w_t_all, w_g_pad, b_pad, q_scale)


def _log_sigmoid(x):
    return jnp.minimum(x, 0.0) - jnp.log(1.0 + jnp.exp(-jnp.abs(x)))


def _gate_cumsums(gc, gr, causal_bf, feeds_bf):
    bc = br = None
    for part in _split_bf16(_log_sigmoid(gc), 3):
        t = jnp.dot(causal_bf, part, preferred_element_type=F32)
        bc = t if bc is None else bc + t
    for part in _split_bf16(_log_sigmoid(gr), 3):
        t = jnp.dot(part, feeds_bf, preferred_element_type=F32)
        br = t if br is None else br + t
    return bc, br


def _mlstm_chunk(q, k, v, i_col, b_col, i_row, b_row, c_st, n_st, m_st, causal, rev):
    l = q.shape[0]
    zero_state = c_st is None
    if zero_state:
        m_st = 0.0
    g_row = i_row - b_row
    log_w = jnp.where(causal, g_row, -jnp.inf)
    c_t = jnp.maximum(m_st, jnp.max(log_w, axis=1, keepdims=True))
    dw = jnp.exp(log_w - c_t)
    a = _bdot_nt(q, k) * dw
    a_hi, a_lo = _split_bf16(a, 2)
    ones = jnp.ones((l, LANES), BF16)
    den = (jnp.dot(a_hi, ones, preferred_element_type=F32)
           + jnp.dot(a_lo, ones, preferred_element_type=F32))[:, 0:1]
    if not zero_state:
        sw = jnp.exp(m_st - c_t)
        n_hi, n_lo = _split_bf16(jnp.broadcast_to(n_st, (LANES, n_st.shape[1])), 2)
        den = den + sw * (_bdot_nt(q, n_hi) + _bdot_nt(q, n_lo))[:, 0:1]
    inv = 1.0 / jnp.maximum(jnp.abs(den), jnp.exp(-(b_col + c_t)))
    h = inv * jnp.dot(a_hi, v, preferred_element_type=F32)
    if not zero_state:
        h = h + (sw * inv) * _bdot(q, c_st)
    b_last = b_col[0:1, :] if rev else b_col[l - 1:l, :]
    log_k = b_last - b_col + i_col
    m_new = jnp.maximum(b_last + m_st, jnp.max(log_k, axis=0, keepdims=True))
    kw = jnp.exp(log_k - m_new)
    kwk = kw * k.astype(F32)
    c_new = _bdot_tn(kwk, v)
    n_new = jnp.sum(kwk, axis=0, keepdims=True)
    if not zero_state:
        decay = jnp.exp(b_last + m_st - m_new)
        c_new = decay * c_st + c_new
        n_new = decay * n_st + n_new
    return h, c_new, n_new, m_new


def _mlstm_scan_kernel(*refs, n_chunks, zero_init):
    single = zero_init and n_chunks == 1
    if zero_init:
        q_ref, k_ref, v_ref, gc_ref, gr_ref, hs_ref, c_ref, n_ref, m_ref, hf_scr, hb_scr = refs
        if not single:
            c_ref[...] = jnp.zeros_like(c_ref)
            n_ref[...] = jnp.zeros_like(n_ref)
            m_ref[...] = jnp.zeros_like(m_ref)
    else:
        (q_ref, k_ref, v_ref, gc_ref, gr_ref, c0_ref, n0_ref, m0_ref,
         hs_ref, c_ref, n_ref, m_ref, hf_scr, hb_scr) = refs
        c_ref[...] = c0_ref[...]
        n_ref[...] = n0_ref[...]
        m_ref[...] = m0_ref[...]
    l = MLSTM_CHUNK
    hh = MLSTM_HEADS
    dk = q_ref.shape[1] // hh
    dv = v_ref.shape[1] // hh

    tt = lax.broadcasted_iota(jnp.int32, (l, l), 0)
    ss = lax.broadcasted_iota(jnp.int32, (l, l), 1)
    masks = (ss <= tt, ss >= tt)
    masks_bf = tuple(jnp.where(m, 1.0, 0.0).astype(BF16) for m in masks)

    def body(c, carry):
        cr = n_chunks - 1 - c
        start = (lambda j: j * l) if single else (lambda j: pl.multiple_of(j * l, l))
        rows = (pl.ds(start(c), l), pl.ds(start(cr), l))
        grs = (gr_ref[c], gr_ref[cr])
        outs, states = ([], []), []
        for d in range(2):
            q, k, v, gc, gr = q_ref[rows[d], :], k_ref[rows[d], :], v_ref[rows[d], :], gc_ref[rows[d], :], grs[d]
            bc, br = _gate_cumsums(gc, gr, masks_bf[d], masks_bf[1 - d])
            for h in range(hh):
                ci, cf = 2 * d * hh + h, (2 * d + 1) * hh + h
                state = (None,) * 3 if single else (c_ref[d, h], n_ref[d, h], m_ref[d, h])
                o, *st = _mlstm_chunk(q[:, h * dk:(h + 1) * dk], k[:, h * dk:(h + 1) * dk],
                                      v[:, h * dv:(h + 1) * dv], gc[:, ci:ci + 1], bc[:, cf:cf + 1],
                                      gr[ci:ci + 1, :], br[cf:cf + 1, :], *state, masks[d], d == 1)
                outs[d].append(o)
                states.append((d, h, st))
        hf_scr[rows[0], :] = jnp.concatenate(outs[0], axis=1)
        hb_scr[rows[1], :] = jnp.concatenate(outs[1], axis=1)
        for d, h, (c_new, n_new, m_new) in states:
            c_ref[d, h], n_ref[d, h], m_ref[d, h] = c_new, n_new, m_new
        return carry

    if single:
        body(0, 0)
    else:
        lax.fori_loop(0, n_chunks, body, 0)
    for h in range(hh):
        cols = pl.ds(h * dv, dv)
        hs = hf_scr[:, cols] + hb_scr[:, cols]
        hs_ref[:, cols] = (hs * lax.rsqrt(jnp.mean(hs * hs, axis=-1, keepdims=True)
                                          + NORM_EPS)).astype(BF16)


def mlstm_scan(qkv, gcol, grow, init, row_off, n_seq, seq_len):
    hh = MLSTM_HEADS
    hv = qkv.shape[1] // 2
    dv = hv // hh
    dk = dv // 2
    l = MLSTM_CHUNK
    nc = seq_len // l
    ob = row_off // seq_len
    kern = functools.partial(_mlstm_scan_kernel, n_chunks=nc, zero_init=init is None)
    st = lambda *tail: pl.BlockSpec((None, 2, hh) + tail, lambda s: (s, 0, 0) + (0,) * len(tail))
    states = [st(dk, dv), st(1, dk), st(1, 1)]
    return pl.pallas_call(
        kern,
        grid=(n_seq,),
        in_specs=[pl.BlockSpec((seq_len, hh * dk), lambda s: (ob + s, 0)),
                  pl.BlockSpec((seq_len, hh * dk), lambda s: (ob + s, 1)),
                  pl.BlockSpec((seq_len, hv), lambda s: (ob + s, 1)),
                  pl.BlockSpec((seq_len, LANES), lambda s: (ob + s, 0)),
                  pl.BlockSpec((nc, 4 * hh, l), lambda s: (ob + s, 0, 0))]
                 + ([] if init is None else states),
        out_specs=[pl.BlockSpec((seq_len, hv), lambda s: (s, 0))] + states,
        out_shape=[jax.ShapeDtypeStruct((n_seq * seq_len, hv), BF16),
                   jax.ShapeDtypeStruct((n_seq, 2, hh, dk, dv), F32),
                   jax.ShapeDtypeStruct((n_seq, 2, hh, 1, dk), F32),
                   jax.ShapeDtypeStruct((n_seq, 2, hh, 1, 1), F32)],
        scratch_shapes=[pltpu.VMEM((seq_len, hv), F32), pltpu.VMEM((seq_len, hv), F32)],
        compiler_params=_params("arbitrary"),
        name="mlstm_scan",
    )(qkv, qkv, qkv, gcol, grow, *(() if init is None else init))


def _mla_proj_kernel(x_ref, g_ref, sh_ref, sc_ref, win_ref, qg_ref, kvg_ref, wqb_ref, cos_ref,
                     sin_ref, q_ref, ckv_ref, kpe_ref, *, q_lora, kv_lora, rope, n_heads):
    h = _modulated(x_ref[...], g_ref[...], sh_ref[...], sc_ref[...])
    proj = _bdot_nt(h, win_ref[...])
    q_lat = proj[:, :q_lora]
    ckv_ref[...] = _rms(proj[:, q_lora:q_lora + kv_lora], kvg_ref[...])
    cos, sin = cos_ref[...], sin_ref[...]
    kpe = proj[:, q_lora + kv_lora:q_lora + kv_lora + rope]
    kpe_rot = proj[:, q_lora + kv_lora + rope:q_lora + kv_lora + 2 * rope]
    kpe_ref[...] = kpe * cos[:, :rope] + kpe_rot * sin[:, :rope]
    q = _bdot_nt(_rms(q_lat, qg_ref[...]), wqb_ref[...])
    n = n_heads * LANES
    pieces = []
    for hd in range(n_heads):
        c = slice(hd * LANES, (hd + 1) * LANES)
        pieces.append(q[:, c])
        pieces.append(q[:, n:2 * n][:, c] * cos + q[:, 2 * n:][:, c] * sin)
    q_ref[...] = jnp.concatenate(pieces, axis=1).astype(BF16)


def _rot_rows(wt):
    n, k = wt.shape
    quarter = MLA_ROPE // 4
    w4 = wt.reshape(n // (2 * quarter), 2, quarter, k)
    return jnp.concatenate([-w4[:, 1:2], w4[:, 0:1]], axis=1).reshape(n, k)


def _rope_tables(n_ctx, dec_batch, dec_seq):
    quarter = MLA_ROPE // 4
    freq = np.power(np.float32(ROPE_BASE), -np.arange(quarter, dtype=np.float32) / np.float32(quarter))
    pos = np.arange(dec_seq)
    ang_r = (pos // GRID_W).astype(np.float32)[:, None] * freq[None, :]
    ang_c = (pos % GRID_W).astype(np.float32)[:, None] * freq[None, :]
    ang = np.concatenate([ang_r, ang_r, ang_c, ang_c], axis=1).astype(np.float32)
    cos = np.concatenate([np.ones((n_ctx, MLA_ROPE), np.float32)] + [np.cos(ang)] * dec_batch, axis=0)
    sin = np.concatenate([np.zeros((n_ctx, MLA_ROPE), np.float32)] + [np.sin(ang)] * dec_batch, axis=0)
    reps = LANES // MLA_ROPE
    return (jnp.asarray(np.tile(cos, (1, reps)), F32), jnp.asarray(np.tile(sin, (1, reps)), F32))


def mla_proj(x, g, shift, scale, w_in, q_g, kv_g, w_qb, cos, sin, n_ctx, dec_seq, tm=512):
    t, d = x.shape
    hh, nope, rope = MLA_HEADS, MLA_NOPE, MLA_ROPE
    q_lora = q_g.shape[1]
    kv_lora = kv_g.shape[1]
    w_in_t = w_in.T
    w_in_ext = jnp.concatenate([w_in_t, _rot_rows(w_in_t[q_lora + kv_lora:])], axis=0)
    w3 = w_qb.T.reshape(hh, nope + rope, q_lora)
    w_qn = w3[:, :nope].reshape(hh * nope, q_lora)
    w_qp = w3[:, nope:].reshape(hh * rope, q_lora)
    assert nope == LANES and rope <= LANES
    head_pad = lambda w: jnp.pad(w.reshape(hh, rope, q_lora),
                                 ((0, 0), (0, LANES - rope), (0, 0))).reshape(hh * LANES, q_lora)
    w_qb_ext = jnp.concatenate([w_qn, head_pad(w_qp), head_pad(_rot_rows(w_qp))], axis=0)
    cidx = _cvec_index(n_ctx, dec_seq, tm)
    mod_spec = pl.BlockSpec((None, 1, d), lambda i: (cidx(i), 0, 0))
    full = lambda a: pl.BlockSpec(a.shape, lambda i: (0,) * a.ndim)
    kern = functools.partial(_mla_proj_kernel, q_lora=q_lora, kv_lora=kv_lora, rope=rope, n_heads=hh)
    return pl.pallas_call(
        kern,
        grid=(t // tm,),
        in_specs=[pl.BlockSpec((tm, d), lambda i: (i, 0)), full(g), mod_spec, mod_spec,
                  full(w_in_ext), full(q_g), full(kv_g), full(w_qb_ext),
                  pl.BlockSpec((tm, LANES), lambda i: (i, 0)),
                  pl.BlockSpec((tm, LANES), lambda i: (i, 0))],
        out_specs=[pl.BlockSpec((tm, 2 * hh * LANES), lambda i: (i, 0)),
                   pl.BlockSpec((tm, kv_lora), lambda i: (i, 0)),
                   pl.BlockSpec((tm, rope), lambda i: (i, 0))],
        out_shape=[jax.ShapeDtypeStruct((t, 2 * hh * LANES), BF16),
                   jax.ShapeDtypeStruct((t, kv_lora), F32),
                   jax.ShapeDtypeStruct((t, rope), F32)],
        compiler_params=_params("arbitrary"),
        name="mla_proj",
    )(x, g, shift, scale, w_in_ext, q_g, kv_g, w_qb_ext, cos, sin)


def _mla_kv_kernel(ckv_ref, kp_ref, w_ref, k_ref, v_ref):
    kv = _bdot(ckv_ref[...], w_ref[...])
    n = v_ref.shape[1]
    kp = kp_ref[...]
    pieces = []
    for hd in range(n // LANES):
        pieces += [kv[:, hd * LANES:(hd + 1) * LANES].astype(BF16), kp]
    k_ref[...] = jnp.concatenate(pieces, axis=1)
    v_ref[...] = kv[:, n:].astype(BF16)


def mla_kv(ckv_all, kp_pad, w_kvb, tm=1024):
    r, kv_lora = ckv_all.shape
    hh, nope, vd = MLA_HEADS, MLA_NOPE, MLA_V
    w3 = w_kvb.reshape(kv_lora, hh, nope + vd)
    w_perm = jnp.concatenate([w3[:, :, :nope].reshape(kv_lora, hh * nope),
                              w3[:, :, nope:].reshape(kv_lora, hh * vd)], axis=1)
    return pl.pallas_call(
        _mla_kv_kernel,
        grid=(r // tm,),
        in_specs=[pl.BlockSpec((tm, kv_lora), lambda i: (i, 0)),
                  pl.BlockSpec((tm, LANES), lambda i: (i, 0)),
                  pl.BlockSpec(w_perm.shape, lambda i: (0, 0))],
        out_specs=[pl.BlockSpec((tm, 2 * hh * LANES), lambda i: (i, 0)),
                   pl.BlockSpec((tm, hh * vd), lambda i: (i, 0))],
        out_shape=[jax.ShapeDtypeStruct((r, 2 * hh * LANES), BF16),
                   jax.ShapeDtypeStruct((r, hh * vd), BF16)],
        compiler_params=_params("arbitrary"),
        name="mla_kv",
    )(ckv_all, kp_pad, w_perm)


def _attn_kernel(q_ref, k_ref, v_ref, o_ref, *, scale):
    c = scale * np.log2(np.e)
    for h in range(MLA_HEADS):
        hk = slice(2 * h * LANES, 2 * (h + 1) * LANES)
        s = _bdot_nt(q_ref[:, hk], k_ref[:, hk])
        e = jnp.exp2((s - jnp.max(s, axis=-1, keepdims=True)) * c)
        o = _bdot(e, v_ref[:, h * MLA_V:(h + 1) * MLA_V]) / jnp.sum(e, axis=-1, keepdims=True)
        o_ref[:, h * MLA_V:(h + 1) * MLA_V] = o.astype(BF16)


def mla_attention(q, k, v, q_row_off, k_row_off, n_seq, q_len, k_len):
    tq = min(ATTN_Q_BLOCK, q_len)
    qb = q_len // tq
    q0 = q_row_off // tq
    k0 = k_row_off // k_len
    dq, dv = q.shape[1], v.shape[1]
    kern = functools.partial(_attn_kernel, scale=(MLA_NOPE + MLA_ROPE) ** -0.5)
    return pl.pallas_call(
        kern,
        grid=(n_seq, qb),
        in_specs=[pl.BlockSpec((tq, dq), lambda s, j: (q0 + s * qb + j, 0)),
                  pl.BlockSpec((k_len, dq), lambda s, j: (k0 + s, 0)),
                  pl.BlockSpec((k_len, dv), lambda s, j: (k0 + s, 0))],
        out_specs=pl.BlockSpec((tq, dv), lambda s, j: (s * qb + j, 0)),
        out_shape=jax.ShapeDtypeStruct((n_seq * q_len, dv), BF16),
        compiler_params=_params("arbitrary", "arbitrary"),
        name="mla_attention",
    )(q, k, v)


def kernel(x_prompt, x_sample, state_mlstm_C, state_mlstm_n, state_mlstm_m, cache_mla_ckv,
           cache_mla_kpe, c, c_ctx, w_ada, b_ada, norm_mix, norm_ffn, norm_final, w_pool,
           pool_scale, w_mlstm_in, b_mlstm_gate, mlstm_head_g, w_mlstm_out, w_mla_in, mla_q_g,
           mla_kv_g, w_mla_qb, w_mla_kvb, w_mla_out, w_router, b_router, w_exp_gate, w_exp_up,
           w_exp_down):
    batch, seq, d = x_prompt.shape
    dec_batch, dec_seq, _ = x_sample.shape
    depth = w_ada.shape[0]
    n_ctx = batch * seq
    n_lat = dec_batch * dec_seq
    hh = MLSTM_HEADS
    past = cache_mla_ckv.shape[2]

    x = (x_prompt.reshape(n_ctx, d), x_sample.reshape(n_lat, d))

    n_cv = 1 + dec_batch
    cvecs = jnp.concatenate([c_ctx[None, :], c, jnp.zeros((SUBLANES - n_cv % SUBLANES, d), F32)], axis=0)
    mod = ada_mod_all(cvecs, w_ada, b_ada).reshape(depth, cvecs.shape[0], 6, 1, d)

    w_router_pad = jnp.pad(w_router, ((0, 0), (0, LANES - N_EXPERTS)))
    w_router_hi = w_router_pad.astype(BF16)
    w_router_lo = (w_router_pad - w_router_hi.astype(F32)).astype(BF16)
    w_router_split = jnp.concatenate([w_router_hi, w_router_lo], axis=1)
    b_router_col = b_router.reshape(N_EXPERTS, 1)
    g_final = norm_final.reshape(1, d)
    row = lambda a: a.reshape(1, -1)

    outs = {}
    for i in range(depth):
        kind, j = i % 3, i // 3
        m = [mod[i, :n_cv, k] for k in range(6)]
        g_mix = row(norm_mix[i])
        if kind == 0:
            x = pool_layer(x, g_mix, m[0], m[1], m[2], w_pool[j], row(pool_scale[j]),
                           n_ctx, seq, dec_seq)
        elif kind == 1:
            qkv, o_gate, gcol, grow = mlstm_proj(x, g_mix, m[0], m[1], w_mlstm_in, j,
                                                 b_mlstm_gate[j], n_ctx, dec_seq)
            t = n_ctx + n_lat
            l = MLSTM_CHUNK
            grow_c = grow.reshape(4 * hh, t // l, l).transpose(1, 0, 2)
            hs_c, c_new, n_new, m_new = mlstm_scan(qkv, gcol, grow_c, None, 0, batch, seq)
            init = (state_mlstm_C[:, j], state_mlstm_n[:, j][:, :, :, None, :],
                    state_mlstm_m[:, j][:, :, :, None, None])
            hs_l, _, _, _ = mlstm_scan(qkv, gcol, grow_c, init, n_ctx, dec_batch, dec_seq)
            outs["C"] = c_new[:, None]
            outs["n"] = n_new[:, None, :, :, 0, :]
            outs["m"] = m_new[:, None, :, :, 0, 0]
            x = mlstm_out(x, hs_c, hs_l, o_gate, row(mlstm_head_g[j]), w_mlstm_out[j], m[2],
                          n_ctx, dec_seq)
        else:
            cos, sin = _rope_tables(n_ctx, dec_batch, dec_seq)
            q_cat, ckv, kpe = mla_proj(x, g_mix, m[0], m[1], w_mla_in[j], row(mla_q_g[j]),
                                       row(mla_kv_g[j]), w_mla_qb[j], cos, sin, n_ctx, dec_seq)
            lat_parts_c, lat_parts_p = [], []
            for b in range(dec_batch):
                lo = n_ctx + b * dec_seq
                lat_parts_c += [cache_mla_ckv[b, j], ckv[lo:lo + dec_seq]]
                lat_parts_p += [cache_mla_kpe[b, j], kpe[lo:lo + dec_seq]]
            ckv_all = jnp.concatenate(lat_parts_c + [ckv[:n_ctx]], axis=0)
            kp_all = jnp.concatenate(lat_parts_p + [kpe[:n_ctx]], axis=0).astype(BF16)
            kp_pad = jnp.pad(kp_all, ((0, 0), (0, LANES - kp_all.shape[1])))
            k_cat, v = mla_kv(ckv_all, kp_pad, w_mla_kvb[j])
            k_lat = past + dec_seq
            o_c = mla_attention(q_cat, k_cat, v, 0, dec_batch * k_lat, batch, seq, seq)
            o_l = mla_attention(q_cat, k_cat, v, n_ctx, 0, dec_batch, dec_seq, k_lat)
            outs["ckv"] = ckv[:n_ctx].reshape(batch, 1, seq, -1)
            outs["kpe"] = kpe[:n_ctx].reshape(batch, 1, seq, -1)
            x = resid_proj(x, o_c, o_l, w_mla_out[j], m[2], n_ctx, dec_seq)
        x = moe_layer(x, row(norm_ffn[i]), m[3], m[4], m[5], w_router_split, b_router_col,
                      w_exp_gate, w_exp_up, w_exp_down, i, g_final, n_ctx, dec_seq,
                      final_norm=(i == depth - 1), split_output=(i == depth - 1))

    y_prompt = x[0].reshape(batch, seq, d)
    y_sample = x[1].reshape(dec_batch, dec_seq, d)
    return (y_prompt, y_sample, outs["C"], outs["n"], outs["m"], outs["ckv"], outs["kpe"])
```

```python
import functools

import numpy as np
import jax
import jax.numpy as jnp
from jax import lax
from jax.experimental import pallas as pl
from jax.experimental.pallas import tpu as pltpu

F32 = jnp.float32
BF16 = jnp.bfloat16

NORM_EPS = 1e-6
GRID_W = 64
POOL_WINDOWS = (2, 4, 8, 16)
MLSTM_HEADS = 4
MLSTM_CHUNK = 256
MLA_HEADS = 8
MLA_NOPE = 128
MLA_ROPE = 64
MLA_V = 128
ROPE_BASE = 10000.0
N_EXPERTS = 16
N_EXPERT_GROUPS = 4
EXPERTS_PER_GROUP = N_EXPERTS // N_EXPERT_GROUPS

LANES = 128
SUBLANES = 8
VMEM_LIMIT = 56 * 1024 * 1024
MOE_VMEM_LIMIT = 60 * 1024 * 1024
POOL_TILE = 256
POOL_HALO = 8
ATTN_Q_BLOCK = 256
MOE_TILE = 512
MOE_SEG_ALIGN = 16
MOE_ROW_BLOCK = 176
MOE_TAIL_BLOCK = 64
MOE_LOAD_EXPERTS = 2


def _params(*sem, vmem_limit=VMEM_LIMIT):
    return pltpu.CompilerParams(dimension_semantics=sem, vmem_limit_bytes=vmem_limit)


def _rms(x, g):
    return x * lax.rsqrt(jnp.mean(x * x, axis=-1, keepdims=True) + NORM_EPS) * g


def _modulated(x, g, shift, scale):
    return _rms(x, g) * (1.0 + scale) + shift


def _silu(x):
    return x * jax.nn.sigmoid(x)


def _bdot(a, b):
    return jnp.dot(a.astype(BF16), b.astype(BF16), preferred_element_type=F32)


def _bdot_nt(a, b):
    return lax.dot_general(a.astype(BF16), b.astype(BF16), (((1,), (1,)), ((), ())),
                           preferred_element_type=F32)


def _bdot_tn(a, b):
    return lax.dot_general(a.astype(BF16), b.astype(BF16), (((0,), (0,)), ((), ())),
                           preferred_element_type=F32)


def _cvec_index(n_ctx, dec_seq, tm):
    def idx(i):
        r = i * tm
        return jnp.where(r < n_ctx, 0, (r - n_ctx) // dec_seq + 1)
    return idx


def _ada_kernel(c_ref, w_ref, b_ref, o_ref):
    o_ref[...] = _bdot(_silu(c_ref[...]), w_ref[...]) + b_ref[...]


def ada_mod_all(cvecs, w_ada, b_ada, tn=3072):
    depth, d, n6 = w_ada.shape
    rows = cvecs.shape[0]
    return pl.pallas_call(
        _ada_kernel,
        grid=(depth, n6 // tn),
        in_specs=[pl.BlockSpec((rows, d), lambda l, n: (0, 0)),
                  pl.BlockSpec((None, d, tn), lambda l, n: (l, 0, n)),
                  pl.BlockSpec((None, 1, tn), lambda l, n: (l, 0, n))],
        out_specs=pl.BlockSpec((None, rows, tn), lambda l, n: (l, 0, n)),
        out_shape=jax.ShapeDtypeStruct((depth, rows, n6), F32),
        compiler_params=_params("arbitrary", "arbitrary"),
        name="ada_mod",
    )(cvecs, w_ada, b_ada.reshape(depth, 1, n6))


def _pool_kernel(*refs, n_ctx_tiles, ctx_seq_tiles, lat_seq_tiles, split_input):
    n_x = 6 if split_input else 3
    x_refs = refs[:n_x]
    g_ref, sh_ref, sc_ref, gt_ref, wp_ref, ps_ref, o_ref, buf_ref, *lvl_refs = refs[n_x:]
    i = pl.program_id(0)
    is_ctx = i < n_ctx_tiles
    j = jnp.where(is_ctx, i % ctx_seq_tiles, (i - n_ctx_tiles) % lat_seq_tiles)
    nt = jnp.where(is_ctx, ctx_seq_tiles, lat_seq_tiles)
    g, sh, sc = g_ref[...], sh_ref[...], sc_ref[...]
    tp, hl = POOL_TILE, POOL_HALO
    gw = o_ref.shape[1] // len(POOL_WINDOWS)

    def fill(xc_ref, xp_ref, xn_ref):
        buf_ref[pl.ds(0, hl), :] = jnp.where(j == 0, 0.0, _modulated(xp_ref[...], g, sh, sc))
        buf_ref[pl.ds(hl, tp), :] = _modulated(xc_ref[...], g, sh, sc)
        buf_ref[pl.ds(hl + tp, hl), :] = jnp.where(j == nt - 1, 0.0,
                                                    _modulated(xn_ref[...], g, sh, sc))
        o_ref[...] = xc_ref[...]

    if split_input:
        pl.when(is_ctx)(lambda: fill(*x_refs[:3]))
        pl.when(jnp.logical_not(is_ctx))(lambda: fill(*x_refs[3:]))
    else:
        fill(*x_refs)

    rows = tp + 2 * hl
    buf_ref[pl.ds(rows, hl), :] = jnp.zeros((hl, buf_ref.shape[1]), F32)
    for ref in lvl_refs:
        ref[pl.ds(rows, hl), :] = jnp.zeros((hl, gw), F32)
    pos = j * tp + lax.broadcasted_iota(jnp.int32, (tp, 1), 0)
    seq_len = nt * tp
    for gi, w in enumerate(POOL_WINDOWS):
        cols = pl.ds(gi * gw, gw)
        src, src_cols = buf_ref, cols
        for lvl in range(1, w.bit_length() - 1):
            step = 1 << (lvl - 1)
            dst = lvl_refs[lvl - 1]
            dst[pl.ds(0, rows), :] = src[pl.ds(0, rows), src_cols] + src[pl.ds(step, rows), src_cols]
            src, src_cols = dst, pl.ds(0, gw)
        acc = src[pl.ds(hl - w // 2, tp), src_cols] + src[pl.ds(hl, tp), src_cols]
        cnt = jnp.minimum(pos + w // 2, seq_len) - jnp.maximum(pos - w // 2, 0)
        pooled = acc / cnt.astype(F32) - buf_ref[pl.ds(hl, tp), cols]
        y = _bdot(pooled, wp_ref[gi]) * ps_ref[:, cols]
        o_ref[:, cols] = o_ref[:, cols] + gt_ref[:, cols] * y


def _halo_specs(tp, hl, d, tile_off, n_rows):
    hb = tp // hl
    last_tile, last_hblk = n_rows // tp - 1, n_rows // hl - 1
    tile = lambda i: jnp.clip(i - tile_off, 0, last_tile)
    return [pl.BlockSpec((tp, d), lambda i: (tile(i), 0)),
            pl.BlockSpec((hl, d), lambda i: (jnp.clip(tile(i) * hb - 1, 0, last_hblk), 0)),
            pl.BlockSpec((hl, d), lambda i: (jnp.clip((tile(i) + 1) * hb, 0, last_hblk), 0))]


def pool_layer(xs, g, shift, scale, gate, w_pool, pool_scale, n_ctx, seq, dec_seq):
    split = isinstance(xs, tuple)
    tp, hl = POOL_TILE, POOL_HALO
    if split:
        d = xs[0].shape[1]
        t = xs[0].shape[0] + xs[1].shape[0]
        x_specs = (_halo_specs(tp, hl, d, 0, xs[0].shape[0])
                   + _halo_specs(tp, hl, d, n_ctx // tp, xs[1].shape[0]))
        x_args = (xs[0],) * 3 + (xs[1],) * 3
    else:
        t, d = xs.shape
        x_specs = _halo_specs(tp, hl, d, 0, t)
        x_args = (xs,) * 3
    cidx = _cvec_index(n_ctx, dec_seq, tp)
    mod_spec = pl.BlockSpec((None, 1, d), lambda i: (cidx(i), 0, 0))
    row_spec = pl.BlockSpec((1, d), lambda i: (0, 0))
    assert all(w & (w - 1) == 0 and w // 2 <= hl for w in POOL_WINDOWS)
    n_levels = max(POOL_WINDOWS).bit_length() - 2
    kern = functools.partial(_pool_kernel, n_ctx_tiles=n_ctx // tp, ctx_seq_tiles=seq // tp,
                             lat_seq_tiles=dec_seq // tp, split_input=split)
    return pl.pallas_call(
        kern,
        grid=(t // tp,),
        in_specs=x_specs + [row_spec, mod_spec, mod_spec, mod_spec,
                            pl.BlockSpec(w_pool.shape, lambda i: (0, 0, 0)), row_spec],
        out_specs=pl.BlockSpec((tp, d), lambda i: (i, 0)),
        out_shape=jax.ShapeDtypeStruct((t, d), F32),
        scratch_shapes=[pltpu.VMEM((tp + 3 * hl, d), F32)]
        + [pltpu.VMEM((tp + 3 * hl, d // len(POOL_WINDOWS)), F32)] * n_levels,
        compiler_params=_params("arbitrary"),
        name="pool_mixer",
    )(*x_args, g, shift, scale, gate, w_pool, pool_scale)


def _route(sel, scores):
    e, tm = sel.shape
    row = lax.broadcasted_iota(jnp.int32, (e, tm), 0)
    best = jnp.zeros((1, tm), jnp.int32)
    best_sc = None
    for gidx in range(N_EXPERT_GROUPS):
        r = [sel[gidx * EXPERTS_PER_GROUP + k:gidx * EXPERTS_PER_GROUP + k + 1, :]
             for k in range(EXPERTS_PER_GROUP)]
        top2 = None
        for a in range(EXPERTS_PER_GROUP):
            for b in range(a + 1, EXPERTS_PER_GROUP):
                s = r[a] + r[b]
                top2 = s if top2 is None else jnp.maximum(top2, s)
        if best_sc is None:
            best_sc = top2
        else:
            better = top2 > best_sc
            best = jnp.where(better, gidx, best)
            best_sc = jnp.where(better, top2, best_sc)
    neg = -jnp.inf
    masked = jnp.where(row // EXPERTS_PER_GROUP == best, sel, neg)
    m1 = jnp.max(masked, axis=0, keepdims=True)
    i1 = jnp.min(jnp.where(masked == m1, row, e), axis=0, keepdims=True)
    masked2 = jnp.where(row == i1, neg, masked)
    m2 = jnp.max(masked2, axis=0, keepdims=True)
    i2 = jnp.min(jnp.where(masked2 == m2, row, e), axis=0, keepdims=True)
    hot1 = row == i1
    hot2 = row == i2
    w1 = jnp.sum(jnp.where(hot1, scores, 0.0), axis=0, keepdims=True)
    w2 = jnp.sum(jnp.where(hot2, scores, 0.0), axis=0, keepdims=True)
    tot = w1 + w2
    return best, jnp.where(hot1, w1 / tot, 0.0) + jnp.where(hot2, w2 / tot, 0.0)


def _split_bf16(a, parts):
    out = []
    for _ in range(parts):
        p = a.astype(BF16)
        out.append(p)
        a = a - p.astype(F32)
    return out


def _pad_rows(a, rows):
    return jnp.concatenate([a, jnp.zeros((rows - a.shape[0], a.shape[1]), a.dtype)], axis=0)


def _moe_kernel(x_ref, g_ref, sh_ref, sc_ref, gt_ref, wr_ref, br_ref, wgf_ref, wuf_ref, wdf_ref,
                gf_ref, *rest, final_norm, n_ctx_tiles, n_experts):
    out_refs, (wgu_ref, wd_ref), scratch = rest[:-5], rest[-5:-3], rest[-3:]
    i = pl.program_id(0)
    per_step, f = wgf_ref.shape[0], wgf_ref.shape[2]
    n_load = n_experts // per_step
    for s in range(n_load):
        @pl.when(i == s)
        def _(s=s):
            for j in range(per_step):
                gi, k = divmod(s * per_step + j, EXPERTS_PER_GROUP)
                wgu_ref[gi, :, pl.ds(2 * k * f, f)] = wgf_ref[j].astype(BF16)
                wgu_ref[gi, :, pl.ds((2 * k + 1) * f, f)] = wuf_ref[j].astype(BF16)
                wd_ref[gi, pl.ds(k * f, f), :] = wdf_ref[j].astype(BF16)

    @pl.when(i >= n_load)
    def _():
        _moe_tile(x_ref, g_ref, sh_ref, sc_ref, gt_ref, wr_ref, br_ref, wgu_ref, wd_ref,
                  gf_ref, out_refs, scratch, i - n_load, final_norm, n_ctx_tiles)


def _moe_tile(x_ref, g_ref, sh_ref, sc_ref, gt_ref, wr_ref, br_ref, wgu_ref, wd_ref,
              gf_ref, out_refs, scratch, tile, final_norm, n_ctx_tiles):
    hp_scr, cw_scr, yp_scr = scratch
    tr, d = x_ref.shape
    trp = hp_scr.shape[0]
    ng, eg = N_EXPERT_GROUPS, EXPERTS_PER_GROUP
    x = x_ref[...]
    h = _modulated(x, g_ref[...], sh_ref[...], sc_ref[...])
    hb = h.astype(BF16)

    h_lo = (h - hb.astype(F32)).astype(BF16)
    wr = wr_ref[...]
    lg = jnp.dot(hb, wr, preferred_element_type=F32)
    logits = lg[:, :LANES] + lg[:, LANES:] + jnp.dot(h_lo, wr[:, :LANES], preferred_element_type=F32)
    scores = jax.nn.sigmoid(logits.T[:N_EXPERTS, :])
    best, comb_t = _route(scores + br_ref[...], scores)

    grp = lax.broadcasted_iota(jnp.int32, (SUBLANES, tr), 0)
    hot_t = (grp == best).astype(F32)
    cw_t = hot_t[0:1, :] * comb_t[0:eg, :]
    for gi in range(1, ng):
        cw_t = cw_t + hot_t[gi:gi + 1, :] * comb_t[gi * eg:(gi + 1) * eg, :]
    cw_c = _pad_rows(cw_t, LANES).T

    ia = lax.broadcasted_iota(jnp.int32, (tr, tr), 0)
    ib = lax.broadcasted_iota(jnp.int32, (tr, tr), 1)
    before = jnp.where(ia < ib, 1.0, 0.0).astype(BF16)
    rank_t = jnp.dot(hot_t.astype(BF16), before, preferred_element_type=F32)

    starts, counts = [], []
    off = jnp.int32(0)
    for gi in range(ng):
        n = jnp.sum(hot_t[gi:gi + 1, :]).astype(jnp.int32)
        n = ((n + MOE_SEG_ALIGN - 1) // MOE_SEG_ALIGN) * MOE_SEG_ALIGN
        starts.append(off)
        counts.append(n)
        off = off + n

    pos_t = hot_t[0:1, :] * (rank_t[0:1, :] + starts[0].astype(F32))
    for gi in range(1, ng):
        pos_t = pos_t + hot_t[gi:gi + 1, :] * (rank_t[gi:gi + 1, :] + starts[gi].astype(F32))
    pos_c = _pad_rows(pos_t, LANES).T[:, 0:1]
    used = tr + ng * MOE_SEG_ALIGN
    perm = jnp.where(lax.broadcasted_iota(jnp.int32, (used, tr), 0) == pos_t.astype(jnp.int32),
                     1.0, 0.0).astype(BF16)
    perm_t = jnp.where(lax.broadcasted_iota(jnp.int32, (tr, used), 1) == pos_c.astype(jnp.int32),
                       1.0, 0.0).astype(BF16)

    hp_scr[pl.ds(0, used), :] = jnp.dot(perm, hb, preferred_element_type=F32).astype(BF16)
    hp_scr[pl.ds(used, trp - used), :] = jnp.zeros((trp - used, d), BF16)
    cw_pair = jnp.dot(perm, jnp.concatenate(_split_bf16(cw_c, 2), axis=1), preferred_element_type=F32)
    cw_scr[pl.ds(0, used), :] = cw_pair[:, :LANES] + cw_pair[:, LANES:]
    cw_scr[pl.ds(used, trp - used), :] = jnp.zeros((trp - used, LANES), F32)
    yp_scr[...] = jnp.zeros_like(yp_scr)

    f = wd_ref.shape[1] // eg

    def expert_block(gi, row0, n_rows, seg_end=None):
        rows = pl.ds(pl.multiple_of(row0, MOE_SEG_ALIGN), n_rows)
        cwb = cw_scr[rows, :]
        gu = jnp.dot(hp_scr[rows, :], wgu_ref[gi], preferred_element_type=F32)
        hid = [(_silu(gu[:, 2 * k * f:(2 * k + 1) * f]) * gu[:, (2 * k + 1) * f:(2 * k + 2) * f]
                * cwb[:, k:k + 1]).astype(BF16) for k in range(eg)]
        y = jnp.dot(jnp.concatenate(hid, axis=1), wd_ref[gi], preferred_element_type=F32).astype(BF16)
        if seg_end is None:
            yp_scr[rows, :] = y
        else:
            inside = row0 + lax.broadcasted_iota(jnp.int32, (n_rows, 1), 0) < seg_end
            yp_scr[rows, :] = jnp.where(inside, y, yp_scr[rows, :])

    sb, tb = MOE_ROW_BLOCK, MOE_TAIL_BLOCK
    for gi in range(ng):
        expert_block(gi, starts[gi], sb)
    for gi in range(ng):
        def tail(b, carry, gi=gi):
            expert_block(gi, starts[gi] + sb + b * tb, tb, starts[gi] + counts[gi])
            return carry
        lax.fori_loop(0, (jnp.maximum(counts[gi] - sb, 0) + tb - 1) // tb, tail, 0)

    moe = jnp.dot(perm_t, yp_scr[pl.ds(0, used), :], preferred_element_type=F32)
    out = x + gt_ref[...] * moe
    if final_norm:
        out = _rms(out, gf_ref[...])
    if n_ctx_tiles is None:
        out_refs[0][...] = out
    else:
        is_ctx = tile < n_ctx_tiles

        @pl.when(is_ctx)
        def _():
            out_refs[0][...] = out

        @pl.when(jnp.logical_not(is_ctx))
        def _():
            out_refs[1][...] = out


def moe_layer(x, g, shift, scale, gate, w_router_split, b_router_col, wg, wu, wd, layer, g_final,
              n_ctx, dec_seq, final_norm, split_output=False, tm=MOE_TILE):
    t, d = x.shape
    _, n_e, _, f = wg.shape
    trp = tm + N_EXPERT_GROUPS * MOE_SEG_ALIGN + max(MOE_ROW_BLOCK, MOE_TAIL_BLOCK)
    per_step = MOE_LOAD_EXPERTS
    n_load = n_e // per_step
    tile = lambda i: jnp.maximum(i - n_load, 0)
    cidx = _cvec_index(n_ctx, dec_seq, tm)
    mod_spec = pl.BlockSpec((None, 1, d), lambda i: (cidx(tile(i)), 0, 0))
    row_spec = pl.BlockSpec((1, d), lambda i: (0, 0))
    expert = lambda i: jnp.minimum(i, n_load - 1)
    nct = n_ctx // tm
    if split_output:
        out_specs = [pl.BlockSpec((tm, d), lambda i: (jnp.minimum(tile(i), nct - 1), 0)),
                     pl.BlockSpec((tm, d), lambda i: (jnp.maximum(tile(i) - nct, 0), 0))]
        out_shape = [jax.ShapeDtypeStruct((n_ctx, d), F32), jax.ShapeDtypeStruct((t - n_ctx, d), F32)]
    else:
        out_specs = pl.BlockSpec((tm, d), lambda i: (tile(i), 0))
        out_shape = jax.ShapeDtypeStruct((t, d), F32)
    return pl.pallas_call(
        functools.partial(_moe_kernel, final_norm=final_norm,
                          n_ctx_tiles=nct if split_output else None, n_experts=n_e),
        grid=(n_load + t // tm,),
        in_specs=[pl.BlockSpec((tm, d), lambda i: (tile(i), 0)),
                  row_spec, mod_spec, mod_spec, mod_spec,
                  pl.BlockSpec(w_router_split.shape, lambda i: (0, 0)),
                  pl.BlockSpec(b_router_col.shape, lambda i: (0, 0)),
                  pl.BlockSpec((None, per_step, d, f), lambda i: (layer, expert(i), 0, 0)),
                  pl.BlockSpec((None, per_step, d, f), lambda i: (layer, expert(i), 0, 0)),
                  pl.BlockSpec((None, per_step, f, d), lambda i: (layer, expert(i), 0, 0)),
                  row_spec],
        out_specs=out_specs,
        out_shape=out_shape,
        scratch_shapes=[pltpu.VMEM((N_EXPERT_GROUPS, d, 2 * EXPERTS_PER_GROUP * f), BF16),
                        pltpu.VMEM((N_EXPERT_GROUPS, EXPERTS_PER_GROUP * f, d), BF16),
                        pltpu.VMEM((trp, d), BF16), pltpu.VMEM((trp, LANES), F32),
                        pltpu.VMEM((trp, d), BF16)],
        compiler_params=_params("arbitrary", vmem_limit=MOE_VMEM_LIMIT),
        name="moe",
    )(x, g, shift, scale, gate, w_router_split, b_router_col, wg, wu, wd, g_final)


def _per_stream(n_ctx_tiles, ctx_ref, lat_ref, fn):
    i = pl.program_id(0)

    @pl.when(i < n_ctx_tiles)
    def _():
        fn(ctx_ref[...])

    @pl.when(i >= n_ctx_tiles)
    def _():
        fn(lat_ref[...])


def _stream_specs(tm, k, n_ctx_tiles):
    return [pl.BlockSpec((tm, k), lambda i: (jnp.minimum(i, n_ctx_tiles - 1), 0)),
            pl.BlockSpec((tm, k), lambda i: (jnp.maximum(i - n_ctx_tiles, 0), 0))]


def _mlstm_out_kernel(x_ref, hc_ref, hl_ref, og_ref, hg_ref, w_ref, gt_ref, o_ref, *, n_ctx_tiles):
    def run(hs):
        a = jax.nn.sigmoid(og_ref[...].astype(F32)) * (hs.astype(F32) * hg_ref[...])
        o_ref[...] = x_ref[...] + gt_ref[...] * _bdot(a, w_ref[...])
    _per_stream(n_ctx_tiles, hc_ref, hl_ref, run)


def mlstm_out(x, hs_ctx, hs_lat, o_gate, head_g, w, gate, n_ctx, dec_seq, tm=1024):
    t, d = x.shape
    k = hs_ctx.shape[1]
    cidx = _cvec_index(n_ctx, dec_seq, tm)
    nct = n_ctx // tm
    return pl.pallas_call(
        functools.partial(_mlstm_out_kernel, n_ctx_tiles=nct),
        grid=(t // tm,),
        in_specs=[pl.BlockSpec((tm, d), lambda i: (i, 0))] + _stream_specs(tm, k, nct) + [
            pl.BlockSpec((tm, k), lambda i: (i, 0)),
            pl.BlockSpec((1, k), lambda i: (0, 0)),
            pl.BlockSpec((k, d), lambda i: (0, 0)),
            pl.BlockSpec((None, 1, d), lambda i: (cidx(i), 0, 0))],
        out_specs=pl.BlockSpec((tm, d), lambda i: (i, 0)),
        out_shape=jax.ShapeDtypeStruct((t, d), F32),
        compiler_params=_params("arbitrary"),
        name="mlstm_out",
    )(x, hs_ctx, hs_lat, o_gate, head_g, w, gate)


def _mlstm_proj_kernel(x_ref, g_ref, sh_ref, sc_ref, wqk_ref, wv_ref, wo_ref, wgt_ref, bg_ref,
                       qs_ref, qkv_ref, o_ref, gc_ref, gr_ref):
    h = _modulated(x_ref[...], g_ref[...], sh_ref[...], sc_ref[...]).astype(BF16)
    nqk = wqk_ref.shape[0]
    qkv_ref[:, :nqk] = (_bdot_nt(h, wqk_ref[...]) * qs_ref[...]).astype(BF16)
    qkv_ref[:, nqk:] = _bdot_nt(h, wv_ref[...]).astype(BF16)
    o_ref[...] = _bdot_nt(h, wo_ref[...]).astype(BF16)
    gates = _bdot_nt(h, wgt_ref[...]) + bg_ref[...]
    gc_ref[...] = gates
    gr_ref[...] = gates.T[:gr_ref.shape[0], :]


def mlstm_proj(x, g, shift, scale, w_in_all, layer, b_gate, n_ctx, dec_seq, tm=512):
    t, d = x.shape
    hh = MLSTM_HEADS
    hv = d
    hk = hv // 2
    ng = 4 * hh
    w_t_all = jnp.swapaxes(w_in_all, 1, 2)
    w_g_pad = jnp.pad(w_t_all[layer, 2 * hk + 2 * hv:, :], ((0, LANES - ng), (0, 0)))
    b_pad = jnp.pad(b_gate.reshape(1, ng), ((0, 0), (0, LANES - ng)))
    dk = hk // hh
    q_scale = jnp.concatenate([jnp.full((1, hk), dk ** -0.5, F32), jnp.ones((1, hk), F32)], axis=1)
    cidx = _cvec_index(n_ctx, dec_seq, tm)
    mod_spec = pl.BlockSpec((None, 1, d), lambda i: (cidx(i), 0, 0))
    full = lambda a: pl.BlockSpec(a.shape, lambda i: (0,) * a.ndim)
    assert 2 * hk == hv
    w_col = lambda n: pl.BlockSpec((None, hv, d), lambda i: (layer, n, 0))
    return pl.pallas_call(
        _mlstm_proj_kernel,
        grid=(t // tm,),
        in_specs=[pl.BlockSpec((tm, d), lambda i: (i, 0)), full(g), mod_spec, mod_spec,
                  w_col(0), w_col(1), w_col(2), full(w_g_pad), full(b_pad), full(q_scale)],
        out_specs=[pl.BlockSpec((tm, 2 * hk + hv), lambda i: (i, 0)),
                   pl.BlockSpec((tm, hv), lambda i: (i, 0)),
                   pl.BlockSpec((tm, LANES), lambda i: (i, 0)),
                   pl.BlockSpec((ng, tm), lambda i: (0, i))],
        out_shape=[jax.ShapeDtypeStruct((t, 2 * hk + hv), BF16),
                   jax.ShapeDtypeStruct((t, hv), BF16),
                   jax.ShapeDtypeStruct((t, LANES), F32),
                   jax.ShapeDtypeStruct((ng, t), F32)],
        compiler_params=_params("arbitrary"),
        name="mlstm_proj",
    )(x, g, shift, scale, w_t_all, w_t_all, w_t_all, w_g_pad, b_pad, q_scale)


def _log_sigmoid(x):
    return jnp.minimum(x, 0.0) - jnp.log(1.0 + jnp.exp(-jnp.abs(x)))


def _gate_cumsums(gc, gr, causal_bf, feeds_bf):
    bc = br = None
    for part in _split_bf16(_log_sigmoid(gc), 3):
        t = jnp.dot(causal_bf, part, preferred_element_type=F32)
        bc = t if bc is None else bc + t
    for part in _split_bf16(_log_sigmoid(gr), 3):
        t = jnp.dot(part, feeds_bf, preferred_element_type=F32)
        br = t if br is None else br + t
    return bc, br


def _mlstm_chunk(q, k, v, i_col, b_col, i_row, b_row, c_st, n_st, m_st, causal, rev):
    l = q.shape[0]
    zero_state = c_st is None
    if zero_state:
        m_st = 0.0
    g_row = i_row - b_row
    log_w = jnp.where(causal, g_row, -jnp.inf)
    c_t = jnp.maximum(m_st, jnp.max(log_w, axis=1, keepdims=True))
    dw = jnp.exp(log_w - c_t)
    a = _bdot_nt(q, k) * dw
    a_hi, a_lo = _split_bf16(a, 2)
    ones = jnp.ones((l, LANES), BF16)
    den = (jnp.dot(a_hi, ones, preferred_element_type=F32)
           + jnp.dot(a_lo, ones, preferred_element_type=F32))[:, 0:1]
    if not zero_state:
        sw = jnp.exp(m_st - c_t)
        n_hi, n_lo = _split_bf16(jnp.broadcast_to(n_st, (LANES, n_st.shape[1])), 2)
        den = den + sw * (_bdot_nt(q, n_hi) + _bdot_nt(q, n_lo))[:, 0:1]
    inv = 1.0 / jnp.maximum(jnp.abs(den), jnp.exp(-(b_col + c_t)))
    h = inv * jnp.dot(a_hi, v, preferred_element_type=F32)
    if not zero_state:
        h = h + (sw * inv) * _bdot(q, c_st)
    b_last = b_col[0:1, :] if rev else b_col[l - 1:l, :]
    log_k = b_last - b_col + i_col
    m_new = jnp.maximum(b_last + m_st, jnp.max(log_k, axis=0, keepdims=True))
    kw = jnp.exp(log_k - m_new)
    kwk = kw * k.astype(F32)
    c_new = _bdot_tn(kwk, v)
    n_new = jnp.sum(kwk, axis=0, keepdims=True)
    if not zero_state:
        decay = jnp.exp(b_last + m_st - m_new)
        c_new = decay * c_st + c_new
        n_new = decay * n_st + n_new
    return h, c_new, n_new, m_new


def _mlstm_scan_kernel(*refs, n_chunks, zero_init):
    single = zero_init and n_chunks == 1
    if zero_init:
        q_ref, k_ref, v_ref, gc_ref, gr_ref, hs_ref, c_ref, n_ref, m_ref, hf_scr, hb_scr = refs
        if not single:
            c_ref[...] = jnp.zeros_like(c_ref)
            n_ref[...] = jnp.zeros_like(n_ref)
            m_ref[...] = jnp.zeros_like(m_ref)
    else:
        (q_ref, k_ref, v_ref, gc_ref, gr_ref, c0_ref, n0_ref, m0_ref,
         hs_ref, c_ref, n_ref, m_ref, hf_scr, hb_scr) = refs
        c_ref[...] = c0_ref[...]
        n_ref[...] = n0_ref[...]
        m_ref[...] = m0_ref[...]
    l = MLSTM_CHUNK
    hh = MLSTM_HEADS
    dk = q_ref.shape[1] // hh
    dv = v_ref.shape[1] // hh

    tt = lax.broadcasted_iota(jnp.int32, (l, l), 0)
    ss = lax.broadcasted_iota(jnp.int32, (l, l), 1)
    masks = (ss <= tt, ss >= tt)
    masks_bf = tuple(jnp.where(m, 1.0, 0.0).astype(BF16) for m in masks)

    def body(c, carry):
        cr = n_chunks - 1 - c
        start = (lambda j: j * l) if single else (lambda j: pl.multiple_of(j * l, l))
        rows = (pl.ds(start(c), l), pl.ds(start(cr), l))
        grs = (gr_ref[c], gr_ref[cr])
        outs, states = ([], []), []
        for d in range(2):
            q, k, v, gc, gr = q_ref[rows[d], :], k_ref[rows[d], :], v_ref[rows[d], :], gc_ref[rows[d], :], grs[d]
            bc, br = _gate_cumsums(gc, gr, masks_bf[d], masks_bf[1 - d])
            for h in range(hh):
                ci, cf = 2 * d * hh + h, (2 * d + 1) * hh + h
                state = (None,) * 3 if single else (c_ref[d, h], n_ref[d, h], m_ref[d, h])
                o, *st = _mlstm_chunk(q[:, h * dk:(h + 1) * dk], k[:, h * dk:(h + 1) * dk],
                                      v[:, h * dv:(h + 1) * dv], gc[:, ci:ci + 1], bc[:, cf:cf + 1],
                                      gr[ci:ci + 1, :], br[cf:cf + 1, :], *state, masks[d], d == 1)
                outs[d].append(o)
                states.append((d, h, st))
        hf_scr[rows[0], :] = jnp.concatenate(outs[0], axis=1)
        hb_scr[rows[1], :] = jnp.concatenate(outs[1], axis=1)
        for d, h, (c_new, n_new, m_new) in states:
            c_ref[d, h], n_ref[d, h], m_ref[d, h] = c_new, n_new, m_new
        return carry

    if single:
        body(0, 0)
    else:
        lax.fori_loop(0, n_chunks, body, 0)
    for h in range(hh):
        cols = pl.ds(h * dv, dv)
        hs = hf_scr[:, cols] + hb_scr[:, cols]
        hs_ref[:, cols] = (hs * lax.rsqrt(jnp.mean(hs * hs, axis=-1, keepdims=True)
                                          + NORM_EPS)).astype(BF16)


def mlstm_scan(qkv, gcol, grow, init, row_off, n_seq, seq_len):
    hh = MLSTM_HEADS
    hv = qkv.shape[1] // 2
    dv = hv // hh
    dk = dv // 2
    l = MLSTM_CHUNK
    nc = seq_len // l
    ob = row_off // seq_len
    kern = functools.partial(_mlstm_scan_kernel, n_chunks=nc, zero_init=init is None)
    st = lambda *tail: pl.BlockSpec((None, 2, hh) + tail, lambda s: (s, 0, 0) + (0,) * len(tail))
    states = [st(dk, dv), st(1, dk), st(1, 1)]
    return pl.pallas_call(
        kern,
        grid=(n_seq,),
        in_specs=[pl.BlockSpec((seq_len, hh * dk), lambda s: (ob + s, 0)),
                  pl.BlockSpec((seq_len, hh * dk), lambda s: (ob + s, 1)),
                  pl.BlockSpec((seq_len, hv), lambda s: (ob + s, 1)),
                  pl.BlockSpec((seq_len, LANES), lambda s: (ob + s, 0)),
                  pl.BlockSpec((nc, 4 * hh, l), lambda s: (ob + s, 0, 0))]
                 + ([] if init is None else states),
        out_specs=[pl.BlockSpec((seq_len, hv), lambda s: (s, 0))] + states,
        out_shape=[jax.ShapeDtypeStruct((n_seq * seq_len, hv), BF16),
                   jax.ShapeDtypeStruct((n_seq, 2, hh, dk, dv), F32),
                   jax.ShapeDtypeStruct((n_seq, 2, hh, 1, dk), F32),
                   jax.ShapeDtypeStruct((n_seq, 2, hh, 1, 1), F32)],
        scratch_shapes=[pltpu.VMEM((seq_len, hv), F32), pltpu.VMEM((seq_len, hv), F32)],
        compiler_params=_params("arbitrary"),
        name="mlstm_scan",
    )(qkv, qkv, qkv, gcol, grow, *(() if init is None else init))


def _mla_proj_kernel(x_ref, g_ref, sh_ref, sc_ref, win_ref, qg_ref, kvg_ref, wqb_ref, cos_ref,
                     sin_ref, q_ref, ckv_ref, kpe_ref, *, q_lora, kv_lora, rope, n_heads):
    h = _modulated(x_ref[...], g_ref[...], sh_ref[...], sc_ref[...])
    proj = _bdot_nt(h, win_ref[...])
    q_lat = proj[:, :q_lora]
    ckv_ref[...] = _rms(proj[:, q_lora:q_lora + kv_lora], kvg_ref[...])
    cos, sin = cos_ref[...], sin_ref[...]
    kpe = proj[:, q_lora + kv_lora:q_lora + kv_lora + rope]
    kpe_rot = proj[:, q_lora + kv_lora + rope:q_lora + kv_lora + 2 * rope]
    kpe_ref[...] = kpe * cos[:, :rope] + kpe_rot * sin[:, :rope]
    q = _bdot_nt(_rms(q_lat, qg_ref[...]), wqb_ref[...])
    n = n_heads * LANES
    pieces = []
    for hd in range(n_heads):
        c = slice(hd * LANES, (hd + 1) * LANES)
        pieces.append(q[:, c])
        pieces.append(q[:, n:2 * n][:, c] * cos + q[:, 2 * n:][:, c] * sin)
    q_ref[...] = jnp.concatenate(pieces, axis=1).astype(BF16)


def _rot_rows(wt):
    n, k = wt.shape
    quarter = MLA_ROPE // 4
    w4 = wt.reshape(n // (2 * quarter), 2, quarter, k)
    return jnp.concatenate([-w4[:, 1:2], w4[:, 0:1]], axis=1).reshape(n, k)


def _rope_tables(n_ctx, dec_batch, dec_seq):
    quarter = MLA_ROPE // 4
    freq = np.power(np.float32(ROPE_BASE), -np.arange(quarter, dtype=np.float32) / np.float32(quarter))
    pos = np.arange(dec_seq)
    ang_r = (pos // GRID_W).astype(np.float32)[:, None] * freq[None, :]
    ang_c = (pos % GRID_W).astype(np.float32)[:, None] * freq[None, :]
    ang = np.concatenate([ang_r, ang_r, ang_c, ang_c], axis=1).astype(np.float32)
    cos = np.concatenate([np.ones((n_ctx, MLA_ROPE), np.float32)] + [np.cos(ang)] * dec_batch, axis=0)
    sin = np.concatenate([np.zeros((n_ctx, MLA_ROPE), np.float32)] + [np.sin(ang)] * dec_batch, axis=0)
    reps = LANES // MLA_ROPE
    return (jnp.asarray(np.tile(cos, (1, reps)), F32), jnp.asarray(np.tile(sin, (1, reps)), F32))


def mla_proj(x, g, shift, scale, w_in, q_g, kv_g, w_qb, cos, sin, n_ctx, dec_seq, tm=512):
    t, d = x.shape
    hh, nope, rope = MLA_HEADS, MLA_NOPE, MLA_ROPE
    q_lora = q_g.shape[1]
    kv_lora = kv_g.shape[1]
    w_in_t = w_in.T
    w_in_ext = jnp.concatenate([w_in_t, _rot_rows(w_in_t[q_lora + kv_lora:])], axis=0)
    w3 = w_qb.T.reshape(hh, nope + rope, q_lora)
    w_qn = w3[:, :nope].reshape(hh * nope, q_lora)
    w_qp = w3[:, nope:].reshape(hh * rope, q_lora)
    assert nope == LANES and rope <= LANES
    head_pad = lambda w: jnp.pad(w.reshape(hh, rope, q_lora),
                                 ((0, 0), (0, LANES - rope), (0, 0))).reshape(hh * LANES, q_lora)
    w_qb_ext = jnp.concatenate([w_qn, head_pad(w_qp), head_pad(_rot_rows(w_qp))], axis=0)
    cidx = _cvec_index(n_ctx, dec_seq, tm)
    mod_spec = pl.BlockSpec((None, 1, d), lambda i: (cidx(i), 0, 0))
    full = lambda a: pl.BlockSpec(a.shape, lambda i: (0,) * a.ndim)
    kern = functools.partial(_mla_proj_kernel, q_lora=q_lora, kv_lora=kv_lora, rope=rope, n_heads=hh)
    return pl.pallas_call(
        kern,
        grid=(t // tm,),
        in_specs=[pl.BlockSpec((tm, d), lambda i: (i, 0)), full(g), mod_spec, mod_spec,
                  full(w_in_ext), full(q_g), full(kv_g), full(w_qb_ext),
                  pl.BlockSpec((tm, LANES), lambda i: (i, 0)),
                  pl.BlockSpec((tm, LANES), lambda i: (i, 0))],
        out_specs=[pl.BlockSpec((tm, 2 * hh * LANES), lambda i: (i, 0)),
                   pl.BlockSpec((tm, kv_lora), lambda i: (i, 0)),
                   pl.BlockSpec((tm, rope), lambda i: (i, 0))],
        out_shape=[jax.ShapeDtypeStruct((t, 2 * hh * LANES), BF16),
                   jax.ShapeDtypeStruct((t, kv_lora), F32),
                   jax.ShapeDtypeStruct((t, rope), F32)],
        compiler_params=_params("arbitrary"),
        name="mla_proj",
    )(x, g, shift, scale, w_in_ext, q_g, kv_g, w_qb_ext, cos, sin)


def _mla_kv_kernel(ckv_ref, kp_ref, w_ref, k_ref, v_ref):
    kv = _bdot(ckv_ref[...], w_ref[...])
    n = v_ref.shape[1]
    kp = kp_ref[...]
    pieces = []
    for hd in range(n // LANES):
        pieces += [kv[:, hd * LANES:(hd + 1) * LANES].astype(BF16), kp]
    k_ref[...] = jnp.concatenate(pieces, axis=1)
    v_ref[...] = kv[:, n:].astype(BF16)


def mla_kv(ckv_all, kp_pad, w_kvb, tm=1024):
    r, kv_lora = ckv_all.shape
    hh, nope, vd = MLA_HEADS, MLA_NOPE, MLA_V
    w3 = w_kvb.reshape(kv_lora, hh, nope + vd)
    w_perm = jnp.concatenate([w3[:, :, :nope].reshape(kv_lora, hh * nope),
                              w3[:, :, nope:].reshape(kv_lora, hh * vd)], axis=1)
    return pl.pallas_call(
        _mla_kv_kernel,
        grid=(r // tm,),
        in_specs=[pl.BlockSpec((tm, kv_lora), lambda i: (i, 0)),
                  pl.BlockSpec((tm, LANES), lambda i: (i, 0)),
                  pl.BlockSpec(w_perm.shape, lambda i: (0, 0))],
        out_specs=[pl.BlockSpec((tm, 2 * hh * LANES), lambda i: (i, 0)),
                   pl.BlockSpec((tm, hh * vd), lambda i: (i, 0))],
        out_shape=[jax.ShapeDtypeStruct((r, 2 * hh * LANES), BF16),
                   jax.ShapeDtypeStruct((r, hh * vd), BF16)],
        compiler_params=_params("arbitrary"),
        name="mla_kv",
    )(ckv_all, kp_pad, w_perm)


def _attn_kernel(q_ref, k_ref, v_ref, x_ref, w_ref, gt_ref, xo_ref, o_scr, *, scale):
    c = scale * np.log2(np.e)
    for h in range(MLA_HEADS):
        hk = slice(2 * h * LANES, 2 * (h + 1) * LANES)
        s = _bdot_nt(q_ref[:, hk], k_ref[:, hk])
        e = jnp.exp2((s - jnp.max(s, axis=-1, keepdims=True)) * c)
        o = _bdot(e, v_ref[:, h * MLA_V:(h + 1) * MLA_V]) / jnp.sum(e, axis=-1, keepdims=True)
        o_scr[:, h * MLA_V:(h + 1) * MLA_V] = o.astype(BF16)
    xo_ref[...] = x_ref[...] + gt_ref[...] * jnp.dot(o_scr[...], w_ref[...],
                                                     preferred_element_type=F32)


def mla_attention(x, q, k, v, w_out, gate, q_row_off, k_row_off, n_seq, q_len, k_len, n_ctx, dec_seq):
    tq = min(ATTN_Q_BLOCK, q_len)
    qb = q_len // tq
    q0 = q_row_off // tq
    k0 = k_row_off // k_len
    t, d = x.shape
    dq, dv = q.shape[1], v.shape[1]
    cidx = _cvec_index(n_ctx, dec_seq, tq)
    rows = lambda s, j: q0 + s * qb + j
    kern = functools.partial(_attn_kernel, scale=(MLA_NOPE + MLA_ROPE) ** -0.5)
    return pl.pallas_call(
        kern,
        grid=(n_seq, qb),
        in_specs=[pl.BlockSpec((tq, dq), lambda s, j: (rows(s, j), 0)),
                  pl.BlockSpec((k_len, dq), lambda s, j: (k0 + s, 0)),
                  pl.BlockSpec((k_len, dv), lambda s, j: (k0 + s, 0)),
                  pl.BlockSpec((tq, d), lambda s, j: (rows(s, j), 0)),
                  pl.BlockSpec(w_out.shape, lambda s, j: (0, 0)),
                  pl.BlockSpec((None, 1, d), lambda s, j: (cidx(rows(s, j)), 0, 0))],
        out_specs=pl.BlockSpec((tq, d), lambda s, j: (rows(s, j), 0)),
        out_shape=jax.ShapeDtypeStruct((t, d), F32),
        scratch_shapes=[pltpu.VMEM((tq, dv), BF16)],
        input_output_aliases={3: 0},
        compiler_params=_params("arbitrary", "arbitrary"),
        name="mla_attention",
    )(q, k, v, x, w_out, gate)


def kernel(x_prompt, x_sample, state_mlstm_C, state_mlstm_n, state_mlstm_m, cache_mla_ckv,
           cache_mla_kpe, c, c_ctx, w_ada, b_ada, norm_mix, norm_ffn, norm_final, w_pool,
           pool_scale, w_mlstm_in, b_mlstm_gate, mlstm_head_g, w_mlstm_out, w_mla_in, mla_q_g,
           mla_kv_g, w_mla_qb, w_mla_kvb, w_mla_out, w_router, b_router, w_exp_gate, w_exp_up,
           w_exp_down):
    batch, seq, d = x_prompt.shape
    dec_batch, dec_seq, _ = x_sample.shape
    depth = w_ada.shape[0]
    n_ctx = batch * seq
    n_lat = dec_batch * dec_seq
    hh = MLSTM_HEADS
    past = cache_mla_ckv.shape[2]

    x = (x_prompt.reshape(n_ctx, d), x_sample.reshape(n_lat, d))

    n_cv = 1 + dec_batch
    cvecs = jnp.concatenate([c_ctx[None, :], c, jnp.zeros((SUBLANES - n_cv % SUBLANES, d), F32)], axis=0)
    mod = ada_mod_all(cvecs, w_ada, b_ada).reshape(depth, cvecs.shape[0], 6, 1, d)

    w_router_pad = jnp.pad(w_router, ((0, 0), (0, LANES - N_EXPERTS)))
    w_router_hi = w_router_pad.astype(BF16)
    w_router_lo = (w_router_pad - w_router_hi.astype(F32)).astype(BF16)
    w_router_split = jnp.concatenate([w_router_hi, w_router_lo], axis=1)
    b_router_col = b_router.reshape(N_EXPERTS, 1)
    g_final = norm_final.reshape(1, d)
    row = lambda a: a.reshape(1, -1)

    outs = {}
    for i in range(depth):
        kind, j = i % 3, i // 3
        m = [mod[i, :n_cv, k] for k in range(6)]
        g_mix = row(norm_mix[i])
        if kind == 0:
            x = pool_layer(x, g_mix, m[0], m[1], m[2], w_pool[j], row(pool_scale[j]),
                           n_ctx, seq, dec_seq)
        elif kind == 1:
            qkv, o_gate, gcol, grow = mlstm_proj(x, g_mix, m[0], m[1], w_mlstm_in, j,
                                                 b_mlstm_gate[j], n_ctx, dec_seq)
            t = n_ctx + n_lat
            l = MLSTM_CHUNK
            grow_c = grow.reshape(4 * hh, t // l, l).transpose(1, 0, 2)
            hs_c, c_new, n_new, m_new = mlstm_scan(qkv, gcol, grow_c, None, 0, batch, seq)
            init = (state_mlstm_C[:, j], state_mlstm_n[:, j][:, :, :, None, :],
                    state_mlstm_m[:, j][:, :, :, None, None])
            hs_l, _, _, _ = mlstm_scan(qkv, gcol, grow_c, init, n_ctx, dec_batch, dec_seq)
            outs["C"] = c_new[:, None]
            outs["n"] = n_new[:, None, :, :, 0, :]
            outs["m"] = m_new[:, None, :, :, 0, 0]
            x = mlstm_out(x, hs_c, hs_l, o_gate, row(mlstm_head_g[j]), w_mlstm_out[j], m[2],
                          n_ctx, dec_seq)
        else:
            cos, sin = _rope_tables(n_ctx, dec_batch, dec_seq)
            q_cat, ckv, kpe = mla_proj(x, g_mix, m[0], m[1], w_mla_in[j], row(mla_q_g[j]),
                                       row(mla_kv_g[j]), w_mla_qb[j], cos, sin, n_ctx, dec_seq)
            lat_parts_c, lat_parts_p = [], []
            for b in range(dec_batch):
                lo = n_ctx + b * dec_seq
                lat_parts_c += [cache_mla_ckv[b, j], ckv[lo:lo + dec_seq]]
                lat_parts_p += [cache_mla_kpe[b, j], kpe[lo:lo + dec_seq]]
            ckv_all = jnp.concatenate(lat_parts_c + [ckv[:n_ctx]], axis=0)
            kp_all = jnp.concatenate(lat_parts_p + [kpe[:n_ctx]], axis=0).astype(BF16)
            kp_pad = jnp.pad(kp_all, ((0, 0), (0, LANES - kp_all.shape[1])))
            k_cat, v = mla_kv(ckv_all, kp_pad, w_mla_kvb[j])
            k_lat = past + dec_seq
            w_out = w_mla_out[j].astype(BF16)
            x = mla_attention(x, q_cat, k_cat, v, w_out, m[2], 0, dec_batch * k_lat, batch, seq, seq,
                              n_ctx, dec_seq)
            x = mla_attention(x, q_cat, k_cat, v, w_out, m[2], n_ctx, 0, dec_batch, dec_seq, k_lat,
                              n_ctx, dec_seq)
            outs["ckv"] = ckv[:n_ctx].reshape(batch, 1, seq, -1)
            outs["kpe"] = kpe[:n_ctx].reshape(batch, 1, seq, -1)
        x = moe_layer(x, row(norm_ffn[i]), m[3], m[4], m[5], w_router_split, b_router_col,
                      w_exp_gate, w_exp_up, w_exp_down, i, g_final, n_ctx, dec_seq,
                      final_norm=(i == depth - 1), split_output=(i == depth - 1))

    y_prompt = x[0].reshape(batch, seq, d)
    y_sample = x[1].reshape(dec_batch, dec_seq, d)
    return (y_prompt, y_sample, outs["C"], outs["n"], outs["m"], outs["ckv"], outs["kpe"])
```

```python
import functools

import numpy as np
import jax
import jax.numpy as jnp
from jax import lax
from jax.experimental import pallas as pl
from jax.experimental.pallas import tpu as pltpu

F32 = jnp.float32
BF16 = jnp.bfloat16

NORM_EPS = 1e-6
GRID_W = 64
POOL_WINDOWS = (2, 4, 8, 16)
MLSTM_HEADS = 4
MLSTM_CHUNK = 256
MLA_HEADS = 8
MLA_NOPE = 128
MLA_ROPE = 64
MLA_V = 128
ROPE_BASE = 10000.0
N_EXPERTS = 16
N_EXPERT_GROUPS = 4
EXPERTS_PER_GROUP = N_EXPERTS // N_EXPERT_GROUPS

LANES = 128
SUBLANES = 8
VMEM_LIMIT = 56 * 1024 * 1024
MOE_VMEM_LIMIT = 60 * 1024 * 1024
POOL_TILE = 256
POOL_HALO = 8
ATTN_Q_BLOCK = 256
MOE_TILE = 512
MOE_SEG_ALIGN = 16
MOE_ROW_BLOCK = 160
MOE_TAIL_BLOCK = 64
MOE_LOAD_EXPERTS = 2


def _params(*sem, vmem_limit=VMEM_LIMIT):
    return pltpu.CompilerParams(dimension_semantics=sem, vmem_limit_bytes=vmem_limit)


def _rms(x, g):
    return x * lax.rsqrt(jnp.mean(x * x, axis=-1, keepdims=True) + NORM_EPS) * g


def _modulated(x, g, shift, scale):
    return _rms(x, g) * (1.0 + scale) + shift


def _silu(x):
    return x * jax.nn.sigmoid(x)


def _bdot(a, b):
    return jnp.dot(a.astype(BF16), b.astype(BF16), preferred_element_type=F32)


def _bdot_nt(a, b):
    return lax.dot_general(a.astype(BF16), b.astype(BF16), (((1,), (1,)), ((), ())),
                           preferred_element_type=F32)


def _bdot_tn(a, b):
    return lax.dot_general(a.astype(BF16), b.astype(BF16), (((0,), (0,)), ((), ())),
                           preferred_element_type=F32)


def _cvec_index(n_ctx, dec_seq, tm):
    def idx(i):
        r = i * tm
        return jnp.where(r < n_ctx, 0, (r - n_ctx) // dec_seq + 1)
    return idx


def _ada_kernel(c_ref, w_ref, b_ref, o_ref):
    o_ref[...] = _bdot(_silu(c_ref[...]), w_ref[...]) + b_ref[...]


def ada_mod_all(cvecs, w_ada, b_ada, tn=3072):
    depth, d, n6 = w_ada.shape
    rows = cvecs.shape[0]
    return pl.pallas_call(
        _ada_kernel,
        grid=(depth, n6 // tn),
        in_specs=[pl.BlockSpec((rows, d), lambda l, n: (0, 0)),
                  pl.BlockSpec((None, d, tn), lambda l, n: (l, 0, n)),
                  pl.BlockSpec((None, 1, tn), lambda l, n: (l, 0, n))],
        out_specs=pl.BlockSpec((None, rows, tn), lambda l, n: (l, 0, n)),
        out_shape=jax.ShapeDtypeStruct((depth, rows, n6), F32),
        compiler_params=_params("arbitrary", "arbitrary"),
        name="ada_mod",
    )(cvecs, w_ada, b_ada.reshape(depth, 1, n6))


def _pool_kernel(*refs, n_ctx_tiles, ctx_seq_tiles, lat_seq_tiles, split_input):
    n_x = 6 if split_input else 3
    x_refs = refs[:n_x]
    g_ref, sh_ref, sc_ref, gt_ref, wp_ref, ps_ref, o_ref, buf_ref, *lvl_refs = refs[n_x:]
    i = pl.program_id(0)
    is_ctx = i < n_ctx_tiles
    j = jnp.where(is_ctx, i % ctx_seq_tiles, (i - n_ctx_tiles) % lat_seq_tiles)
    nt = jnp.where(is_ctx, ctx_seq_tiles, lat_seq_tiles)
    g, sh, sc = g_ref[...], sh_ref[...], sc_ref[...]
    tp, hl = POOL_TILE, POOL_HALO
    gw = o_ref.shape[1] // len(POOL_WINDOWS)

    def fill(xc_ref, xp_ref, xn_ref):
        buf_ref[pl.ds(0, hl), :] = jnp.where(j == 0, 0.0, _modulated(xp_ref[...], g, sh, sc))
        buf_ref[pl.ds(hl, tp), :] = _modulated(xc_ref[...], g, sh, sc)
        buf_ref[pl.ds(hl + tp, hl), :] = jnp.where(j == nt - 1, 0.0,
                                                    _modulated(xn_ref[...], g, sh, sc))
        o_ref[...] = xc_ref[...]

    if split_input:
        pl.when(is_ctx)(lambda: fill(*x_refs[:3]))
        pl.when(jnp.logical_not(is_ctx))(lambda: fill(*x_refs[3:]))
    else:
        fill(*x_refs)

    rows = tp + 2 * hl
    buf_ref[pl.ds(rows, hl), :] = jnp.zeros((hl, buf_ref.shape[1]), F32)
    for ref in lvl_refs:
        ref[pl.ds(rows, hl), :] = jnp.zeros((hl, gw), F32)
    pos = j * tp + lax.broadcasted_iota(jnp.int32, (tp, 1), 0)
    seq_len = nt * tp
    for gi, w in enumerate(POOL_WINDOWS):
        cols = pl.ds(gi * gw, gw)
        src, src_cols = buf_ref, cols
        for lvl in range(1, w.bit_length() - 1):
            step = 1 << (lvl - 1)
            dst = lvl_refs[lvl - 1]
            dst[pl.ds(0, rows), :] = src[pl.ds(0, rows), src_cols] + src[pl.ds(step, rows), src_cols]
            src, src_cols = dst, pl.ds(0, gw)
        acc = src[pl.ds(hl - w // 2, tp), src_cols] + src[pl.ds(hl, tp), src_cols]
        cnt = jnp.minimum(pos + w // 2, seq_len) - jnp.maximum(pos - w // 2, 0)
        pooled = acc / cnt.astype(F32) - buf_ref[pl.ds(hl, tp), cols]
        y = _bdot(pooled, wp_ref[gi]) * ps_ref[:, cols]
        o_ref[:, cols] = o_ref[:, cols] + gt_ref[:, cols] * y


def _halo_specs(tp, hl, d, tile_off, n_rows):
    hb = tp // hl
    last_tile, last_hblk = n_rows // tp - 1, n_rows // hl - 1
    tile = lambda i: jnp.clip(i - tile_off, 0, last_tile)
    return [pl.BlockSpec((tp, d), lambda i: (tile(i), 0)),
            pl.BlockSpec((hl, d), lambda i: (jnp.clip(tile(i) * hb - 1, 0, last_hblk), 0)),
            pl.BlockSpec((hl, d), lambda i: (jnp.clip((tile(i) + 1) * hb, 0, last_hblk), 0))]


def pool_layer(xs, g, shift, scale, gate, w_pool, pool_scale, n_ctx, seq, dec_seq):
    split = isinstance(xs, tuple)
    tp, hl = POOL_TILE, POOL_HALO
    if split:
        d = xs[0].shape[1]
        t = xs[0].shape[0] + xs[1].shape[0]
        x_specs = (_halo_specs(tp, hl, d, 0, xs[0].shape[0])
                   + _halo_specs(tp, hl, d, n_ctx // tp, xs[1].shape[0]))
        x_args = (xs[0],) * 3 + (xs[1],) * 3
    else:
        t, d = xs.shape
        x_specs = _halo_specs(tp, hl, d, 0, t)
        x_args = (xs,) * 3
    cidx = _cvec_index(n_ctx, dec_seq, tp)
    mod_spec = pl.BlockSpec((None, 1, d), lambda i: (cidx(i), 0, 0))
    row_spec = pl.BlockSpec((1, d), lambda i: (0, 0))
    assert all(w & (w - 1) == 0 and w // 2 <= hl for w in POOL_WINDOWS)
    n_levels = max(POOL_WINDOWS).bit_length() - 2
    kern = functools.partial(_pool_kernel, n_ctx_tiles=n_ctx // tp, ctx_seq_tiles=seq // tp,
                             lat_seq_tiles=dec_seq // tp, split_input=split)
    return pl.pallas_call(
        kern,
        grid=(t // tp,),
        in_specs=x_specs + [row_spec, mod_spec, mod_spec, mod_spec,
                            pl.BlockSpec(w_pool.shape, lambda i: (0, 0, 0)), row_spec],
        out_specs=pl.BlockSpec((tp, d), lambda i: (i, 0)),
        out_shape=jax.ShapeDtypeStruct((t, d), F32),
        scratch_shapes=[pltpu.VMEM((tp + 3 * hl, d), F32)]
        + [pltpu.VMEM((tp + 3 * hl, d // len(POOL_WINDOWS)), F32)] * n_levels,
        compiler_params=_params("arbitrary"),
        name="pool_mixer",
    )(*x_args, g, shift, scale, gate, w_pool, pool_scale)


def _route(sel, scores):
    e, tm = sel.shape
    row = lax.broadcasted_iota(jnp.int32, (e, tm), 0)
    best = jnp.zeros((1, tm), jnp.int32)
    best_sc = None
    for gidx in range(N_EXPERT_GROUPS):
        r = [sel[gidx * EXPERTS_PER_GROUP + k:gidx * EXPERTS_PER_GROUP + k + 1, :]
             for k in range(EXPERTS_PER_GROUP)]
        top2 = None
        for a in range(EXPERTS_PER_GROUP):
            for b in range(a + 1, EXPERTS_PER_GROUP):
                s = r[a] + r[b]
                top2 = s if top2 is None else jnp.maximum(top2, s)
        if best_sc is None:
            best_sc = top2
        else:
            better = top2 > best_sc
            best = jnp.where(better, gidx, best)
            best_sc = jnp.where(better, top2, best_sc)
    neg = -jnp.inf
    masked = jnp.where(row // EXPERTS_PER_GROUP == best, sel, neg)
    m1 = jnp.max(masked, axis=0, keepdims=True)
    i1 = jnp.min(jnp.where(masked == m1, row, e), axis=0, keepdims=True)
    masked2 = jnp.where(row == i1, neg, masked)
    m2 = jnp.max(masked2, axis=0, keepdims=True)
    i2 = jnp.min(jnp.where(masked2 == m2, row, e), axis=0, keepdims=True)
    hot1 = row == i1
    hot2 = row == i2
    w1 = jnp.sum(jnp.where(hot1, scores, 0.0), axis=0, keepdims=True)
    w2 = jnp.sum(jnp.where(hot2, scores, 0.0), axis=0, keepdims=True)
    tot = w1 + w2
    return best, jnp.where(hot1, w1 / tot, 0.0) + jnp.where(hot2, w2 / tot, 0.0)


def _split_bf16(a, parts):
    out = []
    for _ in range(parts):
        p = a.astype(BF16)
        out.append(p)
        a = a - p.astype(F32)
    return out


def _pad_rows(a, rows):
    return jnp.concatenate([a, jnp.zeros((rows - a.shape[0], a.shape[1]), a.dtype)], axis=0)


def _moe_kernel(x_ref, g_ref, sh_ref, sc_ref, gt_ref, wr_ref, br_ref, wgf_ref, wuf_ref, wdf_ref,
                gf_ref, *rest, final_norm, n_ctx_tiles, n_experts):
    out_refs, (wgu_ref, wd_ref), scratch = rest[:-5], rest[-5:-3], rest[-3:]
    i = pl.program_id(0)
    per_step, f = wgf_ref.shape[0], wgf_ref.shape[2]
    n_load = n_experts // per_step
    for s in range(n_load):
        @pl.when(i == s)
        def _(s=s):
            for j in range(per_step):
                gi, k = divmod(s * per_step + j, EXPERTS_PER_GROUP)
                wgu_ref[gi, :, pl.ds(2 * k * f, f)] = wgf_ref[j].astype(BF16)
                wgu_ref[gi, :, pl.ds((2 * k + 1) * f, f)] = wuf_ref[j].astype(BF16)
                wd_ref[gi, pl.ds(k * f, f), :] = wdf_ref[j].astype(BF16)

    @pl.when(i >= n_load)
    def _():
        _moe_tile(x_ref, g_ref, sh_ref, sc_ref, gt_ref, wr_ref, br_ref, wgu_ref, wd_ref,
                  gf_ref, out_refs, scratch, i - n_load, final_norm, n_ctx_tiles)


def _moe_tile(x_ref, g_ref, sh_ref, sc_ref, gt_ref, wr_ref, br_ref, wgu_ref, wd_ref,
              gf_ref, out_refs, scratch, tile, final_norm, n_ctx_tiles):
    hp_scr, cw_scr, yp_scr = scratch
    tr, d = x_ref.shape
    trp = hp_scr.shape[0]
    ng, eg = N_EXPERT_GROUPS, EXPERTS_PER_GROUP
    x = x_ref[...]
    h = _modulated(x, g_ref[...], sh_ref[...], sc_ref[...])
    hb = h.astype(BF16)

    h_lo = (h - hb.astype(F32)).astype(BF16)
    wr = wr_ref[...]
    lg = jnp.dot(hb, wr, preferred_element_type=F32)
    logits = lg[:, :LANES] + lg[:, LANES:] + jnp.dot(h_lo, wr[:, :LANES], preferred_element_type=F32)
    scores = jax.nn.sigmoid(logits.T[:N_EXPERTS, :])
    best, comb_t = _route(scores + br_ref[...], scores)

    grp = lax.broadcasted_iota(jnp.int32, (SUBLANES, tr), 0)
    hot_t = (grp == best).astype(F32)
    cw_t = hot_t[0:1, :] * comb_t[0:eg, :]
    for gi in range(1, ng):
        cw_t = cw_t + hot_t[gi:gi + 1, :] * comb_t[gi * eg:(gi + 1) * eg, :]
    cw_c = _pad_rows(cw_t, LANES).T

    ia = lax.broadcasted_iota(jnp.int32, (tr, tr), 0)
    ib = lax.broadcasted_iota(jnp.int32, (tr, tr), 1)
    before = jnp.where(ia < ib, 1.0, 0.0).astype(BF16)
    rank_t = jnp.dot(hot_t.astype(BF16), before, preferred_element_type=F32)

    starts, counts = [], []
    off = jnp.int32(0)
    for gi in range(ng):
        n = jnp.sum(hot_t[gi:gi + 1, :]).astype(jnp.int32)
        n = ((n + MOE_SEG_ALIGN - 1) // MOE_SEG_ALIGN) * MOE_SEG_ALIGN
        starts.append(off)
        counts.append(n)
        off = off + n

    pos_t = hot_t[0:1, :] * (rank_t[0:1, :] + starts[0].astype(F32))
    for gi in range(1, ng):
        pos_t = pos_t + hot_t[gi:gi + 1, :] * (rank_t[gi:gi + 1, :] + starts[gi].astype(F32))
    pos_c = _pad_rows(pos_t, LANES).T[:, 0:1]
    used = tr + ng * MOE_SEG_ALIGN
    perm = jnp.where(lax.broadcasted_iota(jnp.int32, (used, tr), 0) == pos_t.astype(jnp.int32),
                     1.0, 0.0).astype(BF16)
    perm_t = jnp.where(lax.broadcasted_iota(jnp.int32, (tr, used), 1) == pos_c.astype(jnp.int32),
                       1.0, 0.0).astype(BF16)

    hp_scr[pl.ds(0, used), :] = jnp.dot(perm, hb, preferred_element_type=F32).astype(BF16)
    hp_scr[pl.ds(used, trp - used), :] = jnp.zeros((trp - used, d), BF16)
    cw_pair = jnp.dot(perm, jnp.concatenate(_split_bf16(cw_c, 2), axis=1), preferred_element_type=F32)
    cw_scr[pl.ds(0, used), :] = cw_pair[:, :LANES] + cw_pair[:, LANES:]
    cw_scr[pl.ds(used, trp - used), :] = jnp.zeros((trp - used, LANES), F32)
    yp_scr[...] = jnp.zeros_like(yp_scr)

    f = wd_ref.shape[1] // eg

    def expert_block(gi, row0, n_rows, seg_end=None):
        rows = pl.ds(pl.multiple_of(row0, MOE_SEG_ALIGN), n_rows)
        cwb = cw_scr[rows, :]
        gu = jnp.dot(hp_scr[rows, :], wgu_ref[gi], preferred_element_type=F32)
        hid = [(_silu(gu[:, 2 * k * f:(2 * k + 1) * f]) * gu[:, (2 * k + 1) * f:(2 * k + 2) * f]
                * cwb[:, k:k + 1]).astype(BF16) for k in range(eg)]
        y = jnp.dot(jnp.concatenate(hid, axis=1), wd_ref[gi], preferred_element_type=F32).astype(BF16)
        if seg_end is None:
            yp_scr[rows, :] = y
        else:
            inside = row0 + lax.broadcasted_iota(jnp.int32, (n_rows, 1), 0) < seg_end
            yp_scr[rows, :] = jnp.where(inside, y, yp_scr[rows, :])

    sb, tb = MOE_ROW_BLOCK, MOE_TAIL_BLOCK
    for gi in range(ng):
        expert_block(gi, starts[gi], sb)
    for gi in range(ng):
        def tail(b, carry, gi=gi):
            expert_block(gi, starts[gi] + sb + b * tb, tb, starts[gi] + counts[gi])
            return carry
        lax.fori_loop(0, (jnp.maximum(counts[gi] - sb, 0) + tb - 1) // tb, tail, 0)

    moe = jnp.dot(perm_t, yp_scr[pl.ds(0, used), :], preferred_element_type=F32)
    out = x + gt_ref[...] * moe
    if final_norm:
        out = _rms(out, gf_ref[...])
    if n_ctx_tiles is None:
        out_refs[0][...] = out
    else:
        is_ctx = tile < n_ctx_tiles

        @pl.when(is_ctx)
        def _():
            out_refs[0][...] = out

        @pl.when(jnp.logical_not(is_ctx))
        def _():
            out_refs[1][...] = out


def moe_layer(x, g, shift, scale, gate, w_router_split, b_router_col, wg, wu, wd, layer, g_final,
              n_ctx, dec_seq, final_norm, split_output=False, tm=MOE_TILE):
    t, d = x.shape
    _, n_e, _, f = wg.shape
    trp = tm + N_EXPERT_GROUPS * MOE_SEG_ALIGN + max(MOE_ROW_BLOCK, MOE_TAIL_BLOCK)
    per_step = MOE_LOAD_EXPERTS
    n_load = n_e // per_step
    tile = lambda i: jnp.maximum(i - n_load, 0)
    cidx = _cvec_index(n_ctx, dec_seq, tm)
    mod_spec = pl.BlockSpec((None, 1, d), lambda i: (cidx(tile(i)), 0, 0))
    row_spec = pl.BlockSpec((1, d), lambda i: (0, 0))
    expert = lambda i: jnp.minimum(i, n_load - 1)
    nct = n_ctx // tm
    if split_output:
        out_specs = [pl.BlockSpec((tm, d), lambda i: (jnp.minimum(tile(i), nct - 1), 0)),
                     pl.BlockSpec((tm, d), lambda i: (jnp.maximum(tile(i) - nct, 0), 0))]
        out_shape = [jax.ShapeDtypeStruct((n_ctx, d), F32), jax.ShapeDtypeStruct((t - n_ctx, d), F32)]
    else:
        out_specs = pl.BlockSpec((tm, d), lambda i: (tile(i), 0))
        out_shape = jax.ShapeDtypeStruct((t, d), F32)
    return pl.pallas_call(
        functools.partial(_moe_kernel, final_norm=final_norm,
                          n_ctx_tiles=nct if split_output else None, n_experts=n_e),
        grid=(n_load + t // tm,),
        in_specs=[pl.BlockSpec((tm, d), lambda i: (tile(i), 0)),
                  row_spec, mod_spec, mod_spec, mod_spec,
                  pl.BlockSpec(w_router_split.shape, lambda i: (0, 0)),
                  pl.BlockSpec(b_router_col.shape, lambda i: (0, 0)),
                  pl.BlockSpec((None, per_step, d, f), lambda i: (layer, expert(i), 0, 0)),
                  pl.BlockSpec((None, per_step, d, f), lambda i: (layer, expert(i), 0, 0)),
                  pl.BlockSpec((None, per_step, f, d), lambda i: (layer, expert(i), 0, 0)),
                  row_spec],
        out_specs=out_specs,
        out_shape=out_shape,
        scratch_shapes=[pltpu.VMEM((N_EXPERT_GROUPS, d, 2 * EXPERTS_PER_GROUP * f), BF16),
                        pltpu.VMEM((N_EXPERT_GROUPS, EXPERTS_PER_GROUP * f, d), BF16),
                        pltpu.VMEM((trp, d), BF16), pltpu.VMEM((trp, LANES), F32),
                        pltpu.VMEM((trp, d), BF16)],
        compiler_params=_params("arbitrary", vmem_limit=MOE_VMEM_LIMIT),
        name="moe",
    )(x, g, shift, scale, gate, w_router_split, b_router_col, wg, wu, wd, g_final)


def _per_stream(n_ctx_tiles, ctx_ref, lat_ref, fn):
    i = pl.program_id(0)

    @pl.when(i < n_ctx_tiles)
    def _():
        fn(ctx_ref[...])

    @pl.when(i >= n_ctx_tiles)
    def _():
        fn(lat_ref[...])


def _stream_specs(tm, k, n_ctx_tiles):
    return [pl.BlockSpec((tm, k), lambda i: (jnp.minimum(i, n_ctx_tiles - 1), 0)),
            pl.BlockSpec((tm, k), lambda i: (jnp.maximum(i - n_ctx_tiles, 0), 0))]


def _mlstm_out_kernel(x_ref, hc_ref, hl_ref, og_ref, hg_ref, w_ref, gt_ref, o_ref, *, n_ctx_tiles):
    def run(hs):
        a = jax.nn.sigmoid(og_ref[...].astype(F32)) * (hs.astype(F32) * hg_ref[...])
        o_ref[...] = x_ref[...] + gt_ref[...] * _bdot(a, w_ref[...])
    _per_stream(n_ctx_tiles, hc_ref, hl_ref, run)


def mlstm_out(x, hs_ctx, hs_lat, o_gate, head_g, w, gate, n_ctx, dec_seq, tm=1024):
    t, d = x.shape
    k = hs_ctx.shape[1]
    cidx = _cvec_index(n_ctx, dec_seq, tm)
    nct = n_ctx // tm
    return pl.pallas_call(
        functools.partial(_mlstm_out_kernel, n_ctx_tiles=nct),
        grid=(t // tm,),
        in_specs=[pl.BlockSpec((tm, d), lambda i: (i, 0))] + _stream_specs(tm, k, nct) + [
            pl.BlockSpec((tm, k), lambda i: (i, 0)),
            pl.BlockSpec((1, k), lambda i: (0, 0)),
            pl.BlockSpec((k, d), lambda i: (0, 0)),
            pl.BlockSpec((None, 1, d), lambda i: (cidx(i), 0, 0))],
        out_specs=pl.BlockSpec((tm, d), lambda i: (i, 0)),
        out_shape=jax.ShapeDtypeStruct((t, d), F32),
        compiler_params=_params("arbitrary"),
        name="mlstm_out",
    )(x, hs_ctx, hs_lat, o_gate, head_g, w, gate)


def _mlstm_proj_kernel(x_ref, g_ref, sh_ref, sc_ref, wqk_ref, wv_ref, wo_ref, wgt_ref, bg_ref,
                       qs_ref, qkv_ref, o_ref, gc_ref, gr_ref):
    h = _modulated(x_ref[...], g_ref[...], sh_ref[...], sc_ref[...]).astype(BF16)
    nqk = wqk_ref.shape[0]
    qkv_ref[:, :nqk] = (_bdot_nt(h, wqk_ref[...]) * qs_ref[...]).astype(BF16)
    qkv_ref[:, nqk:] = _bdot_nt(h, wv_ref[...]).astype(BF16)
    o_ref[...] = _bdot_nt(h, wo_ref[...]).astype(BF16)
    gates = _bdot_nt(h, wgt_ref[...]) + bg_ref[...]
    gc_ref[...] = gates
    gr_ref[...] = gates.T[:gr_ref.shape[0], :]


def mlstm_proj(x, g, shift, scale, w_in_all, layer, b_gate, n_ctx, dec_seq, tm=512):
    t, d = x.shape
    hh = MLSTM_HEADS
    hv = d
    hk = hv // 2
    ng = 4 * hh
    w_t_all = jnp.swapaxes(w_in_all, 1, 2)
    w_g_pad = jnp.pad(w_t_all[layer, 2 * hk + 2 * hv:, :], ((0, LANES - ng), (0, 0)))
    b_pad = jnp.pad(b_gate.reshape(1, ng), ((0, 0), (0, LANES - ng)))
    dk = hk // hh
    q_scale = jnp.concatenate([jnp.full((1, hk), dk ** -0.5, F32), jnp.ones((1, hk), F32)], axis=1)
    cidx = _cvec_index(n_ctx, dec_seq, tm)
    mod_spec = pl.BlockSpec((None, 1, d), lambda i: (cidx(i), 0, 0))
    full = lambda a: pl.BlockSpec(a.shape, lambda i: (0,) * a.ndim)
    assert 2 * hk == hv
    w_col = lambda n: pl.BlockSpec((None, hv, d), lambda i: (layer, n, 0))
    return pl.pallas_call(
        _mlstm_proj_kernel,
        grid=(t // tm,),
        in_specs=[pl.BlockSpec((tm, d), lambda i: (i, 0)), full(g), mod_spec, mod_spec,
                  w_col(0), w_col(1), w_col(2), full(w_g_pad), full(b_pad), full(q_scale)],
        out_specs=[pl.BlockSpec((tm, 2 * hk + hv), lambda i: (i, 0)),
                   pl.BlockSpec((tm, hv), lambda i: (i, 0)),
                   pl.BlockSpec((tm, LANES), lambda i: (i, 0)),
                   pl.BlockSpec((ng, tm), lambda i: (0, i))],
        out_shape=[jax.ShapeDtypeStruct((t, 2 * hk + hv), BF16),
                   jax.ShapeDtypeStruct((t, hv), BF16),
                   jax.ShapeDtypeStruct((t, LANES), F32),
                   jax.ShapeDtypeStruct((ng, t), F32)],
        compiler_params=_params("arbitrary"),
        name="mlstm_proj",
    )(x, g, shift, scale, w_t_all, w_t_all, w_t_all, w_g_pad, b_pad, q_scale)


def _log_sigmoid(x):
    return jnp.minimum(x, 0.0) - jnp.log(1.0 + jnp.exp(-jnp.abs(x)))


def _gate_cumsums(gc, gr, causal_bf, feeds_bf):
    bc = br = None
    for part in _split_bf16(_log_sigmoid(gc), 3):
        t = jnp.dot(causal_bf, part, preferred_element_type=F32)
        bc = t if bc is None else bc + t
    for part in _split_bf16(_log_sigmoid(gr), 3):
        t = jnp.dot(part, feeds_bf, preferred_element_type=F32)
        br = t if br is None else br + t
    return bc, br


def _mlstm_chunk(q, k, v, i_col, b_col, i_row, b_row, c_st, n_st, m_st, causal, rev):
    l = q.shape[0]
    zero_state = c_st is None
    if zero_state:
        m_st = 0.0
    g_row = i_row - b_row
    log_w = jnp.where(causal, g_row, -jnp.inf)
    c_t = jnp.maximum(m_st, jnp.max(log_w, axis=1, keepdims=True))
    dw = jnp.exp(log_w - c_t)
    a = _bdot_nt(q, k) * dw
    a_hi, a_lo = _split_bf16(a, 2)
    ones = jnp.ones((l, LANES), BF16)
    den = (jnp.dot(a_hi, ones, preferred_element_type=F32)
           + jnp.dot(a_lo, ones, preferred_element_type=F32))[:, 0:1]
    if not zero_state:
        sw = jnp.exp(m_st - c_t)
        n_hi, n_lo = _split_bf16(jnp.broadcast_to(n_st, (LANES, n_st.shape[1])), 2)
        den = den + sw * (_bdot_nt(q, n_hi) + _bdot_nt(q, n_lo))[:, 0:1]
    inv = 1.0 / jnp.maximum(jnp.abs(den), jnp.exp(-(b_col + c_t)))
    h = inv * jnp.dot(a_hi, v, preferred_element_type=F32)
    if not zero_state:
        h = h + (sw * inv) * _bdot(q, c_st)
    b_last = b_col[0:1, :] if rev else b_col[l - 1:l, :]
    log_k = b_last - b_col + i_col
    m_new = jnp.maximum(b_last + m_st, jnp.max(log_k, axis=0, keepdims=True))
    kw = jnp.exp(log_k - m_new)
    kwk = kw * k.astype(F32)
    c_new = _bdot_tn(kwk, v)
    n_new = jnp.sum(kwk, axis=0, keepdims=True)
    if not zero_state:
        decay = jnp.exp(b_last + m_st - m_new)
        c_new = decay * c_st + c_new
        n_new = decay * n_st + n_new
    return h, c_new, n_new, m_new


def _mlstm_scan_kernel(*refs, n_chunks, zero_init):
    single = zero_init and n_chunks == 1
    if zero_init:
        q_ref, k_ref, v_ref, gc_ref, gr_ref, hs_ref, c_ref, n_ref, m_ref, hf_scr, hb_scr = refs
        if not single:
            c_ref[...] = jnp.zeros_like(c_ref)
            n_ref[...] = jnp.zeros_like(n_ref)
            m_ref[...] = jnp.zeros_like(m_ref)
    else:
        (q_ref, k_ref, v_ref, gc_ref, gr_ref, c0_ref, n0_ref, m0_ref,
         hs_ref, c_ref, n_ref, m_ref, hf_scr, hb_scr) = refs
        c_ref[...] = c0_ref[...]
        n_ref[...] = n0_ref[...]
        m_ref[...] = m0_ref[...]
    l = MLSTM_CHUNK
    hh = MLSTM_HEADS
    dk = q_ref.shape[1] // hh
    dv = v_ref.shape[1] // hh

    tt = lax.broadcasted_iota(jnp.int32, (l, l), 0)
    ss = lax.broadcasted_iota(jnp.int32, (l, l), 1)
    masks = (ss <= tt, ss >= tt)
    masks_bf = tuple(jnp.where(m, 1.0, 0.0).astype(BF16) for m in masks)

    def body(c, carry):
        cr = n_chunks - 1 - c
        start = (lambda j: j * l) if single else (lambda j: pl.multiple_of(j * l, l))
        rows = (pl.ds(start(c), l), pl.ds(start(cr), l))
        grs = (gr_ref[c], gr_ref[cr])
        outs, states = ([], []), []
        for d in range(2):
            q, k, v, gc, gr = q_ref[rows[d], :], k_ref[rows[d], :], v_ref[rows[d], :], gc_ref[rows[d], :], grs[d]
            bc, br = _gate_cumsums(gc, gr, masks_bf[d], masks_bf[1 - d])
            for h in range(hh):
                ci, cf = 2 * d * hh + h, (2 * d + 1) * hh + h
                state = (None,) * 3 if single else (c_ref[d, h], n_ref[d, h], m_ref[d, h])
                o, *st = _mlstm_chunk(q[:, h * dk:(h + 1) * dk], k[:, h * dk:(h + 1) * dk],
                                      v[:, h * dv:(h + 1) * dv], gc[:, ci:ci + 1], bc[:, cf:cf + 1],
                                      gr[ci:ci + 1, :], br[cf:cf + 1, :], *state, masks[d], d == 1)
                outs[d].append(o)
                states.append((d, h, st))
        hf_scr[rows[0], :] = jnp.concatenate(outs[0], axis=1)
        hb_scr[rows[1], :] = jnp.concatenate(outs[1], axis=1)
        for d, h, (c_new, n_new, m_new) in states:
            c_ref[d, h], n_ref[d, h], m_ref[d, h] = c_new, n_new, m_new
        return carry

    if single:
        body(0, 0)
    else:
        lax.fori_loop(0, n_chunks, body, 0)
    for h in range(hh):
        cols = pl.ds(h * dv, dv)
        hs = hf_scr[:, cols] + hb_scr[:, cols]
        hs_ref[:, cols] = (hs * lax.rsqrt(jnp.mean(hs * hs, axis=-1, keepdims=True)
                                          + NORM_EPS)).astype(BF16)


def mlstm_scan(qkv, gcol, grow, init, row_off, n_seq, seq_len):
    hh = MLSTM_HEADS
    hv = qkv.shape[1] // 2
    dv = hv // hh
    dk = dv // 2
    l = MLSTM_CHUNK
    nc = seq_len // l
    ob = row_off // seq_len
    kern = functools.partial(_mlstm_scan_kernel, n_chunks=nc, zero_init=init is None)
    st = lambda *tail: pl.BlockSpec((None, 2, hh) + tail, lambda s: (s, 0, 0) + (0,) * len(tail))
    states = [st(dk, dv), st(1, dk), st(1, 1)]
    return pl.pallas_call(
        kern,
        grid=(n_seq,),
        in_specs=[pl.BlockSpec((seq_len, hh * dk), lambda s: (ob + s, 0)),
                  pl.BlockSpec((seq_len, hh * dk), lambda s: (ob + s, 1)),
                  pl.BlockSpec((seq_len, hv), lambda s: (ob + s, 1)),
                  pl.BlockSpec((seq_len, LANES), lambda s: (ob + s, 0)),
                  pl.BlockSpec((nc, 4 * hh, l), lambda s: (ob + s, 0, 0))]
                 + ([] if init is None else states),
        out_specs=[pl.BlockSpec((seq_len, hv), lambda s: (s, 0))] + states,
        out_shape=[jax.ShapeDtypeStruct((n_seq * seq_len, hv), BF16),
                   jax.ShapeDtypeStruct((n_seq, 2, hh, dk, dv), F32),
                   jax.ShapeDtypeStruct((n_seq, 2, hh, 1, dk), F32),
                   jax.ShapeDtypeStruct((n_seq, 2, hh, 1, 1), F32)],
        scratch_shapes=[pltpu.VMEM((seq_len, hv), F32), pltpu.VMEM((seq_len, hv), F32)],
        compiler_params=_params("arbitrary"),
        name="mlstm_scan",
    )(qkv, qkv, qkv, gcol, grow, *(() if init is None else init))


def _mla_proj_kernel(x_ref, g_ref, sh_ref, sc_ref, win_ref, qg_ref, kvg_ref, wqb_ref, cos_ref,
                     sin_ref, q_ref, ckv_ref, kpe_ref, *, q_lora, kv_lora, rope, n_heads):
    h = _modulated(x_ref[...], g_ref[...], sh_ref[...], sc_ref[...])
    proj = _bdot_nt(h, win_ref[...])
    q_lat = proj[:, :q_lora]
    ckv_ref[...] = _rms(proj[:, q_lora:q_lora + kv_lora], kvg_ref[...])
    cos, sin = cos_ref[...], sin_ref[...]
    kpe = proj[:, q_lora + kv_lora:q_lora + kv_lora + rope]
    kpe_rot = proj[:, q_lora + kv_lora + rope:q_lora + kv_lora + 2 * rope]
    kpe_ref[...] = kpe * cos[:, :rope] + kpe_rot * sin[:, :rope]
    q = _bdot_nt(_rms(q_lat, qg_ref[...]), wqb_ref[...])
    n = n_heads * LANES
    pieces = []
    for hd in range(n_heads):
        c = slice(hd * LANES, (hd + 1) * LANES)
        pieces.append(q[:, c])
        pieces.append(q[:, n:2 * n][:, c] * cos + q[:, 2 * n:][:, c] * sin)
    q_ref[...] = jnp.concatenate(pieces, axis=1).astype(BF16)


def _rot_rows(wt):
    n, k = wt.shape
    quarter = MLA_ROPE // 4
    w4 = wt.reshape(n // (2 * quarter), 2, quarter, k)
    return jnp.concatenate([-w4[:, 1:2], w4[:, 0:1]], axis=1).reshape(n, k)


def _rope_tables(n_ctx, dec_batch, dec_seq):
    quarter = MLA_ROPE // 4
    freq = np.power(np.float32(ROPE_BASE), -np.arange(quarter, dtype=np.float32) / np.float32(quarter))
    pos = np.arange(dec_seq)
    ang_r = (pos // GRID_W).astype(np.float32)[:, None] * freq[None, :]
    ang_c = (pos % GRID_W).astype(np.float32)[:, None] * freq[None, :]
    ang = np.concatenate([ang_r, ang_r, ang_c, ang_c], axis=1).astype(np.float32)
    cos = np.concatenate([np.ones((n_ctx, MLA_ROPE), np.float32)] + [np.cos(ang)] * dec_batch, axis=0)
    sin = np.concatenate([np.zeros((n_ctx, MLA_ROPE), np.float32)] + [np.sin(ang)] * dec_batch, axis=0)
    reps = LANES // MLA_ROPE
    return (jnp.asarray(np.tile(cos, (1, reps)), F32), jnp.asarray(np.tile(sin, (1, reps)), F32))


def mla_proj(x, g, shift, scale, w_in, q_g, kv_g, w_qb, cos, sin, n_ctx, dec_seq, tm=512):
    t, d = x.shape
    hh, nope, rope = MLA_HEADS, MLA_NOPE, MLA_ROPE
    q_lora = q_g.shape[1]
    kv_lora = kv_g.shape[1]
    w_in_t = w_in.T
    w_in_ext = jnp.concatenate([w_in_t, _rot_rows(w_in_t[q_lora + kv_lora:])], axis=0)
    w3 = w_qb.T.reshape(hh, nope + rope, q_lora)
    w_qn = w3[:, :nope].reshape(hh * nope, q_lora)
    w_qp = w3[:, nope:].reshape(hh * rope, q_lora)
    assert nope == LANES and rope <= LANES
    head_pad = lambda w: jnp.pad(w.reshape(hh, rope, q_lora),
                                 ((0, 0), (0, LANES - rope), (0, 0))).reshape(hh * LANES, q_lora)
    w_qb_ext = jnp.concatenate([w_qn, head_pad(w_qp), head_pad(_rot_rows(w_qp))], axis=0)
    cidx = _cvec_index(n_ctx, dec_seq, tm)
    mod_spec = pl.BlockSpec((None, 1, d), lambda i: (cidx(i), 0, 0))
    full = lambda a: pl.BlockSpec(a.shape, lambda i: (0,) * a.ndim)
    kern = functools.partial(_mla_proj_kernel, q_lora=q_lora, kv_lora=kv_lora, rope=rope, n_heads=hh)
    return pl.pallas_call(
        kern,
        grid=(t // tm,),
        in_specs=[pl.BlockSpec((tm, d), lambda i: (i, 0)), full(g), mod_spec, mod_spec,
                  full(w_in_ext), full(q_g), full(kv_g), full(w_qb_ext),
                  pl.BlockSpec((tm, LANES), lambda i: (i, 0)),
                  pl.BlockSpec((tm, LANES), lambda i: (i, 0))],
        out_specs=[pl.BlockSpec((tm, 2 * hh * LANES), lambda i: (i, 0)),
                   pl.BlockSpec((tm, kv_lora), lambda i: (i, 0)),
                   pl.BlockSpec((tm, rope), lambda i: (i, 0))],
        out_shape=[jax.ShapeDtypeStruct((t, 2 * hh * LANES), BF16),
                   jax.ShapeDtypeStruct((t, kv_lora), F32),
                   jax.ShapeDtypeStruct((t, rope), F32)],
        compiler_params=_params("arbitrary"),
        name="mla_proj",
    )(x, g, shift, scale, w_in_ext, q_g, kv_g, w_qb_ext, cos, sin)


def _mla_kv_kernel(ckv_ref, kp_ref, w_ref, k_ref, v_ref):
    kv = _bdot(ckv_ref[...], w_ref[...])
    n = v_ref.shape[1]
    kp = kp_ref[...]
    pieces = []
    for hd in range(n // LANES):
        pieces += [kv[:, hd * LANES:(hd + 1) * LANES].astype(BF16), kp]
    k_ref[...] = jnp.concatenate(pieces, axis=1)
    v_ref[...] = kv[:, n:].astype(BF16)


def mla_kv(ckv_all, kp_pad, w_kvb, tm=1024):
    r, kv_lora = ckv_all.shape
    hh, nope, vd = MLA_HEADS, MLA_NOPE, MLA_V
    w3 = w_kvb.reshape(kv_lora, hh, nope + vd)
    w_perm = jnp.concatenate([w3[:, :, :nope].reshape(kv_lora, hh * nope),
                              w3[:, :, nope:].reshape(kv_lora, hh * vd)], axis=1)
    return pl.pallas_call(
        _mla_kv_kernel,
        grid=(r // tm,),
        in_specs=[pl.BlockSpec((tm, kv_lora), lambda i: (i, 0)),
                  pl.BlockSpec((tm, LANES), lambda i: (i, 0)),
                  pl.BlockSpec(w_perm.shape, lambda i: (0, 0))],
        out_specs=[pl.BlockSpec((tm, 2 * hh * LANES), lambda i: (i, 0)),
                   pl.BlockSpec((tm, hh * vd), lambda i: (i, 0))],
        out_shape=[jax.ShapeDtypeStruct((r, 2 * hh * LANES), BF16),
                   jax.ShapeDtypeStruct((r, hh * vd), BF16)],
        compiler_params=_params("arbitrary"),
        name="mla_kv",
    )(ckv_all, kp_pad, w_perm)


def _attn_kernel(q_ref, k_ref, v_ref, x_ref, w_ref, gt_ref, xo_ref, o_scr, *, scale):
    c = scale * np.log2(np.e)
    for h in range(MLA_HEADS):
        hk = slice(2 * h * LANES, 2 * (h + 1) * LANES)
        s = _bdot_nt(q_ref[:, hk], k_ref[:, hk])
        e = jnp.exp2((s - jnp.max(s, axis=-1, keepdims=True)) * c)
        o = _bdot(e, v_ref[:, h * MLA_V:(h + 1) * MLA_V]) / jnp.sum(e, axis=-1, keepdims=True)
        o_scr[:, h * MLA_V:(h + 1) * MLA_V] = o.astype(BF16)
    xo_ref[...] = x_ref[...] + gt_ref[...] * jnp.dot(o_scr[...], w_ref[...],
                                                     preferred_element_type=F32)


def mla_attention(x, q, k, v, w_out, gate, q_row_off, k_row_off, n_seq, q_len, k_len, n_ctx, dec_seq):
    tq = min(ATTN_Q_BLOCK, q_len)
    qb = q_len // tq
    q0 = q_row_off // tq
    k0 = k_row_off // k_len
    t, d = x.shape
    dq, dv = q.shape[1], v.shape[1]
    cidx = _cvec_index(n_ctx, dec_seq, tq)
    rows = lambda s, j: q0 + s * qb + j
    kern = functools.partial(_attn_kernel, scale=(MLA_NOPE + MLA_ROPE) ** -0.5)
    return pl.pallas_call(
        kern,
        grid=(n_seq, qb),
        in_specs=[pl.BlockSpec((tq, dq), lambda s, j: (rows(s, j), 0)),
                  pl.BlockSpec((k_len, dq), lambda s, j: (k0 + s, 0)),
                  pl.BlockSpec((k_len, dv), lambda s, j: (k0 + s, 0)),
                  pl.BlockSpec((tq, d), lambda s, j: (rows(s, j), 0)),
                  pl.BlockSpec(w_out.shape, lambda s, j: (0, 0)),
                  pl.BlockSpec((None, 1, d), lambda s, j: (cidx(rows(s, j)), 0, 0))],
        out_specs=pl.BlockSpec((tq, d), lambda s, j: (rows(s, j), 0)),
        out_shape=jax.ShapeDtypeStruct((t, d), F32),
        scratch_shapes=[pltpu.VMEM((tq, dv), BF16)],
        input_output_aliases={3: 0},
        compiler_params=_params("arbitrary", "arbitrary"),
        name="mla_attention",
    )(q, k, v, x, w_out, gate)


def kernel(x_prompt, x_sample, state_mlstm_C, state_mlstm_n, state_mlstm_m, cache_mla_ckv,
           cache_mla_kpe, c, c_ctx, w_ada, b_ada, norm_mix, norm_ffn, norm_final, w_pool,
           pool_scale, w_mlstm_in, b_mlstm_gate, mlstm_head_g, w_mlstm_out, w_mla_in, mla_q_g,
           mla_kv_g, w_mla_qb, w_mla_kvb, w_mla_out, w_router, b_router, w_exp_gate, w_exp_up,
           w_exp_down):
    batch, seq, d = x_prompt.shape
    dec_batch, dec_seq, _ = x_sample.shape
    depth = w_ada.shape[0]
    n_ctx = batch * seq
    n_lat = dec_batch * dec_seq
    hh = MLSTM_HEADS
    past = cache_mla_ckv.shape[2]

    x = (x_prompt.reshape(n_ctx, d), x_sample.reshape(n_lat, d))

    n_cv = 1 + dec_batch
    cvecs = jnp.concatenate([c_ctx[None, :], c, jnp.zeros((SUBLANES - n_cv % SUBLANES, d), F32)], axis=0)
    mod = ada_mod_all(cvecs, w_ada, b_ada).reshape(depth, cvecs.shape[0], 6, 1, d)

    w_router_pad = jnp.pad(w_router, ((0, 0), (0, LANES - N_EXPERTS)))
    w_router_hi = w_router_pad.astype(BF16)
    w_router_lo = (w_router_pad - w_router_hi.astype(F32)).astype(BF16)
    w_router_split = jnp.concatenate([w_router_hi, w_router_lo], axis=1)
    b_router_col = b_router.reshape(N_EXPERTS, 1)
    g_final = norm_final.reshape(1, d)
    row = lambda a: a.reshape(1, -1)

    outs = {}
    for i in range(depth):
        kind, j = i % 3, i // 3
        m = [mod[i, :n_cv, k] for k in range(6)]
        g_mix = row(norm_mix[i])
        if kind == 0:
            x = pool_layer(x, g_mix, m[0], m[1], m[2], w_pool[j], row(pool_scale[j]),
                           n_ctx, seq, dec_seq)
        elif kind == 1:
            qkv, o_gate, gcol, grow = mlstm_proj(x, g_mix, m[0], m[1], w_mlstm_in, j,
                                                 b_mlstm_gate[j], n_ctx, dec_seq)
            t = n_ctx + n_lat
            l = MLSTM_CHUNK
            grow_c = grow.reshape(4 * hh, t // l, l).transpose(1, 0, 2)
            hs_c, c_new, n_new, m_new = mlstm_scan(qkv, gcol, grow_c, None, 0, batch, seq)
            init = (state_mlstm_C[:, j], state_mlstm_n[:, j][:, :, :, None, :],
                    state_mlstm_m[:, j][:, :, :, None, None])
            hs_l, _, _, _ = mlstm_scan(qkv, gcol, grow_c, init, n_ctx, dec_batch, dec_seq)
            outs["C"] = c_new[:, None]
            outs["n"] = n_new[:, None, :, :, 0, :]
            outs["m"] = m_new[:, None, :, :, 0, 0]
            x = mlstm_out(x, hs_c, hs_l, o_gate, row(mlstm_head_g[j]), w_mlstm_out[j], m[2],
                          n_ctx, dec_seq)
        else:
            cos, sin = _rope_tables(n_ctx, dec_batch, dec_seq)
            q_cat, ckv, kpe = mla_proj(x, g_mix, m[0], m[1], w_mla_in[j], row(mla_q_g[j]),
                                       row(mla_kv_g[j]), w_mla_qb[j], cos, sin, n_ctx, dec_seq)
            lat_parts_c, lat_parts_p = [], []
            for b in range(dec_batch):
                lo = n_ctx + b * dec_seq
                lat_parts_c += [cache_mla_ckv[b, j], ckv[lo:lo + dec_seq]]
                lat_parts_p += [cache_mla_kpe[b, j], kpe[lo:lo + dec_seq]]
            ckv_all = jnp.concatenate(lat_parts_c + [ckv[:n_ctx]], axis=0)
            kp_all = jnp.concatenate(lat_parts_p + [kpe[:n_ctx]], axis=0).astype(BF16)
            kp_pad = jnp.pad(kp_all, ((0, 0), (0, LANES - kp_all.shape[1])))
            k_cat, v = mla_kv(ckv_all, kp_pad, w_mla_kvb[j])
            k_lat = past + dec_seq
            w_out = w_mla_out[j].astype(BF16)
            x = mla_attention(x, q_cat, k_cat, v, w_out, m[2], 0, dec_batch * k_lat, batch, seq, seq,
                              n_ctx, dec_seq)
            x = mla_attention(x, q_cat, k_cat, v, w_out, m[2], n_ctx, 0, dec_batch, dec_seq, k_lat,
                              n_ctx, dec_seq)
            outs["ckv"] = ckv[:n_ctx].reshape(batch, 1, seq, -1)
            outs["kpe"] = kpe[:n_ctx].reshape(batch, 1, seq, -1)
        x = moe_layer(x, row(norm_ffn[i]), m[3], m[4], m[5], w_router_split, b_router_col,
                      w_exp_gate, w_exp_up, w_exp_down, i, g_final, n_ctx, dec_seq,
                      final_norm=(i == depth - 1), split_output=(i == depth - 1))

    y_prompt = x[0].reshape(batch, seq, d)
    y_sample = x[1].reshape(dec_batch, dec_seq, d)
    return (y_prompt, y_sample, outs["C"], outs["n"], outs["m"], outs["ckv"], outs["kpe"])
```

```python
import functools

import numpy as np
import jax
import jax.numpy as jnp
from jax import lax
from jax.experimental import pallas as pl
from jax.experimental.pallas import tpu as pltpu

F32 = jnp.float32
BF16 = jnp.bfloat16

NORM_EPS = 1e-6
GRID_W = 64
POOL_WINDOWS = (2, 4, 8, 16)
MLSTM_HEADS = 4
MLSTM_CHUNK = 256
MLA_HEADS = 8
MLA_NOPE = 128
MLA_ROPE = 64
MLA_V = 128
ROPE_BASE = 10000.0
N_EXPERTS = 16
N_EXPERT_GROUPS = 4
EXPERTS_PER_GROUP = N_EXPERTS // N_EXPERT_GROUPS

LANES = 128
SUBLANES = 8
VMEM_LIMIT = 56 * 1024 * 1024
MOE_VMEM_LIMIT = 60 * 1024 * 1024
POOL_TILE = 256
POOL_HALO = 8
ATTN_Q_BLOCK = 256
MOE_TILE = 512
MOE_SEG_ALIGN = 16
MOE_ROW_BLOCK = 160
MOE_TAIL_BLOCK = 64
MOE_LOAD_EXPERTS = 2


def _params(*sem, vmem_limit=VMEM_LIMIT):
    return pltpu.CompilerParams(dimension_semantics=sem, vmem_limit_bytes=vmem_limit)


def _rms(x, g):
    return x * lax.rsqrt(jnp.mean(x * x, axis=-1, keepdims=True) + NORM_EPS) * g


def _modulated(x, g, shift, scale):
    return _rms(x, g) * (1.0 + scale) + shift


def _silu(x):
    return x * jax.nn.sigmoid(x)


def _bdot(a, b):
    return jnp.dot(a.astype(BF16), b.astype(BF16), preferred_element_type=F32)


def _bdot_nt(a, b):
    return lax.dot_general(a.astype(BF16), b.astype(BF16), (((1,), (1,)), ((), ())),
                           preferred_element_type=F32)


def _bdot_tn(a, b):
    return lax.dot_general(a.astype(BF16), b.astype(BF16), (((0,), (0,)), ((), ())),
                           preferred_element_type=F32)


def _cvec_index(n_ctx, dec_seq, tm):
    def idx(i):
        r = i * tm
        return jnp.where(r < n_ctx, 0, (r - n_ctx) // dec_seq + 1)
    return idx


def _ada_kernel(c_ref, w_ref, b_ref, o_ref):
    o_ref[...] = _bdot(_silu(c_ref[...]), w_ref[...]) + b_ref[...]


def ada_mod_all(cvecs, w_ada, b_ada, tn=3072):
    depth, d, n6 = w_ada.shape
    rows = cvecs.shape[0]
    return pl.pallas_call(
        _ada_kernel,
        grid=(depth, n6 // tn),
        in_specs=[pl.BlockSpec((rows, d), lambda l, n: (0, 0)),
                  pl.BlockSpec((None, d, tn), lambda l, n: (l, 0, n)),
                  pl.BlockSpec((None, 1, tn), lambda l, n: (l, 0, n))],
        out_specs=pl.BlockSpec((None, rows, tn), lambda l, n: (l, 0, n)),
        out_shape=jax.ShapeDtypeStruct((depth, rows, n6), F32),
        compiler_params=_params("arbitrary", "arbitrary"),
        name="ada_mod",
    )(cvecs, w_ada, b_ada.reshape(depth, 1, n6))


def _pool_kernel(*refs, n_ctx_tiles, ctx_seq_tiles, lat_seq_tiles, split_input):
    n_x = 6 if split_input else 3
    x_refs = refs[:n_x]
    g_ref, sh_ref, sc_ref, gt_ref, wp_ref, ps_ref, o_ref, buf_ref, *lvl_refs = refs[n_x:]
    i = pl.program_id(0)
    is_ctx = i < n_ctx_tiles
    j = jnp.where(is_ctx, i % ctx_seq_tiles, (i - n_ctx_tiles) % lat_seq_tiles)
    nt = jnp.where(is_ctx, ctx_seq_tiles, lat_seq_tiles)
    g, sh, sc = g_ref[...], sh_ref[...], sc_ref[...]
    tp, hl = POOL_TILE, POOL_HALO
    gw = o_ref.shape[1] // len(POOL_WINDOWS)

    def fill(xc_ref, xp_ref, xn_ref):
        buf_ref[pl.ds(0, hl), :] = jnp.where(j == 0, 0.0, _modulated(xp_ref[...], g, sh, sc))
        buf_ref[pl.ds(hl, tp), :] = _modulated(xc_ref[...], g, sh, sc)
        buf_ref[pl.ds(hl + tp, hl), :] = jnp.where(j == nt - 1, 0.0,
                                                    _modulated(xn_ref[...], g, sh, sc))
        o_ref[...] = xc_ref[...]

    if split_input:
        pl.when(is_ctx)(lambda: fill(*x_refs[:3]))
        pl.when(jnp.logical_not(is_ctx))(lambda: fill(*x_refs[3:]))
    else:
        fill(*x_refs)

    rows = tp + 2 * hl
    buf_ref[pl.ds(rows, hl), :] = jnp.zeros((hl, buf_ref.shape[1]), F32)
    for ref in lvl_refs:
        ref[pl.ds(rows, hl), :] = jnp.zeros((hl, gw), F32)
    pos = j * tp + lax.broadcasted_iota(jnp.int32, (tp, 1), 0)
    seq_len = nt * tp
    for gi, w in enumerate(POOL_WINDOWS):
        cols = pl.ds(gi * gw, gw)
        src, src_cols = buf_ref, cols
        for lvl in range(1, w.bit_length() - 1):
            step = 1 << (lvl - 1)
            dst = lvl_refs[lvl - 1]
            dst[pl.ds(0, rows), :] = src[pl.ds(0, rows), src_cols] + src[pl.ds(step, rows), src_cols]
            src, src_cols = dst, pl.ds(0, gw)
        acc = src[pl.ds(hl - w // 2, tp), src_cols] + src[pl.ds(hl, tp), src_cols]
        cnt = jnp.minimum(pos + w // 2, seq_len) - jnp.maximum(pos - w // 2, 0)
        pooled = acc / cnt.astype(F32) - buf_ref[pl.ds(hl, tp), cols]
        y = _bdot(pooled, wp_ref[gi]) * ps_ref[:, cols]
        o_ref[:, cols] = o_ref[:, cols] + gt_ref[:, cols] * y


def _halo_specs(tp, hl, d, tile_off, n_rows):
    hb = tp // hl
    last_tile, last_hblk = n_rows // tp - 1, n_rows // hl - 1
    tile = lambda i: jnp.clip(i - tile_off, 0, last_tile)
    return [pl.BlockSpec((tp, d), lambda i: (tile(i), 0)),
            pl.BlockSpec((hl, d), lambda i: (jnp.clip(tile(i) * hb - 1, 0, last_hblk), 0)),
            pl.BlockSpec((hl, d), lambda i: (jnp.clip((tile(i) + 1) * hb, 0, last_hblk), 0))]


def pool_layer(xs, g, shift, scale, gate, w_pool, pool_scale, n_ctx, seq, dec_seq):
    split = isinstance(xs, tuple)
    tp, hl = POOL_TILE, POOL_HALO
    if split:
        d = xs[0].shape[1]
        t = xs[0].shape[0] + xs[1].shape[0]
        x_specs = (_halo_specs(tp, hl, d, 0, xs[0].shape[0])
                   + _halo_specs(tp, hl, d, n_ctx // tp, xs[1].shape[0]))
        x_args = (xs[0],) * 3 + (xs[1],) * 3
    else:
        t, d = xs.shape
        x_specs = _halo_specs(tp, hl, d, 0, t)
        x_args = (xs,) * 3
    cidx = _cvec_index(n_ctx, dec_seq, tp)
    mod_spec = pl.BlockSpec((None, 1, d), lambda i: (cidx(i), 0, 0))
    row_spec = pl.BlockSpec((1, d), lambda i: (0, 0))
    assert all(w & (w - 1) == 0 and w // 2 <= hl for w in POOL_WINDOWS)
    n_levels = max(POOL_WINDOWS).bit_length() - 2
    kern = functools.partial(_pool_kernel, n_ctx_tiles=n_ctx // tp, ctx_seq_tiles=seq // tp,
                             lat_seq_tiles=dec_seq // tp, split_input=split)
    return pl.pallas_call(
        kern,
        grid=(t // tp,),
        in_specs=x_specs + [row_spec, mod_spec, mod_spec, mod_spec,
                            pl.BlockSpec(w_pool.shape, lambda i: (0, 0, 0)), row_spec],
        out_specs=pl.BlockSpec((tp, d), lambda i: (i, 0)),
        out_shape=jax.ShapeDtypeStruct((t, d), F32),
        scratch_shapes=[pltpu.VMEM((tp + 3 * hl, d), F32)]
        + [pltpu.VMEM((tp + 3 * hl, d // len(POOL_WINDOWS)), F32)] * n_levels,
        compiler_params=_params("arbitrary"),
        name="pool_mixer",
    )(*x_args, g, shift, scale, gate, w_pool, pool_scale)


def _route(sel, scores):
    e, tm = sel.shape
    row = lax.broadcasted_iota(jnp.int32, (e, tm), 0)
    best = jnp.zeros((1, tm), jnp.int32)
    best_sc = None
    for gidx in range(N_EXPERT_GROUPS):
        r = [sel[gidx * EXPERTS_PER_GROUP + k:gidx * EXPERTS_PER_GROUP + k + 1, :]
             for k in range(EXPERTS_PER_GROUP)]
        top2 = None
        for a in range(EXPERTS_PER_GROUP):
            for b in range(a + 1, EXPERTS_PER_GROUP):
                s = r[a] + r[b]
                top2 = s if top2 is None else jnp.maximum(top2, s)
        if best_sc is None:
            best_sc = top2
        else:
            better = top2 > best_sc
            best = jnp.where(better, gidx, best)
            best_sc = jnp.where(better, top2, best_sc)
    neg = -jnp.inf
    masked = jnp.where(row // EXPERTS_PER_GROUP == best, sel, neg)
    m1 = jnp.max(masked, axis=0, keepdims=True)
    i1 = jnp.min(jnp.where(masked == m1, row, e), axis=0, keepdims=True)
    masked2 = jnp.where(row == i1, neg, masked)
    m2 = jnp.max(masked2, axis=0, keepdims=True)
    i2 = jnp.min(jnp.where(masked2 == m2, row, e), axis=0, keepdims=True)
    hot1 = row == i1
    hot2 = row == i2
    w1 = jnp.sum(jnp.where(hot1, scores, 0.0), axis=0, keepdims=True)
    w2 = jnp.sum(jnp.where(hot2, scores, 0.0), axis=0, keepdims=True)
    tot = w1 + w2
    return best, jnp.where(hot1, w1 / tot, 0.0) + jnp.where(hot2, w2 / tot, 0.0)


def _split_bf16(a, parts):
    out = []
    for _ in range(parts):
        p = a.astype(BF16)
        out.append(p)
        a = a - p.astype(F32)
    return out


def _pad_rows(a, rows):
    return jnp.concatenate([a, jnp.zeros((rows - a.shape[0], a.shape[1]), a.dtype)], axis=0)


def _moe_kernel(x_ref, g_ref, sh_ref, sc_ref, gt_ref, wr_ref, br_ref, wgf_ref, wuf_ref, wdf_ref,
                gf_ref, *rest, final_norm, n_ctx_tiles, n_experts):
    out_refs, (wgu_ref, wd_ref), scratch = rest[:-5], rest[-5:-3], rest[-3:]
    i = pl.program_id(0)
    per_step, f = wgf_ref.shape[0], wgf_ref.shape[2]
    n_load = n_experts // per_step
    for s in range(n_load):
        @pl.when(i == s)
        def _(s=s):
            for j in range(per_step):
                gi, k = divmod(s * per_step + j, EXPERTS_PER_GROUP)
                wgu_ref[gi, :, pl.ds(2 * k * f, f)] = wgf_ref[j].astype(BF16)
                wgu_ref[gi, :, pl.ds((2 * k + 1) * f, f)] = wuf_ref[j].astype(BF16)
                wd_ref[gi, pl.ds(k * f, f), :] = wdf_ref[j].astype(BF16)

    @pl.when(i >= n_load)
    def _():
        _moe_tile(x_ref, g_ref, sh_ref, sc_ref, gt_ref, wr_ref, br_ref, wgu_ref, wd_ref,
                  gf_ref, out_refs, scratch, i - n_load, final_norm, n_ctx_tiles)


def _moe_tile(x_ref, g_ref, sh_ref, sc_ref, gt_ref, wr_ref, br_ref, wgu_ref, wd_ref,
              gf_ref, out_refs, scratch, tile, final_norm, n_ctx_tiles):
    hp_scr, cw_scr, yp_scr = scratch
    tr, d = x_ref.shape
    trp = hp_scr.shape[0]
    ng, eg = N_EXPERT_GROUPS, EXPERTS_PER_GROUP
    x = x_ref[...]
    h = _modulated(x, g_ref[...], sh_ref[...], sc_ref[...])
    hb = h.astype(BF16)

    h_lo = (h - hb.astype(F32)).astype(BF16)
    wr = wr_ref[...]
    lg = jnp.dot(hb, wr, preferred_element_type=F32)
    logits = lg[:, :LANES] + lg[:, LANES:] + jnp.dot(h_lo, wr[:, :LANES], preferred_element_type=F32)
    scores = jax.nn.sigmoid(logits.T[:N_EXPERTS, :])
    best, comb_t = _route(scores + br_ref[...], scores)

    grp = lax.broadcasted_iota(jnp.int32, (SUBLANES, tr), 0)
    hot_t = (grp == best).astype(F32)
    cw_t = hot_t[0:1, :] * comb_t[0:eg, :]
    for gi in range(1, ng):
        cw_t = cw_t + hot_t[gi:gi + 1, :] * comb_t[gi * eg:(gi + 1) * eg, :]
    cw_c = _pad_rows(cw_t, LANES).T

    ia = lax.broadcasted_iota(jnp.int32, (tr, tr), 0)
    ib = lax.broadcasted_iota(jnp.int32, (tr, tr), 1)
    before = jnp.where(ia < ib, 1.0, 0.0).astype(BF16)
    rank_t = jnp.dot(hot_t.astype(BF16), before, preferred_element_type=F32)

    starts, counts = [], []
    off = jnp.int32(0)
    for gi in range(ng):
        n = jnp.sum(hot_t[gi:gi + 1, :]).astype(jnp.int32)
        n = ((n + MOE_SEG_ALIGN - 1) // MOE_SEG_ALIGN) * MOE_SEG_ALIGN
        starts.append(off)
        counts.append(n)
        off = off + n

    pos_t = hot_t[0:1, :] * (rank_t[0:1, :] + starts[0].astype(F32))
    for gi in range(1, ng):
        pos_t = pos_t + hot_t[gi:gi + 1, :] * (rank_t[gi:gi + 1, :] + starts[gi].astype(F32))
    pos_c = _pad_rows(pos_t, LANES).T[:, 0:1]
    used = tr + ng * MOE_SEG_ALIGN
    perm = jnp.where(lax.broadcasted_iota(jnp.int32, (used, tr), 0) == pos_t.astype(jnp.int32),
                     1.0, 0.0).astype(BF16)
    perm_t = jnp.where(lax.broadcasted_iota(jnp.int32, (tr, used), 1) == pos_c.astype(jnp.int32),
                       1.0, 0.0).astype(BF16)

    hp_scr[pl.ds(0, used), :] = jnp.dot(perm, hb, preferred_element_type=F32).astype(BF16)
    hp_scr[pl.ds(used, trp - used), :] = jnp.zeros((trp - used, d), BF16)
    cw_pair = jnp.dot(perm, jnp.concatenate(_split_bf16(cw_c, 2), axis=1), preferred_element_type=F32)
    cw_scr[pl.ds(0, used), :] = cw_pair[:, :LANES] + cw_pair[:, LANES:]
    cw_scr[pl.ds(used, trp - used), :] = jnp.zeros((trp - used, LANES), F32)
    yp_scr[...] = jnp.zeros_like(yp_scr)

    f = wd_ref.shape[1] // eg

    def expert_block(gi, row0, n_rows, seg_end=None):
        rows = pl.ds(pl.multiple_of(row0, MOE_SEG_ALIGN), n_rows)
        cwb = cw_scr[rows, :]
        gu = jnp.dot(hp_scr[rows, :], wgu_ref[gi], preferred_element_type=F32)
        hid = [(_silu(gu[:, 2 * k * f:(2 * k + 1) * f]) * gu[:, (2 * k + 1) * f:(2 * k + 2) * f]
                * cwb[:, k:k + 1]).astype(BF16) for k in range(eg)]
        y = jnp.dot(jnp.concatenate(hid, axis=1), wd_ref[gi], preferred_element_type=F32).astype(BF16)
        if seg_end is None:
            yp_scr[rows, :] = y
        else:
            inside = row0 + lax.broadcasted_iota(jnp.int32, (n_rows, 1), 0) < seg_end
            yp_scr[rows, :] = jnp.where(inside, y, yp_scr[rows, :])

    sb, tb = MOE_ROW_BLOCK, MOE_TAIL_BLOCK
    for gi in range(ng):
        expert_block(gi, starts[gi], sb)
    for gi in range(ng):
        def tail(b, carry, gi=gi):
            expert_block(gi, starts[gi] + sb + b * tb, tb, starts[gi] + counts[gi])
            return carry
        lax.fori_loop(0, (jnp.maximum(counts[gi] - sb, 0) + tb - 1) // tb, tail, 0)

    moe = jnp.dot(perm_t, yp_scr[pl.ds(0, used), :], preferred_element_type=F32)
    out = x + gt_ref[...] * moe
    if final_norm:
        out = _rms(out, gf_ref[...])
    if n_ctx_tiles is None:
        out_refs[0][...] = out
    else:
        is_ctx = tile < n_ctx_tiles

        @pl.when(is_ctx)
        def _():
            out_refs[0][...] = out

        @pl.when(jnp.logical_not(is_ctx))
        def _():
            out_refs[1][...] = out


def moe_layer(x, g, shift, scale, gate, w_router_split, b_router_col, wg, wu, wd, layer, g_final,
              n_ctx, dec_seq, final_norm, split_output=False, tm=MOE_TILE):
    t, d = x.shape
    _, n_e, _, f = wg.shape
    trp = tm + N_EXPERT_GROUPS * MOE_SEG_ALIGN + max(MOE_ROW_BLOCK, MOE_TAIL_BLOCK)
    per_step = MOE_LOAD_EXPERTS
    n_load = n_e // per_step
    tile = lambda i: jnp.maximum(i - n_load, 0)
    cidx = _cvec_index(n_ctx, dec_seq, tm)
    mod_spec = pl.BlockSpec((None, 1, d), lambda i: (cidx(tile(i)), 0, 0))
    row_spec = pl.BlockSpec((1, d), lambda i: (0, 0))
    expert = lambda i: jnp.minimum(i, n_load - 1)
    nct = n_ctx // tm
    if split_output:
        out_specs = [pl.BlockSpec((tm, d), lambda i: (jnp.minimum(tile(i), nct - 1), 0)),
                     pl.BlockSpec((tm, d), lambda i: (jnp.maximum(tile(i) - nct, 0), 0))]
        out_shape = [jax.ShapeDtypeStruct((n_ctx, d), F32), jax.ShapeDtypeStruct((t - n_ctx, d), F32)]
    else:
        out_specs = pl.BlockSpec((tm, d), lambda i: (tile(i), 0))
        out_shape = jax.ShapeDtypeStruct((t, d), F32)
    return pl.pallas_call(
        functools.partial(_moe_kernel, final_norm=final_norm,
                          n_ctx_tiles=nct if split_output else None, n_experts=n_e),
        grid=(n_load + t // tm,),
        in_specs=[pl.BlockSpec((tm, d), lambda i: (tile(i), 0)),
                  row_spec, mod_spec, mod_spec, mod_spec,
                  pl.BlockSpec(w_router_split.shape, lambda i: (0, 0)),
                  pl.BlockSpec(b_router_col.shape, lambda i: (0, 0)),
                  pl.BlockSpec((None, per_step, d, f), lambda i: (layer, expert(i), 0, 0)),
                  pl.BlockSpec((None, per_step, d, f), lambda i: (layer, expert(i), 0, 0)),
                  pl.BlockSpec((None, per_step, f, d), lambda i: (layer, expert(i), 0, 0)),
                  row_spec],
        out_specs=out_specs,
        out_shape=out_shape,
        scratch_shapes=[pltpu.VMEM((N_EXPERT_GROUPS, d, 2 * EXPERTS_PER_GROUP * f), BF16),
                        pltpu.VMEM((N_EXPERT_GROUPS, EXPERTS_PER_GROUP * f, d), BF16),
                        pltpu.VMEM((trp, d), BF16), pltpu.VMEM((trp, LANES), F32),
                        pltpu.VMEM((trp, d), BF16)],
        compiler_params=_params("arbitrary", vmem_limit=MOE_VMEM_LIMIT),
        name="moe",
    )(x, g, shift, scale, gate, w_router_split, b_router_col, wg, wu, wd, g_final)


def _per_stream(n_ctx_tiles, ctx_ref, lat_ref, fn):
    i = pl.program_id(0)

    @pl.when(i < n_ctx_tiles)
    def _():
        fn(ctx_ref[...])

    @pl.when(i >= n_ctx_tiles)
    def _():
        fn(lat_ref[...])


def _stream_specs(tm, k, n_ctx_tiles):
    return [pl.BlockSpec((tm, k), lambda i: (jnp.minimum(i, n_ctx_tiles - 1), 0)),
            pl.BlockSpec((tm, k), lambda i: (jnp.maximum(i - n_ctx_tiles, 0), 0))]


def _mlstm_out_kernel(x_ref, hc_ref, hl_ref, og_ref, hg_ref, w_ref, gt_ref, o_ref, *, n_ctx_tiles):
    def run(hs):
        a = jax.nn.sigmoid(og_ref[...].astype(F32)) * (hs.astype(F32) * hg_ref[...])
        o_ref[...] = x_ref[...] + gt_ref[...] * _bdot(a, w_ref[...])
    _per_stream(n_ctx_tiles, hc_ref, hl_ref, run)


def mlstm_out(x, hs_ctx, hs_lat, o_gate, head_g, w, gate, n_ctx, dec_seq, tm=1024):
    t, d = x.shape
    k = hs_ctx.shape[1]
    cidx = _cvec_index(n_ctx, dec_seq, tm)
    nct = n_ctx // tm
    return pl.pallas_call(
        functools.partial(_mlstm_out_kernel, n_ctx_tiles=nct),
        grid=(t // tm,),
        in_specs=[pl.BlockSpec((tm, d), lambda i: (i, 0))] + _stream_specs(tm, k, nct) + [
            pl.BlockSpec((tm, k), lambda i: (i, 0)),
            pl.BlockSpec((1, k), lambda i: (0, 0)),
            pl.BlockSpec((k, d), lambda i: (0, 0)),
            pl.BlockSpec((None, 1, d), lambda i: (cidx(i), 0, 0))],
        out_specs=pl.BlockSpec((tm, d), lambda i: (i, 0)),
        out_shape=jax.ShapeDtypeStruct((t, d), F32),
        compiler_params=_params("arbitrary"),
        name="mlstm_out",
    )(x, hs_ctx, hs_lat, o_gate, head_g, w, gate)


def _mlstm_proj_kernel(x_ref, g_ref, sh_ref, sc_ref, wqk_ref, wv_ref, wo_ref, wgt_ref, bg_ref,
                       qs_ref, qkv_ref, o_ref, gc_ref, gr_ref):
    h = _modulated(x_ref[...], g_ref[...], sh_ref[...], sc_ref[...]).astype(BF16)
    nqk = wqk_ref.shape[0]
    qkv_ref[:, :nqk] = (_bdot_nt(h, wqk_ref[...]) * qs_ref[...]).astype(BF16)
    qkv_ref[:, nqk:] = _bdot_nt(h, wv_ref[...]).astype(BF16)
    o_ref[...] = _bdot_nt(h, wo_ref[...]).astype(BF16)
    gates = _bdot_nt(h, wgt_ref[...]) + bg_ref[...]
    gc_ref[...] = gates
    gr_ref[...] = gates.T[:gr_ref.shape[0], :]


def mlstm_proj(x, g, shift, scale, w_in_all, layer, b_gate, n_ctx, dec_seq, tm=512):
    t, d = x.shape
    hh = MLSTM_HEADS
    hv = d
    hk = hv // 2
    ng = 4 * hh
    w_t_all = jnp.swapaxes(w_in_all, 1, 2)
    w_g_pad = jnp.pad(w_t_all[layer, 2 * hk + 2 * hv:, :], ((0, LANES - ng), (0, 0)))
    b_pad = jnp.pad(b_gate.reshape(1, ng), ((0, 0), (0, LANES - ng)))
    dk = hk // hh
    q_scale = jnp.concatenate([jnp.full((1, hk), dk ** -0.5, F32), jnp.ones((1, hk), F32)], axis=1)
    cidx = _cvec_index(n_ctx, dec_seq, tm)
    mod_spec = pl.BlockSpec((None, 1, d), lambda i: (cidx(i), 0, 0))
    full = lambda a: pl.BlockSpec(a.shape, lambda i: (0,) * a.ndim)
    assert 2 * hk == hv
    w_col = lambda n: pl.BlockSpec((None, hv, d), lambda i: (layer, n, 0))
    return pl.pallas_call(
        _mlstm_proj_kernel,
        grid=(t // tm,),
        in_specs=[pl.BlockSpec((tm, d), lambda i: (i, 0)), full(g), mod_spec, mod_spec,
                  w_col(0), w_col(1), w_col(2), full(w_g_pad), full(b_pad), full(q_scale)],
        out_specs=[pl.BlockSpec((tm, 2 * hk + hv), lambda i: (i, 0)),
                   pl.BlockSpec((tm, hv), lambda i: (i, 0)),
                   pl.BlockSpec((tm, LANES), lambda i: (i, 0)),
                   pl.BlockSpec((ng, tm), lambda i: (0, i))],
        out_shape=[jax.ShapeDtypeStruct((t, 2 * hk + hv), BF16),
                   jax.ShapeDtypeStruct((t, hv), BF16),
                   jax.ShapeDtypeStruct((t, LANES), F32),
                   jax.ShapeDtypeStruct((ng, t), F32)],
        compiler_params=_params("arbitrary"),
        name="mlstm_proj",
    )(x, g, shift, scale, w_t_all, w_t_all, w_t_all, w_g_pad, b_pad, q_scale)


def _log_sigmoid(x):
    return jnp.minimum(x, 0.0) - jnp.log(1.0 + jnp.exp(-jnp.abs(x)))


def _gate_cumsums(gc, gr, causal_bf, feeds_bf):
    bc = br = None
    for part in _split_bf16(_log_sigmoid(gc), 3):
        t = jnp.dot(causal_bf, part, preferred_element_type=F32)
        bc = t if bc is None else bc + t
    for part in _split_bf16(_log_sigmoid(gr), 3):
        t = jnp.dot(part, feeds_bf, preferred_element_type=F32)
        br = t if br is None else br + t
    return bc, br


def _mlstm_chunk(q, k, v, i_col, b_col, i_row, b_row, c_st, n_st, m_st, causal, rev):
    l = q.shape[0]
    zero_state = c_st is None
    if zero_state:
        m_st = 0.0
    g_row = i_row - b_row
    log_w = jnp.where(causal, g_row, -jnp.inf)
    c_t = jnp.maximum(m_st, jnp.max(log_w, axis=1, keepdims=True))
    dw = jnp.exp(log_w - c_t)
    a = _bdot_nt(q, k) * dw
    a_hi, a_lo = _split_bf16(a, 2)
    ones = jnp.ones((l, LANES), BF16)
    den = (jnp.dot(a_hi, ones, preferred_element_type=F32)
           + jnp.dot(a_lo, ones, preferred_element_type=F32))[:, 0:1]
    if not zero_state:
        sw = jnp.exp(m_st - c_t)
        n_hi, n_lo = _split_bf16(jnp.broadcast_to(n_st, (LANES, n_st.shape[1])), 2)
        den = den + sw * (_bdot_nt(q, n_hi) + _bdot_nt(q, n_lo))[:, 0:1]
    inv = 1.0 / jnp.maximum(jnp.abs(den), jnp.exp(-(b_col + c_t)))
    h = inv * jnp.dot(a_hi, v, preferred_element_type=F32)
    if not zero_state:
        h = h + (sw * inv) * _bdot(q, c_st)
    b_last = b_col[0:1, :] if rev else b_col[l - 1:l, :]
    log_k = b_last - b_col + i_col
    m_new = jnp.maximum(b_last + m_st, jnp.max(log_k, axis=0, keepdims=True))
    kw = jnp.exp(log_k - m_new)
    kwk = kw * k.astype(F32)
    c_new = _bdot_tn(kwk, v)
    n_new = jnp.sum(kwk, axis=0, keepdims=True)
    if not zero_state:
        decay = jnp.exp(b_last + m_st - m_new)
        c_new = decay * c_st + c_new
        n_new = decay * n_st + n_new
    return h, c_new, n_new, m_new


def _mlstm_scan_kernel(*refs, n_chunks, zero_init):
    single = zero_init and n_chunks == 1
    if zero_init:
        q_ref, k_ref, v_ref, gc_ref, gr_ref, hs_ref, c_ref, n_ref, m_ref, hf_scr, hb_scr = refs
        if not single:
            c_ref[...] = jnp.zeros_like(c_ref)
            n_ref[...] = jnp.zeros_like(n_ref)
            m_ref[...] = jnp.zeros_like(m_ref)
    else:
        (q_ref, k_ref, v_ref, gc_ref, gr_ref, c0_ref, n0_ref, m0_ref,
         hs_ref, c_ref, n_ref, m_ref, hf_scr, hb_scr) = refs
        c_ref[...] = c0_ref[...]
        n_ref[...] = n0_ref[...]
        m_ref[...] = m0_ref[...]
    l = MLSTM_CHUNK
    hh = MLSTM_HEADS
    dk = q_ref.shape[1] // hh
    dv = v_ref.shape[1] // hh

    tt = lax.broadcasted_iota(jnp.int32, (l, l), 0)
    ss = lax.broadcasted_iota(jnp.int32, (l, l), 1)
    masks = (ss <= tt, ss >= tt)
    masks_bf = tuple(jnp.where(m, 1.0, 0.0).astype(BF16) for m in masks)

    def body(c, carry):
        cr = n_chunks - 1 - c
        start = (lambda j: j * l) if single else (lambda j: pl.multiple_of(j * l, l))
        rows = (pl.ds(start(c), l), pl.ds(start(cr), l))
        grs = (gr_ref[c], gr_ref[cr])
        outs, states = ([], []), []
        for d in range(2):
            q, k, v, gc, gr = q_ref[rows[d], :], k_ref[rows[d], :], v_ref[rows[d], :], gc_ref[rows[d], :], grs[d]
            bc, br = _gate_cumsums(gc, gr, masks_bf[d], masks_bf[1 - d])
            for h in range(hh):
                ci, cf = 2 * d * hh + h, (2 * d + 1) * hh + h
                state = (None,) * 3 if single else (c_ref[d, h], n_ref[d, h], m_ref[d, h])
                o, *st = _mlstm_chunk(q[:, h * dk:(h + 1) * dk], k[:, h * dk:(h + 1) * dk],
                                      v[:, h * dv:(h + 1) * dv], gc[:, ci:ci + 1], bc[:, cf:cf + 1],
                                      gr[ci:ci + 1, :], br[cf:cf + 1, :], *state, masks[d], d == 1)
                outs[d].append(o)
                states.append((d, h, st))
        hf_scr[rows[0], :] = jnp.concatenate(outs[0], axis=1)
        hb_scr[rows[1], :] = jnp.concatenate(outs[1], axis=1)
        for d, h, (c_new, n_new, m_new) in states:
            c_ref[d, h], n_ref[d, h], m_ref[d, h] = c_new, n_new, m_new
        return carry

    if single:
        body(0, 0)
    else:
        lax.fori_loop(0, n_chunks, body, 0)
    for h in range(hh):
        cols = pl.ds(h * dv, dv)
        hs = hf_scr[:, cols] + hb_scr[:, cols]
        hs_ref[:, cols] = (hs * lax.rsqrt(jnp.mean(hs * hs, axis=-1, keepdims=True)
                                          + NORM_EPS)).astype(BF16)


def mlstm_scan(qkv, gcol, grow, init, row_off, n_seq, seq_len):
    hh = MLSTM_HEADS
    hv = qkv.shape[1] // 2
    dv = hv // hh
    dk = dv // 2
    l = MLSTM_CHUNK
    nc = seq_len // l
    ob = row_off // seq_len
    kern = functools.partial(_mlstm_scan_kernel, n_chunks=nc, zero_init=init is None)
    st = lambda *tail: pl.BlockSpec((None, 2, hh) + tail, lambda s: (s, 0, 0) + (0,) * len(tail))
    states = [st(dk, dv), st(1, dk), st(1, 1)]
    return pl.pallas_call(
        kern,
        grid=(n_seq,),
        in_specs=[pl.BlockSpec((seq_len, hh * dk), lambda s: (ob + s, 0)),
                  pl.BlockSpec((seq_len, hh * dk), lambda s: (ob + s, 1)),
                  pl.BlockSpec((seq_len, hv), lambda s: (ob + s, 1)),
                  pl.BlockSpec((seq_len, LANES), lambda s: (ob + s, 0)),
                  pl.BlockSpec((nc, 4 * hh, l), lambda s: (ob + s, 0, 0))]
                 + ([] if init is None else states),
        out_specs=[pl.BlockSpec((seq_len, hv), lambda s: (s, 0))] + states,
        out_shape=[jax.ShapeDtypeStruct((n_seq * seq_len, hv), BF16),
                   jax.ShapeDtypeStruct((n_seq, 2, hh, dk, dv), F32),
                   jax.ShapeDtypeStruct((n_seq, 2, hh, 1, dk), F32),
                   jax.ShapeDtypeStruct((n_seq, 2, hh, 1, 1), F32)],
        scratch_shapes=[pltpu.VMEM((seq_len, hv), F32), pltpu.VMEM((seq_len, hv), F32)],
        compiler_params=_params("arbitrary"),
        name="mlstm_scan",
    )(qkv, qkv, qkv, gcol, grow, *(() if init is None else init))


def _mla_proj_kernel(x_ref, g_ref, sh_ref, sc_ref, win_ref, qg_ref, kvg_ref, wqb_ref, cos_ref,
                     sin_ref, q_ref, ckv_ref, kpe_ref, *, q_lora, kv_lora, rope, n_heads):
    h = _modulated(x_ref[...], g_ref[...], sh_ref[...], sc_ref[...])
    proj = _bdot_nt(h, win_ref[...])
    q_lat = proj[:, :q_lora]
    ckv_ref[...] = _rms(proj[:, q_lora:q_lora + kv_lora], kvg_ref[...])
    cos, sin = cos_ref[...], sin_ref[...]
    kpe = proj[:, q_lora + kv_lora:q_lora + kv_lora + rope]
    kpe_rot = proj[:, q_lora + kv_lora + rope:q_lora + kv_lora + 2 * rope]
    kpe_ref[...] = kpe * cos[:, :rope] + kpe_rot * sin[:, :rope]
    q = _bdot_nt(_rms(q_lat, qg_ref[...]), wqb_ref[...])
    n, nr = n_heads * LANES, n_heads * rope
    reps = nr // LANES
    roped = (q[:, n:n + nr] * jnp.concatenate([cos] * reps, axis=1)
             + q[:, n + nr:] * jnp.concatenate([sin] * reps, axis=1))
    pad = jnp.zeros((q.shape[0], LANES - rope), F32)
    pieces = []
    for hd in range(n_heads):
        pieces += [q[:, hd * LANES:(hd + 1) * LANES], roped[:, hd * rope:(hd + 1) * rope], pad]
    q_ref[...] = jnp.concatenate(pieces, axis=1).astype(BF16)


def _rot_rows(wt):
    n, k = wt.shape
    quarter = MLA_ROPE // 4
    w4 = wt.reshape(n // (2 * quarter), 2, quarter, k)
    return jnp.concatenate([-w4[:, 1:2], w4[:, 0:1]], axis=1).reshape(n, k)


def _rope_tables(n_ctx, dec_batch, dec_seq):
    quarter = MLA_ROPE // 4
    freq = np.power(np.float32(ROPE_BASE), -np.arange(quarter, dtype=np.float32) / np.float32(quarter))
    pos = np.arange(dec_seq)
    ang_r = (pos // GRID_W).astype(np.float32)[:, None] * freq[None, :]
    ang_c = (pos % GRID_W).astype(np.float32)[:, None] * freq[None, :]
    ang = np.concatenate([ang_r, ang_r, ang_c, ang_c], axis=1).astype(np.float32)
    cos = np.concatenate([np.ones((n_ctx, MLA_ROPE), np.float32)] + [np.cos(ang)] * dec_batch, axis=0)
    sin = np.concatenate([np.zeros((n_ctx, MLA_ROPE), np.float32)] + [np.sin(ang)] * dec_batch, axis=0)
    reps = LANES // MLA_ROPE
    return (jnp.asarray(np.tile(cos, (1, reps)), F32), jnp.asarray(np.tile(sin, (1, reps)), F32))


def mla_proj(x, g, shift, scale, w_in, q_g, kv_g, w_qb, cos, sin, n_ctx, dec_seq, tm=512):
    t, d = x.shape
    hh, nope, rope = MLA_HEADS, MLA_NOPE, MLA_ROPE
    q_lora = q_g.shape[1]
    kv_lora = kv_g.shape[1]
    w_in_t = w_in.T
    w_in_ext = jnp.concatenate([w_in_t, _rot_rows(w_in_t[q_lora + kv_lora:])], axis=0)
    w3 = w_qb.T.reshape(hh, nope + rope, q_lora)
    w_qn = w3[:, :nope].reshape(hh * nope, q_lora)
    w_qp = w3[:, nope:].reshape(hh * rope, q_lora)
    assert nope == LANES and LANES % rope == 0 and (hh * rope) % LANES == 0
    w_qb_ext = jnp.concatenate([w_qn, w_qp, _rot_rows(w_qp)], axis=0)
    cidx = _cvec_index(n_ctx, dec_seq, tm)
    mod_spec = pl.BlockSpec((None, 1, d), lambda i: (cidx(i), 0, 0))
    full = lambda a: pl.BlockSpec(a.shape, lambda i: (0,) * a.ndim)
    kern = functools.partial(_mla_proj_kernel, q_lora=q_lora, kv_lora=kv_lora, rope=rope, n_heads=hh)
    return pl.pallas_call(
        kern,
        grid=(t // tm,),
        in_specs=[pl.BlockSpec((tm, d), lambda i: (i, 0)), full(g), mod_spec, mod_spec,
                  full(w_in_ext), full(q_g), full(kv_g), full(w_qb_ext),
                  pl.BlockSpec((tm, LANES), lambda i: (i, 0)),
                  pl.BlockSpec((tm, LANES), lambda i: (i, 0))],
        out_specs=[pl.BlockSpec((tm, 2 * hh * LANES), lambda i: (i, 0)),
                   pl.BlockSpec((tm, kv_lora), lambda i: (i, 0)),
                   pl.BlockSpec((tm, rope), lambda i: (i, 0))],
        out_shape=[jax.ShapeDtypeStruct((t, 2 * hh * LANES), BF16),
                   jax.ShapeDtypeStruct((t, kv_lora), F32),
                   jax.ShapeDtypeStruct((t, rope), F32)],
        compiler_params=_params("arbitrary"),
        name="mla_proj",
    )(x, g, shift, scale, w_in_ext, q_g, kv_g, w_qb_ext, cos, sin)


def _mla_kv_kernel(ckv_ref, kp_ref, w_ref, k_ref, v_ref):
    kv = _bdot(ckv_ref[...], w_ref[...])
    n = v_ref.shape[1]
    kp = kp_ref[...]
    pieces = []
    for hd in range(n // LANES):
        pieces += [kv[:, hd * LANES:(hd + 1) * LANES].astype(BF16), kp]
    k_ref[...] = jnp.concatenate(pieces, axis=1)
    v_ref[...] = kv[:, n:].astype(BF16)


def mla_kv(ckv_all, kp_pad, w_kvb, tm=1024):
    r, kv_lora = ckv_all.shape
    hh, nope, vd = MLA_HEADS, MLA_NOPE, MLA_V
    w3 = w_kvb.reshape(kv_lora, hh, nope + vd)
    w_perm = jnp.concatenate([w3[:, :, :nope].reshape(kv_lora, hh * nope),
                              w3[:, :, nope:].reshape(kv_lora, hh * vd)], axis=1)
    return pl.pallas_call(
        _mla_kv_kernel,
        grid=(r // tm,),
        in_specs=[pl.BlockSpec((tm, kv_lora), lambda i: (i, 0)),
                  pl.BlockSpec((tm, LANES), lambda i: (i, 0)),
                  pl.BlockSpec(w_perm.shape, lambda i: (0, 0))],
        out_specs=[pl.BlockSpec((tm, 2 * hh * LANES), lambda i: (i, 0)),
                   pl.BlockSpec((tm, hh * vd), lambda i: (i, 0))],
        out_shape=[jax.ShapeDtypeStruct((r, 2 * hh * LANES), BF16),
                   jax.ShapeDtypeStruct((r, hh * vd), BF16)],
        compiler_params=_params("arbitrary"),
        name="mla_kv",
    )(ckv_all, kp_pad, w_perm)


def _attn_kernel(q_ref, k_ref, v_ref, x_ref, w_ref, gt_ref, xo_ref, o_scr, *, scale):
    c = scale * np.log2(np.e)
    for h in range(MLA_HEADS):
        hk = slice(2 * h * LANES, 2 * (h + 1) * LANES)
        s = _bdot_nt(q_ref[:, hk], k_ref[:, hk])
        e = jnp.exp2((s - jnp.max(s, axis=-1, keepdims=True)) * c)
        o = _bdot(e, v_ref[:, h * MLA_V:(h + 1) * MLA_V]) / jnp.sum(e, axis=-1, keepdims=True)
        o_scr[:, h * MLA_V:(h + 1) * MLA_V] = o.astype(BF16)
    xo_ref[...] = x_ref[...] + gt_ref[...] * jnp.dot(o_scr[...], w_ref[...],
                                                     preferred_element_type=F32)


def mla_attention(x, q, k, v, w_out, gate, q_row_off, k_row_off, n_seq, q_len, k_len, n_ctx, dec_seq):
    tq = min(ATTN_Q_BLOCK, q_len)
    qb = q_len // tq
    q0 = q_row_off // tq
    k0 = k_row_off // k_len
    t, d = x.shape
    dq, dv = q.shape[1], v.shape[1]
    cidx = _cvec_index(n_ctx, dec_seq, tq)
    rows = lambda s, j: q0 + s * qb + j
    kern = functools.partial(_attn_kernel, scale=(MLA_NOPE + MLA_ROPE) ** -0.5)
    return pl.pallas_call(
        kern,
        grid=(n_seq, qb),
        in_specs=[pl.BlockSpec((tq, dq), lambda s, j: (rows(s, j), 0)),
                  pl.BlockSpec((k_len, dq), lambda s, j: (k0 + s, 0)),
                  pl.BlockSpec((k_len, dv), lambda s, j: (k0 + s, 0)),
                  pl.BlockSpec((tq, d), lambda s, j: (rows(s, j), 0)),
                  pl.BlockSpec(w_out.shape, lambda s, j: (0, 0)),
                  pl.BlockSpec((None, 1, d), lambda s, j: (cidx(rows(s, j)), 0, 0))],
        out_specs=pl.BlockSpec((tq, d), lambda s, j: (rows(s, j), 0)),
        out_shape=jax.ShapeDtypeStruct((t, d), F32),
        scratch_shapes=[pltpu.VMEM((tq, dv), BF16)],
        input_output_aliases={3: 0},
        compiler_params=_params("arbitrary", "arbitrary"),
        name="mla_attention",
    )(q, k, v, x, w_out, gate)


def kernel(x_prompt, x_sample, state_mlstm_C, state_mlstm_n, state_mlstm_m, cache_mla_ckv,
           cache_mla_kpe, c, c_ctx, w_ada, b_ada, norm_mix, norm_ffn, norm_final, w_pool,
           pool_scale, w_mlstm_in, b_mlstm_gate, mlstm_head_g, w_mlstm_out, w_mla_in, mla_q_g,
           mla_kv_g, w_mla_qb, w_mla_kvb, w_mla_out, w_router, b_router, w_exp_gate, w_exp_up,
           w_exp_down):
    batch, seq, d = x_prompt.shape
    dec_batch, dec_seq, _ = x_sample.shape
    depth = w_ada.shape[0]
    n_ctx = batch * seq
    n_lat = dec_batch * dec_seq
    hh = MLSTM_HEADS
    past = cache_mla_ckv.shape[2]

    x = (x_prompt.reshape(n_ctx, d), x_sample.reshape(n_lat, d))

    n_cv = 1 + dec_batch
    cvecs = jnp.concatenate([c_ctx[None, :], c, jnp.zeros((SUBLANES - n_cv % SUBLANES, d), F32)], axis=0)
    mod = ada_mod_all(cvecs, w_ada, b_ada).reshape(depth, cvecs.shape[0], 6, 1, d)

    w_router_pad = jnp.pad(w_router, ((0, 0), (0, LANES - N_EXPERTS)))
    w_router_hi = w_router_pad.astype(BF16)
    w_router_lo = (w_router_pad - w_router_hi.astype(F32)).astype(BF16)
    w_router_split = jnp.concatenate([w_router_hi, w_router_lo], axis=1)
    b_router_col = b_router.reshape(N_EXPERTS, 1)
    g_final = norm_final.reshape(1, d)
    row = lambda a: a.reshape(1, -1)

    outs = {}
    for i in range(depth):
        kind, j = i % 3, i // 3
        m = [mod[i, :n_cv, k] for k in range(6)]
        g_mix = row(norm_mix[i])
        if kind == 0:
            x = pool_layer(x, g_mix, m[0], m[1], m[2], w_pool[j], row(pool_scale[j]),
                           n_ctx, seq, dec_seq)
        elif kind == 1:
            qkv, o_gate, gcol, grow = mlstm_proj(x, g_mix, m[0], m[1], w_mlstm_in, j,
                                                 b_mlstm_gate[j], n_ctx, dec_seq)
            t = n_ctx + n_lat
            l = MLSTM_CHUNK
            grow_c = grow.reshape(4 * hh, t // l, l).transpose(1, 0, 2)
            hs_c, c_new, n_new, m_new = mlstm_scan(qkv, gcol, grow_c, None, 0, batch, seq)
            init = (state_mlstm_C[:, j], state_mlstm_n[:, j][:, :, :, None, :],
                    state_mlstm_m[:, j][:, :, :, None, None])
            hs_l, _, _, _ = mlstm_scan(qkv, gcol, grow_c, init, n_ctx, dec_batch, dec_seq)
            outs["C"] = c_new[:, None]
            outs["n"] = n_new[:, None, :, :, 0, :]
            outs["m"] = m_new[:, None, :, :, 0, 0]
            x = mlstm_out(x, hs_c, hs_l, o_gate, row(mlstm_head_g[j]), w_mlstm_out[j], m[2],
                          n_ctx, dec_seq)
        else:
            cos, sin = _rope_tables(n_ctx, dec_batch, dec_seq)
            q_cat, ckv, kpe = mla_proj(x, g_mix, m[0], m[1], w_mla_in[j], row(mla_q_g[j]),
                                       row(mla_kv_g[j]), w_mla_qb[j], cos, sin, n_ctx, dec_seq)
            lat_parts_c, lat_parts_p = [], []
            for b in range(dec_batch):
                lo = n_ctx + b * dec_seq
                lat_parts_c += [cache_mla_ckv[b, j], ckv[lo:lo + dec_seq]]
                lat_parts_p += [cache_mla_kpe[b, j], kpe[lo:lo + dec_seq]]
            ckv_all = jnp.concatenate(lat_parts_c + [ckv[:n_ctx]], axis=0)
            kp_all = jnp.concatenate(lat_parts_p + [kpe[:n_ctx]], axis=0).astype(BF16)
            kp_pad = jnp.pad(kp_all, ((0, 0), (0, LANES - kp_all.shape[1])))
            k_cat, v = mla_kv(ckv_all, kp_pad, w_mla_kvb[j])
            k_lat = past + dec_seq
            w_out = w_mla_out[j].astype(BF16)
            x = mla_attention(x, q_cat, k_cat, v, w_out, m[2], 0, dec_batch * k_lat, batch, seq, seq,
                              n_ctx, dec_seq)
            x = mla_attention(x, q_cat, k_cat, v, w_out, m[2], n_ctx, 0, dec_batch, dec_seq, k_lat,
                              n_ctx, dec_seq)
            outs["ckv"] = ckv[:n_ctx].reshape(batch, 1, seq, -1)
            outs["kpe"] = kpe[:n_ctx].reshape(batch, 1, seq, -1)
        x = moe_layer(x, row(norm_ffn[i]), m[3], m[4], m[5], w_router_split, b_router_col,
                      w_exp_gate, w_exp_up, w_exp_down, i, g_final, n_ctx, dec_seq,
                      final_norm=(i == depth - 1), split_output=(i == depth - 1))

    y_prompt = x[0].reshape(batch, seq, d)
    y_sample = x[1].reshape(dec_batch, dec_seq, d)
    return (y_prompt, y_sample, outs["C"], outs["n"], outs["m"], outs["ckv"], outs["kpe"])
```

```python
import functools

import numpy as np
import jax
import jax.numpy as jnp
from jax import lax
from jax.experimental import pallas as pl
from jax.experimental.pallas import tpu as pltpu

F32 = jnp.float32
BF16 = jnp.bfloat16

NORM_EPS = 1e-6
GRID_W = 64
POOL_WINDOWS = (2, 4, 8, 16)
MLSTM_HEADS = 4
MLSTM_CHUNK = 256
MLA_HEADS = 8
MLA_NOPE = 128
MLA_ROPE = 64
MLA_V = 128
ROPE_BASE = 10000.0
N_EXPERTS = 16
N_EXPERT_GROUPS = 4
EXPERTS_PER_GROUP = N_EXPERTS // N_EXPERT_GROUPS

LANES = 128
SUBLANES = 8
VMEM_LIMIT = 56 * 1024 * 1024
MOE_VMEM_LIMIT = 60 * 1024 * 1024
POOL_TILE = 256
POOL_HALO = 8
ATTN_Q_BLOCK = 256
MOE_TILE = 512
MOE_SEG_ALIGN = 16
MOE_ROW_BLOCK = 160
MOE_TAIL_BLOCK = 64
MOE_LOAD_EXPERTS = 2


def _params(*sem, vmem_limit=VMEM_LIMIT):
    return pltpu.CompilerParams(dimension_semantics=sem, vmem_limit_bytes=vmem_limit)


def _rms(x, g):
    return x * lax.rsqrt(jnp.mean(x * x, axis=-1, keepdims=True) + NORM_EPS) * g


def _modulated(x, g, shift, scale):
    return _rms(x, g) * (1.0 + scale) + shift


def _silu(x):
    return x * jax.nn.sigmoid(x)


def _bdot(a, b):
    return jnp.dot(a.astype(BF16), b.astype(BF16), preferred_element_type=F32)


def _bdot_nt(a, b):
    return lax.dot_general(a.astype(BF16), b.astype(BF16), (((1,), (1,)), ((), ())),
                           preferred_element_type=F32)


def _bdot_tn(a, b):
    return lax.dot_general(a.astype(BF16), b.astype(BF16), (((0,), (0,)), ((), ())),
                           preferred_element_type=F32)


def _cvec_index(n_ctx, dec_seq, tm):
    def idx(i):
        r = i * tm
        return jnp.where(r < n_ctx, 0, (r - n_ctx) // dec_seq + 1)
    return idx


def _ada_kernel(c_ref, w_ref, b_ref, o_ref):
    o_ref[...] = _bdot(_silu(c_ref[...]), w_ref[...]) + b_ref[...]


def ada_mod_all(cvecs, w_ada, b_ada, tn=3072):
    depth, d, n6 = w_ada.shape
    rows = cvecs.shape[0]
    return pl.pallas_call(
        _ada_kernel,
        grid=(depth, n6 // tn),
        in_specs=[pl.BlockSpec((rows, d), lambda l, n: (0, 0)),
                  pl.BlockSpec((None, d, tn), lambda l, n: (l, 0, n)),
                  pl.BlockSpec((None, 1, tn), lambda l, n: (l, 0, n))],
        out_specs=pl.BlockSpec((None, rows, tn), lambda l, n: (l, 0, n)),
        out_shape=jax.ShapeDtypeStruct((depth, rows, n6), F32),
        compiler_params=_params("arbitrary", "arbitrary"),
        name="ada_mod",
    )(cvecs, w_ada, b_ada.reshape(depth, 1, n6))


def _pool_kernel(*refs, n_ctx_tiles, ctx_seq_tiles, lat_seq_tiles, split_input):
    n_x = 6 if split_input else 3
    x_refs = refs[:n_x]
    g_ref, sh_ref, sc_ref, gt_ref, wp_ref, ps_ref, o_ref, buf_ref, *lvl_refs = refs[n_x:]
    i = pl.program_id(0)
    is_ctx = i < n_ctx_tiles
    j = jnp.where(is_ctx, i % ctx_seq_tiles, (i - n_ctx_tiles) % lat_seq_tiles)
    nt = jnp.where(is_ctx, ctx_seq_tiles, lat_seq_tiles)
    g, sh, sc = g_ref[...], sh_ref[...], sc_ref[...]
    tp, hl = POOL_TILE, POOL_HALO
    gw = o_ref.shape[1] // len(POOL_WINDOWS)

    def fill(xc_ref, xp_ref, xn_ref):
        buf_ref[pl.ds(0, hl), :] = jnp.where(j == 0, 0.0, _modulated(xp_ref[...], g, sh, sc))
        buf_ref[pl.ds(hl, tp), :] = _modulated(xc_ref[...], g, sh, sc)
        buf_ref[pl.ds(hl + tp, hl), :] = jnp.where(j == nt - 1, 0.0,
                                                    _modulated(xn_ref[...], g, sh, sc))
        o_ref[...] = xc_ref[...]

    if split_input:
        pl.when(is_ctx)(lambda: fill(*x_refs[:3]))
        pl.when(jnp.logical_not(is_ctx))(lambda: fill(*x_refs[3:]))
    else:
        fill(*x_refs)

    rows = tp + 2 * hl
    buf_ref[pl.ds(rows, hl), :] = jnp.zeros((hl, buf_ref.shape[1]), F32)
    for ref in lvl_refs:
        ref[pl.ds(rows, hl), :] = jnp.zeros((hl, gw), F32)
    pos = j * tp + lax.broadcasted_iota(jnp.int32, (tp, 1), 0)
    seq_len = nt * tp
    for gi, w in enumerate(POOL_WINDOWS):
        cols = pl.ds(gi * gw, gw)
        src, src_cols = buf_ref, cols
        for lvl in range(1, w.bit_length() - 1):
            step = 1 << (lvl - 1)
            dst = lvl_refs[lvl - 1]
            dst[pl.ds(0, rows), :] = src[pl.ds(0, rows), src_cols] + src[pl.ds(step, rows), src_cols]
            src, src_cols = dst, pl.ds(0, gw)
        acc = src[pl.ds(hl - w // 2, tp), src_cols] + src[pl.ds(hl, tp), src_cols]
        cnt = jnp.minimum(pos + w // 2, seq_len) - jnp.maximum(pos - w // 2, 0)
        pooled = acc / cnt.astype(F32) - buf_ref[pl.ds(hl, tp), cols]
        y = _bdot(pooled, wp_ref[gi]) * ps_ref[:, cols]
        o_ref[:, cols] = o_ref[:, cols] + gt_ref[:, cols] * y


def _halo_specs(tp, hl, d, tile_off, n_rows):
    hb = tp // hl
    last_tile, last_hblk = n_rows // tp - 1, n_rows // hl - 1
    tile = lambda i: jnp.clip(i - tile_off, 0, last_tile)
    return [pl.BlockSpec((tp, d), lambda i: (tile(i), 0)),
            pl.BlockSpec((hl, d), lambda i: (jnp.clip(tile(i) * hb - 1, 0, last_hblk), 0)),
            pl.BlockSpec((hl, d), lambda i: (jnp.clip((tile(i) + 1) * hb, 0, last_hblk), 0))]


def pool_layer(xs, g, shift, scale, gate, w_pool, pool_scale, n_ctx, seq, dec_seq):
    split = isinstance(xs, tuple)
    tp, hl = POOL_TILE, POOL_HALO
    if split:
        d = xs[0].shape[1]
        t = xs[0].shape[0] + xs[1].shape[0]
        x_specs = (_halo_specs(tp, hl, d, 0, xs[0].shape[0])
                   + _halo_specs(tp, hl, d, n_ctx // tp, xs[1].shape[0]))
        x_args = (xs[0],) * 3 + (xs[1],) * 3
    else:
        t, d = xs.shape
        x_specs = _halo_specs(tp, hl, d, 0, t)
        x_args = (xs,) * 3
    cidx = _cvec_index(n_ctx, dec_seq, tp)
    mod_spec = pl.BlockSpec((None, 1, d), lambda i: (cidx(i), 0, 0))
    row_spec = pl.BlockSpec((1, d), lambda i: (0, 0))
    assert all(w & (w - 1) == 0 and w // 2 <= hl for w in POOL_WINDOWS)
    n_levels = max(POOL_WINDOWS).bit_length() - 2
    kern = functools.partial(_pool_kernel, n_ctx_tiles=n_ctx // tp, ctx_seq_tiles=seq // tp,
                             lat_seq_tiles=dec_seq // tp, split_input=split)
    return pl.pallas_call(
        kern,
        grid=(t // tp,),
        in_specs=x_specs + [row_spec, mod_spec, mod_spec, mod_spec,
                            pl.BlockSpec(w_pool.shape, lambda i: (0, 0, 0)), row_spec],
        out_specs=pl.BlockSpec((tp, d), lambda i: (i, 0)),
        out_shape=jax.ShapeDtypeStruct((t, d), F32),
        scratch_shapes=[pltpu.VMEM((tp + 3 * hl, d), F32)]
        + [pltpu.VMEM((tp + 3 * hl, d // len(POOL_WINDOWS)), F32)] * n_levels,
        compiler_params=_params("arbitrary"),
        name="pool_mixer",
    )(*x_args, g, shift, scale, gate, w_pool, pool_scale)


def _route(sel, scores):
    e, tm = sel.shape
    row = lax.broadcasted_iota(jnp.int32, (e, tm), 0)
    best = jnp.zeros((1, tm), jnp.int32)
    best_sc = None
    for gidx in range(N_EXPERT_GROUPS):
        r = [sel[gidx * EXPERTS_PER_GROUP + k:gidx * EXPERTS_PER_GROUP + k + 1, :]
             for k in range(EXPERTS_PER_GROUP)]
        top2 = None
        for a in range(EXPERTS_PER_GROUP):
            for b in range(a + 1, EXPERTS_PER_GROUP):
                s = r[a] + r[b]
                top2 = s if top2 is None else jnp.maximum(top2, s)
        if best_sc is None:
            best_sc = top2
        else:
            better = top2 > best_sc
            best = jnp.where(better, gidx, best)
            best_sc = jnp.where(better, top2, best_sc)
    neg = -jnp.inf
    masked = jnp.where(row // EXPERTS_PER_GROUP == best, sel, neg)
    m1 = jnp.max(masked, axis=0, keepdims=True)
    i1 = jnp.min(jnp.where(masked == m1, row, e), axis=0, keepdims=True)
    masked2 = jnp.where(row == i1, neg, masked)
    m2 = jnp.max(masked2, axis=0, keepdims=True)
    i2 = jnp.min(jnp.where(masked2 == m2, row, e), axis=0, keepdims=True)
    hot1 = row == i1
    hot2 = row == i2
    w1 = jnp.sum(jnp.where(hot1, scores, 0.0), axis=0, keepdims=True)
    w2 = jnp.sum(jnp.where(hot2, scores, 0.0), axis=0, keepdims=True)
    tot = w1 + w2
    return best, jnp.where(hot1, w1 / tot, 0.0) + jnp.where(hot2, w2 / tot, 0.0)


def _split_bf16(a, parts):
    out = []
    for _ in range(parts):
        p = a.astype(BF16)
        out.append(p)
        a = a - p.astype(F32)
    return out


def _pad_rows(a, rows):
    return jnp.concatenate([a, jnp.zeros((rows - a.shape[0], a.shape[1]), a.dtype)], axis=0)


def _moe_kernel(x_ref, g_ref, sh_ref, sc_ref, gt_ref, wr_ref, br_ref, wgf_ref, wuf_ref, wdf_ref,
                gf_ref, *rest, final_norm, n_ctx_tiles, n_experts):
    out_refs, (wgu_ref, wd_ref), scratch = rest[:-5], rest[-5:-3], rest[-3:]
    i = pl.program_id(0)
    per_step, f = wgf_ref.shape[0], wgf_ref.shape[2]
    n_load = n_experts // per_step
    for s in range(n_load):
        @pl.when(i == s)
        def _(s=s):
            for j in range(per_step):
                gi, k = divmod(s * per_step + j, EXPERTS_PER_GROUP)
                wgu_ref[gi, :, pl.ds(2 * k * f, f)] = wgf_ref[j].astype(BF16)
                wgu_ref[gi, :, pl.ds((2 * k + 1) * f, f)] = wuf_ref[j].astype(BF16)
                wd_ref[gi, pl.ds(k * f, f), :] = wdf_ref[j].astype(BF16)

    @pl.when(i >= n_load)
    def _():
        _moe_tile(x_ref, g_ref, sh_ref, sc_ref, gt_ref, wr_ref, br_ref, wgu_ref, wd_ref,
                  gf_ref, out_refs, scratch, i - n_load, final_norm, n_ctx_tiles)


def _moe_tile(x_ref, g_ref, sh_ref, sc_ref, gt_ref, wr_ref, br_ref, wgu_ref, wd_ref,
              gf_ref, out_refs, scratch, tile, final_norm, n_ctx_tiles):
    hp_scr, cw_scr, yp_scr = scratch
    tr, d = x_ref.shape
    trp = hp_scr.shape[0]
    ng, eg = N_EXPERT_GROUPS, EXPERTS_PER_GROUP
    x = x_ref[...]
    h = _modulated(x, g_ref[...], sh_ref[...], sc_ref[...])
    hb = h.astype(BF16)

    h_lo = (h - hb.astype(F32)).astype(BF16)
    wr = wr_ref[...]
    lg = jnp.dot(hb, wr, preferred_element_type=F32)
    logits = lg[:, :LANES] + lg[:, LANES:] + jnp.dot(h_lo, wr[:, :LANES], preferred_element_type=F32)
    scores = jax.nn.sigmoid(logits.T[:N_EXPERTS, :])
    best, comb_t = _route(scores + br_ref[...], scores)

    grp = lax.broadcasted_iota(jnp.int32, (SUBLANES, tr), 0)
    hot_t = (grp == best).astype(F32)
    cw_t = hot_t[0:1, :] * comb_t[0:eg, :]
    for gi in range(1, ng):
        cw_t = cw_t + hot_t[gi:gi + 1, :] * comb_t[gi * eg:(gi + 1) * eg, :]
    cw_c = _pad_rows(cw_t, LANES).T

    ia = lax.broadcasted_iota(jnp.int32, (tr, tr), 0)
    ib = lax.broadcasted_iota(jnp.int32, (tr, tr), 1)
    before = jnp.where(ia < ib, 1.0, 0.0).astype(BF16)
    rank_t = jnp.dot(hot_t.astype(BF16), before, preferred_element_type=F32)

    starts, counts = [], []
    off = jnp.int32(0)
    for gi in range(ng):
        n = jnp.sum(hot_t[gi:gi + 1, :]).astype(jnp.int32)
        n = ((n + MOE_SEG_ALIGN - 1) // MOE_SEG_ALIGN) * MOE_SEG_ALIGN
        starts.append(off)
        counts.append(n)
        off = off + n

    pos_t = hot_t[0:1, :] * (rank_t[0:1, :] + starts[0].astype(F32))
    for gi in range(1, ng):
        pos_t = pos_t + hot_t[gi:gi + 1, :] * (rank_t[gi:gi + 1, :] + starts[gi].astype(F32))
    pos_c = _pad_rows(pos_t, LANES).T[:, 0:1]
    used = tr + ng * MOE_SEG_ALIGN
    perm = jnp.where(lax.broadcasted_iota(jnp.int32, (used, tr), 0) == pos_t.astype(jnp.int32),
                     1.0, 0.0).astype(BF16)
    perm_t = jnp.where(lax.broadcasted_iota(jnp.int32, (tr, used), 1) == pos_c.astype(jnp.int32),
                       1.0, 0.0).astype(BF16)

    hp_scr[pl.ds(0, used), :] = jnp.dot(perm, hb, preferred_element_type=F32).astype(BF16)
    hp_scr[pl.ds(used, trp - used), :] = jnp.zeros((trp - used, d), BF16)
    cw_pair = jnp.dot(perm, jnp.concatenate(_split_bf16(cw_c, 2), axis=1), preferred_element_type=F32)
    cw_scr[pl.ds(0, used), :] = cw_pair[:, :LANES] + cw_pair[:, LANES:]
    cw_scr[pl.ds(used, trp - used), :] = jnp.zeros((trp - used, LANES), F32)
    yp_scr[...] = jnp.zeros_like(yp_scr)

    f = wd_ref.shape[1] // eg

    def expert_block(gi, row0, n_rows, seg_end=None):
        rows = pl.ds(pl.multiple_of(row0, MOE_SEG_ALIGN), n_rows)
        cwb = cw_scr[rows, :]
        gu = jnp.dot(hp_scr[rows, :], wgu_ref[gi], preferred_element_type=F32)
        hid = [(_silu(gu[:, 2 * k * f:(2 * k + 1) * f]) * gu[:, (2 * k + 1) * f:(2 * k + 2) * f]
                * cwb[:, k:k + 1]).astype(BF16) for k in range(eg)]
        y = jnp.dot(jnp.concatenate(hid, axis=1), wd_ref[gi], preferred_element_type=F32).astype(BF16)
        if seg_end is None:
            yp_scr[rows, :] = y
        else:
            inside = row0 + lax.broadcasted_iota(jnp.int32, (n_rows, 1), 0) < seg_end
            yp_scr[rows, :] = jnp.where(inside, y, yp_scr[rows, :])

    sb, tb = MOE_ROW_BLOCK, MOE_TAIL_BLOCK
    for gi in range(ng):
        expert_block(gi, starts[gi], sb)
    for gi in range(ng):
        def tail(b, carry, gi=gi):
            expert_block(gi, starts[gi] + sb + b * tb, tb, starts[gi] + counts[gi])
            return carry
        lax.fori_loop(0, (jnp.maximum(counts[gi] - sb, 0) + tb - 1) // tb, tail, 0)

    moe = jnp.dot(perm_t, yp_scr[pl.ds(0, used), :], preferred_element_type=F32)
    out = x + gt_ref[...] * moe
    if final_norm:
        out = _rms(out, gf_ref[...])
    if n_ctx_tiles is None:
        out_refs[0][...] = out
    else:
        is_ctx = tile < n_ctx_tiles

        @pl.when(is_ctx)
        def _():
            out_refs[0][...] = out

        @pl.when(jnp.logical_not(is_ctx))
        def _():
            out_refs[1][...] = out


def moe_layer(x, g, shift, scale, gate, w_router_split, b_router_col, wg, wu, wd, layer, g_final,
              n_ctx, dec_seq, final_norm, split_output=False, tm=MOE_TILE):
    t, d = x.shape
    _, n_e, _, f = wg.shape
    trp = tm + N_EXPERT_GROUPS * MOE_SEG_ALIGN + max(MOE_ROW_BLOCK, MOE_TAIL_BLOCK)
    per_step = MOE_LOAD_EXPERTS
    n_load = n_e // per_step
    tile = lambda i: jnp.maximum(i - n_load, 0)
    cidx = _cvec_index(n_ctx, dec_seq, tm)
    mod_spec = pl.BlockSpec((None, 1, d), lambda i: (cidx(tile(i)), 0, 0))
    row_spec = pl.BlockSpec((1, d), lambda i: (0, 0))
    expert = lambda i: jnp.minimum(i, n_load - 1)
    nct = n_ctx // tm
    if split_output:
        out_specs = [pl.BlockSpec((tm, d), lambda i: (jnp.minimum(tile(i), nct - 1), 0)),
                     pl.BlockSpec((tm, d), lambda i: (jnp.maximum(tile(i) - nct, 0), 0))]
        out_shape = [jax.ShapeDtypeStruct((n_ctx, d), F32), jax.ShapeDtypeStruct((t - n_ctx, d), F32)]
    else:
        out_specs = pl.BlockSpec((tm, d), lambda i: (tile(i), 0))
        out_shape = jax.ShapeDtypeStruct((t, d), F32)
    return pl.pallas_call(
        functools.partial(_moe_kernel, final_norm=final_norm,
                          n_ctx_tiles=nct if split_output else None, n_experts=n_e),
        grid=(n_load + t // tm,),
        in_specs=[pl.BlockSpec((tm, d), lambda i: (tile(i), 0)),
                  row_spec, mod_spec, mod_spec, mod_spec,
                  pl.BlockSpec(w_router_split.shape, lambda i: (0, 0)),
                  pl.BlockSpec(b_router_col.shape, lambda i: (0, 0)),
                  pl.BlockSpec((None, per_step, d, f), lambda i: (layer, expert(i), 0, 0)),
                  pl.BlockSpec((None, per_step, d, f), lambda i: (layer, expert(i), 0, 0)),
                  pl.BlockSpec((None, per_step, f, d), lambda i: (layer, expert(i), 0, 0)),
                  row_spec],
        out_specs=out_specs,
        out_shape=out_shape,
        scratch_shapes=[pltpu.VMEM((N_EXPERT_GROUPS, d, 2 * EXPERTS_PER_GROUP * f), BF16),
                        pltpu.VMEM((N_EXPERT_GROUPS, EXPERTS_PER_GROUP * f, d), BF16),
                        pltpu.VMEM((trp, d), BF16), pltpu.VMEM((trp, LANES), F32),
                        pltpu.VMEM((trp, d), BF16)],
        compiler_params=_params("arbitrary", vmem_limit=MOE_VMEM_LIMIT),
        name="moe",
    )(x, g, shift, scale, gate, w_router_split, b_router_col, wg, wu, wd, g_final)


def _per_stream(n_ctx_tiles, ctx_ref, lat_ref, fn):
    i = pl.program_id(0)

    @pl.when(i < n_ctx_tiles)
    def _():
        fn(ctx_ref[...])

    @pl.when(i >= n_ctx_tiles)
    def _():
        fn(lat_ref[...])


def _stream_specs(tm, k, n_ctx_tiles):
    return [pl.BlockSpec((tm, k), lambda i: (jnp.minimum(i, n_ctx_tiles - 1), 0)),
            pl.BlockSpec((tm, k), lambda i: (jnp.maximum(i - n_ctx_tiles, 0), 0))]


def _mlstm_out_kernel(x_ref, hc_ref, hl_ref, og_ref, hg_ref, w_ref, gt_ref, o_ref, *, n_ctx_tiles):
    def run(hs):
        a = jax.nn.sigmoid(og_ref[...].astype(F32)) * (hs.astype(F32) * hg_ref[...])
        o_ref[...] = x_ref[...] + gt_ref[...] * _bdot(a, w_ref[...])
    _per_stream(n_ctx_tiles, hc_ref, hl_ref, run)


def mlstm_out(x, hs_ctx, hs_lat, o_gate, head_g, w, gate, n_ctx, dec_seq, tm=1024):
    t, d = x.shape
    k = hs_ctx.shape[1]
    cidx = _cvec_index(n_ctx, dec_seq, tm)
    nct = n_ctx // tm
    return pl.pallas_call(
        functools.partial(_mlstm_out_kernel, n_ctx_tiles=nct),
        grid=(t // tm,),
        in_specs=[pl.BlockSpec((tm, d), lambda i: (i, 0))] + _stream_specs(tm, k, nct) + [
            pl.BlockSpec((tm, k), lambda i: (i, 0)),
            pl.BlockSpec((1, k), lambda i: (0, 0)),
            pl.BlockSpec((k, d), lambda i: (0, 0)),
            pl.BlockSpec((None, 1, d), lambda i: (cidx(i), 0, 0))],
        out_specs=pl.BlockSpec((tm, d), lambda i: (i, 0)),
        out_shape=jax.ShapeDtypeStruct((t, d), F32),
        compiler_params=_params("arbitrary"),
        name="mlstm_out",
    )(x, hs_ctx, hs_lat, o_gate, head_g, w, gate)


def _mlstm_proj_kernel(x_ref, g_ref, sh_ref, sc_ref, wqk_ref, wv_ref, wo_ref, wgt_ref, bg_ref,
                       qs_ref, qkv_ref, o_ref, gc_ref, gr_ref):
    h = _modulated(x_ref[...], g_ref[...], sh_ref[...], sc_ref[...]).astype(BF16)
    nqk = wqk_ref.shape[0]
    qkv_ref[:, :nqk] = (_bdot_nt(h, wqk_ref[...]) * qs_ref[...]).astype(BF16)
    qkv_ref[:, nqk:] = _bdot_nt(h, wv_ref[...]).astype(BF16)
    o_ref[...] = _bdot_nt(h, wo_ref[...]).astype(BF16)
    gates = _bdot_nt(h, wgt_ref[...]) + bg_ref[...]
    gc_ref[...] = gates
    gr_ref[...] = gates.T[:gr_ref.shape[0], :]


def mlstm_proj(x, g, shift, scale, w_in_all, layer, b_gate, n_ctx, dec_seq, tm=1024):
    t, d = x.shape
    hh = MLSTM_HEADS
    hv = d
    hk = hv // 2
    ng = 4 * hh
    w_t_all = jnp.swapaxes(w_in_all, 1, 2)
    w_g_pad = jnp.pad(w_t_all[layer, 2 * hk + 2 * hv:, :], ((0, LANES - ng), (0, 0)))
    b_pad = jnp.pad(b_gate.reshape(1, ng), ((0, 0), (0, LANES - ng)))
    dk = hk // hh
    q_scale = jnp.concatenate([jnp.full((1, hk), dk ** -0.5, F32), jnp.ones((1, hk), F32)], axis=1)
    cidx = _cvec_index(n_ctx, dec_seq, tm)
    mod_spec = pl.BlockSpec((None, 1, d), lambda i: (cidx(i), 0, 0))
    full = lambda a: pl.BlockSpec(a.shape, lambda i: (0,) * a.ndim)
    assert 2 * hk == hv
    w_col = lambda n: pl.BlockSpec((None, hv, d), lambda i: (layer, n, 0))
    return pl.pallas_call(
        _mlstm_proj_kernel,
        grid=(t // tm,),
        in_specs=[pl.BlockSpec((tm, d), lambda i: (i, 0)), full(g), mod_spec, mod_spec,
                  w_col(0), w_col(1), w_col(2), full(w_g_pad), full(b_pad), full(q_scale)],
        out_specs=[pl.BlockSpec((tm, 2 * hk + hv), lambda i: (i, 0)),
                   pl.BlockSpec((tm, hv), lambda i: (i, 0)),
                   pl.BlockSpec((tm, LANES), lambda i: (i, 0)),
                   pl.BlockSpec((ng, tm), lambda i: (0, i))],
        out_shape=[jax.ShapeDtypeStruct((t, 2 * hk + hv), BF16),
                   jax.ShapeDtypeStruct((t, hv), BF16),
                   jax.ShapeDtypeStruct((t, LANES), F32),
                   jax.ShapeDtypeStruct((ng, t), F32)],
        compiler_params=_params("arbitrary"),
        name="mlstm_proj",
    )(x, g, shift, scale, w_t_all, w_t_all, w_t_all, w_g_pad, b_pad, q_scale)


def _log_sigmoid(x):
    return jnp.minimum(x, 0.0) - jnp.log(1.0 + jnp.exp(-jnp.abs(x)))


def _gate_cumsums(gc, gr, causal_bf, feeds_bf):
    bc = br = None
    for part in _split_bf16(_log_sigmoid(gc), 3):
        t = jnp.dot(causal_bf, part, preferred_element_type=F32)
        bc = t if bc is None else bc + t
    for part in _split_bf16(_log_sigmoid(gr), 3):
        t = jnp.dot(part, feeds_bf, preferred_element_type=F32)
        br = t if br is None else br + t
    return bc, br


def _mlstm_chunk(q, k, v, i_col, b_col, i_row, b_row, c_st, n_st, m_st, causal, rev):
    l = q.shape[0]
    zero_state = c_st is None
    if zero_state:
        m_st = 0.0
    g_row = i_row - b_row
    log_w = jnp.where(causal, g_row, -jnp.inf)
    c_t = jnp.maximum(m_st, jnp.max(log_w, axis=1, keepdims=True))
    dw = jnp.exp(log_w - c_t)
    a = _bdot_nt(q, k) * dw
    a_hi, a_lo = _split_bf16(a, 2)
    ones = jnp.ones((l, LANES), BF16)
    den = (jnp.dot(a_hi, ones, preferred_element_type=F32)
           + jnp.dot(a_lo, ones, preferred_element_type=F32))[:, 0:1]
    if not zero_state:
        sw = jnp.exp(m_st - c_t)
        n_hi, n_lo = _split_bf16(jnp.broadcast_to(n_st, (LANES, n_st.shape[1])), 2)
        den = den + sw * (_bdot_nt(q, n_hi) + _bdot_nt(q, n_lo))[:, 0:1]
    inv = 1.0 / jnp.maximum(jnp.abs(den), jnp.exp(-(b_col + c_t)))
    h = inv * jnp.dot(a_hi, v, preferred_element_type=F32)
    if not zero_state:
        h = h + (sw * inv) * _bdot(q, c_st)
    b_last = b_col[0:1, :] if rev else b_col[l - 1:l, :]
    log_k = b_last - b_col + i_col
    m_new = jnp.maximum(b_last + m_st, jnp.max(log_k, axis=0, keepdims=True))
    kw = jnp.exp(log_k - m_new)
    kwk = kw * k.astype(F32)
    c_new = _bdot_tn(kwk, v)
    n_new = jnp.sum(kwk, axis=0, keepdims=True)
    if not zero_state:
        decay = jnp.exp(b_last + m_st - m_new)
        c_new = decay * c_st + c_new
        n_new = decay * n_st + n_new
    return h, c_new, n_new, m_new


def _mlstm_scan_kernel(*refs, n_chunks, zero_init):
    single = zero_init and n_chunks == 1
    if zero_init:
        q_ref, k_ref, v_ref, gc_ref, gr_ref, hs_ref, c_ref, n_ref, m_ref, hf_scr, hb_scr = refs
        if not single:
            c_ref[...] = jnp.zeros_like(c_ref)
            n_ref[...] = jnp.zeros_like(n_ref)
            m_ref[...] = jnp.zeros_like(m_ref)
    else:
        (q_ref, k_ref, v_ref, gc_ref, gr_ref, c0_ref, n0_ref, m0_ref,
         hs_ref, c_ref, n_ref, m_ref, hf_scr, hb_scr) = refs
        c_ref[...] = c0_ref[...]
        n_ref[...] = n0_ref[...]
        m_ref[...] = m0_ref[...]
    l = MLSTM_CHUNK
    hh = MLSTM_HEADS
    dk = q_ref.shape[1] // hh
    dv = v_ref.shape[1] // hh

    tt = lax.broadcasted_iota(jnp.int32, (l, l), 0)
    ss = lax.broadcasted_iota(jnp.int32, (l, l), 1)
    masks = (ss <= tt, ss >= tt)
    masks_bf = tuple(jnp.where(m, 1.0, 0.0).astype(BF16) for m in masks)

    def body(c, carry):
        cr = n_chunks - 1 - c
        start = (lambda j: j * l) if single else (lambda j: pl.multiple_of(j * l, l))
        rows = (pl.ds(start(c), l), pl.ds(start(cr), l))
        grs = (gr_ref[c], gr_ref[cr])
        outs, states = ([], []), []
        for d in range(2):
            q, k, v, gc, gr = q_ref[rows[d], :], k_ref[rows[d], :], v_ref[rows[d], :], gc_ref[rows[d], :], grs[d]
            bc, br = _gate_cumsums(gc, gr, masks_bf[d], masks_bf[1 - d])
            for h in range(hh):
                ci, cf = 2 * d * hh + h, (2 * d + 1) * hh + h
                state = (None,) * 3 if single else (c_ref[d, h], n_ref[d, h], m_ref[d, h])
                o, *st = _mlstm_chunk(q[:, h * dk:(h + 1) * dk], k[:, h * dk:(h + 1) * dk],
                                      v[:, h * dv:(h + 1) * dv], gc[:, ci:ci + 1], bc[:, cf:cf + 1],
                                      gr[ci:ci + 1, :], br[cf:cf + 1, :], *state, masks[d], d == 1)
                outs[d].append(o)
                states.append((d, h, st))
        hf_scr[rows[0], :] = jnp.concatenate(outs[0], axis=1)
        hb_scr[rows[1], :] = jnp.concatenate(outs[1], axis=1)
        for d, h, (c_new, n_new, m_new) in states:
            c_ref[d, h], n_ref[d, h], m_ref[d, h] = c_new, n_new, m_new
        return carry

    if single:
        body(0, 0)
    else:
        lax.fori_loop(0, n_chunks, body, 0)
    for h in range(hh):
        cols = pl.ds(h * dv, dv)
        hs = hf_scr[:, cols] + hb_scr[:, cols]
        hs_ref[:, cols] = (hs * lax.rsqrt(jnp.mean(hs * hs, axis=-1, keepdims=True)
                                          + NORM_EPS)).astype(BF16)


def mlstm_scan(qkv, gcol, grow, init, row_off, n_seq, seq_len):
    hh = MLSTM_HEADS
    hv = qkv.shape[1] // 2
    dv = hv // hh
    dk = dv // 2
    l = MLSTM_CHUNK
    nc = seq_len // l
    ob = row_off // seq_len
    kern = functools.partial(_mlstm_scan_kernel, n_chunks=nc, zero_init=init is None)
    st = lambda *tail: pl.BlockSpec((None, 2, hh) + tail, lambda s: (s, 0, 0) + (0,) * len(tail))
    states = [st(dk, dv), st(1, dk), st(1, 1)]
    return pl.pallas_call(
        kern,
        grid=(n_seq,),
        in_specs=[pl.BlockSpec((seq_len, hh * dk), lambda s: (ob + s, 0)),
                  pl.BlockSpec((seq_len, hh * dk), lambda s: (ob + s, 1)),
                  pl.BlockSpec((seq_len, hv), lambda s: (ob + s, 1)),
                  pl.BlockSpec((seq_len, LANES), lambda s: (ob + s, 0)),
                  pl.BlockSpec((nc, 4 * hh, l), lambda s: (ob + s, 0, 0))]
                 + ([] if init is None else states),
        out_specs=[pl.BlockSpec((seq_len, hv), lambda s: (s, 0))] + states,
        out_shape=[jax.ShapeDtypeStruct((n_seq * seq_len, hv), BF16),
                   jax.ShapeDtypeStruct((n_seq, 2, hh, dk, dv), F32),
                   jax.ShapeDtypeStruct((n_seq, 2, hh, 1, dk), F32),
                   jax.ShapeDtypeStruct((n_seq, 2, hh, 1, 1), F32)],
        scratch_shapes=[pltpu.VMEM((seq_len, hv), F32), pltpu.VMEM((seq_len, hv), F32)],
        compiler_params=_params("arbitrary"),
        name="mlstm_scan",
    )(qkv, qkv, qkv, gcol, grow, *(() if init is None else init))


def _mla_proj_kernel(x_ref, g_ref, sh_ref, sc_ref, win_ref, qg_ref, kvg_ref, wqb_ref, cos_ref,
                     sin_ref, q_ref, ckv_ref, kpe_ref, *, q_lora, kv_lora, rope, n_heads):
    h = _modulated(x_ref[...], g_ref[...], sh_ref[...], sc_ref[...])
    proj = _bdot_nt(h, win_ref[...])
    q_lat = proj[:, :q_lora]
    ckv_ref[...] = _rms(proj[:, q_lora:q_lora + kv_lora], kvg_ref[...])
    cos, sin = cos_ref[...], sin_ref[...]
    kpe = proj[:, q_lora + kv_lora:q_lora + kv_lora + rope]
    kpe_rot = proj[:, q_lora + kv_lora + rope:q_lora + kv_lora + 2 * rope]
    kpe_ref[...] = kpe * cos[:, :rope] + kpe_rot * sin[:, :rope]
    q = _bdot_nt(_rms(q_lat, qg_ref[...]), wqb_ref[...])
    n, nr = n_heads * LANES, n_heads * rope
    reps = nr // LANES
    roped = (q[:, n:n + nr] * jnp.concatenate([cos] * reps, axis=1)
             + q[:, n + nr:] * jnp.concatenate([sin] * reps, axis=1))
    pad = jnp.zeros((q.shape[0], LANES - rope), F32)
    pieces = []
    for hd in range(n_heads):
        pieces += [q[:, hd * LANES:(hd + 1) * LANES], roped[:, hd * rope:(hd + 1) * rope], pad]
    q_ref[...] = jnp.concatenate(pieces, axis=1).astype(BF16)


def _rot_rows(wt):
    n, k = wt.shape
    quarter = MLA_ROPE // 4
    w4 = wt.reshape(n // (2 * quarter), 2, quarter, k)
    return jnp.concatenate([-w4[:, 1:2], w4[:, 0:1]], axis=1).reshape(n, k)


def _rope_tables(n_ctx, dec_batch, dec_seq):
    quarter = MLA_ROPE // 4
    freq = np.power(np.float32(ROPE_BASE), -np.arange(quarter, dtype=np.float32) / np.float32(quarter))
    pos = np.arange(dec_seq)
    ang_r = (pos // GRID_W).astype(np.float32)[:, None] * freq[None, :]
    ang_c = (pos % GRID_W).astype(np.float32)[:, None] * freq[None, :]
    ang = np.concatenate([ang_r, ang_r, ang_c, ang_c], axis=1).astype(np.float32)
    cos = np.concatenate([np.ones((n_ctx, MLA_ROPE), np.float32)] + [np.cos(ang)] * dec_batch, axis=0)
    sin = np.concatenate([np.zeros((n_ctx, MLA_ROPE), np.float32)] + [np.sin(ang)] * dec_batch, axis=0)
    reps = LANES // MLA_ROPE
    return (jnp.asarray(np.tile(cos, (1, reps)), F32), jnp.asarray(np.tile(sin, (1, reps)), F32))


def mla_proj(x, g, shift, scale, w_in, q_g, kv_g, w_qb, cos, sin, n_ctx, dec_seq, tm=1024):
    t, d = x.shape
    hh, nope, rope = MLA_HEADS, MLA_NOPE, MLA_ROPE
    q_lora = q_g.shape[1]
    kv_lora = kv_g.shape[1]
    w_in_t = w_in.T
    w_in_ext = jnp.concatenate([w_in_t, _rot_rows(w_in_t[q_lora + kv_lora:])], axis=0)
    w3 = w_qb.T.reshape(hh, nope + rope, q_lora)
    w_qn = w3[:, :nope].reshape(hh * nope, q_lora)
    w_qp = w3[:, nope:].reshape(hh * rope, q_lora)
    assert nope == LANES and LANES % rope == 0 and (hh * rope) % LANES == 0
    w_qb_ext = jnp.concatenate([w_qn, w_qp, _rot_rows(w_qp)], axis=0)
    cidx = _cvec_index(n_ctx, dec_seq, tm)
    mod_spec = pl.BlockSpec((None, 1, d), lambda i: (cidx(i), 0, 0))
    full = lambda a: pl.BlockSpec(a.shape, lambda i: (0,) * a.ndim)
    kern = functools.partial(_mla_proj_kernel, q_lora=q_lora, kv_lora=kv_lora, rope=rope, n_heads=hh)
    return pl.pallas_call(
        kern,
        grid=(t // tm,),
        in_specs=[pl.BlockSpec((tm, d), lambda i: (i, 0)), full(g), mod_spec, mod_spec,
                  full(w_in_ext), full(q_g), full(kv_g), full(w_qb_ext),
                  pl.BlockSpec((tm, LANES), lambda i: (i, 0)),
                  pl.BlockSpec((tm, LANES), lambda i: (i, 0))],
        out_specs=[pl.BlockSpec((tm, 2 * hh * LANES), lambda i: (i, 0)),
                   pl.BlockSpec((tm, kv_lora), lambda i: (i, 0)),
                   pl.BlockSpec((tm, rope), lambda i: (i, 0))],
        out_shape=[jax.ShapeDtypeStruct((t, 2 * hh * LANES), BF16),
                   jax.ShapeDtypeStruct((t, kv_lora), F32),
                   jax.ShapeDtypeStruct((t, rope), F32)],
        compiler_params=_params("arbitrary"),
        name="mla_proj",
    )(x, g, shift, scale, w_in_ext, q_g, kv_g, w_qb_ext, cos, sin)


def _mla_kv_kernel(ckv_ref, kp_ref, w_ref, k_ref, v_ref):
    kv = _bdot(ckv_ref[...], w_ref[...])
    n = v_ref.shape[1]
    kp = kp_ref[...]
    pieces = []
    for hd in range(n // LANES):
        pieces += [kv[:, hd * LANES:(hd + 1) * LANES].astype(BF16), kp]
    k_ref[...] = jnp.concatenate(pieces, axis=1)
    v_ref[...] = kv[:, n:].astype(BF16)


def mla_kv(ckv_all, kp_pad, w_kvb, tm=1024):
    r, kv_lora = ckv_all.shape
    hh, nope, vd = MLA_HEADS, MLA_NOPE, MLA_V
    w3 = w_kvb.reshape(kv_lora, hh, nope + vd)
    w_perm = jnp.concatenate([w3[:, :, :nope].reshape(kv_lora, hh * nope),
                              w3[:, :, nope:].reshape(kv_lora, hh * vd)], axis=1)
    return pl.pallas_call(
        _mla_kv_kernel,
        grid=(r // tm,),
        in_specs=[pl.BlockSpec((tm, kv_lora), lambda i: (i, 0)),
                  pl.BlockSpec((tm, LANES), lambda i: (i, 0)),
                  pl.BlockSpec(w_perm.shape, lambda i: (0, 0))],
        out_specs=[pl.BlockSpec((tm, 2 * hh * LANES), lambda i: (i, 0)),
                   pl.BlockSpec((tm, hh * vd), lambda i: (i, 0))],
        out_shape=[jax.ShapeDtypeStruct((r, 2 * hh * LANES), BF16),
                   jax.ShapeDtypeStruct((r, hh * vd), BF16)],
        compiler_params=_params("arbitrary"),
        name="mla_kv",
    )(ckv_all, kp_pad, w_perm)


def _attn_kernel(q_ref, k_ref, v_ref, x_ref, w_ref, gt_ref, xo_ref, o_scr, *, scale):
    c = scale * np.log2(np.e)
    for h in range(MLA_HEADS):
        hk = slice(2 * h * LANES, 2 * (h + 1) * LANES)
        s = _bdot_nt(q_ref[:, hk], k_ref[:, hk])
        e = jnp.exp2((s - jnp.max(s, axis=-1, keepdims=True)) * c)
        o = _bdot(e, v_ref[:, h * MLA_V:(h + 1) * MLA_V]) / jnp.sum(e, axis=-1, keepdims=True)
        o_scr[:, h * MLA_V:(h + 1) * MLA_V] = o.astype(BF16)
    xo_ref[...] = x_ref[...] + gt_ref[...] * jnp.dot(o_scr[...], w_ref[...],
                                                     preferred_element_type=F32)


def mla_attention(x, q, k, v, w_out, gate, q_row_off, k_row_off, n_seq, q_len, k_len, n_ctx, dec_seq):
    tq = min(ATTN_Q_BLOCK, q_len)
    qb = q_len // tq
    q0 = q_row_off // tq
    k0 = k_row_off // k_len
    t, d = x.shape
    dq, dv = q.shape[1], v.shape[1]
    cidx = _cvec_index(n_ctx, dec_seq, tq)
    rows = lambda s, j: q0 + s * qb + j
    kern = functools.partial(_attn_kernel, scale=(MLA_NOPE + MLA_ROPE) ** -0.5)
    return pl.pallas_call(
        kern,
        grid=(n_seq, qb),
        in_specs=[pl.BlockSpec((tq, dq), lambda s, j: (rows(s, j), 0)),
                  pl.BlockSpec((k_len, dq), lambda s, j: (k0 + s, 0)),
                  pl.BlockSpec((k_len, dv), lambda s, j: (k0 + s, 0)),
                  pl.BlockSpec((tq, d), lambda s, j: (rows(s, j), 0)),
                  pl.BlockSpec(w_out.shape, lambda s, j: (0, 0)),
                  pl.BlockSpec((None, 1, d), lambda s, j: (cidx(rows(s, j)), 0, 0))],
        out_specs=pl.BlockSpec((tq, d), lambda s, j: (rows(s, j), 0)),
        out_shape=jax.ShapeDtypeStruct((t, d), F32),
        scratch_shapes=[pltpu.VMEM((tq, dv), BF16)],
        input_output_aliases={3: 0},
        compiler_params=_params("arbitrary", "arbitrary"),
        name="mla_attention",
    )(q, k, v, x, w_out, gate)


def kernel(x_prompt, x_sample, state_mlstm_C, state_mlstm_n, state_mlstm_m, cache_mla_ckv,
           cache_mla_kpe, c, c_ctx, w_ada, b_ada, norm_mix, norm_ffn, norm_final, w_pool,
           pool_scale, w_mlstm_in, b_mlstm_gate, mlstm_head_g, w_mlstm_out, w_mla_in, mla_q_g,
           mla_kv_g, w_mla_qb, w_mla_kvb, w_mla_out, w_router, b_router, w_exp_gate, w_exp_up,
           w_exp_down):
    batch, seq, d = x_prompt.shape
    dec_batch, dec_seq, _ = x_sample.shape
    depth = w_ada.shape[0]
    n_ctx = batch * seq
    n_lat = dec_batch * dec_seq
    hh = MLSTM_HEADS
    past = cache_mla_ckv.shape[2]

    x = (x_prompt.reshape(n_ctx, d), x_sample.reshape(n_lat, d))

    n_cv = 1 + dec_batch
    cvecs = jnp.concatenate([c_ctx[None, :], c, jnp.zeros((SUBLANES - n_cv % SUBLANES, d), F32)], axis=0)
    mod = ada_mod_all(cvecs, w_ada, b_ada).reshape(depth, cvecs.shape[0], 6, 1, d)

    w_router_pad = jnp.pad(w_router, ((0, 0), (0, LANES - N_EXPERTS)))
    w_router_hi = w_router_pad.astype(BF16)
    w_router_lo = (w_router_pad - w_router_hi.astype(F32)).astype(BF16)
    w_router_split = jnp.concatenate([w_router_hi, w_router_lo], axis=1)
    b_router_col = b_router.reshape(N_EXPERTS, 1)
    g_final = norm_final.reshape(1, d)
    row = lambda a: a.reshape(1, -1)

    outs = {}
    for i in range(depth):
        kind, j = i % 3, i // 3
        m = [mod[i, :n_cv, k] for k in range(6)]
        g_mix = row(norm_mix[i])
        if kind == 0:
            x = pool_layer(x, g_mix, m[0], m[1], m[2], w_pool[j], row(pool_scale[j]),
                           n_ctx, seq, dec_seq)
        elif kind == 1:
            qkv, o_gate, gcol, grow = mlstm_proj(x, g_mix, m[0], m[1], w_mlstm_in, j,
                                                 b_mlstm_gate[j], n_ctx, dec_seq)
            t = n_ctx + n_lat
            l = MLSTM_CHUNK
            grow_c = grow.reshape(4 * hh, t // l, l).transpose(1, 0, 2)
            hs_c, c_new, n_new, m_new = mlstm_scan(qkv, gcol, grow_c, None, 0, batch, seq)
            init = (state_mlstm_C[:, j], state_mlstm_n[:, j][:, :, :, None, :],
                    state_mlstm_m[:, j][:, :, :, None, None])
            hs_l, _, _, _ = mlstm_scan(qkv, gcol, grow_c, init, n_ctx, dec_batch, dec_seq)
            outs["C"] = c_new[:, None]
            outs["n"] = n_new[:, None, :, :, 0, :]
            outs["m"] = m_new[:, None, :, :, 0, 0]
            x = mlstm_out(x, hs_c, hs_l, o_gate, row(mlstm_head_g[j]), w_mlstm_out[j], m[2],
                          n_ctx, dec_seq)
        else:
            cos, sin = _rope_tables(n_ctx, dec_batch, dec_seq)
            q_cat, ckv, kpe = mla_proj(x, g_mix, m[0], m[1], w_mla_in[j], row(mla_q_g[j]),
                                       row(mla_kv_g[j]), w_mla_qb[j], cos, sin, n_ctx, dec_seq)
            lat_parts_c, lat_parts_p = [], []
            for b in range(dec_batch):
                lo = n_ctx + b * dec_seq
                lat_parts_c += [cache_mla_ckv[b, j], ckv[lo:lo + dec_seq]]
                lat_parts_p += [cache_mla_kpe[b, j], kpe[lo:lo + dec_seq]]
            ckv_all = jnp.concatenate(lat_parts_c + [ckv[:n_ctx]], axis=0)
            kp_all = jnp.concatenate(lat_parts_p + [kpe[:n_ctx]], axis=0).astype(BF16)
            kp_pad = jnp.pad(kp_all, ((0, 0), (0, LANES - kp_all.shape[1])))
            k_cat, v = mla_kv(ckv_all, kp_pad, w_mla_kvb[j])
            k_lat = past + dec_seq
            w_out = w_mla_out[j].astype(BF16)
            x = mla_attention(x, q_cat, k_cat, v, w_out, m[2], 0, dec_batch * k_lat, batch, seq, seq,
                              n_ctx, dec_seq)
            x = mla_attention(x, q_cat, k_cat, v, w_out, m[2], n_ctx, 0, dec_batch, dec_seq, k_lat,
                              n_ctx, dec_seq)
            outs["ckv"] = ckv[:n_ctx].reshape(batch, 1, seq, -1)
            outs["kpe"] = kpe[:n_ctx].reshape(batch, 1, seq, -1)
        x = moe_layer(x, row(norm_ffn[i]), m[3], m[4], m[5], w_router_split, b_router_col,
                      w_exp_gate, w_exp_up, w_exp_down, i, g_final, n_ctx, dec_seq,
                      final_norm=(i == depth - 1), split_output=(i == depth - 1))

    y_prompt = x[0].reshape(batch, seq, d)
    y_sample = x[1].reshape(dec_batch, dec_seq, d)
    return (y_prompt, y_sample, outs["C"], outs["n"], outs["m"], outs["ckv"], outs["kpe"])
```

```python
import functools

import numpy as np
import jax
import jax.numpy as jnp
from jax import lax
from jax.experimental import pallas as pl
from jax.experimental.pallas import tpu as pltpu

F32 = jnp.float32
BF16 = jnp.bfloat16

NORM_EPS = 1e-6
GRID_W = 64
POOL_WINDOWS = (2, 4, 8, 16)
MLSTM_HEADS = 4
MLSTM_CHUNK = 256
MLA_HEADS = 8
MLA_NOPE = 128
MLA_ROPE = 64
MLA_V = 128
ROPE_BASE = 10000.0
N_EXPERTS = 16
N_EXPERT_GROUPS = 4
EXPERTS_PER_GROUP = N_EXPERTS // N_EXPERT_GROUPS

LANES = 128
SUBLANES = 8
VMEM_LIMIT = 56 * 1024 * 1024
MOE_VMEM_LIMIT = 60 * 1024 * 1024
POOL_TILE = 256
POOL_HALO = 8
ATTN_Q_BLOCK = 512
MOE_TILE = 512
MOE_SEG_ALIGN = 16
MOE_ROW_BLOCK = 160
MOE_TAIL_BLOCK = 64
MOE_LOAD_EXPERTS = 2


def _params(*sem, vmem_limit=VMEM_LIMIT):
    return pltpu.CompilerParams(dimension_semantics=sem, vmem_limit_bytes=vmem_limit)


def _rms(x, g):
    return x * lax.rsqrt(jnp.mean(x * x, axis=-1, keepdims=True) + NORM_EPS) * g


def _modulated(x, g, shift, scale):
    return _rms(x, g) * (1.0 + scale) + shift


def _silu(x):
    return x * jax.nn.sigmoid(x)


def _bdot(a, b):
    return jnp.dot(a.astype(BF16), b.astype(BF16), preferred_element_type=F32)


def _bdot_nt(a, b):
    return lax.dot_general(a.astype(BF16), b.astype(BF16), (((1,), (1,)), ((), ())),
                           preferred_element_type=F32)


def _bdot_tn(a, b):
    return lax.dot_general(a.astype(BF16), b.astype(BF16), (((0,), (0,)), ((), ())),
                           preferred_element_type=F32)


def _cvec_index(n_ctx, dec_seq, tm):
    def idx(i):
        r = i * tm
        return jnp.where(r < n_ctx, 0, (r - n_ctx) // dec_seq + 1)
    return idx


def _ada_kernel(c_ref, w_ref, b_ref, o_ref):
    o_ref[...] = _bdot(_silu(c_ref[...]), w_ref[...]) + b_ref[...]


def ada_mod_all(cvecs, w_ada, b_ada, tn=3072):
    depth, d, n6 = w_ada.shape
    rows = cvecs.shape[0]
    return pl.pallas_call(
        _ada_kernel,
        grid=(depth, n6 // tn),
        in_specs=[pl.BlockSpec((rows, d), lambda l, n: (0, 0)),
                  pl.BlockSpec((None, d, tn), lambda l, n: (l, 0, n)),
                  pl.BlockSpec((None, 1, tn), lambda l, n: (l, 0, n))],
        out_specs=pl.BlockSpec((None, rows, tn), lambda l, n: (l, 0, n)),
        out_shape=jax.ShapeDtypeStruct((depth, rows, n6), F32),
        compiler_params=_params("arbitrary", "arbitrary"),
        name="ada_mod",
    )(cvecs, w_ada, b_ada.reshape(depth, 1, n6))


def _pool_kernel(*refs, n_ctx_tiles, ctx_seq_tiles, lat_seq_tiles, split_input):
    n_x = 6 if split_input else 3
    x_refs = refs[:n_x]
    g_ref, sh_ref, sc_ref, gt_ref, wp_ref, ps_ref, o_ref, buf_ref, *lvl_refs = refs[n_x:]
    i = pl.program_id(0)
    is_ctx = i < n_ctx_tiles
    j = jnp.where(is_ctx, i % ctx_seq_tiles, (i - n_ctx_tiles) % lat_seq_tiles)
    nt = jnp.where(is_ctx, ctx_seq_tiles, lat_seq_tiles)
    g, sh, sc = g_ref[...], sh_ref[...], sc_ref[...]
    tp, hl = POOL_TILE, POOL_HALO
    gw = o_ref.shape[1] // len(POOL_WINDOWS)

    def fill(xc_ref, xp_ref, xn_ref):
        buf_ref[pl.ds(0, hl), :] = jnp.where(j == 0, 0.0, _modulated(xp_ref[...], g, sh, sc))
        buf_ref[pl.ds(hl, tp), :] = _modulated(xc_ref[...], g, sh, sc)
        buf_ref[pl.ds(hl + tp, hl), :] = jnp.where(j == nt - 1, 0.0,
                                                    _modulated(xn_ref[...], g, sh, sc))
        o_ref[...] = xc_ref[...]

    if split_input:
        pl.when(is_ctx)(lambda: fill(*x_refs[:3]))
        pl.when(jnp.logical_not(is_ctx))(lambda: fill(*x_refs[3:]))
    else:
        fill(*x_refs)

    rows = tp + 2 * hl
    buf_ref[pl.ds(rows, hl), :] = jnp.zeros((hl, buf_ref.shape[1]), F32)
    for ref in lvl_refs:
        ref[pl.ds(rows, hl), :] = jnp.zeros((hl, gw), F32)
    pos = j * tp + lax.broadcasted_iota(jnp.int32, (tp, 1), 0)
    seq_len = nt * tp
    for gi, w in enumerate(POOL_WINDOWS):
        cols = pl.ds(gi * gw, gw)
        src, src_cols = buf_ref, cols
        for lvl in range(1, w.bit_length() - 1):
            step = 1 << (lvl - 1)
            dst = lvl_refs[lvl - 1]
            dst[pl.ds(0, rows), :] = src[pl.ds(0, rows), src_cols] + src[pl.ds(step, rows), src_cols]
            src, src_cols = dst, pl.ds(0, gw)
        acc = src[pl.ds(hl - w // 2, tp), src_cols] + src[pl.ds(hl, tp), src_cols]
        cnt = jnp.minimum(pos + w // 2, seq_len) - jnp.maximum(pos - w // 2, 0)
        pooled = acc / cnt.astype(F32) - buf_ref[pl.ds(hl, tp), cols]
        y = _bdot(pooled, wp_ref[gi]) * ps_ref[:, cols]
        o_ref[:, cols] = o_ref[:, cols] + gt_ref[:, cols] * y


def _halo_specs(tp, hl, d, tile_off, n_rows):
    hb = tp // hl
    last_tile, last_hblk = n_rows // tp - 1, n_rows // hl - 1
    tile = lambda i: jnp.clip(i - tile_off, 0, last_tile)
    return [pl.BlockSpec((tp, d), lambda i: (tile(i), 0)),
            pl.BlockSpec((hl, d), lambda i: (jnp.clip(tile(i) * hb - 1, 0, last_hblk), 0)),
            pl.BlockSpec((hl, d), lambda i: (jnp.clip((tile(i) + 1) * hb, 0, last_hblk), 0))]


def pool_layer(xs, g, shift, scale, gate, w_pool, pool_scale, n_ctx, seq, dec_seq):
    split = isinstance(xs, tuple)
    tp, hl = POOL_TILE, POOL_HALO
    if split:
        d = xs[0].shape[1]
        t = xs[0].shape[0] + xs[1].shape[0]
        x_specs = (_halo_specs(tp, hl, d, 0, xs[0].shape[0])
                   + _halo_specs(tp, hl, d, n_ctx // tp, xs[1].shape[0]))
        x_args = (xs[0],) * 3 + (xs[1],) * 3
    else:
        t, d = xs.shape
        x_specs = _halo_specs(tp, hl, d, 0, t)
        x_args = (xs,) * 3
    cidx = _cvec_index(n_ctx, dec_seq, tp)
    mod_spec = pl.BlockSpec((None, 1, d), lambda i: (cidx(i), 0, 0))
    row_spec = pl.BlockSpec((1, d), lambda i: (0, 0))
    assert all(w & (w - 1) == 0 and w // 2 <= hl for w in POOL_WINDOWS)
    n_levels = max(POOL_WINDOWS).bit_length() - 2
    kern = functools.partial(_pool_kernel, n_ctx_tiles=n_ctx // tp, ctx_seq_tiles=seq // tp,
                             lat_seq_tiles=dec_seq // tp, split_input=split)
    return pl.pallas_call(
        kern,
        grid=(t // tp,),
        in_specs=x_specs + [row_spec, mod_spec, mod_spec, mod_spec,
                            pl.BlockSpec(w_pool.shape, lambda i: (0, 0, 0)), row_spec],
        out_specs=pl.BlockSpec((tp, d), lambda i: (i, 0)),
        out_shape=jax.ShapeDtypeStruct((t, d), F32),
        scratch_shapes=[pltpu.VMEM((tp + 3 * hl, d), F32)]
        + [pltpu.VMEM((tp + 3 * hl, d // len(POOL_WINDOWS)), F32)] * n_levels,
        compiler_params=_params("arbitrary"),
        name="pool_mixer",
    )(*x_args, g, shift, scale, gate, w_pool, pool_scale)


def _route(sel, scores):
    e, tm = sel.shape
    row = lax.broadcasted_iota(jnp.int32, (e, tm), 0)
    best = jnp.zeros((1, tm), jnp.int32)
    best_sc = None
    for gidx in range(N_EXPERT_GROUPS):
        r = [sel[gidx * EXPERTS_PER_GROUP + k:gidx * EXPERTS_PER_GROUP + k + 1, :]
             for k in range(EXPERTS_PER_GROUP)]
        top2 = None
        for a in range(EXPERTS_PER_GROUP):
            for b in range(a + 1, EXPERTS_PER_GROUP):
                s = r[a] + r[b]
                top2 = s if top2 is None else jnp.maximum(top2, s)
        if best_sc is None:
            best_sc = top2
        else:
            better = top2 > best_sc
            best = jnp.where(better, gidx, best)
            best_sc = jnp.where(better, top2, best_sc)
    neg = -jnp.inf
    masked = jnp.where(row // EXPERTS_PER_GROUP == best, sel, neg)
    m1 = jnp.max(masked, axis=0, keepdims=True)
    i1 = jnp.min(jnp.where(masked == m1, row, e), axis=0, keepdims=True)
    masked2 = jnp.where(row == i1, neg, masked)
    m2 = jnp.max(masked2, axis=0, keepdims=True)
    i2 = jnp.min(jnp.where(masked2 == m2, row, e), axis=0, keepdims=True)
    hot1 = row == i1
    hot2 = row == i2
    w1 = jnp.sum(jnp.where(hot1, scores, 0.0), axis=0, keepdims=True)
    w2 = jnp.sum(jnp.where(hot2, scores, 0.0), axis=0, keepdims=True)
    tot = w1 + w2
    return best, jnp.where(hot1, w1 / tot, 0.0) + jnp.where(hot2, w2 / tot, 0.0)


def _split_bf16(a, parts):
    out = []
    for _ in range(parts):
        p = a.astype(BF16)
        out.append(p)
        a = a - p.astype(F32)
    return out


def _pad_rows(a, rows):
    return jnp.concatenate([a, jnp.zeros((rows - a.shape[0], a.shape[1]), a.dtype)], axis=0)


def _moe_kernel(x_ref, g_ref, sh_ref, sc_ref, gt_ref, wr_ref, br_ref, wgf_ref, wuf_ref, wdf_ref,
                gf_ref, *rest, final_norm, n_ctx_tiles, n_experts):
    out_refs, (wgu_ref, wd_ref), scratch = rest[:-5], rest[-5:-3], rest[-3:]
    i = pl.program_id(0)
    per_step, f = wgf_ref.shape[0], wgf_ref.shape[2]
    n_load = n_experts // per_step
    for s in range(n_load):
        @pl.when(i == s)
        def _(s=s):
            for j in range(per_step):
                gi, k = divmod(s * per_step + j, EXPERTS_PER_GROUP)
                wgu_ref[gi, :, pl.ds(2 * k * f, f)] = wgf_ref[j].astype(BF16)
                wgu_ref[gi, :, pl.ds((2 * k + 1) * f, f)] = wuf_ref[j].astype(BF16)
                wd_ref[gi, pl.ds(k * f, f), :] = wdf_ref[j].astype(BF16)

    @pl.when(i >= n_load)
    def _():
        _moe_tile(x_ref, g_ref, sh_ref, sc_ref, gt_ref, wr_ref, br_ref, wgu_ref, wd_ref,
                  gf_ref, out_refs, scratch, i - n_load, final_norm, n_ctx_tiles)


def _moe_tile(x_ref, g_ref, sh_ref, sc_ref, gt_ref, wr_ref, br_ref, wgu_ref, wd_ref,
              gf_ref, out_refs, scratch, tile, final_norm, n_ctx_tiles):
    hp_scr, cw_scr, yp_scr = scratch
    tr, d = x_ref.shape
    trp = hp_scr.shape[0]
    ng, eg = N_EXPERT_GROUPS, EXPERTS_PER_GROUP
    x = x_ref[...]
    h = _modulated(x, g_ref[...], sh_ref[...], sc_ref[...])
    hb = h.astype(BF16)

    h_lo = (h - hb.astype(F32)).astype(BF16)
    wr = wr_ref[...]
    lg = jnp.dot(hb, wr, preferred_element_type=F32)
    logits = lg[:, :LANES] + lg[:, LANES:] + jnp.dot(h_lo, wr[:, :LANES], preferred_element_type=F32)
    scores = jax.nn.sigmoid(logits.T[:N_EXPERTS, :])
    best, comb_t = _route(scores + br_ref[...], scores)

    grp = lax.broadcasted_iota(jnp.int32, (SUBLANES, tr), 0)
    hot_t = (grp == best).astype(F32)
    cw_t = hot_t[0:1, :] * comb_t[0:eg, :]
    for gi in range(1, ng):
        cw_t = cw_t + hot_t[gi:gi + 1, :] * comb_t[gi * eg:(gi + 1) * eg, :]
    cw_c = _pad_rows(cw_t, LANES).T

    ia = lax.broadcasted_iota(jnp.int32, (tr, tr), 0)
    ib = lax.broadcasted_iota(jnp.int32, (tr, tr), 1)
    before = jnp.where(ia < ib, 1.0, 0.0).astype(BF16)
    rank_t = jnp.dot(hot_t.astype(BF16), before, preferred_element_type=F32)

    starts, counts = [], []
    off = jnp.int32(0)
    for gi in range(ng):
        n = jnp.sum(hot_t[gi:gi + 1, :]).astype(jnp.int32)
        n = ((n + MOE_SEG_ALIGN - 1) // MOE_SEG_ALIGN) * MOE_SEG_ALIGN
        starts.append(off)
        counts.append(n)
        off = off + n

    pos_t = hot_t[0:1, :] * (rank_t[0:1, :] + starts[0].astype(F32))
    for gi in range(1, ng):
        pos_t = pos_t + hot_t[gi:gi + 1, :] * (rank_t[gi:gi + 1, :] + starts[gi].astype(F32))
    pos_c = _pad_rows(pos_t, LANES).T[:, 0:1]
    used = tr + ng * MOE_SEG_ALIGN
    perm = jnp.where(lax.broadcasted_iota(jnp.int32, (used, tr), 0) == pos_t.astype(jnp.int32),
                     1.0, 0.0).astype(BF16)
    perm_t = jnp.where(lax.broadcasted_iota(jnp.int32, (tr, used), 1) == pos_c.astype(jnp.int32),
                       1.0, 0.0).astype(BF16)

    hp_scr[pl.ds(0, used), :] = jnp.dot(perm, hb, preferred_element_type=F32).astype(BF16)
    hp_scr[pl.ds(used, trp - used), :] = jnp.zeros((trp - used, d), BF16)
    cw_pair = jnp.dot(perm, jnp.concatenate(_split_bf16(cw_c, 2), axis=1), preferred_element_type=F32)
    cw_scr[pl.ds(0, used), :] = cw_pair[:, :LANES] + cw_pair[:, LANES:]
    cw_scr[pl.ds(used, trp - used), :] = jnp.zeros((trp - used, LANES), F32)
    yp_scr[...] = jnp.zeros_like(yp_scr)

    f = wd_ref.shape[1] // eg

    def expert_block(gi, row0, n_rows, seg_end=None):
        rows = pl.ds(pl.multiple_of(row0, MOE_SEG_ALIGN), n_rows)
        cwb = cw_scr[rows, :]
        gu = jnp.dot(hp_scr[rows, :], wgu_ref[gi], preferred_element_type=F32)
        hid = [(_silu(gu[:, 2 * k * f:(2 * k + 1) * f]) * gu[:, (2 * k + 1) * f:(2 * k + 2) * f]
                * cwb[:, k:k + 1]).astype(BF16) for k in range(eg)]
        y = jnp.dot(jnp.concatenate(hid, axis=1), wd_ref[gi], preferred_element_type=F32).astype(BF16)
        if seg_end is None:
            yp_scr[rows, :] = y
        else:
            inside = row0 + lax.broadcasted_iota(jnp.int32, (n_rows, 1), 0) < seg_end
            yp_scr[rows, :] = jnp.where(inside, y, yp_scr[rows, :])

    sb, tb = MOE_ROW_BLOCK, MOE_TAIL_BLOCK
    for gi in range(ng):
        expert_block(gi, starts[gi], sb)
    for gi in range(ng):
        def tail(b, carry, gi=gi):
            expert_block(gi, starts[gi] + sb + b * tb, tb, starts[gi] + counts[gi])
            return carry
        lax.fori_loop(0, (jnp.maximum(counts[gi] - sb, 0) + tb - 1) // tb, tail, 0)

    moe = jnp.dot(perm_t, yp_scr[pl.ds(0, used), :], preferred_element_type=F32)
    out = x + gt_ref[...] * moe
    if final_norm:
        out = _rms(out, gf_ref[...])
    if n_ctx_tiles is None:
        out_refs[0][...] = out
    else:
        is_ctx = tile < n_ctx_tiles

        @pl.when(is_ctx)
        def _():
            out_refs[0][...] = out

        @pl.when(jnp.logical_not(is_ctx))
        def _():
            out_refs[1][...] = out


def moe_layer(x, g, shift, scale, gate, w_router_split, b_router_col, wg, wu, wd, layer, g_final,
              n_ctx, dec_seq, final_norm, split_output=False, tm=MOE_TILE):
    t, d = x.shape
    _, n_e, _, f = wg.shape
    trp = tm + N_EXPERT_GROUPS * MOE_SEG_ALIGN + max(MOE_ROW_BLOCK, MOE_TAIL_BLOCK)
    per_step = MOE_LOAD_EXPERTS
    n_load = n_e // per_step
    tile = lambda i: jnp.maximum(i - n_load, 0)
    cidx = _cvec_index(n_ctx, dec_seq, tm)
    mod_spec = pl.BlockSpec((None, 1, d), lambda i: (cidx(tile(i)), 0, 0))
    row_spec = pl.BlockSpec((1, d), lambda i: (0, 0))
    expert = lambda i: jnp.minimum(i, n_load - 1)
    nct = n_ctx // tm
    if split_output:
        out_specs = [pl.BlockSpec((tm, d), lambda i: (jnp.minimum(tile(i), nct - 1), 0)),
                     pl.BlockSpec((tm, d), lambda i: (jnp.maximum(tile(i) - nct, 0), 0))]
        out_shape = [jax.ShapeDtypeStruct((n_ctx, d), F32), jax.ShapeDtypeStruct((t - n_ctx, d), F32)]
    else:
        out_specs = pl.BlockSpec((tm, d), lambda i: (tile(i), 0))
        out_shape = jax.ShapeDtypeStruct((t, d), F32)
    return pl.pallas_call(
        functools.partial(_moe_kernel, final_norm=final_norm,
                          n_ctx_tiles=nct if split_output else None, n_experts=n_e),
        grid=(n_load + t // tm,),
        in_specs=[pl.BlockSpec((tm, d), lambda i: (tile(i), 0)),
                  row_spec, mod_spec, mod_spec, mod_spec,
                  pl.BlockSpec(w_router_split.shape, lambda i: (0, 0)),
                  pl.BlockSpec(b_router_col.shape, lambda i: (0, 0)),
                  pl.BlockSpec((None, per_step, d, f), lambda i: (layer, expert(i), 0, 0)),
                  pl.BlockSpec((None, per_step, d, f), lambda i: (layer, expert(i), 0, 0)),
                  pl.BlockSpec((None, per_step, f, d), lambda i: (layer, expert(i), 0, 0)),
                  row_spec],
        out_specs=out_specs,
        out_shape=out_shape,
        scratch_shapes=[pltpu.VMEM((N_EXPERT_GROUPS, d, 2 * EXPERTS_PER_GROUP * f), BF16),
                        pltpu.VMEM((N_EXPERT_GROUPS, EXPERTS_PER_GROUP * f, d), BF16),
                        pltpu.VMEM((trp, d), BF16), pltpu.VMEM((trp, LANES), F32),
                        pltpu.VMEM((trp, d), BF16)],
        compiler_params=_params("arbitrary", vmem_limit=MOE_VMEM_LIMIT),
        name="moe",
    )(x, g, shift, scale, gate, w_router_split, b_router_col, wg, wu, wd, g_final)


def _per_stream(n_ctx_tiles, ctx_ref, lat_ref, fn):
    i = pl.program_id(0)

    @pl.when(i < n_ctx_tiles)
    def _():
        fn(ctx_ref[...])

    @pl.when(i >= n_ctx_tiles)
    def _():
        fn(lat_ref[...])


def _stream_specs(tm, k, n_ctx_tiles):
    return [pl.BlockSpec((tm, k), lambda i: (jnp.minimum(i, n_ctx_tiles - 1), 0)),
            pl.BlockSpec((tm, k), lambda i: (jnp.maximum(i - n_ctx_tiles, 0), 0))]


def _mlstm_out_kernel(x_ref, hc_ref, hl_ref, og_ref, hg_ref, w_ref, gt_ref, o_ref, *, n_ctx_tiles):
    def run(hs):
        a = jax.nn.sigmoid(og_ref[...].astype(F32)) * (hs.astype(F32) * hg_ref[...])
        o_ref[...] = x_ref[...] + gt_ref[...] * _bdot(a, w_ref[...])
    _per_stream(n_ctx_tiles, hc_ref, hl_ref, run)


def mlstm_out(x, hs_ctx, hs_lat, o_gate, head_g, w, gate, n_ctx, dec_seq, tm=1024):
    t, d = x.shape
    k = hs_ctx.shape[1]
    cidx = _cvec_index(n_ctx, dec_seq, tm)
    nct = n_ctx // tm
    return pl.pallas_call(
        functools.partial(_mlstm_out_kernel, n_ctx_tiles=nct),
        grid=(t // tm,),
        in_specs=[pl.BlockSpec((tm, d), lambda i: (i, 0))] + _stream_specs(tm, k, nct) + [
            pl.BlockSpec((tm, k), lambda i: (i, 0)),
            pl.BlockSpec((1, k), lambda i: (0, 0)),
            pl.BlockSpec((k, d), lambda i: (0, 0)),
            pl.BlockSpec((None, 1, d), lambda i: (cidx(i), 0, 0))],
        out_specs=pl.BlockSpec((tm, d), lambda i: (i, 0)),
        out_shape=jax.ShapeDtypeStruct((t, d), F32),
        compiler_params=_params("arbitrary"),
        name="mlstm_out",
    )(x, hs_ctx, hs_lat, o_gate, head_g, w, gate)


def _mlstm_proj_kernel(x_ref, g_ref, sh_ref, sc_ref, wqk_ref, wv_ref, wo_ref, wgt_ref, bg_ref,
                       qs_ref, qkv_ref, o_ref, gc_ref, gr_ref):
    h = _modulated(x_ref[...], g_ref[...], sh_ref[...], sc_ref[...]).astype(BF16)
    nqk = wqk_ref.shape[0]
    qkv_ref[:, :nqk] = (_bdot_nt(h, wqk_ref[...]) * qs_ref[...]).astype(BF16)
    qkv_ref[:, nqk:] = _bdot_nt(h, wv_ref[...]).astype(BF16)
    o_ref[...] = _bdot_nt(h, wo_ref[...]).astype(BF16)
    gates = _bdot_nt(h, wgt_ref[...]) + bg_ref[...]
    gc_ref[...] = gates
    gr_ref[...] = gates.T[:gr_ref.shape[0], :]


def mlstm_proj(x, g, shift, scale, w_in_all, layer, b_gate, n_ctx, dec_seq, tm=1024):
    t, d = x.shape
    hh = MLSTM_HEADS
    hv = d
    hk = hv // 2
    ng = 4 * hh
    w_t_all = jnp.swapaxes(w_in_all, 1, 2)
    w_g_pad = jnp.pad(w_t_all[layer, 2 * hk + 2 * hv:, :], ((0, LANES - ng), (0, 0)))
    b_pad = jnp.pad(b_gate.reshape(1, ng), ((0, 0), (0, LANES - ng)))
    dk = hk // hh
    q_scale = jnp.concatenate([jnp.full((1, hk), dk ** -0.5, F32), jnp.ones((1, hk), F32)], axis=1)
    cidx = _cvec_index(n_ctx, dec_seq, tm)
    mod_spec = pl.BlockSpec((None, 1, d), lambda i: (cidx(i), 0, 0))
    full = lambda a: pl.BlockSpec(a.shape, lambda i: (0,) * a.ndim)
    assert 2 * hk == hv
    w_col = lambda n: pl.BlockSpec((None, hv, d), lambda i: (layer, n, 0))
    return pl.pallas_call(
        _mlstm_proj_kernel,
        grid=(t // tm,),
        in_specs=[pl.BlockSpec((tm, d), lambda i: (i, 0)), full(g), mod_spec, mod_spec,
                  w_col(0), w_col(1), w_col(2), full(w_g_pad), full(b_pad), full(q_scale)],
        out_specs=[pl.BlockSpec((tm, 2 * hk + hv), lambda i: (i, 0)),
                   pl.BlockSpec((tm, hv), lambda i: (i, 0)),
                   pl.BlockSpec((tm, LANES), lambda i: (i, 0)),
                   pl.BlockSpec((ng, tm), lambda i: (0, i))],
        out_shape=[jax.ShapeDtypeStruct((t, 2 * hk + hv), BF16),
                   jax.ShapeDtypeStruct((t, hv), BF16),
                   jax.ShapeDtypeStruct((t, LANES), F32),
                   jax.ShapeDtypeStruct((ng, t), F32)],
        compiler_params=_params("arbitrary"),
        name="mlstm_proj",
    )(x, g, shift, scale, w_t_all, w_t_all, w_t_all, w_g_pad, b_pad, q_scale)


def _log_sigmoid(x):
    return jnp.minimum(x, 0.0) - jnp.log(1.0 + jnp.exp(-jnp.abs(x)))


def _gate_cumsums(gc, gr, causal_bf, feeds_bf):
    bc = br = None
    for part in _split_bf16(_log_sigmoid(gc), 3):
        t = jnp.dot(causal_bf, part, preferred_element_type=F32)
        bc = t if bc is None else bc + t
    for part in _split_bf16(_log_sigmoid(gr), 3):
        t = jnp.dot(part, feeds_bf, preferred_element_type=F32)
        br = t if br is None else br + t
    return bc, br


def _mlstm_chunk(q, k, v, i_col, b_col, i_row, b_row, c_st, n_st, m_st, causal, rev):
    l = q.shape[0]
    zero_state = c_st is None
    if zero_state:
        m_st = 0.0
    g_row = i_row - b_row
    log_w = jnp.where(causal, g_row, -jnp.inf)
    c_t = jnp.maximum(m_st, jnp.max(log_w, axis=1, keepdims=True))
    dw = jnp.exp(log_w - c_t)
    a = _bdot_nt(q, k) * dw
    a_hi, a_lo = _split_bf16(a, 2)
    ones = jnp.ones((l, LANES), BF16)
    den = (jnp.dot(a_hi, ones, preferred_element_type=F32)
           + jnp.dot(a_lo, ones, preferred_element_type=F32))[:, 0:1]
    if not zero_state:
        sw = jnp.exp(m_st - c_t)
        n_hi, n_lo = _split_bf16(jnp.broadcast_to(n_st, (LANES, n_st.shape[1])), 2)
        den = den + sw * (_bdot_nt(q, n_hi) + _bdot_nt(q, n_lo))[:, 0:1]
    inv = 1.0 / jnp.maximum(jnp.abs(den), jnp.exp(-(b_col + c_t)))
    h = inv * jnp.dot(a_hi, v, preferred_element_type=F32)
    if not zero_state:
        h = h + (sw * inv) * _bdot(q, c_st)
    b_last = b_col[0:1, :] if rev else b_col[l - 1:l, :]
    log_k = b_last - b_col + i_col
    m_new = jnp.maximum(b_last + m_st, jnp.max(log_k, axis=0, keepdims=True))
    kw = jnp.exp(log_k - m_new)
    kwk = kw * k.astype(F32)
    c_new = _bdot_tn(kwk, v)
    n_new = jnp.sum(kwk, axis=0, keepdims=True)
    if not zero_state:
        decay = jnp.exp(b_last + m_st - m_new)
        c_new = decay * c_st + c_new
        n_new = decay * n_st + n_new
    return h, c_new, n_new, m_new


def _mlstm_scan_kernel(*refs, n_chunks, zero_init):
    single = zero_init and n_chunks == 1
    if zero_init:
        q_ref, k_ref, v_ref, gc_ref, gr_ref, hs_ref, c_ref, n_ref, m_ref, hf_scr, hb_scr = refs
        if not single:
            c_ref[...] = jnp.zeros_like(c_ref)
            n_ref[...] = jnp.zeros_like(n_ref)
            m_ref[...] = jnp.zeros_like(m_ref)
    else:
        (q_ref, k_ref, v_ref, gc_ref, gr_ref, c0_ref, n0_ref, m0_ref,
         hs_ref, c_ref, n_ref, m_ref, hf_scr, hb_scr) = refs
        c_ref[...] = c0_ref[...]
        n_ref[...] = n0_ref[...]
        m_ref[...] = m0_ref[...]
    l = MLSTM_CHUNK
    hh = MLSTM_HEADS
    dk = q_ref.shape[1] // hh
    dv = v_ref.shape[1] // hh

    tt = lax.broadcasted_iota(jnp.int32, (l, l), 0)
    ss = lax.broadcasted_iota(jnp.int32, (l, l), 1)
    masks = (ss <= tt, ss >= tt)
    masks_bf = tuple(jnp.where(m, 1.0, 0.0).astype(BF16) for m in masks)

    def body(c, carry):
        cr = n_chunks - 1 - c
        start = (lambda j: j * l) if single else (lambda j: pl.multiple_of(j * l, l))
        rows = (pl.ds(start(c), l), pl.ds(start(cr), l))
        grs = (gr_ref[c], gr_ref[cr])
        outs, states = ([], []), []
        for d in range(2):
            q, k, v, gc, gr = q_ref[rows[d], :], k_ref[rows[d], :], v_ref[rows[d], :], gc_ref[rows[d], :], grs[d]
            bc, br = _gate_cumsums(gc, gr, masks_bf[d], masks_bf[1 - d])
            for h in range(hh):
                ci, cf = 2 * d * hh + h, (2 * d + 1) * hh + h
                state = (None,) * 3 if single else (c_ref[d, h], n_ref[d, h], m_ref[d, h])
                o, *st = _mlstm_chunk(q[:, h * dk:(h + 1) * dk], k[:, h * dk:(h + 1) * dk],
                                      v[:, h * dv:(h + 1) * dv], gc[:, ci:ci + 1], bc[:, cf:cf + 1],
                                      gr[ci:ci + 1, :], br[cf:cf + 1, :], *state, masks[d], d == 1)
                outs[d].append(o)
                states.append((d, h, st))
        hf_scr[rows[0], :] = jnp.concatenate(outs[0], axis=1)
        hb_scr[rows[1], :] = jnp.concatenate(outs[1], axis=1)
        for d, h, (c_new, n_new, m_new) in states:
            c_ref[d, h], n_ref[d, h], m_ref[d, h] = c_new, n_new, m_new
        return carry

    if single:
        body(0, 0)
    else:
        lax.fori_loop(0, n_chunks, body, 0)
    for h in range(hh):
        cols = pl.ds(h * dv, dv)
        hs = hf_scr[:, cols] + hb_scr[:, cols]
        hs_ref[:, cols] = (hs * lax.rsqrt(jnp.mean(hs * hs, axis=-1, keepdims=True)
                                          + NORM_EPS)).astype(BF16)


def mlstm_scan(qkv, gcol, grow, init, row_off, n_seq, seq_len):
    hh = MLSTM_HEADS
    hv = qkv.shape[1] // 2
    dv = hv // hh
    dk = dv // 2
    l = MLSTM_CHUNK
    nc = seq_len // l
    ob = row_off // seq_len
    kern = functools.partial(_mlstm_scan_kernel, n_chunks=nc, zero_init=init is None)
    st = lambda *tail: pl.BlockSpec((None, 2, hh) + tail, lambda s: (s, 0, 0) + (0,) * len(tail))
    states = [st(dk, dv), st(1, dk), st(1, 1)]
    return pl.pallas_call(
        kern,
        grid=(n_seq,),
        in_specs=[pl.BlockSpec((seq_len, hh * dk), lambda s: (ob + s, 0)),
                  pl.BlockSpec((seq_len, hh * dk), lambda s: (ob + s, 1)),
                  pl.BlockSpec((seq_len, hv), lambda s: (ob + s, 1)),
                  pl.BlockSpec((seq_len, LANES), lambda s: (ob + s, 0)),
                  pl.BlockSpec((nc, 4 * hh, l), lambda s: (ob + s, 0, 0))]
                 + ([] if init is None else states),
        out_specs=[pl.BlockSpec((seq_len, hv), lambda s: (s, 0))] + states,
        out_shape=[jax.ShapeDtypeStruct((n_seq * seq_len, hv), BF16),
                   jax.ShapeDtypeStruct((n_seq, 2, hh, dk, dv), F32),
                   jax.ShapeDtypeStruct((n_seq, 2, hh, 1, dk), F32),
                   jax.ShapeDtypeStruct((n_seq, 2, hh, 1, 1), F32)],
        scratch_shapes=[pltpu.VMEM((seq_len, hv), F32), pltpu.VMEM((seq_len, hv), F32)],
        compiler_params=_params("arbitrary"),
        name="mlstm_scan",
    )(qkv, qkv, qkv, gcol, grow, *(() if init is None else init))


def _mla_proj_kernel(x_ref, g_ref, sh_ref, sc_ref, win_ref, qg_ref, kvg_ref, wqb_ref, cos_ref,
                     sin_ref, q_ref, ckv_ref, kpe_ref, *, q_lora, kv_lora, rope, n_heads):
    h = _modulated(x_ref[...], g_ref[...], sh_ref[...], sc_ref[...])
    proj = _bdot_nt(h, win_ref[...])
    q_lat = proj[:, :q_lora]
    ckv_ref[...] = _rms(proj[:, q_lora:q_lora + kv_lora], kvg_ref[...])
    cos, sin = cos_ref[...], sin_ref[...]
    kpe = proj[:, q_lora + kv_lora:q_lora + kv_lora + rope]
    kpe_rot = proj[:, q_lora + kv_lora + rope:q_lora + kv_lora + 2 * rope]
    kpe_ref[...] = kpe * cos[:, :rope] + kpe_rot * sin[:, :rope]
    q = _bdot_nt(_rms(q_lat, qg_ref[...]), wqb_ref[...])
    n, nr = n_heads * LANES, n_heads * rope
    reps = nr // LANES
    roped = (q[:, n:n + nr] * jnp.concatenate([cos] * reps, axis=1)
             + q[:, n + nr:] * jnp.concatenate([sin] * reps, axis=1))
    pad = jnp.zeros((q.shape[0], LANES - rope), F32)
    pieces = []
    for hd in range(n_heads):
        pieces += [q[:, hd * LANES:(hd + 1) * LANES], roped[:, hd * rope:(hd + 1) * rope], pad]
    q_ref[...] = jnp.concatenate(pieces, axis=1).astype(BF16)


def _rot_rows(wt):
    n, k = wt.shape
    quarter = MLA_ROPE // 4
    w4 = wt.reshape(n // (2 * quarter), 2, quarter, k)
    return jnp.concatenate([-w4[:, 1:2], w4[:, 0:1]], axis=1).reshape(n, k)


def _rope_tables(n_ctx, dec_batch, dec_seq):
    quarter = MLA_ROPE // 4
    freq = np.power(np.float32(ROPE_BASE), -np.arange(quarter, dtype=np.float32) / np.float32(quarter))
    pos = np.arange(dec_seq)
    ang_r = (pos // GRID_W).astype(np.float32)[:, None] * freq[None, :]
    ang_c = (pos % GRID_W).astype(np.float32)[:, None] * freq[None, :]
    ang = np.concatenate([ang_r, ang_r, ang_c, ang_c], axis=1).astype(np.float32)
    cos = np.concatenate([np.ones((n_ctx, MLA_ROPE), np.float32)] + [np.cos(ang)] * dec_batch, axis=0)
    sin = np.concatenate([np.zeros((n_ctx, MLA_ROPE), np.float32)] + [np.sin(ang)] * dec_batch, axis=0)
    reps = LANES // MLA_ROPE
    return (jnp.asarray(np.tile(cos, (1, reps)), F32), jnp.asarray(np.tile(sin, (1, reps)), F32))


def mla_proj(x, g, shift, scale, w_in, q_g, kv_g, w_qb, cos, sin, n_ctx, dec_seq, tm=1024):
    t, d = x.shape
    hh, nope, rope = MLA_HEADS, MLA_NOPE, MLA_ROPE
    q_lora = q_g.shape[1]
    kv_lora = kv_g.shape[1]
    w_in_t = w_in.T
    w_in_ext = jnp.concatenate([w_in_t, _rot_rows(w_in_t[q_lora + kv_lora:])], axis=0)
    w3 = w_qb.T.reshape(hh, nope + rope, q_lora)
    w_qn = w3[:, :nope].reshape(hh * nope, q_lora)
    w_qp = w3[:, nope:].reshape(hh * rope, q_lora)
    assert nope == LANES and LANES % rope == 0 and (hh * rope) % LANES == 0
    w_qb_ext = jnp.concatenate([w_qn, w_qp, _rot_rows(w_qp)], axis=0)
    cidx = _cvec_index(n_ctx, dec_seq, tm)
    mod_spec = pl.BlockSpec((None, 1, d), lambda i: (cidx(i), 0, 0))
    full = lambda a: pl.BlockSpec(a.shape, lambda i: (0,) * a.ndim)
    kern = functools.partial(_mla_proj_kernel, q_lora=q_lora, kv_lora=kv_lora, rope=rope, n_heads=hh)
    return pl.pallas_call(
        kern,
        grid=(t // tm,),
        in_specs=[pl.BlockSpec((tm, d), lambda i: (i, 0)), full(g), mod_spec, mod_spec,
                  full(w_in_ext), full(q_g), full(kv_g), full(w_qb_ext),
                  pl.BlockSpec((tm, LANES), lambda i: (i, 0)),
                  pl.BlockSpec((tm, LANES), lambda i: (i, 0))],
        out_specs=[pl.BlockSpec((tm, 2 * hh * LANES), lambda i: (i, 0)),
                   pl.BlockSpec((tm, kv_lora), lambda i: (i, 0)),
                   pl.BlockSpec((tm, rope), lambda i: (i, 0))],
        out_shape=[jax.ShapeDtypeStruct((t, 2 * hh * LANES), BF16),
                   jax.ShapeDtypeStruct((t, kv_lora), F32),
                   jax.ShapeDtypeStruct((t, rope), F32)],
        compiler_params=_params("arbitrary"),
        name="mla_proj",
    )(x, g, shift, scale, w_in_ext, q_g, kv_g, w_qb_ext, cos, sin)


def _mla_kv_kernel(ckv_ref, kp_ref, w_ref, k_ref, v_ref):
    kv = _bdot(ckv_ref[...], w_ref[...])
    n = v_ref.shape[1]
    kp = kp_ref[...]
    pieces = []
    for hd in range(n // LANES):
        pieces += [kv[:, hd * LANES:(hd + 1) * LANES].astype(BF16), kp]
    k_ref[...] = jnp.concatenate(pieces, axis=1)
    v_ref[...] = kv[:, n:].astype(BF16)


def mla_kv(ckv_all, kp_pad, w_kvb, tm=1024):
    r, kv_lora = ckv_all.shape
    hh, nope, vd = MLA_HEADS, MLA_NOPE, MLA_V
    w3 = w_kvb.reshape(kv_lora, hh, nope + vd)
    w_perm = jnp.concatenate([w3[:, :, :nope].reshape(kv_lora, hh * nope),
                              w3[:, :, nope:].reshape(kv_lora, hh * vd)], axis=1)
    return pl.pallas_call(
        _mla_kv_kernel,
        grid=(r // tm,),
        in_specs=[pl.BlockSpec((tm, kv_lora), lambda i: (i, 0)),
                  pl.BlockSpec((tm, LANES), lambda i: (i, 0)),
                  pl.BlockSpec(w_perm.shape, lambda i: (0, 0))],
        out_specs=[pl.BlockSpec((tm, 2 * hh * LANES), lambda i: (i, 0)),
                   pl.BlockSpec((tm, hh * vd), lambda i: (i, 0))],
        out_shape=[jax.ShapeDtypeStruct((r, 2 * hh * LANES), BF16),
                   jax.ShapeDtypeStruct((r, hh * vd), BF16)],
        compiler_params=_params("arbitrary"),
        name="mla_kv",
    )(ckv_all, kp_pad, w_perm)


def _attn_kernel(q_ref, k_ref, v_ref, x_ref, w_ref, gt_ref, xo_ref, o_scr, *, scale):
    c = scale * np.log2(np.e)
    for h in range(MLA_HEADS):
        hk = slice(2 * h * LANES, 2 * (h + 1) * LANES)
        s = _bdot_nt(q_ref[:, hk], k_ref[:, hk])
        e = jnp.exp2((s - jnp.max(s, axis=-1, keepdims=True)) * c)
        o = _bdot(e, v_ref[:, h * MLA_V:(h + 1) * MLA_V]) / jnp.sum(e, axis=-1, keepdims=True)
        o_scr[:, h * MLA_V:(h + 1) * MLA_V] = o.astype(BF16)
    xo_ref[...] = x_ref[...] + gt_ref[...] * jnp.dot(o_scr[...], w_ref[...],
                                                     preferred_element_type=F32)


def mla_attention(x, q, k, v, w_out, gate, q_row_off, k_row_off, n_seq, q_len, k_len, n_ctx, dec_seq):
    tq = min(ATTN_Q_BLOCK, q_len)
    qb = q_len // tq
    q0 = q_row_off // tq
    k0 = k_row_off // k_len
    t, d = x.shape
    dq, dv = q.shape[1], v.shape[1]
    cidx = _cvec_index(n_ctx, dec_seq, tq)
    rows = lambda s, j: q0 + s * qb + j
    kv_mode = pl.Buffered(1) if qb > 1 else None
    kern = functools.partial(_attn_kernel, scale=(MLA_NOPE + MLA_ROPE) ** -0.5)
    return pl.pallas_call(
        kern,
        grid=(n_seq, qb),
        in_specs=[pl.BlockSpec((tq, dq), lambda s, j: (rows(s, j), 0)),
                  pl.BlockSpec((k_len, dq), lambda s, j: (k0 + s, 0), pipeline_mode=kv_mode),
                  pl.BlockSpec((k_len, dv), lambda s, j: (k0 + s, 0), pipeline_mode=kv_mode),
                  pl.BlockSpec((tq, d), lambda s, j: (rows(s, j), 0)),
                  pl.BlockSpec(w_out.shape, lambda s, j: (0, 0)),
                  pl.BlockSpec((None, 1, d), lambda s, j: (cidx(rows(s, j)), 0, 0))],
        out_specs=pl.BlockSpec((tq, d), lambda s, j: (rows(s, j), 0)),
        out_shape=jax.ShapeDtypeStruct((t, d), F32),
        scratch_shapes=[pltpu.VMEM((tq, dv), BF16)],
        input_output_aliases={3: 0},
        compiler_params=_params("arbitrary", "arbitrary"),
        name="mla_attention",
    )(q, k, v, x, w_out, gate)


def kernel(x_prompt, x_sample, state_mlstm_C, state_mlstm_n, state_mlstm_m, cache_mla_ckv,
           cache_mla_kpe, c, c_ctx, w_ada, b_ada, norm_mix, norm_ffn, norm_final, w_pool,
           pool_scale, w_mlstm_in, b_mlstm_gate, mlstm_head_g, w_mlstm_out, w_mla_in, mla_q_g,
           mla_kv_g, w_mla_qb, w_mla_kvb, w_mla_out, w_router, b_router, w_exp_gate, w_exp_up,
           w_exp_down):
    batch, seq, d = x_prompt.shape
    dec_batch, dec_seq, _ = x_sample.shape
    depth = w_ada.shape[0]
    n_ctx = batch * seq
    n_lat = dec_batch * dec_seq
    hh = MLSTM_HEADS
    past = cache_mla_ckv.shape[2]

    x = (x_prompt.reshape(n_ctx, d), x_sample.reshape(n_lat, d))

    n_cv = 1 + dec_batch
    cvecs = jnp.concatenate([c_ctx[None, :], c, jnp.zeros((SUBLANES - n_cv % SUBLANES, d), F32)], axis=0)
    mod = ada_mod_all(cvecs, w_ada, b_ada).reshape(depth, cvecs.shape[0], 6, 1, d)

    w_router_pad = jnp.pad(w_router, ((0, 0), (0, LANES - N_EXPERTS)))
    w_router_hi = w_router_pad.astype(BF16)
    w_router_lo = (w_router_pad - w_router_hi.astype(F32)).astype(BF16)
    w_router_split = jnp.concatenate([w_router_hi, w_router_lo], axis=1)
    b_router_col = b_router.reshape(N_EXPERTS, 1)
    g_final = norm_final.reshape(1, d)
    row = lambda a: a.reshape(1, -1)

    outs = {}
    for i in range(depth):
        kind, j = i % 3, i // 3
        m = [mod[i, :n_cv, k] for k in range(6)]
        g_mix = row(norm_mix[i])
        if kind == 0:
            x = pool_layer(x, g_mix, m[0], m[1], m[2], w_pool[j], row(pool_scale[j]),
                           n_ctx, seq, dec_seq)
        elif kind == 1:
            qkv, o_gate, gcol, grow = mlstm_proj(x, g_mix, m[0], m[1], w_mlstm_in, j,
                                                 b_mlstm_gate[j], n_ctx, dec_seq)
            t = n_ctx + n_lat
            l = MLSTM_CHUNK
            grow_c = grow.reshape(4 * hh, t // l, l).transpose(1, 0, 2)
            hs_c, c_new, n_new, m_new = mlstm_scan(qkv, gcol, grow_c, None, 0, batch, seq)
            init = (state_mlstm_C[:, j], state_mlstm_n[:, j][:, :, :, None, :],
                    state_mlstm_m[:, j][:, :, :, None, None])
            hs_l, _, _, _ = mlstm_scan(qkv, gcol, grow_c, init, n_ctx, dec_batch, dec_seq)
            outs["C"] = c_new[:, None]
            outs["n"] = n_new[:, None, :, :, 0, :]
            outs["m"] = m_new[:, None, :, :, 0, 0]
            x = mlstm_out(x, hs_c, hs_l, o_gate, row(mlstm_head_g[j]), w_mlstm_out[j], m[2],
                          n_ctx, dec_seq)
        else:
            cos, sin = _rope_tables(n_ctx, dec_batch, dec_seq)
            q_cat, ckv, kpe = mla_proj(x, g_mix, m[0], m[1], w_mla_in[j], row(mla_q_g[j]),
                                       row(mla_kv_g[j]), w_mla_qb[j], cos, sin, n_ctx, dec_seq)
            lat_parts_c, lat_parts_p = [], []
            for b in range(dec_batch):
                lo = n_ctx + b * dec_seq
                lat_parts_c += [cache_mla_ckv[b, j], ckv[lo:lo + dec_seq]]
                lat_parts_p += [cache_mla_kpe[b, j], kpe[lo:lo + dec_seq]]
            ckv_all = jnp.concatenate(lat_parts_c + [ckv[:n_ctx]], axis=0)
            kp_all = jnp.concatenate(lat_parts_p + [kpe[:n_ctx]], axis=0).astype(BF16)
            kp_pad = jnp.pad(kp_all, ((0, 0), (0, LANES - kp_all.shape[1])))
            k_cat, v = mla_kv(ckv_all, kp_pad, w_mla_kvb[j])
            k_lat = past + dec_seq
            w_out = w_mla_out[j].astype(BF16)
            x = mla_attention(x, q_cat, k_cat, v, w_out, m[2], 0, dec_batch * k_lat, batch, seq, seq,
                              n_ctx, dec_seq)
            x = mla_attention(x, q_cat, k_cat, v, w_out, m[2], n_ctx, 0, dec_batch, dec_seq, k_lat,
                              n_ctx, dec_seq)
            outs["ckv"] = ckv[:n_ctx].reshape(batch, 1, seq, -1)
            outs["kpe"] = kpe[:n_ctx].reshape(batch, 1, seq, -1)
        x = moe_layer(x, row(norm_ffn[i]), m[3], m[4], m[5], w_router_split, b_router_col,
                      w_exp_gate, w_exp_up, w_exp_down, i, g_final, n_ctx, dec_seq,
                      final_norm=(i == depth - 1), split_output=(i == depth - 1))

    y_prompt = x[0].reshape(batch, seq, d)
    y_sample = x[1].reshape(dec_batch, dec_seq, d)
    return (y_prompt, y_sample, outs["C"], outs["n"], outs["m"], outs["ckv"], outs["kpe"])
```

```python
import functools

import numpy as np
import jax
import jax.numpy as jnp
from jax import lax
from jax.experimental import pallas as pl
from jax.experimental.pallas import tpu as pltpu

F32 = jnp.float32
BF16 = jnp.bfloat16

NORM_EPS = 1e-6
GRID_W = 64
POOL_WINDOWS = (2, 4, 8, 16)
MLSTM_HEADS = 4
MLSTM_CHUNK = 256
MLA_HEADS = 8
MLA_NOPE = 128
MLA_ROPE = 64
MLA_V = 128
ROPE_BASE = 10000.0
N_EXPERTS = 16
N_EXPERT_GROUPS = 4
EXPERTS_PER_GROUP = N_EXPERTS // N_EXPERT_GROUPS

LANES = 128
SUBLANES = 8
VMEM_LIMIT = 56 * 1024 * 1024
MOE_VMEM_LIMIT = 60 * 1024 * 1024
POOL_TILE = 256
POOL_HALO = 8
ATTN_Q_BLOCK = 512
ATTN_SEQ_PACK = 2
MOE_TILE = 512
MOE_SEG_ALIGN = 16
MOE_ROW_BLOCK = 160
MOE_TAIL_BLOCK = 64
MOE_LOAD_EXPERTS = 2


def _params(*sem, vmem_limit=VMEM_LIMIT):
    return pltpu.CompilerParams(dimension_semantics=sem, vmem_limit_bytes=vmem_limit)


def _rms(x, g):
    return x * lax.rsqrt(jnp.mean(x * x, axis=-1, keepdims=True) + NORM_EPS) * g


def _modulated(x, g, shift, scale):
    return _rms(x, g) * (1.0 + scale) + shift


def _silu(x):
    return x * jax.nn.sigmoid(x)


def _bdot(a, b):
    return jnp.dot(a.astype(BF16), b.astype(BF16), preferred_element_type=F32)


def _bdot_nt(a, b):
    return lax.dot_general(a.astype(BF16), b.astype(BF16), (((1,), (1,)), ((), ())),
                           preferred_element_type=F32)


def _bdot_tn(a, b):
    return lax.dot_general(a.astype(BF16), b.astype(BF16), (((0,), (0,)), ((), ())),
                           preferred_element_type=F32)


def _cvec_index(n_ctx, dec_seq, tm):
    def idx(i):
        r = i * tm
        return jnp.where(r < n_ctx, 0, (r - n_ctx) // dec_seq + 1)
    return idx


def _ada_kernel(c_ref, w_ref, b_ref, o_ref):
    o_ref[...] = _bdot(_silu(c_ref[...]), w_ref[...]) + b_ref[...]


def ada_mod_all(cvecs, w_ada, b_ada, tn=3072):
    depth, d, n6 = w_ada.shape
    rows = cvecs.shape[0]
    return pl.pallas_call(
        _ada_kernel,
        grid=(depth, n6 // tn),
        in_specs=[pl.BlockSpec((rows, d), lambda l, n: (0, 0)),
                  pl.BlockSpec((None, d, tn), lambda l, n: (l, 0, n)),
                  pl.BlockSpec((None, 1, tn), lambda l, n: (l, 0, n))],
        out_specs=pl.BlockSpec((None, rows, tn), lambda l, n: (l, 0, n)),
        out_shape=jax.ShapeDtypeStruct((depth, rows, n6), F32),
        compiler_params=_params("arbitrary", "arbitrary"),
        name="ada_mod",
    )(cvecs, w_ada, b_ada.reshape(depth, 1, n6))


def _pool_kernel(*refs, n_ctx_tiles, ctx_seq_tiles, lat_seq_tiles, split_input):
    n_x = 6 if split_input else 3
    x_refs = refs[:n_x]
    g_ref, sh_ref, sc_ref, gt_ref, wp_ref, ps_ref, o_ref, buf_ref, *lvl_refs = refs[n_x:]
    i = pl.program_id(0)
    is_ctx = i < n_ctx_tiles
    j = jnp.where(is_ctx, i % ctx_seq_tiles, (i - n_ctx_tiles) % lat_seq_tiles)
    nt = jnp.where(is_ctx, ctx_seq_tiles, lat_seq_tiles)
    g, sh, sc = g_ref[...], sh_ref[...], sc_ref[...]
    tp, hl = POOL_TILE, POOL_HALO
    gw = o_ref.shape[1] // len(POOL_WINDOWS)

    def fill(xc_ref, xp_ref, xn_ref):
        buf_ref[pl.ds(0, hl), :] = jnp.where(j == 0, 0.0, _modulated(xp_ref[...], g, sh, sc))
        buf_ref[pl.ds(hl, tp), :] = _modulated(xc_ref[...], g, sh, sc)
        buf_ref[pl.ds(hl + tp, hl), :] = jnp.where(j == nt - 1, 0.0,
                                                    _modulated(xn_ref[...], g, sh, sc))
        o_ref[...] = xc_ref[...]

    if split_input:
        pl.when(is_ctx)(lambda: fill(*x_refs[:3]))
        pl.when(jnp.logical_not(is_ctx))(lambda: fill(*x_refs[3:]))
    else:
        fill(*x_refs)

    rows = tp + 2 * hl
    buf_ref[pl.ds(rows, hl), :] = jnp.zeros((hl, buf_ref.shape[1]), F32)
    for ref in lvl_refs:
        ref[pl.ds(rows, hl), :] = jnp.zeros((hl, gw), F32)
    pos = j * tp + lax.broadcasted_iota(jnp.int32, (tp, 1), 0)
    seq_len = nt * tp
    for gi, w in enumerate(POOL_WINDOWS):
        cols = pl.ds(gi * gw, gw)
        src, src_cols = buf_ref, cols
        for lvl in range(1, w.bit_length() - 1):
            step = 1 << (lvl - 1)
            dst = lvl_refs[lvl - 1]
            dst[pl.ds(0, rows), :] = src[pl.ds(0, rows), src_cols] + src[pl.ds(step, rows), src_cols]
            src, src_cols = dst, pl.ds(0, gw)
        acc = src[pl.ds(hl - w // 2, tp), src_cols] + src[pl.ds(hl, tp), src_cols]
        cnt = jnp.minimum(pos + w // 2, seq_len) - jnp.maximum(pos - w // 2, 0)
        pooled = acc / cnt.astype(F32) - buf_ref[pl.ds(hl, tp), cols]
        y = _bdot(pooled, wp_ref[gi]) * ps_ref[:, cols]
        o_ref[:, cols] = o_ref[:, cols] + gt_ref[:, cols] * y


def _halo_specs(tp, hl, d, tile_off, n_rows):
    hb = tp // hl
    last_tile, last_hblk = n_rows // tp - 1, n_rows // hl - 1
    tile = lambda i: jnp.clip(i - tile_off, 0, last_tile)
    return [pl.BlockSpec((tp, d), lambda i: (tile(i), 0)),
            pl.BlockSpec((hl, d), lambda i: (jnp.clip(tile(i) * hb - 1, 0, last_hblk), 0)),
            pl.BlockSpec((hl, d), lambda i: (jnp.clip((tile(i) + 1) * hb, 0, last_hblk), 0))]


def pool_layer(xs, g, shift, scale, gate, w_pool, pool_scale, n_ctx, seq, dec_seq):
    split = isinstance(xs, tuple)
    tp, hl = POOL_TILE, POOL_HALO
    if split:
        d = xs[0].shape[1]
        t = xs[0].shape[0] + xs[1].shape[0]
        x_specs = (_halo_specs(tp, hl, d, 0, xs[0].shape[0])
                   + _halo_specs(tp, hl, d, n_ctx // tp, xs[1].shape[0]))
        x_args = (xs[0],) * 3 + (xs[1],) * 3
    else:
        t, d = xs.shape
        x_specs = _halo_specs(tp, hl, d, 0, t)
        x_args = (xs,) * 3
    cidx = _cvec_index(n_ctx, dec_seq, tp)
    mod_spec = pl.BlockSpec((None, 1, d), lambda i: (cidx(i), 0, 0))
    row_spec = pl.BlockSpec((1, d), lambda i: (0, 0))
    assert all(w & (w - 1) == 0 and w // 2 <= hl for w in POOL_WINDOWS)
    n_levels = max(POOL_WINDOWS).bit_length() - 2
    kern = functools.partial(_pool_kernel, n_ctx_tiles=n_ctx // tp, ctx_seq_tiles=seq // tp,
                             lat_seq_tiles=dec_seq // tp, split_input=split)
    return pl.pallas_call(
        kern,
        grid=(t // tp,),
        in_specs=x_specs + [row_spec, mod_spec, mod_spec, mod_spec,
                            pl.BlockSpec(w_pool.shape, lambda i: (0, 0, 0)), row_spec],
        out_specs=pl.BlockSpec((tp, d), lambda i: (i, 0)),
        out_shape=jax.ShapeDtypeStruct((t, d), F32),
        scratch_shapes=[pltpu.VMEM((tp + 3 * hl, d), F32)]
        + [pltpu.VMEM((tp + 3 * hl, d // len(POOL_WINDOWS)), F32)] * n_levels,
        compiler_params=_params("arbitrary"),
        name="pool_mixer",
    )(*x_args, g, shift, scale, gate, w_pool, pool_scale)


def _route(sel, scores):
    e, tm = sel.shape
    row = lax.broadcasted_iota(jnp.int32, (e, tm), 0)
    best = jnp.zeros((1, tm), jnp.int32)
    best_sc = None
    for gidx in range(N_EXPERT_GROUPS):
        r = [sel[gidx * EXPERTS_PER_GROUP + k:gidx * EXPERTS_PER_GROUP + k + 1, :]
             for k in range(EXPERTS_PER_GROUP)]
        top2 = None
        for a in range(EXPERTS_PER_GROUP):
            for b in range(a + 1, EXPERTS_PER_GROUP):
                s = r[a] + r[b]
                top2 = s if top2 is None else jnp.maximum(top2, s)
        if best_sc is None:
            best_sc = top2
        else:
            better = top2 > best_sc
            best = jnp.where(better, gidx, best)
            best_sc = jnp.where(better, top2, best_sc)
    neg = -jnp.inf
    masked = jnp.where(row // EXPERTS_PER_GROUP == best, sel, neg)
    m1 = jnp.max(masked, axis=0, keepdims=True)
    i1 = jnp.min(jnp.where(masked == m1, row, e), axis=0, keepdims=True)
    masked2 = jnp.where(row == i1, neg, masked)
    m2 = jnp.max(masked2, axis=0, keepdims=True)
    i2 = jnp.min(jnp.where(masked2 == m2, row, e), axis=0, keepdims=True)
    hot1 = row == i1
    hot2 = row == i2
    w1 = jnp.sum(jnp.where(hot1, scores, 0.0), axis=0, keepdims=True)
    w2 = jnp.sum(jnp.where(hot2, scores, 0.0), axis=0, keepdims=True)
    tot = w1 + w2
    return best, jnp.where(hot1, w1 / tot, 0.0) + jnp.where(hot2, w2 / tot, 0.0)


def _split_bf16(a, parts):
    out = []
    for _ in range(parts):
        p = a.astype(BF16)
        out.append(p)
        a = a - p.astype(F32)
    return out


def _pad_rows(a, rows):
    return jnp.concatenate([a, jnp.zeros((rows - a.shape[0], a.shape[1]), a.dtype)], axis=0)


def _moe_kernel(x_ref, g_ref, sh_ref, sc_ref, gt_ref, wr_ref, br_ref, wgf_ref, wuf_ref, wdf_ref,
                gf_ref, *rest, final_norm, n_ctx_tiles, n_experts):
    out_refs, (wgu_ref, wd_ref), scratch = rest[:-5], rest[-5:-3], rest[-3:]
    i = pl.program_id(0)
    per_step, f = wgf_ref.shape[0], wgf_ref.shape[2]
    n_load = n_experts // per_step
    for s in range(n_load):
        @pl.when(i == s)
        def _(s=s):
            for j in range(per_step):
                gi, k = divmod(s * per_step + j, EXPERTS_PER_GROUP)
                wgu_ref[gi, :, pl.ds(2 * k * f, f)] = wgf_ref[j].astype(BF16)
                wgu_ref[gi, :, pl.ds((2 * k + 1) * f, f)] = wuf_ref[j].astype(BF16)
                wd_ref[gi, pl.ds(k * f, f), :] = wdf_ref[j].astype(BF16)

    @pl.when(i >= n_load)
    def _():
        _moe_tile(x_ref, g_ref, sh_ref, sc_ref, gt_ref, wr_ref, br_ref, wgu_ref, wd_ref,
                  gf_ref, out_refs, scratch, i - n_load, final_norm, n_ctx_tiles)


def _moe_tile(x_ref, g_ref, sh_ref, sc_ref, gt_ref, wr_ref, br_ref, wgu_ref, wd_ref,
              gf_ref, out_refs, scratch, tile, final_norm, n_ctx_tiles):
    hp_scr, cw_scr, yp_scr = scratch
    tr, d = x_ref.shape
    trp = hp_scr.shape[0]
    ng, eg = N_EXPERT_GROUPS, EXPERTS_PER_GROUP
    x = x_ref[...]
    h = _modulated(x, g_ref[...], sh_ref[...], sc_ref[...])
    hb = h.astype(BF16)

    h_lo = (h - hb.astype(F32)).astype(BF16)
    wr = wr_ref[...]
    lg = jnp.dot(hb, wr, preferred_element_type=F32)
    logits = lg[:, :LANES] + lg[:, LANES:] + jnp.dot(h_lo, wr[:, :LANES], preferred_element_type=F32)
    scores = jax.nn.sigmoid(logits.T[:N_EXPERTS, :])
    best, comb_t = _route(scores + br_ref[...], scores)

    grp = lax.broadcasted_iota(jnp.int32, (SUBLANES, tr), 0)
    hot_t = (grp == best).astype(F32)
    cw_t = hot_t[0:1, :] * comb_t[0:eg, :]
    for gi in range(1, ng):
        cw_t = cw_t + hot_t[gi:gi + 1, :] * comb_t[gi * eg:(gi + 1) * eg, :]
    cw_c = _pad_rows(cw_t, LANES).T

    ia = lax.broadcasted_iota(jnp.int32, (tr, tr), 0)
    ib = lax.broadcasted_iota(jnp.int32, (tr, tr), 1)
    before = jnp.where(ia < ib, 1.0, 0.0).astype(BF16)
    rank_t = jnp.dot(hot_t.astype(BF16), before, preferred_element_type=F32)

    starts, counts = [], []
    off = jnp.int32(0)
    for gi in range(ng):
        n = jnp.sum(hot_t[gi:gi + 1, :]).astype(jnp.int32)
        n = ((n + MOE_SEG_ALIGN - 1) // MOE_SEG_ALIGN) * MOE_SEG_ALIGN
        starts.append(off)
        counts.append(n)
        off = off + n

    pos_t = hot_t[0:1, :] * (rank_t[0:1, :] + starts[0].astype(F32))
    for gi in range(1, ng):
        pos_t = pos_t + hot_t[gi:gi + 1, :] * (rank_t[gi:gi + 1, :] + starts[gi].astype(F32))
    pos_c = _pad_rows(pos_t, LANES).T[:, 0:1]
    used = tr + ng * MOE_SEG_ALIGN
    perm = jnp.where(lax.broadcasted_iota(jnp.int32, (used, tr), 0) == pos_t.astype(jnp.int32),
                     1.0, 0.0).astype(BF16)
    perm_t = jnp.where(lax.broadcasted_iota(jnp.int32, (tr, used), 1) == pos_c.astype(jnp.int32),
                       1.0, 0.0).astype(BF16)

    hp_scr[pl.ds(0, used), :] = jnp.dot(perm, hb, preferred_element_type=F32).astype(BF16)
    hp_scr[pl.ds(used, trp - used), :] = jnp.zeros((trp - used, d), BF16)
    cw_pair = jnp.dot(perm, jnp.concatenate(_split_bf16(cw_c, 2), axis=1), preferred_element_type=F32)
    cw_scr[pl.ds(0, used), :] = cw_pair[:, :LANES] + cw_pair[:, LANES:]
    cw_scr[pl.ds(used, trp - used), :] = jnp.zeros((trp - used, LANES), F32)
    yp_scr[...] = jnp.zeros_like(yp_scr)

    f = wd_ref.shape[1] // eg

    def expert_block(gi, row0, n_rows, seg_end=None):
        rows = pl.ds(pl.multiple_of(row0, MOE_SEG_ALIGN), n_rows)
        cwb = cw_scr[rows, :]
        gu = jnp.dot(hp_scr[rows, :], wgu_ref[gi], preferred_element_type=F32)
        hid = [(_silu(gu[:, 2 * k * f:(2 * k + 1) * f]) * gu[:, (2 * k + 1) * f:(2 * k + 2) * f]
                * cwb[:, k:k + 1]).astype(BF16) for k in range(eg)]
        y = jnp.dot(jnp.concatenate(hid, axis=1), wd_ref[gi], preferred_element_type=F32).astype(BF16)
        if seg_end is None:
            yp_scr[rows, :] = y
        else:
            inside = row0 + lax.broadcasted_iota(jnp.int32, (n_rows, 1), 0) < seg_end
            yp_scr[rows, :] = jnp.where(inside, y, yp_scr[rows, :])

    sb, tb = MOE_ROW_BLOCK, MOE_TAIL_BLOCK
    for gi in range(ng):
        expert_block(gi, starts[gi], sb)
    for gi in range(ng):
        def tail(b, carry, gi=gi):
            expert_block(gi, starts[gi] + sb + b * tb, tb, starts[gi] + counts[gi])
            return carry
        lax.fori_loop(0, (jnp.maximum(counts[gi] - sb, 0) + tb - 1) // tb, tail, 0)

    moe = jnp.dot(perm_t, yp_scr[pl.ds(0, used), :], preferred_element_type=F32)
    out = x + gt_ref[...] * moe
    if final_norm:
        out = _rms(out, gf_ref[...])
    if n_ctx_tiles is None:
        out_refs[0][...] = out
    else:
        is_ctx = tile < n_ctx_tiles

        @pl.when(is_ctx)
        def _():
            out_refs[0][...] = out

        @pl.when(jnp.logical_not(is_ctx))
        def _():
            out_refs[1][...] = out


def moe_layer(x, g, shift, scale, gate, w_router_split, b_router_col, wg, wu, wd, layer, g_final,
              n_ctx, dec_seq, final_norm, split_output=False, tm=MOE_TILE):
    t, d = x.shape
    _, n_e, _, f = wg.shape
    trp = tm + N_EXPERT_GROUPS * MOE_SEG_ALIGN + max(MOE_ROW_BLOCK, MOE_TAIL_BLOCK)
    per_step = MOE_LOAD_EXPERTS
    n_load = n_e // per_step
    tile = lambda i: jnp.maximum(i - n_load, 0)
    cidx = _cvec_index(n_ctx, dec_seq, tm)
    mod_spec = pl.BlockSpec((None, 1, d), lambda i: (cidx(tile(i)), 0, 0))
    row_spec = pl.BlockSpec((1, d), lambda i: (0, 0))
    expert = lambda i: jnp.minimum(i, n_load - 1)
    nct = n_ctx // tm
    if split_output:
        out_specs = [pl.BlockSpec((tm, d), lambda i: (jnp.minimum(tile(i), nct - 1), 0)),
                     pl.BlockSpec((tm, d), lambda i: (jnp.maximum(tile(i) - nct, 0), 0))]
        out_shape = [jax.ShapeDtypeStruct((n_ctx, d), F32), jax.ShapeDtypeStruct((t - n_ctx, d), F32)]
    else:
        out_specs = pl.BlockSpec((tm, d), lambda i: (tile(i), 0))
        out_shape = jax.ShapeDtypeStruct((t, d), F32)
    return pl.pallas_call(
        functools.partial(_moe_kernel, final_norm=final_norm,
                          n_ctx_tiles=nct if split_output else None, n_experts=n_e),
        grid=(n_load + t // tm,),
        in_specs=[pl.BlockSpec((tm, d), lambda i: (tile(i), 0)),
                  row_spec, mod_spec, mod_spec, mod_spec,
                  pl.BlockSpec(w_router_split.shape, lambda i: (0, 0)),
                  pl.BlockSpec(b_router_col.shape, lambda i: (0, 0)),
                  pl.BlockSpec((None, per_step, d, f), lambda i: (layer, expert(i), 0, 0)),
                  pl.BlockSpec((None, per_step, d, f), lambda i: (layer, expert(i), 0, 0)),
                  pl.BlockSpec((None, per_step, f, d), lambda i: (layer, expert(i), 0, 0)),
                  row_spec],
        out_specs=out_specs,
        out_shape=out_shape,
        scratch_shapes=[pltpu.VMEM((N_EXPERT_GROUPS, d, 2 * EXPERTS_PER_GROUP * f), BF16),
                        pltpu.VMEM((N_EXPERT_GROUPS, EXPERTS_PER_GROUP * f, d), BF16),
                        pltpu.VMEM((trp, d), BF16), pltpu.VMEM((trp, LANES), F32),
                        pltpu.VMEM((trp, d), BF16)],
        compiler_params=_params("arbitrary", vmem_limit=MOE_VMEM_LIMIT),
        name="moe",
    )(x, g, shift, scale, gate, w_router_split, b_router_col, wg, wu, wd, g_final)


def _per_stream(n_ctx_tiles, ctx_ref, lat_ref, fn):
    i = pl.program_id(0)

    @pl.when(i < n_ctx_tiles)
    def _():
        fn(ctx_ref[...])

    @pl.when(i >= n_ctx_tiles)
    def _():
        fn(lat_ref[...])


def _stream_specs(tm, k, n_ctx_tiles):
    return [pl.BlockSpec((tm, k), lambda i: (jnp.minimum(i, n_ctx_tiles - 1), 0)),
            pl.BlockSpec((tm, k), lambda i: (jnp.maximum(i - n_ctx_tiles, 0), 0))]


def _mlstm_out_kernel(x_ref, hc_ref, hl_ref, og_ref, hg_ref, w_ref, gt_ref, o_ref, *, n_ctx_tiles):
    def run(hs):
        a = jax.nn.sigmoid(og_ref[...].astype(F32)) * (hs.astype(F32) * hg_ref[...])
        o_ref[...] = x_ref[...] + gt_ref[...] * _bdot(a, w_ref[...])
    _per_stream(n_ctx_tiles, hc_ref, hl_ref, run)


def mlstm_out(x, hs_ctx, hs_lat, o_gate, head_g, w, gate, n_ctx, dec_seq, tm=1024):
    t, d = x.shape
    k = hs_ctx.shape[1]
    cidx = _cvec_index(n_ctx, dec_seq, tm)
    nct = n_ctx // tm
    return pl.pallas_call(
        functools.partial(_mlstm_out_kernel, n_ctx_tiles=nct),
        grid=(t // tm,),
        in_specs=[pl.BlockSpec((tm, d), lambda i: (i, 0))] + _stream_specs(tm, k, nct) + [
            pl.BlockSpec((tm, k), lambda i: (i, 0)),
            pl.BlockSpec((1, k), lambda i: (0, 0)),
            pl.BlockSpec((k, d), lambda i: (0, 0)),
            pl.BlockSpec((None, 1, d), lambda i: (cidx(i), 0, 0))],
        out_specs=pl.BlockSpec((tm, d), lambda i: (i, 0)),
        out_shape=jax.ShapeDtypeStruct((t, d), F32),
        compiler_params=_params("arbitrary"),
        name="mlstm_out",
    )(x, hs_ctx, hs_lat, o_gate, head_g, w, gate)


def _mlstm_proj_kernel(x_ref, g_ref, sh_ref, sc_ref, wqk_ref, wv_ref, wo_ref, wgt_ref, bg_ref,
                       qs_ref, qkv_ref, o_ref, gc_ref, gr_ref):
    h = _modulated(x_ref[...], g_ref[...], sh_ref[...], sc_ref[...]).astype(BF16)
    nqk = wqk_ref.shape[0]
    qkv_ref[:, :nqk] = (_bdot_nt(h, wqk_ref[...]) * qs_ref[...]).astype(BF16)
    qkv_ref[:, nqk:] = _bdot_nt(h, wv_ref[...]).astype(BF16)
    o_ref[...] = _bdot_nt(h, wo_ref[...]).astype(BF16)
    gates = _bdot_nt(h, wgt_ref[...]) + bg_ref[...]
    gc_ref[...] = gates
    gr_ref[...] = gates.T[:gr_ref.shape[0], :]


def mlstm_proj(x, g, shift, scale, w_in_all, layer, b_gate, n_ctx, dec_seq, tm=1024):
    t, d = x.shape
    hh = MLSTM_HEADS
    hv = d
    hk = hv // 2
    ng = 4 * hh
    w_t_all = jnp.swapaxes(w_in_all, 1, 2)
    w_g_pad = jnp.pad(w_t_all[layer, 2 * hk + 2 * hv:, :], ((0, LANES - ng), (0, 0)))
    b_pad = jnp.pad(b_gate.reshape(1, ng), ((0, 0), (0, LANES - ng)))
    dk = hk // hh
    q_scale = jnp.concatenate([jnp.full((1, hk), dk ** -0.5, F32), jnp.ones((1, hk), F32)], axis=1)
    cidx = _cvec_index(n_ctx, dec_seq, tm)
    mod_spec = pl.BlockSpec((None, 1, d), lambda i: (cidx(i), 0, 0))
    full = lambda a: pl.BlockSpec(a.shape, lambda i: (0,) * a.ndim)
    assert 2 * hk == hv
    w_col = lambda n: pl.BlockSpec((None, hv, d), lambda i: (layer, n, 0))
    return pl.pallas_call(
        _mlstm_proj_kernel,
        grid=(t // tm,),
        in_specs=[pl.BlockSpec((tm, d), lambda i: (i, 0)), full(g), mod_spec, mod_spec,
                  w_col(0), w_col(1), w_col(2), full(w_g_pad), full(b_pad), full(q_scale)],
        out_specs=[pl.BlockSpec((tm, 2 * hk + hv), lambda i: (i, 0)),
                   pl.BlockSpec((tm, hv), lambda i: (i, 0)),
                   pl.BlockSpec((tm, LANES), lambda i: (i, 0)),
                   pl.BlockSpec((ng, tm), lambda i: (0, i))],
        out_shape=[jax.ShapeDtypeStruct((t, 2 * hk + hv), BF16),
                   jax.ShapeDtypeStruct((t, hv), BF16),
                   jax.ShapeDtypeStruct((t, LANES), F32),
                   jax.ShapeDtypeStruct((ng, t), F32)],
        compiler_params=_params("arbitrary"),
        name="mlstm_proj",
    )(x, g, shift, scale, w_t_all, w_t_all, w_t_all, w_g_pad, b_pad, q_scale)


def _log_sigmoid(x):
    return jnp.minimum(x, 0.0) - jnp.log(1.0 + jnp.exp(-jnp.abs(x)))


def _gate_cumsums(gc, gr, causal_bf, feeds_bf):
    bc = br = None
    for part in _split_bf16(_log_sigmoid(gc), 3):
        t = jnp.dot(causal_bf, part, preferred_element_type=F32)
        bc = t if bc is None else bc + t
    for part in _split_bf16(_log_sigmoid(gr), 3):
        t = jnp.dot(part, feeds_bf, preferred_element_type=F32)
        br = t if br is None else br + t
    return bc, br


def _mlstm_chunk(q, k, v, i_col, b_col, i_row, b_row, c_st, n_st, m_st, causal, rev):
    l = q.shape[0]
    zero_state = c_st is None
    if zero_state:
        m_st = 0.0
    g_row = i_row - b_row
    log_w = jnp.where(causal, g_row, -jnp.inf)
    c_t = jnp.maximum(m_st, jnp.max(log_w, axis=1, keepdims=True))
    dw = jnp.exp(log_w - c_t)
    a = _bdot_nt(q, k) * dw
    a_hi, a_lo = _split_bf16(a, 2)
    ones = jnp.ones((l, LANES), BF16)
    den = (jnp.dot(a_hi, ones, preferred_element_type=F32)
           + jnp.dot(a_lo, ones, preferred_element_type=F32))[:, 0:1]
    if not zero_state:
        sw = jnp.exp(m_st - c_t)
        n_hi, n_lo = _split_bf16(jnp.broadcast_to(n_st, (LANES, n_st.shape[1])), 2)
        den = den + sw * (_bdot_nt(q, n_hi) + _bdot_nt(q, n_lo))[:, 0:1]
    inv = 1.0 / jnp.maximum(jnp.abs(den), jnp.exp(-(b_col + c_t)))
    h = inv * jnp.dot(a_hi, v, preferred_element_type=F32)
    if not zero_state:
        h = h + (sw * inv) * _bdot(q, c_st)
    b_last = b_col[0:1, :] if rev else b_col[l - 1:l, :]
    log_k = b_last - b_col + i_col
    m_new = jnp.maximum(b_last + m_st, jnp.max(log_k, axis=0, keepdims=True))
    kw = jnp.exp(log_k - m_new)
    kwk = kw * k.astype(F32)
    c_new = _bdot_tn(kwk, v)
    n_new = jnp.sum(kwk, axis=0, keepdims=True)
    if not zero_state:
        decay = jnp.exp(b_last + m_st - m_new)
        c_new = decay * c_st + c_new
        n_new = decay * n_st + n_new
    return h, c_new, n_new, m_new


def _mlstm_scan_kernel(*refs, n_chunks, zero_init):
    single = zero_init and n_chunks == 1
    if zero_init:
        q_ref, k_ref, v_ref, gc_ref, gr_ref, hs_ref, c_ref, n_ref, m_ref, hf_scr, hb_scr = refs
        if not single:
            c_ref[...] = jnp.zeros_like(c_ref)
            n_ref[...] = jnp.zeros_like(n_ref)
            m_ref[...] = jnp.zeros_like(m_ref)
    else:
        (q_ref, k_ref, v_ref, gc_ref, gr_ref, c0_ref, n0_ref, m0_ref,
         hs_ref, c_ref, n_ref, m_ref, hf_scr, hb_scr) = refs
        c_ref[...] = c0_ref[...]
        n_ref[...] = n0_ref[...]
        m_ref[...] = m0_ref[...]
    l = MLSTM_CHUNK
    hh = MLSTM_HEADS
    dk = q_ref.shape[1] // hh
    dv = v_ref.shape[1] // hh

    tt = lax.broadcasted_iota(jnp.int32, (l, l), 0)
    ss = lax.broadcasted_iota(jnp.int32, (l, l), 1)
    masks = (ss <= tt, ss >= tt)
    masks_bf = tuple(jnp.where(m, 1.0, 0.0).astype(BF16) for m in masks)

    def body(c, carry):
        cr = n_chunks - 1 - c
        start = (lambda j: j * l) if single else (lambda j: pl.multiple_of(j * l, l))
        rows = (pl.ds(start(c), l), pl.ds(start(cr), l))
        grs = (gr_ref[c], gr_ref[cr])
        outs, states = ([], []), []
        for d in range(2):
            q, k, v, gc, gr = q_ref[rows[d], :], k_ref[rows[d], :], v_ref[rows[d], :], gc_ref[rows[d], :], grs[d]
            bc, br = _gate_cumsums(gc, gr, masks_bf[d], masks_bf[1 - d])
            for h in range(hh):
                ci, cf = 2 * d * hh + h, (2 * d + 1) * hh + h
                state = (None,) * 3 if single else (c_ref[d, h], n_ref[d, h], m_ref[d, h])
                o, *st = _mlstm_chunk(q[:, h * dk:(h + 1) * dk], k[:, h * dk:(h + 1) * dk],
                                      v[:, h * dv:(h + 1) * dv], gc[:, ci:ci + 1], bc[:, cf:cf + 1],
                                      gr[ci:ci + 1, :], br[cf:cf + 1, :], *state, masks[d], d == 1)
                outs[d].append(o)
                states.append((d, h, st))
        hf_scr[rows[0], :] = jnp.concatenate(outs[0], axis=1)
        hb_scr[rows[1], :] = jnp.concatenate(outs[1], axis=1)
        for d, h, (c_new, n_new, m_new) in states:
            c_ref[d, h], n_ref[d, h], m_ref[d, h] = c_new, n_new, m_new
        return carry

    if single:
        body(0, 0)
    else:
        lax.fori_loop(0, n_chunks, body, 0)
    for h in range(hh):
        cols = pl.ds(h * dv, dv)
        hs = hf_scr[:, cols] + hb_scr[:, cols]
        hs_ref[:, cols] = (hs * lax.rsqrt(jnp.mean(hs * hs, axis=-1, keepdims=True)
                                          + NORM_EPS)).astype(BF16)


def mlstm_scan(qkv, gcol, grow, init, row_off, n_seq, seq_len):
    hh = MLSTM_HEADS
    hv = qkv.shape[1] // 2
    dv = hv // hh
    dk = dv // 2
    l = MLSTM_CHUNK
    nc = seq_len // l
    ob = row_off // seq_len
    kern = functools.partial(_mlstm_scan_kernel, n_chunks=nc, zero_init=init is None)
    st = lambda *tail: pl.BlockSpec((None, 2, hh) + tail, lambda s: (s, 0, 0) + (0,) * len(tail))
    states = [st(dk, dv), st(1, dk), st(1, 1)]
    return pl.pallas_call(
        kern,
        grid=(n_seq,),
        in_specs=[pl.BlockSpec((seq_len, hh * dk), lambda s: (ob + s, 0)),
                  pl.BlockSpec((seq_len, hh * dk), lambda s: (ob + s, 1)),
                  pl.BlockSpec((seq_len, hv), lambda s: (ob + s, 1)),
                  pl.BlockSpec((seq_len, LANES), lambda s: (ob + s, 0)),
                  pl.BlockSpec((nc, 4 * hh, l), lambda s: (ob + s, 0, 0))]
                 + ([] if init is None else states),
        out_specs=[pl.BlockSpec((seq_len, hv), lambda s: (s, 0))] + states,
        out_shape=[jax.ShapeDtypeStruct((n_seq * seq_len, hv), BF16),
                   jax.ShapeDtypeStruct((n_seq, 2, hh, dk, dv), F32),
                   jax.ShapeDtypeStruct((n_seq, 2, hh, 1, dk), F32),
                   jax.ShapeDtypeStruct((n_seq, 2, hh, 1, 1), F32)],
        scratch_shapes=[pltpu.VMEM((seq_len, hv), F32), pltpu.VMEM((seq_len, hv), F32)],
        compiler_params=_params("arbitrary"),
        name="mlstm_scan",
    )(qkv, qkv, qkv, gcol, grow, *(() if init is None else init))


def _mla_proj_kernel(x_ref, g_ref, sh_ref, sc_ref, win_ref, qg_ref, kvg_ref, wqb_ref, cos_ref,
                     sin_ref, q_ref, ckv_ref, kpe_ref, *, q_lora, kv_lora, rope, n_heads):
    h = _modulated(x_ref[...], g_ref[...], sh_ref[...], sc_ref[...])
    proj = _bdot_nt(h, win_ref[...])
    q_lat = proj[:, :q_lora]
    ckv_ref[...] = _rms(proj[:, q_lora:q_lora + kv_lora], kvg_ref[...])
    cos, sin = cos_ref[...], sin_ref[...]
    kpe = proj[:, q_lora + kv_lora:q_lora + kv_lora + rope]
    kpe_rot = proj[:, q_lora + kv_lora + rope:q_lora + kv_lora + 2 * rope]
    kpe_ref[...] = kpe * cos[:, :rope] + kpe_rot * sin[:, :rope]
    q = _bdot_nt(_rms(q_lat, qg_ref[...]), wqb_ref[...])
    n, nr = n_heads * LANES, n_heads * rope
    reps = nr // LANES
    roped = (q[:, n:n + nr] * jnp.concatenate([cos] * reps, axis=1)
             + q[:, n + nr:] * jnp.concatenate([sin] * reps, axis=1))
    pad = jnp.zeros((q.shape[0], LANES - rope), F32)
    pieces = []
    for hd in range(n_heads):
        pieces += [q[:, hd * LANES:(hd + 1) * LANES], roped[:, hd * rope:(hd + 1) * rope], pad]
    q_ref[...] = jnp.concatenate(pieces, axis=1).astype(BF16)


def _rot_rows(wt):
    n, k = wt.shape
    quarter = MLA_ROPE // 4
    w4 = wt.reshape(n // (2 * quarter), 2, quarter, k)
    return jnp.concatenate([-w4[:, 1:2], w4[:, 0:1]], axis=1).reshape(n, k)


def _rope_tables(n_ctx, dec_batch, dec_seq):
    quarter = MLA_ROPE // 4
    freq = np.power(np.float32(ROPE_BASE), -np.arange(quarter, dtype=np.float32) / np.float32(quarter))
    pos = np.arange(dec_seq)
    ang_r = (pos // GRID_W).astype(np.float32)[:, None] * freq[None, :]
    ang_c = (pos % GRID_W).astype(np.float32)[:, None] * freq[None, :]
    ang = np.concatenate([ang_r, ang_r, ang_c, ang_c], axis=1).astype(np.float32)
    cos = np.concatenate([np.ones((n_ctx, MLA_ROPE), np.float32)] + [np.cos(ang)] * dec_batch, axis=0)
    sin = np.concatenate([np.zeros((n_ctx, MLA_ROPE), np.float32)] + [np.sin(ang)] * dec_batch, axis=0)
    reps = LANES // MLA_ROPE
    return (jnp.asarray(np.tile(cos, (1, reps)), F32), jnp.asarray(np.tile(sin, (1, reps)), F32))


def mla_proj(x, g, shift, scale, w_in, q_g, kv_g, w_qb, cos, sin, n_ctx, dec_seq, tm=1024):
    t, d = x.shape
    hh, nope, rope = MLA_HEADS, MLA_NOPE, MLA_ROPE
    q_lora = q_g.shape[1]
    kv_lora = kv_g.shape[1]
    w_in_t = w_in.T
    w_in_ext = jnp.concatenate([w_in_t, _rot_rows(w_in_t[q_lora + kv_lora:])], axis=0)
    w3 = w_qb.T.reshape(hh, nope + rope, q_lora)
    w_qn = w3[:, :nope].reshape(hh * nope, q_lora)
    w_qp = w3[:, nope:].reshape(hh * rope, q_lora)
    assert nope == LANES and LANES % rope == 0 and (hh * rope) % LANES == 0
    w_qb_ext = jnp.concatenate([w_qn, w_qp, _rot_rows(w_qp)], axis=0)
    cidx = _cvec_index(n_ctx, dec_seq, tm)
    mod_spec = pl.BlockSpec((None, 1, d), lambda i: (cidx(i), 0, 0))
    full = lambda a: pl.BlockSpec(a.shape, lambda i: (0,) * a.ndim)
    kern = functools.partial(_mla_proj_kernel, q_lora=q_lora, kv_lora=kv_lora, rope=rope, n_heads=hh)
    return pl.pallas_call(
        kern,
        grid=(t // tm,),
        in_specs=[pl.BlockSpec((tm, d), lambda i: (i, 0)), full(g), mod_spec, mod_spec,
                  full(w_in_ext), full(q_g), full(kv_g), full(w_qb_ext),
                  pl.BlockSpec((tm, LANES), lambda i: (i, 0)),
                  pl.BlockSpec((tm, LANES), lambda i: (i, 0))],
        out_specs=[pl.BlockSpec((tm, 2 * hh * LANES), lambda i: (i, 0)),
                   pl.BlockSpec((tm, kv_lora), lambda i: (i, 0)),
                   pl.BlockSpec((tm, rope), lambda i: (i, 0))],
        out_shape=[jax.ShapeDtypeStruct((t, 2 * hh * LANES), BF16),
                   jax.ShapeDtypeStruct((t, kv_lora), F32),
                   jax.ShapeDtypeStruct((t, rope), F32)],
        compiler_params=_params("arbitrary"),
        name="mla_proj",
    )(x, g, shift, scale, w_in_ext, q_g, kv_g, w_qb_ext, cos, sin)


def _mla_kv_kernel(ckv_ref, kp_ref, w_ref, k_ref, v_ref):
    kv = _bdot(ckv_ref[...], w_ref[...])
    n = v_ref.shape[1]
    kp = kp_ref[...]
    pieces = []
    for hd in range(n // LANES):
        pieces += [kv[:, hd * LANES:(hd + 1) * LANES].astype(BF16), kp]
    k_ref[...] = jnp.concatenate(pieces, axis=1)
    v_ref[...] = kv[:, n:].astype(BF16)


def mla_kv(ckv_all, kp_pad, w_kvb, tm=1024):
    r, kv_lora = ckv_all.shape
    hh, nope, vd = MLA_HEADS, MLA_NOPE, MLA_V
    w3 = w_kvb.reshape(kv_lora, hh, nope + vd)
    w_perm = jnp.concatenate([w3[:, :, :nope].reshape(kv_lora, hh * nope),
                              w3[:, :, nope:].reshape(kv_lora, hh * vd)], axis=1)
    return pl.pallas_call(
        _mla_kv_kernel,
        grid=(r // tm,),
        in_specs=[pl.BlockSpec((tm, kv_lora), lambda i: (i, 0)),
                  pl.BlockSpec((tm, LANES), lambda i: (i, 0)),
                  pl.BlockSpec(w_perm.shape, lambda i: (0, 0))],
        out_specs=[pl.BlockSpec((tm, 2 * hh * LANES), lambda i: (i, 0)),
                   pl.BlockSpec((tm, hh * vd), lambda i: (i, 0))],
        out_shape=[jax.ShapeDtypeStruct((r, 2 * hh * LANES), BF16),
                   jax.ShapeDtypeStruct((r, hh * vd), BF16)],
        compiler_params=_params("arbitrary"),
        name="mla_kv",
    )(ckv_all, kp_pad, w_perm)


def _attn_kernel(q_ref, k_ref, v_ref, x_ref, w_ref, gt_ref, xo_ref, o_scr, *, scale, pack):
    c = scale * np.log2(np.e)
    tq, kl = q_ref.shape[0] // pack, k_ref.shape[0] // pack
    for sub in range(pack):
        qr, kr = pl.ds(sub * tq, tq), pl.ds(sub * kl, kl)
        for h in range(MLA_HEADS):
            hk = slice(2 * h * LANES, 2 * (h + 1) * LANES)
            hv = slice(h * MLA_V, (h + 1) * MLA_V)
            s = _bdot_nt(q_ref[qr, hk], k_ref[kr, hk])
            e = jnp.exp2((s - jnp.max(s, axis=-1, keepdims=True)) * c)
            o = _bdot(e, v_ref[kr, hv]) / jnp.sum(e, axis=-1, keepdims=True)
            o_scr[qr, hv] = o.astype(BF16)
    xo_ref[...] = x_ref[...] + gt_ref[...] * jnp.dot(o_scr[...], w_ref[...],
                                                     preferred_element_type=F32)


def mla_attention(x, q, k, v, w_out, gate, q_row_off, k_row_off, n_seq, q_len, k_len, n_ctx, dec_seq):
    tq = min(ATTN_Q_BLOCK, q_len)
    qb = q_len // tq
    pack = ATTN_SEQ_PACK if qb == 1 and n_seq % ATTN_SEQ_PACK == 0 else 1
    bq, bk = pack * tq, pack * k_len
    q0 = q_row_off // bq
    k0 = k_row_off // bk
    t, d = x.shape
    dq, dv = q.shape[1], v.shape[1]
    cidx = _cvec_index(n_ctx, dec_seq, bq)
    rows = lambda s, j: q0 + s * qb + j
    kv_mode = pl.Buffered(1) if qb > 1 else None
    kern = functools.partial(_attn_kernel, scale=(MLA_NOPE + MLA_ROPE) ** -0.5, pack=pack)
    return pl.pallas_call(
        kern,
        grid=(n_seq // pack, qb),
        in_specs=[pl.BlockSpec((bq, dq), lambda s, j: (rows(s, j), 0)),
                  pl.BlockSpec((bk, dq), lambda s, j: (k0 + s, 0), pipeline_mode=kv_mode),
                  pl.BlockSpec((bk, dv), lambda s, j: (k0 + s, 0), pipeline_mode=kv_mode),
                  pl.BlockSpec((bq, d), lambda s, j: (rows(s, j), 0)),
                  pl.BlockSpec(w_out.shape, lambda s, j: (0, 0)),
                  pl.BlockSpec((None, 1, d), lambda s, j: (cidx(rows(s, j)), 0, 0))],
        out_specs=pl.BlockSpec((bq, d), lambda s, j: (rows(s, j), 0)),
        out_shape=jax.ShapeDtypeStruct((t, d), F32),
        scratch_shapes=[pltpu.VMEM((bq, dv), BF16)],
        input_output_aliases={3: 0},
        compiler_params=_params("arbitrary", "arbitrary"),
        name="mla_attention",
    )(q, k, v, x, w_out, gate)


def kernel(x_prompt, x_sample, state_mlstm_C, state_mlstm_n, state_mlstm_m, cache_mla_ckv,
           cache_mla_kpe, c, c_ctx, w_ada, b_ada, norm_mix, norm_ffn, norm_final, w_pool,
           pool_scale, w_mlstm_in, b_mlstm_gate, mlstm_head_g, w_mlstm_out, w_mla_in, mla_q_g,
           mla_kv_g, w_mla_qb, w_mla_kvb, w_mla_out, w_router, b_router, w_exp_gate, w_exp_up,
           w_exp_down):
    batch, seq, d = x_prompt.shape
    dec_batch, dec_seq, _ = x_sample.shape
    depth = w_ada.shape[0]
    n_ctx = batch * seq
    n_lat = dec_batch * dec_seq
    hh = MLSTM_HEADS
    past = cache_mla_ckv.shape[2]

    x = (x_prompt.reshape(n_ctx, d), x_sample.reshape(n_lat, d))

    n_cv = 1 + dec_batch
    cvecs = jnp.concatenate([c_ctx[None, :], c, jnp.zeros((SUBLANES - n_cv % SUBLANES, d), F32)], axis=0)
    mod = ada_mod_all(cvecs, w_ada, b_ada).reshape(depth, cvecs.shape[0], 6, 1, d)

    w_router_pad = jnp.pad(w_router, ((0, 0), (0, LANES - N_EXPERTS)))
    w_router_hi = w_router_pad.astype(BF16)
    w_router_lo = (w_router_pad - w_router_hi.astype(F32)).astype(BF16)
    w_router_split = jnp.concatenate([w_router_hi, w_router_lo], axis=1)
    b_router_col = b_router.reshape(N_EXPERTS, 1)
    g_final = norm_final.reshape(1, d)
    row = lambda a: a.reshape(1, -1)

    outs = {}
    for i in range(depth):
        kind, j = i % 3, i // 3
        m = [mod[i, :n_cv, k] for k in range(6)]
        g_mix = row(norm_mix[i])
        if kind == 0:
            x = pool_layer(x, g_mix, m[0], m[1], m[2], w_pool[j], row(pool_scale[j]),
                           n_ctx, seq, dec_seq)
        elif kind == 1:
            qkv, o_gate, gcol, grow = mlstm_proj(x, g_mix, m[0], m[1], w_mlstm_in, j,
                                                 b_mlstm_gate[j], n_ctx, dec_seq)
            t = n_ctx + n_lat
            l = MLSTM_CHUNK
            grow_c = grow.reshape(4 * hh, t // l, l).transpose(1, 0, 2)
            hs_c, c_new, n_new, m_new = mlstm_scan(qkv, gcol, grow_c, None, 0, batch, seq)
            init = (state_mlstm_C[:, j], state_mlstm_n[:, j][:, :, :, None, :],
                    state_mlstm_m[:, j][:, :, :, None, None])
            hs_l, _, _, _ = mlstm_scan(qkv, gcol, grow_c, init, n_ctx, dec_batch, dec_seq)
            outs["C"] = c_new[:, None]
            outs["n"] = n_new[:, None, :, :, 0, :]
            outs["m"] = m_new[:, None, :, :, 0, 0]
            x = mlstm_out(x, hs_c, hs_l, o_gate, row(mlstm_head_g[j]), w_mlstm_out[j], m[2],
                          n_ctx, dec_seq)
        else:
            cos, sin = _rope_tables(n_ctx, dec_batch, dec_seq)
            q_cat, ckv, kpe = mla_proj(x, g_mix, m[0], m[1], w_mla_in[j], row(mla_q_g[j]),
                                       row(mla_kv_g[j]), w_mla_qb[j], cos, sin, n_ctx, dec_seq)
            lat_parts_c, lat_parts_p = [], []
            for b in range(dec_batch):
                lo = n_ctx + b * dec_seq
                lat_parts_c += [cache_mla_ckv[b, j], ckv[lo:lo + dec_seq]]
                lat_parts_p += [cache_mla_kpe[b, j], kpe[lo:lo + dec_seq]]
            ckv_all = jnp.concatenate(lat_parts_c + [ckv[:n_ctx]], axis=0)
            kp_all = jnp.concatenate(lat_parts_p + [kpe[:n_ctx]], axis=0).astype(BF16)
            kp_pad = jnp.pad(kp_all, ((0, 0), (0, LANES - kp_all.shape[1])))
            k_cat, v = mla_kv(ckv_all, kp_pad, w_mla_kvb[j])
            k_lat = past + dec_seq
            w_out = w_mla_out[j].astype(BF16)
            x = mla_attention(x, q_cat, k_cat, v, w_out, m[2], 0, dec_batch * k_lat, batch, seq, seq,
                              n_ctx, dec_seq)
            x = mla_attention(x, q_cat, k_cat, v, w_out, m[2], n_ctx, 0, dec_batch, dec_seq, k_lat,
                              n_ctx, dec_seq)
            outs["ckv"] = ckv[:n_ctx].reshape(batch, 1, seq, -1)
            outs["kpe"] = kpe[:n_ctx].reshape(batch, 1, seq, -1)
        x = moe_layer(x, row(norm_ffn[i]), m[3], m[4], m[5], w_router_split, b_router_col,
                      w_exp_gate, w_exp_up, w_exp_down, i, g_final, n_ctx, dec_seq,
                      final_norm=(i == depth - 1), split_output=(i == depth - 1))

    y_prompt = x[0].reshape(batch, seq, d)
    y_sample = x[1].reshape(dec_batch, dec_seq, d)
    return (y_prompt, y_sample, outs["C"], outs["n"], outs["m"], outs["ckv"], outs["kpe"])
```
